```python
import math
import jax, jax.numpy as jnp
from jax import lax
import numpy as np

D_MODEL = 1024
BATCH = 8
SEQ = 4096
DEPTH = 1

D_DN = 512
DN_HEADS = 4
DN_HEAD_DIM = 128
CONV_WIDTH = 4
CHUNK = 64
D_ATT = 512
ATT_HEADS = 8
ATT_HEAD_DIM = 64
DILATED_PATTERNS = ((128, 1), (512, 4), (2048, 16))
N_BUCKETS = 32
MAX_DISTANCE = 2048
D_MIX = D_DN + D_ATT
D_IN = 4 * D_DN + 2 * DN_HEADS + 4 * D_ATT
EPS = 1e-6

kernel_name = "hybrid_deltanet_dilated_attention_layer"


def rms_norm(x, w):
    xf = x.astype(jnp.float32)
    return xf * lax.rsqrt(jnp.mean(xf * xf, axis=-1, keepdims=True) + EPS) * w.astype(jnp.float32)


def l2_norm(x):
    return x * lax.rsqrt(jnp.sum(x * x, axis=-1, keepdims=True) + EPS)


def split_heads(t, n_heads):
    b, s, _ = t.shape
    return t.reshape(b, s, n_heads, -1).transpose(0, 2, 1, 3)


def causal_depthwise_conv(u, w):
    k_width = w.shape[0]
    s = u.shape[1]
    up = jnp.pad(u, ((0, 0), (k_width - 1, 0), (0, 0)))
    y = up[:, 0:s] * w[0]
    for i in range(1, k_width):
        y = y + up[:, i:i + s] * w[i]
    return y


def gated_delta_rule(q, k, v, g, beta):
    b, h, s, dk = q.shape
    dv = v.shape[-1]
    n = s // CHUNK
    q = q * (dk ** -0.5)
    qc = q.reshape(b, h, n, CHUNK, dk)
    kc = k.reshape(b, h, n, CHUNK, dk)
    vc = v.reshape(b, h, n, CHUNK, dv)
    bc = beta.reshape(b, h, n, CHUNK)
    gc = jnp.cumsum(g.reshape(b, h, n, CHUNK), axis=-1)
    tril_incl = np.tril(np.ones((CHUNK, CHUNK), dtype=bool))
    tril_strict = np.tril(np.ones((CHUNK, CHUNK), dtype=bool), -1)
    diff = gc[..., :, None] - gc[..., None, :]
    decay = jnp.exp(jnp.where(tril_incl, diff, -jnp.inf))
    kb = kc * bc[..., None]
    a_mat = jnp.where(tril_strict, jnp.einsum('bhnid,bhnjd->bhnij', kb, kc) * decay, 0.0)
    l_mat = a_mat + jnp.eye(CHUNK, dtype=a_mat.dtype)
    u = lax.linalg.triangular_solve(l_mat, vc * bc[..., None], left_side=True, lower=True, unit_diagonal=True)
    w = lax.linalg.triangular_solve(l_mat, kb * jnp.exp(gc)[..., None], left_side=True, lower=True, unit_diagonal=True)
    attn_intra = jnp.einsum('bhnid,bhnjd->bhnij', qc, kc) * decay
    q_dec = qc * jnp.exp(gc)[..., None]
    k_tail = kc * jnp.exp(gc[..., -1:] - gc)[..., None]
    g_last = jnp.exp(gc[..., -1])
    xs = tuple(jnp.moveaxis(t, 2, 0) for t in (attn_intra, q_dec, k_tail, u, w, g_last))

    def step(state, inp):
        a_i, qd_i, kt_i, u_i, w_i, gl_i = inp
        v_new = u_i - jnp.einsum('bhck,bhkv->bhcv', w_i, state)
        o_i = jnp.einsum('bhck,bhkv->bhcv', qd_i, state) + jnp.einsum('bhij,bhjv->bhiv', a_i, v_new)
        state = state * gl_i[..., None, None] + jnp.einsum('bhck,bhcv->bhkv', kt_i, v_new)
        return state, o_i

    s0 = jnp.zeros((b, h, dk, dv), jnp.float32)
    _, o = lax.scan(step, s0, xs)
    return jnp.moveaxis(o, 0, 2).reshape(b, h, s, dv)


def t5_bucket(dist):
    max_exact = N_BUCKETS // 2
    d = np.maximum(dist, 1).astype(np.float64)
    large = max_exact + (np.log(d / max_exact) / math.log(MAX_DISTANCE / max_exact)
                         * (N_BUCKETS - max_exact)).astype(np.int32)
    large = np.minimum(large, N_BUCKETS - 1)
    return np.where(dist < max_exact, dist, large).astype(np.int32)


def dilated_pattern(q, k, v, rel_bias, window, dilation):
    b, h, s, hd = q.shape
    r = dilation
    l_sub = s // r
    w_steps = window // r
    blk = w_steps
    n_blk = -(-l_sub // blk)
    l_pad = n_blk * blk

    def to_blocks(t):
        t = t.reshape(b, h, l_sub, r, hd).transpose(0, 1, 3, 2, 4)
        t = jnp.pad(t, ((0, 0), (0, 0), (0, 0), (0, l_pad - l_sub), (0, 0)))
        return t.reshape(b, h, r, n_blk, blk, hd)

    def with_prev(t):
        prev = jnp.pad(t, ((0, 0), (0, 0), (0, 0), (1, 0), (0, 0), (0, 0)))[:, :, :, :-1]
        return jnp.concatenate([prev, t], axis=4)

    qb = to_blocks(q)
    kw = with_prev(to_blocks(k))
    vw = with_prev(to_blocks(v))
    qi = np.arange(blk)[:, None]
    kj = np.arange(2 * blk)[None, :]
    step = qi - kj + blk
    band = (step >= 0) & (step <= w_steps)
    key_idx = np.arange(n_blk)[:, None, None] * blk + kj[None] - blk
    mask = band[None] & (key_idx >= 0)
    buckets = t5_bucket(np.clip(step, 0, None) * r)
    bias = rel_bias.astype(jnp.float32)[:, buckets]
    scores = jnp.einsum('bhrnqd,bhrnkd->bhrnqk', qb, kw).astype(jnp.float32) + bias[:, None, None]
    scores = jnp.where(mask, scores, -jnp.inf)
    lse = jax.nn.logsumexp(scores, axis=-1)
    p = jnp.exp(scores - lse[..., None])
    o = jnp.einsum('bhrnqk,bhrnkd->bhrnqd', p, vw.astype(jnp.float32))
    o = o.reshape(b, h, r, l_pad, hd)[:, :, :, :l_sub].transpose(0, 1, 3, 2, 4).reshape(b, h, s, hd)
    lse = lse.reshape(b, h, r, l_pad)[:, :, :, :l_sub].transpose(0, 1, 3, 2).reshape(b, h, s)
    return o, lse


def deltanet_branch(qkv, z, b_proj, a_proj, conv_w, a_log, dt_bias, dn_norm_w):
    bsz, s, _ = qkv.shape
    qkv = jax.nn.silu(causal_depthwise_conv(qkv.astype(jnp.float32), conv_w.astype(jnp.float32)))
    q, k, v = jnp.split(qkv, 3, axis=-1)
    q = l2_norm(split_heads(q, DN_HEADS))
    k = l2_norm(split_heads(k, DN_HEADS))
    v = split_heads(v, DN_HEADS)
    beta = jax.nn.sigmoid(b_proj.astype(jnp.float32)).transpose(0, 2, 1)
    g = -jnp.exp(a_log.astype(jnp.float32)) * jax.nn.softplus(a_proj.astype(jnp.float32) + dt_bias.astype(jnp.float32))
    g = g.transpose(0, 2, 1)
    o = gated_delta_rule(q, k, v, g, beta)
    o = rms_norm(o, dn_norm_w).transpose(0, 2, 1, 3).reshape(bsz, s, D_DN)
    return o * jax.nn.silu(z.astype(jnp.float32))


def dilated_attention_branch(qkv, gate, q_norm_w, k_norm_w, rel_bias):
    bsz, s, _ = qkv.shape
    q, k, v = jnp.split(qkv, 3, axis=-1)
    q = rms_norm(split_heads(q, ATT_HEADS), q_norm_w) * (ATT_HEAD_DIM ** -0.5)
    k = rms_norm(split_heads(k, ATT_HEADS), k_norm_w)
    v = split_heads(v, ATT_HEADS).astype(jnp.float32)
    outs, lses = [], []
    for window, dilation in DILATED_PATTERNS:
        o_p, lse_p = dilated_pattern(q, k, v, rel_bias, window, dilation)
        outs.append(o_p)
        lses.append(lse_p)
    wts = jax.nn.softmax(jnp.stack(lses), axis=0)
    o = jnp.sum(wts[..., None] * jnp.stack(outs), axis=0)
    o = o.transpose(0, 2, 1, 3).reshape(bsz, s, D_ATT)
    return o * jax.nn.silu(gate.astype(jnp.float32))


def hybrid_layer(x, norm_w, w_in, conv_w, a_log, dt_bias, dn_norm_w, q_norm_w, k_norm_w, rel_bias, w_out):
    h = rms_norm(x, norm_w).astype(x.dtype)
    proj = h @ w_in
    cuts = [3 * D_DN, 4 * D_DN, 4 * D_DN + DN_HEADS, 4 * D_DN + 2 * DN_HEADS,
            4 * D_DN + 2 * DN_HEADS + 3 * D_ATT]
    qkv_dn, z_dn, b_dn, a_dn, qkv_att, gate_att = jnp.split(proj, cuts, axis=-1)
    y_dn = deltanet_branch(qkv_dn, z_dn, b_dn, a_dn, conv_w, a_log, dt_bias, dn_norm_w)
    y_att = dilated_attention_branch(qkv_att, gate_att, q_norm_w, k_norm_w, rel_bias)
    mixed = jnp.concatenate([y_dn, y_att], axis=-1).astype(x.dtype)
    return x + mixed @ w_out


def _fwd_setup_inputs(seed: int = 0) -> dict:
    key = jax.random.key(seed)
    ks = jax.random.split(key, 11)
    x = jax.random.normal(ks[0], (BATCH, SEQ, D_MODEL), jnp.float32)
    norm_w = 1.0 + 0.1 * jax.random.normal(ks[1], (DEPTH, D_MODEL), jnp.float32)
    w_in = jax.random.normal(ks[2], (DEPTH, D_MODEL, D_IN), jnp.float32) * (D_MODEL ** -0.5)
    conv_w = jax.random.normal(ks[3], (DEPTH, CONV_WIDTH, 3 * D_DN), jnp.float32) * (CONV_WIDTH ** -0.5)
    a_log = jnp.log(jax.random.uniform(ks[4], (DEPTH, DN_HEADS), jnp.float32, minval=1.0, maxval=16.0))
    dt = jnp.exp(jax.random.uniform(ks[5], (DEPTH, DN_HEADS), jnp.float32,
                                    minval=math.log(1e-3), maxval=math.log(1e-1)))
    dt_bias = dt + jnp.log(-jnp.expm1(-dt))
    dn_norm_w = 1.0 + 0.1 * jax.random.normal(ks[6], (DEPTH, DN_HEAD_DIM), jnp.float32)
    q_norm_w = 1.0 + 0.1 * jax.random.normal(ks[7], (DEPTH, ATT_HEAD_DIM), jnp.float32)
    k_norm_w = 1.0 + 0.1 * jax.random.normal(ks[8], (DEPTH, ATT_HEAD_DIM), jnp.float32)
    rel_bias = 0.5 * jax.random.normal(ks[9], (ATT_HEADS, N_BUCKETS), jnp.float32)
    w_out = jax.random.normal(ks[10], (DEPTH, D_MIX, D_MODEL), jnp.float32) * (D_MIX ** -0.5)
    return {"x": x, "norm_w": norm_w, "w_in": w_in, "conv_w": conv_w, "a_log": a_log,
            "dt_bias": dt_bias, "dn_norm_w": dn_norm_w, "q_norm_w": q_norm_w,
            "k_norm_w": k_norm_w, "rel_bias": rel_bias, "w_out": w_out}


def _fwd_reference(x, norm_w, w_in, conv_w, a_log, dt_bias, dn_norm_w, q_norm_w, k_norm_w, rel_bias, w_out):
    for layer in range(DEPTH):
        x = hybrid_layer(x, norm_w[layer], w_in[layer], conv_w[layer], a_log[layer], dt_bias[layer],
                         dn_norm_w[layer], q_norm_w[layer], k_norm_w[layer], rel_bias, w_out[layer])
    return x


import jax as _jax
import jax.numpy as _jnp

TWIN_FORMAT = 'train_step'
FWD_PARAMS = ['x', 'norm_w', 'w_in', 'conv_w', 'a_log', 'dt_bias', 'dn_norm_w', 'q_norm_w', 'k_norm_w', 'rel_bias', 'w_out']
TWIN_WEIGHTS = ['norm_w', 'w_in', 'conv_w', 'a_log', 'dt_bias', 'dn_norm_w', 'q_norm_w', 'k_norm_w', 'rel_bias', 'w_out']
TWIN_DIFF_INPUT = 'x'
TWIN_INPUTS = ['x', 'norm_w', 'w_in', 'conv_w', 'a_log', 'dt_bias', 'dn_norm_w', 'q_norm_w', 'k_norm_w', 'rel_bias', 'w_out', 'loss_target', 'm_norm_w', 'm_w_in', 'm_conv_w', 'm_a_log', 'm_dt_bias', 'm_dn_norm_w', 'm_q_norm_w', 'm_k_norm_w', 'm_rel_bias', 'm_w_out', 'v_norm_w', 'v_w_in', 'v_conv_w', 'v_a_log', 'v_dt_bias', 'v_dn_norm_w', 'v_q_norm_w', 'v_k_norm_w', 'v_rel_bias', 'v_w_out']
TWIN_OUTPUTS = ['loss', 'grad_x', 'grad_norm_w', 'grad_w_in', 'grad_conv_w', 'grad_a_log', 'grad_dt_bias', 'grad_dn_norm_w', 'grad_q_norm_w', 'grad_k_norm_w', 'grad_rel_bias', 'grad_w_out', 'delta_norm_w', 'delta_w_in', 'delta_conv_w', 'delta_a_log', 'delta_dt_bias', 'delta_dn_norm_w', 'delta_q_norm_w', 'delta_k_norm_w', 'delta_rel_bias', 'delta_w_out', 'new_m_norm_w', 'new_m_w_in', 'new_m_conv_w', 'new_m_a_log', 'new_m_dt_bias', 'new_m_dn_norm_w', 'new_m_q_norm_w', 'new_m_k_norm_w', 'new_m_rel_bias', 'new_m_w_out', 'new_v_norm_w', 'new_v_w_in', 'new_v_conv_w', 'new_v_a_log', 'new_v_dt_bias', 'new_v_dn_norm_w', 'new_v_q_norm_w', 'new_v_k_norm_w', 'new_v_rel_bias', 'new_v_w_out']
TWIN_LEAF_KINDS = {'loss': 'loss', 'grad_x': 'grad_x', 'grad_norm_w': 'grad_w', 'grad_w_in': 'grad_w', 'grad_conv_w': 'grad_w', 'grad_a_log': 'grad_w', 'grad_dt_bias': 'grad_w', 'grad_dn_norm_w': 'grad_w', 'grad_q_norm_w': 'grad_w', 'grad_k_norm_w': 'grad_w', 'grad_rel_bias': 'grad_w', 'grad_w_out': 'grad_w', 'delta_norm_w': 'delta_w', 'delta_w_in': 'delta_w', 'delta_conv_w': 'delta_w', 'delta_a_log': 'delta_w', 'delta_dt_bias': 'delta_w', 'delta_dn_norm_w': 'delta_w', 'delta_q_norm_w': 'delta_w', 'delta_k_norm_w': 'delta_w', 'delta_rel_bias': 'delta_w', 'delta_w_out': 'delta_w', 'new_m_norm_w': 'new_m', 'new_m_w_in': 'new_m', 'new_m_conv_w': 'new_m', 'new_m_a_log': 'new_m', 'new_m_dt_bias': 'new_m', 'new_m_dn_norm_w': 'new_m', 'new_m_q_norm_w': 'new_m', 'new_m_k_norm_w': 'new_m', 'new_m_rel_bias': 'new_m', 'new_m_w_out': 'new_m', 'new_v_norm_w': 'new_v', 'new_v_w_in': 'new_v', 'new_v_conv_w': 'new_v', 'new_v_a_log': 'new_v', 'new_v_dt_bias': 'new_v', 'new_v_dn_norm_w': 'new_v', 'new_v_q_norm_w': 'new_v', 'new_v_k_norm_w': 'new_v', 'new_v_rel_bias': 'new_v', 'new_v_w_out': 'new_v'}


def _forward(args):
    return _fwd_reference(*[args[k] for k in FWD_PARAMS])


def _output_shape():
    out = _jax.eval_shape(lambda: _forward(_fwd_setup_inputs(0)))
    return out.shape, out.dtype

N_MICROBATCH = 1
ADAM_LR = 0.001
ADAM_B1 = 0.9
ADAM_B2 = 0.999
ADAM_EPS = 1e-08
ADAM_WD = 0.01
ADAM_STEP = 10
PER_EXAMPLE_BATCH_AXIS = {'x': 0, 'loss_target': 0}
SHARED_INPUTS = []
_WEIGHT_DTYPES = {'norm_w': _jnp.float32, 'w_in': _jnp.float32, 'conv_w': _jnp.float32, 'a_log': _jnp.float32, 'dt_bias': _jnp.float32, 'dn_norm_w': _jnp.float32, 'q_norm_w': _jnp.float32, 'k_norm_w': _jnp.float32, 'rel_bias': _jnp.float32, 'w_out': _jnp.float32}
MOMENT_SCALE = {'norm_w': 6.391838e+00, 'w_in': 2.229179e-01, 'conv_w': 5.970725e-01, 'a_log': 4.196334e+01, 'dt_bias': 3.948620e+01, 'dn_norm_w': 4.762099e+01, 'q_norm_w': 7.079349e-01, 'k_norm_w': 7.147470e-01, 'rel_bias': 1.504346e-01, 'w_out': 3.195314e-01}


def _to_microbatches(a, axis):
    t = _jnp.moveaxis(a, axis, 0)
    t = t.reshape((N_MICROBATCH, t.shape[0] // N_MICROBATCH) + t.shape[1:])
    return _jnp.moveaxis(t, 1, axis + 1)


def setup_inputs(seed: int = 0) -> dict:
    inp = _fwd_setup_inputs(seed)
    key = _jax.random.fold_in(_jax.random.key(seed), 7919)
    shape, _ = _output_shape()
    out = dict(inp)
    out["loss_target"] = _jax.random.normal(_jax.random.fold_in(key, 0), shape, _jnp.float32)
    for i, name in enumerate(TWIN_WEIGHTS):
        w = inp[name].astype(_jnp.float32)
        if MOMENT_SCALE is None:
            s = _jnp.sqrt(_jnp.mean(_jnp.square(w)) + 1e-30)
        else:
            s = MOMENT_SCALE[name]
        km, kv = _jax.random.split(_jax.random.fold_in(key, i + 1))
        out[name] = w
        out["m_" + name] = s * _jax.random.normal(km, w.shape, _jnp.float32)
        out["v_" + name] = (s * s) * _jax.random.uniform(kv, w.shape, _jnp.float32, 0.5, 1.5)
    if N_MICROBATCH > 1:
        for name, axis in PER_EXAMPLE_BATCH_AXIS.items():
            out[name] = _to_microbatches(out[name], axis)
    return {'x': out['x'], 'norm_w': out['norm_w'], 'w_in': out['w_in'], 'conv_w': out['conv_w'], 'a_log': out['a_log'], 'dt_bias': out['dt_bias'], 'dn_norm_w': out['dn_norm_w'], 'q_norm_w': out['q_norm_w'], 'k_norm_w': out['k_norm_w'], 'rel_bias': out['rel_bias'], 'w_out': out['w_out'], 'loss_target': out['loss_target'], 'm_norm_w': out['m_norm_w'], 'm_w_in': out['m_w_in'], 'm_conv_w': out['m_conv_w'], 'm_a_log': out['m_a_log'], 'm_dt_bias': out['m_dt_bias'], 'm_dn_norm_w': out['m_dn_norm_w'], 'm_q_norm_w': out['m_q_norm_w'], 'm_k_norm_w': out['m_k_norm_w'], 'm_rel_bias': out['m_rel_bias'], 'm_w_out': out['m_w_out'], 'v_norm_w': out['v_norm_w'], 'v_w_in': out['v_w_in'], 'v_conv_w': out['v_conv_w'], 'v_a_log': out['v_a_log'], 'v_dt_bias': out['v_dt_bias'], 'v_dn_norm_w': out['v_dn_norm_w'], 'v_q_norm_w': out['v_q_norm_w'], 'v_k_norm_w': out['v_k_norm_w'], 'v_rel_bias': out['v_rel_bias'], 'v_w_out': out['v_w_out']}


def _loss(weights, diff, rest, loss_target):
    with _jax.named_scope("forward"):
        args = {**rest, TWIN_DIFF_INPUT: diff, **{k: w.astype(_WEIGHT_DTYPES[k]) for k, w in weights.items()}}
        y = _forward(args)
    with _jax.named_scope("loss_head"):
        err = _jnp.square(y.astype(_jnp.float32) - loss_target)
        return 0.5 * _jnp.sum(_jnp.mean(err, axis=-1)) if err.ndim else 0.5 * err


def _adamw(w, g, m, v):
    m = ADAM_B1 * m + (1.0 - ADAM_B1) * g
    v = ADAM_B2 * v + (1.0 - ADAM_B2) * _jnp.square(g)
    m_hat = m / (1.0 - ADAM_B1 ** ADAM_STEP)
    v_hat = v / (1.0 - ADAM_B2 ** ADAM_STEP)
    delta = -ADAM_LR * (m_hat / (_jnp.sqrt(v_hat) + ADAM_EPS) + ADAM_WD * w)
    return delta, m, v


def reference(x, norm_w, w_in, conv_w, a_log, dt_bias, dn_norm_w, q_norm_w, k_norm_w, rel_bias, w_out, loss_target, m_norm_w, m_w_in, m_conv_w, m_a_log, m_dt_bias, m_dn_norm_w, m_q_norm_w, m_k_norm_w, m_rel_bias, m_w_out, v_norm_w, v_w_in, v_conv_w, v_a_log, v_dt_bias, v_dn_norm_w, v_q_norm_w, v_k_norm_w, v_rel_bias, v_w_out):
    given = dict(x=x, norm_w=norm_w, w_in=w_in, conv_w=conv_w, a_log=a_log, dt_bias=dt_bias, dn_norm_w=dn_norm_w, q_norm_w=q_norm_w, k_norm_w=k_norm_w, rel_bias=rel_bias, w_out=w_out, loss_target=loss_target, m_norm_w=m_norm_w, m_w_in=m_w_in, m_conv_w=m_conv_w, m_a_log=m_a_log, m_dt_bias=m_dt_bias, m_dn_norm_w=m_dn_norm_w, m_q_norm_w=m_q_norm_w, m_k_norm_w=m_k_norm_w, m_rel_bias=m_rel_bias, m_w_out=m_w_out, v_norm_w=v_norm_w, v_w_in=v_w_in, v_conv_w=v_conv_w, v_a_log=v_a_log, v_dt_bias=v_dt_bias, v_dn_norm_w=v_dn_norm_w, v_q_norm_w=v_q_norm_w, v_k_norm_w=v_k_norm_w, v_rel_bias=v_rel_bias, v_w_out=v_w_out)
    weights = {n: given[n] for n in TWIN_WEIGHTS}
    shared = {n: given[n] for n in SHARED_INPUTS}
    per_example = {n: given[n] for n in ['x']}
    grad_fn = _jax.value_and_grad(_loss, argnums=(0, 1))

    def one_microbatch(ex, loss_target):
        ex = dict(ex)
        diff = ex.pop(TWIN_DIFF_INPUT)
        return grad_fn(weights, diff, {**shared, **ex}, loss_target)

    if N_MICROBATCH == 1:
        loss, (grad_w, grad_x) = one_microbatch(per_example, given["loss_target"])
    else:
        def body(carry, xs):
            loss_sum, grad_sum = carry
            l_k, (gw_k, gx_k) = one_microbatch(xs[0], xs[1])
            with _jax.named_scope("update"):
                return (loss_sum + l_k, _jax.tree.map(_jnp.add, grad_sum, gw_k)), gx_k

        init = (_jnp.zeros((), _jnp.float32), _jax.tree.map(_jnp.zeros_like, weights))
        (loss, grad_w), grad_x = _jax.lax.scan(body, init, (per_example, given["loss_target"]))
    with _jax.named_scope("update"):
        delta_w, new_m, new_v = {}, {}, {}
        for n in TWIN_WEIGHTS:
            delta_w[n], new_m[n], new_v[n] = _adamw(weights[n], grad_w[n], given["m_" + n], given["v_" + n])
    return (loss, grad_x, *[grad_w[n] for n in TWIN_WEIGHTS], *[delta_w[n] for n in TWIN_WEIGHTS],
            *[new_m[n] for n in TWIN_WEIGHTS], *[new_v[n] for n in TWIN_WEIGHTS])
```

```python
import functools
import math

import numpy as np
import jax
import jax.numpy as jnp
from jax import lax
from jax.experimental import pallas as pl
from jax.experimental.pallas import tpu as pltpu

F32 = jnp.float32
MXU = jnp.bfloat16
HI = lax.Precision.HIGHEST

D_MODEL = 1024
D_DN = 512
DN_HEADS = 4
DK = 128
CHUNK = 64
D_ATT = 512
ATT_HEADS = 8
HD = 64
PATTERNS = ((128, 1), (512, 4), (2048, 16))
BLK = 128
N_BUCKETS = 32
MAX_DISTANCE = 2048
EPS = 1e-6
W_COLS = 4224
N_DEV = 8
AXES = ("x", "y", "c")

ADAM_LR = 0.001
ADAM_B1 = 0.9
ADAM_B2 = 0.999
ADAM_EPS = 1e-08
ADAM_WD = 0.01
ADAM_STEP = 10

VMEM_LIMIT = 56 * 1024 * 1024
NEG = -1e30


def _dot(a, b):
    return jnp.dot(a.astype(MXU), b.astype(MXU), preferred_element_type=F32)


def _dot_nt(a, b):
    return lax.dot_general(a.astype(MXU), b.astype(MXU), (((1,), (1,)), ((), ())), preferred_element_type=F32)


def _dot_tn(a, b):
    return lax.dot_general(a.astype(MXU), b.astype(MXU), (((0,), (0,)), ((), ())), preferred_element_type=F32)


def _hdot(a, b):
    return jnp.dot(a, b, precision=HI, preferred_element_type=F32)


def _hdot_nt(a, b):
    return lax.dot_general(a, b, (((1,), (1,)), ((), ())), precision=HI, preferred_element_type=F32)


def _hdot_tn(a, b):
    return lax.dot_general(a, b, (((0,), (0,)), ((), ())), precision=HI, preferred_element_type=F32)


def _sigmoid(x):
    return 1.0 / (1.0 + jnp.exp(-x))


def _silu(x):
    return x * _sigmoid(x)


def _dsilu(x):
    s = _sigmoid(x)
    return s * (1.0 + x * (1.0 - s))


def _softplus(x):
    return jnp.maximum(x, 0.0) + jnp.log(1.0 + jnp.exp(-jnp.abs(x)))


def _iota(shape, dim):
    return lax.broadcasted_iota(jnp.int32, shape, dim)


def _lane_col(x, k):
    return jnp.sum(jnp.where(_iota(x.shape, 1) == k, x, 0.0), axis=1, keepdims=True)


def _params(sem=None):
    return pltpu.CompilerParams(dimension_semantics=sem, vmem_limit_bytes=VMEM_LIMIT)


def _t5_bucket(dist):
    max_exact = N_BUCKETS // 2
    d = np.maximum(dist, 1).astype(np.float64)
    large = max_exact + (np.log(d / max_exact) / math.log(MAX_DISTANCE / max_exact)
                         * (N_BUCKETS - max_exact)).astype(np.int32)
    large = np.minimum(large, N_BUCKETS - 1)
    return np.where(dist < max_exact, dist, large).astype(np.int32)


def _bucket_tables():
    qi = np.arange(BLK)[:, None]
    kj = np.arange(2 * BLK)[None, :]
    step = qi - kj + BLK
    band = (step >= 0) & (step <= BLK)
    out = []
    for _, r in PATTERNS:
        b = _t5_bucket(np.clip(step, 0, None) * r)
        out.append(np.where(band, b, -1))
    return np.stack(out).astype(np.int32)


def _group_mats():
    g = np.zeros((D_ATT, 128), np.float32)
    for h in range(ATT_HEADS):
        g[h * HD:(h + 1) * HD, h] = 1.0
    gp = np.zeros((ATT_HEADS // 2, D_ATT, 128), np.float32)
    for h in range(ATT_HEADS):
        gp[h // 2, h * HD:(h + 1) * HD, h % 2] = 1.0
    return g, np.ascontiguousarray(g.T), gp


def _peer(k):
    x, y, c = (lax.axis_index(a) for a in AXES)
    kx, ky, kc = (k >> 2) & 1, (k >> 1) & 1, k & 1
    px = 1 - x if kx else x
    py = 1 - y if ky else y
    pc = 1 - c if kc else c
    return (px, py, pc), 4 * px + 2 * py + pc


def _exchange(arrs, gather, name):
    n = len(arrs)

    def body(*refs):
        ins, outs = refs[:n], refs[n:2 * n]
        send, recv, loc = refs[2 * n:]
        x, y, c = (lax.axis_index(a) for a in AXES)
        me = 4 * x + 2 * y + c
        sends, recvs, locs = [], [], []
        for a in range(n):
            own = ins[a] if gather else ins[a].at[me]
            lc = pltpu.make_async_copy(own, outs[a].at[me], loc.at[a])
            lc.start()
            locs.append(lc)
            for k in range(1, N_DEV):
                peer, plin = _peer(k)
                src = ins[a] if gather else ins[a].at[plin]
                cp = pltpu.make_async_remote_copy(src_ref=src, dst_ref=outs[a].at[me], send_sem=send.at[a, k - 1],
                                                  recv_sem=recv.at[a, k - 1], device_id=peer,
                                                  device_id_type=pl.DeviceIdType.MESH)
                cp.start()
                sends.append(cp)
                recvs.append(pltpu.make_async_remote_copy(src_ref=src, dst_ref=outs[a].at[plin],
                                                          send_sem=send.at[a, k - 1], recv_sem=recv.at[a, k - 1],
                                                          device_id=peer, device_id_type=pl.DeviceIdType.MESH))
        for cp in recvs:
            cp.wait_recv()
        for cp in sends:
            cp.wait_send()
        for lc in locs:
            lc.wait()

    any_spec = pl.BlockSpec(memory_space=pl.ANY)
    out_shape = [jax.ShapeDtypeStruct(((N_DEV,) + a.shape) if gather else a.shape, a.dtype) for a in arrs]
    return pl.pallas_call(
        body, name=name, out_shape=out_shape,
        in_specs=[any_spec] * n, out_specs=[any_spec] * n,
        scratch_shapes=[pltpu.SemaphoreType.DMA((n, N_DEV - 1)), pltpu.SemaphoreType.DMA((n, N_DEV - 1)),
                        pltpu.SemaphoreType.DMA((n,))],
    )(*arrs)


def _inproj(x, nw, w):
    t = x.shape[0]
    tm = 256

    def body(x_ref, nw_ref, w_ref, h_ref, pdn_ref, z_ref, patt_ref, gate_ref, ba_ref):
        xv = x_ref[...]
        rstd = lax.rsqrt(jnp.mean(xv * xv, axis=-1, keepdims=True) + EPS)
        h = (xv * rstd * nw_ref[...]).astype(MXU)
        h_ref[...] = h
        for ref, lo, hi in ((pdn_ref, 0, 1536), (z_ref, 1536, 2048), (patt_ref, 2048, 3584),
                            (gate_ref, 3584, 4096), (ba_ref, 4096, 4224)):
            ref[...] = jnp.dot(h, w_ref[:, lo:hi], preferred_element_type=F32)

    row = lambda n: pl.BlockSpec((tm, n), lambda i: (i, 0))
    full = lambda a: pl.BlockSpec(a.shape, lambda i: (0,) * a.ndim)
    return pl.pallas_call(
        body, name="inproj", grid=(t // tm,),
        in_specs=[row(D_MODEL), full(nw), full(w)],
        out_specs=[row(D_MODEL), row(1536), row(512), row(1536), row(512), row(128)],
        out_shape=[jax.ShapeDtypeStruct((t, D_MODEL), MXU)] +
                  [jax.ShapeDtypeStruct((t, n), F32) for n in (1536, 512, 1536, 512, 128)],
        compiler_params=_params(("arbitrary",)),
    )(x, nw, w)


CONV_ROWS = 512


def _conv_taps(u_ref, c, w_ref):
    r0 = c * CONV_ROWS
    if c == 0:
        ext = jnp.concatenate([jnp.zeros((8, 128), F32), u_ref[0:CONV_ROWS, :]], axis=0)
    else:
        ext = u_ref[r0 - 8:r0 + CONV_ROWS, :]
    taps = [ext[8:, :]] + [pltpu.roll(ext, k, 0)[8:, :] for k in (1, 2, 3)]
    y = taps[0] * w_ref[3:4, :]
    for k in (1, 2, 3):
        y = y + taps[k] * w_ref[3 - k:4 - k, :]
    return taps, y


def _dn_prep(pdn, cw):
    t = pdn.shape[0]

    def body(u_ref, w_ref, o_ref):
        j = pl.program_id(0)
        for c in range(t // CONV_ROWS):
            _, y = _conv_taps(u_ref, c, w_ref)
            a = _silu(y)
            ssq = jnp.sum(a * a, axis=1, keepdims=True)
            f = jnp.where(j < 8, lax.rsqrt(ssq + EPS), 1.0) * jnp.where(j < 4, DK ** -0.5, 1.0)
            o_ref[c * CONV_ROWS:(c + 1) * CONV_ROWS, :] = a * f

    return pl.pallas_call(
        body, name="dn_prep", grid=(12,),
        in_specs=[pl.BlockSpec((t, 128), lambda j: (0, j)), pl.BlockSpec((4, 128), lambda j: (0, j))],
        out_specs=pl.BlockSpec((t, 128), lambda j: (0, j)),
        out_shape=jax.ShapeDtypeStruct((t, 1536), F32),
        compiler_params=_params(("arbitrary",)),
    )(pdn, cw)


def _chunk_common(qkv, ba, arow, dtb, h):
    c = CHUNK
    q = qkv[:, h * DK:(h + 1) * DK]
    k = qkv[:, D_DN + h * DK:D_DN + (h + 1) * DK]
    v = qkv[:, 2 * D_DN + h * DK:2 * D_DN + (h + 1) * DK]
    b_raw = _lane_col(ba, h)
    a_raw = _lane_col(ba, DN_HEADS + h)
    a_h = _lane_col(arow, DN_HEADS + h)
    dt_h = _lane_col(dtb, DN_HEADS + h)
    beta = _sigmoid(b_raw)
    g = a_h * _softplus(a_raw + dt_h)
    ri, ci = _iota((c, c), 0), _iota((c, c), 1)
    tril = (ri >= ci).astype(F32)
    gcb = _hdot(tril, jnp.broadcast_to(g, (c, DK)))
    eye = (ri == ci).astype(F32)
    gcr = _hdot(jnp.ones((c, c), F32), eye * gcb[:, :c])
    decay = jnp.exp(jnp.where(ri >= ci, gcb[:, :c] - gcr, NEG))
    eg = jnp.exp(gcb)
    gl = gcb[c - 1:c, :]
    egl = jnp.exp(gl)
    etail = jnp.exp(gl - gcb)
    return dict(q=q, k=k, v=v, beta=beta, g=g, a_raw=a_raw, a_h=a_h, dt_h=dt_h, ri=ri, ci=ci,
                decay=decay, eg=eg, egl=egl, etail=etail)


def _dn_scan_fwd(qkv, ba, z, arow, dtb, dnw):
    t = qkv.shape[0]
    n = t // CHUNK
    c = CHUNK

    def body(qkv_ref, ba_ref, z_ref, arow_ref, dtb_ref, dnw_ref, o_ref, y_ref, sh_ref, th_ref, s_ref):
        @pl.when(pl.program_id(0) == 0)
        def _():
            s_ref[...] = jnp.zeros_like(s_ref)

        qkv_v, ba_v = qkv_ref[...], ba_ref[...]
        for h in range(DN_HEADS):
            m = _chunk_common(qkv_v, ba_v, arow_ref[...], dtb_ref[...], h)
            q, k, v, beta = m["q"], m["k"], m["v"], m["beta"]
            ri, ci = m["ri"], m["ci"]
            kb = k * beta
            amat = jnp.where(ri > ci, _dot_nt(kb, k) * m["decay"], 0.0)
            tinv = (ri == ci).astype(F32) - amat
            pw = amat
            for _ in range(5):
                pw = _hdot(pw, pw)
                tinv = tinv + _hdot(tinv, pw)
            u = _hdot(tinv, v * beta)
            w = _hdot(tinv, kb * m["eg"])
            attn = jnp.where(ri >= ci, _dot_nt(q, k) * m["decay"], 0.0)
            s = s_ref[h]
            v_new = u - _dot(w, s)
            o = _dot(q * m["eg"], s) + _dot(attn, v_new)
            sh_ref[0, h] = s
            th_ref[0, h] = tinv
            s_ref[h] = s * m["egl"] + _dot_tn(k * m["etail"], v_new)
            o_ref[:, h * DK:(h + 1) * DK] = o
            rs = lax.rsqrt(jnp.mean(o * o, axis=1, keepdims=True) + EPS)
            y_ref[:, h * DK:(h + 1) * DK] = o * rs * dnw_ref[...] * _silu(z_ref[:, h * DK:(h + 1) * DK])

    row = lambda w_: pl.BlockSpec((c, w_), lambda i: (i, 0))
    one = pl.BlockSpec((1, 128), lambda i: (0, 0))
    return pl.pallas_call(
        body, name="dn_scan_fwd", grid=(n,),
        in_specs=[row(1536), row(128), row(512), one, one, one],
        out_specs=[row(512), row(512), pl.BlockSpec((1, DN_HEADS, DK, DK), lambda i: (i, 0, 0, 0)),
                   pl.BlockSpec((1, DN_HEADS, c, c), lambda i: (i, 0, 0, 0))],
        out_shape=[jax.ShapeDtypeStruct((t, 512), F32), jax.ShapeDtypeStruct((t, 512), F32),
                   jax.ShapeDtypeStruct((n, DN_HEADS, DK, DK), F32), jax.ShapeDtypeStruct((n, DN_HEADS, c, c), F32)],
        scratch_shapes=[pltpu.VMEM((DN_HEADS, DK, DK), F32)],
        compiler_params=_params(("arbitrary",)),
    )(qkv, ba, z, arow, dtb, dnw)


def _att_prep(patt, g, gt, wq, wk):
    t = patt.shape[0]
    tm = 512

    def body(p_ref, g_ref, gt_ref, wq_ref, wk_ref, o_ref):
        for lo, w_ref in ((0, wq_ref), (512, wk_ref)):
            xv = p_ref[:, lo:lo + 512]
            rstd = lax.rsqrt(_hdot(xv * xv, g_ref[...]) * (1.0 / HD) + EPS)
            o_ref[:, lo:lo + 512] = xv * _hdot(rstd, gt_ref[...]) * w_ref[...]
        o_ref[:, 1024:1536] = p_ref[:, 1024:1536]

    full = lambda a: pl.BlockSpec(a.shape, lambda i: (0,) * a.ndim)
    return pl.pallas_call(
        body, name="att_prep", grid=(t // tm,),
        in_specs=[pl.BlockSpec((tm, 1536), lambda i: (i, 0)), full(g), full(gt), full(wq), full(wk)],
        out_specs=pl.BlockSpec((tm, 1536), lambda i: (i, 0)),
        out_shape=jax.ShapeDtypeStruct((t, 1536), F32),
        compiler_params=_params(("arbitrary",)),
    )(patt, g, gt, wq, wk)


def _bias_tables(rb_ref, bk_ref, bias_ref, pair):
    for p in range(len(PATTERNS)):
        bk = bk_ref[p]
        for hh in range(2):
            head = 2 * pair + hh
            bm = jnp.full((BLK, 2 * BLK), NEG, F32)
            for b in range(N_BUCKETS):
                bm = jnp.where(bk == b, rb_ref[head, b], bm)
            bias_ref[p, hh] = bm


def _block_rows(t, r, n):
    per_class = (t // r) // BLK
    res = n // per_class
    j = n % per_class
    start = res + BLK * r * j
    pstart = res + BLK * r * jnp.maximum(j - 1, 0)
    if r == 1:
        return pl.ds(pl.multiple_of(start, BLK), BLK), pl.ds(pl.multiple_of(pstart, BLK), BLK), j
    return pl.ds(start, BLK, stride=r), pl.ds(pstart, BLK, stride=r), j


def _att_fwd(qkv, gate, rb, bk):
    t = qkv.shape[0]
    rows = 512

    def body(rb_ref, bk_ref, q_ref, k_ref, v_ref, g_ref, o_ref, y_ref, lse_ref, acc_ref, ml_ref, bias_ref):
        pair = pl.program_id(0)
        _bias_tables(rb_ref, bk_ref, bias_ref, pair)
        lane = _iota((BLK, 128), 1)
        h0 = lane < HD
        for c in range(t // rows):
            sl = slice(c * rows, (c + 1) * rows)
            acc_ref[sl, :] = jnp.zeros((rows, 128), F32)
            ml_ref[sl, :] = jnp.where(_iota((rows, 128), 1) < 2, NEG, 0.0)

        for p, (_, r) in enumerate(PATTERNS):
            def blk(n, carry, p=p, r=r):
                cur, prev, j = _block_rows(t, r, n)
                qb, kc, vc = q_ref[cur, :], k_ref[cur, :], v_ref[cur, :]
                kp, vp = k_ref[prev, :], v_ref[prev, :]
                ml = ml_ref[cur, :]
                pmask = jnp.where(j > 0, 0.0, NEG)
                pv, m_new, l_new, scale = [], [], [], []
                for hh in range(2):
                    hm = h0 if hh == 0 else jnp.logical_not(h0)
                    qm = jnp.where(hm, qb, 0.0)
                    bias = bias_ref[p, hh]
                    s_p = _dot_nt(qm, kp) + bias[:, :BLK] + pmask
                    s_c = _dot_nt(qm, kc) + bias[:, BLK:]
                    m_b = jnp.maximum(jnp.max(s_p, axis=1, keepdims=True), jnp.max(s_c, axis=1, keepdims=True))
                    m_old, l_old = _lane_col(ml, hh), _lane_col(ml, 2 + hh)
                    m_n = jnp.maximum(m_old, m_b)
                    e_p, e_c = jnp.exp(s_p - m_n), jnp.exp(s_c - m_n)
                    a_old = jnp.exp(m_old - m_n)
                    l_n = l_old * a_old + jnp.sum(e_p, axis=1, keepdims=True) + jnp.sum(e_c, axis=1, keepdims=True)
                    pv.append(_dot(e_p, vp) + _dot(e_c, vc))
                    m_new.append(m_n)
                    l_new.append(l_n)
                    scale.append(a_old)
                acc = acc_ref[cur, :]
                acc_ref[cur, :] = jnp.where(h0, acc * scale[0] + pv[0], acc * scale[1] + pv[1])
                ml_ref[cur, :] = jnp.where(lane == 0, m_new[0], jnp.where(lane == 1, m_new[1],
                                           jnp.where(lane == 2, l_new[0], jnp.where(lane == 3, l_new[1], 0.0))))
                return carry

            lax.fori_loop(0, t // BLK, blk, 0)

        for c in range(t // rows):
            sl = slice(c * rows, (c + 1) * rows)
            ml = ml_ref[sl, :]
            ln = _iota((rows, 128), 1)
            l0, l1 = _lane_col(ml, 2), _lane_col(ml, 3)
            o = acc_ref[sl, :] / jnp.where(ln < HD, l0, l1)
            o_ref[sl, :] = o
            y_ref[sl, :] = o * _silu(g_ref[sl, :])
            lse0, lse1 = _lane_col(ml, 0) + jnp.log(l0), _lane_col(ml, 1) + jnp.log(l1)
            lse_ref[sl, :] = jnp.where(ln == 0, lse0, jnp.where(ln == 1, lse1, 0.0))

    col = lambda off: pl.BlockSpec((t, 128), lambda i, off=off: (0, off + i))
    return pl.pallas_call(
        body, name="att_fwd", grid=(ATT_HEADS // 2,),
        in_specs=[pl.BlockSpec(memory_space=pltpu.SMEM), pl.BlockSpec(bk.shape, lambda i: (0, 0, 0)),
                  col(0), col(4), col(8), col(0)],
        out_specs=[col(0), col(0), pl.BlockSpec((None, t, 128), lambda i: (i, 0, 0))],
        out_shape=[jax.ShapeDtypeStruct((t, 512), F32), jax.ShapeDtypeStruct((t, 512), F32),
                   jax.ShapeDtypeStruct((ATT_HEADS // 2, t, 128), F32)],
        scratch_shapes=[pltpu.VMEM((t, 128), F32), pltpu.VMEM((t, 128), F32),
                        pltpu.VMEM((len(PATTERNS), 2, BLK, 2 * BLK), F32)],
        compiler_params=_params(("arbitrary",)),
    )(rb, bk, qkv, qkv, qkv, gate)


def _outproj_loss(x, ydn, yatt, wout, target):
    t = x.shape[0]
    tm = 512

    def body(x_ref, a_ref, b_ref, w_ref, t_ref, dy_ref, mix_ref, loss_ref):
        @pl.when(pl.program_id(0) == 0)
        def _():
            loss_ref[...] = jnp.zeros_like(loss_ref)

        mix = jnp.concatenate([a_ref[...], b_ref[...]], axis=1).astype(MXU)
        mix_ref[...] = mix
        err = x_ref[...] + jnp.dot(mix, w_ref[...], preferred_element_type=F32) - t_ref[...]
        dy_ref[...] = err * (1.0 / D_MODEL)
        loss_ref[...] += jnp.sum(err * err) * (0.5 / D_MODEL)

    row = lambda n: pl.BlockSpec((tm, n), lambda i: (i, 0))
    return pl.pallas_call(
        body, name="outproj_loss", grid=(t // tm,),
        in_specs=[row(D_MODEL), row(512), row(512), pl.BlockSpec(wout.shape, lambda i: (0, 0)), row(D_MODEL)],
        out_specs=[row(D_MODEL), row(D_MODEL), pl.BlockSpec((8, 128), lambda i: (0, 0))],
        out_shape=[jax.ShapeDtypeStruct((t, D_MODEL), F32), jax.ShapeDtypeStruct((t, D_MODEL), MXU),
                   jax.ShapeDtypeStruct((8, 128), F32)],
        compiler_params=_params(("arbitrary",)),
    )(x, ydn, yatt, wout, target)


def _outproj_bwd(dy, wout_t, oraw, z, dnw, oatt, gate, gp):
    t = dy.shape[0]
    tm = 256

    def body(dy_ref, w_ref, o_ref, z_ref, dnw_ref, oa_ref, g_ref, gp_ref,
             do_ref, dz_ref, doa_ref, dg_ref, dd_ref, ddnw_ref):
        @pl.when(pl.program_id(0) == 0)
        def _():
            ddnw_ref[...] = jnp.zeros_like(ddnw_ref)

        dmix = jnp.dot(dy_ref[...].astype(MXU), w_ref[...], preferred_element_type=F32)
        dnw_v = dnw_ref[...]
        acc = jnp.zeros((1, DK), F32)
        for h in range(DN_HEADS):
            sl = slice(h * DK, (h + 1) * DK)
            o, zz, dm = o_ref[:, sl], z_ref[:, sl], dmix[:, sl]
            rs = lax.rsqrt(jnp.mean(o * o, axis=1, keepdims=True) + EPS)
            oh = o * rs
            dz_ref[:, sl] = dm * oh * dnw_v * _dsilu(zz)
            d_on = dm * _silu(zz)
            gg = d_on * dnw_v
            do_ref[:, sl] = rs * (gg - oh * jnp.mean(gg * oh, axis=1, keepdims=True))
            acc = acc + jnp.sum(d_on * oh, axis=0, keepdims=True)
        ddnw_ref[...] += jnp.broadcast_to(acc, (8, DK))
        da, gate_v, oa = dmix[:, 512:], g_ref[...], oa_ref[...]
        doa = da * _silu(gate_v)
        doa_ref[...] = doa
        dg_ref[...] = da * oa * _dsilu(gate_v)
        prod = doa * oa
        for p in range(ATT_HEADS // 2):
            dd_ref[p] = _hdot(prod, gp_ref[p])

    row = lambda n: pl.BlockSpec((tm, n), lambda i: (i, 0))
    full = lambda a: pl.BlockSpec(a.shape, lambda i: (0,) * a.ndim)
    return pl.pallas_call(
        body, name="outproj_bwd", grid=(t // tm,),
        in_specs=[row(D_MODEL), full(wout_t), row(512), row(512), full(dnw), row(512), row(512), full(gp)],
        out_specs=[row(512), row(512), row(512), row(512),
                   pl.BlockSpec((ATT_HEADS // 2, tm, 128), lambda i: (0, i, 0)), pl.BlockSpec((8, DK), lambda i: (0, 0))],
        out_shape=[jax.ShapeDtypeStruct((t, 512), F32)] * 4 +
                  [jax.ShapeDtypeStruct((ATT_HEADS // 2, t, 128), F32), jax.ShapeDtypeStruct((8, DK), F32)],
        compiler_params=_params(("arbitrary",)),
    )(dy, wout_t, oraw, z, dnw, oatt, gate, gp)


def _matmul_tn(a, b, name):
    t, m = a.shape
    n = b.shape[1]
    tk = 512
    tn = n if n <= 512 else 512

    def body(a_ref, b_ref, o_ref):
        @pl.when(pl.program_id(1) == 0)
        def _():
            o_ref[...] = jnp.zeros_like(o_ref)

        o_ref[...] += _dot_tn(a_ref[...], b_ref[...])

    return pl.pallas_call(
        body, name=name, grid=(n // tn, t // tk),
        in_specs=[pl.BlockSpec((tk, m), lambda j, k: (k, 0)), pl.BlockSpec((tk, tn), lambda j, k: (k, j))],
        out_specs=pl.BlockSpec((m, tn), lambda j, k: (0, j)),
        out_shape=jax.ShapeDtypeStruct((m, n), F32),
        compiler_params=_params(("arbitrary", "arbitrary")),
    )(a, b)


def _att_bwd(qkv, do, lse, dd, rb, bk):
    t = qkv.shape[0]
    rows = 512

    def body(rb_ref, bk_ref, q_ref, k_ref, v_ref, do_ref, lse_ref, dd_ref,
             dq_ref, dk_ref, dv_ref, db_ref, bias_ref, ds_ref):
        pair = pl.program_id(0)

        @pl.when(pair == 0)
        def _():
            db_ref[...] = jnp.zeros_like(db_ref)

        _bias_tables(rb_ref, bk_ref, bias_ref, pair)
        ds_ref[...] = jnp.zeros_like(ds_ref)
        for c in range(t // rows):
            sl = slice(c * rows, (c + 1) * rows)
            for ref in (dq_ref, dk_ref, dv_ref):
                ref[sl, :] = jnp.zeros((rows, 128), F32)
        lane = _iota((BLK, 128), 1)
        h0 = lane < HD

        for p, (_, r) in enumerate(PATTERNS):
            def blk(n, carry, p=p, r=r):
                cur, prev, j = _block_rows(t, r, n)
                qb, kc, vc, dob = q_ref[cur, :], k_ref[cur, :], v_ref[cur, :], do_ref[cur, :]
                kp, vp = k_ref[prev, :], v_ref[prev, :]
                lse_b, dd_b = lse_ref[cur, :], dd_ref[cur, :]
                pmask = jnp.where(j > 0, 0.0, NEG)
                outs = []
                for hh in range(2):
                    hm = h0 if hh == 0 else jnp.logical_not(h0)
                    qm, dom = jnp.where(hm, qb, 0.0), jnp.where(hm, dob, 0.0)
                    bias = bias_ref[p, hh]
                    lse_h, d_h = _lane_col(lse_b, hh), _lane_col(dd_b, hh)
                    p_p = jnp.exp(_dot_nt(qm, kp) + bias[:, :BLK] + pmask - lse_h)
                    p_c = jnp.exp(_dot_nt(qm, kc) + bias[:, BLK:] - lse_h)
                    ds_p = p_p * (_dot_nt(dom, vp) - d_h)
                    ds_c = p_c * (_dot_nt(dom, vc) - d_h)
                    ds_ref[p, hh, :, :BLK] += ds_p
                    ds_ref[p, hh, :, BLK:] += ds_c
                    outs.append((_dot(ds_p, kp) + _dot(ds_c, kc), _dot_tn(ds_c, qb), _dot_tn(ds_p, qb),
                                 _dot_tn(p_c, dob), _dot_tn(p_p, dob)))
                dq, dkc, dkp, dvc, dvp = (jnp.where(h0, a, b) for a, b in zip(*outs))
                dq_ref[cur, :] += dq
                dk_ref[cur, :] += dkc
                dv_ref[cur, :] += dvc
                dk_ref[prev, :] += dkp
                dv_ref[prev, :] += dvp
                return carry

            lax.fori_loop(0, t // BLK, blk, 0)

        ri, ci = _iota((8, 128), 0), _iota((8, 128), 1)
        upd = jnp.zeros((8, 128), F32)
        for p in range(len(PATTERNS)):
            bk = bk_ref[p]
            for hh in range(2):
                dsum = ds_ref[p, hh]
                for b in range(N_BUCKETS):
                    val = jnp.sum(jnp.where(bk == b, dsum, 0.0))
                    upd = upd + jnp.where((ri == 2 * pair + hh) & (ci == b), val, 0.0)
        db_ref[...] += upd

    col = lambda off: pl.BlockSpec((t, 128), lambda i, off=off: (0, off + i))
    pr = pl.BlockSpec((None, t, 128), lambda i: (i, 0, 0))
    return pl.pallas_call(
        body, name="att_bwd", grid=(ATT_HEADS // 2,),
        in_specs=[pl.BlockSpec(memory_space=pltpu.SMEM), pl.BlockSpec(bk.shape, lambda i: (0, 0, 0)),
                  col(0), col(4), col(8), col(0), pr, pr],
        out_specs=[col(0), col(0), col(0), pl.BlockSpec((8, 128), lambda i: (0, 0))],
        out_shape=[jax.ShapeDtypeStruct((t, 512), F32)] * 3 + [jax.ShapeDtypeStruct((8, 128), F32)],
        scratch_shapes=[pltpu.VMEM((len(PATTERNS), 2, BLK, 2 * BLK), F32),
                        pltpu.VMEM((len(PATTERNS), 2, BLK, 2 * BLK), F32)],
        compiler_params=_params(("arbitrary",)),
    )(rb, bk, qkv, qkv, qkv, do, lse, dd)


def _att_prep_bwd(patt, dq, dk, dv, g, gt, wq, wk):
    t = patt.shape[0]
    tm = 512

    def body(p_ref, dq_ref, dk_ref, dv_ref, g_ref, gt_ref, wq_ref, wk_ref, o_ref, dwq_ref, dwk_ref):
        @pl.when(pl.program_id(0) == 0)
        def _():
            dwq_ref[...] = jnp.zeros_like(dwq_ref)
            dwk_ref[...] = jnp.zeros_like(dwk_ref)

        for lo, w_ref, d_ref, dw_ref in ((0, wq_ref, dq_ref, dwq_ref), (512, wk_ref, dk_ref, dwk_ref)):
            xv, dyv = p_ref[:, lo:lo + 512], d_ref[...]
            rstd = lax.rsqrt(_hdot(xv * xv, g_ref[...]) * (1.0 / HD) + EPS)
            rsb = _hdot(rstd, gt_ref[...])
            xh = xv * rsb
            gg = dyv * w_ref[...]
            mean = _hdot(_hdot(gg * xh, g_ref[...]) * (1.0 / HD), gt_ref[...])
            o_ref[:, lo:lo + 512] = rsb * (gg - xh * mean)
            dw_ref[...] += jnp.broadcast_to(jnp.sum(dyv * xh, axis=0, keepdims=True), (8, 512))
        o_ref[:, 1024:1536] = dv_ref[...]

    row = lambda n: pl.BlockSpec((tm, n), lambda i: (i, 0))
    full = lambda a: pl.BlockSpec(a.shape, lambda i: (0,) * a.ndim)
    acc = pl.BlockSpec((8, 512), lambda i: (0, 0))
    return pl.pallas_call(
        body, name="att_prep_bwd", grid=(t // tm,),
        in_specs=[row(1536), row(512), row(512), row(512), full(g), full(gt), full(wq), full(wk)],
        out_specs=[row(1536), acc, acc],
        out_shape=[jax.ShapeDtypeStruct((t, 1536), F32), jax.ShapeDtypeStruct((8, 512), F32),
                   jax.ShapeDtypeStruct((8, 512), F32)],
        compiler_params=_params(("arbitrary",)),
    )(patt, dq, dk, dv, g, gt, wq, wk)


def _dn_scan_bwd(qkv, ba, do, sh, th, arow, dtb):
    t = qkv.shape[0]
    n = t // CHUNK
    c = CHUNK

    def body(qkv_ref, ba_ref, do_ref, sh_ref, th_ref, arow_ref, dtb_ref, dqkv_ref, dba_ref, ds_ref):
        @pl.when(pl.program_id(0) == 0)
        def _():
            ds_ref[...] = jnp.zeros_like(ds_ref)

        qkv_v, ba_v = qkv_ref[...], ba_ref[...]
        lane = _iota((c, 128), 1)
        row = _iota((c, 1), 0)
        dba = jnp.zeros((c, 128), F32)
        for h in range(DN_HEADS):
            m = _chunk_common(qkv_v, ba_v, arow_ref[...], dtb_ref[...], h)
            q, k, v, beta = m["q"], m["k"], m["v"], m["beta"]
            ri, ci, decay, eg, egl, etail = m["ri"], m["ci"], m["decay"], m["eg"], m["egl"], m["etail"]
            s, tinv, d_o, d_s = sh_ref[0, h], th_ref[0, h], do_ref[:, h * DK:(h + 1) * DK], ds_ref[h]
            kb, vb = k * beta, v * beta
            kbg = kb * eg
            kk, qk = _dot_nt(kb, k), _dot_nt(q, k)
            amat = jnp.where(ri > ci, kk * decay, 0.0)
            attn = jnp.where(ri >= ci, qk * decay, 0.0)
            u, w = _hdot(tinv, vb), _hdot(tinv, kbg)
            v_new = u - _dot(w, s)
            q_dec, k_tail = q * eg, k * etail

            d_vnew = _dot_tn(attn, d_o) + _dot(k_tail, d_s)
            d_attn = jnp.where(ri >= ci, _dot_nt(d_o, v_new), 0.0)
            d_qdec = _dot_nt(d_o, s)
            ds_ref[h] = _dot_tn(q_dec, d_o) + d_s * egl - _dot_tn(w, d_vnew)
            d_ktail = _dot_nt(v_new, d_s)
            d_gl = jnp.sum(s * d_s) * egl[:, 0:1]
            d_w = -_dot_nt(d_vnew, s)
            d_vb, d_kbg = _hdot_tn(tinv, d_vnew), _hdot_tn(tinv, d_w)
            d_a = -jnp.where(ri > ci, _hdot_nt(d_vb, u) + _hdot_nt(d_kbg, w), 0.0)
            d_qk, d_kk = d_attn * decay, d_a * decay
            d_kb = _dot(d_kk, k) + d_kbg * eg
            d_q = _dot(d_qk, k) + d_qdec * eg
            d_k = _dot_tn(d_qk, q) + _dot_tn(d_kk, kb) + d_ktail * etail + d_kb * beta
            d_v = d_vb * beta
            d_beta = jnp.sum(d_kb * k + d_vb * v, axis=1, keepdims=True)
            mm = d_a * amat + d_attn * attn
            tail_term = jnp.sum(d_ktail * k_tail, axis=1, keepdims=True)
            d_gc = (jnp.sum(mm, axis=1, keepdims=True) - _hdot_tn(mm, jnp.ones((c, 128), F32))[:, 0:1]
                    + jnp.sum(d_qdec * q_dec + d_kbg * kbg, axis=1, keepdims=True) - tail_term)
            d_gc = d_gc + jnp.where(row == c - 1, jnp.sum(tail_term) + d_gl, 0.0)
            triu = (ri <= ci).astype(F32)
            d_g = _hdot(triu, jnp.broadcast_to(d_gc, (c, 128)))[:, 0:1]
            d_braw = d_beta * beta * (1.0 - beta)
            d_araw = d_g * m["a_h"] * _sigmoid(m["a_raw"] + m["dt_h"])
            dba = dba + jnp.where(lane == h, d_braw, 0.0) + jnp.where(lane == DN_HEADS + h, d_araw, 0.0) \
                + jnp.where(lane == 2 * DN_HEADS + h, d_g * m["g"], 0.0)
            dqkv_ref[:, h * DK:(h + 1) * DK] = d_q
            dqkv_ref[:, D_DN + h * DK:D_DN + (h + 1) * DK] = d_k
            dqkv_ref[:, 2 * D_DN + h * DK:2 * D_DN + (h + 1) * DK] = d_v
        dba_ref[...] = dba

    rev = lambda w_: pl.BlockSpec((c, w_), lambda i: (n - 1 - i, 0))
    one = pl.BlockSpec((1, 128), lambda i: (0, 0))
    return pl.pallas_call(
        body, name="dn_scan_bwd", grid=(n,),
        in_specs=[rev(1536), rev(128), rev(512), pl.BlockSpec((1, DN_HEADS, DK, DK), lambda i: (n - 1 - i, 0, 0, 0)),
                  pl.BlockSpec((1, DN_HEADS, c, c), lambda i: (n - 1 - i, 0, 0, 0)), one, one],
        out_specs=[rev(1536), rev(128)],
        out_shape=[jax.ShapeDtypeStruct((t, 1536), F32), jax.ShapeDtypeStruct((t, 128), F32)],
        scratch_shapes=[pltpu.VMEM((DN_HEADS, DK, DK), F32)],
        compiler_params=_params(("arbitrary",)),
    )(qkv, ba, do, sh, th, arow, dtb)


def _dn_prep_bwd(pdn, cw, dact):
    t = pdn.shape[0]
    nchunk = t // CONV_ROWS

    def body(u_ref, w_ref, d_ref, du_ref, dw_ref, dy_ref):
        j = pl.program_id(0)
        dy_ref[t:t + 8, :] = jnp.zeros((8, 128), F32)
        dw = [jnp.zeros((1, 128), F32) for _ in range(4)]
        for c in range(nchunk):
            sl = slice(c * CONV_ROWS, (c + 1) * CONV_ROWS)
            taps, y = _conv_taps(u_ref, c, w_ref)
            a = _silu(y)
            dout = d_ref[sl, :]
            rs = lax.rsqrt(jnp.sum(a * a, axis=1, keepdims=True) + EPS)
            f = jnp.where(j < 8, rs, 1.0) * jnp.where(j < 4, DK ** -0.5, 1.0)
            corr = jnp.where(j < 8, f * rs * rs * jnp.sum(dout * a, axis=1, keepdims=True), 0.0)
            dy = (f * dout - corr * a) * _dsilu(y)
            dy_ref[sl, :] = dy
            for k_ in range(4):
                dw[3 - k_] = dw[3 - k_] + jnp.sum(taps[k_] * dy, axis=0, keepdims=True)
        for i in range(4):
            dw_ref[i:i + 1, :] = dw[i]
        for c in range(nchunk):
            r0 = c * CONV_ROWS
            ext = dy_ref[r0:r0 + CONV_ROWS + 8, :]
            du = ext[:CONV_ROWS, :] * w_ref[3:4, :]
            for k_ in (1, 2, 3):
                du = du + pltpu.roll(ext, CONV_ROWS + 8 - k_, 0)[:CONV_ROWS, :] * w_ref[3 - k_:4 - k_, :]
            du_ref[r0:r0 + CONV_ROWS, :] = du

    return pl.pallas_call(
        body, name="dn_prep_bwd", grid=(12,),
        in_specs=[pl.BlockSpec((t, 128), lambda j: (0, j)), pl.BlockSpec((4, 128), lambda j: (0, j)),
                  pl.BlockSpec((t, 128), lambda j: (0, j))],
        out_specs=[pl.BlockSpec((t, 128), lambda j: (0, j)), pl.BlockSpec((4, 128), lambda j: (0, j))],
        out_shape=[jax.ShapeDtypeStruct((t, 1536), F32), jax.ShapeDtypeStruct((4, 1536), F32)],
        scratch_shapes=[pltpu.VMEM((t + 8, 128), F32)],
        compiler_params=_params(("arbitrary",)),
    )(pdn, cw, dact)


def _inproj_bwd(x, nw, wt, dy, dpdn, dz, dpatt, dgate, dba):
    t = x.shape[0]
    tm = 256

    def body(x_ref, nw_ref, w_ref, dy_ref, a_ref, b_ref, c_ref, d_ref, e_ref, gx_ref, dnw_ref, cs_ref):
        @pl.when(pl.program_id(0) == 0)
        def _():
            dnw_ref[...] = jnp.zeros_like(dnw_ref)
            cs_ref[...] = jnp.zeros_like(cs_ref)

        dh = jnp.zeros((tm, D_MODEL), F32)
        for ref, lo, hi in ((a_ref, 0, 1536), (b_ref, 1536, 2048), (c_ref, 2048, 3584),
                            (d_ref, 3584, 4096), (e_ref, 4096, 4224)):
            dh = dh + jnp.dot(ref[...].astype(MXU), w_ref[lo:hi, :], preferred_element_type=F32)
        xv = x_ref[...]
        rstd = lax.rsqrt(jnp.mean(xv * xv, axis=-1, keepdims=True) + EPS)
        xh = xv * rstd
        gg = dh * nw_ref[...]
        gx_ref[...] = rstd * (gg - xh * jnp.mean(gg * xh, axis=-1, keepdims=True)) + dy_ref[...]
        dnw_ref[...] += jnp.broadcast_to(jnp.sum(dh * xh, axis=0, keepdims=True), (8, D_MODEL))
        cs_ref[...] += jnp.broadcast_to(jnp.sum(e_ref[...], axis=0, keepdims=True), (8, 128))

    row = lambda n: pl.BlockSpec((tm, n), lambda i: (i, 0))
    full = lambda a: pl.BlockSpec(a.shape, lambda i: (0,) * a.ndim)
    return pl.pallas_call(
        body, name="inproj_bwd", grid=(t // tm,),
        in_specs=[row(D_MODEL), full(nw), full(wt), row(D_MODEL), row(1536), row(512), row(1536), row(512), row(128)],
        out_specs=[row(D_MODEL), pl.BlockSpec((8, D_MODEL), lambda i: (0, 0)), pl.BlockSpec((8, 128), lambda i: (0, 0))],
        out_shape=[jax.ShapeDtypeStruct((t, D_MODEL), F32), jax.ShapeDtypeStruct((8, D_MODEL), F32),
                   jax.ShapeDtypeStruct((8, 128), F32)],
        compiler_params=_params(("arbitrary",)),
    )(x, nw, wt, dy, dpdn, dz, dpatt, dgate, dba)


def _adamw_sum8(w, g8, m, v, name):
    r, c = w.shape
    tr = r if r <= 256 else 256
    c1 = 1.0 - ADAM_B1 ** ADAM_STEP
    c2 = 1.0 - ADAM_B2 ** ADAM_STEP

    def body(w_ref, g_ref, m_ref, v_ref, go_ref, d_ref, mo_ref, vo_ref):
        g = g_ref[0]
        for s in range(1, N_DEV):
            g = g + g_ref[s]
        mn = ADAM_B1 * m_ref[...] + (1.0 - ADAM_B1) * g
        vn = ADAM_B2 * v_ref[...] + (1.0 - ADAM_B2) * (g * g)
        go_ref[...] = g
        mo_ref[...] = mn
        vo_ref[...] = vn
        d_ref[...] = -ADAM_LR * ((mn / c1) / (jnp.sqrt(vn / c2) + ADAM_EPS) + ADAM_WD * w_ref[...])

    blk = pl.BlockSpec((tr, c), lambda i: (i, 0))
    return pl.pallas_call(
        body, name=name, grid=(r // tr,),
        in_specs=[blk, pl.BlockSpec((N_DEV, tr, c), lambda i: (0, i, 0)), blk, blk],
        out_specs=[blk] * 4, out_shape=[jax.ShapeDtypeStruct((r, c), F32)] * 4,
        compiler_params=_params(("arbitrary",)),
    )(w, g8, m, v)


def _local_step(x, target, norm_w, w_sect, conv_w, a_log, dt_bias, dn_norm_w, q_norm_w, k_norm_w, rel_bias, w_out):
    lane = np.arange(128)
    arow = jnp.zeros((1, 128), F32).at[0, DN_HEADS:2 * DN_HEADS].set(-jnp.exp(a_log[0]))
    dtb = jnp.zeros((1, 128), F32).at[0, DN_HEADS:2 * DN_HEADS].set(dt_bias[0])
    g_np, gt_np, gp_np = _group_mats()
    g, gt, gp = jnp.asarray(g_np), jnp.asarray(gt_np), jnp.asarray(gp_np)
    bk = jnp.asarray(_bucket_tables())
    wq = jnp.tile(q_norm_w, (1, ATT_HEADS)) * (HD ** -0.5)
    wk = jnp.tile(k_norm_w, (1, ATT_HEADS))
    del lane

    h, pdn, z, patt, gate, ba = _inproj(x, norm_w, w_sect)
    qkv_dn = _dn_prep(pdn, conv_w)
    oraw, ydn, sh, th = _dn_scan_fwd(qkv_dn, ba, z, arow, dtb, dn_norm_w)
    qkv_att = _att_prep(patt, g, gt, wq, wk)
    oatt, yatt, lse = _att_fwd(qkv_att, gate, rel_bias, bk)
    dy, mix, loss8 = _outproj_loss(x, ydn, yatt, w_out, target)

    do_dn, dz, do_att, dgate, dd, ddnw = _outproj_bwd(dy, w_out.T, oraw, z, dn_norm_w, oatt, gate, gp)
    d_wout = _matmul_tn(mix, dy, "dw_out")
    dq, dk, dv, drb = _att_bwd(qkv_att, do_att, lse, dd, rel_bias, bk)
    dpatt, dwq8, dwk8 = _att_prep_bwd(patt, dq, dk, dv, g, gt, wq, wk)
    dqkv_dn, dba = _dn_scan_bwd(qkv_dn, ba, do_dn, sh, th, arow, dtb)
    dpdn, d_conv = _dn_prep_bwd(pdn, conv_w, dqkv_dn)
    grad_x, dnw8, cs8 = _inproj_bwd(x, norm_w, w_sect.T, dy, dpdn, dz, dpatt, dgate, dba)
    dw_sections = [_matmul_tn(h, d_, nm) for d_, nm in
                   ((dpdn, "dw_in_dn"), (dz, "dw_in_z"), (dpatt, "dw_in_att"), (dgate, "dw_in_gate"), (dba, "dw_in_ba"))]

    grads = dict(
        loss=loss8[0, 0], grad_x=grad_x, norm_w=dnw8[0:1], w_in_sections=dw_sections, conv_w=d_conv,
        a_log=cs8[0:1, 2 * DN_HEADS:3 * DN_HEADS], dt_bias=cs8[0:1, DN_HEADS:2 * DN_HEADS],
        dn_norm_w=ddnw[0:1],
        q_norm_w=(dwq8[0].reshape(ATT_HEADS, HD).sum(0) * (HD ** -0.5))[None],
        k_norm_w=dwk8[0].reshape(ATT_HEADS, HD).sum(0)[None],
        rel_bias=drb[:, :N_BUCKETS], w_out=d_wout)
    return grads


SMALL = (("norm_w", 1024), ("a_log", 4), ("dt_bias", 4), ("dn_norm_w", 128), ("q_norm_w", 64), ("k_norm_w", 64),
         ("rel_bias", 256))
SMALL_ROWS = 16


def _pack_small(d):
    flat = jnp.concatenate([d[k].reshape(-1) for k, _ in SMALL])
    return jnp.pad(flat, (0, SMALL_ROWS * 128 - flat.shape[0])).reshape(SMALL_ROWS, 128)


def _unpack_small(p, shapes):
    flat = p.reshape(-1)
    out, off = {}, 0
    for k, n in SMALL:
        out[k] = flat[off:off + n].reshape(shapes[k])
        off += n
    return out


def kernel(x, norm_w, w_in, conv_w, a_log, dt_bias, dn_norm_w, q_norm_w, k_norm_w, rel_bias, w_out, loss_target, m_norm_w, m_w_in, m_conv_w, m_a_log, m_dt_bias, m_dn_norm_w, m_q_norm_w, m_k_norm_w, m_rel_bias, m_w_out, v_norm_w, v_w_in, v_conv_w, v_a_log, v_dt_bias, v_dn_norm_w, v_q_norm_w, v_k_norm_w, v_rel_bias, v_w_out):
    ncol = w_in.shape[2]
    d_in = N_DEV * ncol
    win8, wout8, conv8 = _exchange([w_in[0].astype(MXU), w_out[0].astype(MXU), conv_w[0]], True, "gather_weights")
    w_full = win8.transpose(1, 0, 2).reshape(D_MODEL, d_in)
    w_sect = jnp.concatenate([w_full[:, :2048], w_full[:, 2056:], w_full[:, 2048:2056],
                              jnp.zeros((D_MODEL, W_COLS - d_in), MXU)], axis=1)
    wout_full = wout8.reshape(D_MODEL, D_MODEL)
    conv_full = conv8.transpose(1, 0, 2).reshape(4, 3 * D_DN)

    gr = _local_step(x[0], loss_target[0], norm_w, w_sect, conv_full, a_log, dt_bias, dn_norm_w, q_norm_w, k_norm_w,
                     rel_bias, wout_full)

    s_dn, s_z, s_att, s_gate, s_ba = gr["w_in_sections"]
    dw_full = jnp.concatenate([s_dn, s_z, s_ba[:, :2 * DN_HEADS], s_att, s_gate], axis=1)
    dw8 = dw_full.reshape(D_MODEL, N_DEV, ncol).transpose(1, 0, 2)
    dwout8 = gr["w_out"].reshape(N_DEV, D_MODEL // N_DEV, D_MODEL)
    dconv8 = gr["conv_w"].reshape(4, N_DEV, 3 * D_DN // N_DEV).transpose(1, 0, 2)
    small = _pack_small(gr)
    small8 = jnp.broadcast_to(small[None], (N_DEV, SMALL_ROWS, 128))
    r_win, r_wout, r_conv, r_small = _exchange([dw8, dwout8, dconv8, small8], False, "exchange_grads")

    g_win, d_win, m_win, v_win = _adamw_sum8(w_in[0], r_win, m_w_in[0], v_w_in[0], "adamw_w_in")
    g_wout, d_wout, m_wout, v_wout = _adamw_sum8(w_out[0], r_wout, m_w_out[0], v_w_out[0], "adamw_w_out")
    g_conv, d_conv, m_conv, v_conv = _adamw_sum8(conv_w[0], r_conv, m_conv_w[0], v_conv_w[0], "adamw_conv_w")
    ws = dict(norm_w=norm_w, a_log=a_log, dt_bias=dt_bias, dn_norm_w=dn_norm_w, q_norm_w=q_norm_w,
              k_norm_w=k_norm_w, rel_bias=rel_bias)
    ms = dict(norm_w=m_norm_w, a_log=m_a_log, dt_bias=m_dt_bias, dn_norm_w=m_dn_norm_w, q_norm_w=m_q_norm_w,
              k_norm_w=m_k_norm_w, rel_bias=m_rel_bias)
    vs = dict(norm_w=v_norm_w, a_log=v_a_log, dt_bias=v_dt_bias, dn_norm_w=v_dn_norm_w, q_norm_w=v_q_norm_w,
              k_norm_w=v_k_norm_w, rel_bias=v_rel_bias)
    shapes = {k: a.shape for k, a in ws.items()}
    packs = _adamw_sum8(_pack_small(ws), r_small, _pack_small(ms), _pack_small(vs), "adamw_small")
    g_s, d_s, m_s, v_s = (_unpack_small(p, shapes) for p in packs)

    loss = lax.psum(gr["loss"], AXES)
    names = ("norm_w", "w_in", "conv_w", "a_log", "dt_bias", "dn_norm_w", "q_norm_w", "k_norm_w", "rel_bias", "w_out")
    big = dict(w_in=(g_win, d_win, m_win, v_win), conv_w=(g_conv, d_conv, m_conv, v_conv),
               w_out=(g_wout, d_wout, m_wout, v_wout))
    outs = [loss, gr["grad_x"][None]]
    for i, sm in enumerate((g_s, d_s, m_s, v_s)):
        for nm in names:
            outs.append(big[nm][i][None] if nm in big else sm[nm])
    return tuple(outs)
```

```python
import functools
import math

import numpy as np
import jax
import jax.numpy as jnp
from jax import lax
from jax.experimental import pallas as pl
from jax.experimental.pallas import tpu as pltpu

F32 = jnp.float32
MXU = jnp.bfloat16
GRAD_WIRE = jnp.bfloat16
HI = lax.Precision.HIGHEST

D_MODEL = 1024
D_DN = 512
DN_HEADS = 4
DK = 128
CHUNK = 64
D_ATT = 512
ATT_HEADS = 8
HD = 64
PATTERNS = ((128, 1), (512, 4), (2048, 16))
BLK = 128
N_BUCKETS = 32
MAX_DISTANCE = 2048
EPS = 1e-6
W_COLS = 4224
N_DEV = 8
AXES = ("x", "y", "c")

ADAM_LR = 0.001
ADAM_B1 = 0.9
ADAM_B2 = 0.999
ADAM_EPS = 1e-08
ADAM_WD = 0.01
ADAM_STEP = 10

VMEM_LIMIT = 56 * 1024 * 1024
NEG = -1e30


def _dot(a, b):
    return jnp.dot(a.astype(MXU), b.astype(MXU), preferred_element_type=F32)


def _dot_nt(a, b):
    return lax.dot_general(a.astype(MXU), b.astype(MXU), (((1,), (1,)), ((), ())), preferred_element_type=F32)


def _dot_tn(a, b):
    return lax.dot_general(a.astype(MXU), b.astype(MXU), (((0,), (0,)), ((), ())), preferred_element_type=F32)


def _hdot(a, b):
    return jnp.dot(a, b, precision=HI, preferred_element_type=F32)


def _hdot_nt(a, b):
    return lax.dot_general(a, b, (((1,), (1,)), ((), ())), precision=HI, preferred_element_type=F32)


def _hdot_tn(a, b):
    return lax.dot_general(a, b, (((0,), (0,)), ((), ())), precision=HI, preferred_element_type=F32)


def _sigmoid(x):
    return 1.0 / (1.0 + jnp.exp(-x))


def _silu(x):
    return x * _sigmoid(x)


def _dsilu(x):
    s = _sigmoid(x)
    return s * (1.0 + x * (1.0 - s))


def _softplus(x):
    return jnp.maximum(x, 0.0) + jnp.log(1.0 + jnp.exp(-jnp.abs(x)))


def _iota(shape, dim):
    return lax.broadcasted_iota(jnp.int32, shape, dim)


def _lane_col(x, k):
    return jnp.sum(jnp.where(_iota(x.shape, 1) == k, x, 0.0), axis=1, keepdims=True)


def _params(sem=None):
    return pltpu.CompilerParams(dimension_semantics=sem, vmem_limit_bytes=VMEM_LIMIT)


def _t5_bucket(dist):
    max_exact = N_BUCKETS // 2
    d = np.maximum(dist, 1).astype(np.float64)
    large = max_exact + (np.log(d / max_exact) / math.log(MAX_DISTANCE / max_exact)
                         * (N_BUCKETS - max_exact)).astype(np.int32)
    large = np.minimum(large, N_BUCKETS - 1)
    return np.where(dist < max_exact, dist, large).astype(np.int32)


def _bucket_tables():
    qi = np.arange(BLK)[:, None]
    kj = np.arange(2 * BLK)[None, :]
    step = qi - kj + BLK
    band = (step >= 0) & (step <= BLK)
    out = []
    for _, r in PATTERNS:
        b = _t5_bucket(np.clip(step, 0, None) * r)
        out.append(np.where(band, b, -1))
    return np.stack(out).astype(np.int32)


def _group_mats():
    g = np.zeros((D_ATT, 128), np.float32)
    for h in range(ATT_HEADS):
        g[h * HD:(h + 1) * HD, h] = 1.0
    gp = np.zeros((ATT_HEADS // 2, D_ATT, 128), np.float32)
    for h in range(ATT_HEADS):
        gp[h // 2, h * HD:(h + 1) * HD, h % 2] = 1.0
    return g, np.ascontiguousarray(g.T), gp


CHIP_FLIPS = ((1, 0), (0, 1), (1, 1))
ANY_SPEC = pl.BlockSpec(memory_space=pl.ANY)
MESH_ID = pl.DeviceIdType.MESH


def _other_chips():
    x, y = lax.axis_index("x"), lax.axis_index("y")
    return [((1 - x if fx else x), (1 - y if fy else y)) for fx, fy in CHIP_FLIPS]


def _gather_weights(arrs):
    n = len(arrs)

    def body(*refs):
        ins, outs = refs[:n], refs[n:2 * n]
        send, recv, loc = refs[2 * n:]
        x, y, c = (lax.axis_index(a) for a in AXES)
        sib = (x, y, 1 - c)
        chips = _other_chips()
        lin = lambda px, py, pc: 4 * px + 2 * py + pc

        def copy(a, k, block, to, src=None):
            slot = outs[a].at[lin(*block)]
            return pltpu.make_async_remote_copy(src_ref=slot if src is None else src, dst_ref=slot,
                                                send_sem=send.at[a, k], recv_sem=recv.at[a, k],
                                                device_id=to, device_id_type=MESH_ID)

        started = []
        for a in range(n):
            mine = pltpu.make_async_copy(ins[a], outs[a].at[lin(x, y, c)], loc.at[a])
            mine.start()
            started.append(mine)
        firsts = []
        for a in range(n):
            firsts.append(copy(a, 0, (x, y, c), sib, src=ins[a]))
            firsts += [copy(a, 1 + j, (x, y, c), (*chip, c), src=ins[a]) for j, chip in enumerate(chips)]
        for cp in firsts:
            cp.start()
        passed = []
        for j, chip in enumerate(chips):
            for a in range(n):
                copy(a, 1 + j, (*chip, c), (x, y, c)).wait_recv()
                fw = copy(a, 4 + j, (*chip, c), sib)
                fw.start()
                passed.append(fw)
        for a in range(n):
            copy(a, 0, sib, (x, y, c)).wait_recv()
            for j, chip in enumerate(chips):
                copy(a, 4 + j, (*chip, 1 - c), (x, y, c)).wait_recv()
        for cp in firsts + passed:
            cp.wait_send()
        for mine in started:
            mine.wait()

    return pl.pallas_call(
        body, name="gather_weights", out_shape=[jax.ShapeDtypeStruct((N_DEV,) + a.shape, a.dtype) for a in arrs],
        in_specs=[ANY_SPEC] * n, out_specs=[ANY_SPEC] * n,
        scratch_shapes=[pltpu.SemaphoreType.DMA((n, 7)), pltpu.SemaphoreType.DMA((n, 7)), pltpu.SemaphoreType.DMA((n,))],
    )(*arrs)


def _swap_siblings(arrs):
    n = len(arrs)

    def body(*refs):
        ins, outs = refs[:n], refs[n:2 * n]
        send, recv = refs[2 * n:]
        x, y, c = (lax.axis_index(a) for a in AXES)
        cps = [pltpu.make_async_remote_copy(src_ref=ins[a].at[1 - c], dst_ref=outs[a], send_sem=send.at[a],
                                            recv_sem=recv.at[a], device_id=(x, y, 1 - c), device_id_type=MESH_ID)
               for a in range(n)]
        for cp in cps:
            cp.start()
        for cp in cps:
            cp.wait()

    return pl.pallas_call(
        body, name="swap_siblings", out_shape=[jax.ShapeDtypeStruct(a.shape[1:], a.dtype) for a in arrs],
        in_specs=[ANY_SPEC] * n, out_specs=[ANY_SPEC] * n,
        scratch_shapes=[pltpu.SemaphoreType.DMA((n,)), pltpu.SemaphoreType.DMA((n,))],
    )(*arrs)


def _chip_sum(mine2, theirs, core, wire, name):
    _, nchip, r, cdim = mine2.shape
    tr = r if r <= 256 else 256

    def body(core_ref, a_ref, b_ref, o_ref):
        del core_ref
        o_ref[...] = (a_ref[...] + b_ref[...]).astype(wire)

    grid_spec = pltpu.PrefetchScalarGridSpec(
        num_scalar_prefetch=1, grid=(nchip, r // tr),
        in_specs=[pl.BlockSpec((None, None, tr, cdim), lambda j, i, cr: (cr[0], j, i, 0)),
                  pl.BlockSpec((None, tr, cdim), lambda j, i, cr: (j, i, 0))],
        out_specs=pl.BlockSpec((None, tr, cdim), lambda j, i, cr: (j, i, 0)))
    return pl.pallas_call(
        body, name=name, grid_spec=grid_spec, out_shape=jax.ShapeDtypeStruct((nchip, r, cdim), wire),
        compiler_params=_params(("arbitrary", "arbitrary")),
    )(core, mine2, theirs)


def _swap_chips(arrs):
    n = len(arrs)

    def body(*refs):
        ins, outs = refs[:n], refs[n:2 * n]
        send, recv, loc = refs[2 * n:]
        x, y, c = (lax.axis_index(a) for a in AXES)
        me = 2 * x + y
        chips = _other_chips()
        locs, sends, recvs = [], [], []
        for a in range(n):
            lc = pltpu.make_async_copy(ins[a].at[me], outs[a].at[me], loc.at[a])
            lc.start()
            locs.append(lc)
            for j, (px, py) in enumerate(chips):
                them = 2 * px + py
                cp = pltpu.make_async_remote_copy(src_ref=ins[a].at[them], dst_ref=outs[a].at[me], send_sem=send.at[a, j],
                                                  recv_sem=recv.at[a, j], device_id=(px, py, c), device_id_type=MESH_ID)
                cp.start()
                sends.append(cp)
                recvs.append(pltpu.make_async_remote_copy(src_ref=ins[a].at[them], dst_ref=outs[a].at[them],
                                                          send_sem=send.at[a, j], recv_sem=recv.at[a, j],
                                                          device_id=(px, py, c), device_id_type=MESH_ID))
        for cp in recvs:
            cp.wait_recv()
        for cp in sends:
            cp.wait_send()
        for lc in locs:
            lc.wait()

    return pl.pallas_call(
        body, name="swap_chips", out_shape=[jax.ShapeDtypeStruct(a.shape, a.dtype) for a in arrs],
        in_specs=[ANY_SPEC] * n, out_specs=[ANY_SPEC] * n,
        scratch_shapes=[pltpu.SemaphoreType.DMA((n, 3)), pltpu.SemaphoreType.DMA((n, 3)), pltpu.SemaphoreType.DMA((n,))],
    )(*arrs)


def _inproj(x, nw, w):
    t = x.shape[0]
    tm = 256

    def body(x_ref, nw_ref, w_ref, h_ref, pdn_ref, z_ref, patt_ref, gate_ref, ba_ref):
        xv = x_ref[...]
        rstd = lax.rsqrt(jnp.mean(xv * xv, axis=-1, keepdims=True) + EPS)
        h = (xv * rstd * nw_ref[...]).astype(MXU)
        h_ref[...] = h
        for ref, lo, hi in ((pdn_ref, 0, 1536), (z_ref, 1536, 2048), (patt_ref, 2048, 3584),
                            (gate_ref, 3584, 4096), (ba_ref, 4096, 4224)):
            ref[...] = jnp.dot(h, w_ref[:, lo:hi], preferred_element_type=F32)

    row = lambda n: pl.BlockSpec((tm, n), lambda i: (i, 0))
    full = lambda a: pl.BlockSpec(a.shape, lambda i: (0,) * a.ndim)
    return pl.pallas_call(
        body, name="inproj", grid=(t // tm,),
        in_specs=[row(D_MODEL), full(nw), full(w)],
        out_specs=[row(D_MODEL), row(1536), row(512), row(1536), row(512), row(128)],
        out_shape=[jax.ShapeDtypeStruct((t, D_MODEL), MXU)] +
                  [jax.ShapeDtypeStruct((t, n), F32) for n in (1536, 512, 1536, 512, 128)],
        compiler_params=_params(("arbitrary",)),
    )(x, nw, w)


CONV_ROWS = 512


def _conv_taps(u_ref, c, w_ref):
    r0 = c * CONV_ROWS
    if c == 0:
        ext = jnp.concatenate([jnp.zeros((8, 128), F32), u_ref[0:CONV_ROWS, :]], axis=0)
    else:
        ext = u_ref[r0 - 8:r0 + CONV_ROWS, :]
    taps = [ext[8:, :]] + [pltpu.roll(ext, k, 0)[8:, :] for k in (1, 2, 3)]
    y = taps[0] * w_ref[3:4, :]
    for k in (1, 2, 3):
        y = y + taps[k] * w_ref[3 - k:4 - k, :]
    return taps, y


def _dn_prep(pdn, cw):
    t = pdn.shape[0]

    def body(u_ref, w_ref, o_ref):
        j = pl.program_id(0)
        for c in range(t // CONV_ROWS):
            _, y = _conv_taps(u_ref, c, w_ref)
            a = _silu(y)
            ssq = jnp.sum(a * a, axis=1, keepdims=True)
            f = jnp.where(j < 8, lax.rsqrt(ssq + EPS), 1.0) * jnp.where(j < 4, DK ** -0.5, 1.0)
            o_ref[c * CONV_ROWS:(c + 1) * CONV_ROWS, :] = a * f

    return pl.pallas_call(
        body, name="dn_prep", grid=(12,),
        in_specs=[pl.BlockSpec((t, 128), lambda j: (0, j)), pl.BlockSpec((4, 128), lambda j: (0, j))],
        out_specs=pl.BlockSpec((t, 128), lambda j: (0, j)),
        out_shape=jax.ShapeDtypeStruct((t, 1536), F32),
        compiler_params=_params(("arbitrary",)),
    )(pdn, cw)


def _chunk_common(qkv, ba, arow, dtb):
    c = CHUNK
    ri, ci = _iota((c, c), 0), _iota((c, c), 1)
    lane = _iota((c, 128), 1)
    g_all = jnp.where((lane >= DN_HEADS) & (lane < 2 * DN_HEADS), arow * _softplus(ba + dtb), 0.0)
    gc_all = _hdot((ri >= ci).astype(F32), g_all)
    gc_t = gc_all.T
    beta_all = _sigmoid(ba)
    out = []
    for h in range(DN_HEADS):
        gc = _lane_col(gc_all, DN_HEADS + h)
        gcr = gc_t[DN_HEADS + h:DN_HEADS + h + 1, :]
        gl = gc[c - 1:c, :]
        out.append(dict(
            q=qkv[:, h * DK:(h + 1) * DK], k=qkv[:, D_DN + h * DK:D_DN + (h + 1) * DK],
            v=qkv[:, 2 * D_DN + h * DK:2 * D_DN + (h + 1) * DK],
            beta=_lane_col(beta_all, h), g=_lane_col(g_all, DN_HEADS + h),
            a_raw=_lane_col(ba, DN_HEADS + h), a_h=_lane_col(arow, DN_HEADS + h), dt_h=_lane_col(dtb, DN_HEADS + h),
            decay=jnp.exp(jnp.where(ri >= ci, gc - gcr, NEG)), eg=jnp.exp(gc), egl=jnp.exp(gl), etail=jnp.exp(gl - gc)))
    return out, ri, ci


def _dn_scan_fwd(qkv, ba, z, arow, dtb, dnw):
    t = qkv.shape[0]
    n = t // CHUNK
    c = CHUNK
    hs = range(DN_HEADS)

    def body(qkv_ref, ba_ref, z_ref, arow_ref, dtb_ref, dnw_ref, o_ref, y_ref, sh_ref, th_ref, s_ref):
        @pl.when(pl.program_id(0) == 0)
        def _():
            s_ref[...] = jnp.zeros_like(s_ref)

        ms, ri, ci = _chunk_common(qkv_ref[...], ba_ref[...], arow_ref[...], dtb_ref[...])
        kb = [m["k"] * m["beta"] for m in ms]
        amat = [jnp.where(ri > ci, _dot_nt(kb[h], ms[h]["k"]) * ms[h]["decay"], 0.0) for h in hs]
        attn = [jnp.where(ri >= ci, _dot_nt(m["q"], m["k"]) * m["decay"], 0.0) for m in ms]
        eye = (ri == ci).astype(F32)
        tinv = [eye - a for a in amat]
        pw = amat
        for _ in range(5):
            pw = [_hdot(p, p) for p in pw]
            tinv = [tv + _hdot(tv, p) for tv, p in zip(tinv, pw)]
        u = [_hdot(tinv[h], ms[h]["v"] * ms[h]["beta"]) for h in hs]
        w = [_hdot(tinv[h], kb[h] * ms[h]["eg"]) for h in hs]
        s = [s_ref[h] for h in hs]
        v_new = [u[h] - _dot(w[h], s[h]) for h in hs]
        o = [_dot(ms[h]["q"] * ms[h]["eg"], s[h]) + _dot(attn[h], v_new[h]) for h in hs]
        for h in hs:
            s_ref[h] = s[h] * ms[h]["egl"] + _dot_tn(ms[h]["k"] * ms[h]["etail"], v_new[h])
            sh_ref[0, h] = s[h]
            th_ref[0, h] = tinv[h]
            o_ref[:, h * DK:(h + 1) * DK] = o[h]
            rs = lax.rsqrt(jnp.mean(o[h] * o[h], axis=1, keepdims=True) + EPS)
            y_ref[:, h * DK:(h + 1) * DK] = o[h] * rs * dnw_ref[...] * _silu(z_ref[:, h * DK:(h + 1) * DK])

    row = lambda w_: pl.BlockSpec((c, w_), lambda i: (i, 0))
    one = pl.BlockSpec((1, 128), lambda i: (0, 0))
    return pl.pallas_call(
        body, name="dn_scan_fwd", grid=(n,),
        in_specs=[row(1536), row(128), row(512), one, one, one],
        out_specs=[row(512), row(512), pl.BlockSpec((1, DN_HEADS, DK, DK), lambda i: (i, 0, 0, 0)),
                   pl.BlockSpec((1, DN_HEADS, c, c), lambda i: (i, 0, 0, 0))],
        out_shape=[jax.ShapeDtypeStruct((t, 512), F32), jax.ShapeDtypeStruct((t, 512), F32),
                   jax.ShapeDtypeStruct((n, DN_HEADS, DK, DK), F32), jax.ShapeDtypeStruct((n, DN_HEADS, c, c), F32)],
        scratch_shapes=[pltpu.VMEM((DN_HEADS, DK, DK), F32)],
        compiler_params=_params(("arbitrary",)),
    )(qkv, ba, z, arow, dtb, dnw)


def _att_prep(patt, g, gt, wq, wk):
    t = patt.shape[0]
    tm = 512

    def body(p_ref, g_ref, gt_ref, wq_ref, wk_ref, o_ref):
        for lo, w_ref in ((0, wq_ref), (512, wk_ref)):
            xv = p_ref[:, lo:lo + 512]
            rstd = lax.rsqrt(_hdot(xv * xv, g_ref[...]) * (1.0 / HD) + EPS)
            o_ref[:, lo:lo + 512] = xv * _hdot(rstd, gt_ref[...]) * w_ref[...]
        o_ref[:, 1024:1536] = p_ref[:, 1024:1536]

    full = lambda a: pl.BlockSpec(a.shape, lambda i: (0,) * a.ndim)
    return pl.pallas_call(
        body, name="att_prep", grid=(t // tm,),
        in_specs=[pl.BlockSpec((tm, 1536), lambda i: (i, 0)), full(g), full(gt), full(wq), full(wk)],
        out_specs=pl.BlockSpec((tm, 1536), lambda i: (i, 0)),
        out_shape=jax.ShapeDtypeStruct((t, 1536), F32),
        compiler_params=_params(("arbitrary",)),
    )(patt, g, gt, wq, wk)


def _bias_tables(rb_ref, bk_ref, bias_ref, pair):
    for p in range(len(PATTERNS)):
        bk = bk_ref[p]
        for hh in range(2):
            head = 2 * pair + hh
            bm = jnp.full((BLK, 2 * BLK), NEG, F32)
            for b in range(N_BUCKETS):
                bm = jnp.where(bk == b, rb_ref[head, b], bm)
            bias_ref[p, hh] = bm


def _block_rows(t, r, n):
    per_class = (t // r) // BLK
    res = n // per_class
    j = n % per_class
    start = res + BLK * r * j
    pstart = res + BLK * r * jnp.maximum(j - 1, 0)
    if r == 1:
        return pl.ds(pl.multiple_of(start, BLK), BLK), pl.ds(pl.multiple_of(pstart, BLK), BLK), j
    return pl.ds(start, BLK, stride=r), pl.ds(pstart, BLK, stride=r), j


def _att_fwd(qkv, gate, rb, bk):
    t = qkv.shape[0]
    rows = 512

    def body(rb_ref, bk_ref, q_ref, k_ref, v_ref, g_ref, o_ref, y_ref, lse_ref, acc_ref, ml_ref, bias_ref):
        pair = pl.program_id(0)
        _bias_tables(rb_ref, bk_ref, bias_ref, pair)
        lane = _iota((BLK, 128), 1)
        h0 = lane < HD
        for c in range(t // rows):
            sl = slice(c * rows, (c + 1) * rows)
            acc_ref[sl, :] = jnp.zeros((rows, 128), F32)
            ml_ref[sl, :] = jnp.where(_iota((rows, 128), 1) < 2, NEG, 0.0)

        for p, (_, r) in enumerate(PATTERNS):
            def blk(n, carry, p=p, r=r):
                cur, prev, j = _block_rows(t, r, n)
                qb, kc, vc = q_ref[cur, :], k_ref[cur, :], v_ref[cur, :]
                kp, vp = k_ref[prev, :], v_ref[prev, :]
                ml = ml_ref[cur, :]
                pmask = jnp.where(j > 0, 0.0, NEG)
                pv, m_new, l_new, scale = [], [], [], []
                for hh in range(2):
                    hm = h0 if hh == 0 else jnp.logical_not(h0)
                    qm = jnp.where(hm, qb, 0.0)
                    bias = bias_ref[p, hh]
                    s_p = _dot_nt(qm, kp) + bias[:, :BLK] + pmask
                    s_c = _dot_nt(qm, kc) + bias[:, BLK:]
                    m_b = jnp.maximum(jnp.max(s_p, axis=1, keepdims=True), jnp.max(s_c, axis=1, keepdims=True))
                    m_old, l_old = _lane_col(ml, hh), _lane_col(ml, 2 + hh)
                    m_n = jnp.maximum(m_old, m_b)
                    e_p, e_c = jnp.exp(s_p - m_n), jnp.exp(s_c - m_n)
                    a_old = jnp.exp(m_old - m_n)
                    l_n = l_old * a_old + jnp.sum(e_p, axis=1, keepdims=True) + jnp.sum(e_c, axis=1, keepdims=True)
                    pv.append(_dot(e_p, vp) + _dot(e_c, vc))
                    m_new.append(m_n)
                    l_new.append(l_n)
                    scale.append(a_old)
                acc = acc_ref[cur, :]
                acc_ref[cur, :] = jnp.where(h0, acc * scale[0] + pv[0], acc * scale[1] + pv[1])
                ml_ref[cur, :] = jnp.where(lane == 0, m_new[0], jnp.where(lane == 1, m_new[1],
                                           jnp.where(lane == 2, l_new[0], jnp.where(lane == 3, l_new[1], 0.0))))
                return carry

            lax.fori_loop(0, t // BLK, blk, 0)

        for c in range(t // rows):
            sl = slice(c * rows, (c + 1) * rows)
            ml = ml_ref[sl, :]
            ln = _iota((rows, 128), 1)
            l0, l1 = _lane_col(ml, 2), _lane_col(ml, 3)
            o = acc_ref[sl, :] / jnp.where(ln < HD, l0, l1)
            o_ref[sl, :] = o
            y_ref[sl, :] = o * _silu(g_ref[sl, :])
            lse0, lse1 = _lane_col(ml, 0) + jnp.log(l0), _lane_col(ml, 1) + jnp.log(l1)
            lse_ref[sl, :] = jnp.where(ln == 0, lse0, jnp.where(ln == 1, lse1, 0.0))

    col = lambda off: pl.BlockSpec((t, 128), lambda i, off=off: (0, off + i))
    return pl.pallas_call(
        body, name="att_fwd", grid=(ATT_HEADS // 2,),
        in_specs=[pl.BlockSpec(memory_space=pltpu.SMEM), pl.BlockSpec(bk.shape, lambda i: (0, 0, 0)),
                  col(0), col(4), col(8), col(0)],
        out_specs=[col(0), col(0), pl.BlockSpec((None, t, 128), lambda i: (i, 0, 0))],
        out_shape=[jax.ShapeDtypeStruct((t, 512), F32), jax.ShapeDtypeStruct((t, 512), F32),
                   jax.ShapeDtypeStruct((ATT_HEADS // 2, t, 128), F32)],
        scratch_shapes=[pltpu.VMEM((t, 128), F32), pltpu.VMEM((t, 128), F32),
                        pltpu.VMEM((len(PATTERNS), 2, BLK, 2 * BLK), F32)],
        compiler_params=_params(("arbitrary",)),
    )(rb, bk, qkv, qkv, qkv, gate)


def _outproj_loss(x, ydn, yatt, wout, target):
    t = x.shape[0]
    tm = 512

    def body(x_ref, a_ref, b_ref, w_ref, t_ref, dy_ref, mix_ref, loss_ref):
        @pl.when(pl.program_id(0) == 0)
        def _():
            loss_ref[...] = jnp.zeros_like(loss_ref)

        mix = jnp.concatenate([a_ref[...], b_ref[...]], axis=1).astype(MXU)
        mix_ref[...] = mix
        err = x_ref[...] + jnp.dot(mix, w_ref[...], preferred_element_type=F32) - t_ref[...]
        dy_ref[...] = err * (1.0 / D_MODEL)
        loss_ref[...] += jnp.sum(err * err) * (0.5 / D_MODEL)

    row = lambda n: pl.BlockSpec((tm, n), lambda i: (i, 0))
    return pl.pallas_call(
        body, name="outproj_loss", grid=(t // tm,),
        in_specs=[row(D_MODEL), row(512), row(512), pl.BlockSpec(wout.shape, lambda i: (0, 0)), row(D_MODEL)],
        out_specs=[row(D_MODEL), row(D_MODEL), pl.BlockSpec((8, 128), lambda i: (0, 0))],
        out_shape=[jax.ShapeDtypeStruct((t, D_MODEL), F32), jax.ShapeDtypeStruct((t, D_MODEL), MXU),
                   jax.ShapeDtypeStruct((8, 128), F32)],
        compiler_params=_params(("arbitrary",)),
    )(x, ydn, yatt, wout, target)


def _outproj_bwd(dy, wout_t, oraw, z, dnw, oatt, gate, gp):
    t = dy.shape[0]
    tm = 256

    def body(dy_ref, w_ref, o_ref, z_ref, dnw_ref, oa_ref, g_ref, gp_ref,
             do_ref, dz_ref, doa_ref, dg_ref, dd_ref, ddnw_ref):
        @pl.when(pl.program_id(0) == 0)
        def _():
            ddnw_ref[...] = jnp.zeros_like(ddnw_ref)

        dmix = jnp.dot(dy_ref[...].astype(MXU), w_ref[...], preferred_element_type=F32)
        dnw_v = dnw_ref[...]
        acc = jnp.zeros((1, DK), F32)
        for h in range(DN_HEADS):
            sl = slice(h * DK, (h + 1) * DK)
            o, zz, dm = o_ref[:, sl], z_ref[:, sl], dmix[:, sl]
            rs = lax.rsqrt(jnp.mean(o * o, axis=1, keepdims=True) + EPS)
            oh = o * rs
            dz_ref[:, sl] = dm * oh * dnw_v * _dsilu(zz)
            d_on = dm * _silu(zz)
            gg = d_on * dnw_v
            do_ref[:, sl] = rs * (gg - oh * jnp.mean(gg * oh, axis=1, keepdims=True))
            acc = acc + jnp.sum(d_on * oh, axis=0, keepdims=True)
        ddnw_ref[...] += jnp.broadcast_to(acc, (8, DK))
        da, gate_v, oa = dmix[:, 512:], g_ref[...], oa_ref[...]
        doa = da * _silu(gate_v)
        doa_ref[...] = doa
        dg_ref[...] = da * oa * _dsilu(gate_v)
        prod = doa * oa
        for p in range(ATT_HEADS // 2):
            dd_ref[p] = _hdot(prod, gp_ref[p])

    row = lambda n: pl.BlockSpec((tm, n), lambda i: (i, 0))
    full = lambda a: pl.BlockSpec(a.shape, lambda i: (0,) * a.ndim)
    return pl.pallas_call(
        body, name="outproj_bwd", grid=(t // tm,),
        in_specs=[row(D_MODEL), full(wout_t), row(512), row(512), full(dnw), row(512), row(512), full(gp)],
        out_specs=[row(512), row(512), row(512), row(512),
                   pl.BlockSpec((ATT_HEADS // 2, tm, 128), lambda i: (0, i, 0)), pl.BlockSpec((8, DK), lambda i: (0, 0))],
        out_shape=[jax.ShapeDtypeStruct((t, 512), F32)] * 4 +
                  [jax.ShapeDtypeStruct((ATT_HEADS // 2, t, 128), F32), jax.ShapeDtypeStruct((8, DK), F32)],
        compiler_params=_params(("arbitrary",)),
    )(dy, wout_t, oraw, z, dnw, oatt, gate, gp)


def _matmul_tn(a, b, name):
    t, m = a.shape
    n = b.shape[1]
    tk = 512
    tn = n if n <= 512 else 512

    def body(a_ref, b_ref, o_ref):
        @pl.when(pl.program_id(1) == 0)
        def _():
            o_ref[...] = jnp.zeros_like(o_ref)

        o_ref[...] += _dot_tn(a_ref[...], b_ref[...])

    return pl.pallas_call(
        body, name=name, grid=(n // tn, t // tk),
        in_specs=[pl.BlockSpec((tk, m), lambda j, k: (k, 0)), pl.BlockSpec((tk, tn), lambda j, k: (k, j))],
        out_specs=pl.BlockSpec((m, tn), lambda j, k: (0, j)),
        out_shape=jax.ShapeDtypeStruct((m, n), F32),
        compiler_params=_params(("arbitrary", "arbitrary")),
    )(a, b)


def _att_bwd(qkv, do, lse, dd, rb, bk):
    t = qkv.shape[0]
    rows = 512

    def body(rb_ref, bk_ref, q_ref, k_ref, v_ref, do_ref, lse_ref, dd_ref,
             dq_ref, dk_ref, dv_ref, db_ref, bias_ref, ds_ref):
        pair = pl.program_id(0)

        @pl.when(pair == 0)
        def _():
            db_ref[...] = jnp.zeros_like(db_ref)

        _bias_tables(rb_ref, bk_ref, bias_ref, pair)
        ds_ref[...] = jnp.zeros_like(ds_ref)
        for c in range(t // rows):
            sl = slice(c * rows, (c + 1) * rows)
            for ref in (dq_ref, dk_ref, dv_ref):
                ref[sl, :] = jnp.zeros((rows, 128), F32)
        lane = _iota((BLK, 128), 1)
        h0 = lane < HD

        for p, (_, r) in enumerate(PATTERNS):
            def blk(n, carry, p=p, r=r):
                cur, prev, j = _block_rows(t, r, n)
                qb, kc, vc, dob = q_ref[cur, :], k_ref[cur, :], v_ref[cur, :], do_ref[cur, :]
                kp, vp = k_ref[prev, :], v_ref[prev, :]
                lse_b, dd_b = lse_ref[cur, :], dd_ref[cur, :]
                pmask = jnp.where(j > 0, 0.0, NEG)
                outs = []
                for hh in range(2):
                    hm = h0 if hh == 0 else jnp.logical_not(h0)
                    qm, dom = jnp.where(hm, qb, 0.0), jnp.where(hm, dob, 0.0)
                    bias = bias_ref[p, hh]
                    lse_h, d_h = _lane_col(lse_b, hh), _lane_col(dd_b, hh)
                    p_p = jnp.exp(_dot_nt(qm, kp) + bias[:, :BLK] + pmask - lse_h)
                    p_c = jnp.exp(_dot_nt(qm, kc) + bias[:, BLK:] - lse_h)
                    ds_p = p_p * (_dot_nt(dom, vp) - d_h)
                    ds_c = p_c * (_dot_nt(dom, vc) - d_h)
                    ds_ref[p, hh, :, :BLK] += ds_p
                    ds_ref[p, hh, :, BLK:] += ds_c
                    outs.append((_dot(ds_p, kp) + _dot(ds_c, kc), _dot_tn(ds_c, qb), _dot_tn(ds_p, qb),
                                 _dot_tn(p_c, dob), _dot_tn(p_p, dob)))
                dq, dkc, dkp, dvc, dvp = (jnp.where(h0, a, b) for a, b in zip(*outs))
                dq_ref[cur, :] += dq
                dk_ref[cur, :] += dkc
                dv_ref[cur, :] += dvc
                dk_ref[prev, :] += dkp
                dv_ref[prev, :] += dvp
                return carry

            lax.fori_loop(0, t // BLK, blk, 0)

        ri, ci = _iota((8, 128), 0), _iota((8, 128), 1)
        upd = jnp.zeros((8, 128), F32)
        for p in range(len(PATTERNS)):
            bk = bk_ref[p]
            for hh in range(2):
                dsum = ds_ref[p, hh]
                for b in range(N_BUCKETS):
                    val = jnp.sum(jnp.where(bk == b, dsum, 0.0))
                    upd = upd + jnp.where((ri == 2 * pair + hh) & (ci == b), val, 0.0)
        db_ref[...] += upd

    col = lambda off: pl.BlockSpec((t, 128), lambda i, off=off: (0, off + i))
    pr = pl.BlockSpec((None, t, 128), lambda i: (i, 0, 0))
    return pl.pallas_call(
        body, name="att_bwd", grid=(ATT_HEADS // 2,),
        in_specs=[pl.BlockSpec(memory_space=pltpu.SMEM), pl.BlockSpec(bk.shape, lambda i: (0, 0, 0)),
                  col(0), col(4), col(8), col(0), pr, pr],
        out_specs=[col(0), col(0), col(0), pl.BlockSpec((8, 128), lambda i: (0, 0))],
        out_shape=[jax.ShapeDtypeStruct((t, 512), F32)] * 3 + [jax.ShapeDtypeStruct((8, 128), F32)],
        scratch_shapes=[pltpu.VMEM((len(PATTERNS), 2, BLK, 2 * BLK), F32),
                        pltpu.VMEM((len(PATTERNS), 2, BLK, 2 * BLK), F32)],
        compiler_params=_params(("arbitrary",)),
    )(rb, bk, qkv, qkv, qkv, do, lse, dd)


def _att_prep_bwd(patt, dq, dk, dv, g, gt, wq, wk):
    t = patt.shape[0]
    tm = 512

    def body(p_ref, dq_ref, dk_ref, dv_ref, g_ref, gt_ref, wq_ref, wk_ref, o_ref, dwq_ref, dwk_ref):
        @pl.when(pl.program_id(0) == 0)
        def _():
            dwq_ref[...] = jnp.zeros_like(dwq_ref)
            dwk_ref[...] = jnp.zeros_like(dwk_ref)

        for lo, w_ref, d_ref, dw_ref in ((0, wq_ref, dq_ref, dwq_ref), (512, wk_ref, dk_ref, dwk_ref)):
            xv, dyv = p_ref[:, lo:lo + 512], d_ref[...]
            rstd = lax.rsqrt(_hdot(xv * xv, g_ref[...]) * (1.0 / HD) + EPS)
            rsb = _hdot(rstd, gt_ref[...])
            xh = xv * rsb
            gg = dyv * w_ref[...]
            mean = _hdot(_hdot(gg * xh, g_ref[...]) * (1.0 / HD), gt_ref[...])
            o_ref[:, lo:lo + 512] = rsb * (gg - xh * mean)
            dw_ref[...] += jnp.broadcast_to(jnp.sum(dyv * xh, axis=0, keepdims=True), (8, 512))
        o_ref[:, 1024:1536] = dv_ref[...]

    row = lambda n: pl.BlockSpec((tm, n), lambda i: (i, 0))
    full = lambda a: pl.BlockSpec(a.shape, lambda i: (0,) * a.ndim)
    acc = pl.BlockSpec((8, 512), lambda i: (0, 0))
    return pl.pallas_call(
        body, name="att_prep_bwd", grid=(t // tm,),
        in_specs=[row(1536), row(512), row(512), row(512), full(g), full(gt), full(wq), full(wk)],
        out_specs=[row(1536), acc, acc],
        out_shape=[jax.ShapeDtypeStruct((t, 1536), F32), jax.ShapeDtypeStruct((8, 512), F32),
                   jax.ShapeDtypeStruct((8, 512), F32)],
        compiler_params=_params(("arbitrary",)),
    )(patt, dq, dk, dv, g, gt, wq, wk)


def _dn_scan_bwd(qkv, ba, do, sh, th, arow, dtb):
    t = qkv.shape[0]
    n = t // CHUNK
    c = CHUNK

    def body(qkv_ref, ba_ref, do_ref, sh_ref, th_ref, arow_ref, dtb_ref, dqkv_ref, dba_ref, ds_ref):
        @pl.when(pl.program_id(0) == 0)
        def _():
            ds_ref[...] = jnp.zeros_like(ds_ref)

        hs = range(DN_HEADS)
        ms, ri, ci = _chunk_common(qkv_ref[...], ba_ref[...], arow_ref[...], dtb_ref[...])
        lane = _iota((c, 128), 1)
        row = _iota((c, 1), 0)
        q, k, v = [m["q"] for m in ms], [m["k"] for m in ms], [m["v"] for m in ms]
        beta, decay = [m["beta"] for m in ms], [m["decay"] for m in ms]
        eg, egl, etail = [m["eg"] for m in ms], [m["egl"] for m in ms], [m["etail"] for m in ms]
        s, tinv = [sh_ref[0, h] for h in hs], [th_ref[0, h] for h in hs]
        d_o, d_s = [do_ref[:, h * DK:(h + 1) * DK] for h in hs], [ds_ref[h] for h in hs]
        kb = [k[h] * beta[h] for h in hs]
        vb = [v[h] * beta[h] for h in hs]
        kbg = [kb[h] * eg[h] for h in hs]
        amat = [jnp.where(ri > ci, _dot_nt(kb[h], k[h]) * decay[h], 0.0) for h in hs]
        attn = [jnp.where(ri >= ci, _dot_nt(q[h], k[h]) * decay[h], 0.0) for h in hs]
        u = [_hdot(tinv[h], vb[h]) for h in hs]
        w = [_hdot(tinv[h], kbg[h]) for h in hs]
        v_new = [u[h] - _dot(w[h], s[h]) for h in hs]
        q_dec = [q[h] * eg[h] for h in hs]
        k_tail = [k[h] * etail[h] for h in hs]

        d_vnew = [_dot_tn(attn[h], d_o[h]) + _dot(k_tail[h], d_s[h]) for h in hs]
        d_attn = [jnp.where(ri >= ci, _dot_nt(d_o[h], v_new[h]), 0.0) for h in hs]
        d_qdec = [_dot_nt(d_o[h], s[h]) for h in hs]
        for h in hs:
            ds_ref[h] = _dot_tn(q_dec[h], d_o[h]) + d_s[h] * egl[h] - _dot_tn(w[h], d_vnew[h])
        d_ktail = [_dot_nt(v_new[h], d_s[h]) for h in hs]
        d_gl = [jnp.sum(s[h] * d_s[h]) * egl[h] for h in hs]
        d_w = [-_dot_nt(d_vnew[h], s[h]) for h in hs]
        d_vb = [_hdot_tn(tinv[h], d_vnew[h]) for h in hs]
        d_kbg = [_hdot_tn(tinv[h], d_w[h]) for h in hs]
        d_a = [-jnp.where(ri > ci, _hdot_nt(d_vb[h], u[h]) + _hdot_nt(d_kbg[h], w[h]), 0.0) for h in hs]
        d_qk = [d_attn[h] * decay[h] for h in hs]
        d_kk = [d_a[h] * decay[h] for h in hs]
        d_kb = [_dot(d_kk[h], k[h]) + d_kbg[h] * eg[h] for h in hs]
        d_q = [_dot(d_qk[h], k[h]) + d_qdec[h] * eg[h] for h in hs]
        d_k = [_dot_tn(d_qk[h], q[h]) + _dot_tn(d_kk[h], kb[h]) + d_ktail[h] * etail[h] + d_kb[h] * beta[h] for h in hs]
        d_beta = [jnp.sum(d_kb[h] * k[h] + d_vb[h] * v[h], axis=1, keepdims=True) for h in hs]
        mm = [d_a[h] * amat[h] + d_attn[h] * attn[h] for h in hs]
        rows = jnp.zeros((c, c), F32)
        for h in hs:
            rows = rows + jnp.where(ri == h, jnp.sum(mm[h], axis=0, keepdims=True), 0.0)
        cols_t = jnp.concatenate([rows, jnp.zeros((c, c), F32)], axis=1).T[:c, :]
        d_gc_all = jnp.zeros((c, 128), F32)
        for h in hs:
            tail_term = jnp.sum(d_ktail[h] * k_tail[h], axis=1, keepdims=True)
            d_gc = (jnp.sum(mm[h], axis=1, keepdims=True) - _lane_col(cols_t, h)
                    + jnp.sum(d_qdec[h] * q_dec[h] + d_kbg[h] * kbg[h], axis=1, keepdims=True) - tail_term)
            d_gc = d_gc + jnp.where(row == c - 1, jnp.sum(tail_term) + d_gl[h], 0.0)
            d_gc_all = d_gc_all + jnp.where(lane == DN_HEADS + h, d_gc, 0.0)
        d_g_all = _hdot((ri <= ci).astype(F32), d_gc_all)
        dba = jnp.zeros((c, 128), F32)
        for h in hs:
            d_g = _lane_col(d_g_all, DN_HEADS + h)
            d_braw = d_beta[h] * beta[h] * (1.0 - beta[h])
            d_araw = d_g * ms[h]["a_h"] * _sigmoid(ms[h]["a_raw"] + ms[h]["dt_h"])
            dba = dba + jnp.where(lane == h, d_braw, 0.0) + jnp.where(lane == DN_HEADS + h, d_araw, 0.0) \
                + jnp.where(lane == 2 * DN_HEADS + h, d_g * ms[h]["g"], 0.0)
            dqkv_ref[:, h * DK:(h + 1) * DK] = d_q[h]
            dqkv_ref[:, D_DN + h * DK:D_DN + (h + 1) * DK] = d_k[h]
            dqkv_ref[:, 2 * D_DN + h * DK:2 * D_DN + (h + 1) * DK] = d_vb[h] * beta[h]
        dba_ref[...] = dba

    rev = lambda w_: pl.BlockSpec((c, w_), lambda i: (n - 1 - i, 0))
    one = pl.BlockSpec((1, 128), lambda i: (0, 0))
    return pl.pallas_call(
        body, name="dn_scan_bwd", grid=(n,),
        in_specs=[rev(1536), rev(128), rev(512), pl.BlockSpec((1, DN_HEADS, DK, DK), lambda i: (n - 1 - i, 0, 0, 0)),
                  pl.BlockSpec((1, DN_HEADS, c, c), lambda i: (n - 1 - i, 0, 0, 0)), one, one],
        out_specs=[rev(1536), rev(128)],
        out_shape=[jax.ShapeDtypeStruct((t, 1536), F32), jax.ShapeDtypeStruct((t, 128), F32)],
        scratch_shapes=[pltpu.VMEM((DN_HEADS, DK, DK), F32)],
        compiler_params=_params(("arbitrary",)),
    )(qkv, ba, do, sh, th, arow, dtb)


def _dn_prep_bwd(pdn, cw, dact):
    t = pdn.shape[0]
    nchunk = t // CONV_ROWS

    def body(u_ref, w_ref, d_ref, du_ref, dw_ref, dy_ref):
        j = pl.program_id(0)
        dy_ref[t:t + 8, :] = jnp.zeros((8, 128), F32)
        dw = [jnp.zeros((1, 128), F32) for _ in range(4)]
        for c in range(nchunk):
            sl = slice(c * CONV_ROWS, (c + 1) * CONV_ROWS)
            taps, y = _conv_taps(u_ref, c, w_ref)
            a = _silu(y)
            dout = d_ref[sl, :]
            rs = lax.rsqrt(jnp.sum(a * a, axis=1, keepdims=True) + EPS)
            f = jnp.where(j < 8, rs, 1.0) * jnp.where(j < 4, DK ** -0.5, 1.0)
            corr = jnp.where(j < 8, f * rs * rs * jnp.sum(dout * a, axis=1, keepdims=True), 0.0)
            dy = (f * dout - corr * a) * _dsilu(y)
            dy_ref[sl, :] = dy
            for k_ in range(4):
                dw[3 - k_] = dw[3 - k_] + jnp.sum(taps[k_] * dy, axis=0, keepdims=True)
        for i in range(4):
            dw_ref[i:i + 1, :] = dw[i]
        for c in range(nchunk):
            r0 = c * CONV_ROWS
            ext = dy_ref[r0:r0 + CONV_ROWS + 8, :]
            du = ext[:CONV_ROWS, :] * w_ref[3:4, :]
            for k_ in (1, 2, 3):
                du = du + pltpu.roll(ext, CONV_ROWS + 8 - k_, 0)[:CONV_ROWS, :] * w_ref[3 - k_:4 - k_, :]
            du_ref[r0:r0 + CONV_ROWS, :] = du

    return pl.pallas_call(
        body, name="dn_prep_bwd", grid=(12,),
        in_specs=[pl.BlockSpec((t, 128), lambda j: (0, j)), pl.BlockSpec((4, 128), lambda j: (0, j)),
                  pl.BlockSpec((t, 128), lambda j: (0, j))],
        out_specs=[pl.BlockSpec((t, 128), lambda j: (0, j)), pl.BlockSpec((4, 128), lambda j: (0, j))],
        out_shape=[jax.ShapeDtypeStruct((t, 1536), F32), jax.ShapeDtypeStruct((4, 1536), F32)],
        scratch_shapes=[pltpu.VMEM((t + 8, 128), F32)],
        compiler_params=_params(("arbitrary",)),
    )(pdn, cw, dact)


def _inproj_bwd(x, nw, wt, dy, dpdn, dz, dpatt, dgate, dba):
    t = x.shape[0]
    tm = 256

    def body(x_ref, nw_ref, w_ref, dy_ref, a_ref, b_ref, c_ref, d_ref, e_ref, gx_ref, dnw_ref, cs_ref):
        @pl.when(pl.program_id(0) == 0)
        def _():
            dnw_ref[...] = jnp.zeros_like(dnw_ref)
            cs_ref[...] = jnp.zeros_like(cs_ref)

        dh = jnp.zeros((tm, D_MODEL), F32)
        for ref, lo, hi in ((a_ref, 0, 1536), (b_ref, 1536, 2048), (c_ref, 2048, 3584),
                            (d_ref, 3584, 4096), (e_ref, 4096, 4224)):
            dh = dh + jnp.dot(ref[...].astype(MXU), w_ref[lo:hi, :], preferred_element_type=F32)
        xv = x_ref[...]
        rstd = lax.rsqrt(jnp.mean(xv * xv, axis=-1, keepdims=True) + EPS)
        xh = xv * rstd
        gg = dh * nw_ref[...]
        gx_ref[...] = rstd * (gg - xh * jnp.mean(gg * xh, axis=-1, keepdims=True)) + dy_ref[...]
        dnw_ref[...] += jnp.broadcast_to(jnp.sum(dh * xh, axis=0, keepdims=True), (8, D_MODEL))
        cs_ref[...] += jnp.broadcast_to(jnp.sum(e_ref[...], axis=0, keepdims=True), (8, 128))

    row = lambda n: pl.BlockSpec((tm, n), lambda i: (i, 0))
    full = lambda a: pl.BlockSpec(a.shape, lambda i: (0,) * a.ndim)
    return pl.pallas_call(
        body, name="inproj_bwd", grid=(t // tm,),
        in_specs=[row(D_MODEL), full(nw), full(wt), row(D_MODEL), row(1536), row(512), row(1536), row(512), row(128)],
        out_specs=[row(D_MODEL), pl.BlockSpec((8, D_MODEL), lambda i: (0, 0)), pl.BlockSpec((8, 128), lambda i: (0, 0))],
        out_shape=[jax.ShapeDtypeStruct((t, D_MODEL), F32), jax.ShapeDtypeStruct((8, D_MODEL), F32),
                   jax.ShapeDtypeStruct((8, 128), F32)],
        compiler_params=_params(("arbitrary",)),
    )(x, nw, wt, dy, dpdn, dz, dpatt, dgate, dba)


def _adamw_sum(w, gs, m, v, name):
    r, c = w.shape
    nsum = gs.shape[0]
    tr = r if r <= 256 else 256
    c1 = 1.0 - ADAM_B1 ** ADAM_STEP
    c2 = 1.0 - ADAM_B2 ** ADAM_STEP

    def body(w_ref, g_ref, m_ref, v_ref, go_ref, d_ref, mo_ref, vo_ref):
        g = g_ref[0].astype(F32)
        for s in range(1, nsum):
            g = g + g_ref[s].astype(F32)
        mn = ADAM_B1 * m_ref[...] + (1.0 - ADAM_B1) * g
        vn = ADAM_B2 * v_ref[...] + (1.0 - ADAM_B2) * (g * g)
        go_ref[...] = g
        mo_ref[...] = mn
        vo_ref[...] = vn
        d_ref[...] = -ADAM_LR * ((mn / c1) / (jnp.sqrt(vn / c2) + ADAM_EPS) + ADAM_WD * w_ref[...])

    blk = pl.BlockSpec((tr, c), lambda i: (i, 0))
    return pl.pallas_call(
        body, name=name, grid=(r // tr,),
        in_specs=[blk, pl.BlockSpec((nsum, tr, c), lambda i: (0, i, 0)), blk, blk],
        out_specs=[blk] * 4, out_shape=[jax.ShapeDtypeStruct((r, c), F32)] * 4,
        compiler_params=_params(("arbitrary",)),
    )(w, gs, m, v)


def _local_step(x, target, norm_w, w_sect, conv_w, a_log, dt_bias, dn_norm_w, q_norm_w, k_norm_w, rel_bias, w_out):
    lane = np.arange(128)
    arow = jnp.zeros((1, 128), F32).at[0, DN_HEADS:2 * DN_HEADS].set(-jnp.exp(a_log[0]))
    dtb = jnp.zeros((1, 128), F32).at[0, DN_HEADS:2 * DN_HEADS].set(dt_bias[0])
    g_np, gt_np, gp_np = _group_mats()
    g, gt, gp = jnp.asarray(g_np), jnp.asarray(gt_np), jnp.asarray(gp_np)
    bk = jnp.asarray(_bucket_tables())
    wq = jnp.tile(q_norm_w, (1, ATT_HEADS)) * (HD ** -0.5)
    wk = jnp.tile(k_norm_w, (1, ATT_HEADS))
    del lane

    h, pdn, z, patt, gate, ba = _inproj(x, norm_w, w_sect)
    qkv_dn = _dn_prep(pdn, conv_w)
    oraw, ydn, sh, th = _dn_scan_fwd(qkv_dn, ba, z, arow, dtb, dn_norm_w)
    qkv_att = _att_prep(patt, g, gt, wq, wk)
    oatt, yatt, lse = _att_fwd(qkv_att, gate, rel_bias, bk)
    dy, mix, loss8 = _outproj_loss(x, ydn, yatt, w_out, target)

    do_dn, dz, do_att, dgate, dd, ddnw = _outproj_bwd(dy, w_out.T, oraw, z, dn_norm_w, oatt, gate, gp)
    d_wout = _matmul_tn(mix, dy, "dw_out")
    dq, dk, dv, drb = _att_bwd(qkv_att, do_att, lse, dd, rel_bias, bk)
    dpatt, dwq8, dwk8 = _att_prep_bwd(patt, dq, dk, dv, g, gt, wq, wk)
    dqkv_dn, dba = _dn_scan_bwd(qkv_dn, ba, do_dn, sh, th, arow, dtb)
    dpdn, d_conv = _dn_prep_bwd(pdn, conv_w, dqkv_dn)
    grad_x, dnw8, cs8 = _inproj_bwd(x, norm_w, w_sect.T, dy, dpdn, dz, dpatt, dgate, dba)
    dw_sections = [_matmul_tn(h, d_, nm) for d_, nm in
                   ((dpdn, "dw_in_dn"), (dz, "dw_in_z"), (dpatt, "dw_in_att"), (dgate, "dw_in_gate"), (dba, "dw_in_ba"))]

    grads = dict(
        loss=loss8[0, 0], grad_x=grad_x, norm_w=dnw8[0:1], w_in_sections=dw_sections, conv_w=d_conv,
        a_log=cs8[0:1, 2 * DN_HEADS:3 * DN_HEADS], dt_bias=cs8[0:1, DN_HEADS:2 * DN_HEADS],
        dn_norm_w=ddnw[0:1],
        q_norm_w=(dwq8[0].reshape(ATT_HEADS, HD).sum(0) * (HD ** -0.5))[None],
        k_norm_w=dwk8[0].reshape(ATT_HEADS, HD).sum(0)[None],
        rel_bias=drb[:, :N_BUCKETS], w_out=d_wout)
    return grads


SMALL = (("norm_w", 1024), ("a_log", 4), ("dt_bias", 4), ("dn_norm_w", 128), ("q_norm_w", 64), ("k_norm_w", 64),
         ("rel_bias", 256))
SMALL_ROWS = 16


def _pack_small(d):
    flat = jnp.concatenate([d[k].reshape(-1) for k, _ in SMALL])
    return jnp.pad(flat, (0, SMALL_ROWS * 128 - flat.shape[0])).reshape(SMALL_ROWS, 128)


def _unpack_small(p, shapes):
    flat = p.reshape(-1)
    out, off = {}, 0
    for k, n in SMALL:
        out[k] = flat[off:off + n].reshape(shapes[k])
        off += n
    return out


def kernel(x, norm_w, w_in, conv_w, a_log, dt_bias, dn_norm_w, q_norm_w, k_norm_w, rel_bias, w_out, loss_target, m_norm_w, m_w_in, m_conv_w, m_a_log, m_dt_bias, m_dn_norm_w, m_q_norm_w, m_k_norm_w, m_rel_bias, m_w_out, v_norm_w, v_w_in, v_conv_w, v_a_log, v_dt_bias, v_dn_norm_w, v_q_norm_w, v_k_norm_w, v_rel_bias, v_w_out):
    ncol = w_in.shape[2]
    d_in = N_DEV * ncol
    win8, wout8, conv8 = _gather_weights([w_in[0].astype(MXU), w_out[0].astype(MXU), conv_w[0]])
    w_full = win8.transpose(1, 0, 2).reshape(D_MODEL, d_in)
    w_sect = jnp.concatenate([w_full[:, :2048], w_full[:, 2056:], w_full[:, 2048:2056],
                              jnp.zeros((D_MODEL, W_COLS - d_in), MXU)], axis=1)
    wout_full = wout8.reshape(D_MODEL, D_MODEL)
    conv_full = conv8.transpose(1, 0, 2).reshape(4, 3 * D_DN)

    gr = _local_step(x[0], loss_target[0], norm_w, w_sect, conv_full, a_log, dt_bias, dn_norm_w, q_norm_w, k_norm_w,
                     rel_bias, wout_full)

    s_dn, s_z, s_att, s_gate, s_ba = gr["w_in_sections"]
    dw_full = jnp.concatenate([s_dn, s_z, s_ba[:, :2 * DN_HEADS], s_att, s_gate], axis=1)
    slabs = [dw_full.reshape(D_MODEL, 4, 2, ncol).transpose(2, 1, 0, 3),
             gr["w_out"].reshape(4, 2, D_MODEL // N_DEV, D_MODEL).transpose(1, 0, 2, 3),
             gr["conv_w"].reshape(4, 4, 2, 3 * D_DN // N_DEV).transpose(2, 1, 0, 3),
             jnp.broadcast_to(_pack_small(gr)[None, None], (2, 4, SMALL_ROWS, 128))]
    core = lax.axis_index("c").astype(jnp.int32).reshape(1)
    from_sibling = _swap_siblings(slabs)
    wires = (GRAD_WIRE, GRAD_WIRE, F32, F32)
    partial = [_chip_sum(slabs[i], from_sibling[i], core, wires[i], "chip_sum_%d" % i) for i in range(4)]
    r_win, r_wout, r_conv, r_small = _swap_chips(partial)

    g_win, d_win, m_win, v_win = _adamw_sum(w_in[0], r_win, m_w_in[0], v_w_in[0], "adamw_w_in")
    g_wout, d_wout, m_wout, v_wout = _adamw_sum(w_out[0], r_wout, m_w_out[0], v_w_out[0], "adamw_w_out")
    g_conv, d_conv, m_conv, v_conv = _adamw_sum(conv_w[0], r_conv, m_conv_w[0], v_conv_w[0], "adamw_conv_w")
    ws = dict(norm_w=norm_w, a_log=a_log, dt_bias=dt_bias, dn_norm_w=dn_norm_w, q_norm_w=q_norm_w,
              k_norm_w=k_norm_w, rel_bias=rel_bias)
    ms = dict(norm_w=m_norm_w, a_log=m_a_log, dt_bias=m_dt_bias, dn_norm_w=m_dn_norm_w, q_norm_w=m_q_norm_w,
              k_norm_w=m_k_norm_w, rel_bias=m_rel_bias)
    vs = dict(norm_w=v_norm_w, a_log=v_a_log, dt_bias=v_dt_bias, dn_norm_w=v_dn_norm_w, q_norm_w=v_q_norm_w,
              k_norm_w=v_k_norm_w, rel_bias=v_rel_bias)
    shapes = {k: a.shape for k, a in ws.items()}
    packs = _adamw_sum(_pack_small(ws), r_small, _pack_small(ms), _pack_small(vs), "adamw_small")
    g_s, d_s, m_s, v_s = (_unpack_small(p, shapes) for p in packs)

    loss = lax.psum(gr["loss"], AXES)
    names = ("norm_w", "w_in", "conv_w", "a_log", "dt_bias", "dn_norm_w", "q_norm_w", "k_norm_w", "rel_bias", "w_out")
    big = dict(w_in=(g_win, d_win, m_win, v_win), conv_w=(g_conv, d_conv, m_conv, v_conv),
               w_out=(g_wout, d_wout, m_wout, v_wout))
    outs = [loss, gr["grad_x"][None]]
    for i, sm in enumerate((g_s, d_s, m_s, v_s)):
        for nm in names:
            outs.append(big[nm][i][None] if nm in big else sm[nm])
    return tuple(outs)
```

```python
import functools
import math

import numpy as np
import jax
import jax.numpy as jnp
from jax import lax
from jax.experimental import pallas as pl
from jax.experimental.pallas import tpu as pltpu

F32 = jnp.float32
MXU = jnp.bfloat16
GRAD_WIRE = jnp.bfloat16
HI = lax.Precision.HIGHEST

D_MODEL = 1024
D_DN = 512
DN_HEADS = 4
DK = 128
CHUNK = 64
D_ATT = 512
ATT_HEADS = 8
HD = 64
PATTERNS = ((128, 1), (512, 4), (2048, 16))
BLK = 128
N_BUCKETS = 32
MAX_DISTANCE = 2048
EPS = 1e-6
W_COLS = 4224
N_DEV = 8
AXES = ("x", "y", "c")

ADAM_LR = 0.001
ADAM_B1 = 0.9
ADAM_B2 = 0.999
ADAM_EPS = 1e-08
ADAM_WD = 0.01
ADAM_STEP = 10

VMEM_LIMIT = 56 * 1024 * 1024
NEG = -1e30


def _dot(a, b):
    return jnp.dot(a.astype(MXU), b.astype(MXU), preferred_element_type=F32)


def _dot_nt(a, b):
    return lax.dot_general(a.astype(MXU), b.astype(MXU), (((1,), (1,)), ((), ())), preferred_element_type=F32)


def _dot_tn(a, b):
    return lax.dot_general(a.astype(MXU), b.astype(MXU), (((0,), (0,)), ((), ())), preferred_element_type=F32)


def _split(a):
    hi = a.astype(jnp.bfloat16)
    return hi, (a - hi.astype(F32)).astype(jnp.bfloat16)


def _dot_split(a, b, dims, exact):
    dg = lambda u, v: lax.dot_general(u, v, (dims, ((), ())), preferred_element_type=F32)
    if exact == "b":
        ah, al = _split(a)
        bh = b.astype(jnp.bfloat16)
        return dg(ah, bh) + dg(al, bh)
    if exact == "a":
        bh, bm = _split(b)
        bl = (b - bh.astype(F32) - bm.astype(F32)).astype(jnp.bfloat16)
        ah = a.astype(jnp.bfloat16)
        return dg(ah, bh) + (dg(ah, bm) + dg(ah, bl))
    ah, al = _split(a)
    bh, bl = _split(b)
    return dg(ah, bh) + (dg(ah, bl) + dg(al, bh))


def _hdot(a, b, exact=None):
    return _dot_split(a, b, ((1,), (0,)), exact)


def _hdot_nt(a, b, exact=None):
    return _dot_split(a, b, ((1,), (1,)), exact)


def _hdot_tn(a, b, exact=None):
    return _dot_split(a, b, ((0,), (0,)), exact)


def _sigmoid(x):
    return 1.0 / (1.0 + jnp.exp(-x))


def _silu(x):
    return x * _sigmoid(x)


def _dsilu(x):
    s = _sigmoid(x)
    return s * (1.0 + x * (1.0 - s))


def _softplus(x):
    return jnp.maximum(x, 0.0) + jnp.log(1.0 + jnp.exp(-jnp.abs(x)))


def _iota(shape, dim):
    return lax.broadcasted_iota(jnp.int32, shape, dim)


def _lane_col(x, k):
    return jnp.sum(jnp.where(_iota(x.shape, 1) == k, x, 0.0), axis=1, keepdims=True)


def _params(sem=None):
    return pltpu.CompilerParams(dimension_semantics=sem, vmem_limit_bytes=VMEM_LIMIT)


def _t5_bucket(dist):
    max_exact = N_BUCKETS // 2
    d = np.maximum(dist, 1).astype(np.float64)
    large = max_exact + (np.log(d / max_exact) / math.log(MAX_DISTANCE / max_exact)
                         * (N_BUCKETS - max_exact)).astype(np.int32)
    large = np.minimum(large, N_BUCKETS - 1)
    return np.where(dist < max_exact, dist, large).astype(np.int32)


def _bucket_tables():
    qi = np.arange(BLK)[:, None]
    kj = np.arange(2 * BLK)[None, :]
    step = qi - kj + BLK
    band = (step >= 0) & (step <= BLK)
    out = []
    for _, r in PATTERNS:
        b = _t5_bucket(np.clip(step, 0, None) * r)
        out.append(np.where(band, b, -1))
    return np.stack(out).astype(np.int32)


def _group_mats():
    g = np.zeros((D_ATT, 128), np.float32)
    for h in range(ATT_HEADS):
        g[h * HD:(h + 1) * HD, h] = 1.0
    return g, np.ascontiguousarray(g.T)


CHIP_FLIPS = ((1, 0), (0, 1), (1, 1))
ANY_SPEC = pl.BlockSpec(memory_space=pl.ANY)
MESH_ID = pl.DeviceIdType.MESH


def _other_chips():
    x, y = lax.axis_index("x"), lax.axis_index("y")
    return [((1 - x if fx else x), (1 - y if fy else y)) for fx, fy in CHIP_FLIPS]


def _gather_weights(arrs):
    n = len(arrs)

    def body(*refs):
        ins, outs = refs[:n], refs[n:2 * n]
        send, recv, loc = refs[2 * n:]
        x, y, c = (lax.axis_index(a) for a in AXES)
        sib = (x, y, 1 - c)
        chips = _other_chips()
        lin = lambda px, py, pc: 4 * px + 2 * py + pc

        def copy(a, k, block, to, src=None):
            slot = outs[a].at[lin(*block)]
            return pltpu.make_async_remote_copy(src_ref=slot if src is None else src, dst_ref=slot,
                                                send_sem=send.at[a, k], recv_sem=recv.at[a, k],
                                                device_id=to, device_id_type=MESH_ID)

        started = []
        for a in range(n):
            mine = pltpu.make_async_copy(ins[a], outs[a].at[lin(x, y, c)], loc.at[a])
            mine.start()
            started.append(mine)
        firsts = []
        for a in range(n):
            firsts.append(copy(a, 0, (x, y, c), sib, src=ins[a]))
            firsts += [copy(a, 1 + j, (x, y, c), (*chip, c), src=ins[a]) for j, chip in enumerate(chips)]
        for cp in firsts:
            cp.start()
        passed = []
        for j, chip in enumerate(chips):
            for a in range(n):
                copy(a, 1 + j, (*chip, c), (x, y, c)).wait_recv()
                fw = copy(a, 4 + j, (*chip, c), sib)
                fw.start()
                passed.append(fw)
        for a in range(n):
            copy(a, 0, sib, (x, y, c)).wait_recv()
            for j, chip in enumerate(chips):
                copy(a, 4 + j, (*chip, 1 - c), (x, y, c)).wait_recv()
        for cp in firsts + passed:
            cp.wait_send()
        for mine in started:
            mine.wait()

    return pl.pallas_call(
        body, name="gather_weights", out_shape=[jax.ShapeDtypeStruct((N_DEV,) + a.shape, a.dtype) for a in arrs],
        in_specs=[ANY_SPEC] * n, out_specs=[ANY_SPEC] * n,
        scratch_shapes=[pltpu.SemaphoreType.DMA((n, 7)), pltpu.SemaphoreType.DMA((n, 7)), pltpu.SemaphoreType.DMA((n,))],
    )(*arrs)


def _swap_siblings(arrs):
    n = len(arrs)

    def body(*refs):
        ins, outs = refs[:n], refs[n:2 * n]
        send, recv = refs[2 * n:]
        x, y, c = (lax.axis_index(a) for a in AXES)
        cps = [pltpu.make_async_remote_copy(src_ref=ins[a].at[1 - c], dst_ref=outs[a], send_sem=send.at[a],
                                            recv_sem=recv.at[a], device_id=(x, y, 1 - c), device_id_type=MESH_ID)
               for a in range(n)]
        for cp in cps:
            cp.start()
        for cp in cps:
            cp.wait()

    return pl.pallas_call(
        body, name="swap_siblings", out_shape=[jax.ShapeDtypeStruct(a.shape[1:], a.dtype) for a in arrs],
        in_specs=[ANY_SPEC] * n, out_specs=[ANY_SPEC] * n,
        scratch_shapes=[pltpu.SemaphoreType.DMA((n,)), pltpu.SemaphoreType.DMA((n,))],
    )(*arrs)


def _chip_sum(mine2, theirs, core, wire, name):
    _, nchip, r, cdim = mine2.shape
    tr = r if r <= 256 else 256

    def body(core_ref, a_ref, b_ref, o_ref):
        del core_ref
        o_ref[...] = (a_ref[...] + b_ref[...]).astype(wire)

    grid_spec = pltpu.PrefetchScalarGridSpec(
        num_scalar_prefetch=1, grid=(nchip, r // tr),
        in_specs=[pl.BlockSpec((None, None, tr, cdim), lambda j, i, cr: (cr[0], j, i, 0)),
                  pl.BlockSpec((None, tr, cdim), lambda j, i, cr: (j, i, 0))],
        out_specs=pl.BlockSpec((None, tr, cdim), lambda j, i, cr: (j, i, 0)))
    return pl.pallas_call(
        body, name=name, grid_spec=grid_spec, out_shape=jax.ShapeDtypeStruct((nchip, r, cdim), wire),
        compiler_params=_params(("arbitrary", "arbitrary")),
    )(core, mine2, theirs)


def _swap_chips(arrs):
    n = len(arrs)

    def body(*refs):
        ins, outs = refs[:n], refs[n:2 * n]
        send, recv, loc = refs[2 * n:]
        x, y, c = (lax.axis_index(a) for a in AXES)
        me = 2 * x + y
        chips = _other_chips()
        locs, sends, recvs = [], [], []
        for a in range(n):
            lc = pltpu.make_async_copy(ins[a].at[me], outs[a].at[me], loc.at[a])
            lc.start()
            locs.append(lc)
            for j, (px, py) in enumerate(chips):
                them = 2 * px + py
                cp = pltpu.make_async_remote_copy(src_ref=ins[a].at[them], dst_ref=outs[a].at[me], send_sem=send.at[a, j],
                                                  recv_sem=recv.at[a, j], device_id=(px, py, c), device_id_type=MESH_ID)
                cp.start()
                sends.append(cp)
                recvs.append(pltpu.make_async_remote_copy(src_ref=ins[a].at[them], dst_ref=outs[a].at[them],
                                                          send_sem=send.at[a, j], recv_sem=recv.at[a, j],
                                                          device_id=(px, py, c), device_id_type=MESH_ID))
        for cp in recvs:
            cp.wait_recv()
        for cp in sends:
            cp.wait_send()
        for lc in locs:
            lc.wait()

    return pl.pallas_call(
        body, name="swap_chips", out_shape=[jax.ShapeDtypeStruct(a.shape, a.dtype) for a in arrs],
        in_specs=[ANY_SPEC] * n, out_specs=[ANY_SPEC] * n,
        scratch_shapes=[pltpu.SemaphoreType.DMA((n, 3)), pltpu.SemaphoreType.DMA((n, 3)), pltpu.SemaphoreType.DMA((n,))],
    )(*arrs)


def _inproj(x, nw, w):
    t = x.shape[0]
    tm = 256

    def body(x_ref, nw_ref, w_ref, h_ref, pdn_ref, z_ref, patt_ref, gate_ref, ba_ref):
        xv = x_ref[...]
        rstd = lax.rsqrt(jnp.mean(xv * xv, axis=-1, keepdims=True) + EPS)
        h = (xv * rstd * nw_ref[...]).astype(MXU)
        h_ref[...] = h
        for ref, lo, hi in ((pdn_ref, 0, 1536), (z_ref, 1536, 2048), (patt_ref, 2048, 3584),
                            (gate_ref, 3584, 4096), (ba_ref, 4096, 4224)):
            ref[...] = jnp.dot(h, w_ref[:, lo:hi], preferred_element_type=F32)

    row = lambda n: pl.BlockSpec((tm, n), lambda i: (i, 0))
    full = lambda a: pl.BlockSpec(a.shape, lambda i: (0,) * a.ndim)
    return pl.pallas_call(
        body, name="inproj", grid=(t // tm,),
        in_specs=[row(D_MODEL), full(nw), full(w)],
        out_specs=[row(D_MODEL), row(1536), row(512), row(1536), row(512), row(128)],
        out_shape=[jax.ShapeDtypeStruct((t, D_MODEL), MXU)] +
                  [jax.ShapeDtypeStruct((t, n), F32) for n in (1536, 512, 1536, 512, 128)],
        compiler_params=_params(("arbitrary",)),
    )(x, nw, w)


CONV_ROWS = 512


def _conv_taps(u_ref, c, w_ref):
    r0 = c * CONV_ROWS
    if c == 0:
        ext = jnp.concatenate([jnp.zeros((8, 128), F32), u_ref[0:CONV_ROWS, :]], axis=0)
    else:
        ext = u_ref[r0 - 8:r0 + CONV_ROWS, :]
    taps = [ext[8:, :]] + [pltpu.roll(ext, k, 0)[8:, :] for k in (1, 2, 3)]
    y = taps[0] * w_ref[3:4, :]
    for k in (1, 2, 3):
        y = y + taps[k] * w_ref[3 - k:4 - k, :]
    return taps, y


def _dn_prep(pdn, cw):
    t = pdn.shape[0]

    def body(u_ref, w_ref, o_ref):
        j = pl.program_id(0)
        for c in range(t // CONV_ROWS):
            _, y = _conv_taps(u_ref, c, w_ref)
            a = _silu(y)
            ssq = jnp.sum(a * a, axis=1, keepdims=True)
            f = jnp.where(j < 8, lax.rsqrt(ssq + EPS), 1.0) * jnp.where(j < 4, DK ** -0.5, 1.0)
            o_ref[c * CONV_ROWS:(c + 1) * CONV_ROWS, :] = a * f

    return pl.pallas_call(
        body, name="dn_prep", grid=(12,),
        in_specs=[pl.BlockSpec((t, 128), lambda j: (0, j)), pl.BlockSpec((4, 128), lambda j: (0, j))],
        out_specs=pl.BlockSpec((t, 128), lambda j: (0, j)),
        out_shape=jax.ShapeDtypeStruct((t, 1536), F32),
        compiler_params=_params(("arbitrary",)),
    )(pdn, cw)


def _chunk_common(qkv, ba, arow, dtb):
    c = CHUNK
    ri, ci = _iota((c, c), 0), _iota((c, c), 1)
    lane = _iota((c, 128), 1)
    g_all = jnp.where((lane >= DN_HEADS) & (lane < 2 * DN_HEADS), arow * _softplus(ba + dtb), 0.0)
    gc_all = _hdot((ri >= ci).astype(F32), g_all, "a")
    gc_t = gc_all.T
    beta_all = _sigmoid(ba)
    out = []
    for h in range(DN_HEADS):
        gc = _lane_col(gc_all, DN_HEADS + h)
        gcr = gc_t[DN_HEADS + h:DN_HEADS + h + 1, :]
        gl = gc[c - 1:c, :]
        out.append(dict(
            q=qkv[:, h * DK:(h + 1) * DK], k=qkv[:, D_DN + h * DK:D_DN + (h + 1) * DK],
            v=qkv[:, 2 * D_DN + h * DK:2 * D_DN + (h + 1) * DK],
            beta=_lane_col(beta_all, h), g=_lane_col(g_all, DN_HEADS + h),
            a_raw=_lane_col(ba, DN_HEADS + h), a_h=_lane_col(arow, DN_HEADS + h), dt_h=_lane_col(dtb, DN_HEADS + h),
            decay=jnp.exp(jnp.where(ri >= ci, gc - gcr, NEG)), eg=jnp.exp(gc), egl=jnp.exp(gl), etail=jnp.exp(gl - gc)))
    return out, ri, ci


def _dn_scan_fwd(qkv, ba, z, arow, dtb, dnw):
    t = qkv.shape[0]
    n = t // CHUNK
    c = CHUNK
    hs = range(DN_HEADS)

    def body(qkv_ref, ba_ref, z_ref, arow_ref, dtb_ref, dnw_ref, o_ref, y_ref, sh_ref, th_ref, s_ref):
        @pl.when(pl.program_id(0) == 0)
        def _():
            s_ref[...] = jnp.zeros_like(s_ref)

        ms, ri, ci = _chunk_common(qkv_ref[...], ba_ref[...], arow_ref[...], dtb_ref[...])
        kb = [m["k"] * m["beta"] for m in ms]
        amat = [jnp.where(ri > ci, _dot_nt(kb[h], ms[h]["k"]) * ms[h]["decay"], 0.0) for h in hs]
        attn = [jnp.where(ri >= ci, _dot_nt(m["q"], m["k"]) * m["decay"], 0.0) for m in ms]
        eye = (ri == ci).astype(F32)
        tinv = [eye - a for a in amat]
        pw = amat
        for _ in range(5):
            pw = [_hdot(p, p) for p in pw]
            tinv = [tv + _hdot(tv, p) for tv, p in zip(tinv, pw)]
        u = [_hdot(tinv[h], ms[h]["v"] * ms[h]["beta"]) for h in hs]
        w = [_hdot(tinv[h], kb[h] * ms[h]["eg"]) for h in hs]
        s = [s_ref[h] for h in hs]
        v_new = [u[h] - _dot(w[h], s[h]) for h in hs]
        o = [_dot(ms[h]["q"] * ms[h]["eg"], s[h]) + _dot(attn[h], v_new[h]) for h in hs]
        for h in hs:
            s_ref[h] = s[h] * ms[h]["egl"] + _dot_tn(ms[h]["k"] * ms[h]["etail"], v_new[h])
            sh_ref[0, h] = s[h]
            th_ref[0, h] = tinv[h]
            o_ref[:, h * DK:(h + 1) * DK] = o[h]
            rs = lax.rsqrt(jnp.mean(o[h] * o[h], axis=1, keepdims=True) + EPS)
            y_ref[:, h * DK:(h + 1) * DK] = o[h] * rs * dnw_ref[...] * _silu(z_ref[:, h * DK:(h + 1) * DK])

    row = lambda w_: pl.BlockSpec((c, w_), lambda i: (i, 0))
    one = pl.BlockSpec((1, 128), lambda i: (0, 0))
    return pl.pallas_call(
        body, name="dn_scan_fwd", grid=(n,),
        in_specs=[row(1536), row(128), row(512), one, one, one],
        out_specs=[row(512), row(512), pl.BlockSpec((1, DN_HEADS, DK, DK), lambda i: (i, 0, 0, 0)),
                   pl.BlockSpec((1, DN_HEADS, c, c), lambda i: (i, 0, 0, 0))],
        out_shape=[jax.ShapeDtypeStruct((t, 512), F32), jax.ShapeDtypeStruct((t, 512), F32),
                   jax.ShapeDtypeStruct((n, DN_HEADS, DK, DK), F32), jax.ShapeDtypeStruct((n, DN_HEADS, c, c), F32)],
        scratch_shapes=[pltpu.VMEM((DN_HEADS, DK, DK), F32)],
        compiler_params=_params(("arbitrary",)),
    )(qkv, ba, z, arow, dtb, dnw)


def _att_prep(patt, g, gt, wq, wk):
    t = patt.shape[0]
    tm = 512

    def body(p_ref, g_ref, gt_ref, wq_ref, wk_ref, o_ref):
        for lo, w_ref in ((0, wq_ref), (512, wk_ref)):
            xv = p_ref[:, lo:lo + 512]
            rstd = lax.rsqrt(_hdot(xv * xv, g_ref[...], "b") * (1.0 / HD) + EPS)
            o_ref[:, lo:lo + 512] = xv * _hdot(rstd, gt_ref[...], "b") * w_ref[...]
        o_ref[:, 1024:1536] = p_ref[:, 1024:1536]

    full = lambda a: pl.BlockSpec(a.shape, lambda i: (0,) * a.ndim)
    return pl.pallas_call(
        body, name="att_prep", grid=(t // tm,),
        in_specs=[pl.BlockSpec((tm, 1536), lambda i: (i, 0)), full(g), full(gt), full(wq), full(wk)],
        out_specs=pl.BlockSpec((tm, 1536), lambda i: (i, 0)),
        out_shape=jax.ShapeDtypeStruct((t, 1536), F32),
        compiler_params=_params(("arbitrary",)),
    )(patt, g, gt, wq, wk)


def _bias_tables(rb_ref, bk_ref, bias_ref, pair):
    for p in range(len(PATTERNS)):
        bk = bk_ref[p]
        for hh in range(2):
            head = 2 * pair + hh
            bm = jnp.full((BLK, 2 * BLK), NEG, F32)
            for b in range(N_BUCKETS):
                bm = jnp.where(bk == b, rb_ref[head, b], bm)
            bias_ref[p, hh * BLK:(hh + 1) * BLK, :] = bm


def _stack_heads(xb, h0):
    return jnp.concatenate([jnp.where(h0, xb, 0.0), jnp.where(h0, 0.0, xb)], axis=0).astype(MXU)


def _block_rows(t, r, n):
    per_class = (t // r) // BLK
    res = n // per_class
    j = n % per_class
    start = res + BLK * r * j
    pstart = res + BLK * r * jnp.maximum(j - 1, 0)
    if r == 1:
        return pl.ds(pl.multiple_of(start, BLK), BLK), pl.ds(pl.multiple_of(pstart, BLK), BLK), j
    return pl.ds(start, BLK, stride=r), pl.ds(pstart, BLK, stride=r), j


def _att_fwd(qkv, gate, rb, bk):
    t = qkv.shape[0]
    rows = 512

    def body(rb_ref, bk_ref, q_ref, k_ref, v_ref, g_ref, o_ref, y_ref, lse_ref,
             o0_ref, o1_ref, o2_ref, l0_ref, l1_ref, l2_ref, bias_ref):
        pair = pl.program_id(0)
        _bias_tables(rb_ref, bk_ref, bias_ref, pair)
        h0 = _iota((BLK, 128), 1) < HD
        prev_cols = _iota((2 * BLK, 2 * BLK), 1) < BLK
        op_refs, lp_refs = (o0_ref, o1_ref, o2_ref), (l0_ref, l1_ref, l2_ref)

        for p, (_, r) in enumerate(PATTERNS):
            def blk(n, carry, p=p, r=r):
                cur, prev, j = _block_rows(t, r, n)
                q2 = _stack_heads(q_ref[cur, :], h0)
                k2 = jnp.concatenate([k_ref[prev, :], k_ref[cur, :]], axis=0).astype(MXU)
                v2 = jnp.concatenate([v_ref[prev, :], v_ref[cur, :]], axis=0).astype(MXU)
                s = _dot_nt(q2, k2) + bias_ref[p] + jnp.where(prev_cols & (j == 0), NEG, 0.0)
                m = jnp.max(s, axis=1, keepdims=True)
                e = jnp.exp(s - m)
                l = jnp.sum(e, axis=1, keepdims=True)
                pv = _dot(e, v2) / l
                lse = m + jnp.log(l)
                op_refs[p][cur, :] = jnp.where(h0, pv[:BLK], pv[BLK:])
                lp_refs[p][cur, :] = jnp.where(h0, lse[:BLK], lse[BLK:])
                return carry

            lax.fori_loop(0, t // BLK, blk, 0, unroll=8)

        for c in range(t // rows):
            sl = slice(c * rows, (c + 1) * rows)
            ls = [ref[sl, :] for ref in lp_refs]
            mx = jnp.maximum(jnp.maximum(ls[0], ls[1]), ls[2])
            ws = [jnp.exp(v_ - mx) for v_ in ls]
            den = ws[0] + ws[1] + ws[2]
            o = (ws[0] * o0_ref[sl, :] + ws[1] * o1_ref[sl, :] + ws[2] * o2_ref[sl, :]) / den
            o_ref[sl, :] = o
            y_ref[sl, :] = o * _silu(g_ref[sl, :])
            lse_ref[sl, :] = mx + jnp.log(den)

    col = lambda off: pl.BlockSpec((t, 128), lambda i, off=off: (0, off + i))
    return pl.pallas_call(
        body, name="att_fwd", grid=(ATT_HEADS // 2,),
        in_specs=[pl.BlockSpec(memory_space=pltpu.SMEM), pl.BlockSpec(bk.shape, lambda i: (0, 0, 0)),
                  col(0), col(4), col(8), col(0)],
        out_specs=[col(0), col(0), col(0)],
        out_shape=[jax.ShapeDtypeStruct((t, 512), F32)] * 3,
        scratch_shapes=[pltpu.VMEM((t, 128), F32)] * 6 + [pltpu.VMEM((len(PATTERNS), 2 * BLK, 2 * BLK), F32)],
        compiler_params=_params(("arbitrary",)),
    )(rb, bk, qkv, qkv, qkv, gate)


def _outproj_loss(x, ydn, yatt, wout, target):
    t = x.shape[0]
    tm = 512

    def body(x_ref, a_ref, b_ref, w_ref, t_ref, dy_ref, mix_ref, loss_ref):
        @pl.when(pl.program_id(0) == 0)
        def _():
            loss_ref[...] = jnp.zeros_like(loss_ref)

        mix = jnp.concatenate([a_ref[...], b_ref[...]], axis=1).astype(MXU)
        mix_ref[...] = mix
        err = x_ref[...] + jnp.dot(mix, w_ref[...], preferred_element_type=F32) - t_ref[...]
        dy_ref[...] = err * (1.0 / D_MODEL)
        loss_ref[...] += jnp.sum(err * err) * (0.5 / D_MODEL)

    row = lambda n: pl.BlockSpec((tm, n), lambda i: (i, 0))
    return pl.pallas_call(
        body, name="outproj_loss", grid=(t // tm,),
        in_specs=[row(D_MODEL), row(512), row(512), pl.BlockSpec(wout.shape, lambda i: (0, 0)), row(D_MODEL)],
        out_specs=[row(D_MODEL), row(D_MODEL), pl.BlockSpec((8, 128), lambda i: (0, 0))],
        out_shape=[jax.ShapeDtypeStruct((t, D_MODEL), F32), jax.ShapeDtypeStruct((t, D_MODEL), MXU),
                   jax.ShapeDtypeStruct((8, 128), F32)],
        compiler_params=_params(("arbitrary",)),
    )(x, ydn, yatt, wout, target)


def _outproj_bwd(dy, wout_t, oraw, z, dnw, oatt, gate, g, gt):
    t = dy.shape[0]
    tm = 256

    def body(dy_ref, w_ref, o_ref, z_ref, dnw_ref, oa_ref, g_ref, grp_ref, grpt_ref,
             do_ref, dz_ref, doa_ref, dg_ref, dd_ref, ddnw_ref):
        @pl.when(pl.program_id(0) == 0)
        def _():
            ddnw_ref[...] = jnp.zeros_like(ddnw_ref)

        dmix = jnp.dot(dy_ref[...].astype(MXU), w_ref[...], preferred_element_type=F32)
        dnw_v = dnw_ref[...]
        acc = jnp.zeros((1, DK), F32)
        for h in range(DN_HEADS):
            sl = slice(h * DK, (h + 1) * DK)
            o, zz, dm = o_ref[:, sl], z_ref[:, sl], dmix[:, sl]
            rs = lax.rsqrt(jnp.mean(o * o, axis=1, keepdims=True) + EPS)
            oh = o * rs
            dz_ref[:, sl] = dm * oh * dnw_v * _dsilu(zz)
            d_on = dm * _silu(zz)
            gg = d_on * dnw_v
            do_ref[:, sl] = rs * (gg - oh * jnp.mean(gg * oh, axis=1, keepdims=True))
            acc = acc + jnp.sum(d_on * oh, axis=0, keepdims=True)
        ddnw_ref[...] += jnp.broadcast_to(acc, (8, DK))
        da, gate_v, oa = dmix[:, 512:], g_ref[...], oa_ref[...]
        doa = da * _silu(gate_v)
        doa_ref[...] = doa
        dg_ref[...] = da * oa * _dsilu(gate_v)
        dd_ref[...] = _hdot(_hdot(doa * oa, grp_ref[...], "b"), grpt_ref[...], "b")

    row = lambda n: pl.BlockSpec((tm, n), lambda i: (i, 0))
    full = lambda a: pl.BlockSpec(a.shape, lambda i: (0,) * a.ndim)
    return pl.pallas_call(
        body, name="outproj_bwd", grid=(t // tm,),
        in_specs=[row(D_MODEL), full(wout_t), row(512), row(512), full(dnw), row(512), row(512), full(g), full(gt)],
        out_specs=[row(512)] * 5 + [pl.BlockSpec((8, DK), lambda i: (0, 0))],
        out_shape=[jax.ShapeDtypeStruct((t, 512), F32)] * 5 + [jax.ShapeDtypeStruct((8, DK), F32)],
        compiler_params=_params(("arbitrary",)),
    )(dy, wout_t, oraw, z, dnw, oatt, gate, g, gt)


def _matmul_tn(a, b, name):
    t, m = a.shape
    n = b.shape[1]
    tk = 512
    tn = n if n <= 512 else 512

    def body(a_ref, b_ref, o_ref):
        @pl.when(pl.program_id(1) == 0)
        def _():
            o_ref[...] = jnp.zeros_like(o_ref)

        o_ref[...] += _dot_tn(a_ref[...], b_ref[...])

    return pl.pallas_call(
        body, name=name, grid=(n // tn, t // tk),
        in_specs=[pl.BlockSpec((tk, m), lambda j, k: (k, 0)), pl.BlockSpec((tk, tn), lambda j, k: (k, j))],
        out_specs=pl.BlockSpec((m, tn), lambda j, k: (0, j)),
        out_shape=jax.ShapeDtypeStruct((m, n), F32),
        compiler_params=_params(("arbitrary", "arbitrary")),
    )(a, b)


def _att_bwd(qkv, do, lse, dd, rb, bk):
    t = qkv.shape[0]
    rows = 512

    def body(rb_ref, bk_ref, q_ref, k_ref, v_ref, do_ref, lse_ref, dd_ref,
             dq_ref, dk_ref, dv_ref, db_ref, bias_ref, ds_ref):
        pair = pl.program_id(0)

        @pl.when(pair == 0)
        def _():
            db_ref[...] = jnp.zeros_like(db_ref)

        _bias_tables(rb_ref, bk_ref, bias_ref, pair)
        ds_ref[...] = jnp.zeros_like(ds_ref)
        for c in range(t // rows):
            sl = slice(c * rows, (c + 1) * rows)
            for ref in (dq_ref, dk_ref, dv_ref):
                ref[sl, :] = jnp.zeros((rows, 128), F32)
        h0 = _iota((BLK, 128), 1) < HD
        prev_cols = _iota((2 * BLK, 2 * BLK), 1) < BLK

        def rows_of(xb):
            return jnp.concatenate([xb[:, 0:1], xb[:, HD:HD + 1]], axis=0)

        for p, (_, r) in enumerate(PATTERNS):
            def blk(n, carry, p=p, r=r):
                cur, prev, j = _block_rows(t, r, n)
                q2, do2 = _stack_heads(q_ref[cur, :], h0), _stack_heads(do_ref[cur, :], h0)
                k2 = jnp.concatenate([k_ref[prev, :], k_ref[cur, :]], axis=0).astype(MXU)
                v2 = jnp.concatenate([v_ref[prev, :], v_ref[cur, :]], axis=0).astype(MXU)
                s = _dot_nt(q2, k2) + bias_ref[p] + jnp.where(prev_cols & (j == 0), NEG, 0.0)
                prob = jnp.exp(s - rows_of(lse_ref[cur, :]))
                ds = prob * (_dot_nt(do2, v2) - rows_of(dd_ref[cur, :]))
                ds_ref[p] += ds
                dq2 = _dot(ds, k2)
                dk2 = _dot_tn(ds, q2)
                dv2 = _dot_tn(prob, do2)
                dq_ref[cur, :] += jnp.where(h0, dq2[:BLK], dq2[BLK:])
                dk_ref[prev, :] += dk2[:BLK]
                dv_ref[prev, :] += dv2[:BLK]
                dk_ref[cur, :] += dk2[BLK:]
                dv_ref[cur, :] += dv2[BLK:]
                return carry

            lax.fori_loop(0, t // BLK, blk, 0, unroll=4)

        ri, ci = _iota((8, 128), 0), _iota((8, 128), 1)
        upd = jnp.zeros((8, 128), F32)
        for p in range(len(PATTERNS)):
            bk = bk_ref[p]
            for hh in range(2):
                dsum = ds_ref[p, hh * BLK:(hh + 1) * BLK, :]
                for b in range(N_BUCKETS):
                    val = jnp.sum(jnp.where(bk == b, dsum, 0.0))
                    upd = upd + jnp.where((ri == 2 * pair + hh) & (ci == b), val, 0.0)
        db_ref[...] += upd

    col = lambda off: pl.BlockSpec((t, 128), lambda i, off=off: (0, off + i))
    return pl.pallas_call(
        body, name="att_bwd", grid=(ATT_HEADS // 2,),
        in_specs=[pl.BlockSpec(memory_space=pltpu.SMEM), pl.BlockSpec(bk.shape, lambda i: (0, 0, 0)),
                  col(0), col(4), col(8), col(0), col(0), col(0)],
        out_specs=[col(0), col(0), col(0), pl.BlockSpec((8, 128), lambda i: (0, 0))],
        out_shape=[jax.ShapeDtypeStruct((t, 512), F32)] * 3 + [jax.ShapeDtypeStruct((8, 128), F32)],
        scratch_shapes=[pltpu.VMEM((len(PATTERNS), 2 * BLK, 2 * BLK), F32),
                        pltpu.VMEM((len(PATTERNS), 2 * BLK, 2 * BLK), F32)],
        compiler_params=_params(("arbitrary",)),
    )(rb, bk, qkv, qkv, qkv, do, lse, dd)


def _att_prep_bwd(patt, dq, dk, dv, g, gt, wq, wk):
    t = patt.shape[0]
    tm = 512

    def body(p_ref, dq_ref, dk_ref, dv_ref, g_ref, gt_ref, wq_ref, wk_ref, o_ref, dwq_ref, dwk_ref):
        @pl.when(pl.program_id(0) == 0)
        def _():
            dwq_ref[...] = jnp.zeros_like(dwq_ref)
            dwk_ref[...] = jnp.zeros_like(dwk_ref)

        for lo, w_ref, d_ref, dw_ref in ((0, wq_ref, dq_ref, dwq_ref), (512, wk_ref, dk_ref, dwk_ref)):
            xv, dyv = p_ref[:, lo:lo + 512], d_ref[...]
            rstd = lax.rsqrt(_hdot(xv * xv, g_ref[...], "b") * (1.0 / HD) + EPS)
            rsb = _hdot(rstd, gt_ref[...], "b")
            xh = xv * rsb
            gg = dyv * w_ref[...]
            mean = _hdot(_hdot(gg * xh, g_ref[...], "b") * (1.0 / HD), gt_ref[...], "b")
            o_ref[:, lo:lo + 512] = rsb * (gg - xh * mean)
            dw_ref[...] += jnp.broadcast_to(jnp.sum(dyv * xh, axis=0, keepdims=True), (8, 512))
        o_ref[:, 1024:1536] = dv_ref[...]

    row = lambda n: pl.BlockSpec((tm, n), lambda i: (i, 0))
    full = lambda a: pl.BlockSpec(a.shape, lambda i: (0,) * a.ndim)
    acc = pl.BlockSpec((8, 512), lambda i: (0, 0))
    return pl.pallas_call(
        body, name="att_prep_bwd", grid=(t // tm,),
        in_specs=[row(1536), row(512), row(512), row(512), full(g), full(gt), full(wq), full(wk)],
        out_specs=[row(1536), acc, acc],
        out_shape=[jax.ShapeDtypeStruct((t, 1536), F32), jax.ShapeDtypeStruct((8, 512), F32),
                   jax.ShapeDtypeStruct((8, 512), F32)],
        compiler_params=_params(("arbitrary",)),
    )(patt, dq, dk, dv, g, gt, wq, wk)


def _dn_scan_bwd(qkv, ba, do, sh, th, arow, dtb):
    t = qkv.shape[0]
    n = t // CHUNK
    c = CHUNK

    def body(qkv_ref, ba_ref, do_ref, sh_ref, th_ref, arow_ref, dtb_ref, dqkv_ref, dba_ref, ds_ref):
        @pl.when(pl.program_id(0) == 0)
        def _():
            ds_ref[...] = jnp.zeros_like(ds_ref)

        hs = range(DN_HEADS)
        ms, ri, ci = _chunk_common(qkv_ref[...], ba_ref[...], arow_ref[...], dtb_ref[...])
        lane = _iota((c, 128), 1)
        row = _iota((c, 1), 0)
        q, k, v = [m["q"] for m in ms], [m["k"] for m in ms], [m["v"] for m in ms]
        beta, decay = [m["beta"] for m in ms], [m["decay"] for m in ms]
        eg, egl, etail = [m["eg"] for m in ms], [m["egl"] for m in ms], [m["etail"] for m in ms]
        s, tinv = [sh_ref[0, h] for h in hs], [th_ref[0, h] for h in hs]
        d_o, d_s = [do_ref[:, h * DK:(h + 1) * DK] for h in hs], [ds_ref[h] for h in hs]
        kb = [k[h] * beta[h] for h in hs]
        vb = [v[h] * beta[h] for h in hs]
        kbg = [kb[h] * eg[h] for h in hs]
        amat = [jnp.where(ri > ci, _dot_nt(kb[h], k[h]) * decay[h], 0.0) for h in hs]
        attn = [jnp.where(ri >= ci, _dot_nt(q[h], k[h]) * decay[h], 0.0) for h in hs]
        u = [_hdot(tinv[h], vb[h]) for h in hs]
        w = [_hdot(tinv[h], kbg[h]) for h in hs]
        v_new = [u[h] - _dot(w[h], s[h]) for h in hs]
        q_dec = [q[h] * eg[h] for h in hs]
        k_tail = [k[h] * etail[h] for h in hs]

        d_vnew = [_dot_tn(attn[h], d_o[h]) + _dot(k_tail[h], d_s[h]) for h in hs]
        d_attn = [jnp.where(ri >= ci, _dot_nt(d_o[h], v_new[h]), 0.0) for h in hs]
        d_qdec = [_dot_nt(d_o[h], s[h]) for h in hs]
        for h in hs:
            ds_ref[h] = _dot_tn(q_dec[h], d_o[h]) + d_s[h] * egl[h] - _dot_tn(w[h], d_vnew[h])
        d_ktail = [_dot_nt(v_new[h], d_s[h]) for h in hs]
        d_gl = [jnp.sum(s[h] * d_s[h]) * egl[h] for h in hs]
        d_w = [-_dot_nt(d_vnew[h], s[h]) for h in hs]
        d_vb = [_hdot_tn(tinv[h], d_vnew[h]) for h in hs]
        d_kbg = [_hdot_tn(tinv[h], d_w[h]) for h in hs]
        d_a = [-jnp.where(ri > ci, _hdot_nt(d_vb[h], u[h]) + _hdot_nt(d_kbg[h], w[h]), 0.0) for h in hs]
        d_qk = [d_attn[h] * decay[h] for h in hs]
        d_kk = [d_a[h] * decay[h] for h in hs]
        d_kb = [_dot(d_kk[h], k[h]) + d_kbg[h] * eg[h] for h in hs]
        d_q = [_dot(d_qk[h], k[h]) + d_qdec[h] * eg[h] for h in hs]
        d_k = [_dot_tn(d_qk[h], q[h]) + _dot_tn(d_kk[h], kb[h]) + d_ktail[h] * etail[h] + d_kb[h] * beta[h] for h in hs]
        d_beta = [jnp.sum(d_kb[h] * k[h] + d_vb[h] * v[h], axis=1, keepdims=True) for h in hs]
        mm = [d_a[h] * amat[h] + d_attn[h] * attn[h] for h in hs]
        rows = jnp.zeros((c, c), F32)
        for h in hs:
            rows = rows + jnp.where(ri == h, jnp.sum(mm[h], axis=0, keepdims=True), 0.0)
        cols_t = jnp.concatenate([rows, jnp.zeros((c, c), F32)], axis=1).T[:c, :]
        d_gc_all = jnp.zeros((c, 128), F32)
        for h in hs:
            tail_term = jnp.sum(d_ktail[h] * k_tail[h], axis=1, keepdims=True)
            d_gc = (jnp.sum(mm[h], axis=1, keepdims=True) - _lane_col(cols_t, h)
                    + jnp.sum(d_qdec[h] * q_dec[h] + d_kbg[h] * kbg[h], axis=1, keepdims=True) - tail_term)
            d_gc = d_gc + jnp.where(row == c - 1, jnp.sum(tail_term) + d_gl[h], 0.0)
            d_gc_all = d_gc_all + jnp.where(lane == DN_HEADS + h, d_gc, 0.0)
        d_g_all = _hdot((ri <= ci).astype(F32), d_gc_all, "a")
        dba = jnp.zeros((c, 128), F32)
        for h in hs:
            d_g = _lane_col(d_g_all, DN_HEADS + h)
            d_braw = d_beta[h] * beta[h] * (1.0 - beta[h])
            d_araw = d_g * ms[h]["a_h"] * _sigmoid(ms[h]["a_raw"] + ms[h]["dt_h"])
            dba = dba + jnp.where(lane == h, d_braw, 0.0) + jnp.where(lane == DN_HEADS + h, d_araw, 0.0) \
                + jnp.where(lane == 2 * DN_HEADS + h, d_g * ms[h]["g"], 0.0)
            dqkv_ref[:, h * DK:(h + 1) * DK] = d_q[h]
            dqkv_ref[:, D_DN + h * DK:D_DN + (h + 1) * DK] = d_k[h]
            dqkv_ref[:, 2 * D_DN + h * DK:2 * D_DN + (h + 1) * DK] = d_vb[h] * beta[h]
        dba_ref[...] = dba

    rev = lambda w_: pl.BlockSpec((c, w_), lambda i: (n - 1 - i, 0))
    one = pl.BlockSpec((1, 128), lambda i: (0, 0))
    return pl.pallas_call(
        body, name="dn_scan_bwd", grid=(n,),
        in_specs=[rev(1536), rev(128), rev(512), pl.BlockSpec((1, DN_HEADS, DK, DK), lambda i: (n - 1 - i, 0, 0, 0)),
                  pl.BlockSpec((1, DN_HEADS, c, c), lambda i: (n - 1 - i, 0, 0, 0)), one, one],
        out_specs=[rev(1536), rev(128)],
        out_shape=[jax.ShapeDtypeStruct((t, 1536), F32), jax.ShapeDtypeStruct((t, 128), F32)],
        scratch_shapes=[pltpu.VMEM((DN_HEADS, DK, DK), F32)],
        compiler_params=_params(("arbitrary",)),
    )(qkv, ba, do, sh, th, arow, dtb)


def _dn_prep_bwd(pdn, cw, dact):
    t = pdn.shape[0]
    nchunk = t // CONV_ROWS

    def body(u_ref, w_ref, d_ref, du_ref, dw_ref, dy_ref):
        j = pl.program_id(0)
        dy_ref[t:t + 8, :] = jnp.zeros((8, 128), F32)
        dw = [jnp.zeros((1, 128), F32) for _ in range(4)]
        for c in range(nchunk):
            sl = slice(c * CONV_ROWS, (c + 1) * CONV_ROWS)
            taps, y = _conv_taps(u_ref, c, w_ref)
            a = _silu(y)
            dout = d_ref[sl, :]
            rs = lax.rsqrt(jnp.sum(a * a, axis=1, keepdims=True) + EPS)
            f = jnp.where(j < 8, rs, 1.0) * jnp.where(j < 4, DK ** -0.5, 1.0)
            corr = jnp.where(j < 8, f * rs * rs * jnp.sum(dout * a, axis=1, keepdims=True), 0.0)
            dy = (f * dout - corr * a) * _dsilu(y)
            dy_ref[sl, :] = dy
            for k_ in range(4):
                dw[3 - k_] = dw[3 - k_] + jnp.sum(taps[k_] * dy, axis=0, keepdims=True)
        for i in range(4):
            dw_ref[i:i + 1, :] = dw[i]
        for c in range(nchunk):
            r0 = c * CONV_ROWS
            ext = dy_ref[r0:r0 + CONV_ROWS + 8, :]
            du = ext[:CONV_ROWS, :] * w_ref[3:4, :]
            for k_ in (1, 2, 3):
                du = du + pltpu.roll(ext, CONV_ROWS + 8 - k_, 0)[:CONV_ROWS, :] * w_ref[3 - k_:4 - k_, :]
            du_ref[r0:r0 + CONV_ROWS, :] = du

    return pl.pallas_call(
        body, name="dn_prep_bwd", grid=(12,),
        in_specs=[pl.BlockSpec((t, 128), lambda j: (0, j)), pl.BlockSpec((4, 128), lambda j: (0, j)),
                  pl.BlockSpec((t, 128), lambda j: (0, j))],
        out_specs=[pl.BlockSpec((t, 128), lambda j: (0, j)), pl.BlockSpec((4, 128), lambda j: (0, j))],
        out_shape=[jax.ShapeDtypeStruct((t, 1536), F32), jax.ShapeDtypeStruct((4, 1536), F32)],
        scratch_shapes=[pltpu.VMEM((t + 8, 128), F32)],
        compiler_params=_params(("arbitrary",)),
    )(pdn, cw, dact)


def _inproj_bwd(x, nw, wt, dy, dpdn, dz, dpatt, dgate, dba):
    t = x.shape[0]
    tm = 256

    def body(x_ref, nw_ref, w_ref, dy_ref, a_ref, b_ref, c_ref, d_ref, e_ref, gx_ref, dnw_ref, cs_ref):
        @pl.when(pl.program_id(0) == 0)
        def _():
            dnw_ref[...] = jnp.zeros_like(dnw_ref)
            cs_ref[...] = jnp.zeros_like(cs_ref)

        dh = jnp.zeros((tm, D_MODEL), F32)
        for ref, lo, hi in ((a_ref, 0, 1536), (b_ref, 1536, 2048), (c_ref, 2048, 3584),
                            (d_ref, 3584, 4096), (e_ref, 4096, 4224)):
            dh = dh + jnp.dot(ref[...].astype(MXU), w_ref[lo:hi, :], preferred_element_type=F32)
        xv = x_ref[...]
        rstd = lax.rsqrt(jnp.mean(xv * xv, axis=-1, keepdims=True) + EPS)
        xh = xv * rstd
        gg = dh * nw_ref[...]
        gx_ref[...] = rstd * (gg - xh * jnp.mean(gg * xh, axis=-1, keepdims=True)) + dy_ref[...]
        dnw_ref[...] += jnp.broadcast_to(jnp.sum(dh * xh, axis=0, keepdims=True), (8, D_MODEL))
        cs_ref[...] += jnp.broadcast_to(jnp.sum(e_ref[...], axis=0, keepdims=True), (8, 128))

    row = lambda n: pl.BlockSpec((tm, n), lambda i: (i, 0))
    full = lambda a: pl.BlockSpec(a.shape, lambda i: (0,) * a.ndim)
    return pl.pallas_call(
        body, name="inproj_bwd", grid=(t // tm,),
        in_specs=[row(D_MODEL), full(nw), full(wt), row(D_MODEL), row(1536), row(512), row(1536), row(512), row(128)],
        out_specs=[row(D_MODEL), pl.BlockSpec((8, D_MODEL), lambda i: (0, 0)), pl.BlockSpec((8, 128), lambda i: (0, 0))],
        out_shape=[jax.ShapeDtypeStruct((t, D_MODEL), F32), jax.ShapeDtypeStruct((8, D_MODEL), F32),
                   jax.ShapeDtypeStruct((8, 128), F32)],
        compiler_params=_params(("arbitrary",)),
    )(x, nw, wt, dy, dpdn, dz, dpatt, dgate, dba)


def _adamw_sum(w, gs, m, v, name):
    r, c = w.shape
    nsum = gs.shape[0]
    tr = r if r <= 256 else 256
    c1 = 1.0 - ADAM_B1 ** ADAM_STEP
    c2 = 1.0 - ADAM_B2 ** ADAM_STEP

    def body(w_ref, g_ref, m_ref, v_ref, go_ref, d_ref, mo_ref, vo_ref):
        g = g_ref[0].astype(F32)
        for s in range(1, nsum):
            g = g + g_ref[s].astype(F32)
        mn = ADAM_B1 * m_ref[...] + (1.0 - ADAM_B1) * g
        vn = ADAM_B2 * v_ref[...] + (1.0 - ADAM_B2) * (g * g)
        go_ref[...] = g
        mo_ref[...] = mn
        vo_ref[...] = vn
        d_ref[...] = -ADAM_LR * ((mn / c1) / (jnp.sqrt(vn / c2) + ADAM_EPS) + ADAM_WD * w_ref[...])

    blk = pl.BlockSpec((tr, c), lambda i: (i, 0))
    return pl.pallas_call(
        body, name=name, grid=(r // tr,),
        in_specs=[blk, pl.BlockSpec((nsum, tr, c), lambda i: (0, i, 0)), blk, blk],
        out_specs=[blk] * 4, out_shape=[jax.ShapeDtypeStruct((r, c), F32)] * 4,
        compiler_params=_params(("arbitrary",)),
    )(w, gs, m, v)


def _local_step(x, target, norm_w, w_sect, conv_w, a_log, dt_bias, dn_norm_w, q_norm_w, k_norm_w, rel_bias, w_out):
    lane = np.arange(128)
    arow = jnp.zeros((1, 128), F32).at[0, DN_HEADS:2 * DN_HEADS].set(-jnp.exp(a_log[0]))
    dtb = jnp.zeros((1, 128), F32).at[0, DN_HEADS:2 * DN_HEADS].set(dt_bias[0])
    g_np, gt_np = _group_mats()
    g, gt = jnp.asarray(g_np), jnp.asarray(gt_np)
    bk = jnp.asarray(_bucket_tables())
    wq = jnp.tile(q_norm_w, (1, ATT_HEADS)) * (HD ** -0.5)
    wk = jnp.tile(k_norm_w, (1, ATT_HEADS))
    del lane

    h, pdn, z, patt, gate, ba = _inproj(x, norm_w, w_sect)
    qkv_dn = _dn_prep(pdn, conv_w)
    oraw, ydn, sh, th = _dn_scan_fwd(qkv_dn, ba, z, arow, dtb, dn_norm_w)
    qkv_att = _att_prep(patt, g, gt, wq, wk)
    oatt, yatt, lse = _att_fwd(qkv_att, gate, rel_bias, bk)
    dy, mix, loss8 = _outproj_loss(x, ydn, yatt, w_out, target)

    do_dn, dz, do_att, dgate, dd, ddnw = _outproj_bwd(dy, w_out.T, oraw, z, dn_norm_w, oatt, gate, g, gt)
    d_wout = _matmul_tn(mix, dy, "dw_out")
    dq, dk, dv, drb = _att_bwd(qkv_att, do_att, lse, dd, rel_bias, bk)
    dpatt, dwq8, dwk8 = _att_prep_bwd(patt, dq, dk, dv, g, gt, wq, wk)
    dqkv_dn, dba = _dn_scan_bwd(qkv_dn, ba, do_dn, sh, th, arow, dtb)
    dpdn, d_conv = _dn_prep_bwd(pdn, conv_w, dqkv_dn)
    grad_x, dnw8, cs8 = _inproj_bwd(x, norm_w, w_sect.T, dy, dpdn, dz, dpatt, dgate, dba)
    dw_sections = [_matmul_tn(h, d_, nm) for d_, nm in
                   ((dpdn, "dw_in_dn"), (dz, "dw_in_z"), (dpatt, "dw_in_att"), (dgate, "dw_in_gate"), (dba, "dw_in_ba"))]

    grads = dict(
        loss=loss8[0, 0], grad_x=grad_x, norm_w=dnw8[0:1], w_in_sections=dw_sections, conv_w=d_conv,
        a_log=cs8[0:1, 2 * DN_HEADS:3 * DN_HEADS], dt_bias=cs8[0:1, DN_HEADS:2 * DN_HEADS],
        dn_norm_w=ddnw[0:1],
        q_norm_w=(dwq8[0].reshape(ATT_HEADS, HD).sum(0) * (HD ** -0.5))[None],
        k_norm_w=dwk8[0].reshape(ATT_HEADS, HD).sum(0)[None],
        rel_bias=drb[:, :N_BUCKETS], w_out=d_wout)
    return grads


SMALL = (("norm_w", 1024), ("a_log", 4), ("dt_bias", 4), ("dn_norm_w", 128), ("q_norm_w", 64), ("k_norm_w", 64),
         ("rel_bias", 256))
SMALL_ROWS = 16


def _pack_small(d):
    flat = jnp.concatenate([d[k].reshape(-1) for k, _ in SMALL])
    return jnp.pad(flat, (0, SMALL_ROWS * 128 - flat.shape[0])).reshape(SMALL_ROWS, 128)


def _unpack_small(p, shapes):
    flat = p.reshape(-1)
    out, off = {}, 0
    for k, n in SMALL:
        out[k] = flat[off:off + n].reshape(shapes[k])
        off += n
    return out


def kernel(x, norm_w, w_in, conv_w, a_log, dt_bias, dn_norm_w, q_norm_w, k_norm_w, rel_bias, w_out, loss_target, m_norm_w, m_w_in, m_conv_w, m_a_log, m_dt_bias, m_dn_norm_w, m_q_norm_w, m_k_norm_w, m_rel_bias, m_w_out, v_norm_w, v_w_in, v_conv_w, v_a_log, v_dt_bias, v_dn_norm_w, v_q_norm_w, v_k_norm_w, v_rel_bias, v_w_out):
    ncol = w_in.shape[2]
    d_in = N_DEV * ncol
    win8, wout8, conv8 = _gather_weights([w_in[0].astype(MXU), w_out[0].astype(MXU), conv_w[0]])
    w_full = win8.transpose(1, 0, 2).reshape(D_MODEL, d_in)
    w_sect = jnp.concatenate([w_full[:, :2048], w_full[:, 2056:], w_full[:, 2048:2056],
                              jnp.zeros((D_MODEL, W_COLS - d_in), MXU)], axis=1)
    wout_full = wout8.reshape(D_MODEL, D_MODEL)
    conv_full = conv8.transpose(1, 0, 2).reshape(4, 3 * D_DN)

    gr = _local_step(x[0], loss_target[0], norm_w, w_sect, conv_full, a_log, dt_bias, dn_norm_w, q_norm_w, k_norm_w,
                     rel_bias, wout_full)

    s_dn, s_z, s_att, s_gate, s_ba = gr["w_in_sections"]
    dw_full = jnp.concatenate([s_dn, s_z, s_ba[:, :2 * DN_HEADS], s_att, s_gate], axis=1)
    slabs = [dw_full.reshape(D_MODEL, 4, 2, ncol).transpose(2, 1, 0, 3),
             gr["w_out"].reshape(4, 2, D_MODEL // N_DEV, D_MODEL).transpose(1, 0, 2, 3),
             gr["conv_w"].reshape(4, 4, 2, 3 * D_DN // N_DEV).transpose(2, 1, 0, 3),
             jnp.broadcast_to(_pack_small(gr)[None, None], (2, 4, SMALL_ROWS, 128))]
    core = lax.axis_index("c").astype(jnp.int32).reshape(1)
    from_sibling = _swap_siblings(slabs)
    wires = (GRAD_WIRE, GRAD_WIRE, F32, F32)
    partial = [_chip_sum(slabs[i], from_sibling[i], core, wires[i], "chip_sum_%d" % i) for i in range(4)]
    r_win, r_wout, r_conv, r_small = _swap_chips(partial)

    g_win, d_win, m_win, v_win = _adamw_sum(w_in[0], r_win, m_w_in[0], v_w_in[0], "adamw_w_in")
    g_wout, d_wout, m_wout, v_wout = _adamw_sum(w_out[0], r_wout, m_w_out[0], v_w_out[0], "adamw_w_out")
    g_conv, d_conv, m_conv, v_conv = _adamw_sum(conv_w[0], r_conv, m_conv_w[0], v_conv_w[0], "adamw_conv_w")
    ws = dict(norm_w=norm_w, a_log=a_log, dt_bias=dt_bias, dn_norm_w=dn_norm_w, q_norm_w=q_norm_w,
              k_norm_w=k_norm_w, rel_bias=rel_bias)
    ms = dict(norm_w=m_norm_w, a_log=m_a_log, dt_bias=m_dt_bias, dn_norm_w=m_dn_norm_w, q_norm_w=m_q_norm_w,
              k_norm_w=m_k_norm_w, rel_bias=m_rel_bias)
    vs = dict(norm_w=v_norm_w, a_log=v_a_log, dt_bias=v_dt_bias, dn_norm_w=v_dn_norm_w, q_norm_w=v_q_norm_w,
              k_norm_w=v_k_norm_w, rel_bias=v_rel_bias)
    shapes = {k: a.shape for k, a in ws.items()}
    packs = _adamw_sum(_pack_small(ws), r_small, _pack_small(ms), _pack_small(vs), "adamw_small")
    g_s, d_s, m_s, v_s = (_unpack_small(p, shapes) for p in packs)

    loss = lax.psum(gr["loss"], AXES)
    names = ("norm_w", "w_in", "conv_w", "a_log", "dt_bias", "dn_norm_w", "q_norm_w", "k_norm_w", "rel_bias", "w_out")
    big = dict(w_in=(g_win, d_win, m_win, v_win), conv_w=(g_conv, d_conv, m_conv, v_conv),
               w_out=(g_wout, d_wout, m_wout, v_wout))
    outs = [loss, gr["grad_x"][None]]
    for i, sm in enumerate((g_s, d_s, m_s, v_s)):
        for nm in names:
            outs.append(big[nm][i][None] if nm in big else sm[nm])
    return tuple(outs)
```

```python
import functools
import math

import numpy as np
import jax
import jax.numpy as jnp
from jax import lax
from jax.experimental import pallas as pl
from jax.experimental.pallas import tpu as pltpu

F32 = jnp.float32
MXU = jnp.bfloat16
GRAD_WIRE = jnp.bfloat16
HI = lax.Precision.HIGHEST

D_MODEL = 1024
D_DN = 512
DN_HEADS = 4
DK = 128
CHUNK = 64
D_ATT = 512
ATT_HEADS = 8
HD = 64
PATTERNS = ((128, 1), (512, 4), (2048, 16))
BLK = 128
N_BUCKETS = 32
MAX_DISTANCE = 2048
EPS = 1e-6
W_COLS = 4224
N_DEV = 8
AXES = ("x", "y", "c")

ADAM_LR = 0.001
ADAM_B1 = 0.9
ADAM_B2 = 0.999
ADAM_EPS = 1e-08
ADAM_WD = 0.01
ADAM_STEP = 10

VMEM_LIMIT = 56 * 1024 * 1024
NEG = -1e30


def _dot(a, b):
    return jnp.dot(a.astype(MXU), b.astype(MXU), preferred_element_type=F32)


def _dot_nt(a, b):
    return lax.dot_general(a.astype(MXU), b.astype(MXU), (((1,), (1,)), ((), ())), preferred_element_type=F32)


def _dot_tn(a, b):
    return lax.dot_general(a.astype(MXU), b.astype(MXU), (((0,), (0,)), ((), ())), preferred_element_type=F32)


def _split(a):
    hi = a.astype(jnp.bfloat16)
    return hi, (a - hi.astype(F32)).astype(jnp.bfloat16)


def _dot_split(a, b, dims, exact):
    dg = lambda u, v: lax.dot_general(u, v, (dims, ((), ())), preferred_element_type=F32)
    if exact == "b":
        ah, al = _split(a)
        bh = b.astype(jnp.bfloat16)
        return dg(ah, bh) + dg(al, bh)
    if exact == "a":
        bh, bm = _split(b)
        bl = (b - bh.astype(F32) - bm.astype(F32)).astype(jnp.bfloat16)
        ah = a.astype(jnp.bfloat16)
        return dg(ah, bh) + (dg(ah, bm) + dg(ah, bl))
    ah, al = _split(a)
    bh, bl = _split(b)
    return dg(ah, bh) + (dg(ah, bl) + dg(al, bh))


def _hdot(a, b, exact=None):
    return _dot_split(a, b, ((1,), (0,)), exact)


def _hdot_nt(a, b, exact=None):
    return _dot_split(a, b, ((1,), (1,)), exact)


def _hdot_tn(a, b, exact=None):
    return _dot_split(a, b, ((0,), (0,)), exact)


def _sigmoid(x):
    return 1.0 / (1.0 + jnp.exp(-x))


def _silu(x):
    return x * _sigmoid(x)


def _dsilu(x):
    s = _sigmoid(x)
    return s * (1.0 + x * (1.0 - s))


def _softplus(x):
    return jnp.maximum(x, 0.0) + jnp.log(1.0 + jnp.exp(-jnp.abs(x)))


def _iota(shape, dim):
    return lax.broadcasted_iota(jnp.int32, shape, dim)


def _lane_col(x, k):
    return jnp.sum(jnp.where(_iota(x.shape, 1) == k, x, 0.0), axis=1, keepdims=True)


def _params(sem=None):
    return pltpu.CompilerParams(dimension_semantics=sem, vmem_limit_bytes=VMEM_LIMIT)


def _t5_bucket(dist):
    max_exact = N_BUCKETS // 2
    d = np.maximum(dist, 1).astype(np.float64)
    large = max_exact + (np.log(d / max_exact) / math.log(MAX_DISTANCE / max_exact)
                         * (N_BUCKETS - max_exact)).astype(np.int32)
    large = np.minimum(large, N_BUCKETS - 1)
    return np.where(dist < max_exact, dist, large).astype(np.int32)


def _bucket_tables():
    qi = np.arange(BLK)[:, None]
    kj = np.arange(2 * BLK)[None, :]
    step = qi - kj + BLK
    band = (step >= 0) & (step <= BLK)
    out = []
    for _, r in PATTERNS:
        b = _t5_bucket(np.clip(step, 0, None) * r)
        out.append(np.where(band, b, -1))
    return np.stack(out).astype(np.int32)


def _group_mats():
    g = np.zeros((D_ATT, 128), np.float32)
    for h in range(ATT_HEADS):
        g[h * HD:(h + 1) * HD, h] = 1.0
    return g, np.ascontiguousarray(g.T)


CHIP_FLIPS = ((1, 0), (0, 1), (1, 1))
ANY_SPEC = pl.BlockSpec(memory_space=pl.ANY)
MESH_ID = pl.DeviceIdType.MESH


def _other_chips():
    x, y = lax.axis_index("x"), lax.axis_index("y")
    return [((1 - x if fx else x), (1 - y if fy else y)) for fx, fy in CHIP_FLIPS]


def _gather_weights(arrs):
    n = len(arrs)

    def body(*refs):
        ins, outs = refs[:n], refs[n:2 * n]
        send, recv, loc = refs[2 * n:]
        x, y, c = (lax.axis_index(a) for a in AXES)
        sib = (x, y, 1 - c)
        chips = _other_chips()
        lin = lambda px, py, pc: 4 * px + 2 * py + pc

        def copy(a, k, block, to, src=None):
            slot = outs[a].at[lin(*block)]
            return pltpu.make_async_remote_copy(src_ref=slot if src is None else src, dst_ref=slot,
                                                send_sem=send.at[a, k], recv_sem=recv.at[a, k],
                                                device_id=to, device_id_type=MESH_ID)

        started = []
        for a in range(n):
            mine = pltpu.make_async_copy(ins[a], outs[a].at[lin(x, y, c)], loc.at[a])
            mine.start()
            started.append(mine)
        firsts = []
        for a in range(n):
            firsts.append(copy(a, 0, (x, y, c), sib, src=ins[a]))
            firsts += [copy(a, 1 + j, (x, y, c), (*chip, c), src=ins[a]) for j, chip in enumerate(chips)]
        for cp in firsts:
            cp.start()
        passed = []
        for j, chip in enumerate(chips):
            for a in range(n):
                copy(a, 1 + j, (*chip, c), (x, y, c)).wait_recv()
                fw = copy(a, 4 + j, (*chip, c), sib)
                fw.start()
                passed.append(fw)
        for a in range(n):
            copy(a, 0, sib, (x, y, c)).wait_recv()
            for j, chip in enumerate(chips):
                copy(a, 4 + j, (*chip, 1 - c), (x, y, c)).wait_recv()
        for cp in firsts + passed:
            cp.wait_send()
        for mine in started:
            mine.wait()

    return pl.pallas_call(
        body, name="gather_weights", out_shape=[jax.ShapeDtypeStruct((N_DEV,) + a.shape, a.dtype) for a in arrs],
        in_specs=[ANY_SPEC] * n, out_specs=[ANY_SPEC] * n,
        scratch_shapes=[pltpu.SemaphoreType.DMA((n, 7)), pltpu.SemaphoreType.DMA((n, 7)), pltpu.SemaphoreType.DMA((n,))],
    )(*arrs)


def _swap_siblings(arrs):
    n = len(arrs)

    def body(*refs):
        ins, outs = refs[:n], refs[n:2 * n]
        send, recv = refs[2 * n:]
        x, y, c = (lax.axis_index(a) for a in AXES)
        cps = [pltpu.make_async_remote_copy(src_ref=ins[a].at[1 - c], dst_ref=outs[a], send_sem=send.at[a],
                                            recv_sem=recv.at[a], device_id=(x, y, 1 - c), device_id_type=MESH_ID)
               for a in range(n)]
        for cp in cps:
            cp.start()
        for cp in cps:
            cp.wait()

    return pl.pallas_call(
        body, name="swap_siblings", out_shape=[jax.ShapeDtypeStruct(a.shape[1:], a.dtype) for a in arrs],
        in_specs=[ANY_SPEC] * n, out_specs=[ANY_SPEC] * n,
        scratch_shapes=[pltpu.SemaphoreType.DMA((n,)), pltpu.SemaphoreType.DMA((n,))],
    )(*arrs)


def _chip_sum(mine2, theirs, core, wire, name):
    _, nchip, r, cdim = mine2.shape
    tr = r if r <= 256 else 256

    def body(core_ref, a_ref, b_ref, o_ref):
        del core_ref
        o_ref[...] = (a_ref[...].astype(F32) + b_ref[...].astype(F32)).astype(wire)

    grid_spec = pltpu.PrefetchScalarGridSpec(
        num_scalar_prefetch=1, grid=(nchip, r // tr),
        in_specs=[pl.BlockSpec((None, None, tr, cdim), lambda j, i, cr: (cr[0], j, i, 0)),
                  pl.BlockSpec((None, tr, cdim), lambda j, i, cr: (j, i, 0))],
        out_specs=pl.BlockSpec((None, tr, cdim), lambda j, i, cr: (j, i, 0)))
    return pl.pallas_call(
        body, name=name, grid_spec=grid_spec, out_shape=jax.ShapeDtypeStruct((nchip, r, cdim), wire),
        compiler_params=_params(("arbitrary", "arbitrary")),
    )(core, mine2, theirs)


def _swap_chips(arrs):
    n = len(arrs)

    def body(*refs):
        ins, outs = refs[:n], refs[n:2 * n]
        send, recv, loc = refs[2 * n:]
        x, y, c = (lax.axis_index(a) for a in AXES)
        me = 2 * x + y
        chips = _other_chips()
        locs, sends, recvs = [], [], []
        for a in range(n):
            lc = pltpu.make_async_copy(ins[a].at[me], outs[a].at[me], loc.at[a])
            lc.start()
            locs.append(lc)
            for j, (px, py) in enumerate(chips):
                them = 2 * px + py
                cp = pltpu.make_async_remote_copy(src_ref=ins[a].at[them], dst_ref=outs[a].at[me], send_sem=send.at[a, j],
                                                  recv_sem=recv.at[a, j], device_id=(px, py, c), device_id_type=MESH_ID)
                cp.start()
                sends.append(cp)
                recvs.append(pltpu.make_async_remote_copy(src_ref=ins[a].at[them], dst_ref=outs[a].at[them],
                                                          send_sem=send.at[a, j], recv_sem=recv.at[a, j],
                                                          device_id=(px, py, c), device_id_type=MESH_ID))
        for cp in recvs:
            cp.wait_recv()
        for cp in sends:
            cp.wait_send()
        for lc in locs:
            lc.wait()

    return pl.pallas_call(
        body, name="swap_chips", out_shape=[jax.ShapeDtypeStruct(a.shape, a.dtype) for a in arrs],
        in_specs=[ANY_SPEC] * n, out_specs=[ANY_SPEC] * n,
        scratch_shapes=[pltpu.SemaphoreType.DMA((n, 3)), pltpu.SemaphoreType.DMA((n, 3)), pltpu.SemaphoreType.DMA((n,))],
    )(*arrs)


def _inproj(x, nw, w):
    t = x.shape[0]
    tm = 256

    def body(x_ref, nw_ref, w_ref, ht_ref, pdn_ref, z_ref, patt_ref, gate_ref, ba_ref):
        xv = x_ref[...]
        rstd = lax.rsqrt(jnp.mean(xv * xv, axis=-1, keepdims=True) + EPS)
        hf = xv * rstd * nw_ref[...]
        h = hf.astype(MXU)
        ht_ref[...] = hf.T.astype(MXU)
        for ref, lo, hi in ((pdn_ref, 0, 1536), (z_ref, 1536, 2048), (patt_ref, 2048, 3584),
                            (gate_ref, 3584, 4096), (ba_ref, 4096, 4224)):
            ref[...] = jnp.dot(h, w_ref[:, lo:hi], preferred_element_type=F32)

    row = lambda n: pl.BlockSpec((tm, n), lambda i: (i, 0))
    full = lambda a: pl.BlockSpec(a.shape, lambda i: (0,) * a.ndim)
    return pl.pallas_call(
        body, name="inproj", grid=(t // tm,),
        in_specs=[row(D_MODEL), full(nw), full(w)],
        out_specs=[pl.BlockSpec((D_MODEL, tm), lambda i: (0, i)), row(1536), row(512), row(1536), row(512), row(128)],
        out_shape=[jax.ShapeDtypeStruct((D_MODEL, t), MXU)] +
                  [jax.ShapeDtypeStruct((t, n), F32) for n in (1536, 512, 1536, 512, 128)],
        compiler_params=_params(("arbitrary",)),
    )(x, nw, w)


CONV_ROWS = 512


def _conv_taps(u_ref, c, w_ref):
    r0 = c * CONV_ROWS
    if c == 0:
        ext = jnp.concatenate([jnp.zeros((8, 128), F32), u_ref[0:CONV_ROWS, :]], axis=0)
    else:
        ext = u_ref[r0 - 8:r0 + CONV_ROWS, :]
    taps = [ext[8:, :]] + [pltpu.roll(ext, k, 0)[8:, :] for k in (1, 2, 3)]
    y = taps[0] * w_ref[3:4, :]
    for k in (1, 2, 3):
        y = y + taps[k] * w_ref[3 - k:4 - k, :]
    return taps, y


def _dn_prep(pdn, cw):
    t = pdn.shape[0]

    def body(u_ref, w_ref, o_ref):
        j = pl.program_id(0)
        for c in range(t // CONV_ROWS):
            _, y = _conv_taps(u_ref, c, w_ref)
            a = _silu(y)
            ssq = jnp.sum(a * a, axis=1, keepdims=True)
            f = jnp.where(j < 8, lax.rsqrt(ssq + EPS), 1.0) * jnp.where(j < 4, DK ** -0.5, 1.0)
            o_ref[c * CONV_ROWS:(c + 1) * CONV_ROWS, :] = a * f

    return pl.pallas_call(
        body, name="dn_prep", grid=(12,),
        in_specs=[pl.BlockSpec((t, 128), lambda j: (0, j)), pl.BlockSpec((4, 128), lambda j: (0, j))],
        out_specs=pl.BlockSpec((t, 128), lambda j: (0, j)),
        out_shape=jax.ShapeDtypeStruct((t, 1536), F32),
        compiler_params=_params(("arbitrary",)),
    )(pdn, cw)


def _chunk_common(qkv, ba, arow, dtb):
    c = CHUNK
    ri, ci = _iota((c, c), 0), _iota((c, c), 1)
    lane = _iota((c, 128), 1)
    g_all = jnp.where((lane >= DN_HEADS) & (lane < 2 * DN_HEADS), arow * _softplus(ba + dtb), 0.0)
    gc_all = _hdot((ri >= ci).astype(F32), g_all, "a")
    gc_t = gc_all.T
    beta_all = _sigmoid(ba)
    out = []
    for h in range(DN_HEADS):
        gc = _lane_col(gc_all, DN_HEADS + h)
        gcr = gc_t[DN_HEADS + h:DN_HEADS + h + 1, :]
        gl = gc[c - 1:c, :]
        out.append(dict(
            q=qkv[:, h * DK:(h + 1) * DK], k=qkv[:, D_DN + h * DK:D_DN + (h + 1) * DK],
            v=qkv[:, 2 * D_DN + h * DK:2 * D_DN + (h + 1) * DK],
            beta=_lane_col(beta_all, h), g=_lane_col(g_all, DN_HEADS + h),
            a_raw=_lane_col(ba, DN_HEADS + h), a_h=_lane_col(arow, DN_HEADS + h), dt_h=_lane_col(dtb, DN_HEADS + h),
            decay=jnp.exp(jnp.where(ri >= ci, gc - gcr, NEG)), eg=jnp.exp(gc), egl=jnp.exp(gl), etail=jnp.exp(gl - gc)))
    return out, ri, ci


def _dn_scan_fwd(qkv, ba, z, arow, dtb, dnw):
    t = qkv.shape[0]
    n = t // CHUNK
    c = CHUNK
    hs = range(DN_HEADS)

    def body(qkv_ref, ba_ref, z_ref, arow_ref, dtb_ref, dnw_ref, o_ref, y_ref, sh_ref, th_ref, s_ref):
        @pl.when(pl.program_id(0) == 0)
        def _():
            s_ref[...] = jnp.zeros_like(s_ref)

        ms, ri, ci = _chunk_common(qkv_ref[...], ba_ref[...], arow_ref[...], dtb_ref[...])
        kb = [m["k"] * m["beta"] for m in ms]
        amat = [jnp.where(ri > ci, _dot_nt(kb[h], ms[h]["k"]) * ms[h]["decay"], 0.0) for h in hs]
        attn = [jnp.where(ri >= ci, _dot_nt(m["q"], m["k"]) * m["decay"], 0.0) for m in ms]
        eye = (ri == ci).astype(F32)
        tinv = [eye - a for a in amat]
        pw = amat
        for _ in range(5):
            pw = [_hdot(p, p) for p in pw]
            tinv = [tv + _hdot(tv, p) for tv, p in zip(tinv, pw)]
        u = [_hdot(tinv[h], ms[h]["v"] * ms[h]["beta"]) for h in hs]
        w = [_hdot(tinv[h], kb[h] * ms[h]["eg"]) for h in hs]
        s = [s_ref[h] for h in hs]
        v_new = [u[h] - _dot(w[h], s[h]) for h in hs]
        o = [_dot(ms[h]["q"] * ms[h]["eg"], s[h]) + _dot(attn[h], v_new[h]) for h in hs]
        for h in hs:
            s_ref[h] = s[h] * ms[h]["egl"] + _dot_tn(ms[h]["k"] * ms[h]["etail"], v_new[h])
            sh_ref[0, h] = s[h]
            th_ref[0, h] = tinv[h]
            o_ref[:, h * DK:(h + 1) * DK] = o[h]
            rs = lax.rsqrt(jnp.mean(o[h] * o[h], axis=1, keepdims=True) + EPS)
            y_ref[:, h * DK:(h + 1) * DK] = o[h] * rs * dnw_ref[...] * _silu(z_ref[:, h * DK:(h + 1) * DK])

    row = lambda w_: pl.BlockSpec((c, w_), lambda i: (i, 0))
    one = pl.BlockSpec((1, 128), lambda i: (0, 0))
    return pl.pallas_call(
        body, name="dn_scan_fwd", grid=(n,),
        in_specs=[row(1536), row(128), row(512), one, one, one],
        out_specs=[row(512), row(512), pl.BlockSpec((1, DN_HEADS, DK, DK), lambda i: (i, 0, 0, 0)),
                   pl.BlockSpec((1, DN_HEADS, c, c), lambda i: (i, 0, 0, 0))],
        out_shape=[jax.ShapeDtypeStruct((t, 512), F32), jax.ShapeDtypeStruct((t, 512), F32),
                   jax.ShapeDtypeStruct((n, DN_HEADS, DK, DK), F32), jax.ShapeDtypeStruct((n, DN_HEADS, c, c), F32)],
        scratch_shapes=[pltpu.VMEM((DN_HEADS, DK, DK), F32)],
        compiler_params=_params(("arbitrary",)),
    )(qkv, ba, z, arow, dtb, dnw)


def _att_prep(patt, g, gt, wq, wk):
    t = patt.shape[0]
    tm = 512

    def body(p_ref, g_ref, gt_ref, wq_ref, wk_ref, o_ref):
        for lo, w_ref in ((0, wq_ref), (512, wk_ref)):
            xv = p_ref[:, lo:lo + 512]
            rstd = lax.rsqrt(_hdot(xv * xv, g_ref[...], "b") * (1.0 / HD) + EPS)
            o_ref[:, lo:lo + 512] = xv * _hdot(rstd, gt_ref[...], "b") * w_ref[...]
        o_ref[:, 1024:1536] = p_ref[:, 1024:1536]

    full = lambda a: pl.BlockSpec(a.shape, lambda i: (0,) * a.ndim)
    return pl.pallas_call(
        body, name="att_prep", grid=(t // tm,),
        in_specs=[pl.BlockSpec((tm, 1536), lambda i: (i, 0)), full(g), full(gt), full(wq), full(wk)],
        out_specs=pl.BlockSpec((tm, 1536), lambda i: (i, 0)),
        out_shape=jax.ShapeDtypeStruct((t, 1536), F32),
        compiler_params=_params(("arbitrary",)),
    )(patt, g, gt, wq, wk)


def _bias_tables(rb_ref, bk_ref, bias_ref, pair):
    for p in range(len(PATTERNS)):
        bk = bk_ref[p]
        for hh in range(2):
            head = 2 * pair + hh
            bm = jnp.full((BLK, 2 * BLK), NEG, F32)
            for b in range(N_BUCKETS):
                bm = jnp.where(bk == b, rb_ref[head, b], bm)
            bias_ref[p, hh * BLK:(hh + 1) * BLK, :] = bm


def _stack_heads(xb, h0):
    return jnp.concatenate([jnp.where(h0, xb, 0.0), jnp.where(h0, 0.0, xb)], axis=0).astype(MXU)


def _block_rows(t, r, n):
    per_class = (t // r) // BLK
    res = n // per_class
    j = n % per_class
    start = res + BLK * r * j
    pstart = res + BLK * r * jnp.maximum(j - 1, 0)
    if r == 1:
        return pl.ds(pl.multiple_of(start, BLK), BLK), pl.ds(pl.multiple_of(pstart, BLK), BLK), j
    return pl.ds(start, BLK, stride=r), pl.ds(pstart, BLK, stride=r), j


def _att_fwd(qkv, gate, rb, bk):
    t = qkv.shape[0]
    rows = 512

    def body(rb_ref, bk_ref, q_ref, k_ref, v_ref, g_ref, o_ref, y_ref, lse_ref,
             o0_ref, o1_ref, o2_ref, l0_ref, l1_ref, l2_ref, bias_ref):
        pair = pl.program_id(0)
        _bias_tables(rb_ref, bk_ref, bias_ref, pair)
        h0 = _iota((BLK, 128), 1) < HD
        prev_cols = _iota((2 * BLK, 2 * BLK), 1) < BLK
        op_refs, lp_refs = (o0_ref, o1_ref, o2_ref), (l0_ref, l1_ref, l2_ref)

        for p, (_, r) in enumerate(PATTERNS):
            def blk(n, carry, p=p, r=r):
                cur, prev, j = _block_rows(t, r, n)
                q2 = _stack_heads(q_ref[cur, :], h0)
                k2 = jnp.concatenate([k_ref[prev, :], k_ref[cur, :]], axis=0).astype(MXU)
                v2 = jnp.concatenate([v_ref[prev, :], v_ref[cur, :]], axis=0).astype(MXU)
                s = _dot_nt(q2, k2) + bias_ref[p] + jnp.where(prev_cols & (j == 0), NEG, 0.0)
                m = jnp.max(s, axis=1, keepdims=True)
                e = jnp.exp(s - m)
                l = jnp.sum(e, axis=1, keepdims=True)
                pv = _dot(e, v2) / l
                lse = m + jnp.log(l)
                op_refs[p][cur, :] = jnp.where(h0, pv[:BLK], pv[BLK:])
                lp_refs[p][cur, :] = jnp.where(h0, lse[:BLK], lse[BLK:])
                return carry

            lax.fori_loop(0, t // BLK, blk, 0, unroll=8)

        for c in range(t // rows):
            sl = slice(c * rows, (c + 1) * rows)
            ls = [ref[sl, :] for ref in lp_refs]
            mx = jnp.maximum(jnp.maximum(ls[0], ls[1]), ls[2])
            ws = [jnp.exp(v_ - mx) for v_ in ls]
            den = ws[0] + ws[1] + ws[2]
            o = (ws[0] * o0_ref[sl, :] + ws[1] * o1_ref[sl, :] + ws[2] * o2_ref[sl, :]) / den
            o_ref[sl, :] = o
            y_ref[sl, :] = o * _silu(g_ref[sl, :])
            lse_ref[sl, :] = mx + jnp.log(den)

    col = lambda off: pl.BlockSpec((t, 128), lambda i, off=off: (0, off + i))
    return pl.pallas_call(
        body, name="att_fwd", grid=(ATT_HEADS // 2,),
        in_specs=[pl.BlockSpec(memory_space=pltpu.SMEM), pl.BlockSpec(bk.shape, lambda i: (0, 0, 0)),
                  col(0), col(4), col(8), col(0)],
        out_specs=[col(0), col(0), col(0)],
        out_shape=[jax.ShapeDtypeStruct((t, 512), F32)] * 3,
        scratch_shapes=[pltpu.VMEM((t, 128), F32)] * 6 + [pltpu.VMEM((len(PATTERNS), 2 * BLK, 2 * BLK), F32)],
        compiler_params=_params(("arbitrary",)),
    )(rb, bk, qkv, qkv, qkv, gate)


def _outproj_loss(x, ydn, yatt, wout, target):
    t = x.shape[0]
    tm = 512

    def body(x_ref, a_ref, b_ref, w_ref, t_ref, dy_ref, mix_ref, loss_ref):
        @pl.when(pl.program_id(0) == 0)
        def _():
            loss_ref[...] = jnp.zeros_like(loss_ref)

        mixf = jnp.concatenate([a_ref[...], b_ref[...]], axis=1)
        mix_ref[...] = mixf.T.astype(MXU)
        err = x_ref[...] + jnp.dot(mixf.astype(MXU), w_ref[...], preferred_element_type=F32) - t_ref[...]
        dy_ref[...] = err * (1.0 / D_MODEL)
        loss_ref[...] += jnp.sum(err * err) * (0.5 / D_MODEL)

    row = lambda n: pl.BlockSpec((tm, n), lambda i: (i, 0))
    return pl.pallas_call(
        body, name="outproj_loss", grid=(t // tm,),
        in_specs=[row(D_MODEL), row(512), row(512), pl.BlockSpec(wout.shape, lambda i: (0, 0)), row(D_MODEL)],
        out_specs=[row(D_MODEL), pl.BlockSpec((D_MODEL, tm), lambda i: (0, i)), pl.BlockSpec((8, 128), lambda i: (0, 0))],
        out_shape=[jax.ShapeDtypeStruct((t, D_MODEL), F32), jax.ShapeDtypeStruct((D_MODEL, t), MXU),
                   jax.ShapeDtypeStruct((8, 128), F32)],
        compiler_params=_params(("arbitrary",)),
    )(x, ydn, yatt, wout, target)


def _outproj_bwd(dy, wout_t, oraw, z, dnw, oatt, gate, g, gt):
    t = dy.shape[0]
    tm = 256

    def body(dy_ref, w_ref, o_ref, z_ref, dnw_ref, oa_ref, g_ref, grp_ref, grpt_ref,
             do_ref, dz_ref, doa_ref, dg_ref, dd_ref, ddnw_ref):
        @pl.when(pl.program_id(0) == 0)
        def _():
            ddnw_ref[...] = jnp.zeros_like(ddnw_ref)

        dmix = jnp.dot(dy_ref[...].astype(MXU), w_ref[...], preferred_element_type=F32)
        dnw_v = dnw_ref[...]
        acc = jnp.zeros((1, DK), F32)
        for h in range(DN_HEADS):
            sl = slice(h * DK, (h + 1) * DK)
            o, zz, dm = o_ref[:, sl], z_ref[:, sl], dmix[:, sl]
            rs = lax.rsqrt(jnp.mean(o * o, axis=1, keepdims=True) + EPS)
            oh = o * rs
            dz_ref[:, sl] = dm * oh * dnw_v * _dsilu(zz)
            d_on = dm * _silu(zz)
            gg = d_on * dnw_v
            do_ref[:, sl] = rs * (gg - oh * jnp.mean(gg * oh, axis=1, keepdims=True))
            acc = acc + jnp.sum(d_on * oh, axis=0, keepdims=True)
        ddnw_ref[...] += jnp.broadcast_to(acc, (8, DK))
        da, gate_v, oa = dmix[:, 512:], g_ref[...], oa_ref[...]
        doa = da * _silu(gate_v)
        doa_ref[...] = doa
        dg_ref[...] = da * oa * _dsilu(gate_v)
        dd_ref[...] = _hdot(_hdot(doa * oa, grp_ref[...], "b"), grpt_ref[...], "b")

    row = lambda n: pl.BlockSpec((tm, n), lambda i: (i, 0))
    full = lambda a: pl.BlockSpec(a.shape, lambda i: (0,) * a.ndim)
    return pl.pallas_call(
        body, name="outproj_bwd", grid=(t // tm,),
        in_specs=[row(D_MODEL), full(wout_t), row(512), row(512), full(dnw), row(512), row(512), full(g), full(gt)],
        out_specs=[row(512)] * 5 + [pl.BlockSpec((8, DK), lambda i: (0, 0))],
        out_shape=[jax.ShapeDtypeStruct((t, 512), F32)] * 5 + [jax.ShapeDtypeStruct((8, DK), F32)],
        compiler_params=_params(("arbitrary",)),
    )(dy, wout_t, oraw, z, dnw, oatt, gate, g, gt)


def _grad_matmul(at, b, name):
    m, t = at.shape
    n = b.shape[1]
    tk = 512
    tn = n if n <= 512 else 512
    nk = t // tk

    def body(a_ref, b_ref, o_ref, acc_ref):
        k = pl.program_id(1)

        @pl.when(k == 0)
        def _():
            acc_ref[...] = jnp.zeros_like(acc_ref)

        acc_ref[...] += jnp.dot(a_ref[...], b_ref[...].astype(MXU), preferred_element_type=F32)

        @pl.when(k == nk - 1)
        def _():
            o_ref[...] = acc_ref[...].astype(GRAD_WIRE)

    return pl.pallas_call(
        body, name=name, grid=(n // tn, nk),
        in_specs=[pl.BlockSpec((m, tk), lambda j, k: (0, k)), pl.BlockSpec((tk, tn), lambda j, k: (k, j))],
        out_specs=pl.BlockSpec((m, tn), lambda j, k: (0, j)),
        out_shape=jax.ShapeDtypeStruct((m, n), GRAD_WIRE),
        scratch_shapes=[pltpu.VMEM((m, tn), F32)],
        compiler_params=_params(("arbitrary", "arbitrary")),
    )(at, b)


def _att_bwd(qkv, do, lse, dd, rb, bk):
    t = qkv.shape[0]
    rows = 512

    def body(rb_ref, bk_ref, q_ref, k_ref, v_ref, do_ref, lse_ref, dd_ref,
             dq_ref, dk_ref, dv_ref, db_ref, bias_ref, ds_ref):
        pair = pl.program_id(0)

        @pl.when(pair == 0)
        def _():
            db_ref[...] = jnp.zeros_like(db_ref)

        _bias_tables(rb_ref, bk_ref, bias_ref, pair)
        ds_ref[...] = jnp.zeros_like(ds_ref)
        for c in range(t // rows):
            sl = slice(c * rows, (c + 1) * rows)
            for ref in (dq_ref, dk_ref, dv_ref):
                ref[sl, :] = jnp.zeros((rows, 128), F32)
        h0 = _iota((BLK, 128), 1) < HD
        prev_cols = _iota((2 * BLK, 2 * BLK), 1) < BLK

        def rows_of(xb):
            return jnp.concatenate([xb[:, 0:1], xb[:, HD:HD + 1]], axis=0)

        for p, (_, r) in enumerate(PATTERNS):
            def blk(n, carry, p=p, r=r):
                cur, prev, j = _block_rows(t, r, n)
                q2, do2 = _stack_heads(q_ref[cur, :], h0), _stack_heads(do_ref[cur, :], h0)
                k2 = jnp.concatenate([k_ref[prev, :], k_ref[cur, :]], axis=0).astype(MXU)
                v2 = jnp.concatenate([v_ref[prev, :], v_ref[cur, :]], axis=0).astype(MXU)
                s = _dot_nt(q2, k2) + bias_ref[p] + jnp.where(prev_cols & (j == 0), NEG, 0.0)
                prob = jnp.exp(s - rows_of(lse_ref[cur, :]))
                ds = prob * (_dot_nt(do2, v2) - rows_of(dd_ref[cur, :]))
                ds_ref[p] += ds
                dq2 = _dot(ds, k2)
                dk2 = _dot_tn(ds, q2)
                dv2 = _dot_tn(prob, do2)
                dq_ref[cur, :] += jnp.where(h0, dq2[:BLK], dq2[BLK:])
                dk_ref[prev, :] += dk2[:BLK]
                dv_ref[prev, :] += dv2[:BLK]
                dk_ref[cur, :] += dk2[BLK:]
                dv_ref[cur, :] += dv2[BLK:]
                return carry

            lax.fori_loop(0, t // BLK, blk, 0, unroll=4)

        ri, ci = _iota((8, 128), 0), _iota((8, 128), 1)
        upd = jnp.zeros((8, 128), F32)
        for p in range(len(PATTERNS)):
            bk = bk_ref[p]
            for hh in range(2):
                dsum = ds_ref[p, hh * BLK:(hh + 1) * BLK, :]
                for b in range(N_BUCKETS):
                    val = jnp.sum(jnp.where(bk == b, dsum, 0.0))
                    upd = upd + jnp.where((ri == 2 * pair + hh) & (ci == b), val, 0.0)
        db_ref[...] += upd

    col = lambda off: pl.BlockSpec((t, 128), lambda i, off=off: (0, off + i))
    return pl.pallas_call(
        body, name="att_bwd", grid=(ATT_HEADS // 2,),
        in_specs=[pl.BlockSpec(memory_space=pltpu.SMEM), pl.BlockSpec(bk.shape, lambda i: (0, 0, 0)),
                  col(0), col(4), col(8), col(0), col(0), col(0)],
        out_specs=[col(0), col(0), col(0), pl.BlockSpec((8, 128), lambda i: (0, 0))],
        out_shape=[jax.ShapeDtypeStruct((t, 512), F32)] * 3 + [jax.ShapeDtypeStruct((8, 128), F32)],
        scratch_shapes=[pltpu.VMEM((len(PATTERNS), 2 * BLK, 2 * BLK), F32),
                        pltpu.VMEM((len(PATTERNS), 2 * BLK, 2 * BLK), F32)],
        compiler_params=_params(("arbitrary",)),
    )(rb, bk, qkv, qkv, qkv, do, lse, dd)


def _att_prep_bwd(patt, dq, dk, dv, g, gt, wq, wk):
    t = patt.shape[0]
    tm = 512

    def body(p_ref, dq_ref, dk_ref, dv_ref, g_ref, gt_ref, wq_ref, wk_ref, o_ref, dwq_ref, dwk_ref):
        @pl.when(pl.program_id(0) == 0)
        def _():
            dwq_ref[...] = jnp.zeros_like(dwq_ref)
            dwk_ref[...] = jnp.zeros_like(dwk_ref)

        for lo, w_ref, d_ref, dw_ref in ((0, wq_ref, dq_ref, dwq_ref), (512, wk_ref, dk_ref, dwk_ref)):
            xv, dyv = p_ref[:, lo:lo + 512], d_ref[...]
            rstd = lax.rsqrt(_hdot(xv * xv, g_ref[...], "b") * (1.0 / HD) + EPS)
            rsb = _hdot(rstd, gt_ref[...], "b")
            xh = xv * rsb
            gg = dyv * w_ref[...]
            mean = _hdot(_hdot(gg * xh, g_ref[...], "b") * (1.0 / HD), gt_ref[...], "b")
            o_ref[:, lo:lo + 512] = rsb * (gg - xh * mean)
            dw_ref[...] += jnp.broadcast_to(jnp.sum(dyv * xh, axis=0, keepdims=True), (8, 512))
        o_ref[:, 1024:1536] = dv_ref[...]

    row = lambda n: pl.BlockSpec((tm, n), lambda i: (i, 0))
    full = lambda a: pl.BlockSpec(a.shape, lambda i: (0,) * a.ndim)
    acc = pl.BlockSpec((8, 512), lambda i: (0, 0))
    return pl.pallas_call(
        body, name="att_prep_bwd", grid=(t // tm,),
        in_specs=[row(1536), row(512), row(512), row(512), full(g), full(gt), full(wq), full(wk)],
        out_specs=[row(1536), acc, acc],
        out_shape=[jax.ShapeDtypeStruct((t, 1536), F32), jax.ShapeDtypeStruct((8, 512), F32),
                   jax.ShapeDtypeStruct((8, 512), F32)],
        compiler_params=_params(("arbitrary",)),
    )(patt, dq, dk, dv, g, gt, wq, wk)


def _dn_scan_bwd(qkv, ba, do, sh, th, arow, dtb):
    t = qkv.shape[0]
    n = t // CHUNK
    c = CHUNK

    def body(qkv_ref, ba_ref, do_ref, sh_ref, th_ref, arow_ref, dtb_ref, dqkv_ref, dba_ref, ds_ref):
        @pl.when(pl.program_id(0) == 0)
        def _():
            ds_ref[...] = jnp.zeros_like(ds_ref)

        hs = range(DN_HEADS)
        ms, ri, ci = _chunk_common(qkv_ref[...], ba_ref[...], arow_ref[...], dtb_ref[...])
        lane = _iota((c, 128), 1)
        row = _iota((c, 1), 0)
        q, k, v = [m["q"] for m in ms], [m["k"] for m in ms], [m["v"] for m in ms]
        beta, decay = [m["beta"] for m in ms], [m["decay"] for m in ms]
        eg, egl, etail = [m["eg"] for m in ms], [m["egl"] for m in ms], [m["etail"] for m in ms]
        s, tinv = [sh_ref[0, h] for h in hs], [th_ref[0, h] for h in hs]
        d_o, d_s = [do_ref[:, h * DK:(h + 1) * DK] for h in hs], [ds_ref[h] for h in hs]
        kb = [k[h] * beta[h] for h in hs]
        vb = [v[h] * beta[h] for h in hs]
        kbg = [kb[h] * eg[h] for h in hs]
        amat = [jnp.where(ri > ci, _dot_nt(kb[h], k[h]) * decay[h], 0.0) for h in hs]
        attn = [jnp.where(ri >= ci, _dot_nt(q[h], k[h]) * decay[h], 0.0) for h in hs]
        u = [_hdot(tinv[h], vb[h]) for h in hs]
        w = [_hdot(tinv[h], kbg[h]) for h in hs]
        v_new = [u[h] - _dot(w[h], s[h]) for h in hs]
        q_dec = [q[h] * eg[h] for h in hs]
        k_tail = [k[h] * etail[h] for h in hs]

        d_vnew = [_dot_tn(attn[h], d_o[h]) + _dot(k_tail[h], d_s[h]) for h in hs]
        d_attn = [jnp.where(ri >= ci, _dot_nt(d_o[h], v_new[h]), 0.0) for h in hs]
        d_qdec = [_dot_nt(d_o[h], s[h]) for h in hs]
        for h in hs:
            ds_ref[h] = _dot_tn(q_dec[h], d_o[h]) + d_s[h] * egl[h] - _dot_tn(w[h], d_vnew[h])
        d_ktail = [_dot_nt(v_new[h], d_s[h]) for h in hs]
        d_gl = [jnp.sum(s[h] * d_s[h]) * egl[h] for h in hs]
        d_w = [-_dot_nt(d_vnew[h], s[h]) for h in hs]
        d_vb = [_hdot_tn(tinv[h], d_vnew[h]) for h in hs]
        d_kbg = [_hdot_tn(tinv[h], d_w[h]) for h in hs]
        d_a = [-jnp.where(ri > ci, _hdot_nt(d_vb[h], u[h]) + _hdot_nt(d_kbg[h], w[h]), 0.0) for h in hs]
        d_qk = [d_attn[h] * decay[h] for h in hs]
        d_kk = [d_a[h] * decay[h] for h in hs]
        d_kb = [_dot(d_kk[h], k[h]) + d_kbg[h] * eg[h] for h in hs]
        d_q = [_dot(d_qk[h], k[h]) + d_qdec[h] * eg[h] for h in hs]
        d_k = [_dot_tn(d_qk[h], q[h]) + _dot_tn(d_kk[h], kb[h]) + d_ktail[h] * etail[h] + d_kb[h] * beta[h] for h in hs]
        d_beta = [jnp.sum(d_kb[h] * k[h] + d_vb[h] * v[h], axis=1, keepdims=True) for h in hs]
        mm = [d_a[h] * amat[h] + d_attn[h] * attn[h] for h in hs]
        rows = jnp.zeros((c, c), F32)
        for h in hs:
            rows = rows + jnp.where(ri == h, jnp.sum(mm[h], axis=0, keepdims=True), 0.0)
        cols_t = jnp.concatenate([rows, jnp.zeros((c, c), F32)], axis=1).T[:c, :]
        d_gc_all = jnp.zeros((c, 128), F32)
        for h in hs:
            tail_term = jnp.sum(d_ktail[h] * k_tail[h], axis=1, keepdims=True)
            d_gc = (jnp.sum(mm[h], axis=1, keepdims=True) - _lane_col(cols_t, h)
                    + jnp.sum(d_qdec[h] * q_dec[h] + d_kbg[h] * kbg[h], axis=1, keepdims=True) - tail_term)
            d_gc = d_gc + jnp.where(row == c - 1, jnp.sum(tail_term) + d_gl[h], 0.0)
            d_gc_all = d_gc_all + jnp.where(lane == DN_HEADS + h, d_gc, 0.0)
        d_g_all = _hdot((ri <= ci).astype(F32), d_gc_all, "a")
        dba = jnp.zeros((c, 128), F32)
        for h in hs:
            d_g = _lane_col(d_g_all, DN_HEADS + h)
            d_braw = d_beta[h] * beta[h] * (1.0 - beta[h])
            d_araw = d_g * ms[h]["a_h"] * _sigmoid(ms[h]["a_raw"] + ms[h]["dt_h"])
            dba = dba + jnp.where(lane == h, d_braw, 0.0) + jnp.where(lane == DN_HEADS + h, d_araw, 0.0) \
                + jnp.where(lane == 2 * DN_HEADS + h, d_g * ms[h]["g"], 0.0)
            dqkv_ref[:, h * DK:(h + 1) * DK] = d_q[h]
            dqkv_ref[:, D_DN + h * DK:D_DN + (h + 1) * DK] = d_k[h]
            dqkv_ref[:, 2 * D_DN + h * DK:2 * D_DN + (h + 1) * DK] = d_vb[h] * beta[h]
        dba_ref[...] = dba

    rev = lambda w_: pl.BlockSpec((c, w_), lambda i: (n - 1 - i, 0))
    one = pl.BlockSpec((1, 128), lambda i: (0, 0))
    return pl.pallas_call(
        body, name="dn_scan_bwd", grid=(n,),
        in_specs=[rev(1536), rev(128), rev(512), pl.BlockSpec((1, DN_HEADS, DK, DK), lambda i: (n - 1 - i, 0, 0, 0)),
                  pl.BlockSpec((1, DN_HEADS, c, c), lambda i: (n - 1 - i, 0, 0, 0)), one, one],
        out_specs=[rev(1536), rev(128)],
        out_shape=[jax.ShapeDtypeStruct((t, 1536), F32), jax.ShapeDtypeStruct((t, 128), F32)],
        scratch_shapes=[pltpu.VMEM((DN_HEADS, DK, DK), F32)],
        compiler_params=_params(("arbitrary",)),
    )(qkv, ba, do, sh, th, arow, dtb)


def _dn_prep_bwd(pdn, cw, dact):
    t = pdn.shape[0]
    nchunk = t // CONV_ROWS

    def body(u_ref, w_ref, d_ref, du_ref, dw_ref, dy_ref):
        j = pl.program_id(0)
        dy_ref[t:t + 8, :] = jnp.zeros((8, 128), F32)
        dw = [jnp.zeros((1, 128), F32) for _ in range(4)]
        for c in range(nchunk):
            sl = slice(c * CONV_ROWS, (c + 1) * CONV_ROWS)
            taps, y = _conv_taps(u_ref, c, w_ref)
            a = _silu(y)
            dout = d_ref[sl, :]
            rs = lax.rsqrt(jnp.sum(a * a, axis=1, keepdims=True) + EPS)
            f = jnp.where(j < 8, rs, 1.0) * jnp.where(j < 4, DK ** -0.5, 1.0)
            corr = jnp.where(j < 8, f * rs * rs * jnp.sum(dout * a, axis=1, keepdims=True), 0.0)
            dy = (f * dout - corr * a) * _dsilu(y)
            dy_ref[sl, :] = dy
            for k_ in range(4):
                dw[3 - k_] = dw[3 - k_] + jnp.sum(taps[k_] * dy, axis=0, keepdims=True)
        for i in range(4):
            dw_ref[i:i + 1, :] = dw[i]
        for c in range(nchunk):
            r0 = c * CONV_ROWS
            ext = dy_ref[r0:r0 + CONV_ROWS + 8, :]
            du = ext[:CONV_ROWS, :] * w_ref[3:4, :]
            for k_ in (1, 2, 3):
                du = du + pltpu.roll(ext, CONV_ROWS + 8 - k_, 0)[:CONV_ROWS, :] * w_ref[3 - k_:4 - k_, :]
            du_ref[r0:r0 + CONV_ROWS, :] = du

    return pl.pallas_call(
        body, name="dn_prep_bwd", grid=(12,),
        in_specs=[pl.BlockSpec((t, 128), lambda j: (0, j)), pl.BlockSpec((4, 128), lambda j: (0, j)),
                  pl.BlockSpec((t, 128), lambda j: (0, j))],
        out_specs=[pl.BlockSpec((t, 128), lambda j: (0, j)), pl.BlockSpec((4, 128), lambda j: (0, j))],
        out_shape=[jax.ShapeDtypeStruct((t, 1536), F32), jax.ShapeDtypeStruct((4, 1536), F32)],
        scratch_shapes=[pltpu.VMEM((t + 8, 128), F32)],
        compiler_params=_params(("arbitrary",)),
    )(pdn, cw, dact)


def _inproj_bwd(x, nw, wt, dy, dpdn, dz, dpatt, dgate, dba):
    t = x.shape[0]
    tm = 256

    def body(x_ref, nw_ref, w_ref, dy_ref, a_ref, b_ref, c_ref, d_ref, e_ref, gx_ref, dnw_ref, cs_ref):
        @pl.when(pl.program_id(0) == 0)
        def _():
            dnw_ref[...] = jnp.zeros_like(dnw_ref)
            cs_ref[...] = jnp.zeros_like(cs_ref)

        dh = jnp.zeros((tm, D_MODEL), F32)
        for ref, lo, hi in ((a_ref, 0, 1536), (b_ref, 1536, 2048), (c_ref, 2048, 3584),
                            (d_ref, 3584, 4096), (e_ref, 4096, 4224)):
            dh = dh + jnp.dot(ref[...].astype(MXU), w_ref[lo:hi, :], preferred_element_type=F32)
        xv = x_ref[...]
        rstd = lax.rsqrt(jnp.mean(xv * xv, axis=-1, keepdims=True) + EPS)
        xh = xv * rstd
        gg = dh * nw_ref[...]
        gx_ref[...] = rstd * (gg - xh * jnp.mean(gg * xh, axis=-1, keepdims=True)) + dy_ref[...]
        dnw_ref[...] += jnp.broadcast_to(jnp.sum(dh * xh, axis=0, keepdims=True), (8, D_MODEL))
        cs_ref[...] += jnp.broadcast_to(jnp.sum(e_ref[...], axis=0, keepdims=True), (8, 128))

    row = lambda n: pl.BlockSpec((tm, n), lambda i: (i, 0))
    full = lambda a: pl.BlockSpec(a.shape, lambda i: (0,) * a.ndim)
    return pl.pallas_call(
        body, name="inproj_bwd", grid=(t // tm,),
        in_specs=[row(D_MODEL), full(nw), full(wt), row(D_MODEL), row(1536), row(512), row(1536), row(512), row(128)],
        out_specs=[row(D_MODEL), pl.BlockSpec((8, D_MODEL), lambda i: (0, 0)), pl.BlockSpec((8, 128), lambda i: (0, 0))],
        out_shape=[jax.ShapeDtypeStruct((t, D_MODEL), F32), jax.ShapeDtypeStruct((8, D_MODEL), F32),
                   jax.ShapeDtypeStruct((8, 128), F32)],
        compiler_params=_params(("arbitrary",)),
    )(x, nw, wt, dy, dpdn, dz, dpatt, dgate, dba)


def _adamw_sum(w, gs, m, v, name):
    r, c = w.shape
    nsum = gs.shape[0]
    tr = r if r <= 256 else 256
    c1 = 1.0 - ADAM_B1 ** ADAM_STEP
    c2 = 1.0 - ADAM_B2 ** ADAM_STEP

    def body(w_ref, g_ref, m_ref, v_ref, go_ref, d_ref, mo_ref, vo_ref):
        g = g_ref[0].astype(F32)
        for s in range(1, nsum):
            g = g + g_ref[s].astype(F32)
        mn = ADAM_B1 * m_ref[...] + (1.0 - ADAM_B1) * g
        vn = ADAM_B2 * v_ref[...] + (1.0 - ADAM_B2) * (g * g)
        go_ref[...] = g
        mo_ref[...] = mn
        vo_ref[...] = vn
        d_ref[...] = -ADAM_LR * ((mn / c1) / (jnp.sqrt(vn / c2) + ADAM_EPS) + ADAM_WD * w_ref[...])

    blk = pl.BlockSpec((tr, c), lambda i: (i, 0))
    return pl.pallas_call(
        body, name=name, grid=(r // tr,),
        in_specs=[blk, pl.BlockSpec((nsum, tr, c), lambda i: (0, i, 0)), blk, blk],
        out_specs=[blk] * 4, out_shape=[jax.ShapeDtypeStruct((r, c), F32)] * 4,
        compiler_params=_params(("arbitrary",)),
    )(w, gs, m, v)


def _local_step(x, target, norm_w, w_sect, conv_w, a_log, dt_bias, dn_norm_w, q_norm_w, k_norm_w, rel_bias, w_out):
    lane = np.arange(128)
    arow = jnp.zeros((1, 128), F32).at[0, DN_HEADS:2 * DN_HEADS].set(-jnp.exp(a_log[0]))
    dtb = jnp.zeros((1, 128), F32).at[0, DN_HEADS:2 * DN_HEADS].set(dt_bias[0])
    g_np, gt_np = _group_mats()
    g, gt = jnp.asarray(g_np), jnp.asarray(gt_np)
    bk = jnp.asarray(_bucket_tables())
    wq = jnp.tile(q_norm_w, (1, ATT_HEADS)) * (HD ** -0.5)
    wk = jnp.tile(k_norm_w, (1, ATT_HEADS))
    del lane

    ht, pdn, z, patt, gate, ba = _inproj(x, norm_w, w_sect)
    qkv_dn = _dn_prep(pdn, conv_w)
    oraw, ydn, sh, th = _dn_scan_fwd(qkv_dn, ba, z, arow, dtb, dn_norm_w)
    qkv_att = _att_prep(patt, g, gt, wq, wk)
    oatt, yatt, lse = _att_fwd(qkv_att, gate, rel_bias, bk)
    dy, mix_t, loss8 = _outproj_loss(x, ydn, yatt, w_out, target)

    do_dn, dz, do_att, dgate, dd, ddnw = _outproj_bwd(dy, w_out.T, oraw, z, dn_norm_w, oatt, gate, g, gt)
    d_wout = _grad_matmul(mix_t, dy, "dw_out")
    dq, dk, dv, drb = _att_bwd(qkv_att, do_att, lse, dd, rel_bias, bk)
    dpatt, dwq8, dwk8 = _att_prep_bwd(patt, dq, dk, dv, g, gt, wq, wk)
    dqkv_dn, dba = _dn_scan_bwd(qkv_dn, ba, do_dn, sh, th, arow, dtb)
    dpdn, d_conv = _dn_prep_bwd(pdn, conv_w, dqkv_dn)
    grad_x, dnw8, cs8 = _inproj_bwd(x, norm_w, w_sect.T, dy, dpdn, dz, dpatt, dgate, dba)
    dw_sections = [_grad_matmul(ht, d_, nm) for d_, nm in
                   ((dpdn, "dw_in_dn"), (dz, "dw_in_z"), (dpatt, "dw_in_att"), (dgate, "dw_in_gate"), (dba, "dw_in_ba"))]
    small = _pack_small_grads(dnw8, cs8, ddnw, dwq8, dwk8, drb)
    return dict(loss=loss8[0, 0], grad_x=grad_x, w_in_sections=dw_sections, conv_w=d_conv, w_out=d_wout, small=small)


SMALL_ROWS = 24
SMALL_AT = dict(a_log=(slice(8, 9), slice(0, 4)), dt_bias=(slice(9, 10), slice(0, 4)),
                dn_norm_w=(slice(10, 11), slice(0, 128)), q_norm_w=(slice(11, 12), slice(0, HD)),
                k_norm_w=(slice(12, 13), slice(0, HD)), rel_bias=(slice(16, 24), slice(0, N_BUCKETS)))
SMALL_NAMES = ("norm_w", "a_log", "dt_bias", "dn_norm_w", "q_norm_w", "k_norm_w", "rel_bias")


def _pack_small_grads(dnw8, cs8, ddnw8, dwq8, dwk8, drb):
    def body(dnw_ref, cs_ref, ddnw_ref, dwq_ref, dwk_ref, drb_ref, o_ref):
        lane = _iota((8, 128), 1)
        o_ref[...] = jnp.zeros_like(o_ref)
        for k in range(D_MODEL // 128):
            o_ref[k:k + 1, :] = dnw_ref[0:1, k * 128:(k + 1) * 128]
        cs = cs_ref[...]
        o_ref[8:9, :] = jnp.where(lane < DN_HEADS, pltpu.roll(cs, 128 - 2 * DN_HEADS, 1), 0.0)[0:1, :]
        o_ref[9:10, :] = jnp.where(lane < DN_HEADS, pltpu.roll(cs, 128 - DN_HEADS, 1), 0.0)[0:1, :]
        o_ref[10:11, :] = ddnw_ref[0:1, :]
        for row, ref, scale in ((11, dwq_ref, HD ** -0.5), (12, dwk_ref, 1.0)):
            acc = ref[:, 0:128] + ref[:, 128:256] + ref[:, 256:384] + ref[:, 384:512]
            acc = (acc + pltpu.roll(acc, HD, 1)) * scale
            o_ref[row:row + 1, :] = jnp.where(lane < HD, acc, 0.0)[0:1, :]
        o_ref[16:24, :] = drb_ref[...]

    return pl.pallas_call(body, name="pack_small_grads", out_shape=jax.ShapeDtypeStruct((SMALL_ROWS, 128), F32),
                          )(dnw8, cs8, ddnw8, dwq8, dwk8, drb)


def _adam_math(w, g, m, v):
    c1 = 1.0 - ADAM_B1 ** ADAM_STEP
    c2 = 1.0 - ADAM_B2 ** ADAM_STEP
    mn = ADAM_B1 * m + (1.0 - ADAM_B1) * g
    vn = ADAM_B2 * v + (1.0 - ADAM_B2) * (g * g)
    return -ADAM_LR * ((mn / c1) / (jnp.sqrt(vn / c2) + ADAM_EPS) + ADAM_WD * w), mn, vn


def _adamw_small(gs, ws, ms, vs):
    n = len(SMALL_NAMES)

    def body(g_ref, *refs):
        w_refs, m_refs, v_refs = refs[:n], refs[n:2 * n], refs[2 * n:3 * n]
        outs = refs[3 * n:]

        def one(i, rows, lanes, at):
            g = g_ref[0, rows, lanes]
            for s in range(1, gs.shape[0]):
                g = g + g_ref[s, rows, lanes]
            d, mn, vn = _adam_math(w_refs[i][at], g, m_refs[i][at], v_refs[i][at])
            for kind, val in enumerate((g, d, mn, vn)):
                outs[kind * n + i][at] = val

        for k in range(D_MODEL // 128):
            one(0, slice(k, k + 1), slice(0, 128), (slice(0, 1), slice(k * 128, (k + 1) * 128)))
        for i, nm in enumerate(SMALL_NAMES[1:], start=1):
            rows, lanes = SMALL_AT[nm]
            one(i, rows, lanes, (slice(None), slice(None)))

    shapes = [jax.ShapeDtypeStruct(w.shape, F32) for w in ws]
    res = pl.pallas_call(body, name="adamw_small", out_shape=shapes * 4)(gs, *ws, *ms, *vs)
    return [res[k * n:(k + 1) * n] for k in range(4)]


def kernel(x, norm_w, w_in, conv_w, a_log, dt_bias, dn_norm_w, q_norm_w, k_norm_w, rel_bias, w_out, loss_target, m_norm_w, m_w_in, m_conv_w, m_a_log, m_dt_bias, m_dn_norm_w, m_q_norm_w, m_k_norm_w, m_rel_bias, m_w_out, v_norm_w, v_w_in, v_conv_w, v_a_log, v_dt_bias, v_dn_norm_w, v_q_norm_w, v_k_norm_w, v_rel_bias, v_w_out):
    ncol = w_in.shape[2]
    d_in = N_DEV * ncol
    win8, wout8, conv8 = _gather_weights([w_in[0].astype(MXU), w_out[0].astype(MXU), conv_w[0]])
    w_full = win8.transpose(1, 0, 2).reshape(D_MODEL, d_in)
    w_sect = jnp.concatenate([w_full[:, :2048], w_full[:, 2056:], w_full[:, 2048:2056],
                              jnp.zeros((D_MODEL, W_COLS - d_in), MXU)], axis=1)
    wout_full = wout8.reshape(D_MODEL, D_MODEL)
    conv_full = conv8.transpose(1, 0, 2).reshape(4, 3 * D_DN)

    gr = _local_step(x[0], loss_target[0], norm_w, w_sect, conv_full, a_log, dt_bias, dn_norm_w, q_norm_w, k_norm_w,
                     rel_bias, wout_full)

    s_dn, s_z, s_att, s_gate, s_ba = gr["w_in_sections"]
    dw_full = jnp.concatenate([s_dn, s_z, s_ba[:, :2 * DN_HEADS], s_att, s_gate], axis=1)
    slabs = [dw_full.reshape(D_MODEL, 4, 2, ncol).transpose(2, 1, 0, 3),
             gr["w_out"].reshape(4, 2, D_MODEL // N_DEV, D_MODEL).transpose(1, 0, 2, 3),
             gr["conv_w"].reshape(4, 4, 2, 3 * D_DN // N_DEV).transpose(2, 1, 0, 3),
             jnp.broadcast_to(gr["small"][None, None], (2, 4, SMALL_ROWS, 128))]
    core = lax.axis_index("c").astype(jnp.int32).reshape(1)
    from_sibling = _swap_siblings(slabs)
    wires = (GRAD_WIRE, GRAD_WIRE, F32, F32)
    partial = [_chip_sum(slabs[i], from_sibling[i], core, wires[i], "chip_sum_%d" % i) for i in range(4)]
    r_win, r_wout, r_conv, r_small = _swap_chips(partial)

    g_win, d_win, m_win, v_win = _adamw_sum(w_in[0], r_win, m_w_in[0], v_w_in[0], "adamw_w_in")
    g_wout, d_wout, m_wout, v_wout = _adamw_sum(w_out[0], r_wout, m_w_out[0], v_w_out[0], "adamw_w_out")
    g_conv, d_conv, m_conv, v_conv = _adamw_sum(conv_w[0], r_conv, m_conv_w[0], v_conv_w[0], "adamw_conv_w")
    small = _adamw_small(r_small,
                         (norm_w, a_log, dt_bias, dn_norm_w, q_norm_w, k_norm_w, rel_bias),
                         (m_norm_w, m_a_log, m_dt_bias, m_dn_norm_w, m_q_norm_w, m_k_norm_w, m_rel_bias),
                         (v_norm_w, v_a_log, v_dt_bias, v_dn_norm_w, v_q_norm_w, v_k_norm_w, v_rel_bias))

    loss = lax.psum(gr["loss"], AXES)
    names = ("norm_w", "w_in", "conv_w", "a_log", "dt_bias", "dn_norm_w", "q_norm_w", "k_norm_w", "rel_bias", "w_out")
    big = dict(w_in=(g_win, d_win, m_win, v_win), conv_w=(g_conv, d_conv, m_conv, v_conv),
               w_out=(g_wout, d_wout, m_wout, v_wout))
    outs = [loss, gr["grad_x"][None]]
    for kind in range(4):
        for nm in names:
            outs.append(big[nm][kind][None] if nm in big else small[kind][SMALL_NAMES.index(nm)])
    return tuple(outs)
```

```python
import functools
import math

import numpy as np
import jax
import jax.numpy as jnp
from jax import lax
from jax.experimental import pallas as pl
from jax.experimental.pallas import tpu as pltpu

F32 = jnp.float32
MXU = jnp.bfloat16
GRAD_WIRE = jnp.bfloat16
HI = lax.Precision.HIGHEST

D_MODEL = 1024
D_DN = 512
DN_HEADS = 4
DK = 128
CHUNK = 64
D_ATT = 512
ATT_HEADS = 8
HD = 64
PATTERNS = ((128, 1), (512, 4), (2048, 16))
BLK = 128
N_BUCKETS = 32
MAX_DISTANCE = 2048
EPS = 1e-6
W_COLS = 4224
N_DEV = 8
AXES = ("x", "y", "c")

ADAM_LR = 0.001
ADAM_B1 = 0.9
ADAM_B2 = 0.999
ADAM_EPS = 1e-08
ADAM_WD = 0.01
ADAM_STEP = 10

VMEM_LIMIT = 56 * 1024 * 1024
NEG = -1e30


def _dot(a, b):
    return jnp.dot(a.astype(MXU), b.astype(MXU), preferred_element_type=F32)


def _dot_nt(a, b):
    return lax.dot_general(a.astype(MXU), b.astype(MXU), (((1,), (1,)), ((), ())), preferred_element_type=F32)


def _dot_tn(a, b):
    return lax.dot_general(a.astype(MXU), b.astype(MXU), (((0,), (0,)), ((), ())), preferred_element_type=F32)


def _split(a):
    hi = a.astype(jnp.bfloat16)
    return hi, (a - hi.astype(F32)).astype(jnp.bfloat16)


def _dot_split(a, b, dims, exact):
    dg = lambda u, v: lax.dot_general(u, v, (dims, ((), ())), preferred_element_type=F32)
    if exact == "b":
        ah, al = _split(a)
        bh = b.astype(jnp.bfloat16)
        return dg(ah, bh) + dg(al, bh)
    if exact == "a":
        bh, bm = _split(b)
        bl = (b - bh.astype(F32) - bm.astype(F32)).astype(jnp.bfloat16)
        ah = a.astype(jnp.bfloat16)
        return dg(ah, bh) + (dg(ah, bm) + dg(ah, bl))
    ah, al = _split(a)
    bh, bl = _split(b)
    return dg(ah, bh) + (dg(ah, bl) + dg(al, bh))


def _hdot(a, b, exact=None):
    return _dot_split(a, b, ((1,), (0,)), exact)


def _hdot_nt(a, b, exact=None):
    return _dot_split(a, b, ((1,), (1,)), exact)


def _hdot_tn(a, b, exact=None):
    return _dot_split(a, b, ((0,), (0,)), exact)


def _sigmoid(x):
    return 1.0 / (1.0 + jnp.exp(-x))


def _silu(x):
    return x * _sigmoid(x)


def _dsilu(x):
    s = _sigmoid(x)
    return s * (1.0 + x * (1.0 - s))


def _softplus(x):
    return jnp.maximum(x, 0.0) + jnp.log(1.0 + jnp.exp(-jnp.abs(x)))


def _iota(shape, dim):
    return lax.broadcasted_iota(jnp.int32, shape, dim)


def _lane_col(x, k):
    return jnp.sum(jnp.where(_iota(x.shape, 1) == k, x, 0.0), axis=1, keepdims=True)


def _params(sem=None):
    return pltpu.CompilerParams(dimension_semantics=sem, vmem_limit_bytes=VMEM_LIMIT)


def _t5_bucket(dist):
    max_exact = N_BUCKETS // 2
    d = np.maximum(dist, 1).astype(np.float64)
    large = max_exact + (np.log(d / max_exact) / math.log(MAX_DISTANCE / max_exact)
                         * (N_BUCKETS - max_exact)).astype(np.int32)
    large = np.minimum(large, N_BUCKETS - 1)
    return np.where(dist < max_exact, dist, large).astype(np.int32)


def _bucket_tables():
    qi = np.arange(BLK)[:, None]
    kj = np.arange(2 * BLK)[None, :]
    step = qi - kj + BLK
    band = (step >= 0) & (step <= BLK)
    out = []
    for _, r in PATTERNS:
        b = _t5_bucket(np.clip(step, 0, None) * r)
        out.append(np.where(band, b, -1))
    return np.stack(out).astype(np.int32)


def _group_mats():
    g = np.zeros((D_ATT, 128), np.float32)
    for h in range(ATT_HEADS):
        g[h * HD:(h + 1) * HD, h] = 1.0
    return g, np.ascontiguousarray(g.T)


CHIP_FLIPS = ((1, 0), (0, 1), (1, 1))
ANY_SPEC = pl.BlockSpec(memory_space=pl.ANY)
MESH_ID = pl.DeviceIdType.MESH


def _other_chips():
    x, y = lax.axis_index("x"), lax.axis_index("y")
    return [((1 - x if fx else x), (1 - y if fy else y)) for fx, fy in CHIP_FLIPS]


def _gather_weights(arrs):
    n = len(arrs)

    def body(*refs):
        ins, outs = refs[:n], refs[n:2 * n]
        send, recv, loc = refs[2 * n:]
        x, y, c = (lax.axis_index(a) for a in AXES)
        sib = (x, y, 1 - c)
        chips = _other_chips()
        lin = lambda px, py, pc: 4 * px + 2 * py + pc

        def copy(a, k, block, to, src=None):
            slot = outs[a].at[lin(*block)]
            return pltpu.make_async_remote_copy(src_ref=slot if src is None else src, dst_ref=slot,
                                                send_sem=send.at[a, k], recv_sem=recv.at[a, k],
                                                device_id=to, device_id_type=MESH_ID)

        started = []
        for a in range(n):
            mine = pltpu.make_async_copy(ins[a], outs[a].at[lin(x, y, c)], loc.at[a])
            mine.start()
            started.append(mine)
        firsts = []
        for a in range(n):
            firsts.append(copy(a, 0, (x, y, c), sib, src=ins[a]))
            firsts += [copy(a, 1 + j, (x, y, c), (*chip, c), src=ins[a]) for j, chip in enumerate(chips)]
        for cp in firsts:
            cp.start()
        passed = []
        for j, chip in enumerate(chips):
            for a in range(n):
                copy(a, 1 + j, (*chip, c), (x, y, c)).wait_recv()
                fw = copy(a, 4 + j, (*chip, c), sib)
                fw.start()
                passed.append(fw)
        for a in range(n):
            copy(a, 0, sib, (x, y, c)).wait_recv()
            for j, chip in enumerate(chips):
                copy(a, 4 + j, (*chip, 1 - c), (x, y, c)).wait_recv()
        for cp in firsts + passed:
            cp.wait_send()
        for mine in started:
            mine.wait()

    return pl.pallas_call(
        body, name="gather_weights", out_shape=[jax.ShapeDtypeStruct((N_DEV,) + a.shape, a.dtype) for a in arrs],
        in_specs=[ANY_SPEC] * n, out_specs=[ANY_SPEC] * n,
        scratch_shapes=[pltpu.SemaphoreType.DMA((n, 7)), pltpu.SemaphoreType.DMA((n, 7)), pltpu.SemaphoreType.DMA((n,))],
    )(*arrs)


def _swap_siblings(arrs):
    n = len(arrs)

    def body(*refs):
        ins, outs = refs[:n], refs[n:2 * n]
        send, recv = refs[2 * n:]
        x, y, c = (lax.axis_index(a) for a in AXES)
        cps = [pltpu.make_async_remote_copy(src_ref=ins[a].at[1 - c], dst_ref=outs[a], send_sem=send.at[a],
                                            recv_sem=recv.at[a], device_id=(x, y, 1 - c), device_id_type=MESH_ID)
               for a in range(n)]
        for cp in cps:
            cp.start()
        for cp in cps:
            cp.wait()

    return pl.pallas_call(
        body, name="swap_siblings", out_shape=[jax.ShapeDtypeStruct(a.shape[1:], a.dtype) for a in arrs],
        in_specs=[ANY_SPEC] * n, out_specs=[ANY_SPEC] * n,
        scratch_shapes=[pltpu.SemaphoreType.DMA((n,)), pltpu.SemaphoreType.DMA((n,))],
    )(*arrs)


def _chip_sum(mine2, theirs, core, wire, name):
    _, nchip, r, cdim = mine2.shape
    tr = r if r <= 256 else 256

    def body(core_ref, a_ref, b_ref, o_ref):
        del core_ref
        o_ref[...] = (a_ref[...].astype(F32) + b_ref[...].astype(F32)).astype(wire)

    grid_spec = pltpu.PrefetchScalarGridSpec(
        num_scalar_prefetch=1, grid=(nchip, r // tr),
        in_specs=[pl.BlockSpec((None, None, tr, cdim), lambda j, i, cr: (cr[0], j, i, 0)),
                  pl.BlockSpec((None, tr, cdim), lambda j, i, cr: (j, i, 0))],
        out_specs=pl.BlockSpec((None, tr, cdim), lambda j, i, cr: (j, i, 0)))
    return pl.pallas_call(
        body, name=name, grid_spec=grid_spec, out_shape=jax.ShapeDtypeStruct((nchip, r, cdim), wire),
        compiler_params=_params(("arbitrary", "arbitrary")),
    )(core, mine2, theirs)


def _swap_chips(arrs):
    n = len(arrs)

    def body(*refs):
        ins, outs = refs[:n], refs[n:2 * n]
        send, recv, loc = refs[2 * n:]
        x, y, c = (lax.axis_index(a) for a in AXES)
        me = 2 * x + y
        chips = _other_chips()
        locs, sends, recvs = [], [], []
        for a in range(n):
            lc = pltpu.make_async_copy(ins[a].at[me], outs[a].at[me], loc.at[a])
            lc.start()
            locs.append(lc)
            for j, (px, py) in enumerate(chips):
                them = 2 * px + py
                cp = pltpu.make_async_remote_copy(src_ref=ins[a].at[them], dst_ref=outs[a].at[me], send_sem=send.at[a, j],
                                                  recv_sem=recv.at[a, j], device_id=(px, py, c), device_id_type=MESH_ID)
                cp.start()
                sends.append(cp)
                recvs.append(pltpu.make_async_remote_copy(src_ref=ins[a].at[them], dst_ref=outs[a].at[them],
                                                          send_sem=send.at[a, j], recv_sem=recv.at[a, j],
                                                          device_id=(px, py, c), device_id_type=MESH_ID))
        for cp in recvs:
            cp.wait_recv()
        for cp in sends:
            cp.wait_send()
        for lc in locs:
            lc.wait()

    return pl.pallas_call(
        body, name="swap_chips", out_shape=[jax.ShapeDtypeStruct(a.shape, a.dtype) for a in arrs],
        in_specs=[ANY_SPEC] * n, out_specs=[ANY_SPEC] * n,
        scratch_shapes=[pltpu.SemaphoreType.DMA((n, 3)), pltpu.SemaphoreType.DMA((n, 3)), pltpu.SemaphoreType.DMA((n,))],
    )(*arrs)


W_PARTS = ((0, 0, 2048), (2048, 4096, 8), (2056, 2048, 2048))
SHARD_COLS = 513


def _pieces(lo, hi, parts):
    out = []
    for ref_start, tgt_start, width in parts:
        a, b = max(lo, ref_start), min(hi, ref_start + width)
        if a < b:
            out.append((a - lo, tgt_start + a - ref_start, b - a))
    return out


def _build_w(win8):
    tr = 256

    def body(in_ref, w_ref, wt_ref):
        w_ref[:, 4096:W_COLS] = jnp.zeros((tr, W_COLS - 4096), MXU)
        for p in range(N_DEV):
            for src, dst, width in _pieces(p * SHARD_COLS, (p + 1) * SHARD_COLS, W_PARTS):
                w_ref[:, dst:dst + width] = in_ref[p, :, src:src + width]
        for k in range(W_COLS // 128):
            wt_ref[k * 128:(k + 1) * 128, :] = w_ref[:, k * 128:(k + 1) * 128].astype(F32).T.astype(MXU)

    return pl.pallas_call(
        body, name="build_w", grid=(D_MODEL // tr,),
        in_specs=[pl.BlockSpec((N_DEV, tr, SHARD_COLS), lambda i: (0, i, 0))],
        out_specs=[pl.BlockSpec((tr, W_COLS), lambda i: (i, 0)), pl.BlockSpec((W_COLS, tr), lambda i: (0, i))],
        out_shape=[jax.ShapeDtypeStruct((D_MODEL, W_COLS), MXU), jax.ShapeDtypeStruct((W_COLS, D_MODEL), MXU)],
        compiler_params=_params(("arbitrary",)),
    )(win8)


def _build_slabs(s_dn, s_z, s_att, s_gate, s_ba):
    tr = 256
    secs = (s_dn, s_z, s_ba, s_att, s_gate)
    parts = ((0, 0, 1536), (1536, 1, 512), (2048, 2, 8), (2056, 3, 1536), (3592, 4, 512))

    def body(dn_ref, z_ref, ba_ref, att_ref, gate_ref, o_ref):
        refs = (dn_ref, z_ref, ba_ref, att_ref, gate_ref)
        for p in range(N_DEV):
            lo, hi = p * SHARD_COLS, (p + 1) * SHARD_COLS
            for ref_start, idx, width in parts:
                a, b = max(lo, ref_start), min(hi, ref_start + width)
                if a < b:
                    o_ref[p % 2, p // 2, :, a - lo:b - lo] = refs[idx][:, a - ref_start:b - ref_start]

    return pl.pallas_call(
        body, name="build_slabs", grid=(D_MODEL // tr,),
        in_specs=[pl.BlockSpec((tr, s.shape[1]), lambda i: (i, 0)) for s in secs],
        out_specs=pl.BlockSpec((2, 4, tr, SHARD_COLS), lambda i: (0, 0, i, 0)),
        out_shape=jax.ShapeDtypeStruct((2, 4, D_MODEL, SHARD_COLS), s_dn.dtype),
        compiler_params=_params(("arbitrary",)),
    )(*secs)


def _inproj(x, nw, w):
    t = x.shape[0]
    tm = 256

    def body(x_ref, nw_ref, w_ref, ht_ref, pdn_ref, z_ref, patt_ref, gate_ref, ba_ref):
        xv = x_ref[...]
        rstd = lax.rsqrt(jnp.mean(xv * xv, axis=-1, keepdims=True) + EPS)
        hf = xv * rstd * nw_ref[...]
        h = hf.astype(MXU)
        ht_ref[...] = hf.T.astype(MXU)
        for ref, lo, hi in ((pdn_ref, 0, 1536), (z_ref, 1536, 2048), (patt_ref, 2048, 3584),
                            (gate_ref, 3584, 4096), (ba_ref, 4096, 4224)):
            ref[...] = jnp.dot(h, w_ref[:, lo:hi], preferred_element_type=F32)

    row = lambda n: pl.BlockSpec((tm, n), lambda i: (i, 0))
    full = lambda a: pl.BlockSpec(a.shape, lambda i: (0,) * a.ndim)
    return pl.pallas_call(
        body, name="inproj", grid=(t // tm,),
        in_specs=[row(D_MODEL), full(nw), full(w)],
        out_specs=[pl.BlockSpec((D_MODEL, tm), lambda i: (0, i)), row(1536), row(512), row(1536), row(512), row(128)],
        out_shape=[jax.ShapeDtypeStruct((D_MODEL, t), MXU)] +
                  [jax.ShapeDtypeStruct((t, n), F32) for n in (1536, 512, 1536, 512, 128)],
        compiler_params=_params(("arbitrary",)),
    )(x, nw, w)


CONV_ROWS = 512


def _conv_taps(u_ref, c, w_ref):
    r0 = c * CONV_ROWS
    if c == 0:
        ext = jnp.concatenate([jnp.zeros((8, 128), F32), u_ref[0:CONV_ROWS, :]], axis=0)
    else:
        ext = u_ref[r0 - 8:r0 + CONV_ROWS, :]
    taps = [ext[8:, :]] + [pltpu.roll(ext, k, 0)[8:, :] for k in (1, 2, 3)]
    y = taps[0] * w_ref[3:4, :]
    for k in (1, 2, 3):
        y = y + taps[k] * w_ref[3 - k:4 - k, :]
    return taps, y


def _dn_prep(pdn, cw):
    t = pdn.shape[0]

    def body(u_ref, w_ref, o_ref):
        j = pl.program_id(0)
        for c in range(t // CONV_ROWS):
            _, y = _conv_taps(u_ref, c, w_ref)
            a = _silu(y)
            ssq = jnp.sum(a * a, axis=1, keepdims=True)
            f = jnp.where(j < 8, lax.rsqrt(ssq + EPS), 1.0) * jnp.where(j < 4, DK ** -0.5, 1.0)
            o_ref[c * CONV_ROWS:(c + 1) * CONV_ROWS, :] = a * f

    return pl.pallas_call(
        body, name="dn_prep", grid=(12,),
        in_specs=[pl.BlockSpec((t, 128), lambda j: (0, j)), pl.BlockSpec((4, 128), lambda j: (0, j))],
        out_specs=pl.BlockSpec((t, 128), lambda j: (0, j)),
        out_shape=jax.ShapeDtypeStruct((t, 1536), F32),
        compiler_params=_params(("arbitrary",)),
    )(pdn, cw)


def _chunk_common(qkv, ba, arow, dtb):
    c = CHUNK
    ri, ci = _iota((c, c), 0), _iota((c, c), 1)
    lane = _iota((c, 128), 1)
    g_all = jnp.where((lane >= DN_HEADS) & (lane < 2 * DN_HEADS), arow * _softplus(ba + dtb), 0.0)
    gc_all = _hdot((ri >= ci).astype(F32), g_all, "a")
    gc_t = gc_all.T
    beta_all = _sigmoid(ba)
    out = []
    for h in range(DN_HEADS):
        gc = _lane_col(gc_all, DN_HEADS + h)
        gcr = gc_t[DN_HEADS + h:DN_HEADS + h + 1, :]
        gl = gc[c - 1:c, :]
        out.append(dict(
            q=qkv[:, h * DK:(h + 1) * DK], k=qkv[:, D_DN + h * DK:D_DN + (h + 1) * DK],
            v=qkv[:, 2 * D_DN + h * DK:2 * D_DN + (h + 1) * DK],
            beta=_lane_col(beta_all, h), g=_lane_col(g_all, DN_HEADS + h),
            a_raw=_lane_col(ba, DN_HEADS + h), a_h=_lane_col(arow, DN_HEADS + h), dt_h=_lane_col(dtb, DN_HEADS + h),
            decay=jnp.exp(jnp.where(ri >= ci, gc - gcr, NEG)), eg=jnp.exp(gc), egl=jnp.exp(gl), etail=jnp.exp(gl - gc)))
    return out, ri, ci


def _dn_scan_fwd(qkv, ba, z, arow, dtb, dnw):
    t = qkv.shape[0]
    n = t // CHUNK
    c = CHUNK
    hs = range(DN_HEADS)

    def body(qkv_ref, ba_ref, z_ref, arow_ref, dtb_ref, dnw_ref, o_ref, y_ref, sh_ref, th_ref, s_ref):
        @pl.when(pl.program_id(0) == 0)
        def _():
            s_ref[...] = jnp.zeros_like(s_ref)

        ms, ri, ci = _chunk_common(qkv_ref[...], ba_ref[...], arow_ref[...], dtb_ref[...])
        kb = [m["k"] * m["beta"] for m in ms]
        amat = [jnp.where(ri > ci, _dot_nt(kb[h], ms[h]["k"]) * ms[h]["decay"], 0.0) for h in hs]
        attn = [jnp.where(ri >= ci, _dot_nt(m["q"], m["k"]) * m["decay"], 0.0) for m in ms]
        eye = (ri == ci).astype(F32)
        tinv = [eye - a for a in amat]
        pw = amat
        for _ in range(5):
            pw = [_hdot(p, p) for p in pw]
            tinv = [tv + _hdot(tv, p) for tv, p in zip(tinv, pw)]
        u = [_hdot(tinv[h], ms[h]["v"] * ms[h]["beta"]) for h in hs]
        w = [_hdot(tinv[h], kb[h] * ms[h]["eg"]) for h in hs]
        s = [s_ref[h] for h in hs]
        v_new = [u[h] - _dot(w[h], s[h]) for h in hs]
        o = [_dot(ms[h]["q"] * ms[h]["eg"], s[h]) + _dot(attn[h], v_new[h]) for h in hs]
        for h in hs:
            s_ref[h] = s[h] * ms[h]["egl"] + _dot_tn(ms[h]["k"] * ms[h]["etail"], v_new[h])
            sh_ref[0, h] = s[h]
            th_ref[0, h] = tinv[h]
            o_ref[:, h * DK:(h + 1) * DK] = o[h]
            rs = lax.rsqrt(jnp.mean(o[h] * o[h], axis=1, keepdims=True) + EPS)
            y_ref[:, h * DK:(h + 1) * DK] = o[h] * rs * dnw_ref[...] * _silu(z_ref[:, h * DK:(h + 1) * DK])

    row = lambda w_: pl.BlockSpec((c, w_), lambda i: (i, 0))
    one = pl.BlockSpec((1, 128), lambda i: (0, 0))
    return pl.pallas_call(
        body, name="dn_scan_fwd", grid=(n,),
        in_specs=[row(1536), row(128), row(512), one, one, one],
        out_specs=[row(512), row(512), pl.BlockSpec((1, DN_HEADS, DK, DK), lambda i: (i, 0, 0, 0)),
                   pl.BlockSpec((1, DN_HEADS, c, c), lambda i: (i, 0, 0, 0))],
        out_shape=[jax.ShapeDtypeStruct((t, 512), F32), jax.ShapeDtypeStruct((t, 512), F32),
                   jax.ShapeDtypeStruct((n, DN_HEADS, DK, DK), F32), jax.ShapeDtypeStruct((n, DN_HEADS, c, c), F32)],
        scratch_shapes=[pltpu.VMEM((DN_HEADS, DK, DK), F32)],
        compiler_params=_params(("arbitrary",)),
    )(qkv, ba, z, arow, dtb, dnw)


def _att_prep(patt, g, gt, wq, wk):
    t = patt.shape[0]
    tm = 512

    def body(p_ref, g_ref, gt_ref, wq_ref, wk_ref, o_ref):
        for lo, w_ref in ((0, wq_ref), (512, wk_ref)):
            xv = p_ref[:, lo:lo + 512]
            rstd = lax.rsqrt(_hdot(xv * xv, g_ref[...], "b") * (1.0 / HD) + EPS)
            o_ref[:, lo:lo + 512] = xv * _hdot(rstd, gt_ref[...], "b") * w_ref[...]
        o_ref[:, 1024:1536] = p_ref[:, 1024:1536]

    full = lambda a: pl.BlockSpec(a.shape, lambda i: (0,) * a.ndim)
    return pl.pallas_call(
        body, name="att_prep", grid=(t // tm,),
        in_specs=[pl.BlockSpec((tm, 1536), lambda i: (i, 0)), full(g), full(gt), full(wq), full(wk)],
        out_specs=pl.BlockSpec((tm, 1536), lambda i: (i, 0)),
        out_shape=jax.ShapeDtypeStruct((t, 1536), F32),
        compiler_params=_params(("arbitrary",)),
    )(patt, g, gt, wq, wk)


def _bias_tables(rb_ref, bk_ref, bias_ref, pair):
    for p in range(len(PATTERNS)):
        bk = bk_ref[p]
        for hh in range(2):
            head = 2 * pair + hh
            bm = jnp.full((BLK, 2 * BLK), NEG, F32)
            for b in range(N_BUCKETS):
                bm = jnp.where(bk == b, rb_ref[head, b], bm)
            bias_ref[p, hh * BLK:(hh + 1) * BLK, :] = bm


def _stack_heads(xb, h0):
    return jnp.concatenate([jnp.where(h0, xb, 0.0), jnp.where(h0, 0.0, xb)], axis=0).astype(MXU)


def _block_rows(t, r, n):
    per_class = (t // r) // BLK
    res = n // per_class
    j = n % per_class
    start = res + BLK * r * j
    pstart = res + BLK * r * jnp.maximum(j - 1, 0)
    if r == 1:
        return pl.ds(pl.multiple_of(start, BLK), BLK), pl.ds(pl.multiple_of(pstart, BLK), BLK), j
    return pl.ds(start, BLK, stride=r), pl.ds(pstart, BLK, stride=r), j


def _att_fwd(qkv, gate, rb, bk):
    t = qkv.shape[0]
    rows = 512

    def body(rb_ref, bk_ref, q_ref, k_ref, v_ref, g_ref, o_ref, y_ref, lse_ref,
             o0_ref, o1_ref, o2_ref, l0_ref, l1_ref, l2_ref, bias_ref):
        pair = pl.program_id(0)
        _bias_tables(rb_ref, bk_ref, bias_ref, pair)
        h0 = _iota((BLK, 128), 1) < HD
        prev_cols = _iota((2 * BLK, 2 * BLK), 1) < BLK
        op_refs, lp_refs = (o0_ref, o1_ref, o2_ref), (l0_ref, l1_ref, l2_ref)

        for p, (_, r) in enumerate(PATTERNS):
            def blk(n, carry, p=p, r=r):
                cur, prev, j = _block_rows(t, r, n)
                q2 = _stack_heads(q_ref[cur, :], h0)
                k2 = jnp.concatenate([k_ref[prev, :], k_ref[cur, :]], axis=0).astype(MXU)
                v2 = jnp.concatenate([v_ref[prev, :], v_ref[cur, :]], axis=0).astype(MXU)
                s = _dot_nt(q2, k2) + bias_ref[p] + jnp.where(prev_cols & (j == 0), NEG, 0.0)
                m = jnp.max(s, axis=1, keepdims=True)
                e = jnp.exp(s - m)
                l = jnp.sum(e, axis=1, keepdims=True)
                pv = _dot(e, v2) / l
                lse = m + jnp.log(l)
                op_refs[p][cur, :] = jnp.where(h0, pv[:BLK], pv[BLK:])
                lp_refs[p][cur, :] = jnp.where(h0, lse[:BLK], lse[BLK:])
                return carry

            lax.fori_loop(0, t // BLK, blk, 0, unroll=8)

        for c in range(t // rows):
            sl = slice(c * rows, (c + 1) * rows)
            ls = [ref[sl, :] for ref in lp_refs]
            mx = jnp.maximum(jnp.maximum(ls[0], ls[1]), ls[2])
            ws = [jnp.exp(v_ - mx) for v_ in ls]
            den = ws[0] + ws[1] + ws[2]
            o = (ws[0] * o0_ref[sl, :] + ws[1] * o1_ref[sl, :] + ws[2] * o2_ref[sl, :]) / den
            o_ref[sl, :] = o
            y_ref[sl, :] = o * _silu(g_ref[sl, :])
            lse_ref[sl, :] = mx + jnp.log(den)

    col = lambda off: pl.BlockSpec((t, 128), lambda i, off=off: (0, off + i))
    return pl.pallas_call(
        body, name="att_fwd", grid=(ATT_HEADS // 2,),
        in_specs=[pl.BlockSpec(memory_space=pltpu.SMEM), pl.BlockSpec(bk.shape, lambda i: (0, 0, 0)),
                  col(0), col(4), col(8), col(0)],
        out_specs=[col(0), col(0), col(0)],
        out_shape=[jax.ShapeDtypeStruct((t, 512), F32)] * 3,
        scratch_shapes=[pltpu.VMEM((t, 128), F32)] * 6 + [pltpu.VMEM((len(PATTERNS), 2 * BLK, 2 * BLK), F32)],
        compiler_params=_params(("arbitrary",)),
    )(rb, bk, qkv, qkv, qkv, gate)


def _outproj_loss(x, ydn, yatt, wout, target):
    t = x.shape[0]
    tm = 512

    def body(x_ref, a_ref, b_ref, w_ref, t_ref, dy_ref, mix_ref, loss_ref):
        @pl.when(pl.program_id(0) == 0)
        def _():
            loss_ref[...] = jnp.zeros_like(loss_ref)

        mixf = jnp.concatenate([a_ref[...], b_ref[...]], axis=1)
        mix_ref[...] = mixf.T.astype(MXU)
        err = x_ref[...] + jnp.dot(mixf.astype(MXU), w_ref[...], preferred_element_type=F32) - t_ref[...]
        dy_ref[...] = err * (1.0 / D_MODEL)
        loss_ref[...] += jnp.sum(err * err) * (0.5 / D_MODEL)

    row = lambda n: pl.BlockSpec((tm, n), lambda i: (i, 0))
    return pl.pallas_call(
        body, name="outproj_loss", grid=(t // tm,),
        in_specs=[row(D_MODEL), row(512), row(512), pl.BlockSpec(wout.shape, lambda i: (0, 0)), row(D_MODEL)],
        out_specs=[row(D_MODEL), pl.BlockSpec((D_MODEL, tm), lambda i: (0, i)), pl.BlockSpec((8, 128), lambda i: (0, 0))],
        out_shape=[jax.ShapeDtypeStruct((t, D_MODEL), F32), jax.ShapeDtypeStruct((D_MODEL, t), MXU),
                   jax.ShapeDtypeStruct((8, 128), F32)],
        compiler_params=_params(("arbitrary",)),
    )(x, ydn, yatt, wout, target)


def _outproj_bwd(dy, wout_t, oraw, z, dnw, oatt, gate, g, gt):
    t = dy.shape[0]
    tm = 256

    def body(dy_ref, w_ref, o_ref, z_ref, dnw_ref, oa_ref, g_ref, grp_ref, grpt_ref,
             do_ref, dz_ref, doa_ref, dg_ref, dd_ref, ddnw_ref):
        @pl.when(pl.program_id(0) == 0)
        def _():
            ddnw_ref[...] = jnp.zeros_like(ddnw_ref)

        dmix = jnp.dot(dy_ref[...].astype(MXU), w_ref[...], preferred_element_type=F32)
        dnw_v = dnw_ref[...]
        acc = jnp.zeros((1, DK), F32)
        for h in range(DN_HEADS):
            sl = slice(h * DK, (h + 1) * DK)
            o, zz, dm = o_ref[:, sl], z_ref[:, sl], dmix[:, sl]
            rs = lax.rsqrt(jnp.mean(o * o, axis=1, keepdims=True) + EPS)
            oh = o * rs
            dz_ref[:, sl] = dm * oh * dnw_v * _dsilu(zz)
            d_on = dm * _silu(zz)
            gg = d_on * dnw_v
            do_ref[:, sl] = rs * (gg - oh * jnp.mean(gg * oh, axis=1, keepdims=True))
            acc = acc + jnp.sum(d_on * oh, axis=0, keepdims=True)
        ddnw_ref[...] += jnp.broadcast_to(acc, (8, DK))
        da, gate_v, oa = dmix[:, 512:], g_ref[...], oa_ref[...]
        doa = da * _silu(gate_v)
        doa_ref[...] = doa
        dg_ref[...] = da * oa * _dsilu(gate_v)
        dd_ref[...] = _hdot(_hdot(doa * oa, grp_ref[...], "b"), grpt_ref[...], "b")

    row = lambda n: pl.BlockSpec((tm, n), lambda i: (i, 0))
    full = lambda a: pl.BlockSpec(a.shape, lambda i: (0,) * a.ndim)
    return pl.pallas_call(
        body, name="outproj_bwd", grid=(t // tm,),
        in_specs=[row(D_MODEL), full(wout_t), row(512), row(512), full(dnw), row(512), row(512), full(g), full(gt)],
        out_specs=[row(512)] * 5 + [pl.BlockSpec((8, DK), lambda i: (0, 0))],
        out_shape=[jax.ShapeDtypeStruct((t, 512), F32)] * 5 + [jax.ShapeDtypeStruct((8, DK), F32)],
        compiler_params=_params(("arbitrary",)),
    )(dy, wout_t, oraw, z, dnw, oatt, gate, g, gt)


def _grad_matmul(at, b, name):
    m, t = at.shape
    n = b.shape[1]
    tk = 512
    tn = n if n <= 512 else 512
    nk = t // tk

    def body(a_ref, b_ref, o_ref, acc_ref):
        k = pl.program_id(1)

        @pl.when(k == 0)
        def _():
            acc_ref[...] = jnp.zeros_like(acc_ref)

        acc_ref[...] += jnp.dot(a_ref[...], b_ref[...].astype(MXU), preferred_element_type=F32)

        @pl.when(k == nk - 1)
        def _():
            o_ref[...] = acc_ref[...].astype(GRAD_WIRE)

    return pl.pallas_call(
        body, name=name, grid=(n // tn, nk),
        in_specs=[pl.BlockSpec((m, tk), lambda j, k: (0, k)), pl.BlockSpec((tk, tn), lambda j, k: (k, j))],
        out_specs=pl.BlockSpec((m, tn), lambda j, k: (0, j)),
        out_shape=jax.ShapeDtypeStruct((m, n), GRAD_WIRE),
        scratch_shapes=[pltpu.VMEM((m, tn), F32)],
        compiler_params=_params(("arbitrary", "arbitrary")),
    )(at, b)


def _att_bwd(qkv, do, lse, dd, rb, bk):
    t = qkv.shape[0]
    rows = 512

    def body(rb_ref, bk_ref, q_ref, k_ref, v_ref, do_ref, lse_ref, dd_ref,
             dq_ref, dk_ref, dv_ref, db_ref, bias_ref, ds_ref):
        pair = pl.program_id(0)

        @pl.when(pair == 0)
        def _():
            db_ref[...] = jnp.zeros_like(db_ref)

        _bias_tables(rb_ref, bk_ref, bias_ref, pair)
        ds_ref[...] = jnp.zeros_like(ds_ref)
        for c in range(t // rows):
            sl = slice(c * rows, (c + 1) * rows)
            for ref in (dq_ref, dk_ref, dv_ref):
                ref[sl, :] = jnp.zeros((rows, 128), F32)
        h0 = _iota((BLK, 128), 1) < HD
        prev_cols = _iota((2 * BLK, 2 * BLK), 1) < BLK

        def rows_of(xb):
            return jnp.concatenate([xb[:, 0:1], xb[:, HD:HD + 1]], axis=0)

        for p, (_, r) in enumerate(PATTERNS):
            def blk(n, carry, p=p, r=r):
                cur, prev, j = _block_rows(t, r, n)
                q2, do2 = _stack_heads(q_ref[cur, :], h0), _stack_heads(do_ref[cur, :], h0)
                k2 = jnp.concatenate([k_ref[prev, :], k_ref[cur, :]], axis=0).astype(MXU)
                v2 = jnp.concatenate([v_ref[prev, :], v_ref[cur, :]], axis=0).astype(MXU)
                s = _dot_nt(q2, k2) + bias_ref[p] + jnp.where(prev_cols & (j == 0), NEG, 0.0)
                prob = jnp.exp(s - rows_of(lse_ref[cur, :]))
                ds = prob * (_dot_nt(do2, v2) - rows_of(dd_ref[cur, :]))
                ds_ref[p] += ds
                dq2 = _dot(ds, k2)
                dk2 = _dot_tn(ds, q2)
                dv2 = _dot_tn(prob, do2)
                dq_ref[cur, :] += jnp.where(h0, dq2[:BLK], dq2[BLK:])
                dk_ref[prev, :] += dk2[:BLK]
                dv_ref[prev, :] += dv2[:BLK]
                dk_ref[cur, :] += dk2[BLK:]
                dv_ref[cur, :] += dv2[BLK:]
                return carry

            lax.fori_loop(0, t // BLK, blk, 0, unroll=4)

        ri, ci = _iota((8, 128), 0), _iota((8, 128), 1)
        upd = jnp.zeros((8, 128), F32)
        for p in range(len(PATTERNS)):
            bk = bk_ref[p]
            for hh in range(2):
                dsum = ds_ref[p, hh * BLK:(hh + 1) * BLK, :]
                for b in range(N_BUCKETS):
                    val = jnp.sum(jnp.where(bk == b, dsum, 0.0))
                    upd = upd + jnp.where((ri == 2 * pair + hh) & (ci == b), val, 0.0)
        db_ref[...] += upd

    col = lambda off: pl.BlockSpec((t, 128), lambda i, off=off: (0, off + i))
    return pl.pallas_call(
        body, name="att_bwd", grid=(ATT_HEADS // 2,),
        in_specs=[pl.BlockSpec(memory_space=pltpu.SMEM), pl.BlockSpec(bk.shape, lambda i: (0, 0, 0)),
                  col(0), col(4), col(8), col(0), col(0), col(0)],
        out_specs=[col(0), col(0), col(0), pl.BlockSpec((8, 128), lambda i: (0, 0))],
        out_shape=[jax.ShapeDtypeStruct((t, 512), F32)] * 3 + [jax.ShapeDtypeStruct((8, 128), F32)],
        scratch_shapes=[pltpu.VMEM((len(PATTERNS), 2 * BLK, 2 * BLK), F32),
                        pltpu.VMEM((len(PATTERNS), 2 * BLK, 2 * BLK), F32)],
        compiler_params=_params(("arbitrary",)),
    )(rb, bk, qkv, qkv, qkv, do, lse, dd)


def _att_prep_bwd(patt, dq, dk, dv, g, gt, wq, wk):
    t = patt.shape[0]
    tm = 512

    def body(p_ref, dq_ref, dk_ref, dv_ref, g_ref, gt_ref, wq_ref, wk_ref, o_ref, dwq_ref, dwk_ref):
        @pl.when(pl.program_id(0) == 0)
        def _():
            dwq_ref[...] = jnp.zeros_like(dwq_ref)
            dwk_ref[...] = jnp.zeros_like(dwk_ref)

        for lo, w_ref, d_ref, dw_ref in ((0, wq_ref, dq_ref, dwq_ref), (512, wk_ref, dk_ref, dwk_ref)):
            xv, dyv = p_ref[:, lo:lo + 512], d_ref[...]
            rstd = lax.rsqrt(_hdot(xv * xv, g_ref[...], "b") * (1.0 / HD) + EPS)
            rsb = _hdot(rstd, gt_ref[...], "b")
            xh = xv * rsb
            gg = dyv * w_ref[...]
            mean = _hdot(_hdot(gg * xh, g_ref[...], "b") * (1.0 / HD), gt_ref[...], "b")
            o_ref[:, lo:lo + 512] = rsb * (gg - xh * mean)
            dw_ref[...] += jnp.broadcast_to(jnp.sum(dyv * xh, axis=0, keepdims=True), (8, 512))
        o_ref[:, 1024:1536] = dv_ref[...]

    row = lambda n: pl.BlockSpec((tm, n), lambda i: (i, 0))
    full = lambda a: pl.BlockSpec(a.shape, lambda i: (0,) * a.ndim)
    acc = pl.BlockSpec((8, 512), lambda i: (0, 0))
    return pl.pallas_call(
        body, name="att_prep_bwd", grid=(t // tm,),
        in_specs=[row(1536), row(512), row(512), row(512), full(g), full(gt), full(wq), full(wk)],
        out_specs=[row(1536), acc, acc],
        out_shape=[jax.ShapeDtypeStruct((t, 1536), F32), jax.ShapeDtypeStruct((8, 512), F32),
                   jax.ShapeDtypeStruct((8, 512), F32)],
        compiler_params=_params(("arbitrary",)),
    )(patt, dq, dk, dv, g, gt, wq, wk)


def _dn_scan_bwd(qkv, ba, do, sh, th, arow, dtb):
    t = qkv.shape[0]
    n = t // CHUNK
    c = CHUNK

    def body(qkv_ref, ba_ref, do_ref, sh_ref, th_ref, arow_ref, dtb_ref, dqkv_ref, dba_ref, ds_ref):
        @pl.when(pl.program_id(0) == 0)
        def _():
            ds_ref[...] = jnp.zeros_like(ds_ref)

        hs = range(DN_HEADS)
        ms, ri, ci = _chunk_common(qkv_ref[...], ba_ref[...], arow_ref[...], dtb_ref[...])
        lane = _iota((c, 128), 1)
        row = _iota((c, 1), 0)
        q, k, v = [m["q"] for m in ms], [m["k"] for m in ms], [m["v"] for m in ms]
        beta, decay = [m["beta"] for m in ms], [m["decay"] for m in ms]
        eg, egl, etail = [m["eg"] for m in ms], [m["egl"] for m in ms], [m["etail"] for m in ms]
        s, tinv = [sh_ref[0, h] for h in hs], [th_ref[0, h] for h in hs]
        d_o, d_s = [do_ref[:, h * DK:(h + 1) * DK] for h in hs], [ds_ref[h] for h in hs]
        kb = [k[h] * beta[h] for h in hs]
        vb = [v[h] * beta[h] for h in hs]
        kbg = [kb[h] * eg[h] for h in hs]
        amat = [jnp.where(ri > ci, _dot_nt(kb[h], k[h]) * decay[h], 0.0) for h in hs]
        attn = [jnp.where(ri >= ci, _dot_nt(q[h], k[h]) * decay[h], 0.0) for h in hs]
        u = [_hdot(tinv[h], vb[h]) for h in hs]
        w = [_hdot(tinv[h], kbg[h]) for h in hs]
        v_new = [u[h] - _dot(w[h], s[h]) for h in hs]
        q_dec = [q[h] * eg[h] for h in hs]
        k_tail = [k[h] * etail[h] for h in hs]

        d_vnew = [_dot_tn(attn[h], d_o[h]) + _dot(k_tail[h], d_s[h]) for h in hs]
        d_attn = [jnp.where(ri >= ci, _dot_nt(d_o[h], v_new[h]), 0.0) for h in hs]
        d_qdec = [_dot_nt(d_o[h], s[h]) for h in hs]
        for h in hs:
            ds_ref[h] = _dot_tn(q_dec[h], d_o[h]) + d_s[h] * egl[h] - _dot_tn(w[h], d_vnew[h])
        d_ktail = [_dot_nt(v_new[h], d_s[h]) for h in hs]
        d_gl = [jnp.sum(s[h] * d_s[h]) * egl[h] for h in hs]
        d_w = [-_dot_nt(d_vnew[h], s[h]) for h in hs]
        d_vb = [_hdot_tn(tinv[h], d_vnew[h]) for h in hs]
        d_kbg = [_hdot_tn(tinv[h], d_w[h]) for h in hs]
        d_a = [-jnp.where(ri > ci, _hdot_nt(d_vb[h], u[h]) + _hdot_nt(d_kbg[h], w[h]), 0.0) for h in hs]
        d_qk = [d_attn[h] * decay[h] for h in hs]
        d_kk = [d_a[h] * decay[h] for h in hs]
        d_kb = [_dot(d_kk[h], k[h]) + d_kbg[h] * eg[h] for h in hs]
        d_q = [_dot(d_qk[h], k[h]) + d_qdec[h] * eg[h] for h in hs]
        d_k = [_dot_tn(d_qk[h], q[h]) + _dot_tn(d_kk[h], kb[h]) + d_ktail[h] * etail[h] + d_kb[h] * beta[h] for h in hs]
        d_beta = [jnp.sum(d_kb[h] * k[h] + d_vb[h] * v[h], axis=1, keepdims=True) for h in hs]
        mm = [d_a[h] * amat[h] + d_attn[h] * attn[h] for h in hs]
        rows = jnp.zeros((c, c), F32)
        for h in hs:
            rows = rows + jnp.where(ri == h, jnp.sum(mm[h], axis=0, keepdims=True), 0.0)
        cols_t = jnp.concatenate([rows, jnp.zeros((c, c), F32)], axis=1).T[:c, :]
        d_gc_all = jnp.zeros((c, 128), F32)
        for h in hs:
            tail_term = jnp.sum(d_ktail[h] * k_tail[h], axis=1, keepdims=True)
            d_gc = (jnp.sum(mm[h], axis=1, keepdims=True) - _lane_col(cols_t, h)
                    + jnp.sum(d_qdec[h] * q_dec[h] + d_kbg[h] * kbg[h], axis=1, keepdims=True) - tail_term)
            d_gc = d_gc + jnp.where(row == c - 1, jnp.sum(tail_term) + d_gl[h], 0.0)
            d_gc_all = d_gc_all + jnp.where(lane == DN_HEADS + h, d_gc, 0.0)
        d_g_all = _hdot((ri <= ci).astype(F32), d_gc_all, "a")
        dba = jnp.zeros((c, 128), F32)
        for h in hs:
            d_g = _lane_col(d_g_all, DN_HEADS + h)
            d_braw = d_beta[h] * beta[h] * (1.0 - beta[h])
            d_araw = d_g * ms[h]["a_h"] * _sigmoid(ms[h]["a_raw"] + ms[h]["dt_h"])
            dba = dba + jnp.where(lane == h, d_braw, 0.0) + jnp.where(lane == DN_HEADS + h, d_araw, 0.0) \
                + jnp.where(lane == 2 * DN_HEADS + h, d_g * ms[h]["g"], 0.0)
            dqkv_ref[:, h * DK:(h + 1) * DK] = d_q[h]
            dqkv_ref[:, D_DN + h * DK:D_DN + (h + 1) * DK] = d_k[h]
            dqkv_ref[:, 2 * D_DN + h * DK:2 * D_DN + (h + 1) * DK] = d_vb[h] * beta[h]
        dba_ref[...] = dba

    rev = lambda w_: pl.BlockSpec((c, w_), lambda i: (n - 1 - i, 0))
    one = pl.BlockSpec((1, 128), lambda i: (0, 0))
    return pl.pallas_call(
        body, name="dn_scan_bwd", grid=(n,),
        in_specs=[rev(1536), rev(128), rev(512), pl.BlockSpec((1, DN_HEADS, DK, DK), lambda i: (n - 1 - i, 0, 0, 0)),
                  pl.BlockSpec((1, DN_HEADS, c, c), lambda i: (n - 1 - i, 0, 0, 0)), one, one],
        out_specs=[rev(1536), rev(128)],
        out_shape=[jax.ShapeDtypeStruct((t, 1536), F32), jax.ShapeDtypeStruct((t, 128), F32)],
        scratch_shapes=[pltpu.VMEM((DN_HEADS, DK, DK), F32)],
        compiler_params=_params(("arbitrary",)),
    )(qkv, ba, do, sh, th, arow, dtb)


def _dn_prep_bwd(pdn, cw, dact):
    t = pdn.shape[0]
    nchunk = t // CONV_ROWS

    def body(u_ref, w_ref, d_ref, du_ref, dw_ref, dy_ref):
        j = pl.program_id(0)
        dy_ref[t:t + 8, :] = jnp.zeros((8, 128), F32)
        dw = [jnp.zeros((1, 128), F32) for _ in range(4)]
        for c in range(nchunk):
            sl = slice(c * CONV_ROWS, (c + 1) * CONV_ROWS)
            taps, y = _conv_taps(u_ref, c, w_ref)
            a = _silu(y)
            dout = d_ref[sl, :]
            rs = lax.rsqrt(jnp.sum(a * a, axis=1, keepdims=True) + EPS)
            f = jnp.where(j < 8, rs, 1.0) * jnp.where(j < 4, DK ** -0.5, 1.0)
            corr = jnp.where(j < 8, f * rs * rs * jnp.sum(dout * a, axis=1, keepdims=True), 0.0)
            dy = (f * dout - corr * a) * _dsilu(y)
            dy_ref[sl, :] = dy
            for k_ in range(4):
                dw[3 - k_] = dw[3 - k_] + jnp.sum(taps[k_] * dy, axis=0, keepdims=True)
        for i in range(4):
            dw_ref[i:i + 1, :] = dw[i]
        for c in range(nchunk):
            r0 = c * CONV_ROWS
            ext = dy_ref[r0:r0 + CONV_ROWS + 8, :]
            du = ext[:CONV_ROWS, :] * w_ref[3:4, :]
            for k_ in (1, 2, 3):
                du = du + pltpu.roll(ext, CONV_ROWS + 8 - k_, 0)[:CONV_ROWS, :] * w_ref[3 - k_:4 - k_, :]
            du_ref[r0:r0 + CONV_ROWS, :] = du

    return pl.pallas_call(
        body, name="dn_prep_bwd", grid=(12,),
        in_specs=[pl.BlockSpec((t, 128), lambda j: (0, j)), pl.BlockSpec((4, 128), lambda j: (0, j)),
                  pl.BlockSpec((t, 128), lambda j: (0, j))],
        out_specs=[pl.BlockSpec((t, 128), lambda j: (0, j)), pl.BlockSpec((4, 128), lambda j: (0, j))],
        out_shape=[jax.ShapeDtypeStruct((t, 1536), F32), jax.ShapeDtypeStruct((4, 1536), F32)],
        scratch_shapes=[pltpu.VMEM((t + 8, 128), F32)],
        compiler_params=_params(("arbitrary",)),
    )(pdn, cw, dact)


def _inproj_bwd(x, nw, wt, dy, dpdn, dz, dpatt, dgate, dba):
    t = x.shape[0]
    tm = 256

    def body(x_ref, nw_ref, w_ref, dy_ref, a_ref, b_ref, c_ref, d_ref, e_ref, gx_ref, dnw_ref, cs_ref):
        @pl.when(pl.program_id(0) == 0)
        def _():
            dnw_ref[...] = jnp.zeros_like(dnw_ref)
            cs_ref[...] = jnp.zeros_like(cs_ref)

        dh = jnp.zeros((tm, D_MODEL), F32)
        for ref, lo, hi in ((a_ref, 0, 1536), (b_ref, 1536, 2048), (c_ref, 2048, 3584),
                            (d_ref, 3584, 4096), (e_ref, 4096, 4224)):
            dh = dh + jnp.dot(ref[...].astype(MXU), w_ref[lo:hi, :], preferred_element_type=F32)
        xv = x_ref[...]
        rstd = lax.rsqrt(jnp.mean(xv * xv, axis=-1, keepdims=True) + EPS)
        xh = xv * rstd
        gg = dh * nw_ref[...]
        gx_ref[...] = rstd * (gg - xh * jnp.mean(gg * xh, axis=-1, keepdims=True)) + dy_ref[...]
        dnw_ref[...] += jnp.broadcast_to(jnp.sum(dh * xh, axis=0, keepdims=True), (8, D_MODEL))
        cs_ref[...] += jnp.broadcast_to(jnp.sum(e_ref[...], axis=0, keepdims=True), (8, 128))

    row = lambda n: pl.BlockSpec((tm, n), lambda i: (i, 0))
    full = lambda a: pl.BlockSpec(a.shape, lambda i: (0,) * a.ndim)
    return pl.pallas_call(
        body, name="inproj_bwd", grid=(t // tm,),
        in_specs=[row(D_MODEL), full(nw), full(wt), row(D_MODEL), row(1536), row(512), row(1536), row(512), row(128)],
        out_specs=[row(D_MODEL), pl.BlockSpec((8, D_MODEL), lambda i: (0, 0)), pl.BlockSpec((8, 128), lambda i: (0, 0))],
        out_shape=[jax.ShapeDtypeStruct((t, D_MODEL), F32), jax.ShapeDtypeStruct((8, D_MODEL), F32),
                   jax.ShapeDtypeStruct((8, 128), F32)],
        compiler_params=_params(("arbitrary",)),
    )(x, nw, wt, dy, dpdn, dz, dpatt, dgate, dba)


def _adamw_sum(w, gs, m, v, name):
    r, c = w.shape
    nsum = gs.shape[0]
    tr = r if r <= 256 else 256
    c1 = 1.0 - ADAM_B1 ** ADAM_STEP
    c2 = 1.0 - ADAM_B2 ** ADAM_STEP

    def body(w_ref, g_ref, m_ref, v_ref, go_ref, d_ref, mo_ref, vo_ref):
        g = g_ref[0].astype(F32)
        for s in range(1, nsum):
            g = g + g_ref[s].astype(F32)
        mn = ADAM_B1 * m_ref[...] + (1.0 - ADAM_B1) * g
        vn = ADAM_B2 * v_ref[...] + (1.0 - ADAM_B2) * (g * g)
        go_ref[...] = g
        mo_ref[...] = mn
        vo_ref[...] = vn
        d_ref[...] = -ADAM_LR * ((mn / c1) / (jnp.sqrt(vn / c2) + ADAM_EPS) + ADAM_WD * w_ref[...])

    blk = pl.BlockSpec((tr, c), lambda i: (i, 0))
    return pl.pallas_call(
        body, name=name, grid=(r // tr,),
        in_specs=[blk, pl.BlockSpec((nsum, tr, c), lambda i: (0, i, 0)), blk, blk],
        out_specs=[blk] * 4, out_shape=[jax.ShapeDtypeStruct((r, c), F32)] * 4,
        compiler_params=_params(("arbitrary",)),
    )(w, gs, m, v)


def _local_step(x, target, norm_w, w_sect, w_sect_t, conv_w, a_log, dt_bias, dn_norm_w, q_norm_w, k_norm_w, rel_bias,
                w_out):
    lane = np.arange(128)
    arow = jnp.zeros((1, 128), F32).at[0, DN_HEADS:2 * DN_HEADS].set(-jnp.exp(a_log[0]))
    dtb = jnp.zeros((1, 128), F32).at[0, DN_HEADS:2 * DN_HEADS].set(dt_bias[0])
    g_np, gt_np = _group_mats()
    g, gt = jnp.asarray(g_np), jnp.asarray(gt_np)
    bk = jnp.asarray(_bucket_tables())
    wq = jnp.tile(q_norm_w, (1, ATT_HEADS)) * (HD ** -0.5)
    wk = jnp.tile(k_norm_w, (1, ATT_HEADS))
    del lane

    ht, pdn, z, patt, gate, ba = _inproj(x, norm_w, w_sect)
    qkv_dn = _dn_prep(pdn, conv_w)
    oraw, ydn, sh, th = _dn_scan_fwd(qkv_dn, ba, z, arow, dtb, dn_norm_w)
    qkv_att = _att_prep(patt, g, gt, wq, wk)
    oatt, yatt, lse = _att_fwd(qkv_att, gate, rel_bias, bk)
    dy, mix_t, loss8 = _outproj_loss(x, ydn, yatt, w_out, target)

    do_dn, dz, do_att, dgate, dd, ddnw = _outproj_bwd(dy, w_out.T, oraw, z, dn_norm_w, oatt, gate, g, gt)
    d_wout = _grad_matmul(mix_t, dy, "dw_out")
    dq, dk, dv, drb = _att_bwd(qkv_att, do_att, lse, dd, rel_bias, bk)
    dpatt, dwq8, dwk8 = _att_prep_bwd(patt, dq, dk, dv, g, gt, wq, wk)
    dqkv_dn, dba = _dn_scan_bwd(qkv_dn, ba, do_dn, sh, th, arow, dtb)
    dpdn, d_conv = _dn_prep_bwd(pdn, conv_w, dqkv_dn)
    grad_x, dnw8, cs8 = _inproj_bwd(x, norm_w, w_sect_t, dy, dpdn, dz, dpatt, dgate, dba)
    dw_sections = [_grad_matmul(ht, d_, nm) for d_, nm in
                   ((dpdn, "dw_in_dn"), (dz, "dw_in_z"), (dpatt, "dw_in_att"), (dgate, "dw_in_gate"), (dba, "dw_in_ba"))]
    small = _pack_small_grads(dnw8, cs8, ddnw, dwq8, dwk8, drb)
    return dict(loss=loss8[0, 0], grad_x=grad_x, w_in_sections=dw_sections, conv_w=d_conv, w_out=d_wout, small=small)


SMALL_ROWS = 24
SMALL_AT = dict(a_log=(slice(8, 9), slice(0, 4)), dt_bias=(slice(9, 10), slice(0, 4)),
                dn_norm_w=(slice(10, 11), slice(0, 128)), q_norm_w=(slice(11, 12), slice(0, HD)),
                k_norm_w=(slice(12, 13), slice(0, HD)), rel_bias=(slice(16, 24), slice(0, N_BUCKETS)))
SMALL_NAMES = ("norm_w", "a_log", "dt_bias", "dn_norm_w", "q_norm_w", "k_norm_w", "rel_bias")


def _pack_small_grads(dnw8, cs8, ddnw8, dwq8, dwk8, drb):
    def body(dnw_ref, cs_ref, ddnw_ref, dwq_ref, dwk_ref, drb_ref, o_ref):
        lane = _iota((8, 128), 1)
        o_ref[...] = jnp.zeros_like(o_ref)
        for k in range(D_MODEL // 128):
            o_ref[k:k + 1, :] = dnw_ref[0:1, k * 128:(k + 1) * 128]
        cs = cs_ref[...]
        o_ref[8:9, :] = jnp.where(lane < DN_HEADS, pltpu.roll(cs, 128 - 2 * DN_HEADS, 1), 0.0)[0:1, :]
        o_ref[9:10, :] = jnp.where(lane < DN_HEADS, pltpu.roll(cs, 128 - DN_HEADS, 1), 0.0)[0:1, :]
        o_ref[10:11, :] = ddnw_ref[0:1, :]
        for row, ref, scale in ((11, dwq_ref, HD ** -0.5), (12, dwk_ref, 1.0)):
            acc = ref[:, 0:128] + ref[:, 128:256] + ref[:, 256:384] + ref[:, 384:512]
            acc = (acc + pltpu.roll(acc, HD, 1)) * scale
            o_ref[row:row + 1, :] = jnp.where(lane < HD, acc, 0.0)[0:1, :]
        o_ref[16:24, :] = drb_ref[...]

    return pl.pallas_call(body, name="pack_small_grads", out_shape=jax.ShapeDtypeStruct((SMALL_ROWS, 128), F32),
                          )(dnw8, cs8, ddnw8, dwq8, dwk8, drb)


def _adam_math(w, g, m, v):
    c1 = 1.0 - ADAM_B1 ** ADAM_STEP
    c2 = 1.0 - ADAM_B2 ** ADAM_STEP
    mn = ADAM_B1 * m + (1.0 - ADAM_B1) * g
    vn = ADAM_B2 * v + (1.0 - ADAM_B2) * (g * g)
    return -ADAM_LR * ((mn / c1) / (jnp.sqrt(vn / c2) + ADAM_EPS) + ADAM_WD * w), mn, vn


def _adamw_small(gs, ws, ms, vs):
    n = len(SMALL_NAMES)

    def body(g_ref, *refs):
        w_refs, m_refs, v_refs = refs[:n], refs[n:2 * n], refs[2 * n:3 * n]
        outs = refs[3 * n:]

        def one(i, rows, lanes, at):
            g = g_ref[0, rows, lanes]
            for s in range(1, gs.shape[0]):
                g = g + g_ref[s, rows, lanes]
            d, mn, vn = _adam_math(w_refs[i][at], g, m_refs[i][at], v_refs[i][at])
            for kind, val in enumerate((g, d, mn, vn)):
                outs[kind * n + i][at] = val

        for k in range(D_MODEL // 128):
            one(0, slice(k, k + 1), slice(0, 128), (slice(0, 1), slice(k * 128, (k + 1) * 128)))
        for i, nm in enumerate(SMALL_NAMES[1:], start=1):
            rows, lanes = SMALL_AT[nm]
            one(i, rows, lanes, (slice(None), slice(None)))

    shapes = [jax.ShapeDtypeStruct(w.shape, F32) for w in ws]
    res = pl.pallas_call(body, name="adamw_small", out_shape=shapes * 4)(gs, *ws, *ms, *vs)
    return [res[k * n:(k + 1) * n] for k in range(4)]


def kernel(x, norm_w, w_in, conv_w, a_log, dt_bias, dn_norm_w, q_norm_w, k_norm_w, rel_bias, w_out, loss_target, m_norm_w, m_w_in, m_conv_w, m_a_log, m_dt_bias, m_dn_norm_w, m_q_norm_w, m_k_norm_w, m_rel_bias, m_w_out, v_norm_w, v_w_in, v_conv_w, v_a_log, v_dt_bias, v_dn_norm_w, v_q_norm_w, v_k_norm_w, v_rel_bias, v_w_out):
    assert w_in.shape[2] == SHARD_COLS
    win8, wout8, conv8 = _gather_weights([w_in[0].astype(MXU), w_out[0].astype(MXU), conv_w[0]])
    w_sect, w_sect_t = _build_w(win8)
    wout_full = wout8.reshape(D_MODEL, D_MODEL)
    conv_full = conv8.transpose(1, 0, 2).reshape(4, 3 * D_DN)

    gr = _local_step(x[0], loss_target[0], norm_w, w_sect, w_sect_t, conv_full, a_log, dt_bias, dn_norm_w, q_norm_w,
                     k_norm_w, rel_bias, wout_full)

    slabs = [_build_slabs(*gr["w_in_sections"]),
             gr["w_out"].reshape(4, 2, D_MODEL // N_DEV, D_MODEL).transpose(1, 0, 2, 3),
             gr["conv_w"].reshape(4, 4, 2, 3 * D_DN // N_DEV).transpose(2, 1, 0, 3),
             jnp.broadcast_to(gr["small"][None, None], (2, 4, SMALL_ROWS, 128))]
    core = lax.axis_index("c").astype(jnp.int32).reshape(1)
    from_sibling = _swap_siblings(slabs)
    wires = (GRAD_WIRE, GRAD_WIRE, F32, F32)
    partial = [_chip_sum(slabs[i], from_sibling[i], core, wires[i], "chip_sum_%d" % i) for i in range(4)]
    r_win, r_wout, r_conv, r_small = _swap_chips(partial)

    g_win, d_win, m_win, v_win = _adamw_sum(w_in[0], r_win, m_w_in[0], v_w_in[0], "adamw_w_in")
    g_wout, d_wout, m_wout, v_wout = _adamw_sum(w_out[0], r_wout, m_w_out[0], v_w_out[0], "adamw_w_out")
    g_conv, d_conv, m_conv, v_conv = _adamw_sum(conv_w[0], r_conv, m_conv_w[0], v_conv_w[0], "adamw_conv_w")
    small = _adamw_small(r_small,
                         (norm_w, a_log, dt_bias, dn_norm_w, q_norm_w, k_norm_w, rel_bias),
                         (m_norm_w, m_a_log, m_dt_bias, m_dn_norm_w, m_q_norm_w, m_k_norm_w, m_rel_bias),
                         (v_norm_w, v_a_log, v_dt_bias, v_dn_norm_w, v_q_norm_w, v_k_norm_w, v_rel_bias))

    loss = lax.psum(gr["loss"], AXES)
    names = ("norm_w", "w_in", "conv_w", "a_log", "dt_bias", "dn_norm_w", "q_norm_w", "k_norm_w", "rel_bias", "w_out")
    big = dict(w_in=(g_win, d_win, m_win, v_win), conv_w=(g_conv, d_conv, m_conv, v_conv),
               w_out=(g_wout, d_wout, m_wout, v_wout))
    outs = [loss, gr["grad_x"][None]]
    for kind in range(4):
        for nm in names:
            outs.append(big[nm][kind][None] if nm in big else small[kind][SMALL_NAMES.index(nm)])
    return tuple(outs)
```

```python
import functools
import math

import numpy as np
import jax
import jax.numpy as jnp
from jax import lax
from jax.experimental import pallas as pl
from jax.experimental.pallas import tpu as pltpu

F32 = jnp.float32
MXU = jnp.bfloat16
GRAD_WIRE = jnp.bfloat16
HI = lax.Precision.HIGHEST

D_MODEL = 1024
D_DN = 512
DN_HEADS = 4
DK = 128
CHUNK = 64
D_ATT = 512
ATT_HEADS = 8
HD = 64
PATTERNS = ((128, 1), (512, 4), (2048, 16))
BLK = 128
N_BUCKETS = 32
MAX_DISTANCE = 2048
EPS = 1e-6
W_COLS = 4224
N_DEV = 8
AXES = ("x", "y", "c")

ADAM_LR = 0.001
ADAM_B1 = 0.9
ADAM_B2 = 0.999
ADAM_EPS = 1e-08
ADAM_WD = 0.01
ADAM_STEP = 10

VMEM_LIMIT = 56 * 1024 * 1024
NEG = -1e30


def _dot(a, b):
    return jnp.dot(a.astype(MXU), b.astype(MXU), preferred_element_type=F32)


def _dot_nt(a, b):
    return lax.dot_general(a.astype(MXU), b.astype(MXU), (((1,), (1,)), ((), ())), preferred_element_type=F32)


def _dot_tn(a, b):
    return lax.dot_general(a.astype(MXU), b.astype(MXU), (((0,), (0,)), ((), ())), preferred_element_type=F32)


def _split(a):
    hi = a.astype(jnp.bfloat16)
    return hi, (a - hi.astype(F32)).astype(jnp.bfloat16)


def _dot_split(a, b, dims, exact):
    dg = lambda u, v: lax.dot_general(u, v, (dims, ((), ())), preferred_element_type=F32)
    if exact == "b":
        ah, al = _split(a)
        bh = b.astype(jnp.bfloat16)
        return dg(ah, bh) + dg(al, bh)
    if exact == "a":
        bh, bm = _split(b)
        bl = (b - bh.astype(F32) - bm.astype(F32)).astype(jnp.bfloat16)
        ah = a.astype(jnp.bfloat16)
        return dg(ah, bh) + (dg(ah, bm) + dg(ah, bl))
    ah, al = _split(a)
    bh, bl = _split(b)
    return dg(ah, bh) + (dg(ah, bl) + dg(al, bh))


def _hdot(a, b, exact=None):
    return _dot_split(a, b, ((1,), (0,)), exact)


def _hdot_nt(a, b, exact=None):
    return _dot_split(a, b, ((1,), (1,)), exact)


def _hdot_tn(a, b, exact=None):
    return _dot_split(a, b, ((0,), (0,)), exact)


def _sigmoid(x):
    return 1.0 / (1.0 + jnp.exp(-x))


def _silu(x):
    return x * _sigmoid(x)


def _dsilu(x):
    s = _sigmoid(x)
    return s * (1.0 + x * (1.0 - s))


def _softplus(x):
    return jnp.maximum(x, 0.0) + jnp.log(1.0 + jnp.exp(-jnp.abs(x)))


def _iota(shape, dim):
    return lax.broadcasted_iota(jnp.int32, shape, dim)


def _lane_col(x, k):
    return jnp.sum(jnp.where(_iota(x.shape, 1) == k, x, 0.0), axis=1, keepdims=True)


def _params(sem=None):
    return pltpu.CompilerParams(dimension_semantics=sem, vmem_limit_bytes=VMEM_LIMIT)


def _t5_bucket(dist):
    max_exact = N_BUCKETS // 2
    d = np.maximum(dist, 1).astype(np.float64)
    large = max_exact + (np.log(d / max_exact) / math.log(MAX_DISTANCE / max_exact)
                         * (N_BUCKETS - max_exact)).astype(np.int32)
    large = np.minimum(large, N_BUCKETS - 1)
    return np.where(dist < max_exact, dist, large).astype(np.int32)


def _bucket_tables():
    qi = np.arange(BLK)[:, None]
    kj = np.arange(2 * BLK)[None, :]
    step = qi - kj + BLK
    band = (step >= 0) & (step <= BLK)
    out = []
    for _, r in PATTERNS:
        b = _t5_bucket(np.clip(step, 0, None) * r)
        out.append(np.where(band, b, -1))
    return np.stack(out).astype(np.int32)


def _group_mats():
    g = np.zeros((D_ATT, 128), np.float32)
    for h in range(ATT_HEADS):
        g[h * HD:(h + 1) * HD, h] = 1.0
    return g, np.ascontiguousarray(g.T)


CHIP_FLIPS = ((1, 0), (0, 1), (1, 1))
ANY_SPEC = pl.BlockSpec(memory_space=pl.ANY)
MESH_ID = pl.DeviceIdType.MESH


def _other_chips():
    x, y = lax.axis_index("x"), lax.axis_index("y")
    return [((1 - x if fx else x), (1 - y if fy else y)) for fx, fy in CHIP_FLIPS]


def _gather_weights(arrs):
    n = len(arrs)

    def body(*refs):
        ins, outs = refs[:n], refs[n:2 * n]
        send, recv, loc = refs[2 * n:]
        x, y, c = (lax.axis_index(a) for a in AXES)
        sib = (x, y, 1 - c)
        chips = _other_chips()
        lin = lambda px, py, pc: 4 * px + 2 * py + pc

        def copy(a, k, block, to, src=None):
            slot = outs[a].at[lin(*block)]
            return pltpu.make_async_remote_copy(src_ref=slot if src is None else src, dst_ref=slot,
                                                send_sem=send.at[a, k], recv_sem=recv.at[a, k],
                                                device_id=to, device_id_type=MESH_ID)

        started = []
        for a in range(n):
            mine = pltpu.make_async_copy(ins[a], outs[a].at[lin(x, y, c)], loc.at[a])
            mine.start()
            started.append(mine)
        firsts = []
        for a in range(n):
            firsts.append(copy(a, 0, (x, y, c), sib, src=ins[a]))
            firsts += [copy(a, 1 + j, (x, y, c), (*chip, c), src=ins[a]) for j, chip in enumerate(chips)]
        for cp in firsts:
            cp.start()
        passed = []
        for j, chip in enumerate(chips):
            for a in range(n):
                copy(a, 1 + j, (*chip, c), (x, y, c)).wait_recv()
                fw = copy(a, 4 + j, (*chip, c), sib)
                fw.start()
                passed.append(fw)
        for a in range(n):
            copy(a, 0, sib, (x, y, c)).wait_recv()
            for j, chip in enumerate(chips):
                copy(a, 4 + j, (*chip, 1 - c), (x, y, c)).wait_recv()
        for cp in firsts + passed:
            cp.wait_send()
        for mine in started:
            mine.wait()

    return pl.pallas_call(
        body, name="gather_weights", out_shape=[jax.ShapeDtypeStruct((N_DEV,) + a.shape, a.dtype) for a in arrs],
        in_specs=[ANY_SPEC] * n, out_specs=[ANY_SPEC] * n,
        scratch_shapes=[pltpu.SemaphoreType.DMA((n, 7)), pltpu.SemaphoreType.DMA((n, 7)), pltpu.SemaphoreType.DMA((n,))],
    )(*arrs)


def _swap_siblings(arrs):
    n = len(arrs)

    def body(*refs):
        ins, outs = refs[:n], refs[n:2 * n]
        send, recv = refs[2 * n:]
        x, y, c = (lax.axis_index(a) for a in AXES)
        cps = [pltpu.make_async_remote_copy(src_ref=ins[a].at[1 - c], dst_ref=outs[a], send_sem=send.at[a],
                                            recv_sem=recv.at[a], device_id=(x, y, 1 - c), device_id_type=MESH_ID)
               for a in range(n)]
        for cp in cps:
            cp.start()
        for cp in cps:
            cp.wait()

    return pl.pallas_call(
        body, name="swap_siblings", out_shape=[jax.ShapeDtypeStruct(a.shape[1:], a.dtype) for a in arrs],
        in_specs=[ANY_SPEC] * n, out_specs=[ANY_SPEC] * n,
        scratch_shapes=[pltpu.SemaphoreType.DMA((n,)), pltpu.SemaphoreType.DMA((n,))],
    )(*arrs)


def _chip_sum(mine2, theirs, core, wire, name):
    _, nchip, r, cdim = mine2.shape
    tr = r if r <= 256 else 256

    def body(core_ref, a_ref, b_ref, o_ref):
        del core_ref
        o_ref[...] = (a_ref[...].astype(F32) + b_ref[...].astype(F32)).astype(wire)

    grid_spec = pltpu.PrefetchScalarGridSpec(
        num_scalar_prefetch=1, grid=(nchip, r // tr),
        in_specs=[pl.BlockSpec((None, None, tr, cdim), lambda j, i, cr: (cr[0], j, i, 0)),
                  pl.BlockSpec((None, tr, cdim), lambda j, i, cr: (j, i, 0))],
        out_specs=pl.BlockSpec((None, tr, cdim), lambda j, i, cr: (j, i, 0)))
    return pl.pallas_call(
        body, name=name, grid_spec=grid_spec, out_shape=jax.ShapeDtypeStruct((nchip, r, cdim), wire),
        compiler_params=_params(("arbitrary", "arbitrary")),
    )(core, mine2, theirs)


def _chip_swap_copies(ins, outs, send, recv, loc):
    x, y, c = (lax.axis_index(a) for a in AXES)
    me = 2 * x + y
    starts, arrivals, drains = [], [], []
    for a in range(len(ins)):
        lc = pltpu.make_async_copy(ins[a].at[me], outs[a].at[me], loc.at[a])
        starts.append(lc.start)
        drains.append(lc.wait)
        for j, (px, py) in enumerate(_other_chips()):
            them = 2 * px + py
            cp = pltpu.make_async_remote_copy(src_ref=ins[a].at[them], dst_ref=outs[a].at[me], send_sem=send.at[a, j],
                                              recv_sem=recv.at[a, j], device_id=(px, py, c), device_id_type=MESH_ID)
            landing = pltpu.make_async_remote_copy(src_ref=ins[a].at[them], dst_ref=outs[a].at[them],
                                                   send_sem=send.at[a, j], recv_sem=recv.at[a, j],
                                                   device_id=(px, py, c), device_id_type=MESH_ID)
            starts.append(cp.start)
            arrivals.append(landing.wait_recv)
            drains.append(cp.wait_send)
    return starts, arrivals + drains


def _share_small(pack):
    def body(in_ref, out_ref, send, recv, loc):
        x, y, c = (lax.axis_index(a) for a in AXES)
        me = 4 * x + 2 * y + c
        lc = pltpu.make_async_copy(in_ref, out_ref.at[me], loc.at[0])
        lc.start()
        sends, arrivals = [], []
        for k in range(1, N_DEV):
            px = 1 - x if k & 4 else x
            py = 1 - y if k & 2 else y
            pc = 1 - c if k & 1 else c
            cp = pltpu.make_async_remote_copy(src_ref=in_ref, dst_ref=out_ref.at[me], send_sem=send.at[k - 1],
                                              recv_sem=recv.at[k - 1], device_id=(px, py, pc), device_id_type=MESH_ID)
            cp.start()
            sends.append(cp)
            arrivals.append(pltpu.make_async_remote_copy(src_ref=in_ref, dst_ref=out_ref.at[4 * px + 2 * py + pc],
                                                         send_sem=send.at[k - 1], recv_sem=recv.at[k - 1],
                                                         device_id=(px, py, pc), device_id_type=MESH_ID))
        for cp in arrivals:
            cp.wait_recv()
        for cp in sends:
            cp.wait_send()
        lc.wait()

    return pl.pallas_call(
        body, name="share_small", out_shape=jax.ShapeDtypeStruct((N_DEV,) + pack.shape, pack.dtype),
        in_specs=[ANY_SPEC], out_specs=ANY_SPEC,
        scratch_shapes=[pltpu.SemaphoreType.DMA((N_DEV - 1,)), pltpu.SemaphoreType.DMA((N_DEV - 1,)),
                        pltpu.SemaphoreType.DMA((1,))],
    )(pack)


W_PARTS = ((0, 0, 2048), (2048, 4096, 8), (2056, 2048, 2048))
SHARD_COLS = 513


def _pieces(lo, hi, parts):
    out = []
    for ref_start, tgt_start, width in parts:
        a, b = max(lo, ref_start), min(hi, ref_start + width)
        if a < b:
            out.append((a - lo, tgt_start + a - ref_start, b - a))
    return out


def _build_w(win8):
    tr = 256

    def body(in_ref, w_ref, wt_ref):
        w_ref[:, 4096:W_COLS] = jnp.zeros((tr, W_COLS - 4096), MXU)
        for p in range(N_DEV):
            for src, dst, width in _pieces(p * SHARD_COLS, (p + 1) * SHARD_COLS, W_PARTS):
                w_ref[:, dst:dst + width] = in_ref[p, :, src:src + width]
        for k in range(W_COLS // 128):
            wt_ref[k * 128:(k + 1) * 128, :] = w_ref[:, k * 128:(k + 1) * 128].astype(F32).T.astype(MXU)

    return pl.pallas_call(
        body, name="build_w", grid=(D_MODEL // tr,),
        in_specs=[pl.BlockSpec((N_DEV, tr, SHARD_COLS), lambda i: (0, i, 0))],
        out_specs=[pl.BlockSpec((tr, W_COLS), lambda i: (i, 0)), pl.BlockSpec((W_COLS, tr), lambda i: (0, i))],
        out_shape=[jax.ShapeDtypeStruct((D_MODEL, W_COLS), MXU), jax.ShapeDtypeStruct((W_COLS, D_MODEL), MXU)],
        compiler_params=_params(("arbitrary",)),
    )(win8)


def _build_slabs(s_dn, s_z, s_att, s_gate, s_ba):
    tr = 256
    secs = (s_dn, s_z, s_ba, s_att, s_gate)
    parts = ((0, 0, 1536), (1536, 1, 512), (2048, 2, 8), (2056, 3, 1536), (3592, 4, 512))

    def body(dn_ref, z_ref, ba_ref, att_ref, gate_ref, o_ref):
        refs = (dn_ref, z_ref, ba_ref, att_ref, gate_ref)
        for p in range(N_DEV):
            lo, hi = p * SHARD_COLS, (p + 1) * SHARD_COLS
            for ref_start, idx, width in parts:
                a, b = max(lo, ref_start), min(hi, ref_start + width)
                if a < b:
                    o_ref[p % 2, p // 2, :, a - lo:b - lo] = refs[idx][:, a - ref_start:b - ref_start]

    return pl.pallas_call(
        body, name="build_slabs", grid=(D_MODEL // tr,),
        in_specs=[pl.BlockSpec((tr, s.shape[1]), lambda i: (i, 0)) for s in secs],
        out_specs=pl.BlockSpec((2, 4, tr, SHARD_COLS), lambda i: (0, 0, i, 0)),
        out_shape=jax.ShapeDtypeStruct((2, 4, D_MODEL, SHARD_COLS), s_dn.dtype),
        compiler_params=_params(("arbitrary",)),
    )(*secs)


def _inproj(x, nw, w):
    t = x.shape[0]
    tm = 256

    def body(x_ref, nw_ref, w_ref, ht_ref, pdn_ref, z_ref, patt_ref, gate_ref, ba_ref):
        xv = x_ref[...]
        rstd = lax.rsqrt(jnp.mean(xv * xv, axis=-1, keepdims=True) + EPS)
        hf = xv * rstd * nw_ref[...]
        h = hf.astype(MXU)
        ht_ref[...] = hf.T.astype(MXU)
        for ref, lo, hi in ((pdn_ref, 0, 1536), (z_ref, 1536, 2048), (patt_ref, 2048, 3584),
                            (gate_ref, 3584, 4096), (ba_ref, 4096, 4224)):
            ref[...] = jnp.dot(h, w_ref[:, lo:hi], preferred_element_type=F32)

    row = lambda n: pl.BlockSpec((tm, n), lambda i: (i, 0))
    full = lambda a: pl.BlockSpec(a.shape, lambda i: (0,) * a.ndim)
    return pl.pallas_call(
        body, name="inproj", grid=(t // tm,),
        in_specs=[row(D_MODEL), full(nw), full(w)],
        out_specs=[pl.BlockSpec((D_MODEL, tm), lambda i: (0, i)), row(1536), row(512), row(1536), row(512), row(128)],
        out_shape=[jax.ShapeDtypeStruct((D_MODEL, t), MXU)] +
                  [jax.ShapeDtypeStruct((t, n), F32) for n in (1536, 512, 1536, 512, 128)],
        compiler_params=_params(("arbitrary",)),
    )(x, nw, w)


CONV_ROWS = 512


def _conv_taps(u_ref, c, w_ref):
    r0 = c * CONV_ROWS
    if c == 0:
        ext = jnp.concatenate([jnp.zeros((8, 128), F32), u_ref[0:CONV_ROWS, :]], axis=0)
    else:
        ext = u_ref[r0 - 8:r0 + CONV_ROWS, :]
    taps = [ext[8:, :]] + [pltpu.roll(ext, k, 0)[8:, :] for k in (1, 2, 3)]
    y = taps[0] * w_ref[3:4, :]
    for k in (1, 2, 3):
        y = y + taps[k] * w_ref[3 - k:4 - k, :]
    return taps, y


def _dn_prep(pdn, cw):
    t = pdn.shape[0]

    def body(u_ref, w_ref, o_ref):
        j = pl.program_id(0)
        for c in range(t // CONV_ROWS):
            _, y = _conv_taps(u_ref, c, w_ref)
            a = _silu(y)
            ssq = jnp.sum(a * a, axis=1, keepdims=True)
            f = jnp.where(j < 8, lax.rsqrt(ssq + EPS), 1.0) * jnp.where(j < 4, DK ** -0.5, 1.0)
            o_ref[c * CONV_ROWS:(c + 1) * CONV_ROWS, :] = a * f

    return pl.pallas_call(
        body, name="dn_prep", grid=(12,),
        in_specs=[pl.BlockSpec((t, 128), lambda j: (0, j)), pl.BlockSpec((4, 128), lambda j: (0, j))],
        out_specs=pl.BlockSpec((t, 128), lambda j: (0, j)),
        out_shape=jax.ShapeDtypeStruct((t, 1536), F32),
        compiler_params=_params(("arbitrary",)),
    )(pdn, cw)


def _chunk_common(qkv, ba, arow, dtb):
    c = CHUNK
    ri, ci = _iota((c, c), 0), _iota((c, c), 1)
    lane = _iota((c, 128), 1)
    g_all = jnp.where((lane >= DN_HEADS) & (lane < 2 * DN_HEADS), arow * _softplus(ba + dtb), 0.0)
    gc_all = _hdot((ri >= ci).astype(F32), g_all, "a")
    gc_t = gc_all.T
    beta_all = _sigmoid(ba)
    out = []
    for h in range(DN_HEADS):
        gc = _lane_col(gc_all, DN_HEADS + h)
        gcr = gc_t[DN_HEADS + h:DN_HEADS + h + 1, :]
        gl = gc[c - 1:c, :]
        out.append(dict(
            q=qkv[:, h * DK:(h + 1) * DK], k=qkv[:, D_DN + h * DK:D_DN + (h + 1) * DK],
            v=qkv[:, 2 * D_DN + h * DK:2 * D_DN + (h + 1) * DK],
            beta=_lane_col(beta_all, h), g=_lane_col(g_all, DN_HEADS + h),
            a_raw=_lane_col(ba, DN_HEADS + h), a_h=_lane_col(arow, DN_HEADS + h), dt_h=_lane_col(dtb, DN_HEADS + h),
            decay=jnp.exp(jnp.where(ri >= ci, gc - gcr, NEG)), eg=jnp.exp(gc), egl=jnp.exp(gl), etail=jnp.exp(gl - gc)))
    return out, ri, ci


SCAN_CHUNKS = 8


def _dn_scan_fwd(qkv, ba, z, arow, dtb, dnw):
    t = qkv.shape[0]
    n = t // CHUNK
    c = CHUNK
    cps = SCAN_CHUNKS
    hs = range(DN_HEADS)
    chains = [(j, h) for j in range(cps) for h in hs]

    def body(qkv_ref, ba_ref, z_ref, arow_ref, dtb_ref, dnw_ref, o_ref, y_ref, sh_ref, th_ref, s_ref):
        @pl.when(pl.program_id(0) == 0)
        def _():
            s_ref[...] = jnp.zeros_like(s_ref)

        ms = {}
        for j in range(cps):
            rows = slice(j * c, (j + 1) * c)
            mj, ri, ci = _chunk_common(qkv_ref[rows, :], ba_ref[rows, :], arow_ref[...], dtb_ref[...])
            for h in hs:
                ms[j, h] = mj[h]
        kb = {x: ms[x]["k"] * ms[x]["beta"] for x in chains}
        amat = {x: jnp.where(ri > ci, _dot_nt(kb[x], ms[x]["k"]) * ms[x]["decay"], 0.0) for x in chains}
        attn = {x: jnp.where(ri >= ci, _dot_nt(ms[x]["q"], ms[x]["k"]) * ms[x]["decay"], 0.0) for x in chains}
        eye = (ri == ci).astype(F32)
        tinv = {x: eye - amat[x] for x in chains}
        pw = amat
        for _ in range(5):
            pw = {x: _hdot(pw[x], pw[x]) for x in chains}
            tinv = {x: tinv[x] + _hdot(tinv[x], pw[x]) for x in chains}
        u = {x: _hdot(tinv[x], ms[x]["v"] * ms[x]["beta"]) for x in chains}
        w = {x: _hdot(tinv[x], kb[x] * ms[x]["eg"]) for x in chains}
        q_dec = {x: ms[x]["q"] * ms[x]["eg"] for x in chains}
        k_tail = {x: ms[x]["k"] * ms[x]["etail"] for x in chains}
        s = [s_ref[h] for h in hs]
        for j in range(cps):
            rows = slice(j * c, (j + 1) * c)
            v_new = [u[j, h] - _dot(w[j, h], s[h]) for h in hs]
            o = [_dot(q_dec[j, h], s[h]) + _dot(attn[j, h], v_new[h]) for h in hs]
            for h in hs:
                sh_ref[j, h] = s[h]
                th_ref[j, h] = tinv[j, h]
            s = [s[h] * ms[j, h]["egl"] + _dot_tn(k_tail[j, h], v_new[h]) for h in hs]
            for h in hs:
                cols = slice(h * DK, (h + 1) * DK)
                o_ref[rows, cols] = o[h]
                rs = lax.rsqrt(jnp.mean(o[h] * o[h], axis=1, keepdims=True) + EPS)
                y_ref[rows, cols] = o[h] * rs * dnw_ref[...] * _silu(z_ref[rows, cols])
        for h in hs:
            s_ref[h] = s[h]

    row = lambda w_: pl.BlockSpec((cps * c, w_), lambda i: (i, 0))
    one = pl.BlockSpec((1, 128), lambda i: (0, 0))
    return pl.pallas_call(
        body, name="dn_scan_fwd", grid=(n // cps,),
        in_specs=[row(1536), row(128), row(512), one, one, one],
        out_specs=[row(512), row(512), pl.BlockSpec((cps, DN_HEADS, DK, DK), lambda i: (i, 0, 0, 0)),
                   pl.BlockSpec((cps, DN_HEADS, c, c), lambda i: (i, 0, 0, 0))],
        out_shape=[jax.ShapeDtypeStruct((t, 512), F32), jax.ShapeDtypeStruct((t, 512), F32),
                   jax.ShapeDtypeStruct((n, DN_HEADS, DK, DK), F32), jax.ShapeDtypeStruct((n, DN_HEADS, c, c), F32)],
        scratch_shapes=[pltpu.VMEM((DN_HEADS, DK, DK), F32)],
        compiler_params=_params(("arbitrary",)),
    )(qkv, ba, z, arow, dtb, dnw)


def _att_prep(patt, g, gt, wq, wk):
    t = patt.shape[0]
    tm = 512

    def body(p_ref, g_ref, gt_ref, wq_ref, wk_ref, o_ref):
        for lo, w_ref in ((0, wq_ref), (512, wk_ref)):
            xv = p_ref[:, lo:lo + 512]
            rstd = lax.rsqrt(_hdot(xv * xv, g_ref[...], "b") * (1.0 / HD) + EPS)
            o_ref[:, lo:lo + 512] = xv * _hdot(rstd, gt_ref[...], "b") * w_ref[...]
        o_ref[:, 1024:1536] = p_ref[:, 1024:1536]

    full = lambda a: pl.BlockSpec(a.shape, lambda i: (0,) * a.ndim)
    return pl.pallas_call(
        body, name="att_prep", grid=(t // tm,),
        in_specs=[pl.BlockSpec((tm, 1536), lambda i: (i, 0)), full(g), full(gt), full(wq), full(wk)],
        out_specs=pl.BlockSpec((tm, 1536), lambda i: (i, 0)),
        out_shape=jax.ShapeDtypeStruct((t, 1536), F32),
        compiler_params=_params(("arbitrary",)),
    )(patt, g, gt, wq, wk)


def _bias_tables(rb_ref, bk_ref, bias_ref, pair):
    for p in range(len(PATTERNS)):
        bk = bk_ref[p]
        for hh in range(2):
            head = 2 * pair + hh
            bm = jnp.full((BLK, 2 * BLK), NEG, F32)
            for b in range(N_BUCKETS):
                bm = jnp.where(bk == b, rb_ref[head, b], bm)
            bias_ref[p, hh * BLK:(hh + 1) * BLK, :] = bm


def _stack_heads(xb, h0):
    return jnp.concatenate([jnp.where(h0, xb, 0.0), jnp.where(h0, 0.0, xb)], axis=0).astype(MXU)


def _block_rows(t, r, n):
    per_class = (t // r) // BLK
    res = n // per_class
    j = n % per_class
    start = res + BLK * r * j
    pstart = res + BLK * r * jnp.maximum(j - 1, 0)
    if r == 1:
        return pl.ds(pl.multiple_of(start, BLK), BLK), pl.ds(pl.multiple_of(pstart, BLK), BLK), j
    return pl.ds(start, BLK, stride=r), pl.ds(pstart, BLK, stride=r), j


def _att_fwd(qkv, gate, rb, bk):
    t = qkv.shape[0]
    rows = 512

    def body(rb_ref, bk_ref, q_ref, k_ref, v_ref, g_ref, o_ref, y_ref, lse_ref,
             o0_ref, o1_ref, o2_ref, l0_ref, l1_ref, l2_ref, bias_ref):
        pair = pl.program_id(0)
        _bias_tables(rb_ref, bk_ref, bias_ref, pair)
        h0 = _iota((BLK, 128), 1) < HD
        prev_cols = _iota((2 * BLK, 2 * BLK), 1) < BLK
        op_refs, lp_refs = (o0_ref, o1_ref, o2_ref), (l0_ref, l1_ref, l2_ref)

        for p, (_, r) in enumerate(PATTERNS):
            def blk(n, carry, p=p, r=r):
                cur, prev, j = _block_rows(t, r, n)
                q2 = _stack_heads(q_ref[cur, :], h0)
                k2 = jnp.concatenate([k_ref[prev, :], k_ref[cur, :]], axis=0).astype(MXU)
                v2 = jnp.concatenate([v_ref[prev, :], v_ref[cur, :]], axis=0).astype(MXU)
                s = _dot_nt(q2, k2) + bias_ref[p] + jnp.where(prev_cols & (j == 0), NEG, 0.0)
                m = jnp.max(s, axis=1, keepdims=True)
                e = jnp.exp(s - m)
                l = jnp.sum(e, axis=1, keepdims=True)
                pv = _dot(e, v2) / l
                lse = m + jnp.log(l)
                op_refs[p][cur, :] = jnp.where(h0, pv[:BLK], pv[BLK:])
                lp_refs[p][cur, :] = jnp.where(h0, lse[:BLK], lse[BLK:])
                return carry

            lax.fori_loop(0, t // BLK, blk, 0, unroll=8)

        for c in range(t // rows):
            sl = slice(c * rows, (c + 1) * rows)
            ls = [ref[sl, :] for ref in lp_refs]
            mx = jnp.maximum(jnp.maximum(ls[0], ls[1]), ls[2])
            ws = [jnp.exp(v_ - mx) for v_ in ls]
            den = ws[0] + ws[1] + ws[2]
            o = (ws[0] * o0_ref[sl, :] + ws[1] * o1_ref[sl, :] + ws[2] * o2_ref[sl, :]) / den
            o_ref[sl, :] = o
            y_ref[sl, :] = o * _silu(g_ref[sl, :])
            lse_ref[sl, :] = mx + jnp.log(den)

    col = lambda off: pl.BlockSpec((t, 128), lambda i, off=off: (0, off + i))
    return pl.pallas_call(
        body, name="att_fwd", grid=(ATT_HEADS // 2,),
        in_specs=[pl.BlockSpec(memory_space=pltpu.SMEM), pl.BlockSpec(bk.shape, lambda i: (0, 0, 0)),
                  col(0), col(4), col(8), col(0)],
        out_specs=[col(0), col(0), col(0)],
        out_shape=[jax.ShapeDtypeStruct((t, 512), F32)] * 3,
        scratch_shapes=[pltpu.VMEM((t, 128), F32)] * 6 + [pltpu.VMEM((len(PATTERNS), 2 * BLK, 2 * BLK), F32)],
        compiler_params=_params(("arbitrary",)),
    )(rb, bk, qkv, qkv, qkv, gate)


def _outproj_loss(x, ydn, yatt, wout, target):
    t = x.shape[0]
    tm = 512

    def body(x_ref, a_ref, b_ref, w_ref, t_ref, dy_ref, mix_ref, loss_ref):
        @pl.when(pl.program_id(0) == 0)
        def _():
            loss_ref[...] = jnp.zeros_like(loss_ref)

        mixf = jnp.concatenate([a_ref[...], b_ref[...]], axis=1)
        mix_ref[...] = mixf.T.astype(MXU)
        err = x_ref[...] + jnp.dot(mixf.astype(MXU), w_ref[...], preferred_element_type=F32) - t_ref[...]
        dy_ref[...] = err * (1.0 / D_MODEL)
        loss_ref[...] += jnp.sum(err * err) * (0.5 / D_MODEL)

    row = lambda n: pl.BlockSpec((tm, n), lambda i: (i, 0))
    return pl.pallas_call(
        body, name="outproj_loss", grid=(t // tm,),
        in_specs=[row(D_MODEL), row(512), row(512), pl.BlockSpec(wout.shape, lambda i: (0, 0)), row(D_MODEL)],
        out_specs=[row(D_MODEL), pl.BlockSpec((D_MODEL, tm), lambda i: (0, i)), pl.BlockSpec((8, 128), lambda i: (0, 0))],
        out_shape=[jax.ShapeDtypeStruct((t, D_MODEL), F32), jax.ShapeDtypeStruct((D_MODEL, t), MXU),
                   jax.ShapeDtypeStruct((8, 128), F32)],
        compiler_params=_params(("arbitrary",)),
    )(x, ydn, yatt, wout, target)


def _outproj_bwd(dy, wout_t, oraw, z, dnw, oatt, gate, g, gt):
    t = dy.shape[0]
    tm = 256

    def body(dy_ref, w_ref, o_ref, z_ref, dnw_ref, oa_ref, g_ref, grp_ref, grpt_ref,
             do_ref, dz_ref, doa_ref, dg_ref, dd_ref, ddnw_ref):
        @pl.when(pl.program_id(0) == 0)
        def _():
            ddnw_ref[...] = jnp.zeros_like(ddnw_ref)

        dmix = jnp.dot(dy_ref[...].astype(MXU), w_ref[...], preferred_element_type=F32)
        dnw_v = dnw_ref[...]
        acc = jnp.zeros((1, DK), F32)
        for h in range(DN_HEADS):
            sl = slice(h * DK, (h + 1) * DK)
            o, zz, dm = o_ref[:, sl], z_ref[:, sl], dmix[:, sl]
            rs = lax.rsqrt(jnp.mean(o * o, axis=1, keepdims=True) + EPS)
            oh = o * rs
            dz_ref[:, sl] = dm * oh * dnw_v * _dsilu(zz)
            d_on = dm * _silu(zz)
            gg = d_on * dnw_v
            do_ref[:, sl] = rs * (gg - oh * jnp.mean(gg * oh, axis=1, keepdims=True))
            acc = acc + jnp.sum(d_on * oh, axis=0, keepdims=True)
        ddnw_ref[...] += jnp.broadcast_to(acc, (8, DK))
        da, gate_v, oa = dmix[:, 512:], g_ref[...], oa_ref[...]
        doa = da * _silu(gate_v)
        doa_ref[...] = doa
        dg_ref[...] = da * oa * _dsilu(gate_v)
        dd_ref[...] = _hdot(_hdot(doa * oa, grp_ref[...], "b"), grpt_ref[...], "b")

    row = lambda n: pl.BlockSpec((tm, n), lambda i: (i, 0))
    full = lambda a: pl.BlockSpec(a.shape, lambda i: (0,) * a.ndim)
    return pl.pallas_call(
        body, name="outproj_bwd", grid=(t // tm,),
        in_specs=[row(D_MODEL), full(wout_t), row(512), row(512), full(dnw), row(512), row(512), full(g), full(gt)],
        out_specs=[row(512)] * 5 + [pl.BlockSpec((8, DK), lambda i: (0, 0))],
        out_shape=[jax.ShapeDtypeStruct((t, 512), F32)] * 5 + [jax.ShapeDtypeStruct((8, DK), F32)],
        compiler_params=_params(("arbitrary",)),
    )(dy, wout_t, oraw, z, dnw, oatt, gate, g, gt)


def _grad_matmul(at, b, name):
    m, t = at.shape
    n = b.shape[1]
    tk = 512
    tn = n if n <= 512 else 512
    nk = t // tk

    def body(a_ref, b_ref, o_ref, acc_ref):
        k = pl.program_id(1)

        @pl.when(k == 0)
        def _():
            acc_ref[...] = jnp.zeros_like(acc_ref)

        acc_ref[...] += jnp.dot(a_ref[...], b_ref[...].astype(MXU), preferred_element_type=F32)

        @pl.when(k == nk - 1)
        def _():
            o_ref[...] = acc_ref[...].astype(GRAD_WIRE)

    return pl.pallas_call(
        body, name=name, grid=(n // tn, nk),
        in_specs=[pl.BlockSpec((m, tk), lambda j, k: (0, k)), pl.BlockSpec((tk, tn), lambda j, k: (k, j))],
        out_specs=pl.BlockSpec((m, tn), lambda j, k: (0, j)),
        out_shape=jax.ShapeDtypeStruct((m, n), GRAD_WIRE),
        scratch_shapes=[pltpu.VMEM((m, tn), F32)],
        compiler_params=_params(("arbitrary", "arbitrary")),
    )(at, b)


def _att_bwd(qkv, do, lse, dd, rb, bk):
    t = qkv.shape[0]
    rows = 512

    def body(rb_ref, bk_ref, q_ref, k_ref, v_ref, do_ref, lse_ref, dd_ref,
             dq_ref, dk_ref, dv_ref, db_ref, bias_ref, ds_ref):
        pair = pl.program_id(0)

        @pl.when(pair == 0)
        def _():
            db_ref[...] = jnp.zeros_like(db_ref)

        _bias_tables(rb_ref, bk_ref, bias_ref, pair)
        ds_ref[...] = jnp.zeros_like(ds_ref)
        for c in range(t // rows):
            sl = slice(c * rows, (c + 1) * rows)
            for ref in (dq_ref, dk_ref, dv_ref):
                ref[sl, :] = jnp.zeros((rows, 128), F32)
        h0 = _iota((BLK, 128), 1) < HD
        prev_cols = _iota((2 * BLK, 2 * BLK), 1) < BLK

        def rows_of(xb):
            return jnp.concatenate([xb[:, 0:1], xb[:, HD:HD + 1]], axis=0)

        for p, (_, r) in enumerate(PATTERNS):
            def blk(n, carry, p=p, r=r):
                cur, prev, j = _block_rows(t, r, n)
                q2, do2 = _stack_heads(q_ref[cur, :], h0), _stack_heads(do_ref[cur, :], h0)
                k2 = jnp.concatenate([k_ref[prev, :], k_ref[cur, :]], axis=0).astype(MXU)
                v2 = jnp.concatenate([v_ref[prev, :], v_ref[cur, :]], axis=0).astype(MXU)
                s = _dot_nt(q2, k2) + bias_ref[p] + jnp.where(prev_cols & (j == 0), NEG, 0.0)
                prob = jnp.exp(s - rows_of(lse_ref[cur, :]))
                ds = prob * (_dot_nt(do2, v2) - rows_of(dd_ref[cur, :]))
                ds_ref[p] += ds
                dq2 = _dot(ds, k2)
                dk2 = _dot_tn(ds, q2)
                dv2 = _dot_tn(prob, do2)
                dq_ref[cur, :] += jnp.where(h0, dq2[:BLK], dq2[BLK:])
                dk_ref[prev, :] += dk2[:BLK]
                dv_ref[prev, :] += dv2[:BLK]
                dk_ref[cur, :] += dk2[BLK:]
                dv_ref[cur, :] += dv2[BLK:]
                return carry

            lax.fori_loop(0, t // BLK, blk, 0, unroll=4)

        ri, ci = _iota((8, 128), 0), _iota((8, 128), 1)
        upd = jnp.zeros((8, 128), F32)
        for p in range(len(PATTERNS)):
            bk = bk_ref[p]
            for hh in range(2):
                dsum = ds_ref[p, hh * BLK:(hh + 1) * BLK, :]
                for b in range(N_BUCKETS):
                    val = jnp.sum(jnp.where(bk == b, dsum, 0.0))
                    upd = upd + jnp.where((ri == 2 * pair + hh) & (ci == b), val, 0.0)
        db_ref[...] += upd

    col = lambda off: pl.BlockSpec((t, 128), lambda i, off=off: (0, off + i))
    return pl.pallas_call(
        body, name="att_bwd", grid=(ATT_HEADS // 2,),
        in_specs=[pl.BlockSpec(memory_space=pltpu.SMEM), pl.BlockSpec(bk.shape, lambda i: (0, 0, 0)),
                  col(0), col(4), col(8), col(0), col(0), col(0)],
        out_specs=[col(0), col(0), col(0), pl.BlockSpec((8, 128), lambda i: (0, 0))],
        out_shape=[jax.ShapeDtypeStruct((t, 512), F32)] * 3 + [jax.ShapeDtypeStruct((8, 128), F32)],
        scratch_shapes=[pltpu.VMEM((len(PATTERNS), 2 * BLK, 2 * BLK), F32),
                        pltpu.VMEM((len(PATTERNS), 2 * BLK, 2 * BLK), F32)],
        compiler_params=_params(("arbitrary",)),
    )(rb, bk, qkv, qkv, qkv, do, lse, dd)


def _att_prep_bwd(patt, dq, dk, dv, g, gt, wq, wk):
    t = patt.shape[0]
    tm = 512

    def body(p_ref, dq_ref, dk_ref, dv_ref, g_ref, gt_ref, wq_ref, wk_ref, o_ref, dwq_ref, dwk_ref):
        @pl.when(pl.program_id(0) == 0)
        def _():
            dwq_ref[...] = jnp.zeros_like(dwq_ref)
            dwk_ref[...] = jnp.zeros_like(dwk_ref)

        for lo, w_ref, d_ref, dw_ref in ((0, wq_ref, dq_ref, dwq_ref), (512, wk_ref, dk_ref, dwk_ref)):
            xv, dyv = p_ref[:, lo:lo + 512], d_ref[...]
            rstd = lax.rsqrt(_hdot(xv * xv, g_ref[...], "b") * (1.0 / HD) + EPS)
            rsb = _hdot(rstd, gt_ref[...], "b")
            xh = xv * rsb
            gg = dyv * w_ref[...]
            mean = _hdot(_hdot(gg * xh, g_ref[...], "b") * (1.0 / HD), gt_ref[...], "b")
            o_ref[:, lo:lo + 512] = rsb * (gg - xh * mean)
            dw_ref[...] += jnp.broadcast_to(jnp.sum(dyv * xh, axis=0, keepdims=True), (8, 512))
        o_ref[:, 1024:1536] = dv_ref[...]

    row = lambda n: pl.BlockSpec((tm, n), lambda i: (i, 0))
    full = lambda a: pl.BlockSpec(a.shape, lambda i: (0,) * a.ndim)
    acc = pl.BlockSpec((8, 512), lambda i: (0, 0))
    return pl.pallas_call(
        body, name="att_prep_bwd", grid=(t // tm,),
        in_specs=[row(1536), row(512), row(512), row(512), full(g), full(gt), full(wq), full(wk)],
        out_specs=[row(1536), acc, acc],
        out_shape=[jax.ShapeDtypeStruct((t, 1536), F32), jax.ShapeDtypeStruct((8, 512), F32),
                   jax.ShapeDtypeStruct((8, 512), F32)],
        compiler_params=_params(("arbitrary",)),
    )(patt, dq, dk, dv, g, gt, wq, wk)


def _dn_scan_bwd(qkv, ba, do, sh, th, arow, dtb):
    t = qkv.shape[0]
    n = t // CHUNK
    c = CHUNK
    cps = SCAN_CHUNKS

    def body(qkv_ref, ba_ref, do_ref, sh_ref, th_ref, arow_ref, dtb_ref, dqkv_ref, dba_ref, ds_ref):
        @pl.when(pl.program_id(0) == 0)
        def _():
            ds_ref[...] = jnp.zeros_like(ds_ref)

        hs = range(DN_HEADS)
        chains = [(j, h) for j in range(cps) for h in hs]
        lane = _iota((c, 128), 1)
        row = _iota((c, 1), 0)
        ms = {}
        for j in range(cps):
            rows_j = slice(j * c, (j + 1) * c)
            mj, ri, ci = _chunk_common(qkv_ref[rows_j, :], ba_ref[rows_j, :], arow_ref[...], dtb_ref[...])
            for h in hs:
                ms[j, h] = mj[h]
        q, k, v = ({x: ms[x][nm] for x in chains} for nm in ("q", "k", "v"))
        beta, decay = ({x: ms[x][nm] for x in chains} for nm in ("beta", "decay"))
        eg, egl, etail = ({x: ms[x][nm] for x in chains} for nm in ("eg", "egl", "etail"))
        s = {x: sh_ref[x[0], x[1]] for x in chains}
        tinv = {x: th_ref[x[0], x[1]] for x in chains}
        d_o = {(j, h): do_ref[j * c:(j + 1) * c, h * DK:(h + 1) * DK] for j, h in chains}
        kb = {x: k[x] * beta[x] for x in chains}
        vb = {x: v[x] * beta[x] for x in chains}
        kbg = {x: kb[x] * eg[x] for x in chains}
        amat = {x: jnp.where(ri > ci, _dot_nt(kb[x], k[x]) * decay[x], 0.0) for x in chains}
        attn = {x: jnp.where(ri >= ci, _dot_nt(q[x], k[x]) * decay[x], 0.0) for x in chains}
        u = {x: _hdot(tinv[x], vb[x]) for x in chains}
        w = {x: _hdot(tinv[x], kbg[x]) for x in chains}
        v_new = {x: u[x] - _dot(w[x], s[x]) for x in chains}
        q_dec = {x: q[x] * eg[x] for x in chains}
        k_tail = {x: k[x] * etail[x] for x in chains}
        d_attn = {x: jnp.where(ri >= ci, _dot_nt(d_o[x], v_new[x]), 0.0) for x in chains}
        d_qdec = {x: _dot_nt(d_o[x], s[x]) for x in chains}
        from_o = {x: _dot_tn(attn[x], d_o[x]) for x in chains}
        to_state = {x: _dot_tn(q_dec[x], d_o[x]) for x in chains}

        d_s, d_vnew = {}, {}
        cur = [ds_ref[h] for h in hs]
        for j in reversed(range(cps)):
            for h in hs:
                d_s[j, h] = cur[h]
                d_vnew[j, h] = from_o[j, h] + _dot(k_tail[j, h], cur[h])
            cur = [to_state[j, h] + cur[h] * egl[j, h] - _dot_tn(w[j, h], d_vnew[j, h]) for h in hs]
        for h in hs:
            ds_ref[h] = cur[h]

        d_ktail = {x: _dot_nt(v_new[x], d_s[x]) for x in chains}
        d_gl = {x: jnp.sum(s[x] * d_s[x]) * egl[x] for x in chains}
        d_w = {x: -_dot_nt(d_vnew[x], s[x]) for x in chains}
        d_vb = {x: _hdot_tn(tinv[x], d_vnew[x]) for x in chains}
        d_kbg = {x: _hdot_tn(tinv[x], d_w[x]) for x in chains}
        d_a = {x: -jnp.where(ri > ci, _hdot_nt(d_vb[x], u[x]) + _hdot_nt(d_kbg[x], w[x]), 0.0) for x in chains}
        d_qk = {x: d_attn[x] * decay[x] for x in chains}
        d_kk = {x: d_a[x] * decay[x] for x in chains}
        d_kb = {x: _dot(d_kk[x], k[x]) + d_kbg[x] * eg[x] for x in chains}
        d_q = {x: _dot(d_qk[x], k[x]) + d_qdec[x] * eg[x] for x in chains}
        d_k = {x: _dot_tn(d_qk[x], q[x]) + _dot_tn(d_kk[x], kb[x]) + d_ktail[x] * etail[x] + d_kb[x] * beta[x]
               for x in chains}
        d_beta = {x: jnp.sum(d_kb[x] * k[x] + d_vb[x] * v[x], axis=1, keepdims=True) for x in chains}
        mm = {x: d_a[x] * amat[x] + d_attn[x] * attn[x] for x in chains}
        for j in range(cps):
            rows_j = slice(j * c, (j + 1) * c)
            rows = jnp.zeros((c, c), F32)
            for h in hs:
                rows = rows + jnp.where(ri == h, jnp.sum(mm[j, h], axis=0, keepdims=True), 0.0)
            cols_t = jnp.concatenate([rows, jnp.zeros((c, c), F32)], axis=1).T[:c, :]
            d_gc_all = jnp.zeros((c, 128), F32)
            for h in hs:
                x = (j, h)
                tail_term = jnp.sum(d_ktail[x] * k_tail[x], axis=1, keepdims=True)
                d_gc = (jnp.sum(mm[x], axis=1, keepdims=True) - _lane_col(cols_t, h)
                        + jnp.sum(d_qdec[x] * q_dec[x] + d_kbg[x] * kbg[x], axis=1, keepdims=True) - tail_term)
                d_gc = d_gc + jnp.where(row == c - 1, jnp.sum(tail_term) + d_gl[x], 0.0)
                d_gc_all = d_gc_all + jnp.where(lane == DN_HEADS + h, d_gc, 0.0)
            d_g_all = _hdot((ri <= ci).astype(F32), d_gc_all, "a")
            dba = jnp.zeros((c, 128), F32)
            for h in hs:
                x = (j, h)
                d_g = _lane_col(d_g_all, DN_HEADS + h)
                d_braw = d_beta[x] * beta[x] * (1.0 - beta[x])
                d_araw = d_g * ms[x]["a_h"] * _sigmoid(ms[x]["a_raw"] + ms[x]["dt_h"])
                dba = dba + jnp.where(lane == h, d_braw, 0.0) + jnp.where(lane == DN_HEADS + h, d_araw, 0.0) \
                    + jnp.where(lane == 2 * DN_HEADS + h, d_g * ms[x]["g"], 0.0)
                dqkv_ref[rows_j, h * DK:(h + 1) * DK] = d_q[x]
                dqkv_ref[rows_j, D_DN + h * DK:D_DN + (h + 1) * DK] = d_k[x]
                dqkv_ref[rows_j, 2 * D_DN + h * DK:2 * D_DN + (h + 1) * DK] = d_vb[x] * beta[x]
            dba_ref[rows_j, :] = dba

    nsteps = n // cps
    rev = lambda w_: pl.BlockSpec((cps * c, w_), lambda i: (nsteps - 1 - i, 0))
    one = pl.BlockSpec((1, 128), lambda i: (0, 0))
    return pl.pallas_call(
        body, name="dn_scan_bwd", grid=(nsteps,),
        in_specs=[rev(1536), rev(128), rev(512),
                  pl.BlockSpec((cps, DN_HEADS, DK, DK), lambda i: (nsteps - 1 - i, 0, 0, 0)),
                  pl.BlockSpec((cps, DN_HEADS, c, c), lambda i: (nsteps - 1 - i, 0, 0, 0)), one, one],
        out_specs=[rev(1536), rev(128)],
        out_shape=[jax.ShapeDtypeStruct((t, 1536), F32), jax.ShapeDtypeStruct((t, 128), F32)],
        scratch_shapes=[pltpu.VMEM((DN_HEADS, DK, DK), F32)],
        compiler_params=_params(("arbitrary",)),
    )(qkv, ba, do, sh, th, arow, dtb)


def _dn_prep_bwd(pdn, cw, dact):
    t = pdn.shape[0]
    nchunk = t // CONV_ROWS

    def body(u_ref, w_ref, d_ref, du_ref, dw_ref, dy_ref):
        j = pl.program_id(0)
        dy_ref[t:t + 8, :] = jnp.zeros((8, 128), F32)
        dw = [jnp.zeros((1, 128), F32) for _ in range(4)]
        for c in range(nchunk):
            sl = slice(c * CONV_ROWS, (c + 1) * CONV_ROWS)
            taps, y = _conv_taps(u_ref, c, w_ref)
            a = _silu(y)
            dout = d_ref[sl, :]
            rs = lax.rsqrt(jnp.sum(a * a, axis=1, keepdims=True) + EPS)
            f = jnp.where(j < 8, rs, 1.0) * jnp.where(j < 4, DK ** -0.5, 1.0)
            corr = jnp.where(j < 8, f * rs * rs * jnp.sum(dout * a, axis=1, keepdims=True), 0.0)
            dy = (f * dout - corr * a) * _dsilu(y)
            dy_ref[sl, :] = dy
            for k_ in range(4):
                dw[3 - k_] = dw[3 - k_] + jnp.sum(taps[k_] * dy, axis=0, keepdims=True)
        for i in range(4):
            dw_ref[i:i + 1, :] = dw[i]
        for c in range(nchunk):
            r0 = c * CONV_ROWS
            ext = dy_ref[r0:r0 + CONV_ROWS + 8, :]
            du = ext[:CONV_ROWS, :] * w_ref[3:4, :]
            for k_ in (1, 2, 3):
                du = du + pltpu.roll(ext, CONV_ROWS + 8 - k_, 0)[:CONV_ROWS, :] * w_ref[3 - k_:4 - k_, :]
            du_ref[r0:r0 + CONV_ROWS, :] = du

    return pl.pallas_call(
        body, name="dn_prep_bwd", grid=(12,),
        in_specs=[pl.BlockSpec((t, 128), lambda j: (0, j)), pl.BlockSpec((4, 128), lambda j: (0, j)),
                  pl.BlockSpec((t, 128), lambda j: (0, j))],
        out_specs=[pl.BlockSpec((t, 128), lambda j: (0, j)), pl.BlockSpec((4, 128), lambda j: (0, j))],
        out_shape=[jax.ShapeDtypeStruct((t, 1536), F32), jax.ShapeDtypeStruct((4, 1536), F32)],
        scratch_shapes=[pltpu.VMEM((t + 8, 128), F32)],
        compiler_params=_params(("arbitrary",)),
    )(pdn, cw, dact)


def _inproj_bwd(x, nw, wt, dy, dpdn, dz, dpatt, dgate, dba, partials):
    t = x.shape[0]
    tm = 256
    npart = len(partials)
    nsteps = t // tm

    def body(x_ref, nw_ref, w_ref, dy_ref, a_ref, b_ref, c_ref, d_ref, e_ref, *rest):
        part_refs, (gx_ref, dnw_ref, cs_ref) = rest[:npart], rest[npart:npart + 3]
        got_refs, (send, recv, loc) = rest[npart + 3:2 * npart + 3], rest[2 * npart + 3:]
        starts, waits = _chip_swap_copies(part_refs, got_refs, send, recv, loc)

        @pl.when(pl.program_id(0) == 0)
        def _():
            for start in starts:
                start()
            dnw_ref[...] = jnp.zeros_like(dnw_ref)
            cs_ref[...] = jnp.zeros_like(cs_ref)

        @pl.when(pl.program_id(0) == nsteps - 1)
        def _():
            for wait in waits:
                wait()

        dh = jnp.zeros((tm, D_MODEL), F32)
        for ref, lo, hi in ((a_ref, 0, 1536), (b_ref, 1536, 2048), (c_ref, 2048, 3584),
                            (d_ref, 3584, 4096), (e_ref, 4096, 4224)):
            dh = dh + jnp.dot(ref[...].astype(MXU), w_ref[lo:hi, :], preferred_element_type=F32)
        xv = x_ref[...]
        rstd = lax.rsqrt(jnp.mean(xv * xv, axis=-1, keepdims=True) + EPS)
        xh = xv * rstd
        gg = dh * nw_ref[...]
        gx_ref[...] = rstd * (gg - xh * jnp.mean(gg * xh, axis=-1, keepdims=True)) + dy_ref[...]
        dnw_ref[...] += jnp.broadcast_to(jnp.sum(dh * xh, axis=0, keepdims=True), (8, D_MODEL))
        cs_ref[...] += jnp.broadcast_to(jnp.sum(e_ref[...], axis=0, keepdims=True), (8, 128))

    row = lambda n: pl.BlockSpec((tm, n), lambda i: (i, 0))
    full = lambda a: pl.BlockSpec(a.shape, lambda i: (0,) * a.ndim)
    res = pl.pallas_call(
        body, name="inproj_bwd", grid=(nsteps,),
        in_specs=[row(D_MODEL), full(nw), full(wt), row(D_MODEL), row(1536), row(512), row(1536), row(512), row(128)]
                 + [ANY_SPEC] * npart,
        out_specs=[row(D_MODEL), pl.BlockSpec((8, D_MODEL), lambda i: (0, 0)), pl.BlockSpec((8, 128), lambda i: (0, 0))]
                  + [ANY_SPEC] * npart,
        out_shape=[jax.ShapeDtypeStruct((t, D_MODEL), F32), jax.ShapeDtypeStruct((8, D_MODEL), F32),
                   jax.ShapeDtypeStruct((8, 128), F32)] + [jax.ShapeDtypeStruct(p.shape, p.dtype) for p in partials],
        scratch_shapes=[pltpu.SemaphoreType.DMA((npart, 3)), pltpu.SemaphoreType.DMA((npart, 3)),
                        pltpu.SemaphoreType.DMA((npart,))],
        compiler_params=_params(("arbitrary",)),
    )(x, nw, wt, dy, dpdn, dz, dpatt, dgate, dba, *partials)
    return res[0], res[1], res[2], res[3:]


def _adamw_sum(w, gs, m, v, name):
    r, c = w.shape
    nsum = gs.shape[0]
    tr = r if r <= 256 else 256
    c1 = 1.0 - ADAM_B1 ** ADAM_STEP
    c2 = 1.0 - ADAM_B2 ** ADAM_STEP

    def body(w_ref, g_ref, m_ref, v_ref, go_ref, d_ref, mo_ref, vo_ref):
        g = g_ref[0].astype(F32)
        for s in range(1, nsum):
            g = g + g_ref[s].astype(F32)
        mn = ADAM_B1 * m_ref[...] + (1.0 - ADAM_B1) * g
        vn = ADAM_B2 * v_ref[...] + (1.0 - ADAM_B2) * (g * g)
        go_ref[...] = g
        mo_ref[...] = mn
        vo_ref[...] = vn
        d_ref[...] = -ADAM_LR * ((mn / c1) / (jnp.sqrt(vn / c2) + ADAM_EPS) + ADAM_WD * w_ref[...])

    blk = pl.BlockSpec((tr, c), lambda i: (i, 0))
    return pl.pallas_call(
        body, name=name, grid=(r // tr,),
        in_specs=[blk, pl.BlockSpec((nsum, tr, c), lambda i: (0, i, 0)), blk, blk],
        out_specs=[blk] * 4, out_shape=[jax.ShapeDtypeStruct((r, c), F32)] * 4,
        compiler_params=_params(("arbitrary",)),
    )(w, gs, m, v)


def _local_step(x, target, norm_w, w_sect, w_sect_t, conv_w, a_log, dt_bias, dn_norm_w, q_norm_w, k_norm_w, rel_bias,
                w_out):
    lane = np.arange(128)
    arow = jnp.zeros((1, 128), F32).at[0, DN_HEADS:2 * DN_HEADS].set(-jnp.exp(a_log[0]))
    dtb = jnp.zeros((1, 128), F32).at[0, DN_HEADS:2 * DN_HEADS].set(dt_bias[0])
    g_np, gt_np = _group_mats()
    g, gt = jnp.asarray(g_np), jnp.asarray(gt_np)
    bk = jnp.asarray(_bucket_tables())
    wq = jnp.tile(q_norm_w, (1, ATT_HEADS)) * (HD ** -0.5)
    wk = jnp.tile(k_norm_w, (1, ATT_HEADS))
    del lane

    ht, pdn, z, patt, gate, ba = _inproj(x, norm_w, w_sect)
    qkv_dn = _dn_prep(pdn, conv_w)
    oraw, ydn, sh, th = _dn_scan_fwd(qkv_dn, ba, z, arow, dtb, dn_norm_w)
    qkv_att = _att_prep(patt, g, gt, wq, wk)
    oatt, yatt, lse = _att_fwd(qkv_att, gate, rel_bias, bk)
    dy, mix_t, loss8 = _outproj_loss(x, ydn, yatt, w_out, target)

    do_dn, dz, do_att, dgate, dd, ddnw = _outproj_bwd(dy, w_out.T, oraw, z, dn_norm_w, oatt, gate, g, gt)
    d_wout = _grad_matmul(mix_t, dy, "dw_out")
    dq, dk, dv, drb = _att_bwd(qkv_att, do_att, lse, dd, rel_bias, bk)
    dpatt, dwq8, dwk8 = _att_prep_bwd(patt, dq, dk, dv, g, gt, wq, wk)
    dqkv_dn, dba = _dn_scan_bwd(qkv_dn, ba, do_dn, sh, th, arow, dtb)
    dpdn, d_conv = _dn_prep_bwd(pdn, conv_w, dqkv_dn)
    dw_sections = [_grad_matmul(ht, d_, nm) for d_, nm in
                   ((dpdn, "dw_in_dn"), (dz, "dw_in_z"), (dpatt, "dw_in_att"), (dgate, "dw_in_gate"), (dba, "dw_in_ba"))]
    return dict(loss=loss8[0, 0], w_in_sections=dw_sections, conv_w=d_conv, w_out=d_wout,
                dproj=(dy, dpdn, dz, dpatt, dgate, dba), small_parts=(ddnw, dwq8, dwk8, drb))


def _finish_step(x, norm_w, w_sect_t, gr, partials):
    grad_x, dnw8, cs8, got = _inproj_bwd(x, norm_w, w_sect_t, *gr["dproj"], partials)
    return grad_x, _pack_small_grads(dnw8, cs8, *gr["small_parts"]), got


SMALL_ROWS = 24
SMALL_AT = dict(a_log=(slice(8, 9), slice(0, 4)), dt_bias=(slice(9, 10), slice(0, 4)),
                dn_norm_w=(slice(10, 11), slice(0, 128)), q_norm_w=(slice(11, 12), slice(0, HD)),
                k_norm_w=(slice(12, 13), slice(0, HD)), rel_bias=(slice(16, 24), slice(0, N_BUCKETS)))
SMALL_NAMES = ("norm_w", "a_log", "dt_bias", "dn_norm_w", "q_norm_w", "k_norm_w", "rel_bias")


def _pack_small_grads(dnw8, cs8, ddnw8, dwq8, dwk8, drb):
    def body(dnw_ref, cs_ref, ddnw_ref, dwq_ref, dwk_ref, drb_ref, o_ref):
        lane = _iota((8, 128), 1)
        o_ref[...] = jnp.zeros_like(o_ref)
        for k in range(D_MODEL // 128):
            o_ref[k:k + 1, :] = dnw_ref[0:1, k * 128:(k + 1) * 128]
        cs = cs_ref[...]
        o_ref[8:9, :] = jnp.where(lane < DN_HEADS, pltpu.roll(cs, 128 - 2 * DN_HEADS, 1), 0.0)[0:1, :]
        o_ref[9:10, :] = jnp.where(lane < DN_HEADS, pltpu.roll(cs, 128 - DN_HEADS, 1), 0.0)[0:1, :]
        o_ref[10:11, :] = ddnw_ref[0:1, :]
        for row, ref, scale in ((11, dwq_ref, HD ** -0.5), (12, dwk_ref, 1.0)):
            acc = ref[:, 0:128] + ref[:, 128:256] + ref[:, 256:384] + ref[:, 384:512]
            acc = (acc + pltpu.roll(acc, HD, 1)) * scale
            o_ref[row:row + 1, :] = jnp.where(lane < HD, acc, 0.0)[0:1, :]
        o_ref[16:24, :] = drb_ref[...]

    return pl.pallas_call(body, name="pack_small_grads", out_shape=jax.ShapeDtypeStruct((SMALL_ROWS, 128), F32),
                          )(dnw8, cs8, ddnw8, dwq8, dwk8, drb)


def _adam_math(w, g, m, v):
    c1 = 1.0 - ADAM_B1 ** ADAM_STEP
    c2 = 1.0 - ADAM_B2 ** ADAM_STEP
    mn = ADAM_B1 * m + (1.0 - ADAM_B1) * g
    vn = ADAM_B2 * v + (1.0 - ADAM_B2) * (g * g)
    return -ADAM_LR * ((mn / c1) / (jnp.sqrt(vn / c2) + ADAM_EPS) + ADAM_WD * w), mn, vn


def _adamw_small(gs, ws, ms, vs):
    n = len(SMALL_NAMES)

    def body(g_ref, *refs):
        w_refs, m_refs, v_refs = refs[:n], refs[n:2 * n], refs[2 * n:3 * n]
        outs = refs[3 * n:]

        def one(i, rows, lanes, at):
            g = g_ref[0, rows, lanes]
            for s in range(1, gs.shape[0]):
                g = g + g_ref[s, rows, lanes]
            d, mn, vn = _adam_math(w_refs[i][at], g, m_refs[i][at], v_refs[i][at])
            for kind, val in enumerate((g, d, mn, vn)):
                outs[kind * n + i][at] = val

        for k in range(D_MODEL // 128):
            one(0, slice(k, k + 1), slice(0, 128), (slice(0, 1), slice(k * 128, (k + 1) * 128)))
        for i, nm in enumerate(SMALL_NAMES[1:], start=1):
            rows, lanes = SMALL_AT[nm]
            one(i, rows, lanes, (slice(None), slice(None)))

    shapes = [jax.ShapeDtypeStruct(w.shape, F32) for w in ws]
    res = pl.pallas_call(body, name="adamw_small", out_shape=shapes * 4)(gs, *ws, *ms, *vs)
    return [res[k * n:(k + 1) * n] for k in range(4)]


def kernel(x, norm_w, w_in, conv_w, a_log, dt_bias, dn_norm_w, q_norm_w, k_norm_w, rel_bias, w_out, loss_target, m_norm_w, m_w_in, m_conv_w, m_a_log, m_dt_bias, m_dn_norm_w, m_q_norm_w, m_k_norm_w, m_rel_bias, m_w_out, v_norm_w, v_w_in, v_conv_w, v_a_log, v_dt_bias, v_dn_norm_w, v_q_norm_w, v_k_norm_w, v_rel_bias, v_w_out):
    assert w_in.shape[2] == SHARD_COLS
    win8, wout8, conv8 = _gather_weights([w_in[0].astype(MXU), w_out[0].astype(MXU), conv_w[0]])
    w_sect, w_sect_t = _build_w(win8)
    wout_full = wout8.reshape(D_MODEL, D_MODEL)
    conv_full = conv8.transpose(1, 0, 2).reshape(4, 3 * D_DN)

    gr = _local_step(x[0], loss_target[0], norm_w, w_sect, w_sect_t, conv_full, a_log, dt_bias, dn_norm_w, q_norm_w,
                     k_norm_w, rel_bias, wout_full)

    slabs = [_build_slabs(*gr["w_in_sections"]),
             gr["w_out"].reshape(4, 2, D_MODEL // N_DEV, D_MODEL).transpose(1, 0, 2, 3),
             gr["conv_w"].reshape(4, 4, 2, 3 * D_DN // N_DEV).transpose(2, 1, 0, 3)]
    core = lax.axis_index("c").astype(jnp.int32).reshape(1)
    from_sibling = _swap_siblings(slabs)
    wires = (GRAD_WIRE, GRAD_WIRE, F32)
    partial = [_chip_sum(slabs[i], from_sibling[i], core, wires[i], "chip_sum_%d" % i) for i in range(3)]
    grad_x, small_pack, (r_win, r_wout, r_conv) = _finish_step(x[0], norm_w, w_sect_t, gr, partial)
    r_small = _share_small(small_pack)

    g_win, d_win, m_win, v_win = _adamw_sum(w_in[0], r_win, m_w_in[0], v_w_in[0], "adamw_w_in")
    g_wout, d_wout, m_wout, v_wout = _adamw_sum(w_out[0], r_wout, m_w_out[0], v_w_out[0], "adamw_w_out")
    g_conv, d_conv, m_conv, v_conv = _adamw_sum(conv_w[0], r_conv, m_conv_w[0], v_conv_w[0], "adamw_conv_w")
    small = _adamw_small(r_small,
                         (norm_w, a_log, dt_bias, dn_norm_w, q_norm_w, k_norm_w, rel_bias),
                         (m_norm_w, m_a_log, m_dt_bias, m_dn_norm_w, m_q_norm_w, m_k_norm_w, m_rel_bias),
                         (v_norm_w, v_a_log, v_dt_bias, v_dn_norm_w, v_q_norm_w, v_k_norm_w, v_rel_bias))

    loss = lax.psum(gr["loss"], AXES)
    names = ("norm_w", "w_in", "conv_w", "a_log", "dt_bias", "dn_norm_w", "q_norm_w", "k_norm_w", "rel_bias", "w_out")
    big = dict(w_in=(g_win, d_win, m_win, v_win), conv_w=(g_conv, d_conv, m_conv, v_conv),
               w_out=(g_wout, d_wout, m_wout, v_wout))
    outs = [loss, grad_x[None]]
    for kind in range(4):
        for nm in names:
            outs.append(big[nm][kind][None] if nm in big else small[kind][SMALL_NAMES.index(nm)])
    return tuple(outs)
```

```python
import functools
import math

import numpy as np
import jax
import jax.numpy as jnp
from jax import lax
from jax.experimental import pallas as pl
from jax.experimental.pallas import tpu as pltpu

F32 = jnp.float32
MXU = jnp.bfloat16
GRAD_WIRE = jnp.bfloat16
HI = lax.Precision.HIGHEST

D_MODEL = 1024
D_DN = 512
DN_HEADS = 4
DK = 128
CHUNK = 64
D_ATT = 512
ATT_HEADS = 8
HD = 64
PATTERNS = ((128, 1), (512, 4), (2048, 16))
BLK = 128
N_BUCKETS = 32
MAX_DISTANCE = 2048
EPS = 1e-6
W_COLS = 4224
N_DEV = 8
AXES = ("x", "y", "c")

ADAM_LR = 0.001
ADAM_B1 = 0.9
ADAM_B2 = 0.999
ADAM_EPS = 1e-08
ADAM_WD = 0.01
ADAM_STEP = 10

VMEM_LIMIT = 56 * 1024 * 1024
NEG = -1e30


def _dot(a, b):
    return jnp.dot(a.astype(MXU), b.astype(MXU), preferred_element_type=F32)


def _dot_nt(a, b):
    return lax.dot_general(a.astype(MXU), b.astype(MXU), (((1,), (1,)), ((), ())), preferred_element_type=F32)


def _dot_tn(a, b):
    return lax.dot_general(a.astype(MXU), b.astype(MXU), (((0,), (0,)), ((), ())), preferred_element_type=F32)


def _split(a):
    hi = a.astype(jnp.bfloat16)
    return hi, (a - hi.astype(F32)).astype(jnp.bfloat16)


def _dot_split(a, b, dims, exact):
    dg = lambda u, v: lax.dot_general(u, v, (dims, ((), ())), preferred_element_type=F32)
    if exact == "b":
        ah, al = _split(a)
        bh = b.astype(jnp.bfloat16)
        return dg(ah, bh) + dg(al, bh)
    if exact == "a":
        bh, bm = _split(b)
        bl = (b - bh.astype(F32) - bm.astype(F32)).astype(jnp.bfloat16)
        ah = a.astype(jnp.bfloat16)
        return dg(ah, bh) + (dg(ah, bm) + dg(ah, bl))
    ah, al = _split(a)
    bh, bl = _split(b)
    return dg(ah, bh) + (dg(ah, bl) + dg(al, bh))


def _hdot(a, b, exact=None):
    return _dot_split(a, b, ((1,), (0,)), exact)


def _hdot_nt(a, b, exact=None):
    return _dot_split(a, b, ((1,), (1,)), exact)


def _hdot_tn(a, b, exact=None):
    return _dot_split(a, b, ((0,), (0,)), exact)


def _sigmoid(x):
    return 1.0 / (1.0 + jnp.exp(-x))


def _silu(x):
    return x * _sigmoid(x)


def _silu_and_grad(x):
    s = _sigmoid(x)
    return x * s, s * (1.0 + x * (1.0 - s))


def _softplus(x):
    return jnp.maximum(x, 0.0) + jnp.log(1.0 + jnp.exp(-jnp.abs(x)))


def _iota(shape, dim):
    return lax.broadcasted_iota(jnp.int32, shape, dim)


def _lane_col(x, k):
    return jnp.sum(jnp.where(_iota(x.shape, 1) == k, x, 0.0), axis=1, keepdims=True)


def _params(sem=None):
    return pltpu.CompilerParams(dimension_semantics=sem, vmem_limit_bytes=VMEM_LIMIT)


def _t5_bucket(dist):
    max_exact = N_BUCKETS // 2
    d = np.maximum(dist, 1).astype(np.float64)
    large = max_exact + (np.log(d / max_exact) / math.log(MAX_DISTANCE / max_exact)
                         * (N_BUCKETS - max_exact)).astype(np.int32)
    large = np.minimum(large, N_BUCKETS - 1)
    return np.where(dist < max_exact, dist, large).astype(np.int32)


def _bucket_tables():
    qi = np.arange(BLK)[:, None]
    kj = np.arange(2 * BLK)[None, :]
    step = qi - kj + BLK
    band = (step >= 0) & (step <= BLK)
    out = []
    for _, r in PATTERNS:
        b = _t5_bucket(np.clip(step, 0, None) * r)
        out.append(np.where(band, b, -1))
    return np.stack(out).astype(np.int32)


def _group_mats():
    g = np.zeros((D_ATT, 128), np.float32)
    for h in range(ATT_HEADS):
        g[h * HD:(h + 1) * HD, h] = 1.0
    return g, np.ascontiguousarray(g.T)


CHIP_FLIPS = ((1, 0), (0, 1), (1, 1))
ANY_SPEC = pl.BlockSpec(memory_space=pl.ANY)
MESH_ID = pl.DeviceIdType.MESH


def _other_chips():
    x, y = lax.axis_index("x"), lax.axis_index("y")
    return [((1 - x if fx else x), (1 - y if fy else y)) for fx, fy in CHIP_FLIPS]


def _gather_weights(arrs):
    n = len(arrs)

    def body(*refs):
        ins, outs = refs[:n], refs[n:2 * n]
        send, recv, loc = refs[2 * n:]
        x, y, c = (lax.axis_index(a) for a in AXES)
        sib = (x, y, 1 - c)
        chips = _other_chips()
        lin = lambda px, py, pc: 4 * px + 2 * py + pc

        def copy(a, k, block, to, src=None):
            slot = outs[a].at[lin(*block)]
            return pltpu.make_async_remote_copy(src_ref=slot if src is None else src, dst_ref=slot,
                                                send_sem=send.at[a, k], recv_sem=recv.at[a, k],
                                                device_id=to, device_id_type=MESH_ID)

        started = []
        for a in range(n):
            mine = pltpu.make_async_copy(ins[a], outs[a].at[lin(x, y, c)], loc.at[a])
            mine.start()
            started.append(mine)
        firsts = []
        for a in range(n):
            firsts.append(copy(a, 0, (x, y, c), sib, src=ins[a]))
            firsts += [copy(a, 1 + j, (x, y, c), (*chip, c), src=ins[a]) for j, chip in enumerate(chips)]
        for cp in firsts:
            cp.start()
        passed = []
        for j, chip in enumerate(chips):
            for a in range(n):
                copy(a, 1 + j, (*chip, c), (x, y, c)).wait_recv()
                fw = copy(a, 4 + j, (*chip, c), sib)
                fw.start()
                passed.append(fw)
        for a in range(n):
            copy(a, 0, sib, (x, y, c)).wait_recv()
            for j, chip in enumerate(chips):
                copy(a, 4 + j, (*chip, 1 - c), (x, y, c)).wait_recv()
        for cp in firsts + passed:
            cp.wait_send()
        for mine in started:
            mine.wait()

    return pl.pallas_call(
        body, name="gather_weights", out_shape=[jax.ShapeDtypeStruct((N_DEV,) + a.shape, a.dtype) for a in arrs],
        in_specs=[ANY_SPEC] * n, out_specs=[ANY_SPEC] * n,
        scratch_shapes=[pltpu.SemaphoreType.DMA((n, 7)), pltpu.SemaphoreType.DMA((n, 7)), pltpu.SemaphoreType.DMA((n,))],
    )(*arrs)


def _swap_siblings(arrs):
    n = len(arrs)

    def body(*refs):
        ins, outs = refs[:n], refs[n:2 * n]
        send, recv = refs[2 * n:]
        x, y, c = (lax.axis_index(a) for a in AXES)
        cps = [pltpu.make_async_remote_copy(src_ref=ins[a].at[1 - c], dst_ref=outs[a], send_sem=send.at[a],
                                            recv_sem=recv.at[a], device_id=(x, y, 1 - c), device_id_type=MESH_ID)
               for a in range(n)]
        for cp in cps:
            cp.start()
        for cp in cps:
            cp.wait()

    return pl.pallas_call(
        body, name="swap_siblings", out_shape=[jax.ShapeDtypeStruct(a.shape[1:], a.dtype) for a in arrs],
        in_specs=[ANY_SPEC] * n, out_specs=[ANY_SPEC] * n,
        scratch_shapes=[pltpu.SemaphoreType.DMA((n,)), pltpu.SemaphoreType.DMA((n,))],
    )(*arrs)


def _chip_sum(mine2, theirs, core, wire, name):
    _, nchip, r, cdim = mine2.shape
    tr = r if r <= 256 else 256

    def body(core_ref, a_ref, b_ref, o_ref):
        del core_ref
        o_ref[...] = (a_ref[...].astype(F32) + b_ref[...].astype(F32)).astype(wire)

    grid_spec = pltpu.PrefetchScalarGridSpec(
        num_scalar_prefetch=1, grid=(nchip, r // tr),
        in_specs=[pl.BlockSpec((None, None, tr, cdim), lambda j, i, cr: (cr[0], j, i, 0)),
                  pl.BlockSpec((None, tr, cdim), lambda j, i, cr: (j, i, 0))],
        out_specs=pl.BlockSpec((None, tr, cdim), lambda j, i, cr: (j, i, 0)))
    return pl.pallas_call(
        body, name=name, grid_spec=grid_spec, out_shape=jax.ShapeDtypeStruct((nchip, r, cdim), wire),
        compiler_params=_params(("arbitrary", "arbitrary")),
    )(core, mine2, theirs)


def _chip_swap_copies(ins, outs, send, recv, loc):
    x, y, c = (lax.axis_index(a) for a in AXES)
    me = 2 * x + y
    starts, arrivals, drains = [], [], []
    for a in range(len(ins)):
        lc = pltpu.make_async_copy(ins[a].at[me], outs[a].at[me], loc.at[a])
        starts.append(lc.start)
        drains.append(lc.wait)
        for j, (px, py) in enumerate(_other_chips()):
            them = 2 * px + py
            cp = pltpu.make_async_remote_copy(src_ref=ins[a].at[them], dst_ref=outs[a].at[me], send_sem=send.at[a, j],
                                              recv_sem=recv.at[a, j], device_id=(px, py, c), device_id_type=MESH_ID)
            landing = pltpu.make_async_remote_copy(src_ref=ins[a].at[them], dst_ref=outs[a].at[them],
                                                   send_sem=send.at[a, j], recv_sem=recv.at[a, j],
                                                   device_id=(px, py, c), device_id_type=MESH_ID)
            starts.append(cp.start)
            arrivals.append(landing.wait_recv)
            drains.append(cp.wait_send)
    return starts, arrivals + drains


def _share_small(pack):
    def body(in_ref, out_ref, send, recv, loc):
        x, y, c = (lax.axis_index(a) for a in AXES)
        me = 4 * x + 2 * y + c
        lc = pltpu.make_async_copy(in_ref, out_ref.at[me], loc.at[0])
        lc.start()
        sends, arrivals = [], []
        for k in range(1, N_DEV):
            px = 1 - x if k & 4 else x
            py = 1 - y if k & 2 else y
            pc = 1 - c if k & 1 else c
            cp = pltpu.make_async_remote_copy(src_ref=in_ref, dst_ref=out_ref.at[me], send_sem=send.at[k - 1],
                                              recv_sem=recv.at[k - 1], device_id=(px, py, pc), device_id_type=MESH_ID)
            cp.start()
            sends.append(cp)
            arrivals.append(pltpu.make_async_remote_copy(src_ref=in_ref, dst_ref=out_ref.at[4 * px + 2 * py + pc],
                                                         send_sem=send.at[k - 1], recv_sem=recv.at[k - 1],
                                                         device_id=(px, py, pc), device_id_type=MESH_ID))
        for cp in arrivals:
            cp.wait_recv()
        for cp in sends:
            cp.wait_send()
        lc.wait()

    return pl.pallas_call(
        body, name="share_small", out_shape=jax.ShapeDtypeStruct((N_DEV,) + pack.shape, pack.dtype),
        in_specs=[ANY_SPEC], out_specs=ANY_SPEC,
        scratch_shapes=[pltpu.SemaphoreType.DMA((N_DEV - 1,)), pltpu.SemaphoreType.DMA((N_DEV - 1,)),
                        pltpu.SemaphoreType.DMA((1,))],
    )(pack)


W_PARTS = ((0, 0, 2048), (2048, 4096, 8), (2056, 2048, 2048))
SHARD_COLS = 513


def _pieces(lo, hi, parts):
    out = []
    for ref_start, tgt_start, width in parts:
        a, b = max(lo, ref_start), min(hi, ref_start + width)
        if a < b:
            out.append((a - lo, tgt_start + a - ref_start, b - a))
    return out


def _build_w(win8):
    tr = 256

    def body(in_ref, w_ref, wt_ref):
        w_ref[:, 4096:W_COLS] = jnp.zeros((tr, W_COLS - 4096), MXU)
        for p in range(N_DEV):
            for src, dst, width in _pieces(p * SHARD_COLS, (p + 1) * SHARD_COLS, W_PARTS):
                w_ref[:, dst:dst + width] = in_ref[p, :, src:src + width]
        for k in range(W_COLS // 128):
            wt_ref[k * 128:(k + 1) * 128, :] = w_ref[:, k * 128:(k + 1) * 128].astype(F32).T.astype(MXU)

    return pl.pallas_call(
        body, name="build_w", grid=(D_MODEL // tr,),
        in_specs=[pl.BlockSpec((N_DEV, tr, SHARD_COLS), lambda i: (0, i, 0))],
        out_specs=[pl.BlockSpec((tr, W_COLS), lambda i: (i, 0)), pl.BlockSpec((W_COLS, tr), lambda i: (0, i))],
        out_shape=[jax.ShapeDtypeStruct((D_MODEL, W_COLS), MXU), jax.ShapeDtypeStruct((W_COLS, D_MODEL), MXU)],
        compiler_params=_params(("arbitrary",)),
    )(win8)


def _build_slabs(s_dn, s_z, s_att, s_gate, s_ba):
    tr = 256
    secs = (s_dn, s_z, s_ba, s_att, s_gate)
    parts = ((0, 0, 1536), (1536, 1, 512), (2048, 2, 8), (2056, 3, 1536), (3592, 4, 512))

    def body(dn_ref, z_ref, ba_ref, att_ref, gate_ref, o_ref):
        refs = (dn_ref, z_ref, ba_ref, att_ref, gate_ref)
        for p in range(N_DEV):
            lo, hi = p * SHARD_COLS, (p + 1) * SHARD_COLS
            for ref_start, idx, width in parts:
                a, b = max(lo, ref_start), min(hi, ref_start + width)
                if a < b:
                    o_ref[p % 2, p // 2, :, a - lo:b - lo] = refs[idx][:, a - ref_start:b - ref_start]

    return pl.pallas_call(
        body, name="build_slabs", grid=(D_MODEL // tr,),
        in_specs=[pl.BlockSpec((tr, s.shape[1]), lambda i: (i, 0)) for s in secs],
        out_specs=pl.BlockSpec((2, 4, tr, SHARD_COLS), lambda i: (0, 0, i, 0)),
        out_shape=jax.ShapeDtypeStruct((2, 4, D_MODEL, SHARD_COLS), s_dn.dtype),
        compiler_params=_params(("arbitrary",)),
    )(*secs)


def _inproj(x, nw, w):
    t = x.shape[0]
    tm = 256

    def body(x_ref, nw_ref, w_ref, ht_ref, pdn_ref, z_ref, patt_ref, gate_ref, ba_ref):
        xv = x_ref[...]
        rstd = lax.rsqrt(jnp.mean(xv * xv, axis=-1, keepdims=True) + EPS)
        hf = xv * rstd * nw_ref[...]
        h = hf.astype(MXU)
        ht_ref[...] = hf.T.astype(MXU)
        for ref, lo, hi in ((pdn_ref, 0, 1536), (z_ref, 1536, 2048), (patt_ref, 2048, 3584),
                            (gate_ref, 3584, 4096), (ba_ref, 4096, 4224)):
            ref[...] = jnp.dot(h, w_ref[:, lo:hi], preferred_element_type=F32)

    row = lambda n: pl.BlockSpec((tm, n), lambda i: (i, 0))
    full = lambda a: pl.BlockSpec(a.shape, lambda i: (0,) * a.ndim)
    return pl.pallas_call(
        body, name="inproj", grid=(t // tm,),
        in_specs=[row(D_MODEL), full(nw), full(w)],
        out_specs=[pl.BlockSpec((D_MODEL, tm), lambda i: (0, i)), row(1536), row(512), row(1536), row(512), row(128)],
        out_shape=[jax.ShapeDtypeStruct((D_MODEL, t), MXU)] +
                  [jax.ShapeDtypeStruct((t, n), F32) for n in (1536, 512, 1536, 512, 128)],
        compiler_params=_params(("arbitrary",)),
    )(x, nw, w)


CONV_ROWS = 512


def _conv_taps(u_ref, c, w_ref):
    r0 = c * CONV_ROWS
    if c == 0:
        ext = jnp.concatenate([jnp.zeros((8, 128), F32), u_ref[0:CONV_ROWS, :]], axis=0)
    else:
        ext = u_ref[r0 - 8:r0 + CONV_ROWS, :]
    taps = [ext[8:, :]] + [pltpu.roll(ext, k, 0)[8:, :] for k in (1, 2, 3)]
    y = taps[0] * w_ref[3:4, :]
    for k in (1, 2, 3):
        y = y + taps[k] * w_ref[3 - k:4 - k, :]
    return taps, y


def _dn_prep(pdn, cw):
    t = pdn.shape[0]

    def body(u_ref, w_ref, o_ref):
        j = pl.program_id(0)
        for c in range(t // CONV_ROWS):
            _, y = _conv_taps(u_ref, c, w_ref)
            a = _silu(y)
            ssq = jnp.sum(a * a, axis=1, keepdims=True)
            f = jnp.where(j < 8, lax.rsqrt(ssq + EPS), 1.0) * jnp.where(j < 4, DK ** -0.5, 1.0)
            o_ref[c * CONV_ROWS:(c + 1) * CONV_ROWS, :] = a * f

    return pl.pallas_call(
        body, name="dn_prep", grid=(12,),
        in_specs=[pl.BlockSpec((t, 128), lambda j: (0, j)), pl.BlockSpec((4, 128), lambda j: (0, j))],
        out_specs=pl.BlockSpec((t, 128), lambda j: (0, j)),
        out_shape=jax.ShapeDtypeStruct((t, 1536), F32),
        compiler_params=_params(("arbitrary",)),
    )(pdn, cw)


def _chunk_common(qkv, ba, arow, dtb):
    c = CHUNK
    ri, ci = _iota((c, c), 0), _iota((c, c), 1)
    lane = _iota((c, 128), 1)
    g_all = jnp.where((lane >= DN_HEADS) & (lane < 2 * DN_HEADS), arow * _softplus(ba + dtb), 0.0)
    gc_all = _hdot((ri >= ci).astype(F32), g_all, "a")
    gc_t = gc_all.T
    beta_all = _sigmoid(ba)
    out = []
    for h in range(DN_HEADS):
        gc = _lane_col(gc_all, DN_HEADS + h)
        gcr = gc_t[DN_HEADS + h:DN_HEADS + h + 1, :]
        gl = gc[c - 1:c, :]
        out.append(dict(
            q=qkv[:, h * DK:(h + 1) * DK], k=qkv[:, D_DN + h * DK:D_DN + (h + 1) * DK],
            v=qkv[:, 2 * D_DN + h * DK:2 * D_DN + (h + 1) * DK],
            beta=_lane_col(beta_all, h), g=_lane_col(g_all, DN_HEADS + h),
            a_raw=_lane_col(ba, DN_HEADS + h), a_h=_lane_col(arow, DN_HEADS + h), dt_h=_lane_col(dtb, DN_HEADS + h),
            decay=jnp.exp(jnp.where(ri >= ci, gc - gcr, NEG)), eg=jnp.exp(gc), egl=jnp.exp(gl), etail=jnp.exp(gl - gc)))
    return out, ri, ci


SCAN_CHUNKS = 8


def _dn_scan_fwd(qkv, ba, z, arow, dtb, dnw):
    t = qkv.shape[0]
    n = t // CHUNK
    c = CHUNK
    cps = SCAN_CHUNKS
    hs = range(DN_HEADS)
    chains = [(j, h) for j in range(cps) for h in hs]

    def body(qkv_ref, ba_ref, z_ref, arow_ref, dtb_ref, dnw_ref, o_ref, y_ref, sh_ref, th_ref, s_ref):
        @pl.when(pl.program_id(0) == 0)
        def _():
            s_ref[...] = jnp.zeros_like(s_ref)

        ms = {}
        for j in range(cps):
            rows = slice(j * c, (j + 1) * c)
            mj, ri, ci = _chunk_common(qkv_ref[rows, :], ba_ref[rows, :], arow_ref[...], dtb_ref[...])
            for h in hs:
                ms[j, h] = mj[h]
        kb = {x: ms[x]["k"] * ms[x]["beta"] for x in chains}
        amat = {x: jnp.where(ri > ci, _dot_nt(kb[x], ms[x]["k"]) * ms[x]["decay"], 0.0) for x in chains}
        attn = {x: jnp.where(ri >= ci, _dot_nt(ms[x]["q"], ms[x]["k"]) * ms[x]["decay"], 0.0) for x in chains}
        eye = (ri == ci).astype(F32)
        tinv = {x: eye - amat[x] for x in chains}
        pw = amat
        for _ in range(5):
            pw = {x: _hdot(pw[x], pw[x]) for x in chains}
            tinv = {x: tinv[x] + _hdot(tinv[x], pw[x]) for x in chains}
        u = {x: _hdot(tinv[x], ms[x]["v"] * ms[x]["beta"]) for x in chains}
        w = {x: _hdot(tinv[x], kb[x] * ms[x]["eg"]) for x in chains}
        q_dec = {x: ms[x]["q"] * ms[x]["eg"] for x in chains}
        k_tail = {x: ms[x]["k"] * ms[x]["etail"] for x in chains}
        s = [s_ref[h] for h in hs]
        for j in range(cps):
            rows = slice(j * c, (j + 1) * c)
            v_new = [u[j, h] - _dot(w[j, h], s[h]) for h in hs]
            o = [_dot(q_dec[j, h], s[h]) + _dot(attn[j, h], v_new[h]) for h in hs]
            for h in hs:
                sh_ref[j, h] = s[h]
                th_ref[j, h] = tinv[j, h]
            s = [s[h] * ms[j, h]["egl"] + _dot_tn(k_tail[j, h], v_new[h]) for h in hs]
            for h in hs:
                cols = slice(h * DK, (h + 1) * DK)
                o_ref[rows, cols] = o[h]
                rs = lax.rsqrt(jnp.mean(o[h] * o[h], axis=1, keepdims=True) + EPS)
                y_ref[rows, cols] = o[h] * rs * dnw_ref[...] * _silu(z_ref[rows, cols])
        for h in hs:
            s_ref[h] = s[h]

    row = lambda w_: pl.BlockSpec((cps * c, w_), lambda i: (i, 0))
    one = pl.BlockSpec((1, 128), lambda i: (0, 0))
    return pl.pallas_call(
        body, name="dn_scan_fwd", grid=(n // cps,),
        in_specs=[row(1536), row(128), row(512), one, one, one],
        out_specs=[row(512), row(512), pl.BlockSpec((cps, DN_HEADS, DK, DK), lambda i: (i, 0, 0, 0)),
                   pl.BlockSpec((cps, DN_HEADS, c, c), lambda i: (i, 0, 0, 0))],
        out_shape=[jax.ShapeDtypeStruct((t, 512), F32), jax.ShapeDtypeStruct((t, 512), F32),
                   jax.ShapeDtypeStruct((n, DN_HEADS, DK, DK), F32), jax.ShapeDtypeStruct((n, DN_HEADS, c, c), F32)],
        scratch_shapes=[pltpu.VMEM((DN_HEADS, DK, DK), F32)],
        compiler_params=_params(("arbitrary",)),
    )(qkv, ba, z, arow, dtb, dnw)


def _att_prep(patt, g, gt, wq, wk):
    t = patt.shape[0]
    tm = 512

    def body(p_ref, g_ref, gt_ref, wq_ref, wk_ref, o_ref):
        for lo, w_ref in ((0, wq_ref), (512, wk_ref)):
            xv = p_ref[:, lo:lo + 512]
            rstd = lax.rsqrt(_hdot(xv * xv, g_ref[...], "b") * (1.0 / HD) + EPS)
            o_ref[:, lo:lo + 512] = xv * _hdot(rstd, gt_ref[...], "b") * w_ref[...]
        o_ref[:, 1024:1536] = p_ref[:, 1024:1536]

    full = lambda a: pl.BlockSpec(a.shape, lambda i: (0,) * a.ndim)
    return pl.pallas_call(
        body, name="att_prep", grid=(t // tm,),
        in_specs=[pl.BlockSpec((tm, 1536), lambda i: (i, 0)), full(g), full(gt), full(wq), full(wk)],
        out_specs=pl.BlockSpec((tm, 1536), lambda i: (i, 0)),
        out_shape=jax.ShapeDtypeStruct((t, 1536), F32),
        compiler_params=_params(("arbitrary",)),
    )(patt, g, gt, wq, wk)


def _bias_tables(rb, bk):
    def body(rb_ref, bk_ref, bias_ref):
        pair = pl.program_id(0)
        for p in range(len(PATTERNS)):
            bk_p = bk_ref[p]
            for hh in range(2):
                head = 2 * pair + hh
                bm = jnp.full((BLK, 2 * BLK), NEG, F32)
                for b in range(N_BUCKETS):
                    bm = jnp.where(bk_p == b, rb_ref[head, b], bm)
                bias_ref[p, hh * BLK:(hh + 1) * BLK, :] = bm

    return pl.pallas_call(
        body, name="bias_tables", grid=(ATT_HEADS // 2,),
        in_specs=[pl.BlockSpec(memory_space=pltpu.SMEM), pl.BlockSpec(bk.shape, lambda i: (0, 0, 0))],
        out_specs=pl.BlockSpec((len(PATTERNS), None, 2 * BLK, 2 * BLK), lambda i: (0, i, 0, 0)),
        out_shape=jax.ShapeDtypeStruct((len(PATTERNS), ATT_HEADS // 2, 2 * BLK, 2 * BLK), F32),
        compiler_params=_params(("arbitrary",)),
    )(rb, bk)


BIAS_SPEC = pl.BlockSpec((len(PATTERNS), None, 2 * BLK, 2 * BLK), lambda i: (0, i, 0, 0))


def _stack_heads(xb, h0):
    return jnp.concatenate([jnp.where(h0, xb, 0.0), jnp.where(h0, 0.0, xb)], axis=0).astype(MXU)


def _block_rows(t, r, n):
    per_class = (t // r) // BLK
    res = n // per_class
    j = n % per_class
    start = res + BLK * r * j
    pstart = res + BLK * r * jnp.maximum(j - 1, 0)
    if r == 1:
        return pl.ds(pl.multiple_of(start, BLK), BLK), pl.ds(pl.multiple_of(pstart, BLK), BLK), j
    return pl.ds(start, BLK, stride=r), pl.ds(pstart, BLK, stride=r), j


def _att_fwd(qkv, gate, bias):
    t = qkv.shape[0]
    rows = 512

    def body(bias_ref, q_ref, k_ref, v_ref, g_ref, o_ref, y_ref, lse_ref,
             o0_ref, o1_ref, o2_ref, l0_ref, l1_ref, l2_ref):
        h0 =_iota((BLK, 128), 1) < HD
        prev_cols = _iota((2 * BLK, 2 * BLK), 1) < BLK
        op_refs, lp_refs = (o0_ref, o1_ref, o2_ref), (l0_ref, l1_ref, l2_ref)

        for p, (_, r) in enumerate(PATTERNS):
            def blk(n, carry, p=p, r=r):
                cur, prev, j = _block_rows(t, r, n)
                q2 = _stack_heads(q_ref[cur, :], h0)
                k2 = jnp.concatenate([k_ref[prev, :], k_ref[cur, :]], axis=0).astype(MXU)
                v2 = jnp.concatenate([v_ref[prev, :], v_ref[cur, :]], axis=0).astype(MXU)
                s = _dot_nt(q2, k2) + bias_ref[p] + jnp.where(prev_cols & (j == 0), NEG, 0.0)
                m = jnp.max(s, axis=1, keepdims=True)
                e = jnp.exp(s - m)
                l = jnp.sum(e, axis=1, keepdims=True)
                pv = _dot(e, v2) / l
                lse = m + jnp.log(l)
                op_refs[p][cur, :] = jnp.where(h0, pv[:BLK], pv[BLK:])
                lp_refs[p][cur, :] = jnp.where(h0, lse[:BLK], lse[BLK:])
                return carry

            lax.fori_loop(0, t // BLK, blk, 0, unroll=8)

        for c in range(t // rows):
            sl = slice(c * rows, (c + 1) * rows)
            ls = [ref[sl, :] for ref in lp_refs]
            mx = jnp.maximum(jnp.maximum(ls[0], ls[1]), ls[2])
            ws = [jnp.exp(v_ - mx) for v_ in ls]
            den = ws[0] + ws[1] + ws[2]
            o = (ws[0] * o0_ref[sl, :] + ws[1] * o1_ref[sl, :] + ws[2] * o2_ref[sl, :]) / den
            o_ref[sl, :] = o
            y_ref[sl, :] = o * _silu(g_ref[sl, :])
            lse_ref[sl, :] = mx + jnp.log(den)

    col = lambda off: pl.BlockSpec((t, 128), lambda i, off=off: (0, off + i))
    return pl.pallas_call(
        body, name="att_fwd", grid=(ATT_HEADS // 2,),
        in_specs=[BIAS_SPEC, col(0), col(4), col(8), col(0)],
        out_specs=[col(0), col(0), col(0)],
        out_shape=[jax.ShapeDtypeStruct((t, 512), F32)] * 3,
        scratch_shapes=[pltpu.VMEM((t, 128), F32)] * 6,
        compiler_params=_params(("arbitrary",)),
    )(bias, qkv, qkv, qkv, gate)


def _outproj_loss(x, ydn, yatt, wout, target):
    t = x.shape[0]
    tm = 512

    def body(x_ref, a_ref, b_ref, w_ref, t_ref, dy_ref, mix_ref, loss_ref):
        @pl.when(pl.program_id(0) == 0)
        def _():
            loss_ref[...] = jnp.zeros_like(loss_ref)

        mixf = jnp.concatenate([a_ref[...], b_ref[...]], axis=1)
        mix_ref[...] = mixf.T.astype(MXU)
        err = x_ref[...] + jnp.dot(mixf.astype(MXU), w_ref[...], preferred_element_type=F32) - t_ref[...]
        dy_ref[...] = err * (1.0 / D_MODEL)
        loss_ref[...] += jnp.sum(err * err) * (0.5 / D_MODEL)

    row = lambda n: pl.BlockSpec((tm, n), lambda i: (i, 0))
    return pl.pallas_call(
        body, name="outproj_loss", grid=(t // tm,),
        in_specs=[row(D_MODEL), row(512), row(512), pl.BlockSpec(wout.shape, lambda i: (0, 0)), row(D_MODEL)],
        out_specs=[row(D_MODEL), pl.BlockSpec((D_MODEL, tm), lambda i: (0, i)), pl.BlockSpec((8, 128), lambda i: (0, 0))],
        out_shape=[jax.ShapeDtypeStruct((t, D_MODEL), F32), jax.ShapeDtypeStruct((D_MODEL, t), MXU),
                   jax.ShapeDtypeStruct((8, 128), F32)],
        compiler_params=_params(("arbitrary",)),
    )(x, ydn, yatt, wout, target)


def _outproj_bwd(dy, wout_t, oraw, z, dnw, oatt, gate, g, gt):
    t = dy.shape[0]
    tm = 256

    def body(dy_ref, w_ref, o_ref, z_ref, dnw_ref, oa_ref, g_ref, grp_ref, grpt_ref,
             do_ref, dz_ref, doa_ref, dg_ref, dd_ref, ddnw_ref):
        @pl.when(pl.program_id(0) == 0)
        def _():
            ddnw_ref[...] = jnp.zeros_like(ddnw_ref)

        dmix = jnp.dot(dy_ref[...].astype(MXU), w_ref[...], preferred_element_type=F32)
        dnw_v = dnw_ref[...]
        acc = jnp.zeros((1, DK), F32)
        for h in range(DN_HEADS):
            sl = slice(h * DK, (h + 1) * DK)
            o, zz, dm = o_ref[:, sl], z_ref[:, sl], dmix[:, sl]
            rs = lax.rsqrt(jnp.mean(o * o, axis=1, keepdims=True) + EPS)
            oh = o * rs
            silu_z, dsilu_z = _silu_and_grad(zz)
            dz_ref[:, sl] = dm * oh * dnw_v * dsilu_z
            d_on = dm * silu_z
            gg = d_on * dnw_v
            do_ref[:, sl] = rs * (gg - oh * jnp.mean(gg * oh, axis=1, keepdims=True))
            acc = acc + jnp.sum(d_on * oh, axis=0, keepdims=True)
        ddnw_ref[...] += jnp.broadcast_to(acc, (8, DK))
        da, gate_v, oa = dmix[:, 512:], g_ref[...], oa_ref[...]
        silu_g, dsilu_g = _silu_and_grad(gate_v)
        doa = da * silu_g
        doa_ref[...] = doa
        dg_ref[...] = da * oa * dsilu_g
        dd_ref[...] = _hdot(_hdot(doa * oa, grp_ref[...], "b"), grpt_ref[...], "b")

    row = lambda n: pl.BlockSpec((tm, n), lambda i: (i, 0))
    full = lambda a: pl.BlockSpec(a.shape, lambda i: (0,) * a.ndim)
    return pl.pallas_call(
        body, name="outproj_bwd", grid=(t // tm,),
        in_specs=[row(D_MODEL), full(wout_t), row(512), row(512), full(dnw), row(512), row(512), full(g), full(gt)],
        out_specs=[row(512)] * 5 + [pl.BlockSpec((8, DK), lambda i: (0, 0))],
        out_shape=[jax.ShapeDtypeStruct((t, 512), F32)] * 5 + [jax.ShapeDtypeStruct((8, DK), F32)],
        compiler_params=_params(("arbitrary",)),
    )(dy, wout_t, oraw, z, dnw, oatt, gate, g, gt)


def _grad_matmul(at, b, name):
    m, t = at.shape
    n = b.shape[1]
    tk = 512
    tn = n if n <= 512 else 512
    nk = t // tk

    def body(a_ref, b_ref, o_ref, acc_ref):
        k = pl.program_id(1)

        @pl.when(k == 0)
        def _():
            acc_ref[...] = jnp.zeros_like(acc_ref)

        acc_ref[...] += jnp.dot(a_ref[...], b_ref[...].astype(MXU), preferred_element_type=F32)

        @pl.when(k == nk - 1)
        def _():
            o_ref[...] = acc_ref[...].astype(GRAD_WIRE)

    return pl.pallas_call(
        body, name=name, grid=(n // tn, nk),
        in_specs=[pl.BlockSpec((m, tk), lambda j, k: (0, k)), pl.BlockSpec((tk, tn), lambda j, k: (k, j))],
        out_specs=pl.BlockSpec((m, tn), lambda j, k: (0, j)),
        out_shape=jax.ShapeDtypeStruct((m, n), GRAD_WIRE),
        scratch_shapes=[pltpu.VMEM((m, tn), F32)],
        compiler_params=_params(("arbitrary", "arbitrary")),
    )(at, b)


def _att_bwd(qkv, do, lse, dd, bias, bk):
    t = qkv.shape[0]
    rows = 512

    def body(bias_ref, bk_ref, q_ref, k_ref, v_ref, do_ref, lse_ref, dd_ref,
             dq_ref, dk_ref, dv_ref, db_ref, ds_ref):
        pair = pl.program_id(0)

        @pl.when(pair == 0)
        def _():
            db_ref[...] = jnp.zeros_like(db_ref)

        ds_ref[...] = jnp.zeros_like(ds_ref)
        for c in range(t // rows):
            sl = slice(c * rows, (c + 1) * rows)
            for ref in (dq_ref, dk_ref, dv_ref):
                ref[sl, :] = jnp.zeros((rows, 128), F32)
        h0 = _iota((BLK, 128), 1) < HD
        prev_cols = _iota((2 * BLK, 2 * BLK), 1) < BLK

        def rows_of(xb):
            return jnp.concatenate([xb[:, 0:1], xb[:, HD:HD + 1]], axis=0)

        for p, (_, r) in enumerate(PATTERNS):
            def blk(n, carry, p=p, r=r):
                cur, prev, j = _block_rows(t, r, n)
                q2, do2 = _stack_heads(q_ref[cur, :], h0), _stack_heads(do_ref[cur, :], h0)
                k2 = jnp.concatenate([k_ref[prev, :], k_ref[cur, :]], axis=0).astype(MXU)
                v2 = jnp.concatenate([v_ref[prev, :], v_ref[cur, :]], axis=0).astype(MXU)
                s = _dot_nt(q2, k2) + bias_ref[p] + jnp.where(prev_cols & (j == 0), NEG, 0.0)
                prob = jnp.exp(s - rows_of(lse_ref[cur, :]))
                ds = prob * (_dot_nt(do2, v2) - rows_of(dd_ref[cur, :]))
                ds_ref[p] += ds
                dq2 = _dot(ds, k2)
                dk2 = _dot_tn(ds, q2)
                dv2 = _dot_tn(prob, do2)
                dq_ref[cur, :] += jnp.where(h0, dq2[:BLK], dq2[BLK:])
                dk_ref[prev, :] += dk2[:BLK]
                dv_ref[prev, :] += dv2[:BLK]
                dk_ref[cur, :] += dk2[BLK:]
                dv_ref[cur, :] += dv2[BLK:]
                return carry

            lax.fori_loop(0, t // BLK, blk, 0, unroll=4)

        ri, ci = _iota((8, 128), 0), _iota((8, 128), 1)
        upd = jnp.zeros((8, 128), F32)
        for p in range(len(PATTERNS)):
            bk = bk_ref[p]
            for hh in range(2):
                dsum = ds_ref[p, hh * BLK:(hh + 1) * BLK, :]
                for b in range(N_BUCKETS):
                    val = jnp.sum(jnp.where(bk == b, dsum, 0.0))
                    upd = upd + jnp.where((ri == 2 * pair + hh) & (ci == b), val, 0.0)
        db_ref[...] += upd

    col = lambda off: pl.BlockSpec((t, 128), lambda i, off=off: (0, off + i))
    return pl.pallas_call(
        body, name="att_bwd", grid=(ATT_HEADS // 2,),
        in_specs=[BIAS_SPEC, pl.BlockSpec(bk.shape, lambda i: (0, 0, 0)),
                  col(0), col(4), col(8), col(0), col(0), col(0)],
        out_specs=[col(0), col(0), col(0), pl.BlockSpec((8, 128), lambda i: (0, 0))],
        out_shape=[jax.ShapeDtypeStruct((t, 512), F32)] * 3 + [jax.ShapeDtypeStruct((8, 128), F32)],
        scratch_shapes=[pltpu.VMEM((len(PATTERNS), 2 * BLK, 2 * BLK), F32)],
        compiler_params=_params(("arbitrary",)),
    )(bias, bk, qkv, qkv, qkv, do, lse, dd)


def _att_prep_bwd(patt, dq, dk, dv, g, gt, wq, wk):
    t = patt.shape[0]
    tm = 512

    def body(p_ref, dq_ref, dk_ref, dv_ref, g_ref, gt_ref, wq_ref, wk_ref, o_ref, dwq_ref, dwk_ref):
        @pl.when(pl.program_id(0) == 0)
        def _():
            dwq_ref[...] = jnp.zeros_like(dwq_ref)
            dwk_ref[...] = jnp.zeros_like(dwk_ref)

        for lo, w_ref, d_ref, dw_ref in ((0, wq_ref, dq_ref, dwq_ref), (512, wk_ref, dk_ref, dwk_ref)):
            xv, dyv = p_ref[:, lo:lo + 512], d_ref[...]
            rstd = lax.rsqrt(_hdot(xv * xv, g_ref[...], "b") * (1.0 / HD) + EPS)
            rsb = _hdot(rstd, gt_ref[...], "b")
            xh = xv * rsb
            gg = dyv * w_ref[...]
            mean = _hdot(_hdot(gg * xh, g_ref[...], "b") * (1.0 / HD), gt_ref[...], "b")
            o_ref[:, lo:lo + 512] = rsb * (gg - xh * mean)
            dw_ref[...] += jnp.broadcast_to(jnp.sum(dyv * xh, axis=0, keepdims=True), (8, 512))
        o_ref[:, 1024:1536] = dv_ref[...]

    row = lambda n: pl.BlockSpec((tm, n), lambda i: (i, 0))
    full = lambda a: pl.BlockSpec(a.shape, lambda i: (0,) * a.ndim)
    acc = pl.BlockSpec((8, 512), lambda i: (0, 0))
    return pl.pallas_call(
        body, name="att_prep_bwd", grid=(t // tm,),
        in_specs=[row(1536), row(512), row(512), row(512), full(g), full(gt), full(wq), full(wk)],
        out_specs=[row(1536), acc, acc],
        out_shape=[jax.ShapeDtypeStruct((t, 1536), F32), jax.ShapeDtypeStruct((8, 512), F32),
                   jax.ShapeDtypeStruct((8, 512), F32)],
        compiler_params=_params(("arbitrary",)),
    )(patt, dq, dk, dv, g, gt, wq, wk)


def _dn_scan_bwd(qkv, ba, do, sh, th, arow, dtb):
    t = qkv.shape[0]
    n = t // CHUNK
    c = CHUNK
    cps = SCAN_CHUNKS

    def body(qkv_ref, ba_ref, do_ref, sh_ref, th_ref, arow_ref, dtb_ref, dqkv_ref, dba_ref, ds_ref):
        @pl.when(pl.program_id(0) == 0)
        def _():
            ds_ref[...] = jnp.zeros_like(ds_ref)

        hs = range(DN_HEADS)
        chains = [(j, h) for j in range(cps) for h in hs]
        lane = _iota((c, 128), 1)
        row = _iota((c, 1), 0)
        ms = {}
        for j in range(cps):
            rows_j = slice(j * c, (j + 1) * c)
            mj, ri, ci = _chunk_common(qkv_ref[rows_j, :], ba_ref[rows_j, :], arow_ref[...], dtb_ref[...])
            for h in hs:
                ms[j, h] = mj[h]
        q, k, v = ({x: ms[x][nm] for x in chains} for nm in ("q", "k", "v"))
        beta, decay = ({x: ms[x][nm] for x in chains} for nm in ("beta", "decay"))
        eg, egl, etail = ({x: ms[x][nm] for x in chains} for nm in ("eg", "egl", "etail"))
        s = {x: sh_ref[x[0], x[1]] for x in chains}
        tinv = {x: th_ref[x[0], x[1]] for x in chains}
        d_o = {(j, h): do_ref[j * c:(j + 1) * c, h * DK:(h + 1) * DK] for j, h in chains}
        kb = {x: k[x] * beta[x] for x in chains}
        vb = {x: v[x] * beta[x] for x in chains}
        kbg = {x: kb[x] * eg[x] for x in chains}
        amat = {x: jnp.where(ri > ci, _dot_nt(kb[x], k[x]) * decay[x], 0.0) for x in chains}
        attn = {x: jnp.where(ri >= ci, _dot_nt(q[x], k[x]) * decay[x], 0.0) for x in chains}
        u = {x: _hdot(tinv[x], vb[x]) for x in chains}
        w = {x: _hdot(tinv[x], kbg[x]) for x in chains}
        v_new = {x: u[x] - _dot(w[x], s[x]) for x in chains}
        q_dec = {x: q[x] * eg[x] for x in chains}
        k_tail = {x: k[x] * etail[x] for x in chains}
        d_attn = {x: jnp.where(ri >= ci, _dot_nt(d_o[x], v_new[x]), 0.0) for x in chains}
        d_qdec = {x: _dot_nt(d_o[x], s[x]) for x in chains}
        from_o = {x: _dot_tn(attn[x], d_o[x]) for x in chains}
        to_state = {x: _dot_tn(q_dec[x], d_o[x]) for x in chains}

        d_s, d_vnew = {}, {}
        cur = [ds_ref[h] for h in hs]
        for j in reversed(range(cps)):
            for h in hs:
                d_s[j, h] = cur[h]
                d_vnew[j, h] = from_o[j, h] + _dot(k_tail[j, h], cur[h])
            cur = [to_state[j, h] + cur[h] * egl[j, h] - _dot_tn(w[j, h], d_vnew[j, h]) for h in hs]
        for h in hs:
            ds_ref[h] = cur[h]

        d_ktail = {x: _dot_nt(v_new[x], d_s[x]) for x in chains}
        d_gl = {x: jnp.sum(s[x] * d_s[x]) * egl[x] for x in chains}
        d_w = {x: -_dot_nt(d_vnew[x], s[x]) for x in chains}
        d_vb = {x: _hdot_tn(tinv[x], d_vnew[x]) for x in chains}
        d_kbg = {x: _hdot_tn(tinv[x], d_w[x]) for x in chains}
        d_a = {x: -jnp.where(ri > ci, _hdot_nt(d_vb[x], u[x]) + _hdot_nt(d_kbg[x], w[x]), 0.0) for x in chains}
        d_qk = {x: d_attn[x] * decay[x] for x in chains}
        d_kk = {x: d_a[x] * decay[x] for x in chains}
        d_kb = {x: _dot(d_kk[x], k[x]) + d_kbg[x] * eg[x] for x in chains}
        d_q = {x: _dot(d_qk[x], k[x]) + d_qdec[x] * eg[x] for x in chains}
        d_k = {x: _dot_tn(d_qk[x], q[x]) + _dot_tn(d_kk[x], kb[x]) + d_ktail[x] * etail[x] + d_kb[x] * beta[x]
               for x in chains}
        d_beta = {x: jnp.sum(d_kb[x] * k[x] + d_vb[x] * v[x], axis=1, keepdims=True) for x in chains}
        mm = {x: d_a[x] * amat[x] + d_attn[x] * attn[x] for x in chains}
        for j in range(cps):
            rows_j = slice(j * c, (j + 1) * c)
            rows = jnp.zeros((c, c), F32)
            for h in hs:
                rows = rows + jnp.where(ri == h, jnp.sum(mm[j, h], axis=0, keepdims=True), 0.0)
            cols_t = jnp.concatenate([rows, jnp.zeros((c, c), F32)], axis=1).T[:c, :]
            d_gc_all = jnp.zeros((c, 128), F32)
            for h in hs:
                x = (j, h)
                tail_term = jnp.sum(d_ktail[x] * k_tail[x], axis=1, keepdims=True)
                d_gc = (jnp.sum(mm[x], axis=1, keepdims=True) - _lane_col(cols_t, h)
                        + jnp.sum(d_qdec[x] * q_dec[x] + d_kbg[x] * kbg[x], axis=1, keepdims=True) - tail_term)
                d_gc = d_gc + jnp.where(row == c - 1, jnp.sum(tail_term) + d_gl[x], 0.0)
                d_gc_all = d_gc_all + jnp.where(lane == DN_HEADS + h, d_gc, 0.0)
            d_g_all = _hdot((ri <= ci).astype(F32), d_gc_all, "a")
            dba = jnp.zeros((c, 128), F32)
            for h in hs:
                x = (j, h)
                d_g = _lane_col(d_g_all, DN_HEADS + h)
                d_braw = d_beta[x] * beta[x] * (1.0 - beta[x])
                d_araw = d_g * ms[x]["a_h"] * _sigmoid(ms[x]["a_raw"] + ms[x]["dt_h"])
                dba = dba + jnp.where(lane == h, d_braw, 0.0) + jnp.where(lane == DN_HEADS + h, d_araw, 0.0) \
                    + jnp.where(lane == 2 * DN_HEADS + h, d_g * ms[x]["g"], 0.0)
                dqkv_ref[rows_j, h * DK:(h + 1) * DK] = d_q[x]
                dqkv_ref[rows_j, D_DN + h * DK:D_DN + (h + 1) * DK] = d_k[x]
                dqkv_ref[rows_j, 2 * D_DN + h * DK:2 * D_DN + (h + 1) * DK] = d_vb[x] * beta[x]
            dba_ref[rows_j, :] = dba

    nsteps = n // cps
    rev = lambda w_: pl.BlockSpec((cps * c, w_), lambda i: (nsteps - 1 - i, 0))
    one = pl.BlockSpec((1, 128), lambda i: (0, 0))
    return pl.pallas_call(
        body, name="dn_scan_bwd", grid=(nsteps,),
        in_specs=[rev(1536), rev(128), rev(512),
                  pl.BlockSpec((cps, DN_HEADS, DK, DK), lambda i: (nsteps - 1 - i, 0, 0, 0)),
                  pl.BlockSpec((cps, DN_HEADS, c, c), lambda i: (nsteps - 1 - i, 0, 0, 0)), one, one],
        out_specs=[rev(1536), rev(128)],
        out_shape=[jax.ShapeDtypeStruct((t, 1536), F32), jax.ShapeDtypeStruct((t, 128), F32)],
        scratch_shapes=[pltpu.VMEM((DN_HEADS, DK, DK), F32)],
        compiler_params=_params(("arbitrary",)),
    )(qkv, ba, do, sh, th, arow, dtb)


def _dn_prep_bwd(pdn, cw, dact):
    t = pdn.shape[0]
    nchunk = t // CONV_ROWS

    def body(u_ref, w_ref, d_ref, du_ref, dw_ref, dy_ref):
        j = pl.program_id(0)
        dy_ref[t:t + 8, :] = jnp.zeros((8, 128), F32)
        dw = [jnp.zeros((1, 128), F32) for _ in range(4)]
        for c in range(nchunk):
            sl = slice(c * CONV_ROWS, (c + 1) * CONV_ROWS)
            taps, y = _conv_taps(u_ref, c, w_ref)
            a, da_dy = _silu_and_grad(y)
            dout = d_ref[sl, :]
            rs = lax.rsqrt(jnp.sum(a * a, axis=1, keepdims=True) + EPS)
            f = jnp.where(j < 8, rs, 1.0) * jnp.where(j < 4, DK ** -0.5, 1.0)
            corr = jnp.where(j < 8, f * rs * rs * jnp.sum(dout * a, axis=1, keepdims=True), 0.0)
            dy = (f * dout - corr * a) * da_dy
            dy_ref[sl, :] = dy
            for k_ in range(4):
                dw[3 - k_] = dw[3 - k_] + jnp.sum(taps[k_] * dy, axis=0, keepdims=True)
        for i in range(4):
            dw_ref[i:i + 1, :] = dw[i]
        for c in range(nchunk):
            r0 = c * CONV_ROWS
            ext = dy_ref[r0:r0 + CONV_ROWS + 8, :]
            du = ext[:CONV_ROWS, :] * w_ref[3:4, :]
            for k_ in (1, 2, 3):
                du = du + pltpu.roll(ext, CONV_ROWS + 8 - k_, 0)[:CONV_ROWS, :] * w_ref[3 - k_:4 - k_, :]
            du_ref[r0:r0 + CONV_ROWS, :] = du

    return pl.pallas_call(
        body, name="dn_prep_bwd", grid=(12,),
        in_specs=[pl.BlockSpec((t, 128), lambda j: (0, j)), pl.BlockSpec((4, 128), lambda j: (0, j)),
                  pl.BlockSpec((t, 128), lambda j: (0, j))],
        out_specs=[pl.BlockSpec((t, 128), lambda j: (0, j)), pl.BlockSpec((4, 128), lambda j: (0, j))],
        out_shape=[jax.ShapeDtypeStruct((t, 1536), F32), jax.ShapeDtypeStruct((4, 1536), F32)],
        scratch_shapes=[pltpu.VMEM((t + 8, 128), F32)],
        compiler_params=_params(("arbitrary",)),
    )(pdn, cw, dact)


def _inproj_bwd(x, nw, wt, dy, dpdn, dz, dpatt, dgate, dba, partials):
    t = x.shape[0]
    tm = 256
    npart = len(partials)
    nsteps = t // tm

    def body(x_ref, nw_ref, w_ref, dy_ref, a_ref, b_ref, c_ref, d_ref, e_ref, *rest):
        part_refs, (gx_ref, dnw_ref, cs_ref) = rest[:npart], rest[npart:npart + 3]
        got_refs, (send, recv, loc) = rest[npart + 3:2 * npart + 3], rest[2 * npart + 3:]
        starts, waits = _chip_swap_copies(part_refs, got_refs, send, recv, loc)

        @pl.when(pl.program_id(0) == 0)
        def _():
            for start in starts:
                start()
            dnw_ref[...] = jnp.zeros_like(dnw_ref)
            cs_ref[...] = jnp.zeros_like(cs_ref)

        @pl.when(pl.program_id(0) == nsteps - 1)
        def _():
            for wait in waits:
                wait()

        dh = jnp.zeros((tm, D_MODEL), F32)
        for ref, lo, hi in ((a_ref, 0, 1536), (b_ref, 1536, 2048), (c_ref, 2048, 3584),
                            (d_ref, 3584, 4096), (e_ref, 4096, 4224)):
            dh = dh + jnp.dot(ref[...].astype(MXU), w_ref[lo:hi, :], preferred_element_type=F32)
        xv = x_ref[...]
        rstd = lax.rsqrt(jnp.mean(xv * xv, axis=-1, keepdims=True) + EPS)
        xh = xv * rstd
        gg = dh * nw_ref[...]
        gx_ref[...] = rstd * (gg - xh * jnp.mean(gg * xh, axis=-1, keepdims=True)) + dy_ref[...]
        dnw_ref[...] += jnp.broadcast_to(jnp.sum(dh * xh, axis=0, keepdims=True), (8, D_MODEL))
        cs_ref[...] += jnp.broadcast_to(jnp.sum(e_ref[...], axis=0, keepdims=True), (8, 128))

    row = lambda n: pl.BlockSpec((tm, n), lambda i: (i, 0))
    full = lambda a: pl.BlockSpec(a.shape, lambda i: (0,) * a.ndim)
    res = pl.pallas_call(
        body, name="inproj_bwd", grid=(nsteps,),
        in_specs=[row(D_MODEL), full(nw), full(wt), row(D_MODEL), row(1536), row(512), row(1536), row(512), row(128)]
                 + [ANY_SPEC] * npart,
        out_specs=[row(D_MODEL), pl.BlockSpec((8, D_MODEL), lambda i: (0, 0)), pl.BlockSpec((8, 128), lambda i: (0, 0))]
                  + [ANY_SPEC] * npart,
        out_shape=[jax.ShapeDtypeStruct((t, D_MODEL), F32), jax.ShapeDtypeStruct((8, D_MODEL), F32),
                   jax.ShapeDtypeStruct((8, 128), F32)] + [jax.ShapeDtypeStruct(p.shape, p.dtype) for p in partials],
        scratch_shapes=[pltpu.SemaphoreType.DMA((npart, 3)), pltpu.SemaphoreType.DMA((npart, 3)),
                        pltpu.SemaphoreType.DMA((npart,))],
        compiler_params=_params(("arbitrary",)),
    )(x, nw, wt, dy, dpdn, dz, dpatt, dgate, dba, *partials)
    return res[0], res[1], res[2], res[3:]


def _adamw_sum(w, gs, m, v, name):
    r, c = w.shape
    nsum = gs.shape[0]
    tr = r if r <= 256 else 256
    c1 = 1.0 - ADAM_B1 ** ADAM_STEP
    c2 = 1.0 - ADAM_B2 ** ADAM_STEP

    def body(w_ref, g_ref, m_ref, v_ref, go_ref, d_ref, mo_ref, vo_ref):
        g = g_ref[0].astype(F32)
        for s in range(1, nsum):
            g = g + g_ref[s].astype(F32)
        mn = ADAM_B1 * m_ref[...] + (1.0 - ADAM_B1) * g
        vn = ADAM_B2 * v_ref[...] + (1.0 - ADAM_B2) * (g * g)
        go_ref[...] = g
        mo_ref[...] = mn
        vo_ref[...] = vn
        d_ref[...] = -ADAM_LR * ((mn / c1) / (jnp.sqrt(vn / c2) + ADAM_EPS) + ADAM_WD * w_ref[...])

    blk = pl.BlockSpec((tr, c), lambda i: (i, 0))
    return pl.pallas_call(
        body, name=name, grid=(r // tr,),
        in_specs=[blk, pl.BlockSpec((nsum, tr, c), lambda i: (0, i, 0)), blk, blk],
        out_specs=[blk] * 4, out_shape=[jax.ShapeDtypeStruct((r, c), F32)] * 4,
        compiler_params=_params(("arbitrary",)),
    )(w, gs, m, v)


def _local_step(x, target, norm_w, w_sect, w_sect_t, conv_w, a_log, dt_bias, dn_norm_w, q_norm_w, k_norm_w, rel_bias,
                w_out):
    lane = np.arange(128)
    arow = jnp.zeros((1, 128), F32).at[0, DN_HEADS:2 * DN_HEADS].set(-jnp.exp(a_log[0]))
    dtb = jnp.zeros((1, 128), F32).at[0, DN_HEADS:2 * DN_HEADS].set(dt_bias[0])
    g_np, gt_np = _group_mats()
    g, gt = jnp.asarray(g_np), jnp.asarray(gt_np)
    bk = jnp.asarray(_bucket_tables())
    wq = jnp.tile(q_norm_w, (1, ATT_HEADS)) * (HD ** -0.5)
    wk = jnp.tile(k_norm_w, (1, ATT_HEADS))
    del lane

    ht, pdn, z, patt, gate, ba = _inproj(x, norm_w, w_sect)
    qkv_dn = _dn_prep(pdn, conv_w)
    oraw, ydn, sh, th = _dn_scan_fwd(qkv_dn, ba, z, arow, dtb, dn_norm_w)
    qkv_att = _att_prep(patt, g, gt, wq, wk)
    bias = _bias_tables(rel_bias, bk)
    oatt, yatt, lse = _att_fwd(qkv_att, gate, bias)
    dy, mix_t, loss8 = _outproj_loss(x, ydn, yatt, w_out, target)

    do_dn, dz, do_att, dgate, dd, ddnw = _outproj_bwd(dy, w_out.T, oraw, z, dn_norm_w, oatt, gate, g, gt)
    d_wout = _grad_matmul(mix_t, dy, "dw_out")
    dq, dk, dv, drb = _att_bwd(qkv_att, do_att, lse, dd, bias, bk)
    dpatt, dwq8, dwk8 = _att_prep_bwd(patt, dq, dk, dv, g, gt, wq, wk)
    dqkv_dn, dba = _dn_scan_bwd(qkv_dn, ba, do_dn, sh, th, arow, dtb)
    dpdn, d_conv = _dn_prep_bwd(pdn, conv_w, dqkv_dn)
    dw_sections = [_grad_matmul(ht, d_, nm) for d_, nm in
                   ((dpdn, "dw_in_dn"), (dz, "dw_in_z"), (dpatt, "dw_in_att"), (dgate, "dw_in_gate"), (dba, "dw_in_ba"))]
    return dict(w_in_sections=dw_sections, conv_w=d_conv, w_out=d_wout,
                dproj=(dy, dpdn, dz, dpatt, dgate, dba), small_parts=(loss8, ddnw, dwq8, dwk8, drb))


def _finish_step(x, norm_w, w_sect_t, gr, partials):
    grad_x, dnw8, cs8, got = _inproj_bwd(x, norm_w, w_sect_t, *gr["dproj"], partials)
    return grad_x, _pack_small_grads(dnw8, cs8, *gr["small_parts"]), got


SMALL_ROWS = 24
SMALL_AT = dict(a_log=(slice(8, 9), slice(0, 4)), dt_bias=(slice(9, 10), slice(0, 4)),
                dn_norm_w=(slice(10, 11), slice(0, 128)), q_norm_w=(slice(11, 12), slice(0, HD)),
                k_norm_w=(slice(12, 13), slice(0, HD)), rel_bias=(slice(16, 24), slice(0, N_BUCKETS)))
SMALL_NAMES = ("norm_w", "a_log", "dt_bias", "dn_norm_w", "q_norm_w", "k_norm_w", "rel_bias")


LOSS_ROW = 13


def _pack_small_grads(dnw8, cs8, loss8, ddnw8, dwq8, dwk8, drb):
    def body(dnw_ref, cs_ref, loss_ref, ddnw_ref, dwq_ref, dwk_ref, drb_ref, o_ref):
        lane = _iota((8, 128), 1)
        o_ref[...] = jnp.zeros_like(o_ref)
        o_ref[LOSS_ROW:LOSS_ROW + 1, :] = jnp.where(lane == 0, loss_ref[...], 0.0)[0:1, :]
        for k in range(D_MODEL // 128):
            o_ref[k:k + 1, :] = dnw_ref[0:1, k * 128:(k + 1) * 128]
        cs = cs_ref[...]
        o_ref[8:9, :] = jnp.where(lane < DN_HEADS, pltpu.roll(cs, 128 - 2 * DN_HEADS, 1), 0.0)[0:1, :]
        o_ref[9:10, :] = jnp.where(lane < DN_HEADS, pltpu.roll(cs, 128 - DN_HEADS, 1), 0.0)[0:1, :]
        o_ref[10:11, :] = ddnw_ref[0:1, :]
        for row, ref, scale in ((11, dwq_ref, HD ** -0.5), (12, dwk_ref, 1.0)):
            acc = ref[:, 0:128] + ref[:, 128:256] + ref[:, 256:384] + ref[:, 384:512]
            acc = (acc + pltpu.roll(acc, HD, 1)) * scale
            o_ref[row:row + 1, :] = jnp.where(lane < HD, acc, 0.0)[0:1, :]
        o_ref[16:24, :] = drb_ref[...]

    return pl.pallas_call(body, name="pack_small_grads", out_shape=jax.ShapeDtypeStruct((SMALL_ROWS, 128), F32),
                          )(dnw8, cs8, loss8, ddnw8, dwq8, dwk8, drb)


def _adam_math(w, g, m, v):
    c1 = 1.0 - ADAM_B1 ** ADAM_STEP
    c2 = 1.0 - ADAM_B2 ** ADAM_STEP
    mn = ADAM_B1 * m + (1.0 - ADAM_B1) * g
    vn = ADAM_B2 * v + (1.0 - ADAM_B2) * (g * g)
    return -ADAM_LR * ((mn / c1) / (jnp.sqrt(vn / c2) + ADAM_EPS) + ADAM_WD * w), mn, vn


def _adamw_small(gs, ws, ms, vs):
    n = len(SMALL_NAMES)

    def body(g_ref, *refs):
        w_refs, m_refs, v_refs = refs[:n], refs[n:2 * n], refs[2 * n:3 * n]
        outs, loss_ref = refs[3 * n:7 * n], refs[7 * n]
        loss = g_ref[0, LOSS_ROW:LOSS_ROW + 1, :]
        for s in range(1, gs.shape[0]):
            loss = loss + g_ref[s, LOSS_ROW:LOSS_ROW + 1, :]
        loss_ref[...] = loss

        def one(i, rows, lanes, at):
            g = g_ref[0, rows, lanes]
            for s in range(1, gs.shape[0]):
                g = g + g_ref[s, rows, lanes]
            d, mn, vn = _adam_math(w_refs[i][at], g, m_refs[i][at], v_refs[i][at])
            for kind, val in enumerate((g, d, mn, vn)):
                outs[kind * n + i][at] = val

        for k in range(D_MODEL // 128):
            one(0, slice(k, k + 1), slice(0, 128), (slice(0, 1), slice(k * 128, (k + 1) * 128)))
        for i, nm in enumerate(SMALL_NAMES[1:], start=1):
            rows, lanes = SMALL_AT[nm]
            one(i, rows, lanes, (slice(None), slice(None)))

    shapes = [jax.ShapeDtypeStruct(w.shape, F32) for w in ws]
    res = pl.pallas_call(body, name="adamw_small",
                         out_shape=shapes * 4 + [jax.ShapeDtypeStruct((1, 128), F32)])(gs, *ws, *ms, *vs)
    return [res[k * n:(k + 1) * n] for k in range(4)], res[4 * n]


def kernel(x, norm_w, w_in, conv_w, a_log, dt_bias, dn_norm_w, q_norm_w, k_norm_w, rel_bias, w_out, loss_target, m_norm_w, m_w_in, m_conv_w, m_a_log, m_dt_bias, m_dn_norm_w, m_q_norm_w, m_k_norm_w, m_rel_bias, m_w_out, v_norm_w, v_w_in, v_conv_w, v_a_log, v_dt_bias, v_dn_norm_w, v_q_norm_w, v_k_norm_w, v_rel_bias, v_w_out):
    assert w_in.shape[2] == SHARD_COLS
    win8, wout8, conv8 = _gather_weights([w_in[0].astype(MXU), w_out[0].astype(MXU), conv_w[0]])
    w_sect, w_sect_t = _build_w(win8)
    wout_full = wout8.reshape(D_MODEL, D_MODEL)
    conv_full = conv8.transpose(1, 0, 2).reshape(4, 3 * D_DN)

    gr = _local_step(x[0], loss_target[0], norm_w, w_sect, w_sect_t, conv_full, a_log, dt_bias, dn_norm_w, q_norm_w,
                     k_norm_w, rel_bias, wout_full)

    slabs = [_build_slabs(*gr["w_in_sections"]),
             gr["w_out"].reshape(4, 2, D_MODEL // N_DEV, D_MODEL).transpose(1, 0, 2, 3),
             gr["conv_w"].reshape(4, 4, 2, 3 * D_DN // N_DEV).transpose(2, 1, 0, 3)]
    core = lax.axis_index("c").astype(jnp.int32).reshape(1)
    from_sibling = _swap_siblings(slabs)
    wires = (GRAD_WIRE, GRAD_WIRE, F32)
    partial = [_chip_sum(slabs[i], from_sibling[i], core, wires[i], "chip_sum_%d" % i) for i in range(3)]
    grad_x, small_pack, (r_win, r_wout, r_conv) = _finish_step(x[0], norm_w, w_sect_t, gr, partial)
    r_small = _share_small(small_pack)

    g_win, d_win, m_win, v_win = _adamw_sum(w_in[0], r_win, m_w_in[0], v_w_in[0], "adamw_w_in")
    g_wout, d_wout, m_wout, v_wout = _adamw_sum(w_out[0], r_wout, m_w_out[0], v_w_out[0], "adamw_w_out")
    g_conv, d_conv, m_conv, v_conv = _adamw_sum(conv_w[0], r_conv, m_conv_w[0], v_conv_w[0], "adamw_conv_w")
    small, loss_row = _adamw_small(r_small,
                                   (norm_w, a_log, dt_bias, dn_norm_w, q_norm_w, k_norm_w, rel_bias),
                                   (m_norm_w, m_a_log, m_dt_bias, m_dn_norm_w, m_q_norm_w, m_k_norm_w, m_rel_bias),
                                   (v_norm_w, v_a_log, v_dt_bias, v_dn_norm_w, v_q_norm_w, v_k_norm_w, v_rel_bias))

    loss = loss_row[0, 0]
    names = ("norm_w", "w_in", "conv_w", "a_log", "dt_bias", "dn_norm_w", "q_norm_w", "k_norm_w", "rel_bias", "w_out")
    big = dict(w_in=(g_win, d_win, m_win, v_win), conv_w=(g_conv, d_conv, m_conv, v_conv),
               w_out=(g_wout, d_wout, m_wout, v_wout))
    outs = [loss, grad_x[None]]
    for kind in range(4):
        for nm in names:
            outs.append(big[nm][kind][None] if nm in big else small[kind][SMALL_NAMES.index(nm)])
    return tuple(outs)
```

```python
import functools
import math

import numpy as np
import jax
import jax.numpy as jnp
from jax import lax
from jax.experimental import pallas as pl
from jax.experimental.pallas import tpu as pltpu

F32 = jnp.float32
MXU = jnp.bfloat16
GRAD_WIRE = jnp.bfloat16
HI = lax.Precision.HIGHEST

D_MODEL = 1024
D_DN = 512
DN_HEADS = 4
DK = 128
CHUNK = 64
D_ATT = 512
ATT_HEADS = 8
HD = 64
PATTERNS = ((128, 1), (512, 4), (2048, 16))
BLK = 128
N_BUCKETS = 32
MAX_DISTANCE = 2048
EPS = 1e-6
W_COLS = 4224
N_DEV = 8
AXES = ("x", "y", "c")

ADAM_LR = 0.001
ADAM_B1 = 0.9
ADAM_B2 = 0.999
ADAM_EPS = 1e-08
ADAM_WD = 0.01
ADAM_STEP = 10

VMEM_LIMIT = 56 * 1024 * 1024
NEG = -1e30


def _dot(a, b):
    return jnp.dot(a.astype(MXU), b.astype(MXU), preferred_element_type=F32)


def _dot_nt(a, b):
    return lax.dot_general(a.astype(MXU), b.astype(MXU), (((1,), (1,)), ((), ())), preferred_element_type=F32)


def _dot_tn(a, b):
    return lax.dot_general(a.astype(MXU), b.astype(MXU), (((0,), (0,)), ((), ())), preferred_element_type=F32)


def _split(a):
    hi = a.astype(jnp.bfloat16)
    return hi, (a - hi.astype(F32)).astype(jnp.bfloat16)


def _dot_split(a, b, dims, exact):
    dg = lambda u, v: lax.dot_general(u, v, (dims, ((), ())), preferred_element_type=F32)
    if exact == "b":
        ah, al = _split(a)
        bh = b.astype(jnp.bfloat16)
        return dg(ah, bh) + dg(al, bh)
    if exact == "a":
        bh, bm = _split(b)
        bl = (b - bh.astype(F32) - bm.astype(F32)).astype(jnp.bfloat16)
        ah = a.astype(jnp.bfloat16)
        return dg(ah, bh) + (dg(ah, bm) + dg(ah, bl))
    ah, al = _split(a)
    bh, bl = _split(b)
    return dg(ah, bh) + (dg(ah, bl) + dg(al, bh))


def _hdot(a, b, exact=None):
    return _dot_split(a, b, ((1,), (0,)), exact)


def _hdot_nt(a, b, exact=None):
    return _dot_split(a, b, ((1,), (1,)), exact)


def _hdot_tn(a, b, exact=None):
    return _dot_split(a, b, ((0,), (0,)), exact)


def _sigmoid(x):
    return 1.0 / (1.0 + jnp.exp(-x))


def _silu(x):
    return x * _sigmoid(x)


def _silu_and_grad(x):
    s = _sigmoid(x)
    return x * s, s * (1.0 + x * (1.0 - s))


def _softplus(x):
    return jnp.maximum(x, 0.0) + jnp.log(1.0 + jnp.exp(-jnp.abs(x)))


def _iota(shape, dim):
    return lax.broadcasted_iota(jnp.int32, shape, dim)


def _lane_col(x, k):
    return jnp.sum(jnp.where(_iota(x.shape, 1) == k, x, 0.0), axis=1, keepdims=True)


def _params(sem=None):
    return pltpu.CompilerParams(dimension_semantics=sem, vmem_limit_bytes=VMEM_LIMIT)


def _t5_bucket(dist):
    max_exact = N_BUCKETS // 2
    d = np.maximum(dist, 1).astype(np.float64)
    large = max_exact + (np.log(d / max_exact) / math.log(MAX_DISTANCE / max_exact)
                         * (N_BUCKETS - max_exact)).astype(np.int32)
    large = np.minimum(large, N_BUCKETS - 1)
    return np.where(dist < max_exact, dist, large).astype(np.int32)


def _bucket_tables():
    qi = np.arange(BLK)[:, None]
    kj = np.arange(2 * BLK)[None, :]
    step = qi - kj + BLK
    band = (step >= 0) & (step <= BLK)
    out = []
    for _, r in PATTERNS:
        b = _t5_bucket(np.clip(step, 0, None) * r)
        out.append(np.where(band, b, -1))
    return np.stack(out).astype(np.int32)


def _group_mats():
    g = np.zeros((D_ATT, 128), np.float32)
    for h in range(ATT_HEADS):
        g[h * HD:(h + 1) * HD, h] = 1.0
    return g, np.ascontiguousarray(g.T)


CHIP_FLIPS = ((1, 0), (0, 1), (1, 1))
ANY_SPEC = pl.BlockSpec(memory_space=pl.ANY)
MESH_ID = pl.DeviceIdType.MESH


def _other_chips():
    x, y = lax.axis_index("x"), lax.axis_index("y")
    return [((1 - x if fx else x), (1 - y if fy else y)) for fx, fy in CHIP_FLIPS]


def _gather_weights(arrs):
    n = len(arrs)

    def body(*refs):
        ins, outs = refs[:n], refs[n:2 * n]
        send, recv, loc = refs[2 * n:]
        x, y, c = (lax.axis_index(a) for a in AXES)
        sib = (x, y, 1 - c)
        chips = _other_chips()
        lin = lambda px, py, pc: 4 * px + 2 * py + pc

        def copy(a, k, block, to, src=None):
            slot = outs[a].at[lin(*block)]
            return pltpu.make_async_remote_copy(src_ref=slot if src is None else src, dst_ref=slot,
                                                send_sem=send.at[a, k], recv_sem=recv.at[a, k],
                                                device_id=to, device_id_type=MESH_ID)

        started = []
        for a in range(n):
            mine = pltpu.make_async_copy(ins[a], outs[a].at[lin(x, y, c)], loc.at[a])
            mine.start()
            started.append(mine)
        firsts = []
        for a in range(n):
            firsts.append(copy(a, 0, (x, y, c), sib, src=ins[a]))
            firsts += [copy(a, 1 + j, (x, y, c), (*chip, c), src=ins[a]) for j, chip in enumerate(chips)]
        for cp in firsts:
            cp.start()
        passed = []
        for j, chip in enumerate(chips):
            for a in range(n):
                copy(a, 1 + j, (*chip, c), (x, y, c)).wait_recv()
                fw = copy(a, 4 + j, (*chip, c), sib)
                fw.start()
                passed.append(fw)
        for a in range(n):
            copy(a, 0, sib, (x, y, c)).wait_recv()
            for j, chip in enumerate(chips):
                copy(a, 4 + j, (*chip, 1 - c), (x, y, c)).wait_recv()
        for cp in firsts + passed:
            cp.wait_send()
        for mine in started:
            mine.wait()

    return pl.pallas_call(
        body, name="gather_weights", out_shape=[jax.ShapeDtypeStruct((N_DEV,) + a.shape, a.dtype) for a in arrs],
        in_specs=[ANY_SPEC] * n, out_specs=[ANY_SPEC] * n,
        scratch_shapes=[pltpu.SemaphoreType.DMA((n, 7)), pltpu.SemaphoreType.DMA((n, 7)), pltpu.SemaphoreType.DMA((n,))],
    )(*arrs)


def _swap_siblings(arrs):
    n = len(arrs)

    def body(*refs):
        ins, outs = refs[:n], refs[n:2 * n]
        send, recv = refs[2 * n:]
        x, y, c = (lax.axis_index(a) for a in AXES)
        cps = [pltpu.make_async_remote_copy(src_ref=ins[a].at[1 - c], dst_ref=outs[a], send_sem=send.at[a],
                                            recv_sem=recv.at[a], device_id=(x, y, 1 - c), device_id_type=MESH_ID)
               for a in range(n)]
        for cp in cps:
            cp.start()
        for cp in cps:
            cp.wait()

    return pl.pallas_call(
        body, name="swap_siblings", out_shape=[jax.ShapeDtypeStruct(a.shape[1:], a.dtype) for a in arrs],
        in_specs=[ANY_SPEC] * n, out_specs=[ANY_SPEC] * n,
        scratch_shapes=[pltpu.SemaphoreType.DMA((n,)), pltpu.SemaphoreType.DMA((n,))],
    )(*arrs)


def _chip_sum(mine2, theirs, core, wire, name):
    _, nchip, r, cdim = mine2.shape
    tr = r if r <= 256 else 256

    def body(core_ref, a_ref, b_ref, o_ref):
        del core_ref
        o_ref[...] = (a_ref[...].astype(F32) + b_ref[...].astype(F32)).astype(wire)

    grid_spec = pltpu.PrefetchScalarGridSpec(
        num_scalar_prefetch=1, grid=(nchip, r // tr),
        in_specs=[pl.BlockSpec((None, None, tr, cdim), lambda j, i, cr: (cr[0], j, i, 0)),
                  pl.BlockSpec((None, tr, cdim), lambda j, i, cr: (j, i, 0))],
        out_specs=pl.BlockSpec((None, tr, cdim), lambda j, i, cr: (j, i, 0)))
    return pl.pallas_call(
        body, name=name, grid_spec=grid_spec, out_shape=jax.ShapeDtypeStruct((nchip, r, cdim), wire),
        compiler_params=_params(("arbitrary", "arbitrary")),
    )(core, mine2, theirs)


def _chip_swap_copies(ins, outs, send, recv, loc):
    x, y, c = (lax.axis_index(a) for a in AXES)
    me = 2 * x + y
    starts, arrivals, drains = [], [], []
    for a in range(len(ins)):
        lc = pltpu.make_async_copy(ins[a].at[me], outs[a].at[me], loc.at[a])
        starts.append(lc.start)
        drains.append(lc.wait)
        for j, (px, py) in enumerate(_other_chips()):
            them = 2 * px + py
            cp = pltpu.make_async_remote_copy(src_ref=ins[a].at[them], dst_ref=outs[a].at[me], send_sem=send.at[a, j],
                                              recv_sem=recv.at[a, j], device_id=(px, py, c), device_id_type=MESH_ID)
            landing = pltpu.make_async_remote_copy(src_ref=ins[a].at[them], dst_ref=outs[a].at[them],
                                                   send_sem=send.at[a, j], recv_sem=recv.at[a, j],
                                                   device_id=(px, py, c), device_id_type=MESH_ID)
            starts.append(cp.start)
            arrivals.append(landing.wait_recv)
            drains.append(cp.wait_send)
    return starts, arrivals + drains


def _share_small(pack):
    def body(in_ref, out_ref, send, recv, loc):
        x, y, c = (lax.axis_index(a) for a in AXES)
        me = 4 * x + 2 * y + c
        lc = pltpu.make_async_copy(in_ref, out_ref.at[me], loc.at[0])
        lc.start()
        sends, arrivals = [], []
        for k in range(1, N_DEV):
            px = 1 - x if k & 4 else x
            py = 1 - y if k & 2 else y
            pc = 1 - c if k & 1 else c
            cp = pltpu.make_async_remote_copy(src_ref=in_ref, dst_ref=out_ref.at[me], send_sem=send.at[k - 1],
                                              recv_sem=recv.at[k - 1], device_id=(px, py, pc), device_id_type=MESH_ID)
            cp.start()
            sends.append(cp)
            arrivals.append(pltpu.make_async_remote_copy(src_ref=in_ref, dst_ref=out_ref.at[4 * px + 2 * py + pc],
                                                         send_sem=send.at[k - 1], recv_sem=recv.at[k - 1],
                                                         device_id=(px, py, pc), device_id_type=MESH_ID))
        for cp in arrivals:
            cp.wait_recv()
        for cp in sends:
            cp.wait_send()
        lc.wait()

    return pl.pallas_call(
        body, name="share_small", out_shape=jax.ShapeDtypeStruct((N_DEV,) + pack.shape, pack.dtype),
        in_specs=[ANY_SPEC], out_specs=ANY_SPEC,
        scratch_shapes=[pltpu.SemaphoreType.DMA((N_DEV - 1,)), pltpu.SemaphoreType.DMA((N_DEV - 1,)),
                        pltpu.SemaphoreType.DMA((1,))],
    )(pack)


W_PARTS = ((0, 0, 2048), (2048, 4096, 8), (2056, 2048, 2048))
SHARD_COLS = 513


def _pieces(lo, hi, parts):
    out = []
    for ref_start, tgt_start, width in parts:
        a, b = max(lo, ref_start), min(hi, ref_start + width)
        if a < b:
            out.append((a - lo, tgt_start + a - ref_start, b - a))
    return out


def _build_w(win8):
    tr = 256

    def body(in_ref, w_ref, wt_ref):
        w_ref[:, 4096:W_COLS] = jnp.zeros((tr, W_COLS - 4096), MXU)
        for p in range(N_DEV):
            for src, dst, width in _pieces(p * SHARD_COLS, (p + 1) * SHARD_COLS, W_PARTS):
                w_ref[:, dst:dst + width] = in_ref[p, :, src:src + width]
        for k in range(W_COLS // 128):
            wt_ref[k * 128:(k + 1) * 128, :] = w_ref[:, k * 128:(k + 1) * 128].astype(F32).T.astype(MXU)

    return pl.pallas_call(
        body, name="build_w", grid=(D_MODEL // tr,),
        in_specs=[pl.BlockSpec((N_DEV, tr, SHARD_COLS), lambda i: (0, i, 0))],
        out_specs=[pl.BlockSpec((tr, W_COLS), lambda i: (i, 0)), pl.BlockSpec((W_COLS, tr), lambda i: (0, i))],
        out_shape=[jax.ShapeDtypeStruct((D_MODEL, W_COLS), MXU), jax.ShapeDtypeStruct((W_COLS, D_MODEL), MXU)],
        compiler_params=_params(("arbitrary",)),
    )(win8)


def _build_slabs(secs):
    tr = 256
    parts = ((0, 0, 1536), (1536, 1, 512), (2048, 6, 8), (2056, 2, 512), (2568, 3, 512), (3080, 4, 512),
             (3592, 5, 512))

    def body(*refs):
        o_ref = refs[len(secs)]
        for p in range(N_DEV):
            lo, hi = p * SHARD_COLS, (p + 1) * SHARD_COLS
            for ref_start, idx, width in parts:
                a, b = max(lo, ref_start), min(hi, ref_start + width)
                if a < b:
                    o_ref[p % 2, p // 2, :, a - lo:b - lo] = refs[idx][:, a - ref_start:b - ref_start]

    return pl.pallas_call(
        body, name="build_slabs", grid=(D_MODEL // tr,),
        in_specs=[pl.BlockSpec((tr, s.shape[1]), lambda i: (i, 0)) for s in secs],
        out_specs=pl.BlockSpec((2, 4, tr, SHARD_COLS), lambda i: (0, 0, i, 0)),
        out_shape=jax.ShapeDtypeStruct((2, 4, D_MODEL, SHARD_COLS), secs[0].dtype),
        compiler_params=_params(("arbitrary",)),
    )(*secs)


def _inproj(x, nw, w):
    t = x.shape[0]
    tm = 256

    def body(x_ref, nw_ref, w_ref, ht_ref, pdn_ref, z_ref, patt_ref, gate_ref, ba_ref):
        xv = x_ref[...]
        rstd = lax.rsqrt(jnp.mean(xv * xv, axis=-1, keepdims=True) + EPS)
        hf = xv * rstd * nw_ref[...]
        h = hf.astype(MXU)
        ht_ref[...] = hf.T.astype(MXU)
        for ref, lo, hi in ((pdn_ref, 0, 1536), (z_ref, 1536, 2048), (patt_ref, 2048, 3584),
                            (gate_ref, 3584, 4096), (ba_ref, 4096, 4224)):
            ref[...] = jnp.dot(h, w_ref[:, lo:hi], preferred_element_type=F32)

    row = lambda n: pl.BlockSpec((tm, n), lambda i: (i, 0))
    full = lambda a: pl.BlockSpec(a.shape, lambda i: (0,) * a.ndim)
    return pl.pallas_call(
        body, name="inproj", grid=(t // tm,),
        in_specs=[row(D_MODEL), full(nw), full(w)],
        out_specs=[pl.BlockSpec((D_MODEL, tm), lambda i: (0, i)), row(1536), row(512), row(1536), row(512), row(128)],
        out_shape=[jax.ShapeDtypeStruct((D_MODEL, t), MXU)] +
                  [jax.ShapeDtypeStruct((t, n), F32) for n in (1536, 512, 1536, 512, 128)],
        compiler_params=_params(("arbitrary",)),
    )(x, nw, w)


CONV_ROWS = 512


def _conv_taps(u_ref, c, w_ref):
    r0 = c * CONV_ROWS
    if c == 0:
        ext = jnp.concatenate([jnp.zeros((8, 128), F32), u_ref[0:CONV_ROWS, :]], axis=0)
    else:
        ext = u_ref[r0 - 8:r0 + CONV_ROWS, :]
    taps = [ext[8:, :]] + [pltpu.roll(ext, k, 0)[8:, :] for k in (1, 2, 3)]
    y = taps[0] * w_ref[3:4, :]
    for k in (1, 2, 3):
        y = y + taps[k] * w_ref[3 - k:4 - k, :]
    return taps, y


def _dn_prep(pdn, cw):
    t = pdn.shape[0]

    def body(u_ref, w_ref, o_ref):
        j = pl.program_id(0)
        for c in range(t // CONV_ROWS):
            _, y = _conv_taps(u_ref, c, w_ref)
            a = _silu(y)
            ssq = jnp.sum(a * a, axis=1, keepdims=True)
            f = jnp.where(j < 8, lax.rsqrt(ssq + EPS), 1.0) * jnp.where(j < 4, DK ** -0.5, 1.0)
            o_ref[c * CONV_ROWS:(c + 1) * CONV_ROWS, :] = a * f

    return pl.pallas_call(
        body, name="dn_prep", grid=(12,),
        in_specs=[pl.BlockSpec((t, 128), lambda j: (0, j)), pl.BlockSpec((4, 128), lambda j: (0, j))],
        out_specs=pl.BlockSpec((t, 128), lambda j: (0, j)),
        out_shape=jax.ShapeDtypeStruct((t, 1536), F32),
        compiler_params=_params(("arbitrary",)),
    )(pdn, cw)


def _chunk_common(qkv, ba, arow, dtb):
    c = CHUNK
    ri, ci = _iota((c, c), 0), _iota((c, c), 1)
    lane = _iota((c, 128), 1)
    g_all = jnp.where((lane >= DN_HEADS) & (lane < 2 * DN_HEADS), arow * _softplus(ba + dtb), 0.0)
    gc_all = _hdot((ri >= ci).astype(F32), g_all, "a")
    gc_t = gc_all.T
    beta_all = _sigmoid(ba)
    out = []
    for h in range(DN_HEADS):
        gc = _lane_col(gc_all, DN_HEADS + h)
        gcr = gc_t[DN_HEADS + h:DN_HEADS + h + 1, :]
        gl = gc[c - 1:c, :]
        out.append(dict(
            q=qkv[:, h * DK:(h + 1) * DK], k=qkv[:, D_DN + h * DK:D_DN + (h + 1) * DK],
            v=qkv[:, 2 * D_DN + h * DK:2 * D_DN + (h + 1) * DK],
            beta=_lane_col(beta_all, h), g=_lane_col(g_all, DN_HEADS + h),
            a_raw=_lane_col(ba, DN_HEADS + h), a_h=_lane_col(arow, DN_HEADS + h), dt_h=_lane_col(dtb, DN_HEADS + h),
            decay=jnp.exp(jnp.where(ri >= ci, gc - gcr, NEG)), eg=jnp.exp(gc), egl=jnp.exp(gl), etail=jnp.exp(gl - gc)))
    return out, ri, ci


SCAN_CHUNKS = 8


def _dn_scan_fwd(qkv, ba, z, arow, dtb, dnw):
    t = qkv.shape[0]
    n = t // CHUNK
    c = CHUNK
    cps = SCAN_CHUNKS
    hs = range(DN_HEADS)
    chains = [(j, h) for j in range(cps) for h in hs]

    def body(qkv_ref, ba_ref, z_ref, arow_ref, dtb_ref, dnw_ref, o_ref, y_ref, sh_ref, th_ref, s_ref):
        @pl.when(pl.program_id(0) == 0)
        def _():
            s_ref[...] = jnp.zeros_like(s_ref)

        ms = {}
        for j in range(cps):
            rows = slice(j * c, (j + 1) * c)
            mj, ri, ci = _chunk_common(qkv_ref[rows, :], ba_ref[rows, :], arow_ref[...], dtb_ref[...])
            for h in hs:
                ms[j, h] = mj[h]
        kb = {x: ms[x]["k"] * ms[x]["beta"] for x in chains}
        amat = {x: jnp.where(ri > ci, _dot_nt(kb[x], ms[x]["k"]) * ms[x]["decay"], 0.0) for x in chains}
        attn = {x: jnp.where(ri >= ci, _dot_nt(ms[x]["q"], ms[x]["k"]) * ms[x]["decay"], 0.0) for x in chains}
        eye = (ri == ci).astype(F32)
        tinv = {x: eye - amat[x] for x in chains}
        pw = amat
        for _ in range(5):
            pw = {x: _hdot(pw[x], pw[x]) for x in chains}
            tinv = {x: tinv[x] + _hdot(tinv[x], pw[x]) for x in chains}
        u = {x: _hdot(tinv[x], ms[x]["v"] * ms[x]["beta"]) for x in chains}
        w = {x: _hdot(tinv[x], kb[x] * ms[x]["eg"]) for x in chains}
        q_dec = {x: ms[x]["q"] * ms[x]["eg"] for x in chains}
        k_tail = {x: ms[x]["k"] * ms[x]["etail"] for x in chains}
        s = [s_ref[h] for h in hs]
        for j in range(cps):
            rows = slice(j * c, (j + 1) * c)
            v_new = [u[j, h] - _dot(w[j, h], s[h]) for h in hs]
            o = [_dot(q_dec[j, h], s[h]) + _dot(attn[j, h], v_new[h]) for h in hs]
            for h in hs:
                sh_ref[j, h] = s[h]
                th_ref[j, h] = tinv[j, h]
            s = [s[h] * ms[j, h]["egl"] + _dot_tn(k_tail[j, h], v_new[h]) for h in hs]
            for h in hs:
                cols = slice(h * DK, (h + 1) * DK)
                o_ref[rows, cols] = o[h]
                rs = lax.rsqrt(jnp.mean(o[h] * o[h], axis=1, keepdims=True) + EPS)
                y_ref[rows, cols] = o[h] * rs * dnw_ref[...] * _silu(z_ref[rows, cols])
        for h in hs:
            s_ref[h] = s[h]

    row = lambda w_: pl.BlockSpec((cps * c, w_), lambda i: (i, 0))
    one = pl.BlockSpec((1, 128), lambda i: (0, 0))
    return pl.pallas_call(
        body, name="dn_scan_fwd", grid=(n // cps,),
        in_specs=[row(1536), row(128), row(512), one, one, one],
        out_specs=[row(512), row(512), pl.BlockSpec((cps, DN_HEADS, DK, DK), lambda i: (i, 0, 0, 0)),
                   pl.BlockSpec((cps, DN_HEADS, c, c), lambda i: (i, 0, 0, 0))],
        out_shape=[jax.ShapeDtypeStruct((t, 512), F32), jax.ShapeDtypeStruct((t, 512), F32),
                   jax.ShapeDtypeStruct((n, DN_HEADS, DK, DK), F32), jax.ShapeDtypeStruct((n, DN_HEADS, c, c), F32)],
        scratch_shapes=[pltpu.VMEM((DN_HEADS, DK, DK), F32)],
        compiler_params=_params(("arbitrary",)),
    )(qkv, ba, z, arow, dtb, dnw)


ATT_ROWS = 512


def _pair_rstd(xv, g2_ref):
    return lax.rsqrt(_hdot(xv * xv, g2_ref[...], "b") * (1.0 / HD) + EPS)


def _pair_norm(t, raw_refs, w_refs, out_refs, g2_ref):
    for c in range(t // ATT_ROWS):
        sl = slice(c * ATT_ROWS, (c + 1) * ATT_ROWS)
        for raw, w_ref, out in zip(raw_refs, w_refs, out_refs):
            xv = raw[sl, :]
            out[sl, :] = xv * _pair_rstd(xv, g2_ref) * w_ref[...]


def _bias_tables(rb, bk):
    def body(rb_ref, bk_ref, bias_ref):
        pair = pl.program_id(0)
        for p in range(len(PATTERNS)):
            bk_p = bk_ref[p]
            for hh in range(2):
                head = 2 * pair + hh
                bm = jnp.full((BLK, 2 * BLK), NEG, F32)
                for b in range(N_BUCKETS):
                    bm = jnp.where(bk_p == b, rb_ref[head, b], bm)
                bias_ref[p, hh * BLK:(hh + 1) * BLK, :] = bm

    return pl.pallas_call(
        body, name="bias_tables", grid=(ATT_HEADS // 2,),
        in_specs=[pl.BlockSpec(memory_space=pltpu.SMEM), pl.BlockSpec(bk.shape, lambda i: (0, 0, 0))],
        out_specs=pl.BlockSpec((len(PATTERNS), None, 2 * BLK, 2 * BLK), lambda i: (0, i, 0, 0)),
        out_shape=jax.ShapeDtypeStruct((len(PATTERNS), ATT_HEADS // 2, 2 * BLK, 2 * BLK), F32),
        compiler_params=_params(("arbitrary",)),
    )(rb, bk)


BIAS_SPEC = pl.BlockSpec((len(PATTERNS), None, 2 * BLK, 2 * BLK), lambda i: (0, i, 0, 0))
PAIR_ROW_SPEC = pl.BlockSpec((1, 128), lambda i: (0, i))


def _stack_heads(xb, h0):
    return jnp.concatenate([jnp.where(h0, xb, 0.0), jnp.where(h0, 0.0, xb)], axis=0).astype(MXU)


def _block_rows(t, r, n):
    per_class = (t // r) // BLK
    res = n // per_class
    j = n % per_class
    start = res + BLK * r * j
    pstart = res + BLK * r * jnp.maximum(j - 1, 0)
    if r == 1:
        return pl.ds(pl.multiple_of(start, BLK), BLK), pl.ds(pl.multiple_of(pstart, BLK), BLK), j
    return pl.ds(start, BLK, stride=r), pl.ds(pstart, BLK, stride=r), j


def _att_fwd(qkv, gate, bias, wq, wk, g2):
    t = qkv.shape[0]
    rows = ATT_ROWS

    def body(bias_ref, qraw_ref, kraw_ref, v_ref, g_ref, wq_ref, wk_ref, g2_ref, o_ref, y_ref, lse_ref,
             o0_ref, o1_ref, o2_ref, l0_ref, l1_ref, l2_ref, q_ref, k_ref):
        h0 = _iota((BLK, 128), 1) < HD
        prev_cols = _iota((2 * BLK, 2 * BLK), 1) < BLK
        op_refs, lp_refs = (o0_ref, o1_ref, o2_ref), (l0_ref, l1_ref, l2_ref)
        _pair_norm(t, (qraw_ref, kraw_ref), (wq_ref, wk_ref), (q_ref, k_ref), g2_ref)

        for p, (_, r) in enumerate(PATTERNS):
            def blk(n, carry, p=p, r=r):
                cur, prev, j = _block_rows(t, r, n)
                q2 = _stack_heads(q_ref[cur, :], h0)
                k2 = jnp.concatenate([k_ref[prev, :], k_ref[cur, :]], axis=0).astype(MXU)
                v2 = jnp.concatenate([v_ref[prev, :], v_ref[cur, :]], axis=0).astype(MXU)
                s = _dot_nt(q2, k2) + bias_ref[p] + jnp.where(prev_cols & (j == 0), NEG, 0.0)
                m = jnp.max(s, axis=1, keepdims=True)
                e = jnp.exp(s - m)
                l = jnp.sum(e, axis=1, keepdims=True)
                pv = _dot(e, v2) / l
                lse = m + jnp.log(l)
                op_refs[p][cur, :] = jnp.where(h0, pv[:BLK], pv[BLK:])
                lp_refs[p][cur, :] = jnp.where(h0, lse[:BLK], lse[BLK:])
                return carry

            lax.fori_loop(0, t // BLK, blk, 0, unroll=8)

        for c in range(t // rows):
            sl = slice(c * rows, (c + 1) * rows)
            ls = [ref[sl, :] for ref in lp_refs]
            mx = jnp.maximum(jnp.maximum(ls[0], ls[1]), ls[2])
            ws = [jnp.exp(v_ - mx) for v_ in ls]
            den = ws[0] + ws[1] + ws[2]
            o = (ws[0] * o0_ref[sl, :] + ws[1] * o1_ref[sl, :] + ws[2] * o2_ref[sl, :]) / den
            o_ref[sl, :] = o
            y_ref[sl, :] = o * _silu(g_ref[sl, :])
            lse_ref[sl, :] = mx + jnp.log(den)

    col = lambda off: pl.BlockSpec((t, 128), lambda i, off=off: (0, off + i))
    return pl.pallas_call(
        body, name="att_fwd", grid=(ATT_HEADS // 2,),
        in_specs=[BIAS_SPEC, col(0), col(4), col(8), col(0), PAIR_ROW_SPEC, PAIR_ROW_SPEC,
                  pl.BlockSpec((128, 128), lambda i: (0, 0))],
        out_specs=[col(0), col(0), col(0)],
        out_shape=[jax.ShapeDtypeStruct((t, 512), F32)] * 3,
        scratch_shapes=[pltpu.VMEM((t, 128), F32)] * 8,
        compiler_params=_params(("arbitrary",)),
    )(bias, qkv, qkv, qkv, gate, wq, wk, g2)


def _outproj_loss(x, ydn, yatt, wout, target):
    t = x.shape[0]
    tm = 512

    def body(x_ref, a_ref, b_ref, w_ref, t_ref, dy_ref, mix_ref, loss_ref):
        @pl.when(pl.program_id(0) == 0)
        def _():
            loss_ref[...] = jnp.zeros_like(loss_ref)

        mixf = jnp.concatenate([a_ref[...], b_ref[...]], axis=1)
        mix_ref[...] = mixf.T.astype(MXU)
        err = x_ref[...] + jnp.dot(mixf.astype(MXU), w_ref[...], preferred_element_type=F32) - t_ref[...]
        dy_ref[...] = err * (1.0 / D_MODEL)
        loss_ref[...] += jnp.sum(err * err) * (0.5 / D_MODEL)

    row = lambda n: pl.BlockSpec((tm, n), lambda i: (i, 0))
    return pl.pallas_call(
        body, name="outproj_loss", grid=(t // tm,),
        in_specs=[row(D_MODEL), row(512), row(512), pl.BlockSpec(wout.shape, lambda i: (0, 0)), row(D_MODEL)],
        out_specs=[row(D_MODEL), pl.BlockSpec((D_MODEL, tm), lambda i: (0, i)), pl.BlockSpec((8, 128), lambda i: (0, 0))],
        out_shape=[jax.ShapeDtypeStruct((t, D_MODEL), F32), jax.ShapeDtypeStruct((D_MODEL, t), MXU),
                   jax.ShapeDtypeStruct((8, 128), F32)],
        compiler_params=_params(("arbitrary",)),
    )(x, ydn, yatt, wout, target)


def _outproj_bwd(dy, wout_t, oraw, z, dnw, oatt, gate, g, gt):
    t = dy.shape[0]
    tm = 256

    def body(dy_ref, w_ref, o_ref, z_ref, dnw_ref, oa_ref, g_ref, grp_ref, grpt_ref,
             do_ref, dz_ref, doa_ref, dg_ref, dd_ref, ddnw_ref):
        @pl.when(pl.program_id(0) == 0)
        def _():
            ddnw_ref[...] = jnp.zeros_like(ddnw_ref)

        dmix = jnp.dot(dy_ref[...].astype(MXU), w_ref[...], preferred_element_type=F32)
        dnw_v = dnw_ref[...]
        acc = jnp.zeros((1, DK), F32)
        for h in range(DN_HEADS):
            sl = slice(h * DK, (h + 1) * DK)
            o, zz, dm = o_ref[:, sl], z_ref[:, sl], dmix[:, sl]
            rs = lax.rsqrt(jnp.mean(o * o, axis=1, keepdims=True) + EPS)
            oh = o * rs
            silu_z, dsilu_z = _silu_and_grad(zz)
            dz_ref[:, sl] = dm * oh * dnw_v * dsilu_z
            d_on = dm * silu_z
            gg = d_on * dnw_v
            do_ref[:, sl] = rs * (gg - oh * jnp.mean(gg * oh, axis=1, keepdims=True))
            acc = acc + jnp.sum(d_on * oh, axis=0, keepdims=True)
        ddnw_ref[...] += jnp.broadcast_to(acc, (8, DK))
        da, gate_v, oa = dmix[:, 512:], g_ref[...], oa_ref[...]
        silu_g, dsilu_g = _silu_and_grad(gate_v)
        doa = da * silu_g
        doa_ref[...] = doa
        dg_ref[...] = da * oa * dsilu_g
        dd_ref[...] = _hdot(_hdot(doa * oa, grp_ref[...], "b"), grpt_ref[...], "b")

    row = lambda n: pl.BlockSpec((tm, n), lambda i: (i, 0))
    full = lambda a: pl.BlockSpec(a.shape, lambda i: (0,) * a.ndim)
    return pl.pallas_call(
        body, name="outproj_bwd", grid=(t // tm,),
        in_specs=[row(D_MODEL), full(wout_t), row(512), row(512), full(dnw), row(512), row(512), full(g), full(gt)],
        out_specs=[row(512)] * 5 + [pl.BlockSpec((8, DK), lambda i: (0, 0))],
        out_shape=[jax.ShapeDtypeStruct((t, 512), F32)] * 5 + [jax.ShapeDtypeStruct((8, DK), F32)],
        compiler_params=_params(("arbitrary",)),
    )(dy, wout_t, oraw, z, dnw, oatt, gate, g, gt)


def _grad_matmul(at, b, name):
    m, t = at.shape
    n = b.shape[1]
    tk = 512
    tn = n if n <= 512 else 512
    nk = t // tk

    def body(a_ref, b_ref, o_ref, acc_ref):
        k = pl.program_id(1)

        @pl.when(k == 0)
        def _():
            acc_ref[...] = jnp.zeros_like(acc_ref)

        acc_ref[...] += jnp.dot(a_ref[...], b_ref[...].astype(MXU), preferred_element_type=F32)

        @pl.when(k == nk - 1)
        def _():
            o_ref[...] = acc_ref[...].astype(GRAD_WIRE)

    return pl.pallas_call(
        body, name=name, grid=(n // tn, nk),
        in_specs=[pl.BlockSpec((m, tk), lambda j, k: (0, k)), pl.BlockSpec((tk, tn), lambda j, k: (k, j))],
        out_specs=pl.BlockSpec((m, tn), lambda j, k: (0, j)),
        out_shape=jax.ShapeDtypeStruct((m, n), GRAD_WIRE),
        scratch_shapes=[pltpu.VMEM((m, tn), F32)],
        compiler_params=_params(("arbitrary", "arbitrary")),
    )(at, b)


def _att_bwd(qkv, do, lse, dd, bias, bk, wq, wk, g2):
    t = qkv.shape[0]
    rows = ATT_ROWS

    def body(bias_ref, bk_ref, qraw_ref, kraw_ref, v_ref, do_ref, lse_ref, dd_ref, wq_ref, wk_ref, g2_ref,
             dq_ref, dk_ref, dv_ref, db_ref, dwq_ref, dwk_ref, ds_ref, q_ref, k_ref):
        pair = pl.program_id(0)

        @pl.when(pair == 0)
        def _():
            db_ref[...] = jnp.zeros_like(db_ref)

        _pair_norm(t, (qraw_ref, kraw_ref), (wq_ref, wk_ref), (q_ref, k_ref), g2_ref)
        ds_ref[...] = jnp.zeros_like(ds_ref)
        for c in range(t // rows):
            sl = slice(c * rows, (c + 1) * rows)
            for ref in (dq_ref, dk_ref, dv_ref):
                ref[sl, :] = jnp.zeros((rows, 128), F32)
        h0 = _iota((BLK, 128), 1) < HD
        prev_cols = _iota((2 * BLK, 2 * BLK), 1) < BLK

        def rows_of(xb):
            return jnp.concatenate([xb[:, 0:1], xb[:, HD:HD + 1]], axis=0)

        for p, (_, r) in enumerate(PATTERNS):
            def blk(n, carry, p=p, r=r):
                cur, prev, j = _block_rows(t, r, n)
                q2, do2 = _stack_heads(q_ref[cur, :], h0), _stack_heads(do_ref[cur, :], h0)
                k2 = jnp.concatenate([k_ref[prev, :], k_ref[cur, :]], axis=0).astype(MXU)
                v2 = jnp.concatenate([v_ref[prev, :], v_ref[cur, :]], axis=0).astype(MXU)
                s = _dot_nt(q2, k2) + bias_ref[p] + jnp.where(prev_cols & (j == 0), NEG, 0.0)
                prob = jnp.exp(s - rows_of(lse_ref[cur, :]))
                ds = prob * (_dot_nt(do2, v2) - rows_of(dd_ref[cur, :]))
                ds_ref[p] += ds
                dq2 = _dot(ds, k2)
                dk2 = _dot_tn(ds, q2)
                dv2 = _dot_tn(prob, do2)
                dq_ref[cur, :] += jnp.where(h0, dq2[:BLK], dq2[BLK:])
                dk_ref[prev, :] += dk2[:BLK]
                dv_ref[prev, :] += dv2[:BLK]
                dk_ref[cur, :] += dk2[BLK:]
                dv_ref[cur, :] += dv2[BLK:]
                return carry

            lax.fori_loop(0, t // BLK, blk, 0, unroll=4)

        ri, ci = _iota((8, 128), 0), _iota((8, 128), 1)
        upd = jnp.zeros((8, 128), F32)
        for p in range(len(PATTERNS)):
            bk = bk_ref[p]
            for hh in range(2):
                dsum = ds_ref[p, hh * BLK:(hh + 1) * BLK, :]
                for b in range(N_BUCKETS):
                    val = jnp.sum(jnp.where(bk == b, dsum, 0.0))
                    upd = upd + jnp.where((ri == 2 * pair + hh) & (ci == b), val, 0.0)
        db_ref[...] += upd

        for raw, d_ref, w_ref, dw_ref in ((qraw_ref, dq_ref, wq_ref, dwq_ref), (kraw_ref, dk_ref, wk_ref, dwk_ref)):
            acc = jnp.zeros((1, 128), F32)
            for c in range(t // rows):
                sl = slice(c * rows, (c + 1) * rows)
                xv, dyv = raw[sl, :], d_ref[sl, :]
                rs = _pair_rstd(xv, g2_ref)
                xh = xv * rs
                gg = dyv * w_ref[...]
                mean = _hdot(gg * xh, g2_ref[...], "b") * (1.0 / HD)
                d_ref[sl, :] = rs * (gg - xh * mean)
                acc = acc + jnp.sum(dyv * xh, axis=0, keepdims=True)
            dw_ref[...] = jnp.broadcast_to(acc, (8, 128))

    col = lambda off: pl.BlockSpec((t, 128), lambda i, off=off: (0, off + i))
    acc8 = pl.BlockSpec((8, 128), lambda i: (0, i))
    return pl.pallas_call(
        body, name="att_bwd", grid=(ATT_HEADS // 2,),
        in_specs=[BIAS_SPEC, pl.BlockSpec(bk.shape, lambda i: (0, 0, 0)),
                  col(0), col(4), col(8), col(0), col(0), col(0), PAIR_ROW_SPEC, PAIR_ROW_SPEC,
                  pl.BlockSpec((128, 128), lambda i: (0, 0))],
        out_specs=[col(0), col(0), col(0), pl.BlockSpec((8, 128), lambda i: (0, 0)), acc8, acc8],
        out_shape=[jax.ShapeDtypeStruct((t, 512), F32)] * 3 + [jax.ShapeDtypeStruct((8, 128), F32)]
                  + [jax.ShapeDtypeStruct((8, 512), F32)] * 2,
        scratch_shapes=[pltpu.VMEM((len(PATTERNS), 2 * BLK, 2 * BLK), F32), pltpu.VMEM((t, 128), F32),
                        pltpu.VMEM((t, 128), F32)],
        compiler_params=_params(("arbitrary",)),
    )(bias, bk, qkv, qkv, qkv, do, lse, dd, wq, wk, g2)


def _dn_scan_bwd(qkv, ba, do, sh, th, arow, dtb):
    t = qkv.shape[0]
    n = t // CHUNK
    c = CHUNK
    cps = SCAN_CHUNKS

    def body(qkv_ref, ba_ref, do_ref, sh_ref, th_ref, arow_ref, dtb_ref, dqkv_ref, dba_ref, ds_ref):
        @pl.when(pl.program_id(0) == 0)
        def _():
            ds_ref[...] = jnp.zeros_like(ds_ref)

        hs = range(DN_HEADS)
        chains = [(j, h) for j in range(cps) for h in hs]
        lane = _iota((c, 128), 1)
        row = _iota((c, 1), 0)
        ms = {}
        for j in range(cps):
            rows_j = slice(j * c, (j + 1) * c)
            mj, ri, ci = _chunk_common(qkv_ref[rows_j, :], ba_ref[rows_j, :], arow_ref[...], dtb_ref[...])
            for h in hs:
                ms[j, h] = mj[h]
        q, k, v = ({x: ms[x][nm] for x in chains} for nm in ("q", "k", "v"))
        beta, decay = ({x: ms[x][nm] for x in chains} for nm in ("beta", "decay"))
        eg, egl, etail = ({x: ms[x][nm] for x in chains} for nm in ("eg", "egl", "etail"))
        s = {x: sh_ref[x[0], x[1]] for x in chains}
        tinv = {x: th_ref[x[0], x[1]] for x in chains}
        d_o = {(j, h): do_ref[j * c:(j + 1) * c, h * DK:(h + 1) * DK] for j, h in chains}
        kb = {x: k[x] * beta[x] for x in chains}
        vb = {x: v[x] * beta[x] for x in chains}
        kbg = {x: kb[x] * eg[x] for x in chains}
        amat = {x: jnp.where(ri > ci, _dot_nt(kb[x], k[x]) * decay[x], 0.0) for x in chains}
        attn = {x: jnp.where(ri >= ci, _dot_nt(q[x], k[x]) * decay[x], 0.0) for x in chains}
        u = {x: _hdot(tinv[x], vb[x]) for x in chains}
        w = {x: _hdot(tinv[x], kbg[x]) for x in chains}
        v_new = {x: u[x] - _dot(w[x], s[x]) for x in chains}
        q_dec = {x: q[x] * eg[x] for x in chains}
        k_tail = {x: k[x] * etail[x] for x in chains}
        d_attn = {x: jnp.where(ri >= ci, _dot_nt(d_o[x], v_new[x]), 0.0) for x in chains}
        d_qdec = {x: _dot_nt(d_o[x], s[x]) for x in chains}
        from_o = {x: _dot_tn(attn[x], d_o[x]) for x in chains}
        to_state = {x: _dot_tn(q_dec[x], d_o[x]) for x in chains}

        d_s, d_vnew = {}, {}
        cur = [ds_ref[h] for h in hs]
        for j in reversed(range(cps)):
            for h in hs:
                d_s[j, h] = cur[h]
                d_vnew[j, h] = from_o[j, h] + _dot(k_tail[j, h], cur[h])
            cur = [to_state[j, h] + cur[h] * egl[j, h] - _dot_tn(w[j, h], d_vnew[j, h]) for h in hs]
        for h in hs:
            ds_ref[h] = cur[h]

        d_ktail = {x: _dot_nt(v_new[x], d_s[x]) for x in chains}
        d_gl = {x: jnp.sum(s[x] * d_s[x]) * egl[x] for x in chains}
        d_w = {x: -_dot_nt(d_vnew[x], s[x]) for x in chains}
        d_vb = {x: _hdot_tn(tinv[x], d_vnew[x]) for x in chains}
        d_kbg = {x: _hdot_tn(tinv[x], d_w[x]) for x in chains}
        d_a = {x: -jnp.where(ri > ci, _hdot_nt(d_vb[x], u[x]) + _hdot_nt(d_kbg[x], w[x]), 0.0) for x in chains}
        d_qk = {x: d_attn[x] * decay[x] for x in chains}
        d_kk = {x: d_a[x] * decay[x] for x in chains}
        d_kb = {x: _dot(d_kk[x], k[x]) + d_kbg[x] * eg[x] for x in chains}
        d_q = {x: _dot(d_qk[x], k[x]) + d_qdec[x] * eg[x] for x in chains}
        d_k = {x: _dot_tn(d_qk[x], q[x]) + _dot_tn(d_kk[x], kb[x]) + d_ktail[x] * etail[x] + d_kb[x] * beta[x]
               for x in chains}
        d_beta = {x: jnp.sum(d_kb[x] * k[x] + d_vb[x] * v[x], axis=1, keepdims=True) for x in chains}
        mm = {x: d_a[x] * amat[x] + d_attn[x] * attn[x] for x in chains}
        for j in range(cps):
            rows_j = slice(j * c, (j + 1) * c)
            rows = jnp.zeros((c, c), F32)
            for h in hs:
                rows = rows + jnp.where(ri == h, jnp.sum(mm[j, h], axis=0, keepdims=True), 0.0)
            cols_t = jnp.concatenate([rows, jnp.zeros((c, c), F32)], axis=1).T[:c, :]
            d_gc_all = jnp.zeros((c, 128), F32)
            for h in hs:
                x = (j, h)
                tail_term = jnp.sum(d_ktail[x] * k_tail[x], axis=1, keepdims=True)
                d_gc = (jnp.sum(mm[x], axis=1, keepdims=True) - _lane_col(cols_t, h)
                        + jnp.sum(d_qdec[x] * q_dec[x] + d_kbg[x] * kbg[x], axis=1, keepdims=True) - tail_term)
                d_gc = d_gc + jnp.where(row == c - 1, jnp.sum(tail_term) + d_gl[x], 0.0)
                d_gc_all = d_gc_all + jnp.where(lane == DN_HEADS + h, d_gc, 0.0)
            d_g_all = _hdot((ri <= ci).astype(F32), d_gc_all, "a")
            dba = jnp.zeros((c, 128), F32)
            for h in hs:
                x = (j, h)
                d_g = _lane_col(d_g_all, DN_HEADS + h)
                d_braw = d_beta[x] * beta[x] * (1.0 - beta[x])
                d_araw = d_g * ms[x]["a_h"] * _sigmoid(ms[x]["a_raw"] + ms[x]["dt_h"])
                dba = dba + jnp.where(lane == h, d_braw, 0.0) + jnp.where(lane == DN_HEADS + h, d_araw, 0.0) \
                    + jnp.where(lane == 2 * DN_HEADS + h, d_g * ms[x]["g"], 0.0)
                dqkv_ref[rows_j, h * DK:(h + 1) * DK] = d_q[x]
                dqkv_ref[rows_j, D_DN + h * DK:D_DN + (h + 1) * DK] = d_k[x]
                dqkv_ref[rows_j, 2 * D_DN + h * DK:2 * D_DN + (h + 1) * DK] = d_vb[x] * beta[x]
            dba_ref[rows_j, :] = dba

    nsteps = n // cps
    rev = lambda w_: pl.BlockSpec((cps * c, w_), lambda i: (nsteps - 1 - i, 0))
    one = pl.BlockSpec((1, 128), lambda i: (0, 0))
    return pl.pallas_call(
        body, name="dn_scan_bwd", grid=(nsteps,),
        in_specs=[rev(1536), rev(128), rev(512),
                  pl.BlockSpec((cps, DN_HEADS, DK, DK), lambda i: (nsteps - 1 - i, 0, 0, 0)),
                  pl.BlockSpec((cps, DN_HEADS, c, c), lambda i: (nsteps - 1 - i, 0, 0, 0)), one, one],
        out_specs=[rev(1536), rev(128)],
        out_shape=[jax.ShapeDtypeStruct((t, 1536), F32), jax.ShapeDtypeStruct((t, 128), F32)],
        scratch_shapes=[pltpu.VMEM((DN_HEADS, DK, DK), F32)],
        compiler_params=_params(("arbitrary",)),
    )(qkv, ba, do, sh, th, arow, dtb)


def _dn_prep_bwd(pdn, cw, dact):
    t = pdn.shape[0]
    nchunk = t // CONV_ROWS

    def body(u_ref, w_ref, d_ref, du_ref, dw_ref, dy_ref):
        j = pl.program_id(0)
        dy_ref[t:t + 8, :] = jnp.zeros((8, 128), F32)
        dw = [jnp.zeros((1, 128), F32) for _ in range(4)]
        for c in range(nchunk):
            sl = slice(c * CONV_ROWS, (c + 1) * CONV_ROWS)
            taps, y = _conv_taps(u_ref, c, w_ref)
            a, da_dy = _silu_and_grad(y)
            dout = d_ref[sl, :]
            rs = lax.rsqrt(jnp.sum(a * a, axis=1, keepdims=True) + EPS)
            f = jnp.where(j < 8, rs, 1.0) * jnp.where(j < 4, DK ** -0.5, 1.0)
            corr = jnp.where(j < 8, f * rs * rs * jnp.sum(dout * a, axis=1, keepdims=True), 0.0)
            dy = (f * dout - corr * a) * da_dy
            dy_ref[sl, :] = dy
            for k_ in range(4):
                dw[3 - k_] = dw[3 - k_] + jnp.sum(taps[k_] * dy, axis=0, keepdims=True)
        for i in range(4):
            dw_ref[i:i + 1, :] = dw[i]
        for c in range(nchunk):
            r0 = c * CONV_ROWS
            ext = dy_ref[r0:r0 + CONV_ROWS + 8, :]
            du = ext[:CONV_ROWS, :] * w_ref[3:4, :]
            for k_ in (1, 2, 3):
                du = du + pltpu.roll(ext, CONV_ROWS + 8 - k_, 0)[:CONV_ROWS, :] * w_ref[3 - k_:4 - k_, :]
            du_ref[r0:r0 + CONV_ROWS, :] = du

    return pl.pallas_call(
        body, name="dn_prep_bwd", grid=(12,),
        in_specs=[pl.BlockSpec((t, 128), lambda j: (0, j)), pl.BlockSpec((4, 128), lambda j: (0, j)),
                  pl.BlockSpec((t, 128), lambda j: (0, j))],
        out_specs=[pl.BlockSpec((t, 128), lambda j: (0, j)), pl.BlockSpec((4, 128), lambda j: (0, j))],
        out_shape=[jax.ShapeDtypeStruct((t, 1536), F32), jax.ShapeDtypeStruct((4, 1536), F32)],
        scratch_shapes=[pltpu.VMEM((t + 8, 128), F32)],
        compiler_params=_params(("arbitrary",)),
    )(pdn, cw, dact)


SECTIONS = (("dn", 0, 1536), ("z", 1536, 512), ("q", 2048, 512), ("k", 2560, 512), ("v", 3072, 512),
            ("gate", 3584, 512), ("ba", 4096, 128))


def _inproj_bwd(x, nw, wt, dy, dsecs, partials):
    t = x.shape[0]
    tm = 256
    npart = len(partials)
    nsteps = t // tm

    nsec = len(SECTIONS)

    def body(x_ref, nw_ref, w_ref, dy_ref, *rest):
        sec_refs, rest = rest[:nsec], rest[nsec:]
        part_refs, (gx_ref, dnw_ref, cs_ref) = rest[:npart], rest[npart:npart + 3]
        got_refs, (send, recv, loc) = rest[npart + 3:2 * npart + 3], rest[2 * npart + 3:]
        starts, waits = _chip_swap_copies(part_refs, got_refs, send, recv, loc)

        @pl.when(pl.program_id(0) == 0)
        def _():
            for start in starts:
                start()
            dnw_ref[...] = jnp.zeros_like(dnw_ref)
            cs_ref[...] = jnp.zeros_like(cs_ref)

        @pl.when(pl.program_id(0) == nsteps - 1)
        def _():
            for wait in waits:
                wait()

        dh = jnp.zeros((tm, D_MODEL), F32)
        for ref, (_, lo, width) in zip(sec_refs, SECTIONS):
            dh = dh + jnp.dot(ref[...].astype(MXU), w_ref[lo:lo + width, :], preferred_element_type=F32)
        xv = x_ref[...]
        rstd = lax.rsqrt(jnp.mean(xv * xv, axis=-1, keepdims=True) + EPS)
        xh = xv * rstd
        gg = dh * nw_ref[...]
        gx_ref[...] = rstd * (gg - xh * jnp.mean(gg * xh, axis=-1, keepdims=True)) + dy_ref[...]
        dnw_ref[...] += jnp.broadcast_to(jnp.sum(dh * xh, axis=0, keepdims=True), (8, D_MODEL))
        cs_ref[...] += jnp.broadcast_to(jnp.sum(sec_refs[nsec - 1][...], axis=0, keepdims=True), (8, 128))

    row = lambda n: pl.BlockSpec((tm, n), lambda i: (i, 0))
    full = lambda a: pl.BlockSpec(a.shape, lambda i: (0,) * a.ndim)
    res = pl.pallas_call(
        body, name="inproj_bwd", grid=(nsteps,),
        in_specs=[row(D_MODEL), full(nw), full(wt), row(D_MODEL)] + [row(width) for _, _, width in SECTIONS]
                 + [ANY_SPEC] * npart,
        out_specs=[row(D_MODEL), pl.BlockSpec((8, D_MODEL), lambda i: (0, 0)), pl.BlockSpec((8, 128), lambda i: (0, 0))]
                  + [ANY_SPEC] * npart,
        out_shape=[jax.ShapeDtypeStruct((t, D_MODEL), F32), jax.ShapeDtypeStruct((8, D_MODEL), F32),
                   jax.ShapeDtypeStruct((8, 128), F32)] + [jax.ShapeDtypeStruct(p.shape, p.dtype) for p in partials],
        scratch_shapes=[pltpu.SemaphoreType.DMA((npart, 3)), pltpu.SemaphoreType.DMA((npart, 3)),
                        pltpu.SemaphoreType.DMA((npart,))],
        compiler_params=_params(("arbitrary",)),
    )(x, nw, wt, dy, *dsecs, *partials)
    return res[0], res[1], res[2], res[3:]


def _adamw_sum(w, gs, m, v, name):
    r, c = w.shape
    nsum = gs.shape[0]
    tr = r if r <= 256 else 256
    c1 = 1.0 - ADAM_B1 ** ADAM_STEP
    c2 = 1.0 - ADAM_B2 ** ADAM_STEP

    def body(w_ref, g_ref, m_ref, v_ref, go_ref, d_ref, mo_ref, vo_ref):
        g = g_ref[0].astype(F32)
        for s in range(1, nsum):
            g = g + g_ref[s].astype(F32)
        mn = ADAM_B1 * m_ref[...] + (1.0 - ADAM_B1) * g
        vn = ADAM_B2 * v_ref[...] + (1.0 - ADAM_B2) * (g * g)
        go_ref[...] = g
        mo_ref[...] = mn
        vo_ref[...] = vn
        d_ref[...] = -ADAM_LR * ((mn / c1) / (jnp.sqrt(vn / c2) + ADAM_EPS) + ADAM_WD * w_ref[...])

    blk = pl.BlockSpec((tr, c), lambda i: (i, 0))
    return pl.pallas_call(
        body, name=name, grid=(r // tr,),
        in_specs=[blk, pl.BlockSpec((nsum, tr, c), lambda i: (0, i, 0)), blk, blk],
        out_specs=[blk] * 4, out_shape=[jax.ShapeDtypeStruct((r, c), F32)] * 4,
        compiler_params=_params(("arbitrary",)),
    )(w, gs, m, v)


def _local_step(x, target, norm_w, w_sect, w_sect_t, conv_w, a_log, dt_bias, dn_norm_w, q_norm_w, k_norm_w, rel_bias,
                w_out):
    lane = np.arange(128)
    arow = jnp.zeros((1, 128), F32).at[0, DN_HEADS:2 * DN_HEADS].set(-jnp.exp(a_log[0]))
    dtb = jnp.zeros((1, 128), F32).at[0, DN_HEADS:2 * DN_HEADS].set(dt_bias[0])
    g_np, gt_np = _group_mats()
    g, gt = jnp.asarray(g_np), jnp.asarray(gt_np)
    g2 = jnp.asarray(np.kron(np.eye(2, dtype=np.float32), np.ones((HD, HD), np.float32)))
    bk = jnp.asarray(_bucket_tables())
    wq = jnp.tile(q_norm_w, (1, ATT_HEADS)) * (HD ** -0.5)
    wk = jnp.tile(k_norm_w, (1, ATT_HEADS))
    del lane

    ht, pdn, z, patt, gate, ba = _inproj(x, norm_w, w_sect)
    qkv_dn = _dn_prep(pdn, conv_w)
    oraw, ydn, sh, th = _dn_scan_fwd(qkv_dn, ba, z, arow, dtb, dn_norm_w)
    bias = _bias_tables(rel_bias, bk)
    oatt, yatt, lse = _att_fwd(patt, gate, bias, wq, wk, g2)
    dy, mix_t, loss8 = _outproj_loss(x, ydn, yatt, w_out, target)

    do_dn, dz, do_att, dgate, dd, ddnw = _outproj_bwd(dy, w_out.T, oraw, z, dn_norm_w, oatt, gate, g, gt)
    d_wout = _grad_matmul(mix_t, dy, "dw_out")
    dq, dk, dv, drb, dwq8, dwk8 = _att_bwd(patt, do_att, lse, dd, bias, bk, wq, wk, g2)
    dqkv_dn, dba = _dn_scan_bwd(qkv_dn, ba, do_dn, sh, th, arow, dtb)
    dpdn, d_conv = _dn_prep_bwd(pdn, conv_w, dqkv_dn)
    dsecs = (dpdn, dz, dq, dk, dv, dgate, dba)
    dw_sections = [_grad_matmul(ht, d_, "dw_in_" + nm) for d_, (nm, _, _) in zip(dsecs, SECTIONS)]
    return dict(w_in_sections=dw_sections, conv_w=d_conv, w_out=d_wout, dy=dy, dsecs=dsecs,
                small_parts=(loss8, ddnw, dwq8, dwk8, drb))


def _finish_step(x, norm_w, w_sect_t, gr, partials):
    grad_x, dnw8, cs8, got = _inproj_bwd(x, norm_w, w_sect_t, gr["dy"], gr["dsecs"], partials)
    return grad_x, _pack_small_grads(dnw8, cs8, *gr["small_parts"]), got


SMALL_ROWS = 24
SMALL_AT = dict(a_log=(slice(8, 9), slice(0, 4)), dt_bias=(slice(9, 10), slice(0, 4)),
                dn_norm_w=(slice(10, 11), slice(0, 128)), q_norm_w=(slice(11, 12), slice(0, HD)),
                k_norm_w=(slice(12, 13), slice(0, HD)), rel_bias=(slice(16, 24), slice(0, N_BUCKETS)))
SMALL_NAMES = ("norm_w", "a_log", "dt_bias", "dn_norm_w", "q_norm_w", "k_norm_w", "rel_bias")


LOSS_ROW = 13


def _pack_small_grads(dnw8, cs8, loss8, ddnw8, dwq8, dwk8, drb):
    def body(dnw_ref, cs_ref, loss_ref, ddnw_ref, dwq_ref, dwk_ref, drb_ref, o_ref):
        lane = _iota((8, 128), 1)
        o_ref[...] = jnp.zeros_like(o_ref)
        o_ref[LOSS_ROW:LOSS_ROW + 1, :] = jnp.where(lane == 0, loss_ref[...], 0.0)[0:1, :]
        for k in range(D_MODEL // 128):
            o_ref[k:k + 1, :] = dnw_ref[0:1, k * 128:(k + 1) * 128]
        cs = cs_ref[...]
        o_ref[8:9, :] = jnp.where(lane < DN_HEADS, pltpu.roll(cs, 128 - 2 * DN_HEADS, 1), 0.0)[0:1, :]
        o_ref[9:10, :] = jnp.where(lane < DN_HEADS, pltpu.roll(cs, 128 - DN_HEADS, 1), 0.0)[0:1, :]
        o_ref[10:11, :] = ddnw_ref[0:1, :]
        for row, ref, scale in ((11, dwq_ref, HD ** -0.5), (12, dwk_ref, 1.0)):
            acc = ref[:, 0:128] + ref[:, 128:256] + ref[:, 256:384] + ref[:, 384:512]
            acc = (acc + pltpu.roll(acc, HD, 1)) * scale
            o_ref[row:row + 1, :] = jnp.where(lane < HD, acc, 0.0)[0:1, :]
        o_ref[16:24, :] = drb_ref[...]

    return pl.pallas_call(body, name="pack_small_grads", out_shape=jax.ShapeDtypeStruct((SMALL_ROWS, 128), F32),
                          )(dnw8, cs8, loss8, ddnw8, dwq8, dwk8, drb)


def _adam_math(w, g, m, v):
    c1 = 1.0 - ADAM_B1 ** ADAM_STEP
    c2 = 1.0 - ADAM_B2 ** ADAM_STEP
    mn = ADAM_B1 * m + (1.0 - ADAM_B1) * g
    vn = ADAM_B2 * v + (1.0 - ADAM_B2) * (g * g)
    return -ADAM_LR * ((mn / c1) / (jnp.sqrt(vn / c2) + ADAM_EPS) + ADAM_WD * w), mn, vn


def _adamw_small(gs, ws, ms, vs):
    n = len(SMALL_NAMES)

    def body(g_ref, *refs):
        w_refs, m_refs, v_refs = refs[:n], refs[n:2 * n], refs[2 * n:3 * n]
        outs, loss_ref = refs[3 * n:7 * n], refs[7 * n]
        loss = g_ref[0, LOSS_ROW:LOSS_ROW + 1, :]
        for s in range(1, gs.shape[0]):
            loss = loss + g_ref[s, LOSS_ROW:LOSS_ROW + 1, :]
        loss_ref[...] = loss

        def one(i, rows, lanes, at):
            g = g_ref[0, rows, lanes]
            for s in range(1, gs.shape[0]):
                g = g + g_ref[s, rows, lanes]
            d, mn, vn = _adam_math(w_refs[i][at], g, m_refs[i][at], v_refs[i][at])
            for kind, val in enumerate((g, d, mn, vn)):
                outs[kind * n + i][at] = val

        for k in range(D_MODEL // 128):
            one(0, slice(k, k + 1), slice(0, 128), (slice(0, 1), slice(k * 128, (k + 1) * 128)))
        for i, nm in enumerate(SMALL_NAMES[1:], start=1):
            rows, lanes = SMALL_AT[nm]
            one(i, rows, lanes, (slice(None), slice(None)))

    shapes = [jax.ShapeDtypeStruct(w.shape, F32) for w in ws]
    res = pl.pallas_call(body, name="adamw_small",
                         out_shape=shapes * 4 + [jax.ShapeDtypeStruct((1, 128), F32)])(gs, *ws, *ms, *vs)
    return [res[k * n:(k + 1) * n] for k in range(4)], res[4 * n]


def kernel(x, norm_w, w_in, conv_w, a_log, dt_bias, dn_norm_w, q_norm_w, k_norm_w, rel_bias, w_out, loss_target, m_norm_w, m_w_in, m_conv_w, m_a_log, m_dt_bias, m_dn_norm_w, m_q_norm_w, m_k_norm_w, m_rel_bias, m_w_out, v_norm_w, v_w_in, v_conv_w, v_a_log, v_dt_bias, v_dn_norm_w, v_q_norm_w, v_k_norm_w, v_rel_bias, v_w_out):
    assert w_in.shape[2] == SHARD_COLS
    win8, wout8, conv8 = _gather_weights([w_in[0].astype(MXU), w_out[0].astype(MXU), conv_w[0]])
    w_sect, w_sect_t = _build_w(win8)
    wout_full = wout8.reshape(D_MODEL, D_MODEL)
    conv_full = conv8.transpose(1, 0, 2).reshape(4, 3 * D_DN)

    gr = _local_step(x[0], loss_target[0], norm_w, w_sect, w_sect_t, conv_full, a_log, dt_bias, dn_norm_w, q_norm_w,
                     k_norm_w, rel_bias, wout_full)

    slabs = [_build_slabs(gr["w_in_sections"]),
             gr["w_out"].reshape(4, 2, D_MODEL // N_DEV, D_MODEL).transpose(1, 0, 2, 3),
             gr["conv_w"].reshape(4, 4, 2, 3 * D_DN // N_DEV).transpose(2, 1, 0, 3)]
    core = lax.axis_index("c").astype(jnp.int32).reshape(1)
    from_sibling = _swap_siblings(slabs)
    wires = (GRAD_WIRE, GRAD_WIRE, F32)
    partial = [_chip_sum(slabs[i], from_sibling[i], core, wires[i], "chip_sum_%d" % i) for i in range(3)]
    grad_x, small_pack, (r_win, r_wout, r_conv) = _finish_step(x[0], norm_w, w_sect_t, gr, partial)
    r_small = _share_small(small_pack)

    g_win, d_win, m_win, v_win = _adamw_sum(w_in[0], r_win, m_w_in[0], v_w_in[0], "adamw_w_in")
    g_wout, d_wout, m_wout, v_wout = _adamw_sum(w_out[0], r_wout, m_w_out[0], v_w_out[0], "adamw_w_out")
    g_conv, d_conv, m_conv, v_conv = _adamw_sum(conv_w[0], r_conv, m_conv_w[0], v_conv_w[0], "adamw_conv_w")
    small, loss_row = _adamw_small(r_small,
                                   (norm_w, a_log, dt_bias, dn_norm_w, q_norm_w, k_norm_w, rel_bias),
                                   (m_norm_w, m_a_log, m_dt_bias, m_dn_norm_w, m_q_norm_w, m_k_norm_w, m_rel_bias),
                                   (v_norm_w, v_a_log, v_dt_bias, v_dn_norm_w, v_q_norm_w, v_k_norm_w, v_rel_bias))

    loss = loss_row[0, 0]
    names = ("norm_w", "w_in", "conv_w", "a_log", "dt_bias", "dn_norm_w", "q_norm_w", "k_norm_w", "rel_bias", "w_out")
    big = dict(w_in=(g_win, d_win, m_win, v_win), conv_w=(g_conv, d_conv, m_conv, v_conv),
               w_out=(g_wout, d_wout, m_wout, v_wout))
    outs = [loss, grad_x[None]]
    for kind in range(4):
        for nm in names:
            outs.append(big[nm][kind][None] if nm in big else small[kind][SMALL_NAMES.index(nm)])
    return tuple(outs)
```

```python
import functools
import math

import numpy as np
import jax
import jax.numpy as jnp
from jax import lax
from jax.experimental import pallas as pl
from jax.experimental.pallas import tpu as pltpu

F32 = jnp.float32
MXU = jnp.bfloat16
GRAD_WIRE = jnp.bfloat16
HI = lax.Precision.HIGHEST

D_MODEL = 1024
D_DN = 512
DN_HEADS = 4
DK = 128
CHUNK = 64
D_ATT = 512
ATT_HEADS = 8
HD = 64
PATTERNS = ((128, 1), (512, 4), (2048, 16))
BLK = 128
N_BUCKETS = 32
MAX_DISTANCE = 2048
EPS = 1e-6
W_COLS = 4224
N_DEV = 8
AXES = ("x", "y", "c")

ADAM_LR = 0.001
ADAM_B1 = 0.9
ADAM_B2 = 0.999
ADAM_EPS = 1e-08
ADAM_WD = 0.01
ADAM_STEP = 10

VMEM_LIMIT = 56 * 1024 * 1024
NEG = -1e30


def _dot(a, b):
    return jnp.dot(a.astype(MXU), b.astype(MXU), preferred_element_type=F32)


def _dot_nt(a, b):
    return lax.dot_general(a.astype(MXU), b.astype(MXU), (((1,), (1,)), ((), ())), preferred_element_type=F32)


def _dot_tn(a, b):
    return lax.dot_general(a.astype(MXU), b.astype(MXU), (((0,), (0,)), ((), ())), preferred_element_type=F32)


def _split(a):
    hi = a.astype(jnp.bfloat16)
    return hi, (a - hi.astype(F32)).astype(jnp.bfloat16)


def _dot_split(a, b, dims, exact):
    dg = lambda u, v: lax.dot_general(u, v, (dims, ((), ())), preferred_element_type=F32)
    if exact == "b":
        ah, al = _split(a)
        bh = b.astype(jnp.bfloat16)
        return dg(ah, bh) + dg(al, bh)
    if exact == "a":
        bh, bm = _split(b)
        bl = (b - bh.astype(F32) - bm.astype(F32)).astype(jnp.bfloat16)
        ah = a.astype(jnp.bfloat16)
        return dg(ah, bh) + (dg(ah, bm) + dg(ah, bl))
    ah, al = _split(a)
    bh, bl = _split(b)
    return dg(ah, bh) + (dg(ah, bl) + dg(al, bh))


def _wy_inverses(amat, eye):
    tinv = {x: eye - amat[x] for x in amat}
    pw = amat
    for _ in range(5):
        pw = {x: _hdot(pw[x], pw[x]) for x in amat}
        tinv = {x: tinv[x] + _hdot(tinv[x], pw[x]) for x in amat}
    return tinv


def _hdot(a, b, exact=None):
    return _dot_split(a, b, ((1,), (0,)), exact)


def _hdot_nt(a, b, exact=None):
    return _dot_split(a, b, ((1,), (1,)), exact)


def _hdot_tn(a, b, exact=None):
    return _dot_split(a, b, ((0,), (0,)), exact)


def _sigmoid(x):
    return 1.0 / (1.0 + jnp.exp(-x))


def _silu(x):
    return x * _sigmoid(x)


def _silu_and_grad(x):
    s = _sigmoid(x)
    return x * s, s * (1.0 + x * (1.0 - s))


def _softplus(x):
    return jnp.maximum(x, 0.0) + jnp.log(1.0 + jnp.exp(-jnp.abs(x)))


def _iota(shape, dim):
    return lax.broadcasted_iota(jnp.int32, shape, dim)


def _lane_col(x, k):
    return jnp.sum(jnp.where(_iota(x.shape, 1) == k, x, 0.0), axis=1, keepdims=True)


def _params(sem=None):
    return pltpu.CompilerParams(dimension_semantics=sem, vmem_limit_bytes=VMEM_LIMIT)


def _t5_bucket(dist):
    max_exact = N_BUCKETS // 2
    d = np.maximum(dist, 1).astype(np.float64)
    large = max_exact + (np.log(d / max_exact) / math.log(MAX_DISTANCE / max_exact)
                         * (N_BUCKETS - max_exact)).astype(np.int32)
    large = np.minimum(large, N_BUCKETS - 1)
    return np.where(dist < max_exact, dist, large).astype(np.int32)


def _bucket_tables():
    qi = np.arange(BLK)[:, None]
    kj = np.arange(2 * BLK)[None, :]
    step = qi - kj + BLK
    band = (step >= 0) & (step <= BLK)
    out = []
    for _, r in PATTERNS:
        b = _t5_bucket(np.clip(step, 0, None) * r)
        out.append(np.where(band, b, -1))
    return np.stack(out).astype(np.int32)


def _group_mats():
    g = np.zeros((D_ATT, 128), np.float32)
    for h in range(ATT_HEADS):
        g[h * HD:(h + 1) * HD, h] = 1.0
    return g, np.ascontiguousarray(g.T)


CHIP_FLIPS = ((1, 0), (0, 1), (1, 1))
ANY_SPEC = pl.BlockSpec(memory_space=pl.ANY)
MESH_ID = pl.DeviceIdType.MESH


def _other_chips():
    x, y = lax.axis_index("x"), lax.axis_index("y")
    return [((1 - x if fx else x), (1 - y if fy else y)) for fx, fy in CHIP_FLIPS]


def _gather_plan(ins, outs, send, recv, loc):
    n = len(ins)
    x, y, c = (lax.axis_index(a) for a in AXES)
    sib = (x, y, 1 - c)
    chips = _other_chips()
    lin = lambda px, py, pc: 4 * px + 2 * py + pc

    def copy(a, k, block, to, src=None):
        slot = outs[a].at[lin(*block)]
        return pltpu.make_async_remote_copy(src_ref=slot if src is None else src, dst_ref=slot,
                                            send_sem=send.at[a, k], recv_sem=recv.at[a, k],
                                            device_id=to, device_id_type=MESH_ID)

    mine = [pltpu.make_async_copy(ins[a], outs[a].at[lin(x, y, c)], loc.at[a]) for a in range(n)]
    firsts = []
    for a in range(n):
        firsts.append(copy(a, 0, (x, y, c), sib, src=ins[a]))
        firsts += [copy(a, 1 + j, (x, y, c), (*chip, c), src=ins[a]) for j, chip in enumerate(chips)]

    def begin():
        for cp in mine + firsts:
            cp.start()

    def finish():
        passed = []
        for j, chip in enumerate(chips):
            for a in range(n):
                copy(a, 1 + j, (*chip, c), (x, y, c)).wait_recv()
                fw = copy(a, 4 + j, (*chip, c), sib)
                fw.start()
                passed.append(fw)
        for a in range(n):
            copy(a, 0, sib, (x, y, c)).wait_recv()
            for j, chip in enumerate(chips):
                copy(a, 4 + j, (*chip, 1 - c), (x, y, c)).wait_recv()
        for cp in firsts + passed:
            cp.wait_send()
        for cp in mine:
            cp.wait()

    return begin, finish


GATHER_SEMS = lambda n: [pltpu.SemaphoreType.DMA((n, 7)), pltpu.SemaphoreType.DMA((n, 7)), pltpu.SemaphoreType.DMA((n,))]


def _gather_weights(arrs):
    n = len(arrs)

    def body(*refs):
        begin, finish = _gather_plan(refs[:n], refs[n:2 * n], *refs[2 * n:])
        begin()
        finish()

    return pl.pallas_call(
        body, name="gather_weights", out_shape=[jax.ShapeDtypeStruct((N_DEV,) + a.shape, a.dtype) for a in arrs],
        in_specs=[ANY_SPEC] * n, out_specs=[ANY_SPEC] * n, scratch_shapes=GATHER_SEMS(n),
    )(*arrs)


def _swap_siblings(arrs):
    n = len(arrs)

    def body(*refs):
        ins, outs = refs[:n], refs[n:2 * n]
        send, recv = refs[2 * n:]
        x, y, c = (lax.axis_index(a) for a in AXES)
        cps = [pltpu.make_async_remote_copy(src_ref=ins[a].at[1 - c], dst_ref=outs[a], send_sem=send.at[a],
                                            recv_sem=recv.at[a], device_id=(x, y, 1 - c), device_id_type=MESH_ID)
               for a in range(n)]
        for cp in cps:
            cp.start()
        for cp in cps:
            cp.wait()

    return pl.pallas_call(
        body, name="swap_siblings", out_shape=[jax.ShapeDtypeStruct(a.shape[1:], a.dtype) for a in arrs],
        in_specs=[ANY_SPEC] * n, out_specs=[ANY_SPEC] * n,
        scratch_shapes=[pltpu.SemaphoreType.DMA((n,)), pltpu.SemaphoreType.DMA((n,))],
    )(*arrs)


def _chip_sum(mine2, theirs, core, wire, name):
    _, nchip, r, cdim = mine2.shape
    tr = r if r <= 256 else 256

    def body(core_ref, a_ref, b_ref, o_ref):
        del core_ref
        o_ref[...] = (a_ref[...].astype(F32) + b_ref[...].astype(F32)).astype(wire)

    grid_spec = pltpu.PrefetchScalarGridSpec(
        num_scalar_prefetch=1, grid=(nchip, r // tr),
        in_specs=[pl.BlockSpec((None, None, tr, cdim), lambda j, i, cr: (cr[0], j, i, 0)),
                  pl.BlockSpec((None, tr, cdim), lambda j, i, cr: (j, i, 0))],
        out_specs=pl.BlockSpec((None, tr, cdim), lambda j, i, cr: (j, i, 0)))
    return pl.pallas_call(
        body, name=name, grid_spec=grid_spec, out_shape=jax.ShapeDtypeStruct((nchip, r, cdim), wire),
        compiler_params=_params(("arbitrary", "arbitrary")),
    )(core, mine2, theirs)


def _chip_swap_copies(ins, outs, send, recv, loc):
    x, y, c = (lax.axis_index(a) for a in AXES)
    me = 2 * x + y
    starts, arrivals, drains = [], [], []
    for a in range(len(ins)):
        lc = pltpu.make_async_copy(ins[a].at[me], outs[a].at[me], loc.at[a])
        starts.append(lc.start)
        drains.append(lc.wait)
        for j, (px, py) in enumerate(_other_chips()):
            them = 2 * px + py
            cp = pltpu.make_async_remote_copy(src_ref=ins[a].at[them], dst_ref=outs[a].at[me], send_sem=send.at[a, j],
                                              recv_sem=recv.at[a, j], device_id=(px, py, c), device_id_type=MESH_ID)
            landing = pltpu.make_async_remote_copy(src_ref=ins[a].at[them], dst_ref=outs[a].at[them],
                                                   send_sem=send.at[a, j], recv_sem=recv.at[a, j],
                                                   device_id=(px, py, c), device_id_type=MESH_ID)
            starts.append(cp.start)
            arrivals.append(landing.wait_recv)
            drains.append(cp.wait_send)
    return starts, arrivals + drains


def _share_small(pack):
    def body(in_ref, out_ref, send, recv, loc):
        x, y, c = (lax.axis_index(a) for a in AXES)
        me = 4 * x + 2 * y + c
        lc = pltpu.make_async_copy(in_ref, out_ref.at[me], loc.at[0])
        lc.start()
        sends, arrivals = [], []
        for k in range(1, N_DEV):
            px = 1 - x if k & 4 else x
            py = 1 - y if k & 2 else y
            pc = 1 - c if k & 1 else c
            cp = pltpu.make_async_remote_copy(src_ref=in_ref, dst_ref=out_ref.at[me], send_sem=send.at[k - 1],
                                              recv_sem=recv.at[k - 1], device_id=(px, py, pc), device_id_type=MESH_ID)
            cp.start()
            sends.append(cp)
            arrivals.append(pltpu.make_async_remote_copy(src_ref=in_ref, dst_ref=out_ref.at[4 * px + 2 * py + pc],
                                                         send_sem=send.at[k - 1], recv_sem=recv.at[k - 1],
                                                         device_id=(px, py, pc), device_id_type=MESH_ID))
        for cp in arrivals:
            cp.wait_recv()
        for cp in sends:
            cp.wait_send()
        lc.wait()

    return pl.pallas_call(
        body, name="share_small", out_shape=jax.ShapeDtypeStruct((N_DEV,) + pack.shape, pack.dtype),
        in_specs=[ANY_SPEC], out_specs=ANY_SPEC,
        scratch_shapes=[pltpu.SemaphoreType.DMA((N_DEV - 1,)), pltpu.SemaphoreType.DMA((N_DEV - 1,)),
                        pltpu.SemaphoreType.DMA((1,))],
    )(pack)


W_PARTS = ((0, 0, 2048), (2048, 4096, 8), (2056, 2048, 2048))
SHARD_COLS = 513


def _pieces(lo, hi, parts):
    out = []
    for ref_start, tgt_start, width in parts:
        a, b = max(lo, ref_start), min(hi, ref_start + width)
        if a < b:
            out.append((a - lo, tgt_start + a - ref_start, b - a))
    return out


def _build_w(win8):
    tr = 256

    def body(in_ref, w_ref, wt_ref):
        w_ref[:, 4096:W_COLS] = jnp.zeros((tr, W_COLS - 4096), MXU)
        for p in range(N_DEV):
            for src, dst, width in _pieces(p * SHARD_COLS, (p + 1) * SHARD_COLS, W_PARTS):
                w_ref[:, dst:dst + width] = in_ref[p, :, src:src + width]
        for k in range(W_COLS // 128):
            wt_ref[k * 128:(k + 1) * 128, :] = w_ref[:, k * 128:(k + 1) * 128].astype(F32).T.astype(MXU)

    return pl.pallas_call(
        body, name="build_w", grid=(D_MODEL // tr,),
        in_specs=[pl.BlockSpec((N_DEV, tr, SHARD_COLS), lambda i: (0, i, 0))],
        out_specs=[pl.BlockSpec((tr, W_COLS), lambda i: (i, 0)), pl.BlockSpec((W_COLS, tr), lambda i: (0, i))],
        out_shape=[jax.ShapeDtypeStruct((D_MODEL, W_COLS), MXU), jax.ShapeDtypeStruct((W_COLS, D_MODEL), MXU)],
        compiler_params=_params(("arbitrary",)),
    )(win8)


def _build_slabs(secs):
    tr = 256
    parts = ((0, 0, 1536), (1536, 1, 512), (2048, 6, 8), (2056, 2, 512), (2568, 3, 512), (3080, 4, 512),
             (3592, 5, 512))

    def body(*refs):
        o_ref = refs[len(secs)]
        for p in range(N_DEV):
            lo, hi = p * SHARD_COLS, (p + 1) * SHARD_COLS
            for ref_start, idx, width in parts:
                a, b = max(lo, ref_start), min(hi, ref_start + width)
                if a < b:
                    o_ref[p % 2, p // 2, :, a - lo:b - lo] = refs[idx][:, a - ref_start:b - ref_start]

    return pl.pallas_call(
        body, name="build_slabs", grid=(D_MODEL // tr,),
        in_specs=[pl.BlockSpec((tr, s.shape[1]), lambda i: (i, 0)) for s in secs],
        out_specs=pl.BlockSpec((2, 4, tr, SHARD_COLS), lambda i: (0, 0, i, 0)),
        out_shape=jax.ShapeDtypeStruct((2, 4, D_MODEL, SHARD_COLS), secs[0].dtype),
        compiler_params=_params(("arbitrary",)),
    )(*secs)


def _inproj(x, nw, w, cw, wout_shard):
    t = x.shape[0]
    tm = 256
    nsteps = t // tm

    def body(x_ref, nw_ref, w_ref, cw_ref, wo_ref, ht_ref, pdn_ref, qkv_ref, z_ref, patt_ref, gate_ref, ba_ref,
             wo8_ref, halo_ref, send, recv, loc):
        begin, finish = _gather_plan([wo_ref], [wo8_ref], send, recv, loc)

        @pl.when(pl.program_id(0) == 0)
        def _():
            begin()
            halo_ref[...] = jnp.zeros_like(halo_ref)

        @pl.when(pl.program_id(0) == nsteps - 1)
        def _():
            finish()

        xv = x_ref[...]
        rstd = lax.rsqrt(jnp.mean(xv * xv, axis=-1, keepdims=True) + EPS)
        hf = xv * rstd * nw_ref[...]
        h = hf.astype(MXU)
        ht_ref[...] = hf.T.astype(MXU)
        for ref, lo, hi in ((z_ref, 1536, 2048), (patt_ref, 2048, 3584), (gate_ref, 3584, 4096), (ba_ref, 4096, 4224)):
            ref[...] = jnp.dot(h, w_ref[:, lo:hi], preferred_element_type=F32)
        pdn = jnp.dot(h, w_ref[:, 0:3 * D_DN], preferred_element_type=F32)
        pdn_ref[...] = pdn
        _dn_prep_tile(pdn, halo_ref, cw_ref, qkv_ref)

    row = lambda n: pl.BlockSpec((tm, n), lambda i: (i, 0))
    full = lambda a: pl.BlockSpec(a.shape, lambda i: (0,) * a.ndim)
    return pl.pallas_call(
        body, name="inproj", grid=(nsteps,),
        in_specs=[row(D_MODEL), full(nw), full(w), full(cw), ANY_SPEC],
        out_specs=[pl.BlockSpec((D_MODEL, tm), lambda i: (0, i)), row(1536), row(1536), row(512), row(1536), row(512),
                   row(128), ANY_SPEC],
        out_shape=[jax.ShapeDtypeStruct((D_MODEL, t), MXU)] +
                  [jax.ShapeDtypeStruct((t, n), F32) for n in (1536, 1536, 512, 1536, 512, 128)] +
                  [jax.ShapeDtypeStruct((N_DEV,) + wout_shard.shape, wout_shard.dtype)],
        scratch_shapes=[pltpu.VMEM((8, 3 * D_DN), F32)] + GATHER_SEMS(1),
        compiler_params=_params(("arbitrary",)),
    )(x, nw, w, cw, wout_shard)


CONV_ROWS = 512


def _conv_taps(u_ref, c, w_ref):
    r0 = c * CONV_ROWS
    if c == 0:
        ext = jnp.concatenate([jnp.zeros((8, 128), F32), u_ref[0:CONV_ROWS, :]], axis=0)
    else:
        ext = u_ref[r0 - 8:r0 + CONV_ROWS, :]
    taps = [ext[8:, :]] + [pltpu.roll(ext, k, 0)[8:, :] for k in (1, 2, 3)]
    y = taps[0] * w_ref[3:4, :]
    for k in (1, 2, 3):
        y = y + taps[k] * w_ref[3 - k:4 - k, :]
    return taps, y


def _dn_prep_tile(pdn, halo_ref, cw_ref, out_ref):
    rows = pdn.shape[0]
    ext = jnp.concatenate([halo_ref[...], pdn], axis=0)
    halo_ref[...] = pdn[rows - 8:, :]
    for j in range(3 * D_DN // 128):
        cols = slice(j * 128, (j + 1) * 128)
        e = ext[:, cols]
        y = e[8:, :] * cw_ref[3:4, cols]
        for k in (1, 2, 3):
            y = y + pltpu.roll(e, k, 0)[8:, :] * cw_ref[3 - k:4 - k, cols]
        a = _silu(y)
        if j < 2 * DN_HEADS:
            a = a * lax.rsqrt(jnp.sum(a * a, axis=1, keepdims=True) + EPS)
        if j < DN_HEADS:
            a = a * DK ** -0.5
        out_ref[:, cols] = a


def _chunk_common(qkv, ba, arow, dtb):
    c = CHUNK
    ri, ci = _iota((c, c), 0), _iota((c, c), 1)
    lane = _iota((c, 128), 1)
    g_all = jnp.where((lane >= DN_HEADS) & (lane < 2 * DN_HEADS), arow * _softplus(ba + dtb), 0.0)
    gc_all = _hdot((ri >= ci).astype(F32), g_all, "a")
    gc_t = gc_all.T
    beta_all = _sigmoid(ba)
    out = []
    for h in range(DN_HEADS):
        gc = _lane_col(gc_all, DN_HEADS + h)
        gcr = gc_t[DN_HEADS + h:DN_HEADS + h + 1, :]
        gl = gc[c - 1:c, :]
        out.append(dict(
            q=qkv[:, h * DK:(h + 1) * DK], k=qkv[:, D_DN + h * DK:D_DN + (h + 1) * DK],
            v=qkv[:, 2 * D_DN + h * DK:2 * D_DN + (h + 1) * DK],
            beta=_lane_col(beta_all, h), g=_lane_col(g_all, DN_HEADS + h),
            a_raw=_lane_col(ba, DN_HEADS + h), a_h=_lane_col(arow, DN_HEADS + h), dt_h=_lane_col(dtb, DN_HEADS + h),
            decay=jnp.exp(jnp.where(ri >= ci, gc - gcr, NEG)), eg=jnp.exp(gc), egl=jnp.exp(gl), etail=jnp.exp(gl - gc)))
    return out, ri, ci


SCAN_CHUNKS = 8


def _dn_scan_fwd(qkv, ba, z, arow, dtb, dnw):
    t = qkv.shape[0]
    n = t // CHUNK
    c = CHUNK
    cps = SCAN_CHUNKS
    hs = range(DN_HEADS)
    chains = [(j, h) for j in range(cps) for h in hs]

    def body(qkv_ref, ba_ref, z_ref, arow_ref, dtb_ref, dnw_ref, o_ref, y_ref, sh_ref, th_ref, s_ref):
        @pl.when(pl.program_id(0) == 0)
        def _():
            s_ref[...] = jnp.zeros_like(s_ref)

        ms = {}
        for j in range(cps):
            rows = slice(j * c, (j + 1) * c)
            mj, ri, ci = _chunk_common(qkv_ref[rows, :], ba_ref[rows, :], arow_ref[...], dtb_ref[...])
            for h in hs:
                ms[j, h] = mj[h]
        kb = {x: ms[x]["k"] * ms[x]["beta"] for x in chains}
        amat = {x: jnp.where(ri > ci, _dot_nt(kb[x], ms[x]["k"]) * ms[x]["decay"], 0.0) for x in chains}
        attn = {x: jnp.where(ri >= ci, _dot_nt(ms[x]["q"], ms[x]["k"]) * ms[x]["decay"], 0.0) for x in chains}
        tinv = _wy_inverses(amat, (ri == ci).astype(F32))
        uw = {x: _hdot(tinv[x], jnp.concatenate([ms[x]["v"] * ms[x]["beta"], kb[x] * ms[x]["eg"]], axis=1))
              for x in chains}
        u = {x: uw[x][:, :DK] for x in chains}
        w = {x: uw[x][:, DK:] for x in chains}
        q_dec = {x: ms[x]["q"] * ms[x]["eg"] for x in chains}
        k_tail = {x: ms[x]["k"] * ms[x]["etail"] for x in chains}
        s = [s_ref[h] for h in hs]
        for j in range(cps):
            rows = slice(j * c, (j + 1) * c)
            v_new = [u[j, h] - _dot(w[j, h], s[h]) for h in hs]
            o = [_dot(q_dec[j, h], s[h]) + _dot(attn[j, h], v_new[h]) for h in hs]
            for h in hs:
                sh_ref[j, h] = s[h]
                th_ref[j, h] = tinv[j, h]
            s = [s[h] * ms[j, h]["egl"] + _dot_tn(k_tail[j, h], v_new[h]) for h in hs]
            for h in hs:
                cols = slice(h * DK, (h + 1) * DK)
                o_ref[rows, cols] = o[h]
                rs = lax.rsqrt(jnp.mean(o[h] * o[h], axis=1, keepdims=True) + EPS)
                y_ref[rows, cols] = o[h] * rs * dnw_ref[...] * _silu(z_ref[rows, cols])
        for h in hs:
            s_ref[h] = s[h]

    row = lambda w_: pl.BlockSpec((cps * c, w_), lambda i: (i, 0))
    one = pl.BlockSpec((1, 128), lambda i: (0, 0))
    return pl.pallas_call(
        body, name="dn_scan_fwd", grid=(n // cps,),
        in_specs=[row(1536), row(128), row(512), one, one, one],
        out_specs=[row(512), row(512), pl.BlockSpec((cps, DN_HEADS, DK, DK), lambda i: (i, 0, 0, 0)),
                   pl.BlockSpec((cps, DN_HEADS, c, c), lambda i: (i, 0, 0, 0))],
        out_shape=[jax.ShapeDtypeStruct((t, 512), F32), jax.ShapeDtypeStruct((t, 512), F32),
                   jax.ShapeDtypeStruct((n, DN_HEADS, DK, DK), F32), jax.ShapeDtypeStruct((n, DN_HEADS, c, c), F32)],
        scratch_shapes=[pltpu.VMEM((DN_HEADS, DK, DK), F32)],
        compiler_params=_params(("arbitrary",)),
    )(qkv, ba, z, arow, dtb, dnw)


ATT_ROWS = 512


def _pair_rstd(xv, g2_ref):
    return lax.rsqrt(_hdot(xv * xv, g2_ref[...], "b") * (1.0 / HD) + EPS)


def _pair_norm(t, raw_refs, w_refs, out_refs, g2_ref):
    for c in range(t // ATT_ROWS):
        sl = slice(c * ATT_ROWS, (c + 1) * ATT_ROWS)
        for raw, w_ref, out in zip(raw_refs, w_refs, out_refs):
            xv = raw[sl, :]
            out[sl, :] = xv * _pair_rstd(xv, g2_ref) * w_ref[...]


def _bias_tables(rb, bk):
    def body(rb_ref, bk_ref, bias_ref):
        pair = pl.program_id(0)
        for p in range(len(PATTERNS)):
            bk_p = bk_ref[p]
            for hh in range(2):
                head = 2 * pair + hh
                bm = jnp.full((BLK, 2 * BLK), NEG, F32)
                for b in range(N_BUCKETS):
                    bm = jnp.where(bk_p == b, rb_ref[head, b], bm)
                bias_ref[p, hh * BLK:(hh + 1) * BLK, :] = bm

    return pl.pallas_call(
        body, name="bias_tables", grid=(ATT_HEADS // 2,),
        in_specs=[pl.BlockSpec(memory_space=pltpu.SMEM), pl.BlockSpec(bk.shape, lambda i: (0, 0, 0))],
        out_specs=pl.BlockSpec((len(PATTERNS), None, 2 * BLK, 2 * BLK), lambda i: (0, i, 0, 0)),
        out_shape=jax.ShapeDtypeStruct((len(PATTERNS), ATT_HEADS // 2, 2 * BLK, 2 * BLK), F32),
        compiler_params=_params(("arbitrary",)),
    )(rb, bk)


BIAS_SPEC = pl.BlockSpec((len(PATTERNS), None, 2 * BLK, 2 * BLK), lambda i: (0, i, 0, 0))
PAIR_ROW_SPEC = pl.BlockSpec((1, 128), lambda i: (0, i))


def _stack_heads(xb, h0):
    return jnp.concatenate([jnp.where(h0, xb, 0.0), jnp.where(h0, 0.0, xb)], axis=0).astype(MXU)


def _block_rows(t, r, n):
    per_class = (t // r) // BLK
    res = n // per_class
    j = n % per_class
    start = res + BLK * r * j
    pstart = res + BLK * r * jnp.maximum(j - 1, 0)
    if r == 1:
        return pl.ds(pl.multiple_of(start, BLK), BLK), pl.ds(pl.multiple_of(pstart, BLK), BLK), j
    return pl.ds(start, BLK, stride=r), pl.ds(pstart, BLK, stride=r), j


def _att_fwd(qkv, gate, bias, wq, wk, g2):
    t = qkv.shape[0]
    rows = ATT_ROWS

    def body(bias_ref, qraw_ref, kraw_ref, v_ref, g_ref, wq_ref, wk_ref, g2_ref, o_ref, y_ref, lse_ref,
             o0_ref, o1_ref, o2_ref, l0_ref, l1_ref, l2_ref, q_ref, k_ref):
        h0 = _iota((BLK, 128), 1) < HD
        prev_cols = _iota((2 * BLK, 2 * BLK), 1) < BLK
        op_refs, lp_refs = (o0_ref, o1_ref, o2_ref), (l0_ref, l1_ref, l2_ref)
        _pair_norm(t, (qraw_ref, kraw_ref), (wq_ref, wk_ref), (q_ref, k_ref), g2_ref)

        for p, (_, r) in enumerate(PATTERNS):
            def blk(n, carry, p=p, r=r):
                cur, prev, j = _block_rows(t, r, n)
                q2 = _stack_heads(q_ref[cur, :], h0)
                k2 = jnp.concatenate([k_ref[prev, :], k_ref[cur, :]], axis=0).astype(MXU)
                v2 = jnp.concatenate([v_ref[prev, :], v_ref[cur, :]], axis=0).astype(MXU)
                s = _dot_nt(q2, k2) + bias_ref[p] + jnp.where(prev_cols & (j == 0), NEG, 0.0)
                m = jnp.max(s, axis=1, keepdims=True)
                e = jnp.exp(s - m)
                l = jnp.sum(e, axis=1, keepdims=True)
                pv = _dot(e, v2) / l
                lse = m + jnp.log(l)
                op_refs[p][cur, :] = jnp.where(h0, pv[:BLK], pv[BLK:])
                lp_refs[p][cur, :] = jnp.where(h0, lse[:BLK], lse[BLK:])
                return carry

            lax.fori_loop(0, t // BLK, blk, 0, unroll=8)

        for c in range(t // rows):
            sl = slice(c * rows, (c + 1) * rows)
            ls = [ref[sl, :] for ref in lp_refs]
            mx = jnp.maximum(jnp.maximum(ls[0], ls[1]), ls[2])
            ws = [jnp.exp(v_ - mx) for v_ in ls]
            den = ws[0] + ws[1] + ws[2]
            o = (ws[0] * o0_ref[sl, :] + ws[1] * o1_ref[sl, :] + ws[2] * o2_ref[sl, :]) / den
            o_ref[sl, :] = o
            y_ref[sl, :] = o * _silu(g_ref[sl, :])
            lse_ref[sl, :] = mx + jnp.log(den)

    col = lambda off: pl.BlockSpec((t, 128), lambda i, off=off: (0, off + i))
    return pl.pallas_call(
        body, name="att_fwd", grid=(ATT_HEADS // 2,),
        in_specs=[BIAS_SPEC, col(0), col(4), col(8), col(0), PAIR_ROW_SPEC, PAIR_ROW_SPEC,
                  pl.BlockSpec((128, 128), lambda i: (0, 0))],
        out_specs=[col(0), col(0), col(0)],
        out_shape=[jax.ShapeDtypeStruct((t, 512), F32)] * 3,
        scratch_shapes=[pltpu.VMEM((t, 128), F32)] * 8,
        compiler_params=_params(("arbitrary",)),
    )(bias, qkv, qkv, qkv, gate, wq, wk, g2)


def _outproj_loss(x, ydn, yatt, wout, target):
    t = x.shape[0]
    tm = 512

    def body(x_ref, a_ref, b_ref, w_ref, t_ref, dy_ref, mix_ref, loss_ref):
        @pl.when(pl.program_id(0) == 0)
        def _():
            loss_ref[...] = jnp.zeros_like(loss_ref)

        mixf = jnp.concatenate([a_ref[...], b_ref[...]], axis=1)
        mix_ref[...] = mixf.T.astype(MXU)
        err = x_ref[...] + jnp.dot(mixf.astype(MXU), w_ref[...], preferred_element_type=F32) - t_ref[...]
        dy_ref[...] = err * (1.0 / D_MODEL)
        loss_ref[...] += jnp.sum(err * err) * (0.5 / D_MODEL)

    row = lambda n: pl.BlockSpec((tm, n), lambda i: (i, 0))
    return pl.pallas_call(
        body, name="outproj_loss", grid=(t // tm,),
        in_specs=[row(D_MODEL), row(512), row(512), pl.BlockSpec(wout.shape, lambda i: (0, 0)), row(D_MODEL)],
        out_specs=[row(D_MODEL), pl.BlockSpec((D_MODEL, tm), lambda i: (0, i)), pl.BlockSpec((8, 128), lambda i: (0, 0))],
        out_shape=[jax.ShapeDtypeStruct((t, D_MODEL), F32), jax.ShapeDtypeStruct((D_MODEL, t), MXU),
                   jax.ShapeDtypeStruct((8, 128), F32)],
        compiler_params=_params(("arbitrary",)),
    )(x, ydn, yatt, wout, target)


def _outproj_bwd(dy, wout_t, oraw, z, dnw, oatt, gate, g, gt):
    t = dy.shape[0]
    tm = 256

    def body(dy_ref, w_ref, o_ref, z_ref, dnw_ref, oa_ref, g_ref, grp_ref, grpt_ref,
             do_ref, dz_ref, doa_ref, dg_ref, dd_ref, ddnw_ref):
        @pl.when(pl.program_id(0) == 0)
        def _():
            ddnw_ref[...] = jnp.zeros_like(ddnw_ref)

        dmix = jnp.dot(dy_ref[...].astype(MXU), w_ref[...], preferred_element_type=F32)
        dnw_v = dnw_ref[...]
        acc = jnp.zeros((1, DK), F32)
        for h in range(DN_HEADS):
            sl = slice(h * DK, (h + 1) * DK)
            o, zz, dm = o_ref[:, sl], z_ref[:, sl], dmix[:, sl]
            rs = lax.rsqrt(jnp.mean(o * o, axis=1, keepdims=True) + EPS)
            oh = o * rs
            silu_z, dsilu_z = _silu_and_grad(zz)
            dz_ref[:, sl] = dm * oh * dnw_v * dsilu_z
            d_on = dm * silu_z
            gg = d_on * dnw_v
            do_ref[:, sl] = rs * (gg - oh * jnp.mean(gg * oh, axis=1, keepdims=True))
            acc = acc + jnp.sum(d_on * oh, axis=0, keepdims=True)
        ddnw_ref[...] += jnp.broadcast_to(acc, (8, DK))
        da, gate_v, oa = dmix[:, 512:], g_ref[...], oa_ref[...]
        silu_g, dsilu_g = _silu_and_grad(gate_v)
        doa = da * silu_g
        doa_ref[...] = doa
        dg_ref[...] = da * oa * dsilu_g
        dd_ref[...] = _hdot(_hdot(doa * oa, grp_ref[...], "b"), grpt_ref[...], "b")

    row = lambda n: pl.BlockSpec((tm, n), lambda i: (i, 0))
    full = lambda a: pl.BlockSpec(a.shape, lambda i: (0,) * a.ndim)
    return pl.pallas_call(
        body, name="outproj_bwd", grid=(t // tm,),
        in_specs=[row(D_MODEL), full(wout_t), row(512), row(512), full(dnw), row(512), row(512), full(g), full(gt)],
        out_specs=[row(512)] * 5 + [pl.BlockSpec((8, DK), lambda i: (0, 0))],
        out_shape=[jax.ShapeDtypeStruct((t, 512), F32)] * 5 + [jax.ShapeDtypeStruct((8, DK), F32)],
        compiler_params=_params(("arbitrary",)),
    )(dy, wout_t, oraw, z, dnw, oatt, gate, g, gt)


def _grad_matmul(at, b, name):
    m, t = at.shape
    n = b.shape[1]
    tk = 512
    tn = n if n <= 512 else 512
    nk = t // tk

    def body(a_ref, b_ref, o_ref, acc_ref):
        k = pl.program_id(1)

        @pl.when(k == 0)
        def _():
            acc_ref[...] = jnp.zeros_like(acc_ref)

        acc_ref[...] += jnp.dot(a_ref[...], b_ref[...].astype(MXU), preferred_element_type=F32)

        @pl.when(k == nk - 1)
        def _():
            o_ref[...] = acc_ref[...].astype(GRAD_WIRE)

    return pl.pallas_call(
        body, name=name, grid=(n // tn, nk),
        in_specs=[pl.BlockSpec((m, tk), lambda j, k: (0, k)), pl.BlockSpec((tk, tn), lambda j, k: (k, j))],
        out_specs=pl.BlockSpec((m, tn), lambda j, k: (0, j)),
        out_shape=jax.ShapeDtypeStruct((m, n), GRAD_WIRE),
        scratch_shapes=[pltpu.VMEM((m, tn), F32)],
        compiler_params=_params(("arbitrary", "arbitrary")),
    )(at, b)


def _att_bwd(qkv, do, lse, dd, bias, bk, wq, wk, g2):
    t = qkv.shape[0]
    rows = ATT_ROWS

    def body(bias_ref, bk_ref, qraw_ref, kraw_ref, v_ref, do_ref, lse_ref, dd_ref, wq_ref, wk_ref, g2_ref,
             dq_ref, dk_ref, dv_ref, db_ref, dwq_ref, dwk_ref, ds_ref, q_ref, k_ref):
        pair = pl.program_id(0)

        @pl.when(pair == 0)
        def _():
            db_ref[...] = jnp.zeros_like(db_ref)

        _pair_norm(t, (qraw_ref, kraw_ref), (wq_ref, wk_ref), (q_ref, k_ref), g2_ref)
        ds_ref[...] = jnp.zeros_like(ds_ref)
        for c in range(t // rows):
            sl = slice(c * rows, (c + 1) * rows)
            for ref in (dq_ref, dk_ref, dv_ref):
                ref[sl, :] = jnp.zeros((rows, 128), F32)
        h0 = _iota((BLK, 128), 1) < HD
        prev_cols = _iota((2 * BLK, 2 * BLK), 1) < BLK

        def rows_of(xb):
            return jnp.concatenate([xb[:, 0:1], xb[:, HD:HD + 1]], axis=0)

        for p, (_, r) in enumerate(PATTERNS):
            def blk(n, carry, p=p, r=r):
                cur, prev, j = _block_rows(t, r, n)
                q2, do2 = _stack_heads(q_ref[cur, :], h0), _stack_heads(do_ref[cur, :], h0)
                k2 = jnp.concatenate([k_ref[prev, :], k_ref[cur, :]], axis=0).astype(MXU)
                v2 = jnp.concatenate([v_ref[prev, :], v_ref[cur, :]], axis=0).astype(MXU)
                s = _dot_nt(q2, k2) + bias_ref[p] + jnp.where(prev_cols & (j == 0), NEG, 0.0)
                prob = jnp.exp(s - rows_of(lse_ref[cur, :]))
                ds = prob * (_dot_nt(do2, v2) - rows_of(dd_ref[cur, :]))
                ds_ref[p] += ds
                dq2 = _dot(ds, k2)
                dk2 = _dot_tn(ds, q2)
                dv2 = _dot_tn(prob, do2)
                dq_ref[cur, :] += jnp.where(h0, dq2[:BLK], dq2[BLK:])
                dk_ref[prev, :] += dk2[:BLK]
                dv_ref[prev, :] += dv2[:BLK]
                dk_ref[cur, :] += dk2[BLK:]
                dv_ref[cur, :] += dv2[BLK:]
                return carry

            lax.fori_loop(0, t // BLK, blk, 0, unroll=4)

        ri, ci = _iota((8, 128), 0), _iota((8, 128), 1)
        upd = jnp.zeros((8, 128), F32)
        for p in range(len(PATTERNS)):
            bk = bk_ref[p]
            for hh in range(2):
                dsum = ds_ref[p, hh * BLK:(hh + 1) * BLK, :]
                for b in range(N_BUCKETS):
                    val = jnp.sum(jnp.where(bk == b, dsum, 0.0))
                    upd = upd + jnp.where((ri == 2 * pair + hh) & (ci == b), val, 0.0)
        db_ref[...] += upd

        for raw, d_ref, w_ref, dw_ref in ((qraw_ref, dq_ref, wq_ref, dwq_ref), (kraw_ref, dk_ref, wk_ref, dwk_ref)):
            acc = jnp.zeros((1, 128), F32)
            for c in range(t // rows):
                sl = slice(c * rows, (c + 1) * rows)
                xv, dyv = raw[sl, :], d_ref[sl, :]
                rs = _pair_rstd(xv, g2_ref)
                xh = xv * rs
                gg = dyv * w_ref[...]
                mean = _hdot(gg * xh, g2_ref[...], "b") * (1.0 / HD)
                d_ref[sl, :] = rs * (gg - xh * mean)
                acc = acc + jnp.sum(dyv * xh, axis=0, keepdims=True)
            dw_ref[...] = jnp.broadcast_to(acc, (8, 128))

    col = lambda off: pl.BlockSpec((t, 128), lambda i, off=off: (0, off + i))
    acc8 = pl.BlockSpec((8, 128), lambda i: (0, i))
    return pl.pallas_call(
        body, name="att_bwd", grid=(ATT_HEADS // 2,),
        in_specs=[BIAS_SPEC, pl.BlockSpec(bk.shape, lambda i: (0, 0, 0)),
                  col(0), col(4), col(8), col(0), col(0), col(0), PAIR_ROW_SPEC, PAIR_ROW_SPEC,
                  pl.BlockSpec((128, 128), lambda i: (0, 0))],
        out_specs=[col(0), col(0), col(0), pl.BlockSpec((8, 128), lambda i: (0, 0)), acc8, acc8],
        out_shape=[jax.ShapeDtypeStruct((t, 512), F32)] * 3 + [jax.ShapeDtypeStruct((8, 128), F32)]
                  + [jax.ShapeDtypeStruct((8, 512), F32)] * 2,
        scratch_shapes=[pltpu.VMEM((len(PATTERNS), 2 * BLK, 2 * BLK), F32)] + [pltpu.VMEM((t, 128), F32)] * 2,
        compiler_params=_params(("arbitrary",)),
    )(bias, bk, qkv, qkv, qkv, do, lse, dd, wq, wk, g2)


def _dn_scan_bwd(qkv, ba, do, sh, th, arow, dtb):
    t = qkv.shape[0]
    n = t // CHUNK
    c = CHUNK
    cps = SCAN_CHUNKS

    def body(qkv_ref, ba_ref, do_ref, sh_ref, th_ref, arow_ref, dtb_ref, dqkv_ref, dba_ref, ds_ref):
        @pl.when(pl.program_id(0) == 0)
        def _():
            ds_ref[...] = jnp.zeros_like(ds_ref)

        hs = range(DN_HEADS)
        chains = [(j, h) for j in range(cps) for h in hs]
        lane = _iota((c, 128), 1)
        row = _iota((c, 1), 0)
        ms = {}
        for j in range(cps):
            rows_j = slice(j * c, (j + 1) * c)
            mj, ri, ci = _chunk_common(qkv_ref[rows_j, :], ba_ref[rows_j, :], arow_ref[...], dtb_ref[...])
            for h in hs:
                ms[j, h] = mj[h]
        q, k, v = ({x: ms[x][nm] for x in chains} for nm in ("q", "k", "v"))
        beta, decay = ({x: ms[x][nm] for x in chains} for nm in ("beta", "decay"))
        eg, egl, etail = ({x: ms[x][nm] for x in chains} for nm in ("eg", "egl", "etail"))
        s = {x: sh_ref[x[0], x[1]] for x in chains}
        tinv = {x: th_ref[x[0], x[1]] for x in chains}
        d_o = {(j, h): do_ref[j * c:(j + 1) * c, h * DK:(h + 1) * DK] for j, h in chains}
        kb = {x: k[x] * beta[x] for x in chains}
        vb = {x: v[x] * beta[x] for x in chains}
        kbg = {x: kb[x] * eg[x] for x in chains}
        amat = {x: jnp.where(ri > ci, _dot_nt(kb[x], k[x]) * decay[x], 0.0) for x in chains}
        attn = {x: jnp.where(ri >= ci, _dot_nt(q[x], k[x]) * decay[x], 0.0) for x in chains}
        uw = {x: _hdot(tinv[x], jnp.concatenate([vb[x], kbg[x]], axis=1)) for x in chains}
        u = {x: uw[x][:, :DK] for x in chains}
        w = {x: uw[x][:, DK:] for x in chains}
        v_new = {x: u[x] - _dot(w[x], s[x]) for x in chains}
        q_dec = {x: q[x] * eg[x] for x in chains}
        k_tail = {x: k[x] * etail[x] for x in chains}
        d_attn = {x: jnp.where(ri >= ci, _dot_nt(d_o[x], v_new[x]), 0.0) for x in chains}
        d_qdec = {x: _dot_nt(d_o[x], s[x]) for x in chains}
        from_o = {x: _dot_tn(attn[x], d_o[x]) for x in chains}
        to_state = {x: _dot_tn(q_dec[x], d_o[x]) for x in chains}

        d_s, d_vnew = {}, {}
        cur = [ds_ref[h] for h in hs]
        for j in reversed(range(cps)):
            for h in hs:
                d_s[j, h] = cur[h]
                d_vnew[j, h] = from_o[j, h] + _dot(k_tail[j, h], cur[h])
            cur = [to_state[j, h] + cur[h] * egl[j, h] - _dot_tn(w[j, h], d_vnew[j, h]) for h in hs]
        for h in hs:
            ds_ref[h] = cur[h]

        d_ktail = {x: _dot_nt(v_new[x], d_s[x]) for x in chains}
        d_gl = {x: jnp.sum(s[x] * d_s[x]) * egl[x] for x in chains}
        d_w = {x: -_dot_nt(d_vnew[x], s[x]) for x in chains}
        d_both = {x: _hdot_tn(tinv[x], jnp.concatenate([d_vnew[x], d_w[x]], axis=1)) for x in chains}
        d_vb = {x: d_both[x][:, :DK] for x in chains}
        d_kbg = {x: d_both[x][:, DK:] for x in chains}
        d_a = {x: -jnp.where(ri > ci, _hdot_nt(d_both[x], uw[x]), 0.0) for x in chains}
        d_qk = {x: d_attn[x] * decay[x] for x in chains}
        d_kk = {x: d_a[x] * decay[x] for x in chains}
        d_kb = {x: _dot(d_kk[x], k[x]) + d_kbg[x] * eg[x] for x in chains}
        d_q = {x: _dot(d_qk[x], k[x]) + d_qdec[x] * eg[x] for x in chains}
        d_k = {x: _dot_tn(d_qk[x], q[x]) + _dot_tn(d_kk[x], kb[x]) + d_ktail[x] * etail[x] + d_kb[x] * beta[x]
               for x in chains}
        d_beta = {x: jnp.sum(d_kb[x] * k[x] + d_vb[x] * v[x], axis=1, keepdims=True) for x in chains}
        mm = {x: d_a[x] * amat[x] + d_attn[x] * attn[x] for x in chains}
        for j in range(cps):
            rows_j = slice(j * c, (j + 1) * c)
            rows = jnp.zeros((c, c), F32)
            for h in hs:
                rows = rows + jnp.where(ri == h, jnp.sum(mm[j, h], axis=0, keepdims=True), 0.0)
            cols_t = jnp.concatenate([rows, jnp.zeros((c, c), F32)], axis=1).T[:c, :]
            d_gc_all = jnp.zeros((c, 128), F32)
            for h in hs:
                x = (j, h)
                tail_term = jnp.sum(d_ktail[x] * k_tail[x], axis=1, keepdims=True)
                d_gc = (jnp.sum(mm[x], axis=1, keepdims=True) - _lane_col(cols_t, h)
                        + jnp.sum(d_qdec[x] * q_dec[x] + d_kbg[x] * kbg[x], axis=1, keepdims=True) - tail_term)
                d_gc = d_gc + jnp.where(row == c - 1, jnp.sum(tail_term) + d_gl[x], 0.0)
                d_gc_all = d_gc_all + jnp.where(lane == DN_HEADS + h, d_gc, 0.0)
            d_g_all = _hdot((ri <= ci).astype(F32), d_gc_all, "a")
            dba = jnp.zeros((c, 128), F32)
            for h in hs:
                x = (j, h)
                d_g = _lane_col(d_g_all, DN_HEADS + h)
                d_braw = d_beta[x] * beta[x] * (1.0 - beta[x])
                d_araw = d_g * ms[x]["a_h"] * _sigmoid(ms[x]["a_raw"] + ms[x]["dt_h"])
                dba = dba + jnp.where(lane == h, d_braw, 0.0) + jnp.where(lane == DN_HEADS + h, d_araw, 0.0) \
                    + jnp.where(lane == 2 * DN_HEADS + h, d_g * ms[x]["g"], 0.0)
                dqkv_ref[rows_j, h * DK:(h + 1) * DK] = d_q[x]
                dqkv_ref[rows_j, D_DN + h * DK:D_DN + (h + 1) * DK] = d_k[x]
                dqkv_ref[rows_j, 2 * D_DN + h * DK:2 * D_DN + (h + 1) * DK] = d_vb[x] * beta[x]
            dba_ref[rows_j, :] = dba

    nsteps = n // cps
    rev = lambda w_: pl.BlockSpec((cps * c, w_), lambda i: (nsteps - 1 - i, 0))
    one = pl.BlockSpec((1, 128), lambda i: (0, 0))
    return pl.pallas_call(
        body, name="dn_scan_bwd", grid=(nsteps,),
        in_specs=[rev(1536), rev(128), rev(512),
                  pl.BlockSpec((cps, DN_HEADS, DK, DK), lambda i: (nsteps - 1 - i, 0, 0, 0)),
                  pl.BlockSpec((cps, DN_HEADS, c, c), lambda i: (nsteps - 1 - i, 0, 0, 0)), one, one],
        out_specs=[rev(1536), rev(128)],
        out_shape=[jax.ShapeDtypeStruct((t, 1536), F32), jax.ShapeDtypeStruct((t, 128), F32)],
        scratch_shapes=[pltpu.VMEM((DN_HEADS, DK, DK), F32)],
        compiler_params=_params(("arbitrary",)),
    )(qkv, ba, do, sh, th, arow, dtb)


def _dn_prep_bwd(pdn, cw, dact):
    t = pdn.shape[0]
    nchunk = t // CONV_ROWS

    def body(u_ref, w_ref, d_ref, du_ref, dw_ref, dy_ref):
        j = pl.program_id(0)
        dy_ref[t:t + 8, :] = jnp.zeros((8, 128), F32)
        dw = [jnp.zeros((1, 128), F32) for _ in range(4)]
        for c in range(nchunk):
            sl = slice(c * CONV_ROWS, (c + 1) * CONV_ROWS)
            taps, y = _conv_taps(u_ref, c, w_ref)
            a, da_dy = _silu_and_grad(y)
            dout = d_ref[sl, :]
            rs = lax.rsqrt(jnp.sum(a * a, axis=1, keepdims=True) + EPS)
            f = jnp.where(j < 8, rs, 1.0) * jnp.where(j < 4, DK ** -0.5, 1.0)
            corr = jnp.where(j < 8, f * rs * rs * jnp.sum(dout * a, axis=1, keepdims=True), 0.0)
            dy = (f * dout - corr * a) * da_dy
            dy_ref[sl, :] = dy
            for k_ in range(4):
                dw[3 - k_] = dw[3 - k_] + jnp.sum(taps[k_] * dy, axis=0, keepdims=True)
        for i in range(4):
            dw_ref[i:i + 1, :] = dw[i]
        for c in range(nchunk):
            r0 = c * CONV_ROWS
            ext = dy_ref[r0:r0 + CONV_ROWS + 8, :]
            du = ext[:CONV_ROWS, :] * w_ref[3:4, :]
            for k_ in (1, 2, 3):
                du = du + pltpu.roll(ext, CONV_ROWS + 8 - k_, 0)[:CONV_ROWS, :] * w_ref[3 - k_:4 - k_, :]
            du_ref[r0:r0 + CONV_ROWS, :] = du

    return pl.pallas_call(
        body, name="dn_prep_bwd", grid=(12,),
        in_specs=[pl.BlockSpec((t, 128), lambda j: (0, j)), pl.BlockSpec((4, 128), lambda j: (0, j)),
                  pl.BlockSpec((t, 128), lambda j: (0, j))],
        out_specs=[pl.BlockSpec((t, 128), lambda j: (0, j)), pl.BlockSpec((4, 128), lambda j: (0, j))],
        out_shape=[jax.ShapeDtypeStruct((t, 1536), F32), jax.ShapeDtypeStruct((4, 1536), F32)],
        scratch_shapes=[pltpu.VMEM((t + 8, 128), F32)],
        compiler_params=_params(("arbitrary",)),
    )(pdn, cw, dact)


SECTIONS = (("dn", 0, 1536), ("z", 1536, 512), ("q", 2048, 512), ("k", 2560, 512), ("v", 3072, 512),
            ("gate", 3584, 512), ("ba", 4096, 128))


def _inproj_bwd(x, nw, wt, dy, dsecs, partials):
    t = x.shape[0]
    tm = 256
    npart = len(partials)
    nsteps = t // tm

    nsec = len(SECTIONS)

    def body(x_ref, nw_ref, w_ref, dy_ref, *rest):
        sec_refs, rest = rest[:nsec], rest[nsec:]
        part_refs, (gx_ref, dnw_ref, cs_ref) = rest[:npart], rest[npart:npart + 3]
        got_refs, (send, recv, loc) = rest[npart + 3:2 * npart + 3], rest[2 * npart + 3:]
        starts, waits = _chip_swap_copies(part_refs, got_refs, send, recv, loc)

        @pl.when(pl.program_id(0) == 0)
        def _():
            for start in starts:
                start()
            dnw_ref[...] = jnp.zeros_like(dnw_ref)
            cs_ref[...] = jnp.zeros_like(cs_ref)

        @pl.when(pl.program_id(0) == nsteps - 1)
        def _():
            for wait in waits:
                wait()

        dh = jnp.zeros((tm, D_MODEL), F32)
        for ref, (_, lo, width) in zip(sec_refs, SECTIONS):
            dh = dh + jnp.dot(ref[...].astype(MXU), w_ref[lo:lo + width, :], preferred_element_type=F32)
        xv = x_ref[...]
        rstd = lax.rsqrt(jnp.mean(xv * xv, axis=-1, keepdims=True) + EPS)
        xh = xv * rstd
        gg = dh * nw_ref[...]
        gx_ref[...] = rstd * (gg - xh * jnp.mean(gg * xh, axis=-1, keepdims=True)) + dy_ref[...]
        dnw_ref[...] += jnp.broadcast_to(jnp.sum(dh * xh, axis=0, keepdims=True), (8, D_MODEL))
        cs_ref[...] += jnp.broadcast_to(jnp.sum(sec_refs[nsec - 1][...], axis=0, keepdims=True), (8, 128))

    row = lambda n: pl.BlockSpec((tm, n), lambda i: (i, 0))
    full = lambda a: pl.BlockSpec(a.shape, lambda i: (0,) * a.ndim)
    res = pl.pallas_call(
        body, name="inproj_bwd", grid=(nsteps,),
        in_specs=[row(D_MODEL), full(nw), full(wt), row(D_MODEL)] + [row(width) for _, _, width in SECTIONS]
                 + [ANY_SPEC] * npart,
        out_specs=[row(D_MODEL), pl.BlockSpec((8, D_MODEL), lambda i: (0, 0)), pl.BlockSpec((8, 128), lambda i: (0, 0))]
                  + [ANY_SPEC] * npart,
        out_shape=[jax.ShapeDtypeStruct((t, D_MODEL), F32), jax.ShapeDtypeStruct((8, D_MODEL), F32),
                   jax.ShapeDtypeStruct((8, 128), F32)] + [jax.ShapeDtypeStruct(p.shape, p.dtype) for p in partials],
        scratch_shapes=[pltpu.SemaphoreType.DMA((npart, 3)), pltpu.SemaphoreType.DMA((npart, 3)),
                        pltpu.SemaphoreType.DMA((npart,))],
        compiler_params=_params(("arbitrary",)),
    )(x, nw, wt, dy, *dsecs, *partials)
    return res[0], res[1], res[2], res[3:]


def _adamw_sum(w, gs, m, v, name):
    r, c = w.shape
    nsum = gs.shape[0]
    tr = r if r <= 256 else 256
    c1 = 1.0 - ADAM_B1 ** ADAM_STEP
    c2 = 1.0 - ADAM_B2 ** ADAM_STEP

    def body(w_ref, g_ref, m_ref, v_ref, go_ref, d_ref, mo_ref, vo_ref):
        g = g_ref[0].astype(F32)
        for s in range(1, nsum):
            g = g + g_ref[s].astype(F32)
        mn = ADAM_B1 * m_ref[...] + (1.0 - ADAM_B1) * g
        vn = ADAM_B2 * v_ref[...] + (1.0 - ADAM_B2) * (g * g)
        go_ref[...] = g
        mo_ref[...] = mn
        vo_ref[...] = vn
        d_ref[...] = -ADAM_LR * ((mn / c1) / (jnp.sqrt(vn / c2) + ADAM_EPS) + ADAM_WD * w_ref[...])

    blk = pl.BlockSpec((tr, c), lambda i: (i, 0))
    return pl.pallas_call(
        body, name=name, grid=(r // tr,),
        in_specs=[blk, pl.BlockSpec((nsum, tr, c), lambda i: (0, i, 0)), blk, blk],
        out_specs=[blk] * 4, out_shape=[jax.ShapeDtypeStruct((r, c), F32)] * 4,
        compiler_params=_params(("arbitrary",)),
    )(w, gs, m, v)


def _local_step(x, target, norm_w, w_sect, w_sect_t, conv_w, a_log, dt_bias, dn_norm_w, q_norm_w, k_norm_w, rel_bias,
                wout_shard):
    lane = np.arange(128)
    arow = jnp.zeros((1, 128), F32).at[0, DN_HEADS:2 * DN_HEADS].set(-jnp.exp(a_log[0]))
    dtb = jnp.zeros((1, 128), F32).at[0, DN_HEADS:2 * DN_HEADS].set(dt_bias[0])
    g_np, gt_np = _group_mats()
    g, gt = jnp.asarray(g_np), jnp.asarray(gt_np)
    g2 = jnp.asarray(np.kron(np.eye(2, dtype=np.float32), np.ones((HD, HD), np.float32)))
    bk = jnp.asarray(_bucket_tables())
    wq = jnp.tile(q_norm_w, (1, ATT_HEADS)) * (HD ** -0.5)
    wk = jnp.tile(k_norm_w, (1, ATT_HEADS))
    del lane

    ht, pdn, qkv_dn, z, patt, gate, ba, wout8 = _inproj(x, norm_w, w_sect, conv_w, wout_shard)
    w_out = wout8.reshape(D_MODEL, D_MODEL)
    oraw, ydn, sh, th = _dn_scan_fwd(qkv_dn, ba, z, arow, dtb, dn_norm_w)
    bias = _bias_tables(rel_bias, bk)
    oatt, yatt, lse = _att_fwd(patt, gate, bias, wq, wk, g2)
    dy, mix_t, loss8 = _outproj_loss(x, ydn, yatt, w_out, target)

    do_dn, dz, do_att, dgate, dd, ddnw = _outproj_bwd(dy, w_out.T, oraw, z, dn_norm_w, oatt, gate, g, gt)
    d_wout = _grad_matmul(mix_t, dy, "dw_out")
    dq, dk, dv, drb, dwq8, dwk8 = _att_bwd(patt, do_att, lse, dd, bias, bk, wq, wk, g2)
    dqkv_dn, dba = _dn_scan_bwd(qkv_dn, ba, do_dn, sh, th, arow, dtb)
    dpdn, d_conv = _dn_prep_bwd(pdn, conv_w, dqkv_dn)
    dsecs = (dpdn, dz, dq, dk, dv, dgate, dba)
    dw_sections = [_grad_matmul(ht, d_, "dw_in_" + nm) for d_, (nm, _, _) in zip(dsecs, SECTIONS)]
    return dict(w_in_sections=dw_sections, conv_w=d_conv, w_out=d_wout, dy=dy, dsecs=dsecs,
                small_parts=(loss8, ddnw, dwq8, dwk8, drb))


def _finish_step(x, norm_w, w_sect_t, gr, partials):
    grad_x, dnw8, cs8, got = _inproj_bwd(x, norm_w, w_sect_t, gr["dy"], gr["dsecs"], partials)
    return grad_x, _pack_small_grads(dnw8, cs8, *gr["small_parts"]), got


SMALL_ROWS = 24
SMALL_AT = dict(a_log=(slice(8, 9), slice(0, 4)), dt_bias=(slice(9, 10), slice(0, 4)),
                dn_norm_w=(slice(10, 11), slice(0, 128)), q_norm_w=(slice(11, 12), slice(0, HD)),
                k_norm_w=(slice(12, 13), slice(0, HD)), rel_bias=(slice(16, 24), slice(0, N_BUCKETS)))
SMALL_NAMES = ("norm_w", "a_log", "dt_bias", "dn_norm_w", "q_norm_w", "k_norm_w", "rel_bias")


LOSS_ROW = 13


def _pack_small_grads(dnw8, cs8, loss8, ddnw8, dwq8, dwk8, drb):
    def body(dnw_ref, cs_ref, loss_ref, ddnw_ref, dwq_ref, dwk_ref, drb_ref, o_ref):
        lane = _iota((8, 128), 1)
        o_ref[...] = jnp.zeros_like(o_ref)
        o_ref[LOSS_ROW:LOSS_ROW + 1, :] = jnp.where(lane == 0, loss_ref[...], 0.0)[0:1, :]
        for k in range(D_MODEL // 128):
            o_ref[k:k + 1, :] = dnw_ref[0:1, k * 128:(k + 1) * 128]
        cs = cs_ref[...]
        o_ref[8:9, :] = jnp.where(lane < DN_HEADS, pltpu.roll(cs, 128 - 2 * DN_HEADS, 1), 0.0)[0:1, :]
        o_ref[9:10, :] = jnp.where(lane < DN_HEADS, pltpu.roll(cs, 128 - DN_HEADS, 1), 0.0)[0:1, :]
        o_ref[10:11, :] = ddnw_ref[0:1, :]
        for row, ref, scale in ((11, dwq_ref, HD ** -0.5), (12, dwk_ref, 1.0)):
            acc = ref[:, 0:128] + ref[:, 128:256] + ref[:, 256:384] + ref[:, 384:512]
            acc = (acc + pltpu.roll(acc, HD, 1)) * scale
            o_ref[row:row + 1, :] = jnp.where(lane < HD, acc, 0.0)[0:1, :]
        o_ref[16:24, :] = drb_ref[...]

    return pl.pallas_call(body, name="pack_small_grads", out_shape=jax.ShapeDtypeStruct((SMALL_ROWS, 128), F32),
                          )(dnw8, cs8, loss8, ddnw8, dwq8, dwk8, drb)


def _adam_math(w, g, m, v):
    c1 = 1.0 - ADAM_B1 ** ADAM_STEP
    c2 = 1.0 - ADAM_B2 ** ADAM_STEP
    mn = ADAM_B1 * m + (1.0 - ADAM_B1) * g
    vn = ADAM_B2 * v + (1.0 - ADAM_B2) * (g * g)
    return -ADAM_LR * ((mn / c1) / (jnp.sqrt(vn / c2) + ADAM_EPS) + ADAM_WD * w), mn, vn


def _adamw_small(gs, ws, ms, vs):
    n = len(SMALL_NAMES)

    def body(g_ref, *refs):
        w_refs, m_refs, v_refs = refs[:n], refs[n:2 * n], refs[2 * n:3 * n]
        outs, loss_ref = refs[3 * n:7 * n], refs[7 * n]
        loss = g_ref[0, LOSS_ROW:LOSS_ROW + 1, :]
        for s in range(1, gs.shape[0]):
            loss = loss + g_ref[s, LOSS_ROW:LOSS_ROW + 1, :]
        loss_ref[...] = loss

        def one(i, rows, lanes, at):
            g = g_ref[0, rows, lanes]
            for s in range(1, gs.shape[0]):
                g = g + g_ref[s, rows, lanes]
            d, mn, vn = _adam_math(w_refs[i][at], g, m_refs[i][at], v_refs[i][at])
            for kind, val in enumerate((g, d, mn, vn)):
                outs[kind * n + i][at] = val

        for k in range(D_MODEL // 128):
            one(0, slice(k, k + 1), slice(0, 128), (slice(0, 1), slice(k * 128, (k + 1) * 128)))
        for i, nm in enumerate(SMALL_NAMES[1:], start=1):
            rows, lanes = SMALL_AT[nm]
            one(i, rows, lanes, (slice(None), slice(None)))

    shapes = [jax.ShapeDtypeStruct(w.shape, F32) for w in ws]
    res = pl.pallas_call(body, name="adamw_small",
                         out_shape=shapes * 4 + [jax.ShapeDtypeStruct((1, 128), F32)])(gs, *ws, *ms, *vs)
    return [res[k * n:(k + 1) * n] for k in range(4)], res[4 * n]


def kernel(x, norm_w, w_in, conv_w, a_log, dt_bias, dn_norm_w, q_norm_w, k_norm_w, rel_bias, w_out, loss_target, m_norm_w, m_w_in, m_conv_w, m_a_log, m_dt_bias, m_dn_norm_w, m_q_norm_w, m_k_norm_w, m_rel_bias, m_w_out, v_norm_w, v_w_in, v_conv_w, v_a_log, v_dt_bias, v_dn_norm_w, v_q_norm_w, v_k_norm_w, v_rel_bias, v_w_out):
    assert w_in.shape[2] == SHARD_COLS
    win8, conv8 = _gather_weights([w_in[0].astype(MXU), conv_w[0]])
    w_sect, w_sect_t = _build_w(win8)
    conv_full = conv8.transpose(1, 0, 2).reshape(4, 3 * D_DN)

    gr = _local_step(x[0], loss_target[0], norm_w, w_sect, w_sect_t, conv_full, a_log, dt_bias, dn_norm_w, q_norm_w,
                     k_norm_w, rel_bias, w_out[0].astype(MXU))

    slabs = [_build_slabs(gr["w_in_sections"]),
             gr["w_out"].reshape(4, 2, D_MODEL // N_DEV, D_MODEL).transpose(1, 0, 2, 3),
             gr["conv_w"].reshape(4, 4, 2, 3 * D_DN // N_DEV).transpose(2, 1, 0, 3)]
    core = lax.axis_index("c").astype(jnp.int32).reshape(1)
    from_sibling = _swap_siblings(slabs)
    wires = (GRAD_WIRE, GRAD_WIRE, F32)
    partial = [_chip_sum(slabs[i], from_sibling[i], core, wires[i], "chip_sum_%d" % i) for i in range(3)]
    grad_x, small_pack, (r_win, r_wout, r_conv) = _finish_step(x[0], norm_w, w_sect_t, gr, partial)
    r_small = _share_small(small_pack)

    g_win, d_win, m_win, v_win = _adamw_sum(w_in[0], r_win, m_w_in[0], v_w_in[0], "adamw_w_in")
    g_wout, d_wout, m_wout, v_wout = _adamw_sum(w_out[0], r_wout, m_w_out[0], v_w_out[0], "adamw_w_out")
    g_conv, d_conv, m_conv, v_conv = _adamw_sum(conv_w[0], r_conv, m_conv_w[0], v_conv_w[0], "adamw_conv_w")
    small, loss_row = _adamw_small(r_small,
                                   (norm_w, a_log, dt_bias, dn_norm_w, q_norm_w, k_norm_w, rel_bias),
                                   (m_norm_w, m_a_log, m_dt_bias, m_dn_norm_w, m_q_norm_w, m_k_norm_w, m_rel_bias),
                                   (v_norm_w, v_a_log, v_dt_bias, v_dn_norm_w, v_q_norm_w, v_k_norm_w, v_rel_bias))

    loss = loss_row[0, 0]
    names = ("norm_w", "w_in", "conv_w", "a_log", "dt_bias", "dn_norm_w", "q_norm_w", "k_norm_w", "rel_bias", "w_out")
    big = dict(w_in=(g_win, d_win, m_win, v_win), conv_w=(g_conv, d_conv, m_conv, v_conv),
               w_out=(g_wout, d_wout, m_wout, v_wout))
    outs = [loss, grad_x[None]]
    for kind in range(4):
        for nm in names:
            outs.append(big[nm][kind][None] if nm in big else small[kind][SMALL_NAMES.index(nm)])
    return tuple(outs)
```

```python
import functools
import math

import numpy as np
import jax
import jax.numpy as jnp
from jax import lax
from jax.experimental import pallas as pl
from jax.experimental.pallas import tpu as pltpu

F32 = jnp.float32
MXU = jnp.bfloat16
GRAD_WIRE = jnp.bfloat16
HI = lax.Precision.HIGHEST

D_MODEL = 1024
D_DN = 512
DN_HEADS = 4
DK = 128
CHUNK = 64
D_ATT = 512
ATT_HEADS = 8
HD = 64
PATTERNS = ((128, 1), (512, 4), (2048, 16))
BLK = 128
N_BUCKETS = 32
MAX_DISTANCE = 2048
EPS = 1e-6
W_COLS = 4224
N_DEV = 8
AXES = ("x", "y", "c")

ADAM_LR = 0.001
ADAM_B1 = 0.9
ADAM_B2 = 0.999
ADAM_EPS = 1e-08
ADAM_WD = 0.01
ADAM_STEP = 10

VMEM_LIMIT = 56 * 1024 * 1024
NEG = -1e30


def _dot(a, b):
    return jnp.dot(a.astype(MXU), b.astype(MXU), preferred_element_type=F32)


def _dot_nt(a, b):
    return lax.dot_general(a.astype(MXU), b.astype(MXU), (((1,), (1,)), ((), ())), preferred_element_type=F32)


def _dot_tn(a, b):
    return lax.dot_general(a.astype(MXU), b.astype(MXU), (((0,), (0,)), ((), ())), preferred_element_type=F32)


def _split(a):
    hi = a.astype(jnp.bfloat16)
    return hi, (a - hi.astype(F32)).astype(jnp.bfloat16)


def _dot_split(a, b, dims, exact):
    dg = lambda u, v: lax.dot_general(u, v, (dims, ((), ())), preferred_element_type=F32)
    if exact == "b":
        ah, al = _split(a)
        bh = b.astype(jnp.bfloat16)
        return dg(ah, bh) + dg(al, bh)
    if exact == "a":
        bh, bm = _split(b)
        bl = (b - bh.astype(F32) - bm.astype(F32)).astype(jnp.bfloat16)
        ah = a.astype(jnp.bfloat16)
        return dg(ah, bh) + (dg(ah, bm) + dg(ah, bl))
    ah, al = _split(a)
    bh, bl = _split(b)
    return dg(ah, bh) + (dg(ah, bl) + dg(al, bh))


def _wy_inverses(amat, eye):
    tinv = {x: eye - amat[x] for x in amat}
    pw = amat
    for _ in range(5):
        pw = {x: _hdot(pw[x], pw[x]) for x in amat}
        tinv = {x: tinv[x] + _hdot(tinv[x], pw[x]) for x in amat}
    return tinv


def _hdot(a, b, exact=None):
    return _dot_split(a, b, ((1,), (0,)), exact)


def _hdot_nt(a, b, exact=None):
    return _dot_split(a, b, ((1,), (1,)), exact)


def _hdot_tn(a, b, exact=None):
    return _dot_split(a, b, ((0,), (0,)), exact)


def _sigmoid(x):
    return 1.0 / (1.0 + jnp.exp(-x))


def _silu(x):
    return x * _sigmoid(x)


def _silu_and_grad(x):
    s = _sigmoid(x)
    return x * s, s * (1.0 + x * (1.0 - s))


def _softplus(x):
    return jnp.maximum(x, 0.0) + jnp.log(1.0 + jnp.exp(-jnp.abs(x)))


def _iota(shape, dim):
    return lax.broadcasted_iota(jnp.int32, shape, dim)


def _lane_col(x, k):
    return jnp.sum(jnp.where(_iota(x.shape, 1) == k, x, 0.0), axis=1, keepdims=True)


def _params(sem=None):
    return pltpu.CompilerParams(dimension_semantics=sem, vmem_limit_bytes=VMEM_LIMIT)


def _t5_bucket(dist):
    max_exact = N_BUCKETS // 2
    d = np.maximum(dist, 1).astype(np.float64)
    large = max_exact + (np.log(d / max_exact) / math.log(MAX_DISTANCE / max_exact)
                         * (N_BUCKETS - max_exact)).astype(np.int32)
    large = np.minimum(large, N_BUCKETS - 1)
    return np.where(dist < max_exact, dist, large).astype(np.int32)


def _bucket_tables():
    qi = np.arange(BLK)[:, None]
    kj = np.arange(2 * BLK)[None, :]
    step = qi - kj + BLK
    band = (step >= 0) & (step <= BLK)
    out = []
    for _, r in PATTERNS:
        b = _t5_bucket(np.clip(step, 0, None) * r)
        out.append(np.where(band, b, -1))
    return np.stack(out).astype(np.int32)


def _group_mats():
    g = np.zeros((D_ATT, 128), np.float32)
    for h in range(ATT_HEADS):
        g[h * HD:(h + 1) * HD, h] = 1.0
    return g, np.ascontiguousarray(g.T)


CHIP_FLIPS = ((1, 0), (0, 1), (1, 1))
ANY_SPEC = pl.BlockSpec(memory_space=pl.ANY)
MESH_ID = pl.DeviceIdType.MESH


def _other_chips():
    x, y = lax.axis_index("x"), lax.axis_index("y")
    return [((1 - x if fx else x), (1 - y if fy else y)) for fx, fy in CHIP_FLIPS]


def _gather_plan(ins, outs, send, recv, loc):
    n = len(ins)
    x, y, c = (lax.axis_index(a) for a in AXES)
    sib = (x, y, 1 - c)
    chips = _other_chips()
    lin = lambda px, py, pc: 4 * px + 2 * py + pc

    def copy(a, k, block, to, src=None):
        slot = outs[a].at[lin(*block)]
        return pltpu.make_async_remote_copy(src_ref=slot if src is None else src, dst_ref=slot,
                                            send_sem=send.at[a, k], recv_sem=recv.at[a, k],
                                            device_id=to, device_id_type=MESH_ID)

    mine = [pltpu.make_async_copy(ins[a], outs[a].at[lin(x, y, c)], loc.at[a]) for a in range(n)]
    firsts = []
    for a in range(n):
        firsts.append(copy(a, 0, (x, y, c), sib, src=ins[a]))
        firsts += [copy(a, 1 + j, (x, y, c), (*chip, c), src=ins[a]) for j, chip in enumerate(chips)]

    def begin():
        for cp in mine + firsts:
            cp.start()

    def finish():
        passed = []
        for j, chip in enumerate(chips):
            for a in range(n):
                copy(a, 1 + j, (*chip, c), (x, y, c)).wait_recv()
                fw = copy(a, 4 + j, (*chip, c), sib)
                fw.start()
                passed.append(fw)
        for a in range(n):
            copy(a, 0, sib, (x, y, c)).wait_recv()
            for j, chip in enumerate(chips):
                copy(a, 4 + j, (*chip, 1 - c), (x, y, c)).wait_recv()
        for cp in firsts + passed:
            cp.wait_send()
        for cp in mine:
            cp.wait()

    return begin, finish


GATHER_SEMS = lambda n: [pltpu.SemaphoreType.DMA((n, 7)), pltpu.SemaphoreType.DMA((n, 7)), pltpu.SemaphoreType.DMA((n,))]


def _swap_siblings(arrs):
    n = len(arrs)

    def body(*refs):
        ins, outs = refs[:n], refs[n:2 * n]
        send, recv = refs[2 * n:]
        x, y, c = (lax.axis_index(a) for a in AXES)
        cps = [pltpu.make_async_remote_copy(src_ref=ins[a].at[1 - c], dst_ref=outs[a], send_sem=send.at[a],
                                            recv_sem=recv.at[a], device_id=(x, y, 1 - c), device_id_type=MESH_ID)
               for a in range(n)]
        for cp in cps:
            cp.start()
        for cp in cps:
            cp.wait()

    return pl.pallas_call(
        body, name="swap_siblings", out_shape=[jax.ShapeDtypeStruct(a.shape[1:], a.dtype) for a in arrs],
        in_specs=[ANY_SPEC] * n, out_specs=[ANY_SPEC] * n,
        scratch_shapes=[pltpu.SemaphoreType.DMA((n,)), pltpu.SemaphoreType.DMA((n,))],
    )(*arrs)


def _chip_sum(mine2, theirs, core, wire, name):
    _, nchip, r, cdim = mine2.shape
    tr = r if r <= 256 else 256

    def body(core_ref, a_ref, b_ref, o_ref):
        del core_ref
        o_ref[...] = (a_ref[...].astype(F32) + b_ref[...].astype(F32)).astype(wire)

    grid_spec = pltpu.PrefetchScalarGridSpec(
        num_scalar_prefetch=1, grid=(nchip, r // tr),
        in_specs=[pl.BlockSpec((None, None, tr, cdim), lambda j, i, cr: (cr[0], j, i, 0)),
                  pl.BlockSpec((None, tr, cdim), lambda j, i, cr: (j, i, 0))],
        out_specs=pl.BlockSpec((None, tr, cdim), lambda j, i, cr: (j, i, 0)))
    return pl.pallas_call(
        body, name=name, grid_spec=grid_spec, out_shape=jax.ShapeDtypeStruct((nchip, r, cdim), wire),
        compiler_params=_params(("arbitrary", "arbitrary")),
    )(core, mine2, theirs)


def _chip_swap_copies(ins, outs, send, recv, loc):
    x, y, c = (lax.axis_index(a) for a in AXES)
    me = 2 * x + y
    starts, arrivals, drains = [], [], []
    for a in range(len(ins)):
        lc = pltpu.make_async_copy(ins[a].at[me], outs[a].at[me], loc.at[a])
        starts.append(lc.start)
        drains.append(lc.wait)
        for j, (px, py) in enumerate(_other_chips()):
            them = 2 * px + py
            cp = pltpu.make_async_remote_copy(src_ref=ins[a].at[them], dst_ref=outs[a].at[me], send_sem=send.at[a, j],
                                              recv_sem=recv.at[a, j], device_id=(px, py, c), device_id_type=MESH_ID)
            landing = pltpu.make_async_remote_copy(src_ref=ins[a].at[them], dst_ref=outs[a].at[them],
                                                   send_sem=send.at[a, j], recv_sem=recv.at[a, j],
                                                   device_id=(px, py, c), device_id_type=MESH_ID)
            starts.append(cp.start)
            arrivals.append(landing.wait_recv)
            drains.append(cp.wait_send)
    return starts, arrivals + drains


def _share_small(pack):
    def body(in_ref, out_ref, send, recv, loc):
        x, y, c = (lax.axis_index(a) for a in AXES)
        me = 4 * x + 2 * y + c
        lc = pltpu.make_async_copy(in_ref, out_ref.at[me], loc.at[0])
        lc.start()
        sends, arrivals = [], []
        for k in range(1, N_DEV):
            px = 1 - x if k & 4 else x
            py = 1 - y if k & 2 else y
            pc = 1 - c if k & 1 else c
            cp = pltpu.make_async_remote_copy(src_ref=in_ref, dst_ref=out_ref.at[me], send_sem=send.at[k - 1],
                                              recv_sem=recv.at[k - 1], device_id=(px, py, pc), device_id_type=MESH_ID)
            cp.start()
            sends.append(cp)
            arrivals.append(pltpu.make_async_remote_copy(src_ref=in_ref, dst_ref=out_ref.at[4 * px + 2 * py + pc],
                                                         send_sem=send.at[k - 1], recv_sem=recv.at[k - 1],
                                                         device_id=(px, py, pc), device_id_type=MESH_ID))
        for cp in arrivals:
            cp.wait_recv()
        for cp in sends:
            cp.wait_send()
        lc.wait()

    return pl.pallas_call(
        body, name="share_small", out_shape=jax.ShapeDtypeStruct((N_DEV,) + pack.shape, pack.dtype),
        in_specs=[ANY_SPEC], out_specs=ANY_SPEC,
        scratch_shapes=[pltpu.SemaphoreType.DMA((N_DEV - 1,)), pltpu.SemaphoreType.DMA((N_DEV - 1,)),
                        pltpu.SemaphoreType.DMA((1,))],
    )(pack)


W_PARTS = ((0, 0, 2048), (2048, 4096, 8), (2056, 2048, 2048))
SHARD_COLS = 513


def _pieces(lo, hi, parts):
    out = []
    for ref_start, tgt_start, width in parts:
        a, b = max(lo, ref_start), min(hi, ref_start + width)
        if a < b:
            out.append((a - lo, tgt_start + a - ref_start, b - a))
    return out


def _build_w(win8):
    tr = 256

    def body(in_ref, w_ref, wt_ref):
        w_ref[:, 4096:W_COLS] = jnp.zeros((tr, W_COLS - 4096), MXU)
        for p in range(N_DEV):
            for src, dst, width in _pieces(p * SHARD_COLS, (p + 1) * SHARD_COLS, W_PARTS):
                w_ref[:, dst:dst + width] = in_ref[p, :, src:src + width]
        for k in range(W_COLS // 128):
            wt_ref[k * 128:(k + 1) * 128, :] = w_ref[:, k * 128:(k + 1) * 128].astype(F32).T.astype(MXU)

    return pl.pallas_call(
        body, name="build_w", grid=(D_MODEL // tr,),
        in_specs=[pl.BlockSpec((N_DEV, tr, SHARD_COLS), lambda i: (0, i, 0))],
        out_specs=[pl.BlockSpec((tr, W_COLS), lambda i: (i, 0)), pl.BlockSpec((W_COLS, tr), lambda i: (0, i))],
        out_shape=[jax.ShapeDtypeStruct((D_MODEL, W_COLS), MXU), jax.ShapeDtypeStruct((W_COLS, D_MODEL), MXU)],
        compiler_params=_params(("arbitrary",)),
    )(win8)


def _build_slabs(secs):
    tr = 256
    parts = ((0, 0, 1536), (1536, 1, 512), (2048, 6, 8), (2056, 2, 512), (2568, 3, 512), (3080, 4, 512),
             (3592, 5, 512))

    def body(*refs):
        o_ref = refs[len(secs)]
        for p in range(N_DEV):
            lo, hi = p * SHARD_COLS, (p + 1) * SHARD_COLS
            for ref_start, idx, width in parts:
                a, b = max(lo, ref_start), min(hi, ref_start + width)
                if a < b:
                    o_ref[p % 2, p // 2, :, a - lo:b - lo] = refs[idx][:, a - ref_start:b - ref_start]

    return pl.pallas_call(
        body, name="build_slabs", grid=(D_MODEL // tr,),
        in_specs=[pl.BlockSpec((tr, s.shape[1]), lambda i: (i, 0)) for s in secs],
        out_specs=pl.BlockSpec((2, 4, tr, SHARD_COLS), lambda i: (0, 0, i, 0)),
        out_shape=jax.ShapeDtypeStruct((2, 4, D_MODEL, SHARD_COLS), secs[0].dtype),
        compiler_params=_params(("arbitrary",)),
    )(*secs)


def _norm_and_gather(x, nw, shards):
    t = x.shape[0]
    tm = 512
    nsteps = t // tm
    n = len(shards)

    def body(x_ref, nw_ref, *rest):
        ins, (h_ref, ht_ref), outs, sems = rest[:n], rest[n:n + 2], rest[n + 2:2 * n + 2], rest[2 * n + 2:]
        begin, finish = _gather_plan(ins, outs, *sems)

        @pl.when(pl.program_id(0) == 0)
        def _():
            begin()

        @pl.when(pl.program_id(0) == nsteps - 1)
        def _():
            finish()

        xv = x_ref[...]
        rstd = lax.rsqrt(jnp.mean(xv * xv, axis=-1, keepdims=True) + EPS)
        hf = xv * rstd * nw_ref[...]
        h_ref[...] = hf.astype(MXU)
        ht_ref[...] = hf.T.astype(MXU)

    res = pl.pallas_call(
        body, name="norm_and_gather", grid=(nsteps,),
        in_specs=[pl.BlockSpec((tm, D_MODEL), lambda i: (i, 0)), pl.BlockSpec(nw.shape, lambda i: (0, 0))]
                 + [ANY_SPEC] * n,
        out_specs=[pl.BlockSpec((tm, D_MODEL), lambda i: (i, 0)), pl.BlockSpec((D_MODEL, tm), lambda i: (0, i))]
                  + [ANY_SPEC] * n,
        out_shape=[jax.ShapeDtypeStruct((t, D_MODEL), MXU), jax.ShapeDtypeStruct((D_MODEL, t), MXU)]
                  + [jax.ShapeDtypeStruct((N_DEV,) + a.shape, a.dtype) for a in shards],
        scratch_shapes=GATHER_SEMS(n),
        compiler_params=_params(("arbitrary",)),
    )(x, nw, *shards)
    return res[0], res[1], res[2:]


def _inproj(h_all, w, cw, wout_shard):
    t = h_all.shape[0]
    tm = 256
    nsteps = t // tm

    def body(h_ref, w_ref, cw_ref, wo_ref, pdn_ref, qkv_ref, z_ref, patt_ref, gate_ref, ba_ref,
             wo8_ref, halo_ref, send, recv, loc):
        begin, finish = _gather_plan([wo_ref], [wo8_ref], send, recv, loc)

        @pl.when(pl.program_id(0) == 0)
        def _():
            begin()
            halo_ref[...] = jnp.zeros_like(halo_ref)

        @pl.when(pl.program_id(0) == nsteps - 1)
        def _():
            finish()

        h = h_ref[...]
        for ref, lo, hi in ((z_ref, 1536, 2048), (patt_ref, 2048, 3584), (gate_ref, 3584, 4096), (ba_ref, 4096, 4224)):
            ref[...] = jnp.dot(h, w_ref[:, lo:hi], preferred_element_type=F32)
        pdn = jnp.dot(h, w_ref[:, 0:3 * D_DN], preferred_element_type=F32)
        pdn_ref[...] = pdn
        _dn_prep_tile(pdn, halo_ref, cw_ref, qkv_ref)

    row = lambda n: pl.BlockSpec((tm, n), lambda i: (i, 0))
    full = lambda a: pl.BlockSpec(a.shape, lambda i: (0,) * a.ndim)
    return pl.pallas_call(
        body, name="inproj", grid=(nsteps,),
        in_specs=[row(D_MODEL), full(w), full(cw), ANY_SPEC],
        out_specs=[row(1536), row(1536), row(512), row(1536), row(512), row(128), ANY_SPEC],
        out_shape=[jax.ShapeDtypeStruct((t, n), F32) for n in (1536, 1536, 512, 1536, 512, 128)] +
                  [jax.ShapeDtypeStruct((N_DEV,) + wout_shard.shape, wout_shard.dtype)],
        scratch_shapes=[pltpu.VMEM((8, 3 * D_DN), F32)] + GATHER_SEMS(1),
        compiler_params=_params(("arbitrary",)),
    )(h_all, w, cw, wout_shard)


CONV_ROWS = 512


def _conv_taps(u_ref, c, w_ref):
    r0 = c * CONV_ROWS
    if c == 0:
        ext = jnp.concatenate([jnp.zeros((8, 128), F32), u_ref[0:CONV_ROWS, :]], axis=0)
    else:
        ext = u_ref[r0 - 8:r0 + CONV_ROWS, :]
    taps = [ext[8:, :]] + [pltpu.roll(ext, k, 0)[8:, :] for k in (1, 2, 3)]
    y = taps[0] * w_ref[3:4, :]
    for k in (1, 2, 3):
        y = y + taps[k] * w_ref[3 - k:4 - k, :]
    return taps, y


def _dn_prep_tile(pdn, halo_ref, cw_ref, out_ref):
    rows = pdn.shape[0]
    ext = jnp.concatenate([halo_ref[...], pdn], axis=0)
    halo_ref[...] = pdn[rows - 8:, :]
    for j in range(3 * D_DN // 128):
        cols = slice(j * 128, (j + 1) * 128)
        e = ext[:, cols]
        y = e[8:, :] * cw_ref[3:4, cols]
        for k in (1, 2, 3):
            y = y + pltpu.roll(e, k, 0)[8:, :] * cw_ref[3 - k:4 - k, cols]
        a = _silu(y)
        if j < 2 * DN_HEADS:
            a = a * lax.rsqrt(jnp.sum(a * a, axis=1, keepdims=True) + EPS)
        if j < DN_HEADS:
            a = a * DK ** -0.5
        out_ref[:, cols] = a


def _chunk_common(qkv, ba, arow, dtb):
    c = CHUNK
    ri, ci = _iota((c, c), 0), _iota((c, c), 1)
    lane = _iota((c, 128), 1)
    g_all = jnp.where((lane >= DN_HEADS) & (lane < 2 * DN_HEADS), arow * _softplus(ba + dtb), 0.0)
    gc_all = _hdot((ri >= ci).astype(F32), g_all, "a")
    gc_t = gc_all.T
    beta_all = _sigmoid(ba)
    out = []
    for h in range(DN_HEADS):
        gc = _lane_col(gc_all, DN_HEADS + h)
        gcr = gc_t[DN_HEADS + h:DN_HEADS + h + 1, :]
        gl = gc[c - 1:c, :]
        out.append(dict(
            q=qkv[:, h * DK:(h + 1) * DK], k=qkv[:, D_DN + h * DK:D_DN + (h + 1) * DK],
            v=qkv[:, 2 * D_DN + h * DK:2 * D_DN + (h + 1) * DK],
            beta=_lane_col(beta_all, h), g=_lane_col(g_all, DN_HEADS + h),
            a_raw=_lane_col(ba, DN_HEADS + h), a_h=_lane_col(arow, DN_HEADS + h), dt_h=_lane_col(dtb, DN_HEADS + h),
            decay=jnp.exp(jnp.where(ri >= ci, gc - gcr, NEG)), eg=jnp.exp(gc), egl=jnp.exp(gl), etail=jnp.exp(gl - gc)))
    return out, ri, ci


SCAN_CHUNKS = 8


def _dn_scan_fwd(qkv, ba, z, arow, dtb, dnw):
    t = qkv.shape[0]
    n = t // CHUNK
    c = CHUNK
    cps = SCAN_CHUNKS
    hs = range(DN_HEADS)
    chains = [(j, h) for j in range(cps) for h in hs]

    def body(qkv_ref, ba_ref, z_ref, arow_ref, dtb_ref, dnw_ref, o_ref, y_ref, sh_ref, th_ref, s_ref):
        @pl.when(pl.program_id(0) == 0)
        def _():
            s_ref[...] = jnp.zeros_like(s_ref)

        ms = {}
        for j in range(cps):
            rows = slice(j * c, (j + 1) * c)
            mj, ri, ci = _chunk_common(qkv_ref[rows, :], ba_ref[rows, :], arow_ref[...], dtb_ref[...])
            for h in hs:
                ms[j, h] = mj[h]
        kb = {x: ms[x]["k"] * ms[x]["beta"] for x in chains}
        amat = {x: jnp.where(ri > ci, _dot_nt(kb[x], ms[x]["k"]) * ms[x]["decay"], 0.0) for x in chains}
        attn = {x: jnp.where(ri >= ci, _dot_nt(ms[x]["q"], ms[x]["k"]) * ms[x]["decay"], 0.0) for x in chains}
        tinv = _wy_inverses(amat, (ri == ci).astype(F32))
        uw = {x: _hdot(tinv[x], jnp.concatenate([ms[x]["v"] * ms[x]["beta"], kb[x] * ms[x]["eg"]], axis=1))
              for x in chains}
        u = {x: uw[x][:, :DK] for x in chains}
        w = {x: uw[x][:, DK:] for x in chains}
        q_dec = {x: ms[x]["q"] * ms[x]["eg"] for x in chains}
        k_tail = {x: ms[x]["k"] * ms[x]["etail"] for x in chains}
        s = [s_ref[h] for h in hs]
        for j in range(cps):
            rows = slice(j * c, (j + 1) * c)
            v_new = [u[j, h] - _dot(w[j, h], s[h]) for h in hs]
            o = [_dot(q_dec[j, h], s[h]) + _dot(attn[j, h], v_new[h]) for h in hs]
            for h in hs:
                sh_ref[j, h] = s[h]
                th_ref[j, h] = tinv[j, h]
            s = [s[h] * ms[j, h]["egl"] + _dot_tn(k_tail[j, h], v_new[h]) for h in hs]
            for h in hs:
                cols = slice(h * DK, (h + 1) * DK)
                o_ref[rows, cols] = o[h]
                rs = lax.rsqrt(jnp.mean(o[h] * o[h], axis=1, keepdims=True) + EPS)
                y_ref[rows, cols] = o[h] * rs * dnw_ref[...] * _silu(z_ref[rows, cols])
        for h in hs:
            s_ref[h] = s[h]

    row = lambda w_: pl.BlockSpec((cps * c, w_), lambda i: (i, 0))
    one = pl.BlockSpec((1, 128), lambda i: (0, 0))
    return pl.pallas_call(
        body, name="dn_scan_fwd", grid=(n // cps,),
        in_specs=[row(1536), row(128), row(512), one, one, one],
        out_specs=[row(512), row(512), pl.BlockSpec((cps, DN_HEADS, DK, DK), lambda i: (i, 0, 0, 0)),
                   pl.BlockSpec((cps, DN_HEADS, c, c), lambda i: (i, 0, 0, 0))],
        out_shape=[jax.ShapeDtypeStruct((t, 512), F32), jax.ShapeDtypeStruct((t, 512), F32),
                   jax.ShapeDtypeStruct((n, DN_HEADS, DK, DK), F32), jax.ShapeDtypeStruct((n, DN_HEADS, c, c), F32)],
        scratch_shapes=[pltpu.VMEM((DN_HEADS, DK, DK), F32)],
        compiler_params=_params(("arbitrary",)),
    )(qkv, ba, z, arow, dtb, dnw)


ATT_ROWS = 512


def _pair_rstd(xv, g2_ref):
    return lax.rsqrt(_hdot(xv * xv, g2_ref[...], "b") * (1.0 / HD) + EPS)


def _pair_norm(t, raw_refs, w_refs, out_refs, g2_ref):
    for c in range(t // ATT_ROWS):
        sl = slice(c * ATT_ROWS, (c + 1) * ATT_ROWS)
        for raw, w_ref, out in zip(raw_refs, w_refs, out_refs):
            xv = raw[sl, :]
            out[sl, :] = xv * _pair_rstd(xv, g2_ref) * w_ref[...]


def _bias_tables(rb, bk):
    def body(rb_ref, bk_ref, bias_ref):
        pair = pl.program_id(0)
        for p in range(len(PATTERNS)):
            bk_p = bk_ref[p]
            for hh in range(2):
                head = 2 * pair + hh
                bm = jnp.full((BLK, 2 * BLK), NEG, F32)
                for b in range(N_BUCKETS):
                    bm = jnp.where(bk_p == b, rb_ref[head, b], bm)
                bias_ref[p, hh * BLK:(hh + 1) * BLK, :] = bm

    return pl.pallas_call(
        body, name="bias_tables", grid=(ATT_HEADS // 2,),
        in_specs=[pl.BlockSpec(memory_space=pltpu.SMEM), pl.BlockSpec(bk.shape, lambda i: (0, 0, 0))],
        out_specs=pl.BlockSpec((len(PATTERNS), None, 2 * BLK, 2 * BLK), lambda i: (0, i, 0, 0)),
        out_shape=jax.ShapeDtypeStruct((len(PATTERNS), ATT_HEADS // 2, 2 * BLK, 2 * BLK), F32),
        compiler_params=_params(("arbitrary",)),
    )(rb, bk)


BIAS_SPEC = pl.BlockSpec((len(PATTERNS), None, 2 * BLK, 2 * BLK), lambda i: (0, i, 0, 0))
PAIR_ROW_SPEC = pl.BlockSpec((1, 128), lambda i: (0, i))


def _stack_heads(xb, h0):
    return jnp.concatenate([jnp.where(h0, xb, 0.0), jnp.where(h0, 0.0, xb)], axis=0).astype(MXU)


def _block_rows(t, r, n):
    per_class = (t // r) // BLK
    res = n // per_class
    j = n % per_class
    start = res + BLK * r * j
    pstart = res + BLK * r * jnp.maximum(j - 1, 0)
    if r == 1:
        return pl.ds(pl.multiple_of(start, BLK), BLK), pl.ds(pl.multiple_of(pstart, BLK), BLK), j
    return pl.ds(start, BLK, stride=r), pl.ds(pstart, BLK, stride=r), j


def _att_fwd(qkv, gate, bias, wq, wk, g2):
    t = qkv.shape[0]
    rows = ATT_ROWS

    def body(bias_ref, qraw_ref, kraw_ref, v_ref, g_ref, wq_ref, wk_ref, g2_ref, o_ref, y_ref, lse_ref,
             o0_ref, o1_ref, o2_ref, l0_ref, l1_ref, l2_ref, q_ref, k_ref):
        h0 = _iota((BLK, 128), 1) < HD
        prev_cols = _iota((2 * BLK, 2 * BLK), 1) < BLK
        op_refs, lp_refs = (o0_ref, o1_ref, o2_ref), (l0_ref, l1_ref, l2_ref)
        _pair_norm(t, (qraw_ref, kraw_ref), (wq_ref, wk_ref), (q_ref, k_ref), g2_ref)

        for p, (_, r) in enumerate(PATTERNS):
            def blk(n, carry, p=p, r=r):
                cur, prev, j = _block_rows(t, r, n)
                q2 = _stack_heads(q_ref[cur, :], h0)
                k2 = jnp.concatenate([k_ref[prev, :], k_ref[cur, :]], axis=0).astype(MXU)
                v2 = jnp.concatenate([v_ref[prev, :], v_ref[cur, :]], axis=0).astype(MXU)
                s = _dot_nt(q2, k2) + bias_ref[p] + jnp.where(prev_cols & (j == 0), NEG, 0.0)
                m = jnp.max(s, axis=1, keepdims=True)
                e = jnp.exp(s - m)
                l = jnp.sum(e, axis=1, keepdims=True)
                pv = _dot(e, v2) / l
                lse = m + jnp.log(l)
                op_refs[p][cur, :] = jnp.where(h0, pv[:BLK], pv[BLK:])
                lp_refs[p][cur, :] = jnp.where(h0, lse[:BLK], lse[BLK:])
                return carry

            lax.fori_loop(0, t // BLK, blk, 0, unroll=8)

        for c in range(t // rows):
            sl = slice(c * rows, (c + 1) * rows)
            ls = [ref[sl, :] for ref in lp_refs]
            mx = jnp.maximum(jnp.maximum(ls[0], ls[1]), ls[2])
            ws = [jnp.exp(v_ - mx) for v_ in ls]
            den = ws[0] + ws[1] + ws[2]
            o = (ws[0] * o0_ref[sl, :] + ws[1] * o1_ref[sl, :] + ws[2] * o2_ref[sl, :]) / den
            o_ref[sl, :] = o
            y_ref[sl, :] = o * _silu(g_ref[sl, :])
            lse_ref[sl, :] = mx + jnp.log(den)

    col = lambda off: pl.BlockSpec((t, 128), lambda i, off=off: (0, off + i))
    return pl.pallas_call(
        body, name="att_fwd", grid=(ATT_HEADS // 2,),
        in_specs=[BIAS_SPEC, col(0), col(4), col(8), col(0), PAIR_ROW_SPEC, PAIR_ROW_SPEC,
                  pl.BlockSpec((128, 128), lambda i: (0, 0))],
        out_specs=[col(0), col(0), col(0)],
        out_shape=[jax.ShapeDtypeStruct((t, 512), F32)] * 3,
        scratch_shapes=[pltpu.VMEM((t, 128), F32)] * 8,
        compiler_params=_params(("arbitrary",)),
    )(bias, qkv, qkv, qkv, gate, wq, wk, g2)


def _outproj_loss(x, ydn, yatt, wout, target):
    t = x.shape[0]
    tm = 512

    def body(x_ref, a_ref, b_ref, w_ref, t_ref, dy_ref, mix_ref, loss_ref):
        @pl.when(pl.program_id(0) == 0)
        def _():
            loss_ref[...] = jnp.zeros_like(loss_ref)

        mixf = jnp.concatenate([a_ref[...], b_ref[...]], axis=1)
        mix_ref[...] = mixf.T.astype(MXU)
        err = x_ref[...] + jnp.dot(mixf.astype(MXU), w_ref[...], preferred_element_type=F32) - t_ref[...]
        dy_ref[...] = err * (1.0 / D_MODEL)
        loss_ref[...] += jnp.sum(err * err) * (0.5 / D_MODEL)

    row = lambda n: pl.BlockSpec((tm, n), lambda i: (i, 0))
    return pl.pallas_call(
        body, name="outproj_loss", grid=(t // tm,),
        in_specs=[row(D_MODEL), row(512), row(512), pl.BlockSpec(wout.shape, lambda i: (0, 0)), row(D_MODEL)],
        out_specs=[row(D_MODEL), pl.BlockSpec((D_MODEL, tm), lambda i: (0, i)), pl.BlockSpec((8, 128), lambda i: (0, 0))],
        out_shape=[jax.ShapeDtypeStruct((t, D_MODEL), F32), jax.ShapeDtypeStruct((D_MODEL, t), MXU),
                   jax.ShapeDtypeStruct((8, 128), F32)],
        compiler_params=_params(("arbitrary",)),
    )(x, ydn, yatt, wout, target)


def _outproj_bwd(dy, wout_t, oraw, z, dnw, oatt, gate, g, gt):
    t = dy.shape[0]
    tm = 256

    def body(dy_ref, w_ref, o_ref, z_ref, dnw_ref, oa_ref, g_ref, grp_ref, grpt_ref,
             do_ref, dz_ref, doa_ref, dg_ref, dd_ref, ddnw_ref):
        @pl.when(pl.program_id(0) == 0)
        def _():
            ddnw_ref[...] = jnp.zeros_like(ddnw_ref)

        dmix = jnp.dot(dy_ref[...].astype(MXU), w_ref[...], preferred_element_type=F32)
        dnw_v = dnw_ref[...]
        acc = jnp.zeros((1, DK), F32)
        for h in range(DN_HEADS):
            sl = slice(h * DK, (h + 1) * DK)
            o, zz, dm = o_ref[:, sl], z_ref[:, sl], dmix[:, sl]
            rs = lax.rsqrt(jnp.mean(o * o, axis=1, keepdims=True) + EPS)
            oh = o * rs
            silu_z, dsilu_z = _silu_and_grad(zz)
            dz_ref[:, sl] = dm * oh * dnw_v * dsilu_z
            d_on = dm * silu_z
            gg = d_on * dnw_v
            do_ref[:, sl] = rs * (gg - oh * jnp.mean(gg * oh, axis=1, keepdims=True))
            acc = acc + jnp.sum(d_on * oh, axis=0, keepdims=True)
        ddnw_ref[...] += jnp.broadcast_to(acc, (8, DK))
        da, gate_v, oa = dmix[:, 512:], g_ref[...], oa_ref[...]
        silu_g, dsilu_g = _silu_and_grad(gate_v)
        doa = da * silu_g
        doa_ref[...] = doa
        dg_ref[...] = da * oa * dsilu_g
        dd_ref[...] = _hdot(_hdot(doa * oa, grp_ref[...], "b"), grpt_ref[...], "b")

    row = lambda n: pl.BlockSpec((tm, n), lambda i: (i, 0))
    full = lambda a: pl.BlockSpec(a.shape, lambda i: (0,) * a.ndim)
    return pl.pallas_call(
        body, name="outproj_bwd", grid=(t // tm,),
        in_specs=[row(D_MODEL), full(wout_t), row(512), row(512), full(dnw), row(512), row(512), full(g), full(gt)],
        out_specs=[row(512)] * 5 + [pl.BlockSpec((8, DK), lambda i: (0, 0))],
        out_shape=[jax.ShapeDtypeStruct((t, 512), F32)] * 5 + [jax.ShapeDtypeStruct((8, DK), F32)],
        compiler_params=_params(("arbitrary",)),
    )(dy, wout_t, oraw, z, dnw, oatt, gate, g, gt)


def _grad_matmul(at, b, name):
    m, t = at.shape
    n = b.shape[1]
    tk = 512
    tn = n if n <= 512 else 512
    nk = t // tk

    def body(a_ref, b_ref, o_ref, acc_ref):
        k = pl.program_id(1)

        @pl.when(k == 0)
        def _():
            acc_ref[...] = jnp.zeros_like(acc_ref)

        acc_ref[...] += jnp.dot(a_ref[...], b_ref[...].astype(MXU), preferred_element_type=F32)

        @pl.when(k == nk - 1)
        def _():
            o_ref[...] = acc_ref[...].astype(GRAD_WIRE)

    return pl.pallas_call(
        body, name=name, grid=(n // tn, nk),
        in_specs=[pl.BlockSpec((m, tk), lambda j, k: (0, k)), pl.BlockSpec((tk, tn), lambda j, k: (k, j))],
        out_specs=pl.BlockSpec((m, tn), lambda j, k: (0, j)),
        out_shape=jax.ShapeDtypeStruct((m, n), GRAD_WIRE),
        scratch_shapes=[pltpu.VMEM((m, tn), F32)],
        compiler_params=_params(("arbitrary", "arbitrary")),
    )(at, b)


def _att_bwd(qkv, do, lse, dd, bias, bk, wq, wk, g2):
    t = qkv.shape[0]
    rows = ATT_ROWS

    def body(bias_ref, bk_ref, qraw_ref, kraw_ref, v_ref, do_ref, lse_ref, dd_ref, wq_ref, wk_ref, g2_ref,
             dq_ref, dk_ref, dv_ref, db_ref, dwq_ref, dwk_ref, ds_ref, q_ref, k_ref):
        pair = pl.program_id(0)

        @pl.when(pair == 0)
        def _():
            db_ref[...] = jnp.zeros_like(db_ref)

        _pair_norm(t, (qraw_ref, kraw_ref), (wq_ref, wk_ref), (q_ref, k_ref), g2_ref)
        ds_ref[...] = jnp.zeros_like(ds_ref)
        for c in range(t // rows):
            sl = slice(c * rows, (c + 1) * rows)
            for ref in (dq_ref, dk_ref, dv_ref):
                ref[sl, :] = jnp.zeros((rows, 128), F32)
        h0 = _iota((BLK, 128), 1) < HD
        prev_cols = _iota((2 * BLK, 2 * BLK), 1) < BLK

        def rows_of(xb):
            return jnp.concatenate([xb[:, 0:1], xb[:, HD:HD + 1]], axis=0)

        for p, (_, r) in enumerate(PATTERNS):
            def blk(n, carry, p=p, r=r):
                cur, prev, j = _block_rows(t, r, n)
                q2, do2 = _stack_heads(q_ref[cur, :], h0), _stack_heads(do_ref[cur, :], h0)
                k2 = jnp.concatenate([k_ref[prev, :], k_ref[cur, :]], axis=0).astype(MXU)
                v2 = jnp.concatenate([v_ref[prev, :], v_ref[cur, :]], axis=0).astype(MXU)
                s = _dot_nt(q2, k2) + bias_ref[p] + jnp.where(prev_cols & (j == 0), NEG, 0.0)
                prob = jnp.exp(s - rows_of(lse_ref[cur, :]))
                ds = prob * (_dot_nt(do2, v2) - rows_of(dd_ref[cur, :]))
                ds_ref[p] += ds
                dq2 = _dot(ds, k2)
                dk2 = _dot_tn(ds, q2)
                dv2 = _dot_tn(prob, do2)
                dq_ref[cur, :] += jnp.where(h0, dq2[:BLK], dq2[BLK:])
                dk_ref[prev, :] += dk2[:BLK]
                dv_ref[prev, :] += dv2[:BLK]
                dk_ref[cur, :] += dk2[BLK:]
                dv_ref[cur, :] += dv2[BLK:]
                return carry

            lax.fori_loop(0, t // BLK, blk, 0, unroll=8)

        ri, ci = _iota((8, 128), 0), _iota((8, 128), 1)
        upd = jnp.zeros((8, 128), F32)
        for p in range(len(PATTERNS)):
            bk = bk_ref[p]
            for hh in range(2):
                dsum = ds_ref[p, hh * BLK:(hh + 1) * BLK, :]
                for b in range(N_BUCKETS):
                    val = jnp.sum(jnp.where(bk == b, dsum, 0.0))
                    upd = upd + jnp.where((ri == 2 * pair + hh) & (ci == b), val, 0.0)
        db_ref[...] += upd

        for raw, d_ref, w_ref, dw_ref in ((qraw_ref, dq_ref, wq_ref, dwq_ref), (kraw_ref, dk_ref, wk_ref, dwk_ref)):
            acc = jnp.zeros((1, 128), F32)
            for c in range(t // rows):
                sl = slice(c * rows, (c + 1) * rows)
                xv, dyv = raw[sl, :], d_ref[sl, :]
                rs = _pair_rstd(xv, g2_ref)
                xh = xv * rs
                gg = dyv * w_ref[...]
                mean = _hdot(gg * xh, g2_ref[...], "b") * (1.0 / HD)
                d_ref[sl, :] = rs * (gg - xh * mean)
                acc = acc + jnp.sum(dyv * xh, axis=0, keepdims=True)
            dw_ref[...] = jnp.broadcast_to(acc, (8, 128))

    col = lambda off: pl.BlockSpec((t, 128), lambda i, off=off: (0, off + i))
    acc8 = pl.BlockSpec((8, 128), lambda i: (0, i))
    return pl.pallas_call(
        body, name="att_bwd", grid=(ATT_HEADS // 2,),
        in_specs=[BIAS_SPEC, pl.BlockSpec(bk.shape, lambda i: (0, 0, 0)),
                  col(0), col(4), col(8), col(0), col(0), col(0), PAIR_ROW_SPEC, PAIR_ROW_SPEC,
                  pl.BlockSpec((128, 128), lambda i: (0, 0))],
        out_specs=[col(0), col(0), col(0), pl.BlockSpec((8, 128), lambda i: (0, 0)), acc8, acc8],
        out_shape=[jax.ShapeDtypeStruct((t, 512), F32)] * 3 + [jax.ShapeDtypeStruct((8, 128), F32)]
                  + [jax.ShapeDtypeStruct((8, 512), F32)] * 2,
        scratch_shapes=[pltpu.VMEM((len(PATTERNS), 2 * BLK, 2 * BLK), F32)] + [pltpu.VMEM((t, 128), F32)] * 2,
        compiler_params=_params(("arbitrary",)),
    )(bias, bk, qkv, qkv, qkv, do, lse, dd, wq, wk, g2)


def _dn_scan_bwd(qkv, ba, do, sh, th, arow, dtb):
    t = qkv.shape[0]
    n = t // CHUNK
    c = CHUNK
    cps = SCAN_CHUNKS

    def body(qkv_ref, ba_ref, do_ref, sh_ref, th_ref, arow_ref, dtb_ref, dqkv_ref, dba_ref, ds_ref):
        @pl.when(pl.program_id(0) == 0)
        def _():
            ds_ref[...] = jnp.zeros_like(ds_ref)

        hs = range(DN_HEADS)
        chains = [(j, h) for j in range(cps) for h in hs]
        lane = _iota((c, 128), 1)
        row = _iota((c, 1), 0)
        ms = {}
        for j in range(cps):
            rows_j = slice(j * c, (j + 1) * c)
            mj, ri, ci = _chunk_common(qkv_ref[rows_j, :], ba_ref[rows_j, :], arow_ref[...], dtb_ref[...])
            for h in hs:
                ms[j, h] = mj[h]
        q, k, v = ({x: ms[x][nm] for x in chains} for nm in ("q", "k", "v"))
        beta, decay = ({x: ms[x][nm] for x in chains} for nm in ("beta", "decay"))
        eg, egl, etail = ({x: ms[x][nm] for x in chains} for nm in ("eg", "egl", "etail"))
        s = {x: sh_ref[x[0], x[1]] for x in chains}
        tinv = {x: th_ref[x[0], x[1]] for x in chains}
        d_o = {(j, h): do_ref[j * c:(j + 1) * c, h * DK:(h + 1) * DK] for j, h in chains}
        kb = {x: k[x] * beta[x] for x in chains}
        vb = {x: v[x] * beta[x] for x in chains}
        kbg = {x: kb[x] * eg[x] for x in chains}
        amat = {x: jnp.where(ri > ci, _dot_nt(kb[x], k[x]) * decay[x], 0.0) for x in chains}
        attn = {x: jnp.where(ri >= ci, _dot_nt(q[x], k[x]) * decay[x], 0.0) for x in chains}
        uw = {x: _hdot(tinv[x], jnp.concatenate([vb[x], kbg[x]], axis=1)) for x in chains}
        u = {x: uw[x][:, :DK] for x in chains}
        w = {x: uw[x][:, DK:] for x in chains}
        v_new = {x: u[x] - _dot(w[x], s[x]) for x in chains}
        q_dec = {x: q[x] * eg[x] for x in chains}
        k_tail = {x: k[x] * etail[x] for x in chains}
        d_attn = {x: jnp.where(ri >= ci, _dot_nt(d_o[x], v_new[x]), 0.0) for x in chains}
        d_qdec = {x: _dot_nt(d_o[x], s[x]) for x in chains}
        from_o = {x: _dot_tn(attn[x], d_o[x]) for x in chains}
        to_state = {x: _dot_tn(q_dec[x], d_o[x]) for x in chains}

        d_s, d_vnew = {}, {}
        cur = [ds_ref[h] for h in hs]
        for j in reversed(range(cps)):
            for h in hs:
                d_s[j, h] = cur[h]
                d_vnew[j, h] = from_o[j, h] + _dot(k_tail[j, h], cur[h])
            cur = [to_state[j, h] + cur[h] * egl[j, h] - _dot_tn(w[j, h], d_vnew[j, h]) for h in hs]
        for h in hs:
            ds_ref[h] = cur[h]

        d_ktail = {x: _dot_nt(v_new[x], d_s[x]) for x in chains}
        d_gl = {x: jnp.sum(s[x] * d_s[x]) * egl[x] for x in chains}
        d_w = {x: -_dot_nt(d_vnew[x], s[x]) for x in chains}
        d_both = {x: _hdot_tn(tinv[x], jnp.concatenate([d_vnew[x], d_w[x]], axis=1)) for x in chains}
        d_vb = {x: d_both[x][:, :DK] for x in chains}
        d_kbg = {x: d_both[x][:, DK:] for x in chains}
        d_a = {x: -jnp.where(ri > ci, _hdot_nt(d_both[x], uw[x]), 0.0) for x in chains}
        d_qk = {x: d_attn[x] * decay[x] for x in chains}
        d_kk = {x: d_a[x] * decay[x] for x in chains}
        d_kb = {x: _dot(d_kk[x], k[x]) + d_kbg[x] * eg[x] for x in chains}
        d_q = {x: _dot(d_qk[x], k[x]) + d_qdec[x] * eg[x] for x in chains}
        d_k = {x: _dot_tn(d_qk[x], q[x]) + _dot_tn(d_kk[x], kb[x]) + d_ktail[x] * etail[x] + d_kb[x] * beta[x]
               for x in chains}
        d_beta = {x: jnp.sum(d_kb[x] * k[x] + d_vb[x] * v[x], axis=1, keepdims=True) for x in chains}
        mm = {x: d_a[x] * amat[x] + d_attn[x] * attn[x] for x in chains}
        for j in range(cps):
            rows_j = slice(j * c, (j + 1) * c)
            rows = jnp.zeros((c, c), F32)
            for h in hs:
                rows = rows + jnp.where(ri == h, jnp.sum(mm[j, h], axis=0, keepdims=True), 0.0)
            cols_t = jnp.concatenate([rows, jnp.zeros((c, c), F32)], axis=1).T[:c, :]
            d_gc_all = jnp.zeros((c, 128), F32)
            for h in hs:
                x = (j, h)
                tail_term = jnp.sum(d_ktail[x] * k_tail[x], axis=1, keepdims=True)
                d_gc = (jnp.sum(mm[x], axis=1, keepdims=True) - _lane_col(cols_t, h)
                        + jnp.sum(d_qdec[x] * q_dec[x] + d_kbg[x] * kbg[x], axis=1, keepdims=True) - tail_term)
                d_gc = d_gc + jnp.where(row == c - 1, jnp.sum(tail_term) + d_gl[x], 0.0)
                d_gc_all = d_gc_all + jnp.where(lane == DN_HEADS + h, d_gc, 0.0)
            d_g_all = _hdot((ri <= ci).astype(F32), d_gc_all, "a")
            dba = jnp.zeros((c, 128), F32)
            for h in hs:
                x = (j, h)
                d_g = _lane_col(d_g_all, DN_HEADS + h)
                d_braw = d_beta[x] * beta[x] * (1.0 - beta[x])
                d_araw = d_g * ms[x]["a_h"] * _sigmoid(ms[x]["a_raw"] + ms[x]["dt_h"])
                dba = dba + jnp.where(lane == h, d_braw, 0.0) + jnp.where(lane == DN_HEADS + h, d_araw, 0.0) \
                    + jnp.where(lane == 2 * DN_HEADS + h, d_g * ms[x]["g"], 0.0)
                dqkv_ref[rows_j, h * DK:(h + 1) * DK] = d_q[x]
                dqkv_ref[rows_j, D_DN + h * DK:D_DN + (h + 1) * DK] = d_k[x]
                dqkv_ref[rows_j, 2 * D_DN + h * DK:2 * D_DN + (h + 1) * DK] = d_vb[x] * beta[x]
            dba_ref[rows_j, :] = dba

    nsteps = n // cps
    rev = lambda w_: pl.BlockSpec((cps * c, w_), lambda i: (nsteps - 1 - i, 0))
    one = pl.BlockSpec((1, 128), lambda i: (0, 0))
    return pl.pallas_call(
        body, name="dn_scan_bwd", grid=(nsteps,),
        in_specs=[rev(1536), rev(128), rev(512),
                  pl.BlockSpec((cps, DN_HEADS, DK, DK), lambda i: (nsteps - 1 - i, 0, 0, 0)),
                  pl.BlockSpec((cps, DN_HEADS, c, c), lambda i: (nsteps - 1 - i, 0, 0, 0)), one, one],
        out_specs=[rev(1536), rev(128)],
        out_shape=[jax.ShapeDtypeStruct((t, 1536), F32), jax.ShapeDtypeStruct((t, 128), F32)],
        scratch_shapes=[pltpu.VMEM((DN_HEADS, DK, DK), F32)],
        compiler_params=_params(("arbitrary",)),
    )(qkv, ba, do, sh, th, arow, dtb)


def _dn_prep_bwd(pdn, cw, dact):
    t = pdn.shape[0]
    nchunk = t // CONV_ROWS

    def body(u_ref, w_ref, d_ref, du_ref, dw_ref, dy_ref):
        j = pl.program_id(0)
        dy_ref[t:t + 8, :] = jnp.zeros((8, 128), F32)
        dw = [jnp.zeros((1, 128), F32) for _ in range(4)]
        for c in range(nchunk):
            sl = slice(c * CONV_ROWS, (c + 1) * CONV_ROWS)
            taps, y = _conv_taps(u_ref, c, w_ref)
            a, da_dy = _silu_and_grad(y)
            dout = d_ref[sl, :]
            rs = lax.rsqrt(jnp.sum(a * a, axis=1, keepdims=True) + EPS)
            f = jnp.where(j < 8, rs, 1.0) * jnp.where(j < 4, DK ** -0.5, 1.0)
            corr = jnp.where(j < 8, f * rs * rs * jnp.sum(dout * a, axis=1, keepdims=True), 0.0)
            dy = (f * dout - corr * a) * da_dy
            dy_ref[sl, :] = dy
            for k_ in range(4):
                dw[3 - k_] = dw[3 - k_] + jnp.sum(taps[k_] * dy, axis=0, keepdims=True)
        for i in range(4):
            dw_ref[i:i + 1, :] = dw[i]
        for c in range(nchunk):
            r0 = c * CONV_ROWS
            ext = dy_ref[r0:r0 + CONV_ROWS + 8, :]
            du = ext[:CONV_ROWS, :] * w_ref[3:4, :]
            for k_ in (1, 2, 3):
                du = du + pltpu.roll(ext, CONV_ROWS + 8 - k_, 0)[:CONV_ROWS, :] * w_ref[3 - k_:4 - k_, :]
            du_ref[r0:r0 + CONV_ROWS, :] = du

    return pl.pallas_call(
        body, name="dn_prep_bwd", grid=(12,),
        in_specs=[pl.BlockSpec((t, 128), lambda j: (0, j)), pl.BlockSpec((4, 128), lambda j: (0, j)),
                  pl.BlockSpec((t, 128), lambda j: (0, j))],
        out_specs=[pl.BlockSpec((t, 128), lambda j: (0, j)), pl.BlockSpec((4, 128), lambda j: (0, j))],
        out_shape=[jax.ShapeDtypeStruct((t, 1536), F32), jax.ShapeDtypeStruct((4, 1536), F32)],
        scratch_shapes=[pltpu.VMEM((t + 8, 128), F32)],
        compiler_params=_params(("arbitrary",)),
    )(pdn, cw, dact)


SECTIONS = (("dn", 0, 1536), ("z", 1536, 512), ("q", 2048, 512), ("k", 2560, 512), ("v", 3072, 512),
            ("gate", 3584, 512), ("ba", 4096, 128))


def _inproj_bwd(x, nw, wt, dy, dsecs, partials):
    t = x.shape[0]
    tm = 256
    npart = len(partials)
    nsteps = t // tm

    nsec = len(SECTIONS)

    def body(x_ref, nw_ref, w_ref, dy_ref, *rest):
        sec_refs, rest = rest[:nsec], rest[nsec:]
        part_refs, (gx_ref, dnw_ref, cs_ref) = rest[:npart], rest[npart:npart + 3]
        got_refs, (send, recv, loc) = rest[npart + 3:2 * npart + 3], rest[2 * npart + 3:]
        starts, waits = _chip_swap_copies(part_refs, got_refs, send, recv, loc)

        @pl.when(pl.program_id(0) == 0)
        def _():
            for start in starts:
                start()
            dnw_ref[...] = jnp.zeros_like(dnw_ref)
            cs_ref[...] = jnp.zeros_like(cs_ref)

        @pl.when(pl.program_id(0) == nsteps - 1)
        def _():
            for wait in waits:
                wait()

        dh = jnp.zeros((tm, D_MODEL), F32)
        for ref, (_, lo, width) in zip(sec_refs, SECTIONS):
            dh = dh + jnp.dot(ref[...].astype(MXU), w_ref[lo:lo + width, :], preferred_element_type=F32)
        xv = x_ref[...]
        rstd = lax.rsqrt(jnp.mean(xv * xv, axis=-1, keepdims=True) + EPS)
        xh = xv * rstd
        gg = dh * nw_ref[...]
        gx_ref[...] = rstd * (gg - xh * jnp.mean(gg * xh, axis=-1, keepdims=True)) + dy_ref[...]
        dnw_ref[...] += jnp.broadcast_to(jnp.sum(dh * xh, axis=0, keepdims=True), (8, D_MODEL))
        cs_ref[...] += jnp.broadcast_to(jnp.sum(sec_refs[nsec - 1][...], axis=0, keepdims=True), (8, 128))

    row = lambda n: pl.BlockSpec((tm, n), lambda i: (i, 0))
    full = lambda a: pl.BlockSpec(a.shape, lambda i: (0,) * a.ndim)
    res = pl.pallas_call(
        body, name="inproj_bwd", grid=(nsteps,),
        in_specs=[row(D_MODEL), full(nw), full(wt), row(D_MODEL)] + [row(width) for _, _, width in SECTIONS]
                 + [ANY_SPEC] * npart,
        out_specs=[row(D_MODEL), pl.BlockSpec((8, D_MODEL), lambda i: (0, 0)), pl.BlockSpec((8, 128), lambda i: (0, 0))]
                  + [ANY_SPEC] * npart,
        out_shape=[jax.ShapeDtypeStruct((t, D_MODEL), F32), jax.ShapeDtypeStruct((8, D_MODEL), F32),
                   jax.ShapeDtypeStruct((8, 128), F32)] + [jax.ShapeDtypeStruct(p.shape, p.dtype) for p in partials],
        scratch_shapes=[pltpu.SemaphoreType.DMA((npart, 3)), pltpu.SemaphoreType.DMA((npart, 3)),
                        pltpu.SemaphoreType.DMA((npart,))],
        compiler_params=_params(("arbitrary",)),
    )(x, nw, wt, dy, *dsecs, *partials)
    return res[0], res[1], res[2], res[3:]


def _adamw_sum(w, gs, m, v, name):
    r, c = w.shape
    nsum = gs.shape[0]
    tr = r if r <= 256 else 256
    c1 = 1.0 - ADAM_B1 ** ADAM_STEP
    c2 = 1.0 - ADAM_B2 ** ADAM_STEP

    def body(w_ref, g_ref, m_ref, v_ref, go_ref, d_ref, mo_ref, vo_ref):
        g = g_ref[0].astype(F32)
        for s in range(1, nsum):
            g = g + g_ref[s].astype(F32)
        mn = ADAM_B1 * m_ref[...] + (1.0 - ADAM_B1) * g
        vn = ADAM_B2 * v_ref[...] + (1.0 - ADAM_B2) * (g * g)
        go_ref[...] = g
        mo_ref[...] = mn
        vo_ref[...] = vn
        d_ref[...] = -ADAM_LR * ((mn / c1) / (jnp.sqrt(vn / c2) + ADAM_EPS) + ADAM_WD * w_ref[...])

    blk = pl.BlockSpec((tr, c), lambda i: (i, 0))
    return pl.pallas_call(
        body, name=name, grid=(r // tr,),
        in_specs=[blk, pl.BlockSpec((nsum, tr, c), lambda i: (0, i, 0)), blk, blk],
        out_specs=[blk] * 4, out_shape=[jax.ShapeDtypeStruct((r, c), F32)] * 4,
        compiler_params=_params(("arbitrary",)),
    )(w, gs, m, v)


def _local_step(x, target, h, ht, w_sect, conv_w, a_log, dt_bias, dn_norm_w, q_norm_w, k_norm_w, rel_bias, wout_shard):
    lane = np.arange(128)
    arow = jnp.zeros((1, 128), F32).at[0, DN_HEADS:2 * DN_HEADS].set(-jnp.exp(a_log[0]))
    dtb = jnp.zeros((1, 128), F32).at[0, DN_HEADS:2 * DN_HEADS].set(dt_bias[0])
    g_np, gt_np = _group_mats()
    g, gt = jnp.asarray(g_np), jnp.asarray(gt_np)
    g2 = jnp.asarray(np.kron(np.eye(2, dtype=np.float32), np.ones((HD, HD), np.float32)))
    bk = jnp.asarray(_bucket_tables())
    wq = jnp.tile(q_norm_w, (1, ATT_HEADS)) * (HD ** -0.5)
    wk = jnp.tile(k_norm_w, (1, ATT_HEADS))
    del lane

    pdn, qkv_dn, z, patt, gate, ba, wout8 = _inproj(h, w_sect, conv_w, wout_shard)
    w_out = wout8.reshape(D_MODEL, D_MODEL)
    oraw, ydn, sh, th = _dn_scan_fwd(qkv_dn, ba, z, arow, dtb, dn_norm_w)
    bias = _bias_tables(rel_bias, bk)
    oatt, yatt, lse = _att_fwd(patt, gate, bias, wq, wk, g2)
    dy, mix_t, loss8 = _outproj_loss(x, ydn, yatt, w_out, target)

    do_dn, dz, do_att, dgate, dd, ddnw = _outproj_bwd(dy, w_out.T, oraw, z, dn_norm_w, oatt, gate, g, gt)
    d_wout = _grad_matmul(mix_t, dy, "dw_out")
    dq, dk, dv, drb, dwq8, dwk8 = _att_bwd(patt, do_att, lse, dd, bias, bk, wq, wk, g2)
    dqkv_dn, dba = _dn_scan_bwd(qkv_dn, ba, do_dn, sh, th, arow, dtb)
    dpdn, d_conv = _dn_prep_bwd(pdn, conv_w, dqkv_dn)
    dsecs = (dpdn, dz, dq, dk, dv, dgate, dba)
    dw_sections = [_grad_matmul(ht, d_, "dw_in_" + nm) for d_, (nm, _, _) in zip(dsecs, SECTIONS)]
    return dict(w_in_sections=dw_sections, conv_w=d_conv, w_out=d_wout, dy=dy, dsecs=dsecs,
                small_parts=(loss8, ddnw, dwq8, dwk8, drb))


def _finish_step(x, norm_w, w_sect_t, gr, partials):
    grad_x, dnw8, cs8, got = _inproj_bwd(x, norm_w, w_sect_t, gr["dy"], gr["dsecs"], partials)
    return grad_x, _pack_small_grads(dnw8, cs8, *gr["small_parts"]), got


SMALL_ROWS = 24
SMALL_AT = dict(a_log=(slice(8, 9), slice(0, 4)), dt_bias=(slice(9, 10), slice(0, 4)),
                dn_norm_w=(slice(10, 11), slice(0, 128)), q_norm_w=(slice(11, 12), slice(0, HD)),
                k_norm_w=(slice(12, 13), slice(0, HD)), rel_bias=(slice(16, 24), slice(0, N_BUCKETS)))
SMALL_NAMES = ("norm_w", "a_log", "dt_bias", "dn_norm_w", "q_norm_w", "k_norm_w", "rel_bias")


LOSS_ROW = 13


def _pack_small_grads(dnw8, cs8, loss8, ddnw8, dwq8, dwk8, drb):
    def body(dnw_ref, cs_ref, loss_ref, ddnw_ref, dwq_ref, dwk_ref, drb_ref, o_ref):
        lane = _iota((8, 128), 1)
        o_ref[...] = jnp.zeros_like(o_ref)
        o_ref[LOSS_ROW:LOSS_ROW + 1, :] = jnp.where(lane == 0, loss_ref[...], 0.0)[0:1, :]
        for k in range(D_MODEL // 128):
            o_ref[k:k + 1, :] = dnw_ref[0:1, k * 128:(k + 1) * 128]
        cs = cs_ref[...]
        o_ref[8:9, :] = jnp.where(lane < DN_HEADS, pltpu.roll(cs, 128 - 2 * DN_HEADS, 1), 0.0)[0:1, :]
        o_ref[9:10, :] = jnp.where(lane < DN_HEADS, pltpu.roll(cs, 128 - DN_HEADS, 1), 0.0)[0:1, :]
        o_ref[10:11, :] = ddnw_ref[0:1, :]
        for row, ref, scale in ((11, dwq_ref, HD ** -0.5), (12, dwk_ref, 1.0)):
            acc = ref[:, 0:128] + ref[:, 128:256] + ref[:, 256:384] + ref[:, 384:512]
            acc = (acc + pltpu.roll(acc, HD, 1)) * scale
            o_ref[row:row + 1, :] = jnp.where(lane < HD, acc, 0.0)[0:1, :]
        o_ref[16:24, :] = drb_ref[...]

    return pl.pallas_call(body, name="pack_small_grads", out_shape=jax.ShapeDtypeStruct((SMALL_ROWS, 128), F32),
                          )(dnw8, cs8, loss8, ddnw8, dwq8, dwk8, drb)


def _adam_math(w, g, m, v):
    c1 = 1.0 - ADAM_B1 ** ADAM_STEP
    c2 = 1.0 - ADAM_B2 ** ADAM_STEP
    mn = ADAM_B1 * m + (1.0 - ADAM_B1) * g
    vn = ADAM_B2 * v + (1.0 - ADAM_B2) * (g * g)
    return -ADAM_LR * ((mn / c1) / (jnp.sqrt(vn / c2) + ADAM_EPS) + ADAM_WD * w), mn, vn


def _adamw_small(gs, ws, ms, vs):
    n = len(SMALL_NAMES)

    def body(g_ref, *refs):
        w_refs, m_refs, v_refs = refs[:n], refs[n:2 * n], refs[2 * n:3 * n]
        outs, loss_ref = refs[3 * n:7 * n], refs[7 * n]
        loss = g_ref[0, LOSS_ROW:LOSS_ROW + 1, :]
        for s in range(1, gs.shape[0]):
            loss = loss + g_ref[s, LOSS_ROW:LOSS_ROW + 1, :]
        loss_ref[...] = loss

        def one(i, rows, lanes, at):
            g = g_ref[0, rows, lanes]
            for s in range(1, gs.shape[0]):
                g = g + g_ref[s, rows, lanes]
            d, mn, vn = _adam_math(w_refs[i][at], g, m_refs[i][at], v_refs[i][at])
            for kind, val in enumerate((g, d, mn, vn)):
                outs[kind * n + i][at] = val

        for k in range(D_MODEL // 128):
            one(0, slice(k, k + 1), slice(0, 128), (slice(0, 1), slice(k * 128, (k + 1) * 128)))
        for i, nm in enumerate(SMALL_NAMES[1:], start=1):
            rows, lanes = SMALL_AT[nm]
            one(i, rows, lanes, (slice(None), slice(None)))

    shapes = [jax.ShapeDtypeStruct(w.shape, F32) for w in ws]
    res = pl.pallas_call(body, name="adamw_small",
                         out_shape=shapes * 4 + [jax.ShapeDtypeStruct((1, 128), F32)])(gs, *ws, *ms, *vs)
    return [res[k * n:(k + 1) * n] for k in range(4)], res[4 * n]


def kernel(x, norm_w, w_in, conv_w, a_log, dt_bias, dn_norm_w, q_norm_w, k_norm_w, rel_bias, w_out, loss_target, m_norm_w, m_w_in, m_conv_w, m_a_log, m_dt_bias, m_dn_norm_w, m_q_norm_w, m_k_norm_w, m_rel_bias, m_w_out, v_norm_w, v_w_in, v_conv_w, v_a_log, v_dt_bias, v_dn_norm_w, v_q_norm_w, v_k_norm_w, v_rel_bias, v_w_out):
    assert w_in.shape[2] == SHARD_COLS
    h, ht, (win8, conv8) = _norm_and_gather(x[0], norm_w, [w_in[0].astype(MXU), conv_w[0]])
    w_sect, w_sect_t = _build_w(win8)
    conv_full = conv8.transpose(1, 0, 2).reshape(4, 3 * D_DN)

    gr = _local_step(x[0], loss_target[0], h, ht, w_sect, conv_full, a_log, dt_bias, dn_norm_w, q_norm_w,
                     k_norm_w, rel_bias, w_out[0].astype(MXU))

    slabs = [_build_slabs(gr["w_in_sections"]),
             gr["w_out"].reshape(4, 2, D_MODEL // N_DEV, D_MODEL).transpose(1, 0, 2, 3),
             gr["conv_w"].reshape(4, 4, 2, 3 * D_DN // N_DEV).transpose(2, 1, 0, 3)]
    core = lax.axis_index("c").astype(jnp.int32).reshape(1)
    from_sibling = _swap_siblings(slabs)
    wires = (GRAD_WIRE, GRAD_WIRE, F32)
    partial = [_chip_sum(slabs[i], from_sibling[i], core, wires[i], "chip_sum_%d" % i) for i in range(3)]
    grad_x, small_pack, (r_win, r_wout, r_conv) = _finish_step(x[0], norm_w, w_sect_t, gr, partial)
    r_small = _share_small(small_pack)

    g_win, d_win, m_win, v_win = _adamw_sum(w_in[0], r_win, m_w_in[0], v_w_in[0], "adamw_w_in")
    g_wout, d_wout, m_wout, v_wout = _adamw_sum(w_out[0], r_wout, m_w_out[0], v_w_out[0], "adamw_w_out")
    g_conv, d_conv, m_conv, v_conv = _adamw_sum(conv_w[0], r_conv, m_conv_w[0], v_conv_w[0], "adamw_conv_w")
    small, loss_row = _adamw_small(r_small,
                                   (norm_w, a_log, dt_bias, dn_norm_w, q_norm_w, k_norm_w, rel_bias),
                                   (m_norm_w, m_a_log, m_dt_bias, m_dn_norm_w, m_q_norm_w, m_k_norm_w, m_rel_bias),
                                   (v_norm_w, v_a_log, v_dt_bias, v_dn_norm_w, v_q_norm_w, v_k_norm_w, v_rel_bias))

    loss = loss_row[0, 0]
    names = ("norm_w", "w_in", "conv_w", "a_log", "dt_bias", "dn_norm_w", "q_norm_w", "k_norm_w", "rel_bias", "w_out")
    big = dict(w_in=(g_win, d_win, m_win, v_win), conv_w=(g_conv, d_conv, m_conv, v_conv),
               w_out=(g_wout, d_wout, m_wout, v_wout))
    outs = [loss, grad_x[None]]
    for kind in range(4):
        for nm in names:
            outs.append(big[nm][kind][None] if nm in big else small[kind][SMALL_NAMES.index(nm)])
    return tuple(outs)
```

```python
import math

import numpy as np
import jax
import jax.numpy as jnp
from jax import lax
from jax.experimental import pallas as pl
from jax.experimental.pallas import tpu as pltpu

F32 = jnp.float32
MXU = jnp.bfloat16
GRAD_WIRE = jnp.bfloat16

D_MODEL = 1024
D_DN = 512
DN_HEADS = 4
DK = 128
CHUNK = 64
D_ATT = 512
ATT_HEADS = 8
HD = 64
PATTERNS = ((128, 1), (512, 4), (2048, 16))
BLK = 128
N_BUCKETS = 32
MAX_DISTANCE = 2048
EPS = 1e-6
W_COLS = 4224
N_DEV = 8
AXES = ("x", "y", "c")

ADAM_LR = 0.001
ADAM_B1 = 0.9
ADAM_B2 = 0.999
ADAM_EPS = 1e-08
ADAM_WD = 0.01
ADAM_STEP = 10

VMEM_LIMIT = 56 * 1024 * 1024
NEG = -1e30


def _dot(a, b):
    return jnp.dot(a.astype(MXU), b.astype(MXU), preferred_element_type=F32)


def _dot_nt(a, b):
    return lax.dot_general(a.astype(MXU), b.astype(MXU), (((1,), (1,)), ((), ())), preferred_element_type=F32)


def _dot_tn(a, b):
    return lax.dot_general(a.astype(MXU), b.astype(MXU), (((0,), (0,)), ((), ())), preferred_element_type=F32)


def _split(a):
    hi = a.astype(jnp.bfloat16)
    return hi, (a - hi.astype(F32)).astype(jnp.bfloat16)


def _dot_split(a, b, dims, exact):
    dg = lambda u, v: lax.dot_general(u, v, (dims, ((), ())), preferred_element_type=F32)
    if exact == "b":
        ah, al = _split(a)
        bh = b.astype(jnp.bfloat16)
        return dg(ah, bh) + dg(al, bh)
    if exact == "a":
        bh, bm = _split(b)
        bl = (b - bh.astype(F32) - bm.astype(F32)).astype(jnp.bfloat16)
        ah = a.astype(jnp.bfloat16)
        return dg(ah, bh) + (dg(ah, bm) + dg(ah, bl))
    ah, al = _split(a)
    bh, bl = _split(b)
    return dg(ah, bh) + (dg(ah, bl) + dg(al, bh))


def _wy_inverses(amat, eye):
    tinv = {x: eye - amat[x] for x in amat}
    pw = amat
    for _ in range(5):
        pw = {x: _hdot(pw[x], pw[x]) for x in amat}
        tinv = {x: tinv[x] + _hdot(tinv[x], pw[x]) for x in amat}
    return tinv


def _hdot(a, b, exact=None):
    return _dot_split(a, b, ((1,), (0,)), exact)


def _hdot_nt(a, b, exact=None):
    return _dot_split(a, b, ((1,), (1,)), exact)


def _hdot_tn(a, b, exact=None):
    return _dot_split(a, b, ((0,), (0,)), exact)


def _sigmoid(x):
    return 1.0 / (1.0 + jnp.exp(-x))


def _silu(x):
    return x * _sigmoid(x)


def _silu_and_grad(x):
    s = _sigmoid(x)
    return x * s, s * (1.0 + x * (1.0 - s))


def _softplus(x):
    return jnp.maximum(x, 0.0) + jnp.log(1.0 + jnp.exp(-jnp.abs(x)))


def _iota(shape, dim):
    return lax.broadcasted_iota(jnp.int32, shape, dim)


def _lane_col(x, k):
    return jnp.sum(jnp.where(_iota(x.shape, 1) == k, x, 0.0), axis=1, keepdims=True)


def _params(sem=None):
    return pltpu.CompilerParams(dimension_semantics=sem, vmem_limit_bytes=VMEM_LIMIT)


def _t5_bucket(dist):
    max_exact = N_BUCKETS // 2
    d = np.maximum(dist, 1).astype(np.float64)
    large = max_exact + (np.log(d / max_exact) / math.log(MAX_DISTANCE / max_exact)
                         * (N_BUCKETS - max_exact)).astype(np.int32)
    large = np.minimum(large, N_BUCKETS - 1)
    return np.where(dist < max_exact, dist, large).astype(np.int32)


def _bucket_tables():
    qi = np.arange(BLK)[:, None]
    kj = np.arange(2 * BLK)[None, :]
    step = qi - kj + BLK
    band = (step >= 0) & (step <= BLK)
    out = []
    for _, r in PATTERNS:
        b = _t5_bucket(np.clip(step, 0, None) * r)
        out.append(np.where(band, b, -1))
    return np.stack(out).astype(np.int32)


def _group_mats():
    g = np.zeros((D_ATT, 128), np.float32)
    for h in range(ATT_HEADS):
        g[h * HD:(h + 1) * HD, h] = 1.0
    return g, np.ascontiguousarray(g.T)


CHIP_FLIPS = ((1, 0), (0, 1), (1, 1))
ANY_SPEC = pl.BlockSpec(memory_space=pl.ANY)
MESH_ID = pl.DeviceIdType.MESH


def _other_chips():
    x, y = lax.axis_index("x"), lax.axis_index("y")
    return [((1 - x if fx else x), (1 - y if fy else y)) for fx, fy in CHIP_FLIPS]


def _gather_plan(ins, outs, send, recv, loc):
    n = len(ins)
    x, y, c = (lax.axis_index(a) for a in AXES)
    sib = (x, y, 1 - c)
    chips = _other_chips()
    lin = lambda px, py, pc: 4 * px + 2 * py + pc

    def copy(a, k, block, to, src=None):
        slot = outs[a].at[lin(*block)]
        return pltpu.make_async_remote_copy(src_ref=slot if src is None else src, dst_ref=slot,
                                            send_sem=send.at[a, k], recv_sem=recv.at[a, k],
                                            device_id=to, device_id_type=MESH_ID)

    mine = [pltpu.make_async_copy(ins[a], outs[a].at[lin(x, y, c)], loc.at[a]) for a in range(n)]
    firsts = []
    for a in range(n):
        firsts.append(copy(a, 0, (x, y, c), sib, src=ins[a]))
        firsts += [copy(a, 1 + j, (x, y, c), (*chip, c), src=ins[a]) for j, chip in enumerate(chips)]

    def begin():
        for cp in mine + firsts:
            cp.start()

    def finish():
        passed = []
        for j, chip in enumerate(chips):
            for a in range(n):
                copy(a, 1 + j, (*chip, c), (x, y, c)).wait_recv()
                fw = copy(a, 4 + j, (*chip, c), sib)
                fw.start()
                passed.append(fw)
        for a in range(n):
            copy(a, 0, sib, (x, y, c)).wait_recv()
            for j, chip in enumerate(chips):
                copy(a, 4 + j, (*chip, 1 - c), (x, y, c)).wait_recv()
        for cp in firsts + passed:
            cp.wait_send()
        for cp in mine:
            cp.wait()

    return begin, finish


GATHER_SEMS = lambda n: [pltpu.SemaphoreType.DMA((n, 7)), pltpu.SemaphoreType.DMA((n, 7)), pltpu.SemaphoreType.DMA((n,))]


def _swap_siblings(arrs):
    n = len(arrs)

    def body(*refs):
        ins, outs = refs[:n], refs[n:2 * n]
        send, recv = refs[2 * n:]
        x, y, c = (lax.axis_index(a) for a in AXES)
        cps = [pltpu.make_async_remote_copy(src_ref=ins[a].at[1 - c], dst_ref=outs[a], send_sem=send.at[a],
                                            recv_sem=recv.at[a], device_id=(x, y, 1 - c), device_id_type=MESH_ID)
               for a in range(n)]
        for cp in cps:
            cp.start()
        for cp in cps:
            cp.wait()

    return pl.pallas_call(
        body, name="swap_siblings", out_shape=[jax.ShapeDtypeStruct(a.shape[1:], a.dtype) for a in arrs],
        in_specs=[ANY_SPEC] * n, out_specs=[ANY_SPEC] * n,
        scratch_shapes=[pltpu.SemaphoreType.DMA((n,)), pltpu.SemaphoreType.DMA((n,))],
    )(*arrs)


def _chip_sum(mine2, theirs, core, wire, name):
    _, nchip, r, cdim = mine2.shape
    tr = r if r <= 256 else 256

    def body(core_ref, a_ref, b_ref, o_ref):
        del core_ref
        o_ref[...] = (a_ref[...].astype(F32) + b_ref[...].astype(F32)).astype(wire)

    grid_spec = pltpu.PrefetchScalarGridSpec(
        num_scalar_prefetch=1, grid=(nchip, r // tr),
        in_specs=[pl.BlockSpec((None, None, tr, cdim), lambda j, i, cr: (cr[0], j, i, 0)),
                  pl.BlockSpec((None, tr, cdim), lambda j, i, cr: (j, i, 0))],
        out_specs=pl.BlockSpec((None, tr, cdim), lambda j, i, cr: (j, i, 0)))
    return pl.pallas_call(
        body, name=name, grid_spec=grid_spec, out_shape=jax.ShapeDtypeStruct((nchip, r, cdim), wire),
        compiler_params=_params(("arbitrary", "arbitrary")),
    )(core, mine2, theirs)


def _chip_swap_copies(ins, outs, send, recv, loc):
    x, y, c = (lax.axis_index(a) for a in AXES)
    me = 2 * x + y
    starts, arrivals, drains = [], [], []
    for a in range(len(ins)):
        lc = pltpu.make_async_copy(ins[a].at[me], outs[a].at[me], loc.at[a])
        starts.append(lc.start)
        drains.append(lc.wait)
        for j, (px, py) in enumerate(_other_chips()):
            them = 2 * px + py
            cp = pltpu.make_async_remote_copy(src_ref=ins[a].at[them], dst_ref=outs[a].at[me], send_sem=send.at[a, j],
                                              recv_sem=recv.at[a, j], device_id=(px, py, c), device_id_type=MESH_ID)
            landing = pltpu.make_async_remote_copy(src_ref=ins[a].at[them], dst_ref=outs[a].at[them],
                                                   send_sem=send.at[a, j], recv_sem=recv.at[a, j],
                                                   device_id=(px, py, c), device_id_type=MESH_ID)
            starts.append(cp.start)
            arrivals.append(landing.wait_recv)
            drains.append(cp.wait_send)
    return starts, arrivals + drains


def _share_small(pack):
    def body(in_ref, out_ref, send, recv, loc):
        x, y, c = (lax.axis_index(a) for a in AXES)
        me = 4 * x + 2 * y + c
        lc = pltpu.make_async_copy(in_ref, out_ref.at[me], loc.at[0])
        lc.start()
        sends, arrivals = [], []
        for k in range(1, N_DEV):
            px = 1 - x if k & 4 else x
            py = 1 - y if k & 2 else y
            pc = 1 - c if k & 1 else c
            cp = pltpu.make_async_remote_copy(src_ref=in_ref, dst_ref=out_ref.at[me], send_sem=send.at[k - 1],
                                              recv_sem=recv.at[k - 1], device_id=(px, py, pc), device_id_type=MESH_ID)
            cp.start()
            sends.append(cp)
            arrivals.append(pltpu.make_async_remote_copy(src_ref=in_ref, dst_ref=out_ref.at[4 * px + 2 * py + pc],
                                                         send_sem=send.at[k - 1], recv_sem=recv.at[k - 1],
                                                         device_id=(px, py, pc), device_id_type=MESH_ID))
        for cp in arrivals:
            cp.wait_recv()
        for cp in sends:
            cp.wait_send()
        lc.wait()

    return pl.pallas_call(
        body, name="share_small", out_shape=jax.ShapeDtypeStruct((N_DEV,) + pack.shape, pack.dtype),
        in_specs=[ANY_SPEC], out_specs=ANY_SPEC,
        scratch_shapes=[pltpu.SemaphoreType.DMA((N_DEV - 1,)), pltpu.SemaphoreType.DMA((N_DEV - 1,)),
                        pltpu.SemaphoreType.DMA((1,))],
    )(pack)


W_PARTS = ((0, 0, 2048), (2048, 4096, 8), (2056, 2048, 2048))
SHARD_COLS = 513


def _pieces(lo, hi, parts):
    out = []
    for ref_start, tgt_start, width in parts:
        a, b = max(lo, ref_start), min(hi, ref_start + width)
        if a < b:
            out.append((a - lo, tgt_start + a - ref_start, b - a))
    return out


def _build_w(win8):
    tr = 256

    def body(in_ref, w_ref, wt_ref):
        w_ref[:, 4096:W_COLS] = jnp.zeros((tr, W_COLS - 4096), MXU)
        for p in range(N_DEV):
            for src, dst, width in _pieces(p * SHARD_COLS, (p + 1) * SHARD_COLS, W_PARTS):
                w_ref[:, dst:dst + width] = in_ref[p, :, src:src + width]
        for k in range(W_COLS // 128):
            wt_ref[k * 128:(k + 1) * 128, :] = w_ref[:, k * 128:(k + 1) * 128].astype(F32).T.astype(MXU)

    return pl.pallas_call(
        body, name="build_w", grid=(D_MODEL // tr,),
        in_specs=[pl.BlockSpec((N_DEV, tr, SHARD_COLS), lambda i: (0, i, 0))],
        out_specs=[pl.BlockSpec((tr, W_COLS), lambda i: (i, 0)), pl.BlockSpec((W_COLS, tr), lambda i: (0, i))],
        out_shape=[jax.ShapeDtypeStruct((D_MODEL, W_COLS), MXU), jax.ShapeDtypeStruct((W_COLS, D_MODEL), MXU)],
        compiler_params=_params(("arbitrary",)),
    )(win8)


def _build_slabs(secs):
    tr = 256
    parts = ((0, 0, 1536), (1536, 1, 512), (2048, 6, 8), (2056, 2, 512), (2568, 3, 512), (3080, 4, 512),
             (3592, 5, 512))

    def body(*refs):
        o_ref = refs[len(secs)]
        for p in range(N_DEV):
            lo, hi = p * SHARD_COLS, (p + 1) * SHARD_COLS
            for ref_start, idx, width in parts:
                a, b = max(lo, ref_start), min(hi, ref_start + width)
                if a < b:
                    o_ref[p % 2, p // 2, :, a - lo:b - lo] = refs[idx][:, a - ref_start:b - ref_start]

    return pl.pallas_call(
        body, name="build_slabs", grid=(D_MODEL // tr,),
        in_specs=[pl.BlockSpec((tr, s.shape[1]), lambda i: (i, 0)) for s in secs],
        out_specs=pl.BlockSpec((2, 4, tr, SHARD_COLS), lambda i: (0, 0, i, 0)),
        out_shape=jax.ShapeDtypeStruct((2, 4, D_MODEL, SHARD_COLS), secs[0].dtype),
        compiler_params=_params(("arbitrary",)),
    )(*secs)


def _norm_and_gather(x, nw, shards):
    t = x.shape[0]
    tm = 512
    nsteps = t // tm
    n = len(shards)

    def body(x_ref, nw_ref, *rest):
        ins, (h_ref, ht_ref), outs, sems = rest[:n], rest[n:n + 2], rest[n + 2:2 * n + 2], rest[2 * n + 2:]
        begin, finish = _gather_plan(ins, outs, *sems)

        @pl.when(pl.program_id(0) == 0)
        def _():
            begin()

        @pl.when(pl.program_id(0) == nsteps - 1)
        def _():
            finish()

        xv = x_ref[...]
        rstd = lax.rsqrt(jnp.mean(xv * xv, axis=-1, keepdims=True) + EPS)
        hf = xv * rstd * nw_ref[...]
        h_ref[...] = hf.astype(MXU)
        ht_ref[...] = hf.T.astype(MXU)

    res = pl.pallas_call(
        body, name="norm_and_gather", grid=(nsteps,),
        in_specs=[pl.BlockSpec((tm, D_MODEL), lambda i: (i, 0)), pl.BlockSpec(nw.shape, lambda i: (0, 0))]
                 + [ANY_SPEC] * n,
        out_specs=[pl.BlockSpec((tm, D_MODEL), lambda i: (i, 0)), pl.BlockSpec((D_MODEL, tm), lambda i: (0, i))]
                  + [ANY_SPEC] * n,
        out_shape=[jax.ShapeDtypeStruct((t, D_MODEL), MXU), jax.ShapeDtypeStruct((D_MODEL, t), MXU)]
                  + [jax.ShapeDtypeStruct((N_DEV,) + a.shape, a.dtype) for a in shards],
        scratch_shapes=GATHER_SEMS(n),
        compiler_params=_params(("arbitrary",)),
    )(x, nw, *shards)
    return res[0], res[1], res[2:]


def _inproj(h_all, w, cw, wout_shard):
    t = h_all.shape[0]
    tm = 256
    nsteps = t // tm

    def body(h_ref, w_ref, cw_ref, wo_ref, pdn_ref, qkv_ref, z_ref, patt_ref, gate_ref, ba_ref,
             wo8_ref, halo_ref, send, recv, loc):
        begin, finish = _gather_plan([wo_ref], [wo8_ref], send, recv, loc)

        @pl.when(pl.program_id(0) == 0)
        def _():
            begin()
            halo_ref[...] = jnp.zeros_like(halo_ref)

        @pl.when(pl.program_id(0) == nsteps - 1)
        def _():
            finish()

        h = h_ref[...]
        for ref, lo, hi in ((z_ref, 1536, 2048), (patt_ref, 2048, 3584), (gate_ref, 3584, 4096), (ba_ref, 4096, 4224)):
            ref[...] = jnp.dot(h, w_ref[:, lo:hi], preferred_element_type=F32)
        pdn = jnp.dot(h, w_ref[:, 0:3 * D_DN], preferred_element_type=F32)
        pdn_ref[...] = pdn
        _dn_prep_tile(pdn, halo_ref, cw_ref, qkv_ref)

    row = lambda n: pl.BlockSpec((tm, n), lambda i: (i, 0))
    full = lambda a: pl.BlockSpec(a.shape, lambda i: (0,) * a.ndim)
    return pl.pallas_call(
        body, name="inproj", grid=(nsteps,),
        in_specs=[row(D_MODEL), full(w), full(cw), ANY_SPEC],
        out_specs=[row(1536), row(1536), row(512), row(1536), row(512), row(128), ANY_SPEC],
        out_shape=[jax.ShapeDtypeStruct((t, n), F32) for n in (1536, 1536, 512, 1536, 512, 128)] +
                  [jax.ShapeDtypeStruct((N_DEV,) + wout_shard.shape, wout_shard.dtype)],
        scratch_shapes=[pltpu.VMEM((8, 3 * D_DN), F32)] + GATHER_SEMS(1),
        compiler_params=_params(("arbitrary",)),
    )(h_all, w, cw, wout_shard)


CONV_ROWS = 512


def _conv_taps(u_ref, c, w_ref):
    r0 = c * CONV_ROWS
    if c == 0:
        ext = jnp.concatenate([jnp.zeros((8, 128), F32), u_ref[0:CONV_ROWS, :]], axis=0)
    else:
        ext = u_ref[r0 - 8:r0 + CONV_ROWS, :]
    taps = [ext[8:, :]] + [pltpu.roll(ext, k, 0)[8:, :] for k in (1, 2, 3)]
    y = taps[0] * w_ref[3:4, :]
    for k in (1, 2, 3):
        y = y + taps[k] * w_ref[3 - k:4 - k, :]
    return taps, y


def _dn_prep_tile(pdn, halo_ref, cw_ref, out_ref):
    rows = pdn.shape[0]
    ext = jnp.concatenate([halo_ref[...], pdn], axis=0)
    halo_ref[...] = pdn[rows - 8:, :]
    for j in range(3 * D_DN // 128):
        cols = slice(j * 128, (j + 1) * 128)
        e = ext[:, cols]
        y = e[8:, :] * cw_ref[3:4, cols]
        for k in (1, 2, 3):
            y = y + pltpu.roll(e, k, 0)[8:, :] * cw_ref[3 - k:4 - k, cols]
        a = _silu(y)
        if j < 2 * DN_HEADS:
            a = a * lax.rsqrt(jnp.sum(a * a, axis=1, keepdims=True) + EPS)
        if j < DN_HEADS:
            a = a * DK ** -0.5
        out_ref[:, cols] = a


def _chunk_common(qkv, ba, arow, dtb):
    c = CHUNK
    ri, ci = _iota((c, c), 0), _iota((c, c), 1)
    lane = _iota((c, 128), 1)
    g_all = jnp.where((lane >= DN_HEADS) & (lane < 2 * DN_HEADS), arow * _softplus(ba + dtb), 0.0)
    gc_all = _hdot((ri >= ci).astype(F32), g_all, "a")
    gc_t = gc_all.T
    beta_all = _sigmoid(ba)
    out = []
    for h in range(DN_HEADS):
        gc = _lane_col(gc_all, DN_HEADS + h)
        gcr = gc_t[DN_HEADS + h:DN_HEADS + h + 1, :]
        gl = gc[c - 1:c, :]
        out.append(dict(
            q=qkv[:, h * DK:(h + 1) * DK], k=qkv[:, D_DN + h * DK:D_DN + (h + 1) * DK],
            v=qkv[:, 2 * D_DN + h * DK:2 * D_DN + (h + 1) * DK],
            beta=_lane_col(beta_all, h), g=_lane_col(g_all, DN_HEADS + h),
            a_raw=_lane_col(ba, DN_HEADS + h), a_h=_lane_col(arow, DN_HEADS + h), dt_h=_lane_col(dtb, DN_HEADS + h),
            decay=jnp.exp(jnp.where(ri >= ci, gc - gcr, NEG)), eg=jnp.exp(gc), egl=jnp.exp(gl), etail=jnp.exp(gl - gc)))
    return out, ri, ci


SCAN_CHUNKS = 8


def _dn_scan_fwd(qkv, ba, z, arow, dtb, dnw):
    t = qkv.shape[0]
    n = t // CHUNK
    c = CHUNK
    cps = SCAN_CHUNKS
    hs = range(DN_HEADS)
    chains = [(j, h) for j in range(cps) for h in hs]

    def body(qkv_ref, ba_ref, z_ref, arow_ref, dtb_ref, dnw_ref, o_ref, y_ref, sh_ref, th_ref, s_ref):
        @pl.when(pl.program_id(0) == 0)
        def _():
            s_ref[...] = jnp.zeros_like(s_ref)

        ms = {}
        for j in range(cps):
            rows = slice(j * c, (j + 1) * c)
            mj, ri, ci = _chunk_common(qkv_ref[rows, :], ba_ref[rows, :], arow_ref[...], dtb_ref[...])
            for h in hs:
                ms[j, h] = mj[h]
        kb = {x: ms[x]["k"] * ms[x]["beta"] for x in chains}
        amat = {x: jnp.where(ri > ci, _dot_nt(kb[x], ms[x]["k"]) * ms[x]["decay"], 0.0) for x in chains}
        attn = {x: jnp.where(ri >= ci, _dot_nt(ms[x]["q"], ms[x]["k"]) * ms[x]["decay"], 0.0) for x in chains}
        tinv = _wy_inverses(amat, (ri == ci).astype(F32))
        uw = {x: _hdot(tinv[x], jnp.concatenate([ms[x]["v"] * ms[x]["beta"], kb[x] * ms[x]["eg"]], axis=1))
              for x in chains}
        u = {x: uw[x][:, :DK] for x in chains}
        w = {x: uw[x][:, DK:] for x in chains}
        q_dec = {x: ms[x]["q"] * ms[x]["eg"] for x in chains}
        k_tail = {x: ms[x]["k"] * ms[x]["etail"] for x in chains}
        s = [s_ref[h] for h in hs]
        for j in range(cps):
            rows = slice(j * c, (j + 1) * c)
            v_new = [u[j, h] - _dot(w[j, h], s[h]) for h in hs]
            o = [_dot(q_dec[j, h], s[h]) + _dot(attn[j, h], v_new[h]) for h in hs]
            for h in hs:
                sh_ref[j, h] = s[h]
                th_ref[j, h] = tinv[j, h]
            s = [s[h] * ms[j, h]["egl"] + _dot_tn(k_tail[j, h], v_new[h]) for h in hs]
            for h in hs:
                cols = slice(h * DK, (h + 1) * DK)
                o_ref[rows, cols] = o[h]
                rs = lax.rsqrt(jnp.mean(o[h] * o[h], axis=1, keepdims=True) + EPS)
                y_ref[rows, cols] = o[h] * rs * dnw_ref[...] * _silu(z_ref[rows, cols])
        for h in hs:
            s_ref[h] = s[h]

    row = lambda w_: pl.BlockSpec((cps * c, w_), lambda i: (i, 0))
    one = pl.BlockSpec((1, 128), lambda i: (0, 0))
    return pl.pallas_call(
        body, name="dn_scan_fwd", grid=(n // cps,),
        in_specs=[row(1536), row(128), row(512), one, one, one],
        out_specs=[row(512), row(512), pl.BlockSpec((cps, DN_HEADS, DK, DK), lambda i: (i, 0, 0, 0)),
                   pl.BlockSpec((cps, DN_HEADS, c, c), lambda i: (i, 0, 0, 0))],
        out_shape=[jax.ShapeDtypeStruct((t, 512), F32), jax.ShapeDtypeStruct((t, 512), F32),
                   jax.ShapeDtypeStruct((n, DN_HEADS, DK, DK), F32), jax.ShapeDtypeStruct((n, DN_HEADS, c, c), F32)],
        scratch_shapes=[pltpu.VMEM((DN_HEADS, DK, DK), F32)],
        compiler_params=_params(("arbitrary",)),
    )(qkv, ba, z, arow, dtb, dnw)


ATT_ROWS = 512


def _pair_rstd(xv, g2_ref):
    return lax.rsqrt(_hdot(xv * xv, g2_ref[...], "b") * (1.0 / HD) + EPS)


def _pair_norm(t, raw_refs, w_refs, out_refs, g2_ref):
    for c in range(t // ATT_ROWS):
        sl = slice(c * ATT_ROWS, (c + 1) * ATT_ROWS)
        for raw, w_ref, out in zip(raw_refs, w_refs, out_refs):
            xv = raw[sl, :]
            out[sl, :] = xv * _pair_rstd(xv, g2_ref) * w_ref[...]


def _bias_tables(rb, bk):
    def body(rb_ref, bk_ref, bias_ref):
        pair = pl.program_id(0)
        prev_half = _iota((BLK, 2 * BLK), 1) < BLK
        for p in range(len(PATTERNS)):
            bk_p = bk_ref[p]
            for hh in range(2):
                head = 2 * pair + hh
                bm = jnp.full((BLK, 2 * BLK), NEG, F32)
                for b in range(N_BUCKETS):
                    bm = jnp.where(bk_p == b, rb_ref[head, b], bm)
                bias_ref[2 * p, hh * BLK:(hh + 1) * BLK, :] = bm
                bias_ref[2 * p + 1, hh * BLK:(hh + 1) * BLK, :] = jnp.where(prev_half, NEG, bm)

    return pl.pallas_call(
        body, name="bias_tables", grid=(ATT_HEADS // 2,),
        in_specs=[pl.BlockSpec(memory_space=pltpu.SMEM), pl.BlockSpec(bk.shape, lambda i: (0, 0, 0))],
        out_specs=BIAS_SPEC,
        out_shape=jax.ShapeDtypeStruct((2 * len(PATTERNS), ATT_HEADS // 2, 2 * BLK, 2 * BLK), F32),
        compiler_params=_params(("arbitrary",)),
    )(rb, bk)


BIAS_SPEC = pl.BlockSpec((2 * len(PATTERNS), None, 2 * BLK, 2 * BLK), lambda i: (0, i, 0, 0))
PAIR_ROW_SPEC = pl.BlockSpec((1, 128), lambda i: (0, i))


def _stack_heads(xb, h0):
    return jnp.concatenate([jnp.where(h0, xb, 0.0), jnp.where(h0, 0.0, xb)], axis=0).astype(MXU)


def _block_rows(t, r, n):
    per_class = (t // r) // BLK
    res = n // per_class
    j = n % per_class
    start = res + BLK * r * j
    pstart = res + BLK * r * jnp.maximum(j - 1, 0)
    if r == 1:
        return pl.ds(pl.multiple_of(start, BLK), BLK), pl.ds(pl.multiple_of(pstart, BLK), BLK), j
    return pl.ds(start, BLK, stride=r), pl.ds(pstart, BLK, stride=r), j


def _att_fwd(qkv, gate, bias, wq, wk, g2):
    t = qkv.shape[0]
    rows = ATT_ROWS

    def body(bias_ref, qraw_ref, kraw_ref, v_ref, g_ref, wq_ref, wk_ref, g2_ref, o_ref, y_ref, lse_ref,
             o0_ref, o1_ref, o2_ref, l0_ref, l1_ref, l2_ref, q_ref, k_ref):
        h0 = _iota((BLK, 128), 1) < HD
        op_refs, lp_refs = (o0_ref, o1_ref, o2_ref), (l0_ref, l1_ref, l2_ref)
        _pair_norm(t, (qraw_ref, kraw_ref), (wq_ref, wk_ref), (q_ref, k_ref), g2_ref)

        for p, (_, r) in enumerate(PATTERNS):
            def blk(n, carry, p=p, r=r):
                cur, prev, j = _block_rows(t, r, n)
                q2 = _stack_heads(q_ref[cur, :], h0)
                k2 = jnp.concatenate([k_ref[prev, :], k_ref[cur, :]], axis=0).astype(MXU)
                v2 = jnp.concatenate([v_ref[prev, :], v_ref[cur, :]], axis=0).astype(MXU)
                s = _dot_nt(q2, k2) + bias_ref[2 * p + (j == 0).astype(jnp.int32)]
                m = jnp.max(s, axis=1, keepdims=True)
                e = jnp.exp(s - m)
                l = jnp.sum(e, axis=1, keepdims=True)
                pv = _dot(e, v2) / l
                lse = m + jnp.log(l)
                op_refs[p][cur, :] = jnp.where(h0, pv[:BLK], pv[BLK:])
                lp_refs[p][cur, :] = jnp.where(h0, lse[:BLK], lse[BLK:])
                return carry

            lax.fori_loop(0, t // BLK, blk, 0, unroll=8)

        for c in range(t // rows):
            sl = slice(c * rows, (c + 1) * rows)
            ls = [ref[sl, :] for ref in lp_refs]
            mx = jnp.maximum(jnp.maximum(ls[0], ls[1]), ls[2])
            ws = [jnp.exp(v_ - mx) for v_ in ls]
            den = ws[0] + ws[1] + ws[2]
            o = (ws[0] * o0_ref[sl, :] + ws[1] * o1_ref[sl, :] + ws[2] * o2_ref[sl, :]) / den
            o_ref[sl, :] = o
            y_ref[sl, :] = o * _silu(g_ref[sl, :])
            lse_ref[sl, :] = mx + jnp.log(den)

    col = lambda off: pl.BlockSpec((t, 128), lambda i, off=off: (0, off + i))
    return pl.pallas_call(
        body, name="att_fwd", grid=(ATT_HEADS // 2,),
        in_specs=[BIAS_SPEC, col(0), col(4), col(8), col(0), PAIR_ROW_SPEC, PAIR_ROW_SPEC,
                  pl.BlockSpec((128, 128), lambda i: (0, 0))],
        out_specs=[col(0), col(0), col(0)],
        out_shape=[jax.ShapeDtypeStruct((t, 512), F32)] * 3,
        scratch_shapes=[pltpu.VMEM((t, 128), F32)] * 8,
        compiler_params=_params(("arbitrary",)),
    )(bias, qkv, qkv, qkv, gate, wq, wk, g2)


def _outproj_loss(x, ydn, yatt, wout, target):
    t = x.shape[0]
    tm = 512

    def body(x_ref, a_ref, b_ref, w_ref, t_ref, dy_ref, mix_ref, loss_ref):
        @pl.when(pl.program_id(0) == 0)
        def _():
            loss_ref[...] = jnp.zeros_like(loss_ref)

        mixf = jnp.concatenate([a_ref[...], b_ref[...]], axis=1)
        mix_ref[...] = mixf.T.astype(MXU)
        err = x_ref[...] + jnp.dot(mixf.astype(MXU), w_ref[...], preferred_element_type=F32) - t_ref[...]
        dy_ref[...] = err * (1.0 / D_MODEL)
        loss_ref[...] += jnp.sum(err * err) * (0.5 / D_MODEL)

    row = lambda n: pl.BlockSpec((tm, n), lambda i: (i, 0))
    return pl.pallas_call(
        body, name="outproj_loss", grid=(t // tm,),
        in_specs=[row(D_MODEL), row(512), row(512), pl.BlockSpec(wout.shape, lambda i: (0, 0)), row(D_MODEL)],
        out_specs=[row(D_MODEL), pl.BlockSpec((D_MODEL, tm), lambda i: (0, i)), pl.BlockSpec((8, 128), lambda i: (0, 0))],
        out_shape=[jax.ShapeDtypeStruct((t, D_MODEL), F32), jax.ShapeDtypeStruct((D_MODEL, t), MXU),
                   jax.ShapeDtypeStruct((8, 128), F32)],
        compiler_params=_params(("arbitrary",)),
    )(x, ydn, yatt, wout, target)


def _outproj_bwd(dy, wout_t, oraw, z, dnw, oatt, gate, g, gt):
    t = dy.shape[0]
    tm = 256

    def body(dy_ref, w_ref, o_ref, z_ref, dnw_ref, oa_ref, g_ref, grp_ref, grpt_ref,
             do_ref, dz_ref, doa_ref, dg_ref, dd_ref, ddnw_ref):
        @pl.when(pl.program_id(0) == 0)
        def _():
            ddnw_ref[...] = jnp.zeros_like(ddnw_ref)

        dmix = jnp.dot(dy_ref[...].astype(MXU), w_ref[...], preferred_element_type=F32)
        dnw_v = dnw_ref[...]
        acc = jnp.zeros((1, DK), F32)
        for h in range(DN_HEADS):
            sl = slice(h * DK, (h + 1) * DK)
            o, zz, dm = o_ref[:, sl], z_ref[:, sl], dmix[:, sl]
            rs = lax.rsqrt(jnp.mean(o * o, axis=1, keepdims=True) + EPS)
            oh = o * rs
            silu_z, dsilu_z = _silu_and_grad(zz)
            dz_ref[:, sl] = dm * oh * dnw_v * dsilu_z
            d_on = dm * silu_z
            gg = d_on * dnw_v
            do_ref[:, sl] = rs * (gg - oh * jnp.mean(gg * oh, axis=1, keepdims=True))
            acc = acc + jnp.sum(d_on * oh, axis=0, keepdims=True)
        ddnw_ref[...] += jnp.broadcast_to(acc, (8, DK))
        da, gate_v, oa = dmix[:, 512:], g_ref[...], oa_ref[...]
        silu_g, dsilu_g = _silu_and_grad(gate_v)
        doa = da * silu_g
        doa_ref[...] = doa
        dg_ref[...] = da * oa * dsilu_g
        dd_ref[...] = _hdot(_hdot(doa * oa, grp_ref[...], "b"), grpt_ref[...], "b")

    row = lambda n: pl.BlockSpec((tm, n), lambda i: (i, 0))
    full = lambda a: pl.BlockSpec(a.shape, lambda i: (0,) * a.ndim)
    return pl.pallas_call(
        body, name="outproj_bwd", grid=(t // tm,),
        in_specs=[row(D_MODEL), full(wout_t), row(512), row(512), full(dnw), row(512), row(512), full(g), full(gt)],
        out_specs=[row(512)] * 5 + [pl.BlockSpec((8, DK), lambda i: (0, 0))],
        out_shape=[jax.ShapeDtypeStruct((t, 512), F32)] * 5 + [jax.ShapeDtypeStruct((8, DK), F32)],
        compiler_params=_params(("arbitrary",)),
    )(dy, wout_t, oraw, z, dnw, oatt, gate, g, gt)


def _grad_matmul(at, b, name):
    m, t = at.shape
    n = b.shape[1]
    tk = 512
    tn = n if n <= 512 else 512
    nk = t // tk

    def body(a_ref, b_ref, o_ref, acc_ref):
        k = pl.program_id(1)

        @pl.when(k == 0)
        def _():
            acc_ref[...] = jnp.zeros_like(acc_ref)

        acc_ref[...] += jnp.dot(a_ref[...], b_ref[...].astype(MXU), preferred_element_type=F32)

        @pl.when(k == nk - 1)
        def _():
            o_ref[...] = acc_ref[...].astype(GRAD_WIRE)

    return pl.pallas_call(
        body, name=name, grid=(n // tn, nk),
        in_specs=[pl.BlockSpec((m, tk), lambda j, k: (0, k)), pl.BlockSpec((tk, tn), lambda j, k: (k, j))],
        out_specs=pl.BlockSpec((m, tn), lambda j, k: (0, j)),
        out_shape=jax.ShapeDtypeStruct((m, n), GRAD_WIRE),
        scratch_shapes=[pltpu.VMEM((m, tn), F32)],
        compiler_params=_params(("arbitrary", "arbitrary")),
    )(at, b)


def _att_bwd(qkv, do, lse, dd, bias, bk, wq, wk, g2):
    t = qkv.shape[0]
    rows = ATT_ROWS

    def body(bias_ref, bk_ref, qraw_ref, kraw_ref, v_ref, do_ref, lse_ref, dd_ref, wq_ref, wk_ref, g2_ref,
             dq_ref, dk_ref, dv_ref, db_ref, dwq_ref, dwk_ref, ds_ref, q_ref, k_ref):
        pair = pl.program_id(0)

        @pl.when(pair == 0)
        def _():
            db_ref[...] = jnp.zeros_like(db_ref)

        _pair_norm(t, (qraw_ref, kraw_ref), (wq_ref, wk_ref), (q_ref, k_ref), g2_ref)
        ds_ref[...] = jnp.zeros_like(ds_ref)
        for c in range(t // rows):
            sl = slice(c * rows, (c + 1) * rows)
            for ref in (dq_ref, dk_ref, dv_ref):
                ref[sl, :] = jnp.zeros((rows, 128), F32)
        h0 = _iota((BLK, 128), 1) < HD

        def rows_of(xb):
            return jnp.concatenate([xb[:, 0:1], xb[:, HD:HD + 1]], axis=0)

        for p, (_, r) in enumerate(PATTERNS):
            def blk(n, carry, p=p, r=r):
                cur, prev, j = _block_rows(t, r, n)
                q2, do2 = _stack_heads(q_ref[cur, :], h0), _stack_heads(do_ref[cur, :], h0)
                k2 = jnp.concatenate([k_ref[prev, :], k_ref[cur, :]], axis=0).astype(MXU)
                v2 = jnp.concatenate([v_ref[prev, :], v_ref[cur, :]], axis=0).astype(MXU)
                s = _dot_nt(q2, k2) + bias_ref[2 * p + (j == 0).astype(jnp.int32)]
                prob = jnp.exp(s - rows_of(lse_ref[cur, :]))
                ds = prob * (_dot_nt(do2, v2) - rows_of(dd_ref[cur, :]))
                ds_ref[p] += ds
                dq2 = _dot(ds, k2)
                dk2 = _dot_tn(ds, q2)
                dv2 = _dot_tn(prob, do2)
                dq_ref[cur, :] += jnp.where(h0, dq2[:BLK], dq2[BLK:])
                dk_ref[prev, :] += dk2[:BLK]
                dv_ref[prev, :] += dv2[:BLK]
                dk_ref[cur, :] += dk2[BLK:]
                dv_ref[cur, :] += dv2[BLK:]
                return carry

            lax.fori_loop(0, t // BLK, blk, 0, unroll=8)

        ri, ci = _iota((8, 128), 0), _iota((8, 128), 1)
        upd = jnp.zeros((8, 128), F32)
        for p in range(len(PATTERNS)):
            bk = bk_ref[p]
            for hh in range(2):
                dsum = ds_ref[p, hh * BLK:(hh + 1) * BLK, :]
                for b in range(N_BUCKETS):
                    val = jnp.sum(jnp.where(bk == b, dsum, 0.0))
                    upd = upd + jnp.where((ri == 2 * pair + hh) & (ci == b), val, 0.0)
        db_ref[...] += upd

        for raw, d_ref, w_ref, dw_ref in ((qraw_ref, dq_ref, wq_ref, dwq_ref), (kraw_ref, dk_ref, wk_ref, dwk_ref)):
            acc = jnp.zeros((1, 128), F32)
            for c in range(t // rows):
                sl = slice(c * rows, (c + 1) * rows)
                xv, dyv = raw[sl, :], d_ref[sl, :]
                rs = _pair_rstd(xv, g2_ref)
                xh = xv * rs
                gg = dyv * w_ref[...]
                mean = _hdot(gg * xh, g2_ref[...], "b") * (1.0 / HD)
                d_ref[sl, :] = rs * (gg - xh * mean)
                acc = acc + jnp.sum(dyv * xh, axis=0, keepdims=True)
            dw_ref[...] = jnp.broadcast_to(acc, (8, 128))

    col = lambda off: pl.BlockSpec((t, 128), lambda i, off=off: (0, off + i))
    acc8 = pl.BlockSpec((8, 128), lambda i: (0, i))
    return pl.pallas_call(
        body, name="att_bwd", grid=(ATT_HEADS // 2,),
        in_specs=[BIAS_SPEC, pl.BlockSpec(bk.shape, lambda i: (0, 0, 0)),
                  col(0), col(4), col(8), col(0), col(0), col(0), PAIR_ROW_SPEC, PAIR_ROW_SPEC,
                  pl.BlockSpec((128, 128), lambda i: (0, 0))],
        out_specs=[col(0), col(0), col(0), pl.BlockSpec((8, 128), lambda i: (0, 0)), acc8, acc8],
        out_shape=[jax.ShapeDtypeStruct((t, 512), F32)] * 3 + [jax.ShapeDtypeStruct((8, 128), F32)]
                  + [jax.ShapeDtypeStruct((8, 512), F32)] * 2,
        scratch_shapes=[pltpu.VMEM((len(PATTERNS), 2 * BLK, 2 * BLK), F32)] + [pltpu.VMEM((t, 128), F32)] * 2,
        compiler_params=_params(("arbitrary",)),
    )(bias, bk, qkv, qkv, qkv, do, lse, dd, wq, wk, g2)


def _dn_scan_bwd(qkv, ba, do, sh, th, arow, dtb):
    t = qkv.shape[0]
    n = t // CHUNK
    c = CHUNK
    cps = SCAN_CHUNKS

    def body(qkv_ref, ba_ref, do_ref, sh_ref, th_ref, arow_ref, dtb_ref, dqkv_ref, dba_ref, ds_ref):
        @pl.when(pl.program_id(0) == 0)
        def _():
            ds_ref[...] = jnp.zeros_like(ds_ref)

        hs = range(DN_HEADS)
        chains = [(j, h) for j in range(cps) for h in hs]
        lane = _iota((c, 128), 1)
        row = _iota((c, 1), 0)
        ms = {}
        for j in range(cps):
            rows_j = slice(j * c, (j + 1) * c)
            mj, ri, ci = _chunk_common(qkv_ref[rows_j, :], ba_ref[rows_j, :], arow_ref[...], dtb_ref[...])
            for h in hs:
                ms[j, h] = mj[h]
        q, k, v = ({x: ms[x][nm] for x in chains} for nm in ("q", "k", "v"))
        beta, decay = ({x: ms[x][nm] for x in chains} for nm in ("beta", "decay"))
        eg, egl, etail = ({x: ms[x][nm] for x in chains} for nm in ("eg", "egl", "etail"))
        s = {x: sh_ref[x[0], x[1]] for x in chains}
        tinv = {x: th_ref[x[0], x[1]] for x in chains}
        d_o = {(j, h): do_ref[j * c:(j + 1) * c, h * DK:(h + 1) * DK] for j, h in chains}
        kb = {x: k[x] * beta[x] for x in chains}
        vb = {x: v[x] * beta[x] for x in chains}
        kbg = {x: kb[x] * eg[x] for x in chains}
        amat = {x: jnp.where(ri > ci, _dot_nt(kb[x], k[x]) * decay[x], 0.0) for x in chains}
        attn = {x: jnp.where(ri >= ci, _dot_nt(q[x], k[x]) * decay[x], 0.0) for x in chains}
        uw = {x: _hdot(tinv[x], jnp.concatenate([vb[x], kbg[x]], axis=1)) for x in chains}
        u = {x: uw[x][:, :DK] for x in chains}
        w = {x: uw[x][:, DK:] for x in chains}
        v_new = {x: u[x] - _dot(w[x], s[x]) for x in chains}
        q_dec = {x: q[x] * eg[x] for x in chains}
        k_tail = {x: k[x] * etail[x] for x in chains}
        d_attn = {x: jnp.where(ri >= ci, _dot_nt(d_o[x], v_new[x]), 0.0) for x in chains}
        d_qdec = {x: _dot_nt(d_o[x], s[x]) for x in chains}
        from_o = {x: _dot_tn(attn[x], d_o[x]) for x in chains}
        to_state = {x: _dot_tn(q_dec[x], d_o[x]) for x in chains}

        d_s, d_vnew = {}, {}
        cur = [ds_ref[h] for h in hs]
        for j in reversed(range(cps)):
            for h in hs:
                d_s[j, h] = cur[h]
                d_vnew[j, h] = from_o[j, h] + _dot(k_tail[j, h], cur[h])
            cur = [to_state[j, h] + cur[h] * egl[j, h] - _dot_tn(w[j, h], d_vnew[j, h]) for h in hs]
        for h in hs:
            ds_ref[h] = cur[h]

        d_ktail = {x: _dot_nt(v_new[x], d_s[x]) for x in chains}
        d_gl = {x: jnp.sum(s[x] * d_s[x]) * egl[x] for x in chains}
        d_w = {x: -_dot_nt(d_vnew[x], s[x]) for x in chains}
        d_both = {x: _hdot_tn(tinv[x], jnp.concatenate([d_vnew[x], d_w[x]], axis=1)) for x in chains}
        d_vb = {x: d_both[x][:, :DK] for x in chains}
        d_kbg = {x: d_both[x][:, DK:] for x in chains}
        d_a = {x: -jnp.where(ri > ci, _hdot_nt(d_both[x], uw[x]), 0.0) for x in chains}
        d_qk = {x: d_attn[x] * decay[x] for x in chains}
        d_kk = {x: d_a[x] * decay[x] for x in chains}
        d_kb = {x: _dot(d_kk[x], k[x]) + d_kbg[x] * eg[x] for x in chains}
        d_q = {x: _dot(d_qk[x], k[x]) + d_qdec[x] * eg[x] for x in chains}
        d_k = {x: _dot_tn(d_qk[x], q[x]) + _dot_tn(d_kk[x], kb[x]) + d_ktail[x] * etail[x] + d_kb[x] * beta[x]
               for x in chains}
        d_beta = {x: jnp.sum(d_kb[x] * k[x] + d_vb[x] * v[x], axis=1, keepdims=True) for x in chains}
        mm = {x: d_a[x] * amat[x] + d_attn[x] * attn[x] for x in chains}
        for j in range(cps):
            rows_j = slice(j * c, (j + 1) * c)
            rows = jnp.zeros((c, c), F32)
            for h in hs:
                rows = rows + jnp.where(ri == h, jnp.sum(mm[j, h], axis=0, keepdims=True), 0.0)
            cols_t = jnp.concatenate([rows, jnp.zeros((c, c), F32)], axis=1).T[:c, :]
            d_gc_all = jnp.zeros((c, 128), F32)
            for h in hs:
                x = (j, h)
                tail_term = jnp.sum(d_ktail[x] * k_tail[x], axis=1, keepdims=True)
                d_gc = (jnp.sum(mm[x], axis=1, keepdims=True) - _lane_col(cols_t, h)
                        + jnp.sum(d_qdec[x] * q_dec[x] + d_kbg[x] * kbg[x], axis=1, keepdims=True) - tail_term)
                d_gc = d_gc + jnp.where(row == c - 1, jnp.sum(tail_term) + d_gl[x], 0.0)
                d_gc_all = d_gc_all + jnp.where(lane == DN_HEADS + h, d_gc, 0.0)
            d_g_all = _hdot((ri <= ci).astype(F32), d_gc_all, "a")
            dba = jnp.zeros((c, 128), F32)
            for h in hs:
                x = (j, h)
                d_g = _lane_col(d_g_all, DN_HEADS + h)
                d_braw = d_beta[x] * beta[x] * (1.0 - beta[x])
                d_araw = d_g * ms[x]["a_h"] * _sigmoid(ms[x]["a_raw"] + ms[x]["dt_h"])
                dba = dba + jnp.where(lane == h, d_braw, 0.0) + jnp.where(lane == DN_HEADS + h, d_araw, 0.0) \
                    + jnp.where(lane == 2 * DN_HEADS + h, d_g * ms[x]["g"], 0.0)
                dqkv_ref[rows_j, h * DK:(h + 1) * DK] = d_q[x]
                dqkv_ref[rows_j, D_DN + h * DK:D_DN + (h + 1) * DK] = d_k[x]
                dqkv_ref[rows_j, 2 * D_DN + h * DK:2 * D_DN + (h + 1) * DK] = d_vb[x] * beta[x]
            dba_ref[rows_j, :] = dba

    nsteps = n // cps
    rev = lambda w_: pl.BlockSpec((cps * c, w_), lambda i: (nsteps - 1 - i, 0))
    one = pl.BlockSpec((1, 128), lambda i: (0, 0))
    return pl.pallas_call(
        body, name="dn_scan_bwd", grid=(nsteps,),
        in_specs=[rev(1536), rev(128), rev(512),
                  pl.BlockSpec((cps, DN_HEADS, DK, DK), lambda i: (nsteps - 1 - i, 0, 0, 0)),
                  pl.BlockSpec((cps, DN_HEADS, c, c), lambda i: (nsteps - 1 - i, 0, 0, 0)), one, one],
        out_specs=[rev(1536), rev(128)],
        out_shape=[jax.ShapeDtypeStruct((t, 1536), F32), jax.ShapeDtypeStruct((t, 128), F32)],
        scratch_shapes=[pltpu.VMEM((DN_HEADS, DK, DK), F32)],
        compiler_params=_params(("arbitrary",)),
    )(qkv, ba, do, sh, th, arow, dtb)


def _dn_prep_bwd(pdn, cw, dact):
    t = pdn.shape[0]
    nchunk = t // CONV_ROWS

    def body(u_ref, w_ref, d_ref, du_ref, dw_ref, dy_ref):
        j = pl.program_id(0)
        dy_ref[t:t + 8, :] = jnp.zeros((8, 128), F32)
        dw = [jnp.zeros((1, 128), F32) for _ in range(4)]
        for c in range(nchunk):
            sl = slice(c * CONV_ROWS, (c + 1) * CONV_ROWS)
            taps, y = _conv_taps(u_ref, c, w_ref)
            a, da_dy = _silu_and_grad(y)
            dout = d_ref[sl, :]
            rs = lax.rsqrt(jnp.sum(a * a, axis=1, keepdims=True) + EPS)
            f = jnp.where(j < 8, rs, 1.0) * jnp.where(j < 4, DK ** -0.5, 1.0)
            corr = jnp.where(j < 8, f * rs * rs * jnp.sum(dout * a, axis=1, keepdims=True), 0.0)
            dy = (f * dout - corr * a) * da_dy
            dy_ref[sl, :] = dy
            for k_ in range(4):
                dw[3 - k_] = dw[3 - k_] + jnp.sum(taps[k_] * dy, axis=0, keepdims=True)
        for i in range(4):
            dw_ref[i:i + 1, :] = dw[i]
        for c in range(nchunk):
            r0 = c * CONV_ROWS
            ext = dy_ref[r0:r0 + CONV_ROWS + 8, :]
            du = ext[:CONV_ROWS, :] * w_ref[3:4, :]
            for k_ in (1, 2, 3):
                du = du + pltpu.roll(ext, CONV_ROWS + 8 - k_, 0)[:CONV_ROWS, :] * w_ref[3 - k_:4 - k_, :]
            du_ref[r0:r0 + CONV_ROWS, :] = du

    return pl.pallas_call(
        body, name="dn_prep_bwd", grid=(12,),
        in_specs=[pl.BlockSpec((t, 128), lambda j: (0, j)), pl.BlockSpec((4, 128), lambda j: (0, j)),
                  pl.BlockSpec((t, 128), lambda j: (0, j))],
        out_specs=[pl.BlockSpec((t, 128), lambda j: (0, j)), pl.BlockSpec((4, 128), lambda j: (0, j))],
        out_shape=[jax.ShapeDtypeStruct((t, 1536), F32), jax.ShapeDtypeStruct((4, 1536), F32)],
        scratch_shapes=[pltpu.VMEM((t + 8, 128), F32)],
        compiler_params=_params(("arbitrary",)),
    )(pdn, cw, dact)


SECTIONS = (("dn", 0, 1536), ("z", 1536, 512), ("q", 2048, 512), ("k", 2560, 512), ("v", 3072, 512),
            ("gate", 3584, 512), ("ba", 4096, 128))


def _inproj_bwd(x, nw, wt, dy, dsecs, partials):
    t = x.shape[0]
    tm = 256
    npart = len(partials)
    nsteps = t // tm

    nsec = len(SECTIONS)

    def body(x_ref, nw_ref, w_ref, dy_ref, *rest):
        sec_refs, rest = rest[:nsec], rest[nsec:]
        part_refs, (gx_ref, dnw_ref, cs_ref) = rest[:npart], rest[npart:npart + 3]
        got_refs, (send, recv, loc) = rest[npart + 3:2 * npart + 3], rest[2 * npart + 3:]
        starts, waits = _chip_swap_copies(part_refs, got_refs, send, recv, loc)

        @pl.when(pl.program_id(0) == 0)
        def _():
            for start in starts:
                start()
            dnw_ref[...] = jnp.zeros_like(dnw_ref)
            cs_ref[...] = jnp.zeros_like(cs_ref)

        @pl.when(pl.program_id(0) == nsteps - 1)
        def _():
            for wait in waits:
                wait()

        dh = jnp.zeros((tm, D_MODEL), F32)
        for ref, (_, lo, width) in zip(sec_refs, SECTIONS):
            dh = dh + jnp.dot(ref[...].astype(MXU), w_ref[lo:lo + width, :], preferred_element_type=F32)
        xv = x_ref[...]
        rstd = lax.rsqrt(jnp.mean(xv * xv, axis=-1, keepdims=True) + EPS)
        xh = xv * rstd
        gg = dh * nw_ref[...]
        gx_ref[...] = rstd * (gg - xh * jnp.mean(gg * xh, axis=-1, keepdims=True)) + dy_ref[...]
        dnw_ref[...] += jnp.broadcast_to(jnp.sum(dh * xh, axis=0, keepdims=True), (8, D_MODEL))
        cs_ref[...] += jnp.broadcast_to(jnp.sum(sec_refs[nsec - 1][...], axis=0, keepdims=True), (8, 128))

    row = lambda n: pl.BlockSpec((tm, n), lambda i: (i, 0))
    full = lambda a: pl.BlockSpec(a.shape, lambda i: (0,) * a.ndim)
    res = pl.pallas_call(
        body, name="inproj_bwd", grid=(nsteps,),
        in_specs=[row(D_MODEL), full(nw), full(wt), row(D_MODEL)] + [row(width) for _, _, width in SECTIONS]
                 + [ANY_SPEC] * npart,
        out_specs=[row(D_MODEL), pl.BlockSpec((8, D_MODEL), lambda i: (0, 0)), pl.BlockSpec((8, 128), lambda i: (0, 0))]
                  + [ANY_SPEC] * npart,
        out_shape=[jax.ShapeDtypeStruct((t, D_MODEL), F32), jax.ShapeDtypeStruct((8, D_MODEL), F32),
                   jax.ShapeDtypeStruct((8, 128), F32)] + [jax.ShapeDtypeStruct(p.shape, p.dtype) for p in partials],
        scratch_shapes=[pltpu.SemaphoreType.DMA((npart, 3)), pltpu.SemaphoreType.DMA((npart, 3)),
                        pltpu.SemaphoreType.DMA((npart,))],
        compiler_params=_params(("arbitrary",)),
    )(x, nw, wt, dy, *dsecs, *partials)
    return res[0], res[1], res[2], res[3:]


def _adamw_sum(w, gs, m, v, name):
    r, c = w.shape
    nsum = gs.shape[0]
    tr = r if r <= 256 else 256
    c1 = 1.0 - ADAM_B1 ** ADAM_STEP
    c2 = 1.0 - ADAM_B2 ** ADAM_STEP

    def body(w_ref, g_ref, m_ref, v_ref, go_ref, d_ref, mo_ref, vo_ref):
        g = g_ref[0].astype(F32)
        for s in range(1, nsum):
            g = g + g_ref[s].astype(F32)
        mn = ADAM_B1 * m_ref[...] + (1.0 - ADAM_B1) * g
        vn = ADAM_B2 * v_ref[...] + (1.0 - ADAM_B2) * (g * g)
        go_ref[...] = g
        mo_ref[...] = mn
        vo_ref[...] = vn
        d_ref[...] = -ADAM_LR * ((mn / c1) / (jnp.sqrt(vn / c2) + ADAM_EPS) + ADAM_WD * w_ref[...])

    blk = pl.BlockSpec((tr, c), lambda i: (i, 0))
    return pl.pallas_call(
        body, name=name, grid=(r // tr,),
        in_specs=[blk, pl.BlockSpec((nsum, tr, c), lambda i: (0, i, 0)), blk, blk],
        out_specs=[blk] * 4, out_shape=[jax.ShapeDtypeStruct((r, c), F32)] * 4,
        compiler_params=_params(("arbitrary",)),
    )(w, gs, m, v)


def _local_step(x, target, h, ht, w_sect, conv_w, a_log, dt_bias, dn_norm_w, q_norm_w, k_norm_w, rel_bias, wout_shard):
    arow =jnp.zeros((1, 128), F32).at[0, DN_HEADS:2 * DN_HEADS].set(-jnp.exp(a_log[0]))
    dtb = jnp.zeros((1, 128), F32).at[0, DN_HEADS:2 * DN_HEADS].set(dt_bias[0])
    g_np, gt_np = _group_mats()
    g, gt = jnp.asarray(g_np), jnp.asarray(gt_np)
    g2 = jnp.asarray(np.kron(np.eye(2, dtype=np.float32), np.ones((HD, HD), np.float32)))
    bk = jnp.asarray(_bucket_tables())
    wq = jnp.tile(q_norm_w, (1, ATT_HEADS)) * (HD ** -0.5)
    wk = jnp.tile(k_norm_w, (1, ATT_HEADS))

    pdn, qkv_dn, z, patt, gate, ba, wout8 = _inproj(h, w_sect, conv_w, wout_shard)
    w_out = wout8.reshape(D_MODEL, D_MODEL)
    oraw, ydn, sh, th = _dn_scan_fwd(qkv_dn, ba, z, arow, dtb, dn_norm_w)
    bias = _bias_tables(rel_bias, bk)
    oatt, yatt, lse = _att_fwd(patt, gate, bias, wq, wk, g2)
    dy, mix_t, loss8 = _outproj_loss(x, ydn, yatt, w_out, target)

    do_dn, dz, do_att, dgate, dd, ddnw = _outproj_bwd(dy, w_out.T, oraw, z, dn_norm_w, oatt, gate, g, gt)
    d_wout = _grad_matmul(mix_t, dy, "dw_out")
    dq, dk, dv, drb, dwq8, dwk8 = _att_bwd(patt, do_att, lse, dd, bias, bk, wq, wk, g2)
    dqkv_dn, dba = _dn_scan_bwd(qkv_dn, ba, do_dn, sh, th, arow, dtb)
    dpdn, d_conv = _dn_prep_bwd(pdn, conv_w, dqkv_dn)
    dsecs = (dpdn, dz, dq, dk, dv, dgate, dba)
    dw_sections = [_grad_matmul(ht, d_, "dw_in_" + nm) for d_, (nm, _, _) in zip(dsecs, SECTIONS)]
    return dict(w_in_sections=dw_sections, conv_w=d_conv, w_out=d_wout, dy=dy, dsecs=dsecs,
                small_parts=(loss8, ddnw, dwq8, dwk8, drb))


def _finish_step(x, norm_w, w_sect_t, gr, partials):
    grad_x, dnw8, cs8, got = _inproj_bwd(x, norm_w, w_sect_t, gr["dy"], gr["dsecs"], partials)
    return grad_x, _pack_small_grads(dnw8, cs8, *gr["small_parts"]), got


SMALL_ROWS = 24
SMALL_AT = dict(a_log=(slice(8, 9), slice(0, 4)), dt_bias=(slice(9, 10), slice(0, 4)),
                dn_norm_w=(slice(10, 11), slice(0, 128)), q_norm_w=(slice(11, 12), slice(0, HD)),
                k_norm_w=(slice(12, 13), slice(0, HD)), rel_bias=(slice(16, 24), slice(0, N_BUCKETS)))
SMALL_NAMES = ("norm_w", "a_log", "dt_bias", "dn_norm_w", "q_norm_w", "k_norm_w", "rel_bias")


LOSS_ROW = 13


def _pack_small_grads(dnw8, cs8, loss8, ddnw8, dwq8, dwk8, drb):
    def body(dnw_ref, cs_ref, loss_ref, ddnw_ref, dwq_ref, dwk_ref, drb_ref, o_ref):
        lane = _iota((8, 128), 1)
        o_ref[...] = jnp.zeros_like(o_ref)
        o_ref[LOSS_ROW:LOSS_ROW + 1, :] = jnp.where(lane == 0, loss_ref[...], 0.0)[0:1, :]
        for k in range(D_MODEL // 128):
            o_ref[k:k + 1, :] = dnw_ref[0:1, k * 128:(k + 1) * 128]
        cs = cs_ref[...]
        o_ref[8:9, :] = jnp.where(lane < DN_HEADS, pltpu.roll(cs, 128 - 2 * DN_HEADS, 1), 0.0)[0:1, :]
        o_ref[9:10, :] = jnp.where(lane < DN_HEADS, pltpu.roll(cs, 128 - DN_HEADS, 1), 0.0)[0:1, :]
        o_ref[10:11, :] = ddnw_ref[0:1, :]
        for row, ref, scale in ((11, dwq_ref, HD ** -0.5), (12, dwk_ref, 1.0)):
            acc = ref[:, 0:128] + ref[:, 128:256] + ref[:, 256:384] + ref[:, 384:512]
            acc = (acc + pltpu.roll(acc, HD, 1)) * scale
            o_ref[row:row + 1, :] = jnp.where(lane < HD, acc, 0.0)[0:1, :]
        o_ref[16:24, :] = drb_ref[...]

    return pl.pallas_call(body, name="pack_small_grads", out_shape=jax.ShapeDtypeStruct((SMALL_ROWS, 128), F32),
                          )(dnw8, cs8, loss8, ddnw8, dwq8, dwk8, drb)


def _adam_math(w, g, m, v):
    c1 = 1.0 - ADAM_B1 ** ADAM_STEP
    c2 = 1.0 - ADAM_B2 ** ADAM_STEP
    mn = ADAM_B1 * m + (1.0 - ADAM_B1) * g
    vn = ADAM_B2 * v + (1.0 - ADAM_B2) * (g * g)
    return -ADAM_LR * ((mn / c1) / (jnp.sqrt(vn / c2) + ADAM_EPS) + ADAM_WD * w), mn, vn


def _adamw_small(gs, ws, ms, vs):
    n = len(SMALL_NAMES)

    def body(g_ref, *refs):
        w_refs, m_refs, v_refs = refs[:n], refs[n:2 * n], refs[2 * n:3 * n]
        outs, loss_ref = refs[3 * n:7 * n], refs[7 * n]
        loss = g_ref[0, LOSS_ROW:LOSS_ROW + 1, :]
        for s in range(1, gs.shape[0]):
            loss = loss + g_ref[s, LOSS_ROW:LOSS_ROW + 1, :]
        loss_ref[...] = loss

        def one(i, rows, lanes, at):
            g = g_ref[0, rows, lanes]
            for s in range(1, gs.shape[0]):
                g = g + g_ref[s, rows, lanes]
            d, mn, vn = _adam_math(w_refs[i][at], g, m_refs[i][at], v_refs[i][at])
            for kind, val in enumerate((g, d, mn, vn)):
                outs[kind * n + i][at] = val

        for k in range(D_MODEL // 128):
            one(0, slice(k, k + 1), slice(0, 128), (slice(0, 1), slice(k * 128, (k + 1) * 128)))
        for i, nm in enumerate(SMALL_NAMES[1:], start=1):
            rows, lanes = SMALL_AT[nm]
            one(i, rows, lanes, (slice(None), slice(None)))

    shapes = [jax.ShapeDtypeStruct(w.shape, F32) for w in ws]
    res = pl.pallas_call(body, name="adamw_small",
                         out_shape=shapes * 4 + [jax.ShapeDtypeStruct((1, 128), F32)])(gs, *ws, *ms, *vs)
    return [res[k * n:(k + 1) * n] for k in range(4)], res[4 * n]


def kernel(x, norm_w, w_in, conv_w, a_log, dt_bias, dn_norm_w, q_norm_w, k_norm_w, rel_bias, w_out, loss_target, m_norm_w, m_w_in, m_conv_w, m_a_log, m_dt_bias, m_dn_norm_w, m_q_norm_w, m_k_norm_w, m_rel_bias, m_w_out, v_norm_w, v_w_in, v_conv_w, v_a_log, v_dt_bias, v_dn_norm_w, v_q_norm_w, v_k_norm_w, v_rel_bias, v_w_out):
    assert w_in.shape[2] == SHARD_COLS
    h, ht, (win8, conv8) = _norm_and_gather(x[0], norm_w, [w_in[0].astype(MXU), conv_w[0]])
    w_sect, w_sect_t = _build_w(win8)
    conv_full = conv8.transpose(1, 0, 2).reshape(4, 3 * D_DN)

    gr = _local_step(x[0], loss_target[0], h, ht, w_sect, conv_full, a_log, dt_bias, dn_norm_w, q_norm_w,
                     k_norm_w, rel_bias, w_out[0].astype(MXU))

    slabs = [_build_slabs(gr["w_in_sections"]),
             gr["w_out"].reshape(4, 2, D_MODEL // N_DEV, D_MODEL).transpose(1, 0, 2, 3),
             gr["conv_w"].reshape(4, 4, 2, 3 * D_DN // N_DEV).transpose(2, 1, 0, 3)]
    core = lax.axis_index("c").astype(jnp.int32).reshape(1)
    from_sibling = _swap_siblings(slabs)
    wires = (GRAD_WIRE, GRAD_WIRE, F32)
    partial = [_chip_sum(slabs[i], from_sibling[i], core, wires[i], "chip_sum_%d" % i) for i in range(3)]
    grad_x, small_pack, (r_win, r_wout, r_conv) = _finish_step(x[0], norm_w, w_sect_t, gr, partial)
    r_small = _share_small(small_pack)

    g_win, d_win, m_win, v_win = _adamw_sum(w_in[0], r_win, m_w_in[0], v_w_in[0], "adamw_w_in")
    g_wout, d_wout, m_wout, v_wout = _adamw_sum(w_out[0], r_wout, m_w_out[0], v_w_out[0], "adamw_w_out")
    g_conv, d_conv, m_conv, v_conv = _adamw_sum(conv_w[0], r_conv, m_conv_w[0], v_conv_w[0], "adamw_conv_w")
    small, loss_row = _adamw_small(r_small,
                                   (norm_w, a_log, dt_bias, dn_norm_w, q_norm_w, k_norm_w, rel_bias),
                                   (m_norm_w, m_a_log, m_dt_bias, m_dn_norm_w, m_q_norm_w, m_k_norm_w, m_rel_bias),
                                   (v_norm_w, v_a_log, v_dt_bias, v_dn_norm_w, v_q_norm_w, v_k_norm_w, v_rel_bias))

    loss = loss_row[0, 0]
    names = ("norm_w", "w_in", "conv_w", "a_log", "dt_bias", "dn_norm_w", "q_norm_w", "k_norm_w", "rel_bias", "w_out")
    big = dict(w_in=(g_win, d_win, m_win, v_win), conv_w=(g_conv, d_conv, m_conv, v_conv),
               w_out=(g_wout, d_wout, m_wout, v_wout))
    outs = [loss, grad_x[None]]
    for kind in range(4):
        for nm in names:
            outs.append(big[nm][kind][None] if nm in big else small[kind][SMALL_NAMES.index(nm)])
    return tuple(outs)
```

```python
import math

import numpy as np
import jax
import jax.numpy as jnp
from jax import lax
from jax.experimental import pallas as pl
from jax.experimental.pallas import tpu as pltpu

F32 = jnp.float32
MXU = jnp.bfloat16
GRAD_WIRE = jnp.bfloat16

D_MODEL = 1024
D_DN = 512
DN_HEADS = 4
DK = 128
CHUNK = 64
D_ATT = 512
ATT_HEADS = 8
HD = 64
PATTERNS = ((128, 1), (512, 4), (2048, 16))
BLK = 128
N_BUCKETS = 32
MAX_DISTANCE = 2048
EPS = 1e-6
W_COLS = 4224
N_DEV = 8
AXES = ("x", "y", "c")

ADAM_LR = 0.001
ADAM_B1 = 0.9
ADAM_B2 = 0.999
ADAM_EPS = 1e-08
ADAM_WD = 0.01
ADAM_STEP = 10

VMEM_LIMIT = 56 * 1024 * 1024
NEG = -1e30


def _dot(a, b):
    return jnp.dot(a.astype(MXU), b.astype(MXU), preferred_element_type=F32)


def _dot_nt(a, b):
    return lax.dot_general(a.astype(MXU), b.astype(MXU), (((1,), (1,)), ((), ())), preferred_element_type=F32)


def _dot_tn(a, b):
    return lax.dot_general(a.astype(MXU), b.astype(MXU), (((0,), (0,)), ((), ())), preferred_element_type=F32)


def _split(a):
    hi = a.astype(jnp.bfloat16)
    return hi, (a - hi.astype(F32)).astype(jnp.bfloat16)


def _dot_split(a, b, dims, exact):
    dg = lambda u, v: lax.dot_general(u, v, (dims, ((), ())), preferred_element_type=F32)
    if exact == "b":
        ah, al = _split(a)
        bh = b.astype(jnp.bfloat16)
        return dg(ah, bh) + dg(al, bh)
    if exact == "a":
        bh, bm = _split(b)
        bl = (b - bh.astype(F32) - bm.astype(F32)).astype(jnp.bfloat16)
        ah = a.astype(jnp.bfloat16)
        return dg(ah, bh) + (dg(ah, bm) + dg(ah, bl))
    ah, al = _split(a)
    bh, bl = _split(b)
    return dg(ah, bh) + (dg(ah, bl) + dg(al, bh))


def _wy_inverses(amat, eye):
    tinv = {x: eye - amat[x] for x in amat}
    pw = amat
    for _ in range(5):
        pw = {x: _hdot(pw[x], pw[x]) for x in amat}
        tinv = {x: tinv[x] + _hdot(tinv[x], pw[x]) for x in amat}
    return tinv


def _hdot(a, b, exact=None):
    return _dot_split(a, b, ((1,), (0,)), exact)


def _hdot_nt(a, b, exact=None):
    return _dot_split(a, b, ((1,), (1,)), exact)


def _hdot_tn(a, b, exact=None):
    return _dot_split(a, b, ((0,), (0,)), exact)


def _sigmoid(x):
    return 1.0 / (1.0 + jnp.exp(-x))


def _silu(x):
    return x * _sigmoid(x)


def _silu_and_grad(x):
    s = _sigmoid(x)
    return x * s, s * (1.0 + x * (1.0 - s))


def _softplus(x):
    return jnp.maximum(x, 0.0) + jnp.log(1.0 + jnp.exp(-jnp.abs(x)))


def _iota(shape, dim):
    return lax.broadcasted_iota(jnp.int32, shape, dim)


def _lane_col(x, k):
    return jnp.sum(jnp.where(_iota(x.shape, 1) == k, x, 0.0), axis=1, keepdims=True)


def _params(sem=None):
    return pltpu.CompilerParams(dimension_semantics=sem, vmem_limit_bytes=VMEM_LIMIT)


def _t5_bucket(dist):
    max_exact = N_BUCKETS // 2
    d = np.maximum(dist, 1).astype(np.float64)
    large = max_exact + (np.log(d / max_exact) / math.log(MAX_DISTANCE / max_exact)
                         * (N_BUCKETS - max_exact)).astype(np.int32)
    large = np.minimum(large, N_BUCKETS - 1)
    return np.where(dist < max_exact, dist, large).astype(np.int32)


def _bucket_tables():
    qi = np.arange(BLK)[:, None]
    kj = np.arange(2 * BLK)[None, :]
    step = qi - kj + BLK
    band = (step >= 0) & (step <= BLK)
    out = []
    for _, r in PATTERNS:
        b = _t5_bucket(np.clip(step, 0, None) * r)
        out.append(np.where(band, b, -1))
    return np.stack(out).astype(np.int32)


def _group_mats():
    g = np.zeros((D_ATT, 128), np.float32)
    for h in range(ATT_HEADS):
        g[h * HD:(h + 1) * HD, h] = 1.0
    return g, np.ascontiguousarray(g.T)


CHIP_FLIPS = ((1, 0), (0, 1), (1, 1))
ANY_SPEC = pl.BlockSpec(memory_space=pl.ANY)
MESH_ID = pl.DeviceIdType.MESH


def _other_chips():
    x, y = lax.axis_index("x"), lax.axis_index("y")
    return [((1 - x if fx else x), (1 - y if fy else y)) for fx, fy in CHIP_FLIPS]


def _gather_plan(ins, outs, send, recv, loc):
    n = len(ins)
    x, y, c = (lax.axis_index(a) for a in AXES)
    sib = (x, y, 1 - c)
    chips = _other_chips()
    lin = lambda px, py, pc: 4 * px + 2 * py + pc

    def copy(a, k, block, to, src=None):
        slot = outs[a].at[lin(*block)]
        return pltpu.make_async_remote_copy(src_ref=slot if src is None else src, dst_ref=slot,
                                            send_sem=send.at[a, k], recv_sem=recv.at[a, k],
                                            device_id=to, device_id_type=MESH_ID)

    mine = [pltpu.make_async_copy(ins[a], outs[a].at[lin(x, y, c)], loc.at[a]) for a in range(n)]
    firsts = []
    for a in range(n):
        firsts.append(copy(a, 0, (x, y, c), sib, src=ins[a]))
        firsts += [copy(a, 1 + j, (x, y, c), (*chip, c), src=ins[a]) for j, chip in enumerate(chips)]

    def begin():
        for cp in mine + firsts:
            cp.start()

    def finish():
        passed = []
        for j, chip in enumerate(chips):
            for a in range(n):
                copy(a, 1 + j, (*chip, c), (x, y, c)).wait_recv()
                fw = copy(a, 4 + j, (*chip, c), sib)
                fw.start()
                passed.append(fw)
        for a in range(n):
            copy(a, 0, sib, (x, y, c)).wait_recv()
            for j, chip in enumerate(chips):
                copy(a, 4 + j, (*chip, 1 - c), (x, y, c)).wait_recv()
        for cp in firsts + passed:
            cp.wait_send()
        for cp in mine:
            cp.wait()

    return begin, finish


GATHER_SEMS = lambda n: [pltpu.SemaphoreType.DMA((n, 7)), pltpu.SemaphoreType.DMA((n, 7)), pltpu.SemaphoreType.DMA((n,))]


def _swap_siblings(arrs):
    n = len(arrs)

    def body(*refs):
        ins, outs = refs[:n], refs[n:2 * n]
        send, recv = refs[2 * n:]
        x, y, c = (lax.axis_index(a) for a in AXES)
        cps = [pltpu.make_async_remote_copy(src_ref=ins[a].at[1 - c], dst_ref=outs[a], send_sem=send.at[a],
                                            recv_sem=recv.at[a], device_id=(x, y, 1 - c), device_id_type=MESH_ID)
               for a in range(n)]
        for cp in cps:
            cp.start()
        for cp in cps:
            cp.wait()

    return pl.pallas_call(
        body, name="swap_siblings", out_shape=[jax.ShapeDtypeStruct(a.shape[1:], a.dtype) for a in arrs],
        in_specs=[ANY_SPEC] * n, out_specs=[ANY_SPEC] * n,
        scratch_shapes=[pltpu.SemaphoreType.DMA((n,)), pltpu.SemaphoreType.DMA((n,))],
    )(*arrs)


def _chip_sum(mine2, theirs, core, wire, name):
    _, nchip, r, cdim = mine2.shape
    tr = r if r <= 256 else 256

    def body(core_ref, a_ref, b_ref, o_ref):
        del core_ref
        o_ref[...] = (a_ref[...].astype(F32) + b_ref[...].astype(F32)).astype(wire)

    grid_spec = pltpu.PrefetchScalarGridSpec(
        num_scalar_prefetch=1, grid=(nchip, r // tr),
        in_specs=[pl.BlockSpec((None, None, tr, cdim), lambda j, i, cr: (cr[0], j, i, 0)),
                  pl.BlockSpec((None, tr, cdim), lambda j, i, cr: (j, i, 0))],
        out_specs=pl.BlockSpec((None, tr, cdim), lambda j, i, cr: (j, i, 0)))
    return pl.pallas_call(
        body, name=name, grid_spec=grid_spec, out_shape=jax.ShapeDtypeStruct((nchip, r, cdim), wire),
        compiler_params=_params(("arbitrary", "arbitrary")),
    )(core, mine2, theirs)


def _chip_swap_copies(ins, outs, send, recv, loc):
    x, y, c = (lax.axis_index(a) for a in AXES)
    me = 2 * x + y
    starts, arrivals, drains = [], [], []
    for a in range(len(ins)):
        lc = pltpu.make_async_copy(ins[a].at[me], outs[a].at[me], loc.at[a])
        starts.append(lc.start)
        drains.append(lc.wait)
        for j, (px, py) in enumerate(_other_chips()):
            them = 2 * px + py
            cp = pltpu.make_async_remote_copy(src_ref=ins[a].at[them], dst_ref=outs[a].at[me], send_sem=send.at[a, j],
                                              recv_sem=recv.at[a, j], device_id=(px, py, c), device_id_type=MESH_ID)
            landing = pltpu.make_async_remote_copy(src_ref=ins[a].at[them], dst_ref=outs[a].at[them],
                                                   send_sem=send.at[a, j], recv_sem=recv.at[a, j],
                                                   device_id=(px, py, c), device_id_type=MESH_ID)
            starts.append(cp.start)
            arrivals.append(landing.wait_recv)
            drains.append(cp.wait_send)
    return starts, arrivals + drains


def _share_small(pack):
    def body(in_ref, out_ref, send, recv, loc):
        x, y, c = (lax.axis_index(a) for a in AXES)
        me = 4 * x + 2 * y + c
        lc = pltpu.make_async_copy(in_ref, out_ref.at[me], loc.at[0])
        lc.start()
        sends, arrivals = [], []
        for k in range(1, N_DEV):
            px = 1 - x if k & 4 else x
            py = 1 - y if k & 2 else y
            pc = 1 - c if k & 1 else c
            cp = pltpu.make_async_remote_copy(src_ref=in_ref, dst_ref=out_ref.at[me], send_sem=send.at[k - 1],
                                              recv_sem=recv.at[k - 1], device_id=(px, py, pc), device_id_type=MESH_ID)
            cp.start()
            sends.append(cp)
            arrivals.append(pltpu.make_async_remote_copy(src_ref=in_ref, dst_ref=out_ref.at[4 * px + 2 * py + pc],
                                                         send_sem=send.at[k - 1], recv_sem=recv.at[k - 1],
                                                         device_id=(px, py, pc), device_id_type=MESH_ID))
        for cp in arrivals:
            cp.wait_recv()
        for cp in sends:
            cp.wait_send()
        lc.wait()

    return pl.pallas_call(
        body, name="share_small", out_shape=jax.ShapeDtypeStruct((N_DEV,) + pack.shape, pack.dtype),
        in_specs=[ANY_SPEC], out_specs=ANY_SPEC,
        scratch_shapes=[pltpu.SemaphoreType.DMA((N_DEV - 1,)), pltpu.SemaphoreType.DMA((N_DEV - 1,)),
                        pltpu.SemaphoreType.DMA((1,))],
    )(pack)


W_PARTS = ((0, 0, 2048), (2048, 4096, 8), (2056, 2048, 2048))
SHARD_COLS = 513


def _pieces(lo, hi, parts):
    out = []
    for ref_start, tgt_start, width in parts:
        a, b = max(lo, ref_start), min(hi, ref_start + width)
        if a < b:
            out.append((a - lo, tgt_start + a - ref_start, b - a))
    return out


def _build_w(win8):
    tr = 256

    def body(in_ref, w_ref, wt_ref):
        w_ref[:, 4096:W_COLS] = jnp.zeros((tr, W_COLS - 4096), MXU)
        for p in range(N_DEV):
            for src, dst, width in _pieces(p * SHARD_COLS, (p + 1) * SHARD_COLS, W_PARTS):
                w_ref[:, dst:dst + width] = in_ref[p, :, src:src + width]
        for k in range(W_COLS // 128):
            wt_ref[k * 128:(k + 1) * 128, :] = w_ref[:, k * 128:(k + 1) * 128].astype(F32).T.astype(MXU)

    return pl.pallas_call(
        body, name="build_w", grid=(D_MODEL // tr,),
        in_specs=[pl.BlockSpec((N_DEV, tr, SHARD_COLS), lambda i: (0, i, 0))],
        out_specs=[pl.BlockSpec((tr, W_COLS), lambda i: (i, 0)), pl.BlockSpec((W_COLS, tr), lambda i: (0, i))],
        out_shape=[jax.ShapeDtypeStruct((D_MODEL, W_COLS), MXU), jax.ShapeDtypeStruct((W_COLS, D_MODEL), MXU)],
        compiler_params=_params(("arbitrary",)),
    )(win8)


def _build_slabs(secs):
    tr = 256
    parts = ((0, 0, 1536), (1536, 1, 512), (2048, 6, 8), (2056, 2, 512), (2568, 3, 512), (3080, 4, 512),
             (3592, 5, 512))

    def body(*refs):
        o_ref = refs[len(secs)]
        for p in range(N_DEV):
            lo, hi = p * SHARD_COLS, (p + 1) * SHARD_COLS
            for ref_start, idx, width in parts:
                a, b = max(lo, ref_start), min(hi, ref_start + width)
                if a < b:
                    o_ref[p % 2, p // 2, :, a - lo:b - lo] = refs[idx][:, a - ref_start:b - ref_start]

    return pl.pallas_call(
        body, name="build_slabs", grid=(D_MODEL // tr,),
        in_specs=[pl.BlockSpec((tr, s.shape[1]), lambda i: (i, 0)) for s in secs],
        out_specs=pl.BlockSpec((2, 4, tr, SHARD_COLS), lambda i: (0, 0, i, 0)),
        out_shape=jax.ShapeDtypeStruct((2, 4, D_MODEL, SHARD_COLS), secs[0].dtype),
        compiler_params=_params(("arbitrary",)),
    )(*secs)


def _norm_and_gather(x, nw, shards):
    t = x.shape[0]
    tm = 512
    nsteps = t // tm
    n = len(shards)

    def body(x_ref, nw_ref, *rest):
        ins, (h_ref, ht_ref), outs, sems = rest[:n], rest[n:n + 2], rest[n + 2:2 * n + 2], rest[2 * n + 2:]
        begin, finish = _gather_plan(ins, outs, *sems)

        @pl.when(pl.program_id(0) == 0)
        def _():
            begin()

        @pl.when(pl.program_id(0) == nsteps - 1)
        def _():
            finish()

        xv = x_ref[...]
        rstd = lax.rsqrt(jnp.mean(xv * xv, axis=-1, keepdims=True) + EPS)
        hf = xv * rstd * nw_ref[...]
        h_ref[...] = hf.astype(MXU)
        ht_ref[...] = hf.T.astype(MXU)

    res = pl.pallas_call(
        body, name="norm_and_gather", grid=(nsteps,),
        in_specs=[pl.BlockSpec((tm, D_MODEL), lambda i: (i, 0)), pl.BlockSpec(nw.shape, lambda i: (0, 0))]
                 + [ANY_SPEC] * n,
        out_specs=[pl.BlockSpec((tm, D_MODEL), lambda i: (i, 0)), pl.BlockSpec((D_MODEL, tm), lambda i: (0, i))]
                  + [ANY_SPEC] * n,
        out_shape=[jax.ShapeDtypeStruct((t, D_MODEL), MXU), jax.ShapeDtypeStruct((D_MODEL, t), MXU)]
                  + [jax.ShapeDtypeStruct((N_DEV,) + a.shape, a.dtype) for a in shards],
        scratch_shapes=GATHER_SEMS(n),
        compiler_params=_params(("arbitrary",)),
    )(x, nw, *shards)
    return res[0], res[1], res[2:]


def _inproj(h_all, w, cw, wout_shard):
    t = h_all.shape[0]
    tm = 512
    nsteps = t // tm

    def body(h_ref, w_ref, cw_ref, wo_ref, pdn_ref, qkv_ref, z_ref, patt_ref, gate_ref, ba_ref,
             wo8_ref, halo_ref, send, recv, loc):
        begin, finish = _gather_plan([wo_ref], [wo8_ref], send, recv, loc)

        @pl.when(pl.program_id(0) == 0)
        def _():
            begin()
            halo_ref[...] = jnp.zeros_like(halo_ref)

        @pl.when(pl.program_id(0) == nsteps - 1)
        def _():
            finish()

        h = h_ref[...]
        for ref, lo, hi in ((z_ref, 1536, 2048), (patt_ref, 2048, 3584), (gate_ref, 3584, 4096), (ba_ref, 4096, 4224)):
            ref[...] = jnp.dot(h, w_ref[:, lo:hi], preferred_element_type=F32)
        pdn = jnp.dot(h, w_ref[:, 0:3 * D_DN], preferred_element_type=F32)
        pdn_ref[...] = pdn
        _dn_prep_tile(pdn, halo_ref, cw_ref, qkv_ref)

    row = lambda n: pl.BlockSpec((tm, n), lambda i: (i, 0))
    full = lambda a: pl.BlockSpec(a.shape, lambda i: (0,) * a.ndim)
    return pl.pallas_call(
        body, name="inproj", grid=(nsteps,),
        in_specs=[row(D_MODEL), full(w), full(cw), ANY_SPEC],
        out_specs=[row(1536), row(1536), row(512), row(1536), row(512), row(128), ANY_SPEC],
        out_shape=[jax.ShapeDtypeStruct((t, n), F32) for n in (1536, 1536, 512, 1536, 512, 128)] +
                  [jax.ShapeDtypeStruct((N_DEV,) + wout_shard.shape, wout_shard.dtype)],
        scratch_shapes=[pltpu.VMEM((8, 3 * D_DN), F32)] + GATHER_SEMS(1),
        compiler_params=_params(("arbitrary",)),
    )(h_all, w, cw, wout_shard)


CONV_ROWS = 512


def _conv_taps(u_ref, c, w_ref):
    r0 = c * CONV_ROWS
    if c == 0:
        ext = jnp.concatenate([jnp.zeros((8, 128), F32), u_ref[0:CONV_ROWS, :]], axis=0)
    else:
        ext = u_ref[r0 - 8:r0 + CONV_ROWS, :]
    taps = [ext[8:, :]] + [pltpu.roll(ext, k, 0)[8:, :] for k in (1, 2, 3)]
    y = taps[0] * w_ref[3:4, :]
    for k in (1, 2, 3):
        y = y + taps[k] * w_ref[3 - k:4 - k, :]
    return taps, y


def _dn_prep_tile(pdn, halo_ref, cw_ref, out_ref):
    rows = pdn.shape[0]
    ext = jnp.concatenate([halo_ref[...], pdn], axis=0)
    halo_ref[...] = pdn[rows - 8:, :]
    for j in range(3 * D_DN // 128):
        cols = slice(j * 128, (j + 1) * 128)
        e = ext[:, cols]
        y = e[8:, :] * cw_ref[3:4, cols]
        for k in (1, 2, 3):
            y = y + pltpu.roll(e, k, 0)[8:, :] * cw_ref[3 - k:4 - k, cols]
        a = _silu(y)
        if j < 2 * DN_HEADS:
            a = a * lax.rsqrt(jnp.sum(a * a, axis=1, keepdims=True) + EPS)
        if j < DN_HEADS:
            a = a * DK ** -0.5
        out_ref[:, cols] = a


def _chunk_common(qkv, ba, arow, dtb):
    c = CHUNK
    ri, ci = _iota((c, c), 0), _iota((c, c), 1)
    lane = _iota((c, 128), 1)
    g_all = jnp.where((lane >= DN_HEADS) & (lane < 2 * DN_HEADS), arow * _softplus(ba + dtb), 0.0)
    gc_all = _hdot((ri >= ci).astype(F32), g_all, "a")
    gc_t = gc_all.T
    beta_all = _sigmoid(ba)
    out = []
    for h in range(DN_HEADS):
        gc = _lane_col(gc_all, DN_HEADS + h)
        gcr = gc_t[DN_HEADS + h:DN_HEADS + h + 1, :]
        gl = gc[c - 1:c, :]
        out.append(dict(
            q=qkv[:, h * DK:(h + 1) * DK], k=qkv[:, D_DN + h * DK:D_DN + (h + 1) * DK],
            v=qkv[:, 2 * D_DN + h * DK:2 * D_DN + (h + 1) * DK],
            beta=_lane_col(beta_all, h), g=_lane_col(g_all, DN_HEADS + h),
            a_raw=_lane_col(ba, DN_HEADS + h), a_h=_lane_col(arow, DN_HEADS + h), dt_h=_lane_col(dtb, DN_HEADS + h),
            decay=jnp.exp(jnp.where(ri >= ci, gc - gcr, NEG)), eg=jnp.exp(gc), egl=jnp.exp(gl), etail=jnp.exp(gl - gc)))
    return out, ri, ci


SCAN_CHUNKS = 8


def _dn_scan_fwd(qkv, ba, z, arow, dtb, dnw):
    t = qkv.shape[0]
    n = t // CHUNK
    c = CHUNK
    cps = SCAN_CHUNKS
    hs = range(DN_HEADS)
    chains = [(j, h) for j in range(cps) for h in hs]

    def body(qkv_ref, ba_ref, z_ref, arow_ref, dtb_ref, dnw_ref, o_ref, y_ref, sh_ref, th_ref, s_ref):
        @pl.when(pl.program_id(0) == 0)
        def _():
            s_ref[...] = jnp.zeros_like(s_ref)

        ms = {}
        for j in range(cps):
            rows = slice(j * c, (j + 1) * c)
            mj, ri, ci = _chunk_common(qkv_ref[rows, :], ba_ref[rows, :], arow_ref[...], dtb_ref[...])
            for h in hs:
                ms[j, h] = mj[h]
        kb = {x: ms[x]["k"] * ms[x]["beta"] for x in chains}
        amat = {x: jnp.where(ri > ci, _dot_nt(kb[x], ms[x]["k"]) * ms[x]["decay"], 0.0) for x in chains}
        attn = {x: jnp.where(ri >= ci, _dot_nt(ms[x]["q"], ms[x]["k"]) * ms[x]["decay"], 0.0) for x in chains}
        tinv = _wy_inverses(amat, (ri == ci).astype(F32))
        uw = {x: _hdot(tinv[x], jnp.concatenate([ms[x]["v"] * ms[x]["beta"], kb[x] * ms[x]["eg"]], axis=1))
              for x in chains}
        u = {x: uw[x][:, :DK] for x in chains}
        w = {x: uw[x][:, DK:] for x in chains}
        q_dec = {x: ms[x]["q"] * ms[x]["eg"] for x in chains}
        k_tail = {x: ms[x]["k"] * ms[x]["etail"] for x in chains}
        s = [s_ref[h] for h in hs]
        for j in range(cps):
            rows = slice(j * c, (j + 1) * c)
            v_new = [u[j, h] - _dot(w[j, h], s[h]) for h in hs]
            o = [_dot(q_dec[j, h], s[h]) + _dot(attn[j, h], v_new[h]) for h in hs]
            for h in hs:
                sh_ref[j, h] = s[h]
                th_ref[j, h] = tinv[j, h]
            s = [s[h] * ms[j, h]["egl"] + _dot_tn(k_tail[j, h], v_new[h]) for h in hs]
            for h in hs:
                cols = slice(h * DK, (h + 1) * DK)
                o_ref[rows, cols] = o[h]
                rs = lax.rsqrt(jnp.mean(o[h] * o[h], axis=1, keepdims=True) + EPS)
                y_ref[rows, cols] = o[h] * rs * dnw_ref[...] * _silu(z_ref[rows, cols])
        for h in hs:
            s_ref[h] = s[h]

    row = lambda w_: pl.BlockSpec((cps * c, w_), lambda i: (i, 0))
    one = pl.BlockSpec((1, 128), lambda i: (0, 0))
    return pl.pallas_call(
        body, name="dn_scan_fwd", grid=(n // cps,),
        in_specs=[row(1536), row(128), row(512), one, one, one],
        out_specs=[row(512), row(512), pl.BlockSpec((cps, DN_HEADS, DK, DK), lambda i: (i, 0, 0, 0)),
                   pl.BlockSpec((cps, DN_HEADS, c, c), lambda i: (i, 0, 0, 0))],
        out_shape=[jax.ShapeDtypeStruct((t, 512), F32), jax.ShapeDtypeStruct((t, 512), F32),
                   jax.ShapeDtypeStruct((n, DN_HEADS, DK, DK), F32), jax.ShapeDtypeStruct((n, DN_HEADS, c, c), F32)],
        scratch_shapes=[pltpu.VMEM((DN_HEADS, DK, DK), F32)],
        compiler_params=_params(("arbitrary",)),
    )(qkv, ba, z, arow, dtb, dnw)


ATT_ROWS = 512


def _pair_rstd(xv, g2_ref):
    return lax.rsqrt(_hdot(xv * xv, g2_ref[...], "b") * (1.0 / HD) + EPS)


def _pair_norm(t, raw_refs, w_refs, out_refs, g2_ref):
    for c in range(t // ATT_ROWS):
        sl = slice(c * ATT_ROWS, (c + 1) * ATT_ROWS)
        for raw, w_ref, out in zip(raw_refs, w_refs, out_refs):
            xv = raw[sl, :]
            out[sl, :] = xv * _pair_rstd(xv, g2_ref) * w_ref[...]


def _bias_tables(rb, bk):
    def body(rb_ref, bk_ref, bias_ref):
        pair = pl.program_id(0)
        for p in range(len(PATTERNS)):
            bk_p = bk_ref[p]
            for hh in range(2):
                head = 2 * pair + hh
                bm = jnp.full((BLK, 2 * BLK), NEG, F32)
                for b in range(N_BUCKETS):
                    bm = jnp.where(bk_p == b, rb_ref[head, b], bm)
                bias_ref[p, hh * BLK:(hh + 1) * BLK, :] = bm

    return pl.pallas_call(
        body, name="bias_tables", grid=(ATT_HEADS // 2,),
        in_specs=[pl.BlockSpec(memory_space=pltpu.SMEM), pl.BlockSpec(bk.shape, lambda i: (0, 0, 0))],
        out_specs=BIAS_SPEC,
        out_shape=jax.ShapeDtypeStruct((len(PATTERNS), ATT_HEADS // 2, 2 * BLK, 2 * BLK), F32),
        compiler_params=_params(("arbitrary",)),
    )(rb, bk)


BIAS_SPEC = pl.BlockSpec((len(PATTERNS), None, 2 * BLK, 2 * BLK), lambda i: (0, i, 0, 0))
PAIR_ROW_SPEC = pl.BlockSpec((1, 128), lambda i: (0, i))


def _stack_heads(xb, h0):
    return jnp.concatenate([jnp.where(h0, xb, 0.0), jnp.where(h0, 0.0, xb)], axis=0).astype(MXU)


def _block_rows(t, r, n):
    per_class = (t // r) // BLK
    res = n // per_class
    j = n % per_class
    start = res + BLK * r * j
    pstart = res + BLK * r * jnp.maximum(j - 1, 0)
    if r == 1:
        return pl.ds(pl.multiple_of(start, BLK), BLK), pl.ds(pl.multiple_of(pstart, BLK), BLK), j
    return pl.ds(start, BLK, stride=r), pl.ds(pstart, BLK, stride=r), j


def _att_fwd(qkv, gate, bias, wq, wk, g2):
    t = qkv.shape[0]
    rows = ATT_ROWS

    def body(bias_ref, qraw_ref, kraw_ref, v_ref, g_ref, wq_ref, wk_ref, g2_ref, o_ref, y_ref, lse_ref,
             o0_ref, o1_ref, o2_ref, l0_ref, l1_ref, l2_ref, q_ref, k_ref):
        h0 = _iota((BLK, 128), 1) < HD
        prev_cols = _iota((2 * BLK, 2 * BLK), 1) < BLK
        op_refs, lp_refs = (o0_ref, o1_ref, o2_ref), (l0_ref, l1_ref, l2_ref)
        _pair_norm(t, (qraw_ref, kraw_ref), (wq_ref, wk_ref), (q_ref, k_ref), g2_ref)

        for p, (_, r) in enumerate(PATTERNS):
            def blk(n, carry, p=p, r=r):
                cur, prev, j = _block_rows(t, r, n)
                q2 = _stack_heads(q_ref[cur, :], h0)
                k2 = jnp.concatenate([k_ref[prev, :], k_ref[cur, :]], axis=0).astype(MXU)
                v2 = jnp.concatenate([v_ref[prev, :], v_ref[cur, :]], axis=0).astype(MXU)
                s = _dot_nt(q2, k2) + bias_ref[p] + jnp.where(prev_cols & (j == 0), NEG, 0.0)
                m = jnp.max(s, axis=1, keepdims=True)
                e = jnp.exp(s - m)
                l = jnp.sum(e, axis=1, keepdims=True)
                pv = _dot(e, v2) / l
                lse = m + jnp.log(l)
                op_refs[p][cur, :] = jnp.where(h0, pv[:BLK], pv[BLK:])
                lp_refs[p][cur, :] = jnp.where(h0, lse[:BLK], lse[BLK:])
                return carry

            lax.fori_loop(0, t // BLK, blk, 0, unroll=8)

        for c in range(t // rows):
            sl = slice(c * rows, (c + 1) * rows)
            ls = [ref[sl, :] for ref in lp_refs]
            mx = jnp.maximum(jnp.maximum(ls[0], ls[1]), ls[2])
            ws = [jnp.exp(v_ - mx) for v_ in ls]
            den = ws[0] + ws[1] + ws[2]
            o = (ws[0] * o0_ref[sl, :] + ws[1] * o1_ref[sl, :] + ws[2] * o2_ref[sl, :]) / den
            o_ref[sl, :] = o
            y_ref[sl, :] = o * _silu(g_ref[sl, :])
            lse_ref[sl, :] = mx + jnp.log(den)

    col = lambda off: pl.BlockSpec((t, 128), lambda i, off=off: (0, off + i))
    return pl.pallas_call(
        body, name="att_fwd", grid=(ATT_HEADS // 2,),
        in_specs=[BIAS_SPEC, col(0), col(4), col(8), col(0), PAIR_ROW_SPEC, PAIR_ROW_SPEC,
                  pl.BlockSpec((128, 128), lambda i: (0, 0))],
        out_specs=[col(0), col(0), col(0)],
        out_shape=[jax.ShapeDtypeStruct((t, 512), F32)] * 3,
        scratch_shapes=[pltpu.VMEM((t, 128), F32)] * 8,
        compiler_params=_params(("arbitrary",)),
    )(bias, qkv, qkv, qkv, gate, wq, wk, g2)


def _outproj_loss(x, ydn, yatt, wout, target):
    t = x.shape[0]
    tm = 512

    def body(x_ref, a_ref, b_ref, w_ref, t_ref, dy_ref, mix_ref, loss_ref):
        @pl.when(pl.program_id(0) == 0)
        def _():
            loss_ref[...] = jnp.zeros_like(loss_ref)

        mixf = jnp.concatenate([a_ref[...], b_ref[...]], axis=1)
        mix_ref[...] = mixf.T.astype(MXU)
        err = x_ref[...] + jnp.dot(mixf.astype(MXU), w_ref[...], preferred_element_type=F32) - t_ref[...]
        dy_ref[...] = err * (1.0 / D_MODEL)
        loss_ref[...] += jnp.sum(err * err) * (0.5 / D_MODEL)

    row = lambda n: pl.BlockSpec((tm, n), lambda i: (i, 0))
    return pl.pallas_call(
        body, name="outproj_loss", grid=(t // tm,),
        in_specs=[row(D_MODEL), row(512), row(512), pl.BlockSpec(wout.shape, lambda i: (0, 0)), row(D_MODEL)],
        out_specs=[row(D_MODEL), pl.BlockSpec((D_MODEL, tm), lambda i: (0, i)), pl.BlockSpec((8, 128), lambda i: (0, 0))],
        out_shape=[jax.ShapeDtypeStruct((t, D_MODEL), F32), jax.ShapeDtypeStruct((D_MODEL, t), MXU),
                   jax.ShapeDtypeStruct((8, 128), F32)],
        compiler_params=_params(("arbitrary",)),
    )(x, ydn, yatt, wout, target)


def _outproj_bwd(dy, wout_t, oraw, z, dnw, oatt, gate, g, gt):
    t = dy.shape[0]
    tm = 512

    def body(dy_ref, w_ref, o_ref, z_ref, dnw_ref, oa_ref, g_ref, grp_ref, grpt_ref,
             do_ref, dz_ref, doa_ref, dg_ref, dd_ref, ddnw_ref):
        @pl.when(pl.program_id(0) == 0)
        def _():
            ddnw_ref[...] = jnp.zeros_like(ddnw_ref)

        dmix = jnp.dot(dy_ref[...].astype(MXU), w_ref[...], preferred_element_type=F32)
        dnw_v = dnw_ref[...]
        acc = jnp.zeros((1, DK), F32)
        for h in range(DN_HEADS):
            sl = slice(h * DK, (h + 1) * DK)
            o, zz, dm = o_ref[:, sl], z_ref[:, sl], dmix[:, sl]
            rs = lax.rsqrt(jnp.mean(o * o, axis=1, keepdims=True) + EPS)
            oh = o * rs
            silu_z, dsilu_z = _silu_and_grad(zz)
            dz_ref[:, sl] = dm * oh * dnw_v * dsilu_z
            d_on = dm * silu_z
            gg = d_on * dnw_v
            do_ref[:, sl] = rs * (gg - oh * jnp.mean(gg * oh, axis=1, keepdims=True))
            acc = acc + jnp.sum(d_on * oh, axis=0, keepdims=True)
        ddnw_ref[...] += jnp.broadcast_to(acc, (8, DK))
        da, gate_v, oa = dmix[:, 512:], g_ref[...], oa_ref[...]
        silu_g, dsilu_g = _silu_and_grad(gate_v)
        doa = da * silu_g
        doa_ref[...] = doa
        dg_ref[...] = da * oa * dsilu_g
        dd_ref[...] = _hdot(_hdot(doa * oa, grp_ref[...], "b"), grpt_ref[...], "b")

    row = lambda n: pl.BlockSpec((tm, n), lambda i: (i, 0))
    full = lambda a: pl.BlockSpec(a.shape, lambda i: (0,) * a.ndim)
    return pl.pallas_call(
        body, name="outproj_bwd", grid=(t // tm,),
        in_specs=[row(D_MODEL), full(wout_t), row(512), row(512), full(dnw), row(512), row(512), full(g), full(gt)],
        out_specs=[row(512)] * 5 + [pl.BlockSpec((8, DK), lambda i: (0, 0))],
        out_shape=[jax.ShapeDtypeStruct((t, 512), F32)] * 5 + [jax.ShapeDtypeStruct((8, DK), F32)],
        compiler_params=_params(("arbitrary",)),
    )(dy, wout_t, oraw, z, dnw, oatt, gate, g, gt)


def _grad_matmul(at, b, name):
    m, t = at.shape
    n = b.shape[1]
    tk = 1024
    tn = n if n <= 512 else 512
    nk = t // tk

    def body(a_ref, b_ref, o_ref, acc_ref):
        k = pl.program_id(1)

        @pl.when(k == 0)
        def _():
            acc_ref[...] = jnp.zeros_like(acc_ref)

        acc_ref[...] += jnp.dot(a_ref[...], b_ref[...].astype(MXU), preferred_element_type=F32)

        @pl.when(k == nk - 1)
        def _():
            o_ref[...] = acc_ref[...].astype(GRAD_WIRE)

    return pl.pallas_call(
        body, name=name, grid=(n // tn, nk),
        in_specs=[pl.BlockSpec((m, tk), lambda j, k: (0, k)), pl.BlockSpec((tk, tn), lambda j, k: (k, j))],
        out_specs=pl.BlockSpec((m, tn), lambda j, k: (0, j)),
        out_shape=jax.ShapeDtypeStruct((m, n), GRAD_WIRE),
        scratch_shapes=[pltpu.VMEM((m, tn), F32)],
        compiler_params=_params(("arbitrary", "arbitrary")),
    )(at, b)


def _att_bwd(qkv, do, lse, dd, bias, bk, wq, wk, g2):
    t = qkv.shape[0]
    rows = ATT_ROWS

    def body(bias_ref, bk_ref, qraw_ref, kraw_ref, v_ref, do_ref, lse_ref, dd_ref, wq_ref, wk_ref, g2_ref,
             dq_ref, dk_ref, dv_ref, db_ref, dwq_ref, dwk_ref, ds_ref, q_ref, k_ref):
        pair = pl.program_id(0)

        @pl.when(pair == 0)
        def _():
            db_ref[...] = jnp.zeros_like(db_ref)

        _pair_norm(t, (qraw_ref, kraw_ref), (wq_ref, wk_ref), (q_ref, k_ref), g2_ref)
        ds_ref[...] = jnp.zeros_like(ds_ref)
        for c in range(t // rows):
            sl = slice(c * rows, (c + 1) * rows)
            for ref in (dq_ref, dk_ref, dv_ref):
                ref[sl, :] = jnp.zeros((rows, 128), F32)
        h0 = _iota((BLK, 128), 1) < HD
        prev_cols = _iota((2 * BLK, 2 * BLK), 1) < BLK

        def rows_of(xb):
            return jnp.concatenate([xb[:, 0:1], xb[:, HD:HD + 1]], axis=0)

        for p, (_, r) in enumerate(PATTERNS):
            def blk(n, carry, p=p, r=r):
                cur, prev, j = _block_rows(t, r, n)
                q2, do2 = _stack_heads(q_ref[cur, :], h0), _stack_heads(do_ref[cur, :], h0)
                k2 = jnp.concatenate([k_ref[prev, :], k_ref[cur, :]], axis=0).astype(MXU)
                v2 = jnp.concatenate([v_ref[prev, :], v_ref[cur, :]], axis=0).astype(MXU)
                s = _dot_nt(q2, k2) + bias_ref[p] + jnp.where(prev_cols & (j == 0), NEG, 0.0)
                prob = jnp.exp(s - rows_of(lse_ref[cur, :]))
                ds = prob * (_dot_nt(do2, v2) - rows_of(dd_ref[cur, :]))
                ds_ref[p] += ds
                dq2 = _dot(ds, k2)
                dk2 = _dot_tn(ds, q2)
                dv2 = _dot_tn(prob, do2)
                dq_ref[cur, :] += jnp.where(h0, dq2[:BLK], dq2[BLK:])
                dk_ref[prev, :] += dk2[:BLK]
                dv_ref[prev, :] += dv2[:BLK]
                dk_ref[cur, :] += dk2[BLK:]
                dv_ref[cur, :] += dv2[BLK:]
                return carry

            lax.fori_loop(0, t // BLK, blk, 0, unroll=8)

        ri, ci = _iota((8, 128), 0), _iota((8, 128), 1)
        upd = jnp.zeros((8, 128), F32)
        for p in range(len(PATTERNS)):
            bk = bk_ref[p]
            for hh in range(2):
                dsum = ds_ref[p, hh * BLK:(hh + 1) * BLK, :]
                for b in range(N_BUCKETS):
                    val = jnp.sum(jnp.where(bk == b, dsum, 0.0))
                    upd = upd + jnp.where((ri == 2 * pair + hh) & (ci == b), val, 0.0)
        db_ref[...] += upd

        for raw, d_ref, w_ref, dw_ref in ((qraw_ref, dq_ref, wq_ref, dwq_ref), (kraw_ref, dk_ref, wk_ref, dwk_ref)):
            acc = jnp.zeros((1, 128), F32)
            for c in range(t // rows):
                sl = slice(c * rows, (c + 1) * rows)
                xv, dyv = raw[sl, :], d_ref[sl, :]
                rs = _pair_rstd(xv, g2_ref)
                xh = xv * rs
                gg = dyv * w_ref[...]
                mean = _hdot(gg * xh, g2_ref[...], "b") * (1.0 / HD)
                d_ref[sl, :] = rs * (gg - xh * mean)
                acc = acc + jnp.sum(dyv * xh, axis=0, keepdims=True)
            dw_ref[...] = jnp.broadcast_to(acc, (8, 128))

    col = lambda off: pl.BlockSpec((t, 128), lambda i, off=off: (0, off + i))
    acc8 = pl.BlockSpec((8, 128), lambda i: (0, i))
    return pl.pallas_call(
        body, name="att_bwd", grid=(ATT_HEADS // 2,),
        in_specs=[BIAS_SPEC, pl.BlockSpec(bk.shape, lambda i: (0, 0, 0)),
                  col(0), col(4), col(8), col(0), col(0), col(0), PAIR_ROW_SPEC, PAIR_ROW_SPEC,
                  pl.BlockSpec((128, 128), lambda i: (0, 0))],
        out_specs=[col(0), col(0), col(0), pl.BlockSpec((8, 128), lambda i: (0, 0)), acc8, acc8],
        out_shape=[jax.ShapeDtypeStruct((t, 512), F32)] * 3 + [jax.ShapeDtypeStruct((8, 128), F32)]
                  + [jax.ShapeDtypeStruct((8, 512), F32)] * 2,
        scratch_shapes=[pltpu.VMEM((len(PATTERNS), 2 * BLK, 2 * BLK), F32)] + [pltpu.VMEM((t, 128), F32)] * 2,
        compiler_params=_params(("arbitrary",)),
    )(bias, bk, qkv, qkv, qkv, do, lse, dd, wq, wk, g2)


def _dn_scan_bwd(qkv, ba, do, sh, th, arow, dtb):
    t = qkv.shape[0]
    n = t // CHUNK
    c = CHUNK
    cps = SCAN_CHUNKS

    def body(qkv_ref, ba_ref, do_ref, sh_ref, th_ref, arow_ref, dtb_ref, dqkv_ref, dba_ref, ds_ref):
        @pl.when(pl.program_id(0) == 0)
        def _():
            ds_ref[...] = jnp.zeros_like(ds_ref)

        hs = range(DN_HEADS)
        chains = [(j, h) for j in range(cps) for h in hs]
        lane = _iota((c, 128), 1)
        row = _iota((c, 1), 0)
        ms = {}
        for j in range(cps):
            rows_j = slice(j * c, (j + 1) * c)
            mj, ri, ci = _chunk_common(qkv_ref[rows_j, :], ba_ref[rows_j, :], arow_ref[...], dtb_ref[...])
            for h in hs:
                ms[j, h] = mj[h]
        q, k, v = ({x: ms[x][nm] for x in chains} for nm in ("q", "k", "v"))
        beta, decay = ({x: ms[x][nm] for x in chains} for nm in ("beta", "decay"))
        eg, egl, etail = ({x: ms[x][nm] for x in chains} for nm in ("eg", "egl", "etail"))
        s = {x: sh_ref[x[0], x[1]] for x in chains}
        tinv = {x: th_ref[x[0], x[1]] for x in chains}
        d_o = {(j, h): do_ref[j * c:(j + 1) * c, h * DK:(h + 1) * DK] for j, h in chains}
        kb = {x: k[x] * beta[x] for x in chains}
        vb = {x: v[x] * beta[x] for x in chains}
        kbg = {x: kb[x] * eg[x] for x in chains}
        amat = {x: jnp.where(ri > ci, _dot_nt(kb[x], k[x]) * decay[x], 0.0) for x in chains}
        attn = {x: jnp.where(ri >= ci, _dot_nt(q[x], k[x]) * decay[x], 0.0) for x in chains}
        uw = {x: _hdot(tinv[x], jnp.concatenate([vb[x], kbg[x]], axis=1)) for x in chains}
        u = {x: uw[x][:, :DK] for x in chains}
        w = {x: uw[x][:, DK:] for x in chains}
        v_new = {x: u[x] - _dot(w[x], s[x]) for x in chains}
        q_dec = {x: q[x] * eg[x] for x in chains}
        k_tail = {x: k[x] * etail[x] for x in chains}
        d_attn = {x: jnp.where(ri >= ci, _dot_nt(d_o[x], v_new[x]), 0.0) for x in chains}
        d_qdec = {x: _dot_nt(d_o[x], s[x]) for x in chains}
        from_o = {x: _dot_tn(attn[x], d_o[x]) for x in chains}
        to_state = {x: _dot_tn(q_dec[x], d_o[x]) for x in chains}

        d_s, d_vnew = {}, {}
        cur = [ds_ref[h] for h in hs]
        for j in reversed(range(cps)):
            for h in hs:
                d_s[j, h] = cur[h]
                d_vnew[j, h] = from_o[j, h] + _dot(k_tail[j, h], cur[h])
            cur = [to_state[j, h] + cur[h] * egl[j, h] - _dot_tn(w[j, h], d_vnew[j, h]) for h in hs]
        for h in hs:
            ds_ref[h] = cur[h]

        d_ktail = {x: _dot_nt(v_new[x], d_s[x]) for x in chains}
        d_gl = {x: jnp.sum(s[x] * d_s[x]) * egl[x] for x in chains}
        d_w = {x: -_dot_nt(d_vnew[x], s[x]) for x in chains}
        d_both = {x: _hdot_tn(tinv[x], jnp.concatenate([d_vnew[x], d_w[x]], axis=1)) for x in chains}
        d_vb = {x: d_both[x][:, :DK] for x in chains}
        d_kbg = {x: d_both[x][:, DK:] for x in chains}
        d_a = {x: -jnp.where(ri > ci, _hdot_nt(d_both[x], uw[x]), 0.0) for x in chains}
        d_qk = {x: d_attn[x] * decay[x] for x in chains}
        d_kk = {x: d_a[x] * decay[x] for x in chains}
        d_kb = {x: _dot(d_kk[x], k[x]) + d_kbg[x] * eg[x] for x in chains}
        d_q = {x: _dot(d_qk[x], k[x]) + d_qdec[x] * eg[x] for x in chains}
        d_k = {x: _dot_tn(d_qk[x], q[x]) + _dot_tn(d_kk[x], kb[x]) + d_ktail[x] * etail[x] + d_kb[x] * beta[x]
               for x in chains}
        d_beta = {x: jnp.sum(d_kb[x] * k[x] + d_vb[x] * v[x], axis=1, keepdims=True) for x in chains}
        mm = {x: d_a[x] * amat[x] + d_attn[x] * attn[x] for x in chains}
        for j in range(cps):
            rows_j = slice(j * c, (j + 1) * c)
            rows = jnp.zeros((c, c), F32)
            for h in hs:
                rows = rows + jnp.where(ri == h, jnp.sum(mm[j, h], axis=0, keepdims=True), 0.0)
            cols_t = jnp.concatenate([rows, jnp.zeros((c, c), F32)], axis=1).T[:c, :]
            d_gc_all = jnp.zeros((c, 128), F32)
            for h in hs:
                x = (j, h)
                tail_term = jnp.sum(d_ktail[x] * k_tail[x], axis=1, keepdims=True)
                d_gc = (jnp.sum(mm[x], axis=1, keepdims=True) - _lane_col(cols_t, h)
                        + jnp.sum(d_qdec[x] * q_dec[x] + d_kbg[x] * kbg[x], axis=1, keepdims=True) - tail_term)
                d_gc = d_gc + jnp.where(row == c - 1, jnp.sum(tail_term) + d_gl[x], 0.0)
                d_gc_all = d_gc_all + jnp.where(lane == DN_HEADS + h, d_gc, 0.0)
            d_g_all = _hdot((ri <= ci).astype(F32), d_gc_all, "a")
            dba = jnp.zeros((c, 128), F32)
            for h in hs:
                x = (j, h)
                d_g = _lane_col(d_g_all, DN_HEADS + h)
                d_braw = d_beta[x] * beta[x] * (1.0 - beta[x])
                d_araw = d_g * ms[x]["a_h"] * _sigmoid(ms[x]["a_raw"] + ms[x]["dt_h"])
                dba = dba + jnp.where(lane == h, d_braw, 0.0) + jnp.where(lane == DN_HEADS + h, d_araw, 0.0) \
                    + jnp.where(lane == 2 * DN_HEADS + h, d_g * ms[x]["g"], 0.0)
                dqkv_ref[rows_j, h * DK:(h + 1) * DK] = d_q[x]
                dqkv_ref[rows_j, D_DN + h * DK:D_DN + (h + 1) * DK] = d_k[x]
                dqkv_ref[rows_j, 2 * D_DN + h * DK:2 * D_DN + (h + 1) * DK] = d_vb[x] * beta[x]
            dba_ref[rows_j, :] = dba

    nsteps = n // cps
    rev = lambda w_: pl.BlockSpec((cps * c, w_), lambda i: (nsteps - 1 - i, 0))
    one = pl.BlockSpec((1, 128), lambda i: (0, 0))
    return pl.pallas_call(
        body, name="dn_scan_bwd", grid=(nsteps,),
        in_specs=[rev(1536), rev(128), rev(512),
                  pl.BlockSpec((cps, DN_HEADS, DK, DK), lambda i: (nsteps - 1 - i, 0, 0, 0)),
                  pl.BlockSpec((cps, DN_HEADS, c, c), lambda i: (nsteps - 1 - i, 0, 0, 0)), one, one],
        out_specs=[rev(1536), rev(128)],
        out_shape=[jax.ShapeDtypeStruct((t, 1536), F32), jax.ShapeDtypeStruct((t, 128), F32)],
        scratch_shapes=[pltpu.VMEM((DN_HEADS, DK, DK), F32)],
        compiler_params=_params(("arbitrary",)),
    )(qkv, ba, do, sh, th, arow, dtb)


def _dn_prep_bwd(pdn, cw, dact):
    t = pdn.shape[0]
    nchunk = t // CONV_ROWS

    def body(u_ref, w_ref, d_ref, du_ref, dw_ref, dy_ref):
        j = pl.program_id(0)
        dy_ref[t:t + 8, :] = jnp.zeros((8, 128), F32)
        dw = [jnp.zeros((1, 128), F32) for _ in range(4)]
        for c in range(nchunk):
            sl = slice(c * CONV_ROWS, (c + 1) * CONV_ROWS)
            taps, y = _conv_taps(u_ref, c, w_ref)
            a, da_dy = _silu_and_grad(y)
            dout = d_ref[sl, :]
            rs = lax.rsqrt(jnp.sum(a * a, axis=1, keepdims=True) + EPS)
            f = jnp.where(j < 8, rs, 1.0) * jnp.where(j < 4, DK ** -0.5, 1.0)
            corr = jnp.where(j < 8, f * rs * rs * jnp.sum(dout * a, axis=1, keepdims=True), 0.0)
            dy = (f * dout - corr * a) * da_dy
            dy_ref[sl, :] = dy
            for k_ in range(4):
                dw[3 - k_] = dw[3 - k_] + jnp.sum(taps[k_] * dy, axis=0, keepdims=True)
        for i in range(4):
            dw_ref[i:i + 1, :] = dw[i]
        for c in range(nchunk):
            r0 = c * CONV_ROWS
            ext = dy_ref[r0:r0 + CONV_ROWS + 8, :]
            du = ext[:CONV_ROWS, :] * w_ref[3:4, :]
            for k_ in (1, 2, 3):
                du = du + pltpu.roll(ext, CONV_ROWS + 8 - k_, 0)[:CONV_ROWS, :] * w_ref[3 - k_:4 - k_, :]
            du_ref[r0:r0 + CONV_ROWS, :] = du

    return pl.pallas_call(
        body, name="dn_prep_bwd", grid=(12,),
        in_specs=[pl.BlockSpec((t, 128), lambda j: (0, j)), pl.BlockSpec((4, 128), lambda j: (0, j)),
                  pl.BlockSpec((t, 128), lambda j: (0, j))],
        out_specs=[pl.BlockSpec((t, 128), lambda j: (0, j)), pl.BlockSpec((4, 128), lambda j: (0, j))],
        out_shape=[jax.ShapeDtypeStruct((t, 1536), F32), jax.ShapeDtypeStruct((4, 1536), F32)],
        scratch_shapes=[pltpu.VMEM((t + 8, 128), F32)],
        compiler_params=_params(("arbitrary",)),
    )(pdn, cw, dact)


SECTIONS = (("dn", 0, 1536), ("z", 1536, 512), ("q", 2048, 512), ("k", 2560, 512), ("v", 3072, 512),
            ("gate", 3584, 512), ("ba", 4096, 128))


def _inproj_bwd(x, nw, wt, dy, dsecs, partials):
    t = x.shape[0]
    tm = 256
    npart = len(partials)
    nsteps = t // tm

    nsec = len(SECTIONS)

    def body(x_ref, nw_ref, w_ref, dy_ref, *rest):
        sec_refs, rest = rest[:nsec], rest[nsec:]
        part_refs, (gx_ref, dnw_ref, cs_ref) = rest[:npart], rest[npart:npart + 3]
        got_refs, (send, recv, loc) = rest[npart + 3:2 * npart + 3], rest[2 * npart + 3:]
        starts, waits = _chip_swap_copies(part_refs, got_refs, send, recv, loc)

        @pl.when(pl.program_id(0) == 0)
        def _():
            for start in starts:
                start()
            dnw_ref[...] = jnp.zeros_like(dnw_ref)
            cs_ref[...] = jnp.zeros_like(cs_ref)

        @pl.when(pl.program_id(0) == nsteps - 1)
        def _():
            for wait in waits:
                wait()

        dh = jnp.zeros((tm, D_MODEL), F32)
        for ref, (_, lo, width) in zip(sec_refs, SECTIONS):
            dh = dh + jnp.dot(ref[...].astype(MXU), w_ref[lo:lo + width, :], preferred_element_type=F32)
        xv = x_ref[...]
        rstd = lax.rsqrt(jnp.mean(xv * xv, axis=-1, keepdims=True) + EPS)
        xh = xv * rstd
        gg = dh * nw_ref[...]
        gx_ref[...] = rstd * (gg - xh * jnp.mean(gg * xh, axis=-1, keepdims=True)) + dy_ref[...]
        dnw_ref[...] += jnp.broadcast_to(jnp.sum(dh * xh, axis=0, keepdims=True), (8, D_MODEL))
        cs_ref[...] += jnp.broadcast_to(jnp.sum(sec_refs[nsec - 1][...], axis=0, keepdims=True), (8, 128))

    row = lambda n: pl.BlockSpec((tm, n), lambda i: (i, 0))
    full = lambda a: pl.BlockSpec(a.shape, lambda i: (0,) * a.ndim)
    res = pl.pallas_call(
        body, name="inproj_bwd", grid=(nsteps,),
        in_specs=[row(D_MODEL), full(nw), full(wt), row(D_MODEL)] + [row(width) for _, _, width in SECTIONS]
                 + [ANY_SPEC] * npart,
        out_specs=[row(D_MODEL), pl.BlockSpec((8, D_MODEL), lambda i: (0, 0)), pl.BlockSpec((8, 128), lambda i: (0, 0))]
                  + [ANY_SPEC] * npart,
        out_shape=[jax.ShapeDtypeStruct((t, D_MODEL), F32), jax.ShapeDtypeStruct((8, D_MODEL), F32),
                   jax.ShapeDtypeStruct((8, 128), F32)] + [jax.ShapeDtypeStruct(p.shape, p.dtype) for p in partials],
        scratch_shapes=[pltpu.SemaphoreType.DMA((npart, 3)), pltpu.SemaphoreType.DMA((npart, 3)),
                        pltpu.SemaphoreType.DMA((npart,))],
        compiler_params=_params(("arbitrary",)),
    )(x, nw, wt, dy, *dsecs, *partials)
    return res[0], res[1], res[2], res[3:]


def _adamw_sum(w, gs, m, v, name):
    r, c = w.shape
    nsum = gs.shape[0]
    tr = r if r <= 256 else 256
    c1 = 1.0 - ADAM_B1 ** ADAM_STEP
    c2 = 1.0 - ADAM_B2 ** ADAM_STEP

    def body(w_ref, g_ref, m_ref, v_ref, go_ref, d_ref, mo_ref, vo_ref):
        g = g_ref[0].astype(F32)
        for s in range(1, nsum):
            g = g + g_ref[s].astype(F32)
        mn = ADAM_B1 * m_ref[...] + (1.0 - ADAM_B1) * g
        vn = ADAM_B2 * v_ref[...] + (1.0 - ADAM_B2) * (g * g)
        go_ref[...] = g
        mo_ref[...] = mn
        vo_ref[...] = vn
        d_ref[...] = -ADAM_LR * ((mn / c1) / (jnp.sqrt(vn / c2) + ADAM_EPS) + ADAM_WD * w_ref[...])

    blk = pl.BlockSpec((tr, c), lambda i: (i, 0))
    return pl.pallas_call(
        body, name=name, grid=(r // tr,),
        in_specs=[blk, pl.BlockSpec((nsum, tr, c), lambda i: (0, i, 0)), blk, blk],
        out_specs=[blk] * 4, out_shape=[jax.ShapeDtypeStruct((r, c), F32)] * 4,
        compiler_params=_params(("arbitrary",)),
    )(w, gs, m, v)


def _local_step(x, target, h, ht, w_sect, conv_w, a_log, dt_bias, dn_norm_w, q_norm_w, k_norm_w, rel_bias, wout_shard):
    arow = jnp.zeros((1, 128), F32).at[0, DN_HEADS:2 * DN_HEADS].set(-jnp.exp(a_log[0]))
    dtb = jnp.zeros((1, 128), F32).at[0, DN_HEADS:2 * DN_HEADS].set(dt_bias[0])
    g_np, gt_np = _group_mats()
    g, gt = jnp.asarray(g_np), jnp.asarray(gt_np)
    g2 = jnp.asarray(np.kron(np.eye(2, dtype=np.float32), np.ones((HD, HD), np.float32)))
    bk = jnp.asarray(_bucket_tables())
    wq = jnp.tile(q_norm_w, (1, ATT_HEADS)) * (HD ** -0.5)
    wk = jnp.tile(k_norm_w, (1, ATT_HEADS))

    pdn, qkv_dn, z, patt, gate, ba, wout8 = _inproj(h, w_sect, conv_w, wout_shard)
    w_out = wout8.reshape(D_MODEL, D_MODEL)
    oraw, ydn, sh, th = _dn_scan_fwd(qkv_dn, ba, z, arow, dtb, dn_norm_w)
    bias = _bias_tables(rel_bias, bk)
    oatt, yatt, lse = _att_fwd(patt, gate, bias, wq, wk, g2)
    dy, mix_t, loss8 = _outproj_loss(x, ydn, yatt, w_out, target)

    do_dn, dz, do_att, dgate, dd, ddnw = _outproj_bwd(dy, w_out.T, oraw, z, dn_norm_w, oatt, gate, g, gt)
    d_wout = _grad_matmul(mix_t, dy, "dw_out")
    dq, dk, dv, drb, dwq8, dwk8 = _att_bwd(patt, do_att, lse, dd, bias, bk, wq, wk, g2)
    dqkv_dn, dba = _dn_scan_bwd(qkv_dn, ba, do_dn, sh, th, arow, dtb)
    dpdn, d_conv = _dn_prep_bwd(pdn, conv_w, dqkv_dn)
    dsecs = (dpdn, dz, dq, dk, dv, dgate, dba)
    dw_sections = [_grad_matmul(ht, d_, "dw_in_" + nm) for d_, (nm, _, _) in zip(dsecs, SECTIONS)]
    return dict(w_in_sections=dw_sections, conv_w=d_conv, w_out=d_wout, dy=dy, dsecs=dsecs,
                small_parts=(loss8, ddnw, dwq8, dwk8, drb))


def _finish_step(x, norm_w, w_sect_t, gr, partials):
    grad_x, dnw8, cs8, got = _inproj_bwd(x, norm_w, w_sect_t, gr["dy"], gr["dsecs"], partials)
    return grad_x, _pack_small_grads(dnw8, cs8, *gr["small_parts"]), got


SMALL_ROWS = 24
SMALL_AT = dict(a_log=(slice(8, 9), slice(0, 4)), dt_bias=(slice(9, 10), slice(0, 4)),
                dn_norm_w=(slice(10, 11), slice(0, 128)), q_norm_w=(slice(11, 12), slice(0, HD)),
                k_norm_w=(slice(12, 13), slice(0, HD)), rel_bias=(slice(16, 24), slice(0, N_BUCKETS)))
SMALL_NAMES = ("norm_w", "a_log", "dt_bias", "dn_norm_w", "q_norm_w", "k_norm_w", "rel_bias")


LOSS_ROW = 13


def _pack_small_grads(dnw8, cs8, loss8, ddnw8, dwq8, dwk8, drb):
    def body(dnw_ref, cs_ref, loss_ref, ddnw_ref, dwq_ref, dwk_ref, drb_ref, o_ref):
        lane = _iota((8, 128), 1)
        o_ref[...] = jnp.zeros_like(o_ref)
        o_ref[LOSS_ROW:LOSS_ROW + 1, :] = jnp.where(lane == 0, loss_ref[...], 0.0)[0:1, :]
        for k in range(D_MODEL // 128):
            o_ref[k:k + 1, :] = dnw_ref[0:1, k * 128:(k + 1) * 128]
        cs = cs_ref[...]
        o_ref[8:9, :] = jnp.where(lane < DN_HEADS, pltpu.roll(cs, 128 - 2 * DN_HEADS, 1), 0.0)[0:1, :]
        o_ref[9:10, :] = jnp.where(lane < DN_HEADS, pltpu.roll(cs, 128 - DN_HEADS, 1), 0.0)[0:1, :]
        o_ref[10:11, :] = ddnw_ref[0:1, :]
        for row, ref, scale in ((11, dwq_ref, HD ** -0.5), (12, dwk_ref, 1.0)):
            acc = ref[:, 0:128] + ref[:, 128:256] + ref[:, 256:384] + ref[:, 384:512]
            acc = (acc + pltpu.roll(acc, HD, 1)) * scale
            o_ref[row:row + 1, :] = jnp.where(lane < HD, acc, 0.0)[0:1, :]
        o_ref[16:24, :] = drb_ref[...]

    return pl.pallas_call(body, name="pack_small_grads", out_shape=jax.ShapeDtypeStruct((SMALL_ROWS, 128), F32),
                          )(dnw8, cs8, loss8, ddnw8, dwq8, dwk8, drb)


def _adam_math(w, g, m, v):
    c1 = 1.0 - ADAM_B1 ** ADAM_STEP
    c2 = 1.0 - ADAM_B2 ** ADAM_STEP
    mn = ADAM_B1 * m + (1.0 - ADAM_B1) * g
    vn = ADAM_B2 * v + (1.0 - ADAM_B2) * (g * g)
    return -ADAM_LR * ((mn / c1) / (jnp.sqrt(vn / c2) + ADAM_EPS) + ADAM_WD * w), mn, vn


def _adamw_small(gs, ws, ms, vs):
    n = len(SMALL_NAMES)

    def body(g_ref, *refs):
        w_refs, m_refs, v_refs = refs[:n], refs[n:2 * n], refs[2 * n:3 * n]
        outs, loss_ref = refs[3 * n:7 * n], refs[7 * n]
        loss = g_ref[0, LOSS_ROW:LOSS_ROW + 1, :]
        for s in range(1, gs.shape[0]):
            loss = loss + g_ref[s, LOSS_ROW:LOSS_ROW + 1, :]
        loss_ref[...] = loss

        def one(i, rows, lanes, at):
            g = g_ref[0, rows, lanes]
            for s in range(1, gs.shape[0]):
                g = g + g_ref[s, rows, lanes]
            d, mn, vn = _adam_math(w_refs[i][at], g, m_refs[i][at], v_refs[i][at])
            for kind, val in enumerate((g, d, mn, vn)):
                outs[kind * n + i][at] = val

        for k in range(D_MODEL // 128):
            one(0, slice(k, k + 1), slice(0, 128), (slice(0, 1), slice(k * 128, (k + 1) * 128)))
        for i, nm in enumerate(SMALL_NAMES[1:], start=1):
            rows, lanes = SMALL_AT[nm]
            one(i, rows, lanes, (slice(None), slice(None)))

    shapes = [jax.ShapeDtypeStruct(w.shape, F32) for w in ws]
    res = pl.pallas_call(body, name="adamw_small",
                         out_shape=shapes * 4 + [jax.ShapeDtypeStruct((1, 128), F32)])(gs, *ws, *ms, *vs)
    return [res[k * n:(k + 1) * n] for k in range(4)], res[4 * n]


def kernel(x, norm_w, w_in, conv_w, a_log, dt_bias, dn_norm_w, q_norm_w, k_norm_w, rel_bias, w_out, loss_target, m_norm_w, m_w_in, m_conv_w, m_a_log, m_dt_bias, m_dn_norm_w, m_q_norm_w, m_k_norm_w, m_rel_bias, m_w_out, v_norm_w, v_w_in, v_conv_w, v_a_log, v_dt_bias, v_dn_norm_w, v_q_norm_w, v_k_norm_w, v_rel_bias, v_w_out):
    assert w_in.shape[2] == SHARD_COLS
    h, ht, (win8, conv8) = _norm_and_gather(x[0], norm_w, [w_in[0].astype(MXU), conv_w[0]])
    w_sect, w_sect_t = _build_w(win8)
    conv_full = conv8.transpose(1, 0, 2).reshape(4, 3 * D_DN)

    gr = _local_step(x[0], loss_target[0], h, ht, w_sect, conv_full, a_log, dt_bias, dn_norm_w, q_norm_w,
                     k_norm_w, rel_bias, w_out[0].astype(MXU))

    slabs = [_build_slabs(gr["w_in_sections"]),
             gr["w_out"].reshape(4, 2, D_MODEL // N_DEV, D_MODEL).transpose(1, 0, 2, 3),
             gr["conv_w"].reshape(4, 4, 2, 3 * D_DN // N_DEV).transpose(2, 1, 0, 3)]
    core = lax.axis_index("c").astype(jnp.int32).reshape(1)
    from_sibling = _swap_siblings(slabs)
    wires = (GRAD_WIRE, GRAD_WIRE, F32)
    partial = [_chip_sum(slabs[i], from_sibling[i], core, wires[i], "chip_sum_%d" % i) for i in range(3)]
    grad_x, small_pack, (r_win, r_wout, r_conv) = _finish_step(x[0], norm_w, w_sect_t, gr, partial)
    r_small = _share_small(small_pack)

    g_win, d_win, m_win, v_win = _adamw_sum(w_in[0], r_win, m_w_in[0], v_w_in[0], "adamw_w_in")
    g_wout, d_wout, m_wout, v_wout = _adamw_sum(w_out[0], r_wout, m_w_out[0], v_w_out[0], "adamw_w_out")
    g_conv, d_conv, m_conv, v_conv = _adamw_sum(conv_w[0], r_conv, m_conv_w[0], v_conv_w[0], "adamw_conv_w")
    small, loss_row = _adamw_small(r_small,
                                   (norm_w, a_log, dt_bias, dn_norm_w, q_norm_w, k_norm_w, rel_bias),
                                   (m_norm_w, m_a_log, m_dt_bias, m_dn_norm_w, m_q_norm_w, m_k_norm_w, m_rel_bias),
                                   (v_norm_w, v_a_log, v_dt_bias, v_dn_norm_w, v_q_norm_w, v_k_norm_w, v_rel_bias))

    loss = loss_row[0, 0]
    names = ("norm_w", "w_in", "conv_w", "a_log", "dt_bias", "dn_norm_w", "q_norm_w", "k_norm_w", "rel_bias", "w_out")
    big = dict(w_in=(g_win, d_win, m_win, v_win), conv_w=(g_conv, d_conv, m_conv, v_conv),
               w_out=(g_wout, d_wout, m_wout, v_wout))
    outs = [loss, grad_x[None]]
    for kind in range(4):
        for nm in names:
            outs.append(big[nm][kind][None] if nm in big else small[kind][SMALL_NAMES.index(nm)])
    return tuple(outs)
```

```python
import math

import numpy as np
import jax
import jax.numpy as jnp
from jax import lax
from jax.experimental import pallas as pl
from jax.experimental.pallas import tpu as pltpu

F32 = jnp.float32
MXU = jnp.bfloat16
GRAD_WIRE = jnp.bfloat16

D_MODEL = 1024
D_DN = 512
DN_HEADS = 4
DK = 128
CHUNK = 64
D_ATT = 512
ATT_HEADS = 8
HD = 64
PATTERNS = ((128, 1), (512, 4), (2048, 16))
BLK = 128
N_BUCKETS = 32
MAX_DISTANCE = 2048
EPS = 1e-6
W_COLS = 4224
N_DEV = 8
AXES = ("x", "y", "c")

ADAM_LR = 0.001
ADAM_B1 = 0.9
ADAM_B2 = 0.999
ADAM_EPS = 1e-08
ADAM_WD = 0.01
ADAM_STEP = 10

VMEM_LIMIT = 56 * 1024 * 1024
NEG = -1e30


def _dot(a, b):
    return jnp.dot(a.astype(MXU), b.astype(MXU), preferred_element_type=F32)


def _dot_nt(a, b):
    return lax.dot_general(a.astype(MXU), b.astype(MXU), (((1,), (1,)), ((), ())), preferred_element_type=F32)


def _dot_tn(a, b):
    return lax.dot_general(a.astype(MXU), b.astype(MXU), (((0,), (0,)), ((), ())), preferred_element_type=F32)


def _split(a):
    hi = a.astype(jnp.bfloat16)
    return hi, (a - hi.astype(F32)).astype(jnp.bfloat16)


def _dot_split(a, b, dims, exact):
    dg = lambda u, v: lax.dot_general(u, v, (dims, ((), ())), preferred_element_type=F32)
    if exact == "b":
        ah, al = _split(a)
        bh = b.astype(jnp.bfloat16)
        return dg(ah, bh) + dg(al, bh)
    if exact == "a":
        bh, bm = _split(b)
        bl = (b - bh.astype(F32) - bm.astype(F32)).astype(jnp.bfloat16)
        ah = a.astype(jnp.bfloat16)
        return dg(ah, bh) + (dg(ah, bm) + dg(ah, bl))
    ah, al = _split(a)
    bh, bl = _split(b)
    return dg(ah, bh) + (dg(ah, bl) + dg(al, bh))


def _wy_inverses(amat, eye):
    tinv = {x: eye - amat[x] for x in amat}
    pw = amat
    for _ in range(5):
        pw = {x: _hdot(pw[x], pw[x]) for x in amat}
        tinv = {x: tinv[x] + _hdot(tinv[x], pw[x]) for x in amat}
    return tinv


def _hdot(a, b, exact=None):
    return _dot_split(a, b, ((1,), (0,)), exact)


def _hdot_nt(a, b, exact=None):
    return _dot_split(a, b, ((1,), (1,)), exact)


def _hdot_tn(a, b, exact=None):
    return _dot_split(a, b, ((0,), (0,)), exact)


def _sigmoid(x):
    return 1.0 / (1.0 + jnp.exp(-x))


def _silu(x):
    return x * _sigmoid(x)


def _silu_and_grad(x):
    s = _sigmoid(x)
    return x * s, s * (1.0 + x * (1.0 - s))


def _softplus(x):
    return jnp.maximum(x, 0.0) + jnp.log(1.0 + jnp.exp(-jnp.abs(x)))


def _iota(shape, dim):
    return lax.broadcasted_iota(jnp.int32, shape, dim)


def _lane_col(x, k):
    return jnp.sum(jnp.where(_iota(x.shape, 1) == k, x, 0.0), axis=1, keepdims=True)


def _params(sem=None):
    return pltpu.CompilerParams(dimension_semantics=sem, vmem_limit_bytes=VMEM_LIMIT)


def _t5_bucket(dist):
    max_exact = N_BUCKETS // 2
    d = np.maximum(dist, 1).astype(np.float64)
    large = max_exact + (np.log(d / max_exact) / math.log(MAX_DISTANCE / max_exact)
                         * (N_BUCKETS - max_exact)).astype(np.int32)
    large = np.minimum(large, N_BUCKETS - 1)
    return np.where(dist < max_exact, dist, large).astype(np.int32)


def _bucket_tables():
    qi = np.arange(BLK)[:, None]
    kj = np.arange(2 * BLK)[None, :]
    step = qi - kj + BLK
    band = (step >= 0) & (step <= BLK)
    out = []
    for _, r in PATTERNS:
        b = _t5_bucket(np.clip(step, 0, None) * r)
        out.append(np.where(band, b, -1))
    return np.stack(out).astype(np.int32)


def _group_mats():
    g = np.zeros((D_ATT, 128), np.float32)
    for h in range(ATT_HEADS):
        g[h * HD:(h + 1) * HD, h] = 1.0
    return g, np.ascontiguousarray(g.T)


CHIP_FLIPS = ((1, 0), (0, 1), (1, 1))
ANY_SPEC = pl.BlockSpec(memory_space=pl.ANY)
MESH_ID = pl.DeviceIdType.MESH


def _other_chips():
    x, y = lax.axis_index("x"), lax.axis_index("y")
    return [((1 - x if fx else x), (1 - y if fy else y)) for fx, fy in CHIP_FLIPS]


def _gather_plan(ins, outs, send, recv, loc):
    n = len(ins)
    x, y, c = (lax.axis_index(a) for a in AXES)
    sib = (x, y, 1 - c)
    chips = _other_chips()
    lin = lambda px, py, pc: 4 * px + 2 * py + pc

    def copy(a, k, block, to, src=None):
        slot = outs[a].at[lin(*block)]
        return pltpu.make_async_remote_copy(src_ref=slot if src is None else src, dst_ref=slot,
                                            send_sem=send.at[a, k], recv_sem=recv.at[a, k],
                                            device_id=to, device_id_type=MESH_ID)

    mine = [pltpu.make_async_copy(ins[a], outs[a].at[lin(x, y, c)], loc.at[a]) for a in range(n)]
    firsts = []
    for a in range(n):
        firsts.append(copy(a, 0, (x, y, c), sib, src=ins[a]))
        firsts += [copy(a, 1 + j, (x, y, c), (*chip, c), src=ins[a]) for j, chip in enumerate(chips)]

    def begin():
        for cp in mine + firsts:
            cp.start()

    def finish():
        passed = []
        for j, chip in enumerate(chips):
            for a in range(n):
                copy(a, 1 + j, (*chip, c), (x, y, c)).wait_recv()
                fw = copy(a, 4 + j, (*chip, c), sib)
                fw.start()
                passed.append(fw)
        for a in range(n):
            copy(a, 0, sib, (x, y, c)).wait_recv()
            for j, chip in enumerate(chips):
                copy(a, 4 + j, (*chip, 1 - c), (x, y, c)).wait_recv()
        for cp in firsts + passed:
            cp.wait_send()
        for cp in mine:
            cp.wait()

    return begin, finish


GATHER_SEMS = lambda n: [pltpu.SemaphoreType.DMA((n, 7)), pltpu.SemaphoreType.DMA((n, 7)), pltpu.SemaphoreType.DMA((n,))]


def _swap_siblings(arrs):
    n = len(arrs)

    def body(*refs):
        ins, outs = refs[:n], refs[n:2 * n]
        send, recv = refs[2 * n:]
        x, y, c = (lax.axis_index(a) for a in AXES)
        cps = [pltpu.make_async_remote_copy(src_ref=ins[a].at[1 - c], dst_ref=outs[a], send_sem=send.at[a],
                                            recv_sem=recv.at[a], device_id=(x, y, 1 - c), device_id_type=MESH_ID)
               for a in range(n)]
        for cp in cps:
            cp.start()
        for cp in cps:
            cp.wait()

    return pl.pallas_call(
        body, name="swap_siblings", out_shape=[jax.ShapeDtypeStruct(a.shape[1:], a.dtype) for a in arrs],
        in_specs=[ANY_SPEC] * n, out_specs=[ANY_SPEC] * n,
        scratch_shapes=[pltpu.SemaphoreType.DMA((n,)), pltpu.SemaphoreType.DMA((n,))],
    )(*arrs)


def _chip_sum(mine2, theirs, core, wire, name):
    _, nchip, r, cdim = mine2.shape
    tr = r if r <= 1024 else 1024

    def body(core_ref, a_ref, b_ref, o_ref):
        del core_ref
        o_ref[...] = (a_ref[...].astype(F32) + b_ref[...].astype(F32)).astype(wire)

    grid_spec = pltpu.PrefetchScalarGridSpec(
        num_scalar_prefetch=1, grid=(nchip, r // tr),
        in_specs=[pl.BlockSpec((None, None, tr, cdim), lambda j, i, cr: (cr[0], j, i, 0)),
                  pl.BlockSpec((None, tr, cdim), lambda j, i, cr: (j, i, 0))],
        out_specs=pl.BlockSpec((None, tr, cdim), lambda j, i, cr: (j, i, 0)))
    return pl.pallas_call(
        body, name=name, grid_spec=grid_spec, out_shape=jax.ShapeDtypeStruct((nchip, r, cdim), wire),
        compiler_params=_params(("arbitrary", "arbitrary")),
    )(core, mine2, theirs)


def _chip_swap_copies(ins, outs, send, recv, loc):
    x, y, c = (lax.axis_index(a) for a in AXES)
    me = 2 * x + y
    starts, arrivals, drains = [], [], []
    for a in range(len(ins)):
        lc = pltpu.make_async_copy(ins[a].at[me], outs[a].at[me], loc.at[a])
        starts.append(lc.start)
        drains.append(lc.wait)
        for j, (px, py) in enumerate(_other_chips()):
            them = 2 * px + py
            cp = pltpu.make_async_remote_copy(src_ref=ins[a].at[them], dst_ref=outs[a].at[me], send_sem=send.at[a, j],
                                              recv_sem=recv.at[a, j], device_id=(px, py, c), device_id_type=MESH_ID)
            landing = pltpu.make_async_remote_copy(src_ref=ins[a].at[them], dst_ref=outs[a].at[them],
                                                   send_sem=send.at[a, j], recv_sem=recv.at[a, j],
                                                   device_id=(px, py, c), device_id_type=MESH_ID)
            starts.append(cp.start)
            arrivals.append(landing.wait_recv)
            drains.append(cp.wait_send)
    return starts, arrivals + drains


def _share_small(pack):
    def body(in_ref, out_ref, send, recv, loc):
        x, y, c = (lax.axis_index(a) for a in AXES)
        me = 4 * x + 2 * y + c
        lc = pltpu.make_async_copy(in_ref, out_ref.at[me], loc.at[0])
        lc.start()
        sends, arrivals = [], []
        for k in range(1, N_DEV):
            px = 1 - x if k & 4 else x
            py = 1 - y if k & 2 else y
            pc = 1 - c if k & 1 else c
            cp = pltpu.make_async_remote_copy(src_ref=in_ref, dst_ref=out_ref.at[me], send_sem=send.at[k - 1],
                                              recv_sem=recv.at[k - 1], device_id=(px, py, pc), device_id_type=MESH_ID)
            cp.start()
            sends.append(cp)
            arrivals.append(pltpu.make_async_remote_copy(src_ref=in_ref, dst_ref=out_ref.at[4 * px + 2 * py + pc],
                                                         send_sem=send.at[k - 1], recv_sem=recv.at[k - 1],
                                                         device_id=(px, py, pc), device_id_type=MESH_ID))
        for cp in arrivals:
            cp.wait_recv()
        for cp in sends:
            cp.wait_send()
        lc.wait()

    return pl.pallas_call(
        body, name="share_small", out_shape=jax.ShapeDtypeStruct((N_DEV,) + pack.shape, pack.dtype),
        in_specs=[ANY_SPEC], out_specs=ANY_SPEC,
        scratch_shapes=[pltpu.SemaphoreType.DMA((N_DEV - 1,)), pltpu.SemaphoreType.DMA((N_DEV - 1,)),
                        pltpu.SemaphoreType.DMA((1,))],
    )(pack)


W_PARTS = ((0, 0, 2048), (2048, 4096, 8), (2056, 2048, 2048))
SHARD_COLS = 513


def _pieces(lo, hi, parts):
    out = []
    for ref_start, tgt_start, width in parts:
        a, b = max(lo, ref_start), min(hi, ref_start + width)
        if a < b:
            out.append((a - lo, tgt_start + a - ref_start, b - a))
    return out


def _build_w(win8):
    tr = 512

    def body(in_ref, w_ref, wt_ref):
        w_ref[:, 4096:W_COLS] = jnp.zeros((tr, W_COLS - 4096), MXU)
        for p in range(N_DEV):
            for src, dst, width in _pieces(p * SHARD_COLS, (p + 1) * SHARD_COLS, W_PARTS):
                w_ref[:, dst:dst + width] = in_ref[p, :, src:src + width]
        for k in range(W_COLS // 128):
            wt_ref[k * 128:(k + 1) * 128, :] = w_ref[:, k * 128:(k + 1) * 128].astype(F32).T.astype(MXU)

    return pl.pallas_call(
        body, name="build_w", grid=(D_MODEL // tr,),
        in_specs=[pl.BlockSpec((N_DEV, tr, SHARD_COLS), lambda i: (0, i, 0))],
        out_specs=[pl.BlockSpec((tr, W_COLS), lambda i: (i, 0)), pl.BlockSpec((W_COLS, tr), lambda i: (0, i))],
        out_shape=[jax.ShapeDtypeStruct((D_MODEL, W_COLS), MXU), jax.ShapeDtypeStruct((W_COLS, D_MODEL), MXU)],
        compiler_params=_params(("arbitrary",)),
    )(win8)


def _build_slabs(secs):
    tr = 512
    parts = ((0, 0, 1536), (1536, 1, 512), (2048, 6, 8), (2056, 2, 512), (2568, 3, 512), (3080, 4, 512),
             (3592, 5, 512))

    def body(*refs):
        o_ref = refs[len(secs)]
        for p in range(N_DEV):
            lo, hi = p * SHARD_COLS, (p + 1) * SHARD_COLS
            for ref_start, idx, width in parts:
                a, b = max(lo, ref_start), min(hi, ref_start + width)
                if a < b:
                    o_ref[p % 2, p // 2, :, a - lo:b - lo] = refs[idx][:, a - ref_start:b - ref_start]

    return pl.pallas_call(
        body, name="build_slabs", grid=(D_MODEL // tr,),
        in_specs=[pl.BlockSpec((tr, s.shape[1]), lambda i: (i, 0)) for s in secs],
        out_specs=pl.BlockSpec((2, 4, tr, SHARD_COLS), lambda i: (0, 0, i, 0)),
        out_shape=jax.ShapeDtypeStruct((2, 4, D_MODEL, SHARD_COLS), secs[0].dtype),
        compiler_params=_params(("arbitrary",)),
    )(*secs)


def _norm_and_gather(x, nw, shards):
    t = x.shape[0]
    tm = 512
    nsteps = t // tm
    n = len(shards)

    def body(x_ref, nw_ref, *rest):
        ins, (h_ref, ht_ref), outs, sems = rest[:n], rest[n:n + 2], rest[n + 2:2 * n + 2], rest[2 * n + 2:]
        begin, finish = _gather_plan(ins, outs, *sems)

        @pl.when(pl.program_id(0) == 0)
        def _():
            begin()

        @pl.when(pl.program_id(0) == nsteps - 1)
        def _():
            finish()

        xv = x_ref[...]
        rstd = lax.rsqrt(jnp.mean(xv * xv, axis=-1, keepdims=True) + EPS)
        hf = xv * rstd * nw_ref[...]
        h_ref[...] = hf.astype(MXU)
        ht_ref[...] = hf.T.astype(MXU)

    res = pl.pallas_call(
        body, name="norm_and_gather", grid=(nsteps,),
        in_specs=[pl.BlockSpec((tm, D_MODEL), lambda i: (i, 0)), pl.BlockSpec(nw.shape, lambda i: (0, 0))]
                 + [ANY_SPEC] * n,
        out_specs=[pl.BlockSpec((tm, D_MODEL), lambda i: (i, 0)), pl.BlockSpec((D_MODEL, tm), lambda i: (0, i))]
                  + [ANY_SPEC] * n,
        out_shape=[jax.ShapeDtypeStruct((t, D_MODEL), MXU), jax.ShapeDtypeStruct((D_MODEL, t), MXU)]
                  + [jax.ShapeDtypeStruct((N_DEV,) + a.shape, a.dtype) for a in shards],
        scratch_shapes=GATHER_SEMS(n),
        compiler_params=_params(("arbitrary",)),
    )(x, nw, *shards)
    return res[0], res[1], res[2:]


def _inproj(h_all, w, cw, wout_shard):
    t = h_all.shape[0]
    tm = 512
    nsteps = t // tm

    def body(h_ref, w_ref, cw_ref, wo_ref, pdn_ref, qkv_ref, z_ref, patt_ref, gate_ref, ba_ref,
             wo8_ref, halo_ref, send, recv, loc):
        begin, finish = _gather_plan([wo_ref], [wo8_ref], send, recv, loc)

        @pl.when(pl.program_id(0) == 0)
        def _():
            begin()
            halo_ref[...] = jnp.zeros_like(halo_ref)

        @pl.when(pl.program_id(0) == nsteps - 1)
        def _():
            finish()

        h = h_ref[...]
        for ref, lo, hi in ((z_ref, 1536, 2048), (patt_ref, 2048, 3584), (gate_ref, 3584, 4096), (ba_ref, 4096, 4224)):
            ref[...] = jnp.dot(h, w_ref[:, lo:hi], preferred_element_type=F32)
        pdn = jnp.dot(h, w_ref[:, 0:3 * D_DN], preferred_element_type=F32)
        pdn_ref[...] = pdn
        _dn_prep_tile(pdn, halo_ref, cw_ref, qkv_ref)

    row = lambda n: pl.BlockSpec((tm, n), lambda i: (i, 0))
    full = lambda a: pl.BlockSpec(a.shape, lambda i: (0,) * a.ndim)
    return pl.pallas_call(
        body, name="inproj", grid=(nsteps,),
        in_specs=[row(D_MODEL), full(w), full(cw), ANY_SPEC],
        out_specs=[row(1536), row(1536), row(512), row(1536), row(512), row(128), ANY_SPEC],
        out_shape=[jax.ShapeDtypeStruct((t, n), F32) for n in (1536, 1536, 512, 1536, 512, 128)] +
                  [jax.ShapeDtypeStruct((N_DEV,) + wout_shard.shape, wout_shard.dtype)],
        scratch_shapes=[pltpu.VMEM((8, 3 * D_DN), F32)] + GATHER_SEMS(1),
        compiler_params=_params(("arbitrary",)),
    )(h_all, w, cw, wout_shard)


CONV_ROWS = 512


def _conv_taps(u_ref, c, w_ref):
    r0 = c * CONV_ROWS
    if c == 0:
        ext = jnp.concatenate([jnp.zeros((8, 128), F32), u_ref[0:CONV_ROWS, :]], axis=0)
    else:
        ext = u_ref[r0 - 8:r0 + CONV_ROWS, :]
    taps = [ext[8:, :]] + [pltpu.roll(ext, k, 0)[8:, :] for k in (1, 2, 3)]
    y = taps[0] * w_ref[3:4, :]
    for k in (1, 2, 3):
        y = y + taps[k] * w_ref[3 - k:4 - k, :]
    return taps, y


def _dn_prep_tile(pdn, halo_ref, cw_ref, out_ref):
    rows = pdn.shape[0]
    ext = jnp.concatenate([halo_ref[...], pdn], axis=0)
    halo_ref[...] = pdn[rows - 8:, :]
    for j in range(3 * D_DN // 128):
        cols = slice(j * 128, (j + 1) * 128)
        e = ext[:, cols]
        y = e[8:, :] * cw_ref[3:4, cols]
        for k in (1, 2, 3):
            y = y + pltpu.roll(e, k, 0)[8:, :] * cw_ref[3 - k:4 - k, cols]
        a = _silu(y)
        if j < 2 * DN_HEADS:
            a = a * lax.rsqrt(jnp.sum(a * a, axis=1, keepdims=True) + EPS)
        if j < DN_HEADS:
            a = a * DK ** -0.5
        out_ref[:, cols] = a


def _chunk_common(qkv, ba, arow, dtb):
    c = CHUNK
    ri, ci = _iota((c, c), 0), _iota((c, c), 1)
    lane = _iota((c, 128), 1)
    g_all = jnp.where((lane >= DN_HEADS) & (lane < 2 * DN_HEADS), arow * _softplus(ba + dtb), 0.0)
    gc_all = _hdot((ri >= ci).astype(F32), g_all, "a")
    gc_t = gc_all.T
    beta_all = _sigmoid(ba)
    out = []
    for h in range(DN_HEADS):
        gc = _lane_col(gc_all, DN_HEADS + h)
        gcr = gc_t[DN_HEADS + h:DN_HEADS + h + 1, :]
        gl = gc[c - 1:c, :]
        out.append(dict(
            q=qkv[:, h * DK:(h + 1) * DK], k=qkv[:, D_DN + h * DK:D_DN + (h + 1) * DK],
            v=qkv[:, 2 * D_DN + h * DK:2 * D_DN + (h + 1) * DK],
            beta=_lane_col(beta_all, h), g=_lane_col(g_all, DN_HEADS + h),
            a_raw=_lane_col(ba, DN_HEADS + h), a_h=_lane_col(arow, DN_HEADS + h), dt_h=_lane_col(dtb, DN_HEADS + h),
            decay=jnp.exp(jnp.where(ri >= ci, gc - gcr, NEG)), eg=jnp.exp(gc), egl=jnp.exp(gl), etail=jnp.exp(gl - gc)))
    return out, ri, ci


SCAN_CHUNKS = 8


def _dn_scan_fwd(qkv, ba, z, arow, dtb, dnw):
    t = qkv.shape[0]
    n = t // CHUNK
    c = CHUNK
    cps = SCAN_CHUNKS
    hs = range(DN_HEADS)
    chains = [(j, h) for j in range(cps) for h in hs]

    def body(qkv_ref, ba_ref, z_ref, arow_ref, dtb_ref, dnw_ref, o_ref, y_ref, sh_ref, th_ref, s_ref):
        @pl.when(pl.program_id(0) == 0)
        def _():
            s_ref[...] = jnp.zeros_like(s_ref)

        ms = {}
        for j in range(cps):
            rows = slice(j * c, (j + 1) * c)
            mj, ri, ci = _chunk_common(qkv_ref[rows, :], ba_ref[rows, :], arow_ref[...], dtb_ref[...])
            for h in hs:
                ms[j, h] = mj[h]
        kb = {x: ms[x]["k"] * ms[x]["beta"] for x in chains}
        amat = {x: jnp.where(ri > ci, _dot_nt(kb[x], ms[x]["k"]) * ms[x]["decay"], 0.0) for x in chains}
        attn = {x: jnp.where(ri >= ci, _dot_nt(ms[x]["q"], ms[x]["k"]) * ms[x]["decay"], 0.0) for x in chains}
        tinv = _wy_inverses(amat, (ri == ci).astype(F32))
        uw = {x: _hdot(tinv[x], jnp.concatenate([ms[x]["v"] * ms[x]["beta"], kb[x] * ms[x]["eg"]], axis=1))
              for x in chains}
        u = {x: uw[x][:, :DK] for x in chains}
        w = {x: uw[x][:, DK:] for x in chains}
        q_dec = {x: ms[x]["q"] * ms[x]["eg"] for x in chains}
        k_tail = {x: ms[x]["k"] * ms[x]["etail"] for x in chains}
        s = [s_ref[h] for h in hs]
        for j in range(cps):
            rows = slice(j * c, (j + 1) * c)
            v_new = [u[j, h] - _dot(w[j, h], s[h]) for h in hs]
            o = [_dot(q_dec[j, h], s[h]) + _dot(attn[j, h], v_new[h]) for h in hs]
            for h in hs:
                sh_ref[j, h] = s[h]
                th_ref[j, h] = tinv[j, h]
            s = [s[h] * ms[j, h]["egl"] + _dot_tn(k_tail[j, h], v_new[h]) for h in hs]
            for h in hs:
                cols = slice(h * DK, (h + 1) * DK)
                o_ref[rows, cols] = o[h]
                rs = lax.rsqrt(jnp.mean(o[h] * o[h], axis=1, keepdims=True) + EPS)
                y_ref[rows, cols] = o[h] * rs * dnw_ref[...] * _silu(z_ref[rows, cols])
        for h in hs:
            s_ref[h] = s[h]

    row = lambda w_: pl.BlockSpec((cps * c, w_), lambda i: (i, 0))
    one = pl.BlockSpec((1, 128), lambda i: (0, 0))
    return pl.pallas_call(
        body, name="dn_scan_fwd", grid=(n // cps,),
        in_specs=[row(1536), row(128), row(512), one, one, one],
        out_specs=[row(512), row(512), pl.BlockSpec((cps, DN_HEADS, DK, DK), lambda i: (i, 0, 0, 0)),
                   pl.BlockSpec((cps, DN_HEADS, c, c), lambda i: (i, 0, 0, 0))],
        out_shape=[jax.ShapeDtypeStruct((t, 512), F32), jax.ShapeDtypeStruct((t, 512), F32),
                   jax.ShapeDtypeStruct((n, DN_HEADS, DK, DK), F32), jax.ShapeDtypeStruct((n, DN_HEADS, c, c), F32)],
        scratch_shapes=[pltpu.VMEM((DN_HEADS, DK, DK), F32)],
        compiler_params=_params(("arbitrary",)),
    )(qkv, ba, z, arow, dtb, dnw)


ATT_ROWS = 512


def _pair_rstd(xv, g2_ref):
    return lax.rsqrt(_hdot(xv * xv, g2_ref[...], "b") * (1.0 / HD) + EPS)


def _pair_norm(t, raw_refs, w_refs, out_refs, g2_ref):
    for c in range(t // ATT_ROWS):
        sl = slice(c * ATT_ROWS, (c + 1) * ATT_ROWS)
        for raw, w_ref, out in zip(raw_refs, w_refs, out_refs):
            xv = raw[sl, :]
            out[sl, :] = xv * _pair_rstd(xv, g2_ref) * w_ref[...]


def _bias_tables(rb, bk):
    def body(rb_ref, bk_ref, bias_ref):
        pair = pl.program_id(0)
        for p in range(len(PATTERNS)):
            bk_p = bk_ref[p]
            for hh in range(2):
                head = 2 * pair + hh
                bm = jnp.full((BLK, 2 * BLK), NEG, F32)
                for b in range(N_BUCKETS):
                    bm = jnp.where(bk_p == b, rb_ref[head, b], bm)
                bias_ref[p, hh * BLK:(hh + 1) * BLK, :] = bm

    return pl.pallas_call(
        body, name="bias_tables", grid=(ATT_HEADS // 2,),
        in_specs=[pl.BlockSpec(memory_space=pltpu.SMEM), pl.BlockSpec(bk.shape, lambda i: (0, 0, 0))],
        out_specs=BIAS_SPEC,
        out_shape=jax.ShapeDtypeStruct((len(PATTERNS), ATT_HEADS // 2, 2 * BLK, 2 * BLK), F32),
        compiler_params=_params(("arbitrary",)),
    )(rb, bk)


BIAS_SPEC = pl.BlockSpec((len(PATTERNS), None, 2 * BLK, 2 * BLK), lambda i: (0, i, 0, 0))
PAIR_ROW_SPEC = pl.BlockSpec((1, 128), lambda i: (0, i))


def _stack_heads(xb, h0):
    return jnp.concatenate([jnp.where(h0, xb, 0.0), jnp.where(h0, 0.0, xb)], axis=0).astype(MXU)


def _block_rows(t, r, n):
    per_class = (t // r) // BLK
    res = n // per_class
    j = n % per_class
    start = res + BLK * r * j
    pstart = res + BLK * r * jnp.maximum(j - 1, 0)
    if r == 1:
        return pl.ds(pl.multiple_of(start, BLK), BLK), pl.ds(pl.multiple_of(pstart, BLK), BLK), j
    return pl.ds(start, BLK, stride=r), pl.ds(pstart, BLK, stride=r), j


def _att_fwd(qkv, gate, bias, wq, wk, g2):
    t = qkv.shape[0]
    rows = ATT_ROWS

    def body(bias_ref, qraw_ref, kraw_ref, v_ref, g_ref, wq_ref, wk_ref, g2_ref, o_ref, y_ref, lse_ref,
             o0_ref, o1_ref, o2_ref, l0_ref, l1_ref, l2_ref, q_ref, k_ref):
        h0 = _iota((BLK, 128), 1) < HD
        prev_cols = _iota((2 * BLK, 2 * BLK), 1) < BLK
        op_refs, lp_refs = (o0_ref, o1_ref, o2_ref), (l0_ref, l1_ref, l2_ref)
        _pair_norm(t, (qraw_ref, kraw_ref), (wq_ref, wk_ref), (q_ref, k_ref), g2_ref)

        for p, (_, r) in enumerate(PATTERNS):
            def blk(n, carry, p=p, r=r):
                cur, prev, j = _block_rows(t, r, n)
                q2 = _stack_heads(q_ref[cur, :], h0)
                k2 = jnp.concatenate([k_ref[prev, :], k_ref[cur, :]], axis=0).astype(MXU)
                v2 = jnp.concatenate([v_ref[prev, :], v_ref[cur, :]], axis=0).astype(MXU)
                s = _dot_nt(q2, k2) + bias_ref[p] + jnp.where(prev_cols & (j == 0), NEG, 0.0)
                m = jnp.max(s, axis=1, keepdims=True)
                e = jnp.exp(s - m)
                l = jnp.sum(e, axis=1, keepdims=True)
                pv = _dot(e, v2) / l
                lse = m + jnp.log(l)
                op_refs[p][cur, :] = jnp.where(h0, pv[:BLK], pv[BLK:])
                lp_refs[p][cur, :] = jnp.where(h0, lse[:BLK], lse[BLK:])
                return carry

            lax.fori_loop(0, t // BLK, blk, 0, unroll=8)

        for c in range(t // rows):
            sl = slice(c * rows, (c + 1) * rows)
            ls = [ref[sl, :] for ref in lp_refs]
            mx = jnp.maximum(jnp.maximum(ls[0], ls[1]), ls[2])
            ws = [jnp.exp(v_ - mx) for v_ in ls]
            den = ws[0] + ws[1] + ws[2]
            o = (ws[0] * o0_ref[sl, :] + ws[1] * o1_ref[sl, :] + ws[2] * o2_ref[sl, :]) / den
            o_ref[sl, :] = o
            y_ref[sl, :] = o * _silu(g_ref[sl, :])
            lse_ref[sl, :] = mx + jnp.log(den)

    col = lambda off: pl.BlockSpec((t, 128), lambda i, off=off: (0, off + i))
    return pl.pallas_call(
        body, name="att_fwd", grid=(ATT_HEADS // 2,),
        in_specs=[BIAS_SPEC, col(0), col(4), col(8), col(0), PAIR_ROW_SPEC, PAIR_ROW_SPEC,
                  pl.BlockSpec((128, 128), lambda i: (0, 0))],
        out_specs=[col(0), col(0), col(0)],
        out_shape=[jax.ShapeDtypeStruct((t, 512), F32)] * 3,
        scratch_shapes=[pltpu.VMEM((t, 128), F32)] * 8,
        compiler_params=_params(("arbitrary",)),
    )(bias, qkv, qkv, qkv, gate, wq, wk, g2)


def _outproj_loss(x, ydn, yatt, wout, target):
    t = x.shape[0]
    tm = 512

    def body(x_ref, a_ref, b_ref, w_ref, t_ref, dy_ref, mix_ref, loss_ref):
        @pl.when(pl.program_id(0) == 0)
        def _():
            loss_ref[...] = jnp.zeros_like(loss_ref)

        mixf = jnp.concatenate([a_ref[...], b_ref[...]], axis=1)
        mix_ref[...] = mixf.T.astype(MXU)
        err = x_ref[...] + jnp.dot(mixf.astype(MXU), w_ref[...], preferred_element_type=F32) - t_ref[...]
        dy_ref[...] = err * (1.0 / D_MODEL)
        loss_ref[...] += jnp.sum(err * err) * (0.5 / D_MODEL)

    row = lambda n: pl.BlockSpec((tm, n), lambda i: (i, 0))
    return pl.pallas_call(
        body, name="outproj_loss", grid=(t // tm,),
        in_specs=[row(D_MODEL), row(512), row(512), pl.BlockSpec(wout.shape, lambda i: (0, 0)), row(D_MODEL)],
        out_specs=[row(D_MODEL), pl.BlockSpec((D_MODEL, tm), lambda i: (0, i)), pl.BlockSpec((8, 128), lambda i: (0, 0))],
        out_shape=[jax.ShapeDtypeStruct((t, D_MODEL), F32), jax.ShapeDtypeStruct((D_MODEL, t), MXU),
                   jax.ShapeDtypeStruct((8, 128), F32)],
        compiler_params=_params(("arbitrary",)),
    )(x, ydn, yatt, wout, target)


def _outproj_bwd(dy, wout_t, oraw, z, dnw, oatt, gate, g, gt):
    t = dy.shape[0]
    tm = 512

    def body(dy_ref, w_ref, o_ref, z_ref, dnw_ref, oa_ref, g_ref, grp_ref, grpt_ref,
             do_ref, dz_ref, doa_ref, dg_ref, dd_ref, ddnw_ref):
        @pl.when(pl.program_id(0) == 0)
        def _():
            ddnw_ref[...] = jnp.zeros_like(ddnw_ref)

        dmix = jnp.dot(dy_ref[...].astype(MXU), w_ref[...], preferred_element_type=F32)
        dnw_v = dnw_ref[...]
        acc = jnp.zeros((1, DK), F32)
        for h in range(DN_HEADS):
            sl = slice(h * DK, (h + 1) * DK)
            o, zz, dm = o_ref[:, sl], z_ref[:, sl], dmix[:, sl]
            rs = lax.rsqrt(jnp.mean(o * o, axis=1, keepdims=True) + EPS)
            oh = o * rs
            silu_z, dsilu_z = _silu_and_grad(zz)
            dz_ref[:, sl] = dm * oh * dnw_v * dsilu_z
            d_on = dm * silu_z
            gg = d_on * dnw_v
            do_ref[:, sl] = rs * (gg - oh * jnp.mean(gg * oh, axis=1, keepdims=True))
            acc = acc + jnp.sum(d_on * oh, axis=0, keepdims=True)
        ddnw_ref[...] += jnp.broadcast_to(acc, (8, DK))
        da, gate_v, oa = dmix[:, 512:], g_ref[...], oa_ref[...]
        silu_g, dsilu_g = _silu_and_grad(gate_v)
        doa = da * silu_g
        doa_ref[...] = doa
        dg_ref[...] = da * oa * dsilu_g
        dd_ref[...] = _hdot(_hdot(doa * oa, grp_ref[...], "b"), grpt_ref[...], "b")

    row = lambda n: pl.BlockSpec((tm, n), lambda i: (i, 0))
    full = lambda a: pl.BlockSpec(a.shape, lambda i: (0,) * a.ndim)
    return pl.pallas_call(
        body, name="outproj_bwd", grid=(t // tm,),
        in_specs=[row(D_MODEL), full(wout_t), row(512), row(512), full(dnw), row(512), row(512), full(g), full(gt)],
        out_specs=[row(512)] * 5 + [pl.BlockSpec((8, DK), lambda i: (0, 0))],
        out_shape=[jax.ShapeDtypeStruct((t, 512), F32)] * 5 + [jax.ShapeDtypeStruct((8, DK), F32)],
        compiler_params=_params(("arbitrary",)),
    )(dy, wout_t, oraw, z, dnw, oatt, gate, g, gt)


def _grad_matmul(at, b, name):
    m, t = at.shape
    n = b.shape[1]
    tk = 1024
    tn = n if n <= 1536 else 512
    nk = t // tk

    def body(a_ref, b_ref, o_ref, acc_ref):
        k = pl.program_id(1)

        @pl.when(k == 0)
        def _():
            acc_ref[...] = jnp.zeros_like(acc_ref)

        acc_ref[...] += jnp.dot(a_ref[...], b_ref[...].astype(MXU), preferred_element_type=F32)

        @pl.when(k == nk - 1)
        def _():
            o_ref[...] = acc_ref[...].astype(GRAD_WIRE)

    return pl.pallas_call(
        body, name=name, grid=(n // tn, nk),
        in_specs=[pl.BlockSpec((m, tk), lambda j, k: (0, k)), pl.BlockSpec((tk, tn), lambda j, k: (k, j))],
        out_specs=pl.BlockSpec((m, tn), lambda j, k: (0, j)),
        out_shape=jax.ShapeDtypeStruct((m, n), GRAD_WIRE),
        scratch_shapes=[pltpu.VMEM((m, tn), F32)],
        compiler_params=_params(("arbitrary", "arbitrary")),
    )(at, b)


def _att_bwd(qkv, do, lse, dd, bias, bk, wq, wk, g2):
    t = qkv.shape[0]
    rows = ATT_ROWS

    def body(bias_ref, bk_ref, qraw_ref, kraw_ref, v_ref, do_ref, lse_ref, dd_ref, wq_ref, wk_ref, g2_ref,
             dq_ref, dk_ref, dv_ref, db_ref, dwq_ref, dwk_ref, ds_ref, q_ref, k_ref):
        pair = pl.program_id(0)

        @pl.when(pair == 0)
        def _():
            db_ref[...] = jnp.zeros_like(db_ref)

        _pair_norm(t, (qraw_ref, kraw_ref), (wq_ref, wk_ref), (q_ref, k_ref), g2_ref)
        ds_ref[...] = jnp.zeros_like(ds_ref)
        for c in range(t // rows):
            sl = slice(c * rows, (c + 1) * rows)
            for ref in (dq_ref, dk_ref, dv_ref):
                ref[sl, :] = jnp.zeros((rows, 128), F32)
        h0 = _iota((BLK, 128), 1) < HD
        prev_cols = _iota((2 * BLK, 2 * BLK), 1) < BLK

        def rows_of(xb):
            return jnp.concatenate([xb[:, 0:1], xb[:, HD:HD + 1]], axis=0)

        for p, (_, r) in enumerate(PATTERNS):
            def blk(n, carry, p=p, r=r):
                cur, prev, j = _block_rows(t, r, n)
                q2, do2 = _stack_heads(q_ref[cur, :], h0), _stack_heads(do_ref[cur, :], h0)
                k2 = jnp.concatenate([k_ref[prev, :], k_ref[cur, :]], axis=0).astype(MXU)
                v2 = jnp.concatenate([v_ref[prev, :], v_ref[cur, :]], axis=0).astype(MXU)
                s = _dot_nt(q2, k2) + bias_ref[p] + jnp.where(prev_cols & (j == 0), NEG, 0.0)
                prob = jnp.exp(s - rows_of(lse_ref[cur, :]))
                ds = prob * (_dot_nt(do2, v2) - rows_of(dd_ref[cur, :]))
                ds_ref[p] += ds
                dq2 = _dot(ds, k2)
                dk2 = _dot_tn(ds, q2)
                dv2 = _dot_tn(prob, do2)
                dq_ref[cur, :] += jnp.where(h0, dq2[:BLK], dq2[BLK:])
                dk_ref[prev, :] += dk2[:BLK]
                dv_ref[prev, :] += dv2[:BLK]
                dk_ref[cur, :] += dk2[BLK:]
                dv_ref[cur, :] += dv2[BLK:]
                return carry

            lax.fori_loop(0, t // BLK, blk, 0, unroll=8)

        ri, ci = _iota((8, 128), 0), _iota((8, 128), 1)
        upd = jnp.zeros((8, 128), F32)
        for p in range(len(PATTERNS)):
            bk = bk_ref[p]
            for hh in range(2):
                dsum = ds_ref[p, hh * BLK:(hh + 1) * BLK, :]
                for b in range(N_BUCKETS):
                    val = jnp.sum(jnp.where(bk == b, dsum, 0.0))
                    upd = upd + jnp.where((ri == 2 * pair + hh) & (ci == b), val, 0.0)
        db_ref[...] += upd

        for raw, d_ref, w_ref, dw_ref in ((qraw_ref, dq_ref, wq_ref, dwq_ref), (kraw_ref, dk_ref, wk_ref, dwk_ref)):
            acc = jnp.zeros((1, 128), F32)
            for c in range(t // rows):
                sl = slice(c * rows, (c + 1) * rows)
                xv, dyv = raw[sl, :], d_ref[sl, :]
                rs = _pair_rstd(xv, g2_ref)
                xh = xv * rs
                gg = dyv * w_ref[...]
                mean = _hdot(gg * xh, g2_ref[...], "b") * (1.0 / HD)
                d_ref[sl, :] = rs * (gg - xh * mean)
                acc = acc + jnp.sum(dyv * xh, axis=0, keepdims=True)
            dw_ref[...] = jnp.broadcast_to(acc, (8, 128))

    col = lambda off: pl.BlockSpec((t, 128), lambda i, off=off: (0, off + i))
    acc8 = pl.BlockSpec((8, 128), lambda i: (0, i))
    return pl.pallas_call(
        body, name="att_bwd", grid=(ATT_HEADS // 2,),
        in_specs=[BIAS_SPEC, pl.BlockSpec(bk.shape, lambda i: (0, 0, 0)),
                  col(0), col(4), col(8), col(0), col(0), col(0), PAIR_ROW_SPEC, PAIR_ROW_SPEC,
                  pl.BlockSpec((128, 128), lambda i: (0, 0))],
        out_specs=[col(0), col(0), col(0), pl.BlockSpec((8, 128), lambda i: (0, 0)), acc8, acc8],
        out_shape=[jax.ShapeDtypeStruct((t, 512), F32)] * 3 + [jax.ShapeDtypeStruct((8, 128), F32)]
                  + [jax.ShapeDtypeStruct((8, 512), F32)] * 2,
        scratch_shapes=[pltpu.VMEM((len(PATTERNS), 2 * BLK, 2 * BLK), F32)] + [pltpu.VMEM((t, 128), F32)] * 2,
        compiler_params=_params(("arbitrary",)),
    )(bias, bk, qkv, qkv, qkv, do, lse, dd, wq, wk, g2)


def _dn_scan_bwd(qkv, ba, do, sh, th, arow, dtb):
    t = qkv.shape[0]
    n = t // CHUNK
    c = CHUNK
    cps = SCAN_CHUNKS

    def body(qkv_ref, ba_ref, do_ref, sh_ref, th_ref, arow_ref, dtb_ref, dqkv_ref, dba_ref, ds_ref):
        @pl.when(pl.program_id(0) == 0)
        def _():
            ds_ref[...] = jnp.zeros_like(ds_ref)

        hs = range(DN_HEADS)
        chains = [(j, h) for j in range(cps) for h in hs]
        lane = _iota((c, 128), 1)
        row = _iota((c, 1), 0)
        ms = {}
        for j in range(cps):
            rows_j = slice(j * c, (j + 1) * c)
            mj, ri, ci = _chunk_common(qkv_ref[rows_j, :], ba_ref[rows_j, :], arow_ref[...], dtb_ref[...])
            for h in hs:
                ms[j, h] = mj[h]
        q, k, v = ({x: ms[x][nm] for x in chains} for nm in ("q", "k", "v"))
        beta, decay = ({x: ms[x][nm] for x in chains} for nm in ("beta", "decay"))
        eg, egl, etail = ({x: ms[x][nm] for x in chains} for nm in ("eg", "egl", "etail"))
        s = {x: sh_ref[x[0], x[1]] for x in chains}
        tinv = {x: th_ref[x[0], x[1]] for x in chains}
        d_o = {(j, h): do_ref[j * c:(j + 1) * c, h * DK:(h + 1) * DK] for j, h in chains}
        kb = {x: k[x] * beta[x] for x in chains}
        vb = {x: v[x] * beta[x] for x in chains}
        kbg = {x: kb[x] * eg[x] for x in chains}
        amat = {x: jnp.where(ri > ci, _dot_nt(kb[x], k[x]) * decay[x], 0.0) for x in chains}
        attn = {x: jnp.where(ri >= ci, _dot_nt(q[x], k[x]) * decay[x], 0.0) for x in chains}
        uw = {x: _hdot(tinv[x], jnp.concatenate([vb[x], kbg[x]], axis=1)) for x in chains}
        u = {x: uw[x][:, :DK] for x in chains}
        w = {x: uw[x][:, DK:] for x in chains}
        v_new = {x: u[x] - _dot(w[x], s[x]) for x in chains}
        q_dec = {x: q[x] * eg[x] for x in chains}
        k_tail = {x: k[x] * etail[x] for x in chains}
        d_attn = {x: jnp.where(ri >= ci, _dot_nt(d_o[x], v_new[x]), 0.0) for x in chains}
        d_qdec = {x: _dot_nt(d_o[x], s[x]) for x in chains}
        from_o = {x: _dot_tn(attn[x], d_o[x]) for x in chains}
        to_state = {x: _dot_tn(q_dec[x], d_o[x]) for x in chains}

        d_s, d_vnew = {}, {}
        cur = [ds_ref[h] for h in hs]
        for j in reversed(range(cps)):
            for h in hs:
                d_s[j, h] = cur[h]
                d_vnew[j, h] = from_o[j, h] + _dot(k_tail[j, h], cur[h])
            cur = [to_state[j, h] + cur[h] * egl[j, h] - _dot_tn(w[j, h], d_vnew[j, h]) for h in hs]
        for h in hs:
            ds_ref[h] = cur[h]

        d_ktail = {x: _dot_nt(v_new[x], d_s[x]) for x in chains}
        d_gl = {x: jnp.sum(s[x] * d_s[x]) * egl[x] for x in chains}
        d_w = {x: -_dot_nt(d_vnew[x], s[x]) for x in chains}
        d_both = {x: _hdot_tn(tinv[x], jnp.concatenate([d_vnew[x], d_w[x]], axis=1)) for x in chains}
        d_vb = {x: d_both[x][:, :DK] for x in chains}
        d_kbg = {x: d_both[x][:, DK:] for x in chains}
        d_a = {x: -jnp.where(ri > ci, _hdot_nt(d_both[x], uw[x]), 0.0) for x in chains}
        d_qk = {x: d_attn[x] * decay[x] for x in chains}
        d_kk = {x: d_a[x] * decay[x] for x in chains}
        d_kb = {x: _dot(d_kk[x], k[x]) + d_kbg[x] * eg[x] for x in chains}
        d_q = {x: _dot(d_qk[x], k[x]) + d_qdec[x] * eg[x] for x in chains}
        d_k = {x: _dot_tn(d_qk[x], q[x]) + _dot_tn(d_kk[x], kb[x]) + d_ktail[x] * etail[x] + d_kb[x] * beta[x]
               for x in chains}
        d_beta = {x: jnp.sum(d_kb[x] * k[x] + d_vb[x] * v[x], axis=1, keepdims=True) for x in chains}
        mm = {x: d_a[x] * amat[x] + d_attn[x] * attn[x] for x in chains}
        for j in range(cps):
            rows_j = slice(j * c, (j + 1) * c)
            rows = jnp.zeros((c, c), F32)
            for h in hs:
                rows = rows + jnp.where(ri == h, jnp.sum(mm[j, h], axis=0, keepdims=True), 0.0)
            cols_t = jnp.concatenate([rows, jnp.zeros((c, c), F32)], axis=1).T[:c, :]
            d_gc_all = jnp.zeros((c, 128), F32)
            for h in hs:
                x = (j, h)
                tail_term = jnp.sum(d_ktail[x] * k_tail[x], axis=1, keepdims=True)
                d_gc = (jnp.sum(mm[x], axis=1, keepdims=True) - _lane_col(cols_t, h)
                        + jnp.sum(d_qdec[x] * q_dec[x] + d_kbg[x] * kbg[x], axis=1, keepdims=True) - tail_term)
                d_gc = d_gc + jnp.where(row == c - 1, jnp.sum(tail_term) + d_gl[x], 0.0)
                d_gc_all = d_gc_all + jnp.where(lane == DN_HEADS + h, d_gc, 0.0)
            d_g_all = _hdot((ri <= ci).astype(F32), d_gc_all, "a")
            dba = jnp.zeros((c, 128), F32)
            for h in hs:
                x = (j, h)
                d_g = _lane_col(d_g_all, DN_HEADS + h)
                d_braw = d_beta[x] * beta[x] * (1.0 - beta[x])
                d_araw = d_g * ms[x]["a_h"] * _sigmoid(ms[x]["a_raw"] + ms[x]["dt_h"])
                dba = dba + jnp.where(lane == h, d_braw, 0.0) + jnp.where(lane == DN_HEADS + h, d_araw, 0.0) \
                    + jnp.where(lane == 2 * DN_HEADS + h, d_g * ms[x]["g"], 0.0)
                dqkv_ref[rows_j, h * DK:(h + 1) * DK] = d_q[x]
                dqkv_ref[rows_j, D_DN + h * DK:D_DN + (h + 1) * DK] = d_k[x]
                dqkv_ref[rows_j, 2 * D_DN + h * DK:2 * D_DN + (h + 1) * DK] = d_vb[x] * beta[x]
            dba_ref[rows_j, :] = dba

    nsteps = n // cps
    rev = lambda w_: pl.BlockSpec((cps * c, w_), lambda i: (nsteps - 1 - i, 0))
    one = pl.BlockSpec((1, 128), lambda i: (0, 0))
    return pl.pallas_call(
        body, name="dn_scan_bwd", grid=(nsteps,),
        in_specs=[rev(1536), rev(128), rev(512),
                  pl.BlockSpec((cps, DN_HEADS, DK, DK), lambda i: (nsteps - 1 - i, 0, 0, 0)),
                  pl.BlockSpec((cps, DN_HEADS, c, c), lambda i: (nsteps - 1 - i, 0, 0, 0)), one, one],
        out_specs=[rev(1536), rev(128)],
        out_shape=[jax.ShapeDtypeStruct((t, 1536), F32), jax.ShapeDtypeStruct((t, 128), F32)],
        scratch_shapes=[pltpu.VMEM((DN_HEADS, DK, DK), F32)],
        compiler_params=_params(("arbitrary",)),
    )(qkv, ba, do, sh, th, arow, dtb)


def _dn_prep_bwd(pdn, cw, dact):
    t = pdn.shape[0]
    nchunk = t // CONV_ROWS

    def body(u_ref, w_ref, d_ref, du_ref, dw_ref, dy_ref):
        j = pl.program_id(0)
        dy_ref[t:t + 8, :] = jnp.zeros((8, 128), F32)
        dw = [jnp.zeros((1, 128), F32) for _ in range(4)]
        for c in range(nchunk):
            sl = slice(c * CONV_ROWS, (c + 1) * CONV_ROWS)
            taps, y = _conv_taps(u_ref, c, w_ref)
            a, da_dy = _silu_and_grad(y)
            dout = d_ref[sl, :]
            rs = lax.rsqrt(jnp.sum(a * a, axis=1, keepdims=True) + EPS)
            f = jnp.where(j < 8, rs, 1.0) * jnp.where(j < 4, DK ** -0.5, 1.0)
            corr = jnp.where(j < 8, f * rs * rs * jnp.sum(dout * a, axis=1, keepdims=True), 0.0)
            dy = (f * dout - corr * a) * da_dy
            dy_ref[sl, :] = dy
            for k_ in range(4):
                dw[3 - k_] = dw[3 - k_] + jnp.sum(taps[k_] * dy, axis=0, keepdims=True)
        for i in range(4):
            dw_ref[i:i + 1, :] = dw[i]
        for c in range(nchunk):
            r0 = c * CONV_ROWS
            ext = dy_ref[r0:r0 + CONV_ROWS + 8, :]
            du = ext[:CONV_ROWS, :] * w_ref[3:4, :]
            for k_ in (1, 2, 3):
                du = du + pltpu.roll(ext, CONV_ROWS + 8 - k_, 0)[:CONV_ROWS, :] * w_ref[3 - k_:4 - k_, :]
            du_ref[r0:r0 + CONV_ROWS, :] = du

    return pl.pallas_call(
        body, name="dn_prep_bwd", grid=(12,),
        in_specs=[pl.BlockSpec((t, 128), lambda j: (0, j)), pl.BlockSpec((4, 128), lambda j: (0, j)),
                  pl.BlockSpec((t, 128), lambda j: (0, j))],
        out_specs=[pl.BlockSpec((t, 128), lambda j: (0, j)), pl.BlockSpec((4, 128), lambda j: (0, j))],
        out_shape=[jax.ShapeDtypeStruct((t, 1536), F32), jax.ShapeDtypeStruct((4, 1536), F32)],
        scratch_shapes=[pltpu.VMEM((t + 8, 128), F32)],
        compiler_params=_params(("arbitrary",)),
    )(pdn, cw, dact)


SECTIONS = (("dn", 0, 1536), ("z", 1536, 512), ("q", 2048, 512), ("k", 2560, 512), ("v", 3072, 512),
            ("gate", 3584, 512), ("ba", 4096, 128))


def _inproj_bwd(x, nw, wt, dy, dsecs, partials):
    t = x.shape[0]
    tm = 512
    npart = len(partials)
    nsteps = t // tm

    nsec = len(SECTIONS)

    def body(x_ref, nw_ref, w_ref, dy_ref, *rest):
        sec_refs, rest = rest[:nsec], rest[nsec:]
        part_refs, (gx_ref, dnw_ref, cs_ref) = rest[:npart], rest[npart:npart + 3]
        got_refs, (send, recv, loc) = rest[npart + 3:2 * npart + 3], rest[2 * npart + 3:]
        starts, waits = _chip_swap_copies(part_refs, got_refs, send, recv, loc)

        @pl.when(pl.program_id(0) == 0)
        def _():
            for start in starts:
                start()
            dnw_ref[...] = jnp.zeros_like(dnw_ref)
            cs_ref[...] = jnp.zeros_like(cs_ref)

        @pl.when(pl.program_id(0) == nsteps - 1)
        def _():
            for wait in waits:
                wait()

        dh = jnp.zeros((tm, D_MODEL), F32)
        for ref, (_, lo, width) in zip(sec_refs, SECTIONS):
            dh = dh + jnp.dot(ref[...].astype(MXU), w_ref[lo:lo + width, :], preferred_element_type=F32)
        xv = x_ref[...]
        rstd = lax.rsqrt(jnp.mean(xv * xv, axis=-1, keepdims=True) + EPS)
        xh = xv * rstd
        gg = dh * nw_ref[...]
        gx_ref[...] = rstd * (gg - xh * jnp.mean(gg * xh, axis=-1, keepdims=True)) + dy_ref[...]
        dnw_ref[...] += jnp.broadcast_to(jnp.sum(dh * xh, axis=0, keepdims=True), (8, D_MODEL))
        cs_ref[...] += jnp.broadcast_to(jnp.sum(sec_refs[nsec - 1][...], axis=0, keepdims=True), (8, 128))

    row = lambda n: pl.BlockSpec((tm, n), lambda i: (i, 0))
    full = lambda a: pl.BlockSpec(a.shape, lambda i: (0,) * a.ndim)
    res = pl.pallas_call(
        body, name="inproj_bwd", grid=(nsteps,),
        in_specs=[row(D_MODEL), full(nw), full(wt), row(D_MODEL)] + [row(width) for _, _, width in SECTIONS]
                 + [ANY_SPEC] * npart,
        out_specs=[row(D_MODEL), pl.BlockSpec((8, D_MODEL), lambda i: (0, 0)), pl.BlockSpec((8, 128), lambda i: (0, 0))]
                  + [ANY_SPEC] * npart,
        out_shape=[jax.ShapeDtypeStruct((t, D_MODEL), F32), jax.ShapeDtypeStruct((8, D_MODEL), F32),
                   jax.ShapeDtypeStruct((8, 128), F32)] + [jax.ShapeDtypeStruct(p.shape, p.dtype) for p in partials],
        scratch_shapes=[pltpu.SemaphoreType.DMA((npart, 3)), pltpu.SemaphoreType.DMA((npart, 3)),
                        pltpu.SemaphoreType.DMA((npart,))],
        compiler_params=_params(("arbitrary",)),
    )(x, nw, wt, dy, *dsecs, *partials)
    return res[0], res[1], res[2], res[3:]


def _adamw_sum(w, gs, m, v, name):
    r, c = w.shape
    nsum = gs.shape[0]
    tr = r if r <= 256 else 256
    c1 = 1.0 - ADAM_B1 ** ADAM_STEP
    c2 = 1.0 - ADAM_B2 ** ADAM_STEP

    def body(w_ref, g_ref, m_ref, v_ref, go_ref, d_ref, mo_ref, vo_ref):
        g = g_ref[0].astype(F32)
        for s in range(1, nsum):
            g = g + g_ref[s].astype(F32)
        mn = ADAM_B1 * m_ref[...] + (1.0 - ADAM_B1) * g
        vn = ADAM_B2 * v_ref[...] + (1.0 - ADAM_B2) * (g * g)
        go_ref[...] = g
        mo_ref[...] = mn
        vo_ref[...] = vn
        d_ref[...] = -ADAM_LR * ((mn / c1) / (jnp.sqrt(vn / c2) + ADAM_EPS) + ADAM_WD * w_ref[...])

    blk = pl.BlockSpec((tr, c), lambda i: (i, 0))
    return pl.pallas_call(
        body, name=name, grid=(r // tr,),
        in_specs=[blk, pl.BlockSpec((nsum, tr, c), lambda i: (0, i, 0)), blk, blk],
        out_specs=[blk] * 4, out_shape=[jax.ShapeDtypeStruct((r, c), F32)] * 4,
        compiler_params=_params(("arbitrary",)),
    )(w, gs, m, v)


def _local_step(x, target, h, ht, w_sect, conv_w, a_log, dt_bias, dn_norm_w, q_norm_w, k_norm_w, rel_bias, wout_shard):
    arow = jnp.zeros((1, 128), F32).at[0, DN_HEADS:2 * DN_HEADS].set(-jnp.exp(a_log[0]))
    dtb = jnp.zeros((1, 128), F32).at[0, DN_HEADS:2 * DN_HEADS].set(dt_bias[0])
    g_np, gt_np = _group_mats()
    g, gt = jnp.asarray(g_np), jnp.asarray(gt_np)
    g2 = jnp.asarray(np.kron(np.eye(2, dtype=np.float32), np.ones((HD, HD), np.float32)))
    bk = jnp.asarray(_bucket_tables())
    wq = jnp.tile(q_norm_w, (1, ATT_HEADS)) * (HD ** -0.5)
    wk = jnp.tile(k_norm_w, (1, ATT_HEADS))

    pdn, qkv_dn, z, patt, gate, ba, wout8 = _inproj(h, w_sect, conv_w, wout_shard)
    w_out = wout8.reshape(D_MODEL, D_MODEL)
    oraw, ydn, sh, th = _dn_scan_fwd(qkv_dn, ba, z, arow, dtb, dn_norm_w)
    bias = _bias_tables(rel_bias, bk)
    oatt, yatt, lse = _att_fwd(patt, gate, bias, wq, wk, g2)
    dy, mix_t, loss8 = _outproj_loss(x, ydn, yatt, w_out, target)

    do_dn, dz, do_att, dgate, dd, ddnw = _outproj_bwd(dy, w_out.T, oraw, z, dn_norm_w, oatt, gate, g, gt)
    d_wout = _grad_matmul(mix_t, dy, "dw_out")
    dq, dk, dv, drb, dwq8, dwk8 = _att_bwd(patt, do_att, lse, dd, bias, bk, wq, wk, g2)
    dqkv_dn, dba = _dn_scan_bwd(qkv_dn, ba, do_dn, sh, th, arow, dtb)
    dpdn, d_conv = _dn_prep_bwd(pdn, conv_w, dqkv_dn)
    dsecs = (dpdn, dz, dq, dk, dv, dgate, dba)
    dw_sections = [_grad_matmul(ht, d_, "dw_in_" + nm) for d_, (nm, _, _) in zip(dsecs, SECTIONS)]
    return dict(w_in_sections=dw_sections, conv_w=d_conv, w_out=d_wout, dy=dy, dsecs=dsecs,
                small_parts=(loss8, ddnw, dwq8, dwk8, drb))


def _finish_step(x, norm_w, w_sect_t, gr, partials):
    grad_x, dnw8, cs8, got = _inproj_bwd(x, norm_w, w_sect_t, gr["dy"], gr["dsecs"], partials)
    return grad_x, _pack_small_grads(dnw8, cs8, *gr["small_parts"]), got


SMALL_ROWS = 24
SMALL_AT = dict(a_log=(slice(8, 9), slice(0, 4)), dt_bias=(slice(9, 10), slice(0, 4)),
                dn_norm_w=(slice(10, 11), slice(0, 128)), q_norm_w=(slice(11, 12), slice(0, HD)),
                k_norm_w=(slice(12, 13), slice(0, HD)), rel_bias=(slice(16, 24), slice(0, N_BUCKETS)))
SMALL_NAMES = ("norm_w", "a_log", "dt_bias", "dn_norm_w", "q_norm_w", "k_norm_w", "rel_bias")


LOSS_ROW = 13


def _pack_small_grads(dnw8, cs8, loss8, ddnw8, dwq8, dwk8, drb):
    def body(dnw_ref, cs_ref, loss_ref, ddnw_ref, dwq_ref, dwk_ref, drb_ref, o_ref):
        lane = _iota((8, 128), 1)
        o_ref[...] = jnp.zeros_like(o_ref)
        o_ref[LOSS_ROW:LOSS_ROW + 1, :] = jnp.where(lane == 0, loss_ref[...], 0.0)[0:1, :]
        for k in range(D_MODEL // 128):
            o_ref[k:k + 1, :] = dnw_ref[0:1, k * 128:(k + 1) * 128]
        cs = cs_ref[...]
        o_ref[8:9, :] = jnp.where(lane < DN_HEADS, pltpu.roll(cs, 128 - 2 * DN_HEADS, 1), 0.0)[0:1, :]
        o_ref[9:10, :] = jnp.where(lane < DN_HEADS, pltpu.roll(cs, 128 - DN_HEADS, 1), 0.0)[0:1, :]
        o_ref[10:11, :] = ddnw_ref[0:1, :]
        for row, ref, scale in ((11, dwq_ref, HD ** -0.5), (12, dwk_ref, 1.0)):
            acc = ref[:, 0:128] + ref[:, 128:256] + ref[:, 256:384] + ref[:, 384:512]
            acc = (acc + pltpu.roll(acc, HD, 1)) * scale
            o_ref[row:row + 1, :] = jnp.where(lane < HD, acc, 0.0)[0:1, :]
        o_ref[16:24, :] = drb_ref[...]

    return pl.pallas_call(body, name="pack_small_grads", out_shape=jax.ShapeDtypeStruct((SMALL_ROWS, 128), F32),
                          )(dnw8, cs8, loss8, ddnw8, dwq8, dwk8, drb)


def _adam_math(w, g, m, v):
    c1 = 1.0 - ADAM_B1 ** ADAM_STEP
    c2 = 1.0 - ADAM_B2 ** ADAM_STEP
    mn = ADAM_B1 * m + (1.0 - ADAM_B1) * g
    vn = ADAM_B2 * v + (1.0 - ADAM_B2) * (g * g)
    return -ADAM_LR * ((mn / c1) / (jnp.sqrt(vn / c2) + ADAM_EPS) + ADAM_WD * w), mn, vn


def _adamw_small(gs, ws, ms, vs):
    n = len(SMALL_NAMES)

    def body(g_ref, *refs):
        w_refs, m_refs, v_refs = refs[:n], refs[n:2 * n], refs[2 * n:3 * n]
        outs, loss_ref = refs[3 * n:7 * n], refs[7 * n]
        loss = g_ref[0, LOSS_ROW:LOSS_ROW + 1, :]
        for s in range(1, gs.shape[0]):
            loss = loss + g_ref[s, LOSS_ROW:LOSS_ROW + 1, :]
        loss_ref[...] = loss

        def one(i, rows, lanes, at):
            g = g_ref[0, rows, lanes]
            for s in range(1, gs.shape[0]):
                g = g + g_ref[s, rows, lanes]
            d, mn, vn = _adam_math(w_refs[i][at], g, m_refs[i][at], v_refs[i][at])
            for kind, val in enumerate((g, d, mn, vn)):
                outs[kind * n + i][at] = val

        for k in range(D_MODEL // 128):
            one(0, slice(k, k + 1), slice(0, 128), (slice(0, 1), slice(k * 128, (k + 1) * 128)))
        for i, nm in enumerate(SMALL_NAMES[1:], start=1):
            rows, lanes = SMALL_AT[nm]
            one(i, rows, lanes, (slice(None), slice(None)))

    shapes = [jax.ShapeDtypeStruct(w.shape, F32) for w in ws]
    res = pl.pallas_call(body, name="adamw_small",
                         out_shape=shapes * 4 + [jax.ShapeDtypeStruct((1, 128), F32)])(gs, *ws, *ms, *vs)
    return [res[k * n:(k + 1) * n] for k in range(4)], res[4 * n]


def kernel(x, norm_w, w_in, conv_w, a_log, dt_bias, dn_norm_w, q_norm_w, k_norm_w, rel_bias, w_out, loss_target, m_norm_w, m_w_in, m_conv_w, m_a_log, m_dt_bias, m_dn_norm_w, m_q_norm_w, m_k_norm_w, m_rel_bias, m_w_out, v_norm_w, v_w_in, v_conv_w, v_a_log, v_dt_bias, v_dn_norm_w, v_q_norm_w, v_k_norm_w, v_rel_bias, v_w_out):
    assert w_in.shape[2] == SHARD_COLS
    h, ht, (win8, conv8) = _norm_and_gather(x[0], norm_w, [w_in[0].astype(MXU), conv_w[0]])
    w_sect, w_sect_t = _build_w(win8)
    conv_full = conv8.transpose(1, 0, 2).reshape(4, 3 * D_DN)

    gr = _local_step(x[0], loss_target[0], h, ht, w_sect, conv_full, a_log, dt_bias, dn_norm_w, q_norm_w,
                     k_norm_w, rel_bias, w_out[0].astype(MXU))

    slabs = [_build_slabs(gr["w_in_sections"]),
             gr["w_out"].reshape(4, 2, D_MODEL // N_DEV, D_MODEL).transpose(1, 0, 2, 3),
             gr["conv_w"].reshape(4, 4, 2, 3 * D_DN // N_DEV).transpose(2, 1, 0, 3)]
    core = lax.axis_index("c").astype(jnp.int32).reshape(1)
    from_sibling = _swap_siblings(slabs)
    wires = (GRAD_WIRE, GRAD_WIRE, F32)
    partial = [_chip_sum(slabs[i], from_sibling[i], core, wires[i], "chip_sum_%d" % i) for i in range(3)]
    grad_x, small_pack, (r_win, r_wout, r_conv) = _finish_step(x[0], norm_w, w_sect_t, gr, partial)
    r_small = _share_small(small_pack)

    g_win, d_win, m_win, v_win = _adamw_sum(w_in[0], r_win, m_w_in[0], v_w_in[0], "adamw_w_in")
    g_wout, d_wout, m_wout, v_wout = _adamw_sum(w_out[0], r_wout, m_w_out[0], v_w_out[0], "adamw_w_out")
    g_conv, d_conv, m_conv, v_conv = _adamw_sum(conv_w[0], r_conv, m_conv_w[0], v_conv_w[0], "adamw_conv_w")
    small, loss_row = _adamw_small(r_small,
                                   (norm_w, a_log, dt_bias, dn_norm_w, q_norm_w, k_norm_w, rel_bias),
                                   (m_norm_w, m_a_log, m_dt_bias, m_dn_norm_w, m_q_norm_w, m_k_norm_w, m_rel_bias),
                                   (v_norm_w, v_a_log, v_dt_bias, v_dn_norm_w, v_q_norm_w, v_k_norm_w, v_rel_bias))

    loss = loss_row[0, 0]
    names = ("norm_w", "w_in", "conv_w", "a_log", "dt_bias", "dn_norm_w", "q_norm_w", "k_norm_w", "rel_bias", "w_out")
    big = dict(w_in=(g_win, d_win, m_win, v_win), conv_w=(g_conv, d_conv, m_conv, v_conv),
               w_out=(g_wout, d_wout, m_wout, v_wout))
    outs = [loss, grad_x[None]]
    for kind in range(4):
        for nm in names:
            outs.append(big[nm][kind][None] if nm in big else small[kind][SMALL_NAMES.index(nm)])
    return tuple(outs)
```

```python
import math

import numpy as np
import jax
import jax.numpy as jnp
from jax import lax
from jax.experimental import pallas as pl
from jax.experimental.pallas import tpu as pltpu

F32 = jnp.float32
MXU = jnp.bfloat16
GRAD_WIRE = jnp.bfloat16

D_MODEL = 1024
D_DN = 512
DN_HEADS = 4
DK = 128
CHUNK = 64
D_ATT = 512
ATT_HEADS = 8
HD = 64
PATTERNS = ((128, 1), (512, 4), (2048, 16))
BLK = 128
N_BUCKETS = 32
MAX_DISTANCE = 2048
EPS = 1e-6
W_COLS = 4224
N_DEV = 8
AXES = ("x", "y", "c")

ADAM_LR = 0.001
ADAM_B1 = 0.9
ADAM_B2 = 0.999
ADAM_EPS = 1e-08
ADAM_WD = 0.01
ADAM_STEP = 10

VMEM_LIMIT = 56 * 1024 * 1024
NEG = -1e30


def _dot(a, b):
    return jnp.dot(a.astype(MXU), b.astype(MXU), preferred_element_type=F32)


def _dot_nt(a, b):
    return lax.dot_general(a.astype(MXU), b.astype(MXU), (((1,), (1,)), ((), ())), preferred_element_type=F32)


def _dot_tn(a, b):
    return lax.dot_general(a.astype(MXU), b.astype(MXU), (((0,), (0,)), ((), ())), preferred_element_type=F32)


def _split(a):
    hi = a.astype(jnp.bfloat16)
    return hi, (a - hi.astype(F32)).astype(jnp.bfloat16)


def _dot_split(a, b, dims, exact):
    dg = lambda u, v: lax.dot_general(u, v, (dims, ((), ())), preferred_element_type=F32)
    if exact == "b":
        ah, al = _split(a)
        bh = b.astype(jnp.bfloat16)
        return dg(ah, bh) + dg(al, bh)
    if exact == "a":
        bh, bm = _split(b)
        bl = (b - bh.astype(F32) - bm.astype(F32)).astype(jnp.bfloat16)
        ah = a.astype(jnp.bfloat16)
        return dg(ah, bh) + (dg(ah, bm) + dg(ah, bl))
    ah, al = _split(a)
    bh, bl = _split(b)
    return dg(ah, bh) + (dg(ah, bl) + dg(al, bh))


def _wy_inverses(amat, eye):
    tinv = {x: eye - amat[x] for x in amat}
    pw = amat
    for _ in range(5):
        pw = {x: _hdot(pw[x], pw[x]) for x in amat}
        tinv = {x: tinv[x] + _hdot(tinv[x], pw[x]) for x in amat}
    return tinv


def _hdot(a, b, exact=None):
    return _dot_split(a, b, ((1,), (0,)), exact)


def _hdot_nt(a, b, exact=None):
    return _dot_split(a, b, ((1,), (1,)), exact)


def _hdot_tn(a, b, exact=None):
    return _dot_split(a, b, ((0,), (0,)), exact)


def _sigmoid(x):
    return 1.0 / (1.0 + jnp.exp(-x))


def _silu(x):
    return x * _sigmoid(x)


def _silu_and_grad(x):
    s = _sigmoid(x)
    return x * s, s * (1.0 + x * (1.0 - s))


def _softplus(x):
    return jnp.maximum(x, 0.0) + jnp.log(1.0 + jnp.exp(-jnp.abs(x)))


def _iota(shape, dim):
    return lax.broadcasted_iota(jnp.int32, shape, dim)


def _lane_col(x, k):
    return jnp.sum(jnp.where(_iota(x.shape, 1) == k, x, 0.0), axis=1, keepdims=True)


def _params(sem=None):
    return pltpu.CompilerParams(dimension_semantics=sem, vmem_limit_bytes=VMEM_LIMIT)


def _t5_bucket(dist):
    max_exact = N_BUCKETS // 2
    d = np.maximum(dist, 1).astype(np.float64)
    large = max_exact + (np.log(d / max_exact) / math.log(MAX_DISTANCE / max_exact)
                         * (N_BUCKETS - max_exact)).astype(np.int32)
    large = np.minimum(large, N_BUCKETS - 1)
    return np.where(dist < max_exact, dist, large).astype(np.int32)


def _bucket_tables():
    qi = np.arange(BLK)[:, None]
    kj = np.arange(2 * BLK)[None, :]
    step = qi - kj + BLK
    band = (step >= 0) & (step <= BLK)
    out = []
    for _, r in PATTERNS:
        b = _t5_bucket(np.clip(step, 0, None) * r)
        out.append(np.where(band, b, -1))
    return np.stack(out).astype(np.int32)


def _group_mats():
    g = np.zeros((D_ATT, 128), np.float32)
    for h in range(ATT_HEADS):
        g[h * HD:(h + 1) * HD, h] = 1.0
    return g, np.ascontiguousarray(g.T)


CHIP_FLIPS = ((1, 0), (0, 1), (1, 1))
ANY_SPEC = pl.BlockSpec(memory_space=pl.ANY)
MESH_ID = pl.DeviceIdType.MESH


def _other_chips():
    x, y = lax.axis_index("x"), lax.axis_index("y")
    return [((1 - x if fx else x), (1 - y if fy else y)) for fx, fy in CHIP_FLIPS]


def _gather_plan(ins, outs, send, recv, loc):
    n = len(ins)
    x, y, c = (lax.axis_index(a) for a in AXES)
    sib = (x, y, 1 - c)
    chips = _other_chips()
    lin = lambda px, py, pc: 4 * px + 2 * py + pc

    def copy(a, k, block, to, src=None):
        slot = outs[a].at[lin(*block)]
        return pltpu.make_async_remote_copy(src_ref=slot if src is None else src, dst_ref=slot,
                                            send_sem=send.at[a, k], recv_sem=recv.at[a, k],
                                            device_id=to, device_id_type=MESH_ID)

    mine = [pltpu.make_async_copy(ins[a], outs[a].at[lin(x, y, c)], loc.at[a]) for a in range(n)]
    firsts = []
    for a in range(n):
        firsts.append(copy(a, 0, (x, y, c), sib, src=ins[a]))
        firsts += [copy(a, 1 + j, (x, y, c), (*chip, c), src=ins[a]) for j, chip in enumerate(chips)]

    def begin():
        for cp in mine + firsts:
            cp.start()

    def finish():
        passed = []
        for j, chip in enumerate(chips):
            for a in range(n):
                copy(a, 1 + j, (*chip, c), (x, y, c)).wait_recv()
                fw = copy(a, 4 + j, (*chip, c), sib)
                fw.start()
                passed.append(fw)
        for a in range(n):
            copy(a, 0, sib, (x, y, c)).wait_recv()
            for j, chip in enumerate(chips):
                copy(a, 4 + j, (*chip, 1 - c), (x, y, c)).wait_recv()
        for cp in firsts + passed:
            cp.wait_send()
        for cp in mine:
            cp.wait()

    return begin, finish


GATHER_SEMS = lambda n: [pltpu.SemaphoreType.DMA((n, 7)), pltpu.SemaphoreType.DMA((n, 7)), pltpu.SemaphoreType.DMA((n,))]


def _swap_siblings(arrs):
    n = len(arrs)

    def body(*refs):
        ins, outs = refs[:n], refs[n:2 * n]
        send, recv = refs[2 * n:]
        x, y, c = (lax.axis_index(a) for a in AXES)
        cps = [pltpu.make_async_remote_copy(src_ref=ins[a].at[1 - c], dst_ref=outs[a], send_sem=send.at[a],
                                            recv_sem=recv.at[a], device_id=(x, y, 1 - c), device_id_type=MESH_ID)
               for a in range(n)]
        for cp in cps:
            cp.start()
        for cp in cps:
            cp.wait()

    return pl.pallas_call(
        body, name="swap_siblings", out_shape=[jax.ShapeDtypeStruct(a.shape[1:], a.dtype) for a in arrs],
        in_specs=[ANY_SPEC] * n, out_specs=[ANY_SPEC] * n,
        scratch_shapes=[pltpu.SemaphoreType.DMA((n,)), pltpu.SemaphoreType.DMA((n,))],
    )(*arrs)


def _chip_sum(mine2, theirs, core, wire, name):
    _, nchip, r, cdim = mine2.shape
    tr = r if r <= 1024 else 1024

    def body(core_ref, a_ref, b_ref, o_ref):
        del core_ref
        o_ref[...] = (a_ref[...].astype(F32) + b_ref[...].astype(F32)).astype(wire)

    grid_spec = pltpu.PrefetchScalarGridSpec(
        num_scalar_prefetch=1, grid=(nchip, r // tr),
        in_specs=[pl.BlockSpec((None, None, tr, cdim), lambda j, i, cr: (cr[0], j, i, 0)),
                  pl.BlockSpec((None, tr, cdim), lambda j, i, cr: (j, i, 0))],
        out_specs=pl.BlockSpec((None, tr, cdim), lambda j, i, cr: (j, i, 0)))
    return pl.pallas_call(
        body, name=name, grid_spec=grid_spec, out_shape=jax.ShapeDtypeStruct((nchip, r, cdim), wire),
        compiler_params=_params(("arbitrary", "arbitrary")),
    )(core, mine2, theirs)


def _chip_swap_copies(ins, outs, send, recv, loc):
    x, y, c = (lax.axis_index(a) for a in AXES)
    me = 2 * x + y
    starts, arrivals, drains = [], [], []
    for a in range(len(ins)):
        lc = pltpu.make_async_copy(ins[a].at[me], outs[a].at[me], loc.at[a])
        starts.append(lc.start)
        drains.append(lc.wait)
        for j, (px, py) in enumerate(_other_chips()):
            them = 2 * px + py
            cp = pltpu.make_async_remote_copy(src_ref=ins[a].at[them], dst_ref=outs[a].at[me], send_sem=send.at[a, j],
                                              recv_sem=recv.at[a, j], device_id=(px, py, c), device_id_type=MESH_ID)
            landing = pltpu.make_async_remote_copy(src_ref=ins[a].at[them], dst_ref=outs[a].at[them],
                                                   send_sem=send.at[a, j], recv_sem=recv.at[a, j],
                                                   device_id=(px, py, c), device_id_type=MESH_ID)
            starts.append(cp.start)
            arrivals.append(landing.wait_recv)
            drains.append(cp.wait_send)
    return starts, arrivals + drains


def _share_small(pack):
    def body(in_ref, out_ref, send, recv, loc):
        x, y, c = (lax.axis_index(a) for a in AXES)
        me = 4 * x + 2 * y + c
        lc = pltpu.make_async_copy(in_ref, out_ref.at[me], loc.at[0])
        lc.start()
        sends, arrivals = [], []
        for k in range(1, N_DEV):
            px = 1 - x if k & 4 else x
            py = 1 - y if k & 2 else y
            pc = 1 - c if k & 1 else c
            cp = pltpu.make_async_remote_copy(src_ref=in_ref, dst_ref=out_ref.at[me], send_sem=send.at[k - 1],
                                              recv_sem=recv.at[k - 1], device_id=(px, py, pc), device_id_type=MESH_ID)
            cp.start()
            sends.append(cp)
            arrivals.append(pltpu.make_async_remote_copy(src_ref=in_ref, dst_ref=out_ref.at[4 * px + 2 * py + pc],
                                                         send_sem=send.at[k - 1], recv_sem=recv.at[k - 1],
                                                         device_id=(px, py, pc), device_id_type=MESH_ID))
        for cp in arrivals:
            cp.wait_recv()
        for cp in sends:
            cp.wait_send()
        lc.wait()

    return pl.pallas_call(
        body, name="share_small", out_shape=jax.ShapeDtypeStruct((N_DEV,) + pack.shape, pack.dtype),
        in_specs=[ANY_SPEC], out_specs=ANY_SPEC,
        scratch_shapes=[pltpu.SemaphoreType.DMA((N_DEV - 1,)), pltpu.SemaphoreType.DMA((N_DEV - 1,)),
                        pltpu.SemaphoreType.DMA((1,))],
    )(pack)


W_PARTS = ((0, 0, 2048), (2048, 4096, 8), (2056, 2048, 2048))
SHARD_COLS = 513


def _pieces(lo, hi, parts):
    out = []
    for ref_start, tgt_start, width in parts:
        a, b = max(lo, ref_start), min(hi, ref_start + width)
        if a < b:
            out.append((a - lo, tgt_start + a - ref_start, b - a))
    return out


def _build_w(win8):
    tr = 512

    def body(in_ref, w_ref, wt_ref):
        w_ref[:, 4096:W_COLS] = jnp.zeros((tr, W_COLS - 4096), MXU)
        for p in range(N_DEV):
            for src, dst, width in _pieces(p * SHARD_COLS, (p + 1) * SHARD_COLS, W_PARTS):
                w_ref[:, dst:dst + width] = in_ref[p, :, src:src + width]
        for k in range(W_COLS // 128):
            wt_ref[k * 128:(k + 1) * 128, :] = w_ref[:, k * 128:(k + 1) * 128].astype(F32).T.astype(MXU)

    return pl.pallas_call(
        body, name="build_w", grid=(D_MODEL // tr,),
        in_specs=[pl.BlockSpec((N_DEV, tr, SHARD_COLS), lambda i: (0, i, 0))],
        out_specs=[pl.BlockSpec((tr, W_COLS), lambda i: (i, 0)), pl.BlockSpec((W_COLS, tr), lambda i: (0, i))],
        out_shape=[jax.ShapeDtypeStruct((D_MODEL, W_COLS), MXU), jax.ShapeDtypeStruct((W_COLS, D_MODEL), MXU)],
        compiler_params=_params(("arbitrary",)),
    )(win8)


def _build_slabs(secs):
    tr = 512
    parts = ((0, 0, 1536), (1536, 1, 512), (2048, 6, 8), (2056, 2, 512), (2568, 3, 512), (3080, 4, 512),
             (3592, 5, 512))

    def body(*refs):
        o_ref = refs[len(secs)]
        for p in range(N_DEV):
            lo, hi = p * SHARD_COLS, (p + 1) * SHARD_COLS
            for ref_start, idx, width in parts:
                a, b = max(lo, ref_start), min(hi, ref_start + width)
                if a < b:
                    o_ref[p % 2, p // 2, :, a - lo:b - lo] = refs[idx][:, a - ref_start:b - ref_start]

    return pl.pallas_call(
        body, name="build_slabs", grid=(D_MODEL // tr,),
        in_specs=[pl.BlockSpec((tr, s.shape[1]), lambda i: (i, 0)) for s in secs],
        out_specs=pl.BlockSpec((2, 4, tr, SHARD_COLS), lambda i: (0, 0, i, 0)),
        out_shape=jax.ShapeDtypeStruct((2, 4, D_MODEL, SHARD_COLS), secs[0].dtype),
        compiler_params=_params(("arbitrary",)),
    )(*secs)


def _norm_and_gather(x, nw, shards):
    t = x.shape[0]
    tm = 512
    nsteps = t // tm
    n = len(shards)

    def body(x_ref, nw_ref, *rest):
        ins, (h_ref, ht_ref), outs, sems = rest[:n], rest[n:n + 2], rest[n + 2:2 * n + 2], rest[2 * n + 2:]
        begin, finish = _gather_plan(ins, outs, *sems)

        @pl.when(pl.program_id(0) == 0)
        def _():
            begin()

        @pl.when(pl.program_id(0) == nsteps - 1)
        def _():
            finish()

        xv = x_ref[...]
        rstd = lax.rsqrt(jnp.mean(xv * xv, axis=-1, keepdims=True) + EPS)
        hf = xv * rstd * nw_ref[...]
        h_ref[...] = hf.astype(MXU)
        ht_ref[...] = hf.T.astype(MXU)

    res = pl.pallas_call(
        body, name="norm_and_gather", grid=(nsteps,),
        in_specs=[pl.BlockSpec((tm, D_MODEL), lambda i: (i, 0)), pl.BlockSpec(nw.shape, lambda i: (0, 0))]
                 + [ANY_SPEC] * n,
        out_specs=[pl.BlockSpec((tm, D_MODEL), lambda i: (i, 0)), pl.BlockSpec((D_MODEL, tm), lambda i: (0, i))]
                  + [ANY_SPEC] * n,
        out_shape=[jax.ShapeDtypeStruct((t, D_MODEL), MXU), jax.ShapeDtypeStruct((D_MODEL, t), MXU)]
                  + [jax.ShapeDtypeStruct((N_DEV,) + a.shape, a.dtype) for a in shards],
        scratch_shapes=GATHER_SEMS(n),
        compiler_params=_params(("arbitrary",)),
    )(x, nw, *shards)
    return res[0], res[1], res[2:]


def _inproj(h_all, w, cw, wout_shard):
    t = h_all.shape[0]
    tm = 512
    nsteps = t // tm

    def body(h_ref, w_ref, cw_ref, wo_ref, pdn_ref, qkv_ref, z_ref, patt_ref, gate_ref, ba_ref,
             wo8_ref, halo_ref, send, recv, loc):
        begin, finish = _gather_plan([wo_ref], [wo8_ref], send, recv, loc)

        @pl.when(pl.program_id(0) == 0)
        def _():
            begin()
            halo_ref[...] = jnp.zeros_like(halo_ref)

        @pl.when(pl.program_id(0) == nsteps - 1)
        def _():
            finish()

        h = h_ref[...]
        for ref, lo, hi in ((z_ref, 1536, 2048), (patt_ref, 2048, 3584), (gate_ref, 3584, 4096), (ba_ref, 4096, 4224)):
            ref[...] = jnp.dot(h, w_ref[:, lo:hi], preferred_element_type=F32)
        pdn = jnp.dot(h, w_ref[:, 0:3 * D_DN], preferred_element_type=F32)
        pdn_ref[...] = pdn
        _dn_prep_tile(pdn, halo_ref, cw_ref, qkv_ref)

    row = lambda n: pl.BlockSpec((tm, n), lambda i: (i, 0))
    full = lambda a: pl.BlockSpec(a.shape, lambda i: (0,) * a.ndim)
    return pl.pallas_call(
        body, name="inproj", grid=(nsteps,),
        in_specs=[row(D_MODEL), full(w), full(cw), ANY_SPEC],
        out_specs=[row(1536), row(1536), row(512), row(1536), row(512), row(128), ANY_SPEC],
        out_shape=[jax.ShapeDtypeStruct((t, n), F32) for n in (1536, 1536, 512, 1536, 512, 128)] +
                  [jax.ShapeDtypeStruct((N_DEV,) + wout_shard.shape, wout_shard.dtype)],
        scratch_shapes=[pltpu.VMEM((8, 3 * D_DN), F32)] + GATHER_SEMS(1),
        compiler_params=_params(("arbitrary",)),
    )(h_all, w, cw, wout_shard)


CONV_ROWS = 512


def _conv_taps(u_ref, c, w_ref):
    r0 = c * CONV_ROWS
    if c == 0:
        ext = jnp.concatenate([jnp.zeros((8, 128), F32), u_ref[0:CONV_ROWS, :]], axis=0)
    else:
        ext = u_ref[r0 - 8:r0 + CONV_ROWS, :]
    taps = [ext[8:, :]] + [pltpu.roll(ext, k, 0)[8:, :] for k in (1, 2, 3)]
    y = taps[0] * w_ref[3:4, :]
    for k in (1, 2, 3):
        y = y + taps[k] * w_ref[3 - k:4 - k, :]
    return taps, y


def _dn_prep_tile(pdn, halo_ref, cw_ref, out_ref):
    rows = pdn.shape[0]
    ext = jnp.concatenate([halo_ref[...], pdn], axis=0)
    halo_ref[...] = pdn[rows - 8:, :]
    for j in range(3 * D_DN // 128):
        cols = slice(j * 128, (j + 1) * 128)
        e = ext[:, cols]
        y = e[8:, :] * cw_ref[3:4, cols]
        for k in (1, 2, 3):
            y = y + pltpu.roll(e, k, 0)[8:, :] * cw_ref[3 - k:4 - k, cols]
        a = _silu(y)
        if j < 2 * DN_HEADS:
            a = a * lax.rsqrt(jnp.sum(a * a, axis=1, keepdims=True) + EPS)
        if j < DN_HEADS:
            a = a * DK ** -0.5
        out_ref[:, cols] = a


def _chunk_common(qkv, ba, arow, dtb):
    c = CHUNK
    ri, ci = _iota((c, c), 0), _iota((c, c), 1)
    lane = _iota((c, 128), 1)
    g_all = jnp.where((lane >= DN_HEADS) & (lane < 2 * DN_HEADS), arow * _softplus(ba + dtb), 0.0)
    gc_all = _hdot((ri >= ci).astype(F32), g_all, "a")
    gc_t = gc_all.T
    beta_all = _sigmoid(ba)
    out = []
    for h in range(DN_HEADS):
        gc = _lane_col(gc_all, DN_HEADS + h)
        gcr = gc_t[DN_HEADS + h:DN_HEADS + h + 1, :]
        gl = gc[c - 1:c, :]
        out.append(dict(
            q=qkv[:, h * DK:(h + 1) * DK], k=qkv[:, D_DN + h * DK:D_DN + (h + 1) * DK],
            v=qkv[:, 2 * D_DN + h * DK:2 * D_DN + (h + 1) * DK],
            beta=_lane_col(beta_all, h), g=_lane_col(g_all, DN_HEADS + h),
            a_raw=_lane_col(ba, DN_HEADS + h), a_h=_lane_col(arow, DN_HEADS + h), dt_h=_lane_col(dtb, DN_HEADS + h),
            decay=jnp.exp(jnp.where(ri >= ci, gc - gcr, NEG)), eg=jnp.exp(gc), egl=jnp.exp(gl), etail=jnp.exp(gl - gc)))
    return out, ri, ci


SCAN_CHUNKS = 8


def _dn_scan_fwd(qkv, ba, z, arow, dtb, dnw):
    t = qkv.shape[0]
    n = t // CHUNK
    c = CHUNK
    cps = SCAN_CHUNKS
    hs = range(DN_HEADS)
    chains = [(j, h) for j in range(cps) for h in hs]

    def body(qkv_ref, ba_ref, z_ref, arow_ref, dtb_ref, dnw_ref, o_ref, y_ref, sh_ref, th_ref, s_ref):
        @pl.when(pl.program_id(0) == 0)
        def _():
            s_ref[...] = jnp.zeros_like(s_ref)

        ms = {}
        for j in range(cps):
            rows = slice(j * c, (j + 1) * c)
            mj, ri, ci = _chunk_common(qkv_ref[rows, :], ba_ref[rows, :], arow_ref[...], dtb_ref[...])
            for h in hs:
                ms[j, h] = mj[h]
        kb = {x: ms[x]["k"] * ms[x]["beta"] for x in chains}
        amat = {x: jnp.where(ri > ci, _dot_nt(kb[x], ms[x]["k"]) * ms[x]["decay"], 0.0) for x in chains}
        attn = {x: jnp.where(ri >= ci, _dot_nt(ms[x]["q"], ms[x]["k"]) * ms[x]["decay"], 0.0) for x in chains}
        tinv = _wy_inverses(amat, (ri == ci).astype(F32))
        uw = {x: _hdot(tinv[x], jnp.concatenate([ms[x]["v"] * ms[x]["beta"], kb[x] * ms[x]["eg"]], axis=1))
              for x in chains}
        u = {x: uw[x][:, :DK] for x in chains}
        w = {x: uw[x][:, DK:] for x in chains}
        q_dec = {x: ms[x]["q"] * ms[x]["eg"] for x in chains}
        k_tail = {x: ms[x]["k"] * ms[x]["etail"] for x in chains}
        s = [s_ref[h] for h in hs]
        for j in range(cps):
            rows = slice(j * c, (j + 1) * c)
            v_new = [u[j, h] - _dot(w[j, h], s[h]) for h in hs]
            o = [_dot(q_dec[j, h], s[h]) + _dot(attn[j, h], v_new[h]) for h in hs]
            for h in hs:
                sh_ref[j, h] = s[h]
                th_ref[j, h] = tinv[j, h]
            s = [s[h] * ms[j, h]["egl"] + _dot_tn(k_tail[j, h], v_new[h]) for h in hs]
            for h in hs:
                cols = slice(h * DK, (h + 1) * DK)
                o_ref[rows, cols] = o[h]
                rs = lax.rsqrt(jnp.mean(o[h] * o[h], axis=1, keepdims=True) + EPS)
                y_ref[rows, cols] = o[h] * rs * dnw_ref[...] * _silu(z_ref[rows, cols])
        for h in hs:
            s_ref[h] = s[h]

    row = lambda w_: pl.BlockSpec((cps * c, w_), lambda i: (i, 0))
    one = pl.BlockSpec((1, 128), lambda i: (0, 0))
    return pl.pallas_call(
        body, name="dn_scan_fwd", grid=(n // cps,),
        in_specs=[row(1536), row(128), row(512), one, one, one],
        out_specs=[row(512), row(512), pl.BlockSpec((cps, DN_HEADS, DK, DK), lambda i: (i, 0, 0, 0)),
                   pl.BlockSpec((cps, DN_HEADS, c, c), lambda i: (i, 0, 0, 0))],
        out_shape=[jax.ShapeDtypeStruct((t, 512), F32), jax.ShapeDtypeStruct((t, 512), F32),
                   jax.ShapeDtypeStruct((n, DN_HEADS, DK, DK), F32), jax.ShapeDtypeStruct((n, DN_HEADS, c, c), F32)],
        scratch_shapes=[pltpu.VMEM((DN_HEADS, DK, DK), F32)],
        compiler_params=_params(("arbitrary",)),
    )(qkv, ba, z, arow, dtb, dnw)


ATT_ROWS = 512


def _pair_rstd(xv, g2_ref):
    return lax.rsqrt(_hdot(xv * xv, g2_ref[...], "b") * (1.0 / HD) + EPS)


def _pair_norm(t, raw_refs, w_refs, out_refs, g2_ref):
    for c in range(t // ATT_ROWS):
        sl = slice(c * ATT_ROWS, (c + 1) * ATT_ROWS)
        for raw, w_ref, out in zip(raw_refs, w_refs, out_refs):
            xv = raw[sl, :]
            out[sl, :] = xv * _pair_rstd(xv, g2_ref) * w_ref[...]


def _bias_tables(rb, bk):
    def body(rb_ref, bk_ref, bias_ref):
        pair = pl.program_id(0)
        for p in range(len(PATTERNS)):
            bk_p = bk_ref[p]
            for hh in range(2):
                head = 2 * pair + hh
                bm = jnp.full((BLK, 2 * BLK), NEG, F32)
                for b in range(N_BUCKETS):
                    bm = jnp.where(bk_p == b, rb_ref[head, b], bm)
                bias_ref[p, hh * BLK:(hh + 1) * BLK, :] = bm

    return pl.pallas_call(
        body, name="bias_tables", grid=(ATT_HEADS // 2,),
        in_specs=[pl.BlockSpec(memory_space=pltpu.SMEM), pl.BlockSpec(bk.shape, lambda i: (0, 0, 0))],
        out_specs=BIAS_SPEC,
        out_shape=jax.ShapeDtypeStruct((len(PATTERNS), ATT_HEADS // 2, 2 * BLK, 2 * BLK), F32),
        compiler_params=_params(("arbitrary",)),
    )(rb, bk)


BIAS_SPEC = pl.BlockSpec((len(PATTERNS), None, 2 * BLK, 2 * BLK), lambda i: (0, i, 0, 0))
PAIR_ROW_SPEC = pl.BlockSpec((1, 128), lambda i: (0, i))


def _stack_heads(xb, h0):
    return jnp.concatenate([jnp.where(h0, xb, 0.0), jnp.where(h0, 0.0, xb)], axis=0).astype(MXU)


def _block_rows(t, r, n):
    per_class = (t // r) // BLK
    res = n // per_class
    j = n % per_class
    start = res + BLK * r * j
    pstart = res + BLK * r * jnp.maximum(j - 1, 0)
    if r == 1:
        return pl.ds(pl.multiple_of(start, BLK), BLK), pl.ds(pl.multiple_of(pstart, BLK), BLK), j
    return pl.ds(start, BLK, stride=r), pl.ds(pstart, BLK, stride=r), j


def _att_fwd(qkv, gate, bias, wq, wk, g2):
    t = qkv.shape[0]
    rows = ATT_ROWS

    def body(bias_ref, qraw_ref, kraw_ref, v_ref, g_ref, wq_ref, wk_ref, g2_ref, o_ref, y_ref, lse_ref,
             o0_ref, o1_ref, o2_ref, l0_ref, l1_ref, l2_ref, q_ref, k_ref):
        h0 = _iota((BLK, 128), 1) < HD
        prev_cols = _iota((2 * BLK, 2 * BLK), 1) < BLK
        op_refs, lp_refs = (o0_ref, o1_ref, o2_ref), (l0_ref, l1_ref, l2_ref)
        _pair_norm(t, (qraw_ref, kraw_ref), (wq_ref, wk_ref), (q_ref, k_ref), g2_ref)

        for p, (_, r) in enumerate(PATTERNS):
            def blk(n, carry, p=p, r=r):
                cur, prev, j = _block_rows(t, r, n)
                q2 = _stack_heads(q_ref[cur, :], h0)
                k2 = jnp.concatenate([k_ref[prev, :], k_ref[cur, :]], axis=0).astype(MXU)
                v2 = jnp.concatenate([v_ref[prev, :], v_ref[cur, :]], axis=0).astype(MXU)
                s = _dot_nt(q2, k2) + bias_ref[p] + jnp.where(prev_cols & (j == 0), NEG, 0.0)
                m = jnp.max(s, axis=1, keepdims=True)
                e = jnp.exp(s - m)
                l = jnp.sum(e, axis=1, keepdims=True)
                pv = _dot(e, v2) / l
                lse = m + jnp.log(l)
                op_refs[p][cur, :] = jnp.where(h0, pv[:BLK], pv[BLK:])
                lp_refs[p][cur, :] = jnp.where(h0, lse[:BLK], lse[BLK:])
                return carry

            lax.fori_loop(0, t // BLK, blk, 0, unroll=16)

        for c in range(t // rows):
            sl = slice(c * rows, (c + 1) * rows)
            ls = [ref[sl, :] for ref in lp_refs]
            mx = jnp.maximum(jnp.maximum(ls[0], ls[1]), ls[2])
            ws = [jnp.exp(v_ - mx) for v_ in ls]
            den = ws[0] + ws[1] + ws[2]
            o = (ws[0] * o0_ref[sl, :] + ws[1] * o1_ref[sl, :] + ws[2] * o2_ref[sl, :]) / den
            o_ref[sl, :] = o
            y_ref[sl, :] = o * _silu(g_ref[sl, :])
            lse_ref[sl, :] = mx + jnp.log(den)

    col = lambda off: pl.BlockSpec((t, 128), lambda i, off=off: (0, off + i))
    return pl.pallas_call(
        body, name="att_fwd", grid=(ATT_HEADS // 2,),
        in_specs=[BIAS_SPEC, col(0), col(4), col(8), col(0), PAIR_ROW_SPEC, PAIR_ROW_SPEC,
                  pl.BlockSpec((128, 128), lambda i: (0, 0))],
        out_specs=[col(0), col(0), col(0)],
        out_shape=[jax.ShapeDtypeStruct((t, 512), F32)] * 3,
        scratch_shapes=[pltpu.VMEM((t, 128), F32)] * 8,
        compiler_params=_params(("arbitrary",)),
    )(bias, qkv, qkv, qkv, gate, wq, wk, g2)


def _outproj_loss(x, ydn, yatt, wout, target):
    t = x.shape[0]
    tm = 512

    def body(x_ref, a_ref, b_ref, w_ref, t_ref, dy_ref, mix_ref, loss_ref):
        @pl.when(pl.program_id(0) == 0)
        def _():
            loss_ref[...] = jnp.zeros_like(loss_ref)

        mixf = jnp.concatenate([a_ref[...], b_ref[...]], axis=1)
        mix_ref[...] = mixf.T.astype(MXU)
        err = x_ref[...] + jnp.dot(mixf.astype(MXU), w_ref[...], preferred_element_type=F32) - t_ref[...]
        dy_ref[...] = err * (1.0 / D_MODEL)
        loss_ref[...] += jnp.sum(err * err) * (0.5 / D_MODEL)

    row = lambda n: pl.BlockSpec((tm, n), lambda i: (i, 0))
    return pl.pallas_call(
        body, name="outproj_loss", grid=(t // tm,),
        in_specs=[row(D_MODEL), row(512), row(512), pl.BlockSpec(wout.shape, lambda i: (0, 0)), row(D_MODEL)],
        out_specs=[row(D_MODEL), pl.BlockSpec((D_MODEL, tm), lambda i: (0, i)), pl.BlockSpec((8, 128), lambda i: (0, 0))],
        out_shape=[jax.ShapeDtypeStruct((t, D_MODEL), F32), jax.ShapeDtypeStruct((D_MODEL, t), MXU),
                   jax.ShapeDtypeStruct((8, 128), F32)],
        compiler_params=_params(("arbitrary",)),
    )(x, ydn, yatt, wout, target)


def _outproj_bwd(dy, wout_t, oraw, z, dnw, oatt, gate, g, gt):
    t = dy.shape[0]
    tm = 512

    def body(dy_ref, w_ref, o_ref, z_ref, dnw_ref, oa_ref, g_ref, grp_ref, grpt_ref,
             do_ref, dz_ref, doa_ref, dg_ref, dd_ref, ddnw_ref):
        @pl.when(pl.program_id(0) == 0)
        def _():
            ddnw_ref[...] = jnp.zeros_like(ddnw_ref)

        dmix = jnp.dot(dy_ref[...].astype(MXU), w_ref[...], preferred_element_type=F32)
        dnw_v = dnw_ref[...]
        acc = jnp.zeros((1, DK), F32)
        for h in range(DN_HEADS):
            sl = slice(h * DK, (h + 1) * DK)
            o, zz, dm = o_ref[:, sl], z_ref[:, sl], dmix[:, sl]
            rs = lax.rsqrt(jnp.mean(o * o, axis=1, keepdims=True) + EPS)
            oh = o * rs
            silu_z, dsilu_z = _silu_and_grad(zz)
            dz_ref[:, sl] = dm * oh * dnw_v * dsilu_z
            d_on = dm * silu_z
            gg = d_on * dnw_v
            do_ref[:, sl] = rs * (gg - oh * jnp.mean(gg * oh, axis=1, keepdims=True))
            acc = acc + jnp.sum(d_on * oh, axis=0, keepdims=True)
        ddnw_ref[...] += jnp.broadcast_to(acc, (8, DK))
        da, gate_v, oa = dmix[:, 512:], g_ref[...], oa_ref[...]
        silu_g, dsilu_g = _silu_and_grad(gate_v)
        doa = da * silu_g
        doa_ref[...] = doa
        dg_ref[...] = da * oa * dsilu_g
        dd_ref[...] = _hdot(_hdot(doa * oa, grp_ref[...], "b"), grpt_ref[...], "b")

    row = lambda n: pl.BlockSpec((tm, n), lambda i: (i, 0))
    full = lambda a: pl.BlockSpec(a.shape, lambda i: (0,) * a.ndim)
    return pl.pallas_call(
        body, name="outproj_bwd", grid=(t // tm,),
        in_specs=[row(D_MODEL), full(wout_t), row(512), row(512), full(dnw), row(512), row(512), full(g), full(gt)],
        out_specs=[row(512)] * 5 + [pl.BlockSpec((8, DK), lambda i: (0, 0))],
        out_shape=[jax.ShapeDtypeStruct((t, 512), F32)] * 5 + [jax.ShapeDtypeStruct((8, DK), F32)],
        compiler_params=_params(("arbitrary",)),
    )(dy, wout_t, oraw, z, dnw, oatt, gate, g, gt)


def _grad_matmul(at, b, name):
    m, t = at.shape
    n = b.shape[1]
    tk = 1024
    tn = n if n <= 1536 else 512
    nk = t // tk

    def body(a_ref, b_ref, o_ref, acc_ref):
        k = pl.program_id(1)

        @pl.when(k == 0)
        def _():
            acc_ref[...] = jnp.zeros_like(acc_ref)

        acc_ref[...] += jnp.dot(a_ref[...], b_ref[...].astype(MXU), preferred_element_type=F32)

        @pl.when(k == nk - 1)
        def _():
            o_ref[...] = acc_ref[...].astype(GRAD_WIRE)

    return pl.pallas_call(
        body, name=name, grid=(n // tn, nk),
        in_specs=[pl.BlockSpec((m, tk), lambda j, k: (0, k)), pl.BlockSpec((tk, tn), lambda j, k: (k, j))],
        out_specs=pl.BlockSpec((m, tn), lambda j, k: (0, j)),
        out_shape=jax.ShapeDtypeStruct((m, n), GRAD_WIRE),
        scratch_shapes=[pltpu.VMEM((m, tn), F32)],
        compiler_params=_params(("arbitrary", "arbitrary")),
    )(at, b)


def _att_bwd(qkv, do, lse, dd, bias, bk, wq, wk, g2):
    t = qkv.shape[0]
    rows = ATT_ROWS

    def body(bias_ref, bk_ref, qraw_ref, kraw_ref, v_ref, do_ref, lse_ref, dd_ref, wq_ref, wk_ref, g2_ref,
             dq_ref, dk_ref, dv_ref, db_ref, dwq_ref, dwk_ref, ds_ref, q_ref, k_ref):
        pair = pl.program_id(0)

        @pl.when(pair == 0)
        def _():
            db_ref[...] = jnp.zeros_like(db_ref)

        _pair_norm(t, (qraw_ref, kraw_ref), (wq_ref, wk_ref), (q_ref, k_ref), g2_ref)
        ds_ref[...] = jnp.zeros_like(ds_ref)
        for c in range(t // rows):
            sl = slice(c * rows, (c + 1) * rows)
            for ref in (dq_ref, dk_ref, dv_ref):
                ref[sl, :] = jnp.zeros((rows, 128), F32)
        h0 = _iota((BLK, 128), 1) < HD
        prev_cols = _iota((2 * BLK, 2 * BLK), 1) < BLK

        def rows_of(xb):
            return jnp.concatenate([xb[:, 0:1], xb[:, HD:HD + 1]], axis=0)

        for p, (_, r) in enumerate(PATTERNS):
            def blk(n, carry, p=p, r=r):
                cur, prev, j = _block_rows(t, r, n)
                q2, do2 = _stack_heads(q_ref[cur, :], h0), _stack_heads(do_ref[cur, :], h0)
                k2 = jnp.concatenate([k_ref[prev, :], k_ref[cur, :]], axis=0).astype(MXU)
                v2 = jnp.concatenate([v_ref[prev, :], v_ref[cur, :]], axis=0).astype(MXU)
                s = _dot_nt(q2, k2) + bias_ref[p] + jnp.where(prev_cols & (j == 0), NEG, 0.0)
                prob = jnp.exp(s - rows_of(lse_ref[cur, :]))
                ds = prob * (_dot_nt(do2, v2) - rows_of(dd_ref[cur, :]))
                ds_ref[p] += ds
                dq2 = _dot(ds, k2)
                dk2 = _dot_tn(ds, q2)
                dv2 = _dot_tn(prob, do2)
                dq_ref[cur, :] += jnp.where(h0, dq2[:BLK], dq2[BLK:])
                dk_ref[prev, :] += dk2[:BLK]
                dv_ref[prev, :] += dv2[:BLK]
                dk_ref[cur, :] += dk2[BLK:]
                dv_ref[cur, :] += dv2[BLK:]
                return carry

            lax.fori_loop(0, t // BLK, blk, 0, unroll=8)

        ri, ci = _iota((8, 128), 0), _iota((8, 128), 1)
        upd = jnp.zeros((8, 128), F32)
        for p in range(len(PATTERNS)):
            bk = bk_ref[p]
            for hh in range(2):
                dsum = ds_ref[p, hh * BLK:(hh + 1) * BLK, :]
                for b in range(N_BUCKETS):
                    val = jnp.sum(jnp.where(bk == b, dsum, 0.0))
                    upd = upd + jnp.where((ri == 2 * pair + hh) & (ci == b), val, 0.0)
        db_ref[...] += upd

        for raw, d_ref, w_ref, dw_ref in ((qraw_ref, dq_ref, wq_ref, dwq_ref), (kraw_ref, dk_ref, wk_ref, dwk_ref)):
            acc = jnp.zeros((1, 128), F32)
            for c in range(t // rows):
                sl = slice(c * rows, (c + 1) * rows)
                xv, dyv = raw[sl, :], d_ref[sl, :]
                rs = _pair_rstd(xv, g2_ref)
                xh = xv * rs
                gg = dyv * w_ref[...]
                mean = _hdot(gg * xh, g2_ref[...], "b") * (1.0 / HD)
                d_ref[sl, :] = rs * (gg - xh * mean)
                acc = acc + jnp.sum(dyv * xh, axis=0, keepdims=True)
            dw_ref[...] = jnp.broadcast_to(acc, (8, 128))

    col = lambda off: pl.BlockSpec((t, 128), lambda i, off=off: (0, off + i))
    acc8 = pl.BlockSpec((8, 128), lambda i: (0, i))
    return pl.pallas_call(
        body, name="att_bwd", grid=(ATT_HEADS // 2,),
        in_specs=[BIAS_SPEC, pl.BlockSpec(bk.shape, lambda i: (0, 0, 0)),
                  col(0), col(4), col(8), col(0), col(0), col(0), PAIR_ROW_SPEC, PAIR_ROW_SPEC,
                  pl.BlockSpec((128, 128), lambda i: (0, 0))],
        out_specs=[col(0), col(0), col(0), pl.BlockSpec((8, 128), lambda i: (0, 0)), acc8, acc8],
        out_shape=[jax.ShapeDtypeStruct((t, 512), F32)] * 3 + [jax.ShapeDtypeStruct((8, 128), F32)]
                  + [jax.ShapeDtypeStruct((8, 512), F32)] * 2,
        scratch_shapes=[pltpu.VMEM((len(PATTERNS), 2 * BLK, 2 * BLK), F32)] + [pltpu.VMEM((t, 128), F32)] * 2,
        compiler_params=_params(("arbitrary",)),
    )(bias, bk, qkv, qkv, qkv, do, lse, dd, wq, wk, g2)


def _dn_scan_bwd(qkv, ba, do, sh, th, arow, dtb):
    t = qkv.shape[0]
    n = t // CHUNK
    c = CHUNK
    cps = SCAN_CHUNKS

    def body(qkv_ref, ba_ref, do_ref, sh_ref, th_ref, arow_ref, dtb_ref, dqkv_ref, dba_ref, ds_ref):
        @pl.when(pl.program_id(0) == 0)
        def _():
            ds_ref[...] = jnp.zeros_like(ds_ref)

        hs = range(DN_HEADS)
        chains = [(j, h) for j in range(cps) for h in hs]
        lane = _iota((c, 128), 1)
        row = _iota((c, 1), 0)
        ms = {}
        for j in range(cps):
            rows_j = slice(j * c, (j + 1) * c)
            mj, ri, ci = _chunk_common(qkv_ref[rows_j, :], ba_ref[rows_j, :], arow_ref[...], dtb_ref[...])
            for h in hs:
                ms[j, h] = mj[h]
        q, k, v = ({x: ms[x][nm] for x in chains} for nm in ("q", "k", "v"))
        beta, decay = ({x: ms[x][nm] for x in chains} for nm in ("beta", "decay"))
        eg, egl, etail = ({x: ms[x][nm] for x in chains} for nm in ("eg", "egl", "etail"))
        s = {x: sh_ref[x[0], x[1]] for x in chains}
        tinv = {x: th_ref[x[0], x[1]] for x in chains}
        d_o = {(j, h): do_ref[j * c:(j + 1) * c, h * DK:(h + 1) * DK] for j, h in chains}
        kb = {x: k[x] * beta[x] for x in chains}
        vb = {x: v[x] * beta[x] for x in chains}
        kbg = {x: kb[x] * eg[x] for x in chains}
        amat = {x: jnp.where(ri > ci, _dot_nt(kb[x], k[x]) * decay[x], 0.0) for x in chains}
        attn = {x: jnp.where(ri >= ci, _dot_nt(q[x], k[x]) * decay[x], 0.0) for x in chains}
        uw = {x: _hdot(tinv[x], jnp.concatenate([vb[x], kbg[x]], axis=1)) for x in chains}
        u = {x: uw[x][:, :DK] for x in chains}
        w = {x: uw[x][:, DK:] for x in chains}
        v_new = {x: u[x] - _dot(w[x], s[x]) for x in chains}
        q_dec = {x: q[x] * eg[x] for x in chains}
        k_tail = {x: k[x] * etail[x] for x in chains}
        d_attn = {x: jnp.where(ri >= ci, _dot_nt(d_o[x], v_new[x]), 0.0) for x in chains}
        d_qdec = {x: _dot_nt(d_o[x], s[x]) for x in chains}
        from_o = {x: _dot_tn(attn[x], d_o[x]) for x in chains}
        to_state = {x: _dot_tn(q_dec[x], d_o[x]) for x in chains}

        d_s, d_vnew = {}, {}
        cur = [ds_ref[h] for h in hs]
        for j in reversed(range(cps)):
            for h in hs:
                d_s[j, h] = cur[h]
                d_vnew[j, h] = from_o[j, h] + _dot(k_tail[j, h], cur[h])
            cur = [to_state[j, h] + cur[h] * egl[j, h] - _dot_tn(w[j, h], d_vnew[j, h]) for h in hs]
        for h in hs:
            ds_ref[h] = cur[h]

        d_ktail = {x: _dot_nt(v_new[x], d_s[x]) for x in chains}
        d_gl = {x: jnp.sum(s[x] * d_s[x]) * egl[x] for x in chains}
        d_w = {x: -_dot_nt(d_vnew[x], s[x]) for x in chains}
        d_both = {x: _hdot_tn(tinv[x], jnp.concatenate([d_vnew[x], d_w[x]], axis=1)) for x in chains}
        d_vb = {x: d_both[x][:, :DK] for x in chains}
        d_kbg = {x: d_both[x][:, DK:] for x in chains}
        d_a = {x: -jnp.where(ri > ci, _hdot_nt(d_both[x], uw[x]), 0.0) for x in chains}
        d_qk = {x: d_attn[x] * decay[x] for x in chains}
        d_kk = {x: d_a[x] * decay[x] for x in chains}
        d_kb = {x: _dot(d_kk[x], k[x]) + d_kbg[x] * eg[x] for x in chains}
        d_q = {x: _dot(d_qk[x], k[x]) + d_qdec[x] * eg[x] for x in chains}
        d_k = {x: _dot_tn(d_qk[x], q[x]) + _dot_tn(d_kk[x], kb[x]) + d_ktail[x] * etail[x] + d_kb[x] * beta[x]
               for x in chains}
        d_beta = {x: jnp.sum(d_kb[x] * k[x] + d_vb[x] * v[x], axis=1, keepdims=True) for x in chains}
        mm = {x: d_a[x] * amat[x] + d_attn[x] * attn[x] for x in chains}
        for j in range(cps):
            rows_j = slice(j * c, (j + 1) * c)
            rows = jnp.zeros((c, c), F32)
            for h in hs:
                rows = rows + jnp.where(ri == h, jnp.sum(mm[j, h], axis=0, keepdims=True), 0.0)
            cols_t = jnp.concatenate([rows, jnp.zeros((c, c), F32)], axis=1).T[:c, :]
            d_gc_all = jnp.zeros((c, 128), F32)
            for h in hs:
                x = (j, h)
                tail_term = jnp.sum(d_ktail[x] * k_tail[x], axis=1, keepdims=True)
                d_gc = (jnp.sum(mm[x], axis=1, keepdims=True) - _lane_col(cols_t, h)
                        + jnp.sum(d_qdec[x] * q_dec[x] + d_kbg[x] * kbg[x], axis=1, keepdims=True) - tail_term)
                d_gc = d_gc + jnp.where(row == c - 1, jnp.sum(tail_term) + d_gl[x], 0.0)
                d_gc_all = d_gc_all + jnp.where(lane == DN_HEADS + h, d_gc, 0.0)
            d_g_all = _hdot((ri <= ci).astype(F32), d_gc_all, "a")
            dba = jnp.zeros((c, 128), F32)
            for h in hs:
                x = (j, h)
                d_g = _lane_col(d_g_all, DN_HEADS + h)
                d_braw = d_beta[x] * beta[x] * (1.0 - beta[x])
                d_araw = d_g * ms[x]["a_h"] * _sigmoid(ms[x]["a_raw"] + ms[x]["dt_h"])
                dba = dba + jnp.where(lane == h, d_braw, 0.0) + jnp.where(lane == DN_HEADS + h, d_araw, 0.0) \
                    + jnp.where(lane == 2 * DN_HEADS + h, d_g * ms[x]["g"], 0.0)
                dqkv_ref[rows_j, h * DK:(h + 1) * DK] = d_q[x]
                dqkv_ref[rows_j, D_DN + h * DK:D_DN + (h + 1) * DK] = d_k[x]
                dqkv_ref[rows_j, 2 * D_DN + h * DK:2 * D_DN + (h + 1) * DK] = d_vb[x] * beta[x]
            dba_ref[rows_j, :] = dba

    nsteps = n // cps
    rev = lambda w_: pl.BlockSpec((cps * c, w_), lambda i: (nsteps - 1 - i, 0))
    one = pl.BlockSpec((1, 128), lambda i: (0, 0))
    return pl.pallas_call(
        body, name="dn_scan_bwd", grid=(nsteps,),
        in_specs=[rev(1536), rev(128), rev(512),
                  pl.BlockSpec((cps, DN_HEADS, DK, DK), lambda i: (nsteps - 1 - i, 0, 0, 0)),
                  pl.BlockSpec((cps, DN_HEADS, c, c), lambda i: (nsteps - 1 - i, 0, 0, 0)), one, one],
        out_specs=[rev(1536), rev(128)],
        out_shape=[jax.ShapeDtypeStruct((t, 1536), F32), jax.ShapeDtypeStruct((t, 128), F32)],
        scratch_shapes=[pltpu.VMEM((DN_HEADS, DK, DK), F32)],
        compiler_params=_params(("arbitrary",)),
    )(qkv, ba, do, sh, th, arow, dtb)


def _dn_prep_bwd(pdn, cw, dact):
    t = pdn.shape[0]
    nchunk = t // CONV_ROWS

    def body(u_ref, w_ref, d_ref, du_ref, dw_ref, dy_ref):
        j = pl.program_id(0)
        dy_ref[t:t + 8, :] = jnp.zeros((8, 128), F32)
        dw = [jnp.zeros((1, 128), F32) for _ in range(4)]
        for c in range(nchunk):
            sl = slice(c * CONV_ROWS, (c + 1) * CONV_ROWS)
            taps, y = _conv_taps(u_ref, c, w_ref)
            a, da_dy = _silu_and_grad(y)
            dout = d_ref[sl, :]
            rs = lax.rsqrt(jnp.sum(a * a, axis=1, keepdims=True) + EPS)
            f = jnp.where(j < 8, rs, 1.0) * jnp.where(j < 4, DK ** -0.5, 1.0)
            corr = jnp.where(j < 8, f * rs * rs * jnp.sum(dout * a, axis=1, keepdims=True), 0.0)
            dy = (f * dout - corr * a) * da_dy
            dy_ref[sl, :] = dy
            for k_ in range(4):
                dw[3 - k_] = dw[3 - k_] + jnp.sum(taps[k_] * dy, axis=0, keepdims=True)
        for i in range(4):
            dw_ref[i:i + 1, :] = dw[i]
        for c in range(nchunk):
            r0 = c * CONV_ROWS
            ext = dy_ref[r0:r0 + CONV_ROWS + 8, :]
            du = ext[:CONV_ROWS, :] * w_ref[3:4, :]
            for k_ in (1, 2, 3):
                du = du + pltpu.roll(ext, CONV_ROWS + 8 - k_, 0)[:CONV_ROWS, :] * w_ref[3 - k_:4 - k_, :]
            du_ref[r0:r0 + CONV_ROWS, :] = du

    return pl.pallas_call(
        body, name="dn_prep_bwd", grid=(12,),
        in_specs=[pl.BlockSpec((t, 128), lambda j: (0, j)), pl.BlockSpec((4, 128), lambda j: (0, j)),
                  pl.BlockSpec((t, 128), lambda j: (0, j))],
        out_specs=[pl.BlockSpec((t, 128), lambda j: (0, j)), pl.BlockSpec((4, 128), lambda j: (0, j))],
        out_shape=[jax.ShapeDtypeStruct((t, 1536), F32), jax.ShapeDtypeStruct((4, 1536), F32)],
        scratch_shapes=[pltpu.VMEM((t + 8, 128), F32)],
        compiler_params=_params(("arbitrary",)),
    )(pdn, cw, dact)


SECTIONS = (("dn", 0, 1536), ("z", 1536, 512), ("q", 2048, 512), ("k", 2560, 512), ("v", 3072, 512),
            ("gate", 3584, 512), ("ba", 4096, 128))


def _inproj_bwd(x, nw, wt, dy, dsecs, partials):
    t = x.shape[0]
    tm = 256
    npart = len(partials)
    nsteps = t // tm

    nsec = len(SECTIONS)

    def body(x_ref, nw_ref, w_ref, dy_ref, *rest):
        sec_refs, rest = rest[:nsec], rest[nsec:]
        part_refs, (gx_ref, dnw_ref, cs_ref) = rest[:npart], rest[npart:npart + 3]
        got_refs, (send, recv, loc) = rest[npart + 3:2 * npart + 3], rest[2 * npart + 3:]
        starts, waits = _chip_swap_copies(part_refs, got_refs, send, recv, loc)

        @pl.when(pl.program_id(0) == 0)
        def _():
            for start in starts:
                start()
            dnw_ref[...] = jnp.zeros_like(dnw_ref)
            cs_ref[...] = jnp.zeros_like(cs_ref)

        @pl.when(pl.program_id(0) == nsteps - 1)
        def _():
            for wait in waits:
                wait()

        dh = jnp.zeros((tm, D_MODEL), F32)
        for ref, (_, lo, width) in zip(sec_refs, SECTIONS):
            dh = dh + jnp.dot(ref[...].astype(MXU), w_ref[lo:lo + width, :], preferred_element_type=F32)
        xv = x_ref[...]
        rstd = lax.rsqrt(jnp.mean(xv * xv, axis=-1, keepdims=True) + EPS)
        xh = xv * rstd
        gg = dh * nw_ref[...]
        gx_ref[...] = rstd * (gg - xh * jnp.mean(gg * xh, axis=-1, keepdims=True)) + dy_ref[...]
        dnw_ref[...] += jnp.broadcast_to(jnp.sum(dh * xh, axis=0, keepdims=True), (8, D_MODEL))
        cs_ref[...] += jnp.broadcast_to(jnp.sum(sec_refs[nsec - 1][...], axis=0, keepdims=True), (8, 128))

    row = lambda n: pl.BlockSpec((tm, n), lambda i: (i, 0))
    full = lambda a: pl.BlockSpec(a.shape, lambda i: (0,) * a.ndim)
    res = pl.pallas_call(
        body, name="inproj_bwd", grid=(nsteps,),
        in_specs=[row(D_MODEL), full(nw), full(wt), row(D_MODEL)] + [row(width) for _, _, width in SECTIONS]
                 + [ANY_SPEC] * npart,
        out_specs=[row(D_MODEL), pl.BlockSpec((8, D_MODEL), lambda i: (0, 0)), pl.BlockSpec((8, 128), lambda i: (0, 0))]
                  + [ANY_SPEC] * npart,
        out_shape=[jax.ShapeDtypeStruct((t, D_MODEL), F32), jax.ShapeDtypeStruct((8, D_MODEL), F32),
                   jax.ShapeDtypeStruct((8, 128), F32)] + [jax.ShapeDtypeStruct(p.shape, p.dtype) for p in partials],
        scratch_shapes=[pltpu.SemaphoreType.DMA((npart, 3)), pltpu.SemaphoreType.DMA((npart, 3)),
                        pltpu.SemaphoreType.DMA((npart,))],
        compiler_params=_params(("arbitrary",)),
    )(x, nw, wt, dy, *dsecs, *partials)
    return res[0], res[1], res[2], res[3:]


def _adamw_sum(w, gs, m, v, name):
    r, c = w.shape
    nsum = gs.shape[0]
    tr = r if r <= 512 else 512
    c1 = 1.0 - ADAM_B1 ** ADAM_STEP
    c2 = 1.0 - ADAM_B2 ** ADAM_STEP

    def body(w_ref, g_ref, m_ref, v_ref, go_ref, d_ref, mo_ref, vo_ref):
        g = g_ref[0].astype(F32)
        for s in range(1, nsum):
            g = g + g_ref[s].astype(F32)
        mn = ADAM_B1 * m_ref[...] + (1.0 - ADAM_B1) * g
        vn = ADAM_B2 * v_ref[...] + (1.0 - ADAM_B2) * (g * g)
        go_ref[...] = g
        mo_ref[...] = mn
        vo_ref[...] = vn
        d_ref[...] = -ADAM_LR * ((mn / c1) / (jnp.sqrt(vn / c2) + ADAM_EPS) + ADAM_WD * w_ref[...])

    blk = pl.BlockSpec((tr, c), lambda i: (i, 0))
    return pl.pallas_call(
        body, name=name, grid=(r // tr,),
        in_specs=[blk, pl.BlockSpec((nsum, tr, c), lambda i: (0, i, 0)), blk, blk],
        out_specs=[blk] * 4, out_shape=[jax.ShapeDtypeStruct((r, c), F32)] * 4,
        compiler_params=_params(("arbitrary",)),
    )(w, gs, m, v)


def _local_step(x, target, h, ht, w_sect, conv_w, a_log, dt_bias, dn_norm_w, q_norm_w, k_norm_w, rel_bias, wout_shard):
    arow = jnp.zeros((1, 128), F32).at[0, DN_HEADS:2 * DN_HEADS].set(-jnp.exp(a_log[0]))
    dtb = jnp.zeros((1, 128), F32).at[0, DN_HEADS:2 * DN_HEADS].set(dt_bias[0])
    g_np, gt_np = _group_mats()
    g, gt = jnp.asarray(g_np), jnp.asarray(gt_np)
    g2 = jnp.asarray(np.kron(np.eye(2, dtype=np.float32), np.ones((HD, HD), np.float32)))
    bk = jnp.asarray(_bucket_tables())
    wq = jnp.tile(q_norm_w, (1, ATT_HEADS)) * (HD ** -0.5)
    wk = jnp.tile(k_norm_w, (1, ATT_HEADS))

    pdn, qkv_dn, z, patt, gate, ba, wout8 = _inproj(h, w_sect, conv_w, wout_shard)
    w_out = wout8.reshape(D_MODEL, D_MODEL)
    oraw, ydn, sh, th = _dn_scan_fwd(qkv_dn, ba, z, arow, dtb, dn_norm_w)
    bias = _bias_tables(rel_bias, bk)
    oatt, yatt, lse = _att_fwd(patt, gate, bias, wq, wk, g2)
    dy, mix_t, loss8 = _outproj_loss(x, ydn, yatt, w_out, target)

    do_dn, dz, do_att, dgate, dd, ddnw = _outproj_bwd(dy, w_out.T, oraw, z, dn_norm_w, oatt, gate, g, gt)
    d_wout = _grad_matmul(mix_t, dy, "dw_out")
    dq, dk, dv, drb, dwq8, dwk8 = _att_bwd(patt, do_att, lse, dd, bias, bk, wq, wk, g2)
    dqkv_dn, dba = _dn_scan_bwd(qkv_dn, ba, do_dn, sh, th, arow, dtb)
    dpdn, d_conv = _dn_prep_bwd(pdn, conv_w, dqkv_dn)
    dsecs = (dpdn, dz, dq, dk, dv, dgate, dba)
    dw_sections = [_grad_matmul(ht, d_, "dw_in_" + nm) for d_, (nm, _, _) in zip(dsecs, SECTIONS)]
    return dict(w_in_sections=dw_sections, conv_w=d_conv, w_out=d_wout, dy=dy, dsecs=dsecs,
                small_parts=(loss8, ddnw, dwq8, dwk8, drb))


def _finish_step(x, norm_w, w_sect_t, gr, partials):
    grad_x, dnw8, cs8, got = _inproj_bwd(x, norm_w, w_sect_t, gr["dy"], gr["dsecs"], partials)
    return grad_x, _pack_small_grads(dnw8, cs8, *gr["small_parts"]), got


SMALL_ROWS = 24
SMALL_AT = dict(a_log=(slice(8, 9), slice(0, 4)), dt_bias=(slice(9, 10), slice(0, 4)),
                dn_norm_w=(slice(10, 11), slice(0, 128)), q_norm_w=(slice(11, 12), slice(0, HD)),
                k_norm_w=(slice(12, 13), slice(0, HD)), rel_bias=(slice(16, 24), slice(0, N_BUCKETS)))
SMALL_NAMES = ("norm_w", "a_log", "dt_bias", "dn_norm_w", "q_norm_w", "k_norm_w", "rel_bias")


LOSS_ROW = 13


def _pack_small_grads(dnw8, cs8, loss8, ddnw8, dwq8, dwk8, drb):
    def body(dnw_ref, cs_ref, loss_ref, ddnw_ref, dwq_ref, dwk_ref, drb_ref, o_ref):
        lane = _iota((8, 128), 1)
        o_ref[...] = jnp.zeros_like(o_ref)
        o_ref[LOSS_ROW:LOSS_ROW + 1, :] = jnp.where(lane == 0, loss_ref[...], 0.0)[0:1, :]
        for k in range(D_MODEL // 128):
            o_ref[k:k + 1, :] = dnw_ref[0:1, k * 128:(k + 1) * 128]
        cs = cs_ref[...]
        o_ref[8:9, :] = jnp.where(lane < DN_HEADS, pltpu.roll(cs, 128 - 2 * DN_HEADS, 1), 0.0)[0:1, :]
        o_ref[9:10, :] = jnp.where(lane < DN_HEADS, pltpu.roll(cs, 128 - DN_HEADS, 1), 0.0)[0:1, :]
        o_ref[10:11, :] = ddnw_ref[0:1, :]
        for row, ref, scale in ((11, dwq_ref, HD ** -0.5), (12, dwk_ref, 1.0)):
            acc = ref[:, 0:128] + ref[:, 128:256] + ref[:, 256:384] + ref[:, 384:512]
            acc = (acc + pltpu.roll(acc, HD, 1)) * scale
            o_ref[row:row + 1, :] = jnp.where(lane < HD, acc, 0.0)[0:1, :]
        o_ref[16:24, :] = drb_ref[...]

    return pl.pallas_call(body, name="pack_small_grads", out_shape=jax.ShapeDtypeStruct((SMALL_ROWS, 128), F32),
                          )(dnw8, cs8, loss8, ddnw8, dwq8, dwk8, drb)


def _adam_math(w, g, m, v):
    c1 = 1.0 - ADAM_B1 ** ADAM_STEP
    c2 = 1.0 - ADAM_B2 ** ADAM_STEP
    mn = ADAM_B1 * m + (1.0 - ADAM_B1) * g
    vn = ADAM_B2 * v + (1.0 - ADAM_B2) * (g * g)
    return -ADAM_LR * ((mn / c1) / (jnp.sqrt(vn / c2) + ADAM_EPS) + ADAM_WD * w), mn, vn


def _adamw_small(gs, ws, ms, vs):
    n = len(SMALL_NAMES)

    def body(g_ref, *refs):
        w_refs, m_refs, v_refs = refs[:n], refs[n:2 * n], refs[2 * n:3 * n]
        outs, loss_ref = refs[3 * n:7 * n], refs[7 * n]
        loss = g_ref[0, LOSS_ROW:LOSS_ROW + 1, :]
        for s in range(1, gs.shape[0]):
            loss = loss + g_ref[s, LOSS_ROW:LOSS_ROW + 1, :]
        loss_ref[...] = loss

        def one(i, rows, lanes, at):
            g = g_ref[0, rows, lanes]
            for s in range(1, gs.shape[0]):
                g = g + g_ref[s, rows, lanes]
            d, mn, vn = _adam_math(w_refs[i][at], g, m_refs[i][at], v_refs[i][at])
            for kind, val in enumerate((g, d, mn, vn)):
                outs[kind * n + i][at] = val

        for k in range(D_MODEL // 128):
            one(0, slice(k, k + 1), slice(0, 128), (slice(0, 1), slice(k * 128, (k + 1) * 128)))
        for i, nm in enumerate(SMALL_NAMES[1:], start=1):
            rows, lanes = SMALL_AT[nm]
            one(i, rows, lanes, (slice(None), slice(None)))

    shapes = [jax.ShapeDtypeStruct(w.shape, F32) for w in ws]
    res = pl.pallas_call(body, name="adamw_small",
                         out_shape=shapes * 4 + [jax.ShapeDtypeStruct((1, 128), F32)])(gs, *ws, *ms, *vs)
    return [res[k * n:(k + 1) * n] for k in range(4)], res[4 * n]


def kernel(x, norm_w, w_in, conv_w, a_log, dt_bias, dn_norm_w, q_norm_w, k_norm_w, rel_bias, w_out, loss_target, m_norm_w, m_w_in, m_conv_w, m_a_log, m_dt_bias, m_dn_norm_w, m_q_norm_w, m_k_norm_w, m_rel_bias, m_w_out, v_norm_w, v_w_in, v_conv_w, v_a_log, v_dt_bias, v_dn_norm_w, v_q_norm_w, v_k_norm_w, v_rel_bias, v_w_out):
    assert w_in.shape[2] == SHARD_COLS
    h, ht, (win8, conv8) = _norm_and_gather(x[0], norm_w, [w_in[0].astype(MXU), conv_w[0]])
    w_sect, w_sect_t = _build_w(win8)
    conv_full = conv8.transpose(1, 0, 2).reshape(4, 3 * D_DN)

    gr = _local_step(x[0], loss_target[0], h, ht, w_sect, conv_full, a_log, dt_bias, dn_norm_w, q_norm_w,
                     k_norm_w, rel_bias, w_out[0].astype(MXU))

    slabs = [_build_slabs(gr["w_in_sections"]),
             gr["w_out"].reshape(4, 2, D_MODEL // N_DEV, D_MODEL).transpose(1, 0, 2, 3),
             gr["conv_w"].reshape(4, 4, 2, 3 * D_DN // N_DEV).transpose(2, 1, 0, 3)]
    core = lax.axis_index("c").astype(jnp.int32).reshape(1)
    from_sibling = _swap_siblings(slabs)
    wires = (GRAD_WIRE, GRAD_WIRE, F32)
    partial = [_chip_sum(slabs[i], from_sibling[i], core, wires[i], "chip_sum_%d" % i) for i in range(3)]
    grad_x, small_pack, (r_win, r_wout, r_conv) = _finish_step(x[0], norm_w, w_sect_t, gr, partial)
    r_small = _share_small(small_pack)

    g_win, d_win, m_win, v_win = _adamw_sum(w_in[0], r_win, m_w_in[0], v_w_in[0], "adamw_w_in")
    g_wout, d_wout, m_wout, v_wout = _adamw_sum(w_out[0], r_wout, m_w_out[0], v_w_out[0], "adamw_w_out")
    g_conv, d_conv, m_conv, v_conv = _adamw_sum(conv_w[0], r_conv, m_conv_w[0], v_conv_w[0], "adamw_conv_w")
    small, loss_row = _adamw_small(r_small,
                                   (norm_w, a_log, dt_bias, dn_norm_w, q_norm_w, k_norm_w, rel_bias),
                                   (m_norm_w, m_a_log, m_dt_bias, m_dn_norm_w, m_q_norm_w, m_k_norm_w, m_rel_bias),
                                   (v_norm_w, v_a_log, v_dt_bias, v_dn_norm_w, v_q_norm_w, v_k_norm_w, v_rel_bias))

    loss = loss_row[0, 0]
    names = ("norm_w", "w_in", "conv_w", "a_log", "dt_bias", "dn_norm_w", "q_norm_w", "k_norm_w", "rel_bias", "w_out")
    big = dict(w_in=(g_win, d_win, m_win, v_win), conv_w=(g_conv, d_conv, m_conv, v_conv),
               w_out=(g_wout, d_wout, m_wout, v_wout))
    outs = [loss, grad_x[None]]
    for kind in range(4):
        for nm in names:
            outs.append(big[nm][kind][None] if nm in big else small[kind][SMALL_NAMES.index(nm)])
    return tuple(outs)
```

```python
import math

import numpy as np
import jax
import jax.numpy as jnp
from jax import lax
from jax.experimental import pallas as pl
from jax.experimental.pallas import tpu as pltpu

F32 = jnp.float32
MXU = jnp.bfloat16
GRAD_WIRE = jnp.bfloat16

D_MODEL = 1024
D_DN = 512
DN_HEADS = 4
DK = 128
CHUNK = 64
D_ATT = 512
ATT_HEADS = 8
HD = 64
PATTERNS = ((128, 1), (512, 4), (2048, 16))
BLK = 128
N_BUCKETS = 32
MAX_DISTANCE = 2048
EPS = 1e-6
W_COLS = 4224
N_DEV = 8
AXES = ("x", "y", "c")

ADAM_LR = 0.001
ADAM_B1 = 0.9
ADAM_B2 = 0.999
ADAM_EPS = 1e-08
ADAM_WD = 0.01
ADAM_STEP = 10

VMEM_LIMIT = 56 * 1024 * 1024
NEG = -1e30


def _dot(a, b):
    return jnp.dot(a.astype(MXU), b.astype(MXU), preferred_element_type=F32)


def _dot_nt(a, b):
    return lax.dot_general(a.astype(MXU), b.astype(MXU), (((1,), (1,)), ((), ())), preferred_element_type=F32)


def _dot_tn(a, b):
    return lax.dot_general(a.astype(MXU), b.astype(MXU), (((0,), (0,)), ((), ())), preferred_element_type=F32)


def _split(a):
    hi = a.astype(jnp.bfloat16)
    return hi, (a - hi.astype(F32)).astype(jnp.bfloat16)


def _dot_split(a, b, dims, exact):
    dg = lambda u, v: lax.dot_general(u, v, (dims, ((), ())), preferred_element_type=F32)
    if exact == "b":
        ah, al = _split(a)
        bh = b.astype(jnp.bfloat16)
        return dg(ah, bh) + dg(al, bh)
    if exact == "a":
        bh, bm = _split(b)
        bl = (b - bh.astype(F32) - bm.astype(F32)).astype(jnp.bfloat16)
        ah = a.astype(jnp.bfloat16)
        return dg(ah, bh) + (dg(ah, bm) + dg(ah, bl))
    ah, al = _split(a)
    bh, bl = _split(b)
    return dg(ah, bh) + (dg(ah, bl) + dg(al, bh))


def _wy_inverses(amat, eye):
    tinv = {x: eye - amat[x] for x in amat}
    pw = amat
    for _ in range(5):
        pw = {x: _hdot(pw[x], pw[x]) for x in amat}
        tinv = {x: tinv[x] + _hdot(tinv[x], pw[x]) for x in amat}
    return tinv


def _hdot(a, b, exact=None):
    return _dot_split(a, b, ((1,), (0,)), exact)


def _hdot_nt(a, b, exact=None):
    return _dot_split(a, b, ((1,), (1,)), exact)


def _hdot_tn(a, b, exact=None):
    return _dot_split(a, b, ((0,), (0,)), exact)


def _sigmoid(x):
    return 1.0 / (1.0 + jnp.exp(-x))


def _silu(x):
    return x * _sigmoid(x)


def _silu_and_grad(x):
    s = _sigmoid(x)
    return x * s, s * (1.0 + x * (1.0 - s))


def _softplus(x):
    return jnp.maximum(x, 0.0) + jnp.log(1.0 + jnp.exp(-jnp.abs(x)))


def _iota(shape, dim):
    return lax.broadcasted_iota(jnp.int32, shape, dim)


def _lane_col(x, k):
    return jnp.sum(jnp.where(_iota(x.shape, 1) == k, x, 0.0), axis=1, keepdims=True)


def _params(sem=None):
    return pltpu.CompilerParams(dimension_semantics=sem, vmem_limit_bytes=VMEM_LIMIT)


def _t5_bucket(dist):
    max_exact = N_BUCKETS // 2
    d = np.maximum(dist, 1).astype(np.float64)
    large = max_exact + (np.log(d / max_exact) / math.log(MAX_DISTANCE / max_exact)
                         * (N_BUCKETS - max_exact)).astype(np.int32)
    large = np.minimum(large, N_BUCKETS - 1)
    return np.where(dist < max_exact, dist, large).astype(np.int32)


def _bucket_tables():
    qi = np.arange(BLK)[:, None]
    kj = np.arange(2 * BLK)[None, :]
    step = qi - kj + BLK
    band = (step >= 0) & (step <= BLK)
    out = []
    for _, r in PATTERNS:
        b = _t5_bucket(np.clip(step, 0, None) * r)
        out.append(np.where(band, b, -1))
    return np.stack(out).astype(np.int32)


def _group_mats():
    g = np.zeros((D_ATT, 128), np.float32)
    for h in range(ATT_HEADS):
        g[h * HD:(h + 1) * HD, h] = 1.0
    return g, np.ascontiguousarray(g.T)


CHIP_FLIPS = ((1, 0), (0, 1), (1, 1))
ANY_SPEC = pl.BlockSpec(memory_space=pl.ANY)
MESH_ID = pl.DeviceIdType.MESH


def _other_chips():
    x, y = lax.axis_index("x"), lax.axis_index("y")
    return [((1 - x if fx else x), (1 - y if fy else y)) for fx, fy in CHIP_FLIPS]


def _gather_plan(ins, outs, send, recv, loc):
    n = len(ins)
    x, y, c = (lax.axis_index(a) for a in AXES)
    sib = (x, y, 1 - c)
    chips = _other_chips()
    lin = lambda px, py, pc: 4 * px + 2 * py + pc

    def copy(a, k, block, to, src=None):
        slot = outs[a].at[lin(*block)]
        return pltpu.make_async_remote_copy(src_ref=slot if src is None else src, dst_ref=slot,
                                            send_sem=send.at[a, k], recv_sem=recv.at[a, k],
                                            device_id=to, device_id_type=MESH_ID)

    mine = [pltpu.make_async_copy(ins[a], outs[a].at[lin(x, y, c)], loc.at[a]) for a in range(n)]
    firsts = []
    for a in range(n):
        firsts.append(copy(a, 0, (x, y, c), sib, src=ins[a]))
        firsts += [copy(a, 1 + j, (x, y, c), (*chip, c), src=ins[a]) for j, chip in enumerate(chips)]

    def begin():
        for cp in mine + firsts:
            cp.start()

    def finish():
        passed = []
        for j, chip in enumerate(chips):
            for a in range(n):
                copy(a, 1 + j, (*chip, c), (x, y, c)).wait_recv()
                fw = copy(a, 4 + j, (*chip, c), sib)
                fw.start()
                passed.append(fw)
        for a in range(n):
            copy(a, 0, sib, (x, y, c)).wait_recv()
            for j, chip in enumerate(chips):
                copy(a, 4 + j, (*chip, 1 - c), (x, y, c)).wait_recv()
        for cp in firsts + passed:
            cp.wait_send()
        for cp in mine:
            cp.wait()

    return begin, finish


GATHER_SEMS = lambda n: [pltpu.SemaphoreType.DMA((n, 7)), pltpu.SemaphoreType.DMA((n, 7)), pltpu.SemaphoreType.DMA((n,))]


def _swap_siblings(arrs):
    n = len(arrs)

    def body(*refs):
        ins, outs = refs[:n], refs[n:2 * n]
        send, recv = refs[2 * n:]
        x, y, c = (lax.axis_index(a) for a in AXES)
        cps = [pltpu.make_async_remote_copy(src_ref=ins[a].at[1 - c], dst_ref=outs[a], send_sem=send.at[a],
                                            recv_sem=recv.at[a], device_id=(x, y, 1 - c), device_id_type=MESH_ID)
               for a in range(n)]
        for cp in cps:
            cp.start()
        for cp in cps:
            cp.wait()

    return pl.pallas_call(
        body, name="swap_siblings", out_shape=[jax.ShapeDtypeStruct(a.shape[1:], a.dtype) for a in arrs],
        in_specs=[ANY_SPEC] * n, out_specs=[ANY_SPEC] * n,
        scratch_shapes=[pltpu.SemaphoreType.DMA((n,)), pltpu.SemaphoreType.DMA((n,))],
    )(*arrs)


def _chip_sum(mine2, theirs, core, wire, name):
    _, nchip, r, cdim = mine2.shape
    tr = r if r <= 1024 else 1024

    def body(core_ref, a_ref, b_ref, o_ref):
        del core_ref
        o_ref[...] = (a_ref[...].astype(F32) + b_ref[...].astype(F32)).astype(wire)

    grid_spec = pltpu.PrefetchScalarGridSpec(
        num_scalar_prefetch=1, grid=(nchip, r // tr),
        in_specs=[pl.BlockSpec((None, None, tr, cdim), lambda j, i, cr: (cr[0], j, i, 0)),
                  pl.BlockSpec((None, tr, cdim), lambda j, i, cr: (j, i, 0))],
        out_specs=pl.BlockSpec((None, tr, cdim), lambda j, i, cr: (j, i, 0)))
    return pl.pallas_call(
        body, name=name, grid_spec=grid_spec, out_shape=jax.ShapeDtypeStruct((nchip, r, cdim), wire),
        compiler_params=_params(("arbitrary", "arbitrary")),
    )(core, mine2, theirs)


def _chip_swap_copies(ins, outs, send, recv, loc):
    x, y, c = (lax.axis_index(a) for a in AXES)
    me = 2 * x + y
    starts, arrivals, drains = [], [], []
    for a in range(len(ins)):
        lc = pltpu.make_async_copy(ins[a].at[me], outs[a].at[me], loc.at[a])
        starts.append(lc.start)
        drains.append(lc.wait)
        for j, (px, py) in enumerate(_other_chips()):
            them = 2 * px + py
            cp = pltpu.make_async_remote_copy(src_ref=ins[a].at[them], dst_ref=outs[a].at[me], send_sem=send.at[a, j],
                                              recv_sem=recv.at[a, j], device_id=(px, py, c), device_id_type=MESH_ID)
            landing = pltpu.make_async_remote_copy(src_ref=ins[a].at[them], dst_ref=outs[a].at[them],
                                                   send_sem=send.at[a, j], recv_sem=recv.at[a, j],
                                                   device_id=(px, py, c), device_id_type=MESH_ID)
            starts.append(cp.start)
            arrivals.append(landing.wait_recv)
            drains.append(cp.wait_send)
    return starts, arrivals + drains


def _share_small(pack):
    def body(in_ref, out_ref, send, recv, loc):
        x, y, c = (lax.axis_index(a) for a in AXES)
        me = 4 * x + 2 * y + c
        lc = pltpu.make_async_copy(in_ref, out_ref.at[me], loc.at[0])
        lc.start()
        sends, arrivals = [], []
        for k in range(1, N_DEV):
            px = 1 - x if k & 4 else x
            py = 1 - y if k & 2 else y
            pc = 1 - c if k & 1 else c
            cp = pltpu.make_async_remote_copy(src_ref=in_ref, dst_ref=out_ref.at[me], send_sem=send.at[k - 1],
                                              recv_sem=recv.at[k - 1], device_id=(px, py, pc), device_id_type=MESH_ID)
            cp.start()
            sends.append(cp)
            arrivals.append(pltpu.make_async_remote_copy(src_ref=in_ref, dst_ref=out_ref.at[4 * px + 2 * py + pc],
                                                         send_sem=send.at[k - 1], recv_sem=recv.at[k - 1],
                                                         device_id=(px, py, pc), device_id_type=MESH_ID))
        for cp in arrivals:
            cp.wait_recv()
        for cp in sends:
            cp.wait_send()
        lc.wait()

    return pl.pallas_call(
        body, name="share_small", out_shape=jax.ShapeDtypeStruct((N_DEV,) + pack.shape, pack.dtype),
        in_specs=[ANY_SPEC], out_specs=ANY_SPEC,
        scratch_shapes=[pltpu.SemaphoreType.DMA((N_DEV - 1,)), pltpu.SemaphoreType.DMA((N_DEV - 1,)),
                        pltpu.SemaphoreType.DMA((1,))],
    )(pack)


W_PARTS = ((0, 0, 2048), (2048, 4096, 8), (2056, 2048, 2048))
SHARD_COLS = 513


def _pieces(lo, hi, parts):
    out = []
    for ref_start, tgt_start, width in parts:
        a, b = max(lo, ref_start), min(hi, ref_start + width)
        if a < b:
            out.append((a - lo, tgt_start + a - ref_start, b - a))
    return out


def _build_w(win8):
    tr = 512

    def body(in_ref, w_ref, wt_ref):
        w_ref[:, 4096:W_COLS] = jnp.zeros((tr, W_COLS - 4096), MXU)
        for p in range(N_DEV):
            for src, dst, width in _pieces(p * SHARD_COLS, (p + 1) * SHARD_COLS, W_PARTS):
                w_ref[:, dst:dst + width] = in_ref[p, :, src:src + width]
        for k in range(W_COLS // 128):
            wt_ref[k * 128:(k + 1) * 128, :] = w_ref[:, k * 128:(k + 1) * 128].astype(F32).T.astype(MXU)

    return pl.pallas_call(
        body, name="build_w", grid=(D_MODEL // tr,),
        in_specs=[pl.BlockSpec((N_DEV, tr, SHARD_COLS), lambda i: (0, i, 0))],
        out_specs=[pl.BlockSpec((tr, W_COLS), lambda i: (i, 0)), pl.BlockSpec((W_COLS, tr), lambda i: (0, i))],
        out_shape=[jax.ShapeDtypeStruct((D_MODEL, W_COLS), MXU), jax.ShapeDtypeStruct((W_COLS, D_MODEL), MXU)],
        compiler_params=_params(("arbitrary",)),
    )(win8)


def _build_slabs(secs):
    tr = 512
    parts = ((0, 0, 1536), (1536, 1, 512), (2048, 6, 8), (2056, 2, 512), (2568, 3, 512), (3080, 4, 512),
             (3592, 5, 512))

    def body(*refs):
        o_ref = refs[len(secs)]
        for p in range(N_DEV):
            lo, hi = p * SHARD_COLS, (p + 1) * SHARD_COLS
            for ref_start, idx, width in parts:
                a, b = max(lo, ref_start), min(hi, ref_start + width)
                if a < b:
                    o_ref[p % 2, p // 2, :, a - lo:b - lo] = refs[idx][:, a - ref_start:b - ref_start]

    return pl.pallas_call(
        body, name="build_slabs", grid=(D_MODEL // tr,),
        in_specs=[pl.BlockSpec((tr, s.shape[1]), lambda i: (i, 0)) for s in secs],
        out_specs=pl.BlockSpec((2, 4, tr, SHARD_COLS), lambda i: (0, 0, i, 0)),
        out_shape=jax.ShapeDtypeStruct((2, 4, D_MODEL, SHARD_COLS), secs[0].dtype),
        compiler_params=_params(("arbitrary",)),
    )(*secs)


def _fill_bias(rb_ref, bk_ref, bias_ref, pair):
    for p in range(len(PATTERNS)):
        bk_p = bk_ref[p]
        for hh in range(2):
            head = 2 * pair + hh
            bm = jnp.full((BLK, 2 * BLK), NEG, F32)
            for b in range(N_BUCKETS):
                bm = jnp.where(bk_p == b, rb_ref[head, b], bm)
            bias_ref[p, hh * BLK:(hh + 1) * BLK, :] = bm


def _norm_and_gather(x, nw, shards, rb, bk):
    t = x.shape[0]
    tm = 512
    nsteps = t // tm
    n = len(shards)
    npair = ATT_HEADS // 2
    assert nsteps >= npair

    def body(x_ref, nw_ref, *rest):
        ins, (rb_ref, bk_ref, h_ref, ht_ref, bias_ref) = rest[:n], rest[n:n + 5]
        outs, sems = rest[n + 5:2 * n + 5], rest[2 * n + 5:]
        begin, finish = _gather_plan(ins, outs, *sems)
        step = pl.program_id(0)

        @pl.when(step == 0)
        def _():
            begin()

        @pl.when(step == nsteps - 1)
        def _():
            finish()

        @pl.when(step < npair)
        def _():
            _fill_bias(rb_ref, bk_ref, bias_ref, step)

        xv = x_ref[...]
        rstd = lax.rsqrt(jnp.mean(xv * xv, axis=-1, keepdims=True) + EPS)
        hf = xv * rstd * nw_ref[...]
        h_ref[...] = hf.astype(MXU)
        ht_ref[...] = hf.T.astype(MXU)

    res = pl.pallas_call(
        body, name="norm_and_gather", grid=(nsteps,),
        in_specs=[pl.BlockSpec((tm, D_MODEL), lambda i: (i, 0)), pl.BlockSpec(nw.shape, lambda i: (0, 0))]
                 + [ANY_SPEC] * n
                 + [pl.BlockSpec(memory_space=pltpu.SMEM), pl.BlockSpec(bk.shape, lambda i: (0, 0, 0))],
        out_specs=[pl.BlockSpec((tm, D_MODEL), lambda i: (i, 0)), pl.BlockSpec((D_MODEL, tm), lambda i: (0, i)),
                   pl.BlockSpec((len(PATTERNS), None, 2 * BLK, 2 * BLK), lambda i: (0, jnp.minimum(i, npair - 1), 0, 0))]
                  + [ANY_SPEC] * n,
        out_shape=[jax.ShapeDtypeStruct((t, D_MODEL), MXU), jax.ShapeDtypeStruct((D_MODEL, t), MXU),
                   jax.ShapeDtypeStruct((len(PATTERNS), npair, 2 * BLK, 2 * BLK), F32)]
                  + [jax.ShapeDtypeStruct((N_DEV,) + a.shape, a.dtype) for a in shards],
        scratch_shapes=GATHER_SEMS(n),
        compiler_params=_params(("arbitrary",)),
    )(x, nw, *shards, rb, bk)
    return res[0], res[1], res[2], res[3:]


def _inproj(h_all, w, cw, wout_shard):
    t = h_all.shape[0]
    tm = 512
    nsteps = t // tm

    def body(h_ref, w_ref, cw_ref, wo_ref, pdn_ref, qkv_ref, z_ref, patt_ref, gate_ref, ba_ref,
             wo8_ref, halo_ref, send, recv, loc):
        begin, finish = _gather_plan([wo_ref], [wo8_ref], send, recv, loc)

        @pl.when(pl.program_id(0) == 0)
        def _():
            begin()
            halo_ref[...] = jnp.zeros_like(halo_ref)

        @pl.when(pl.program_id(0) == nsteps - 1)
        def _():
            finish()

        h = h_ref[...]
        for ref, lo, hi in ((z_ref, 1536, 2048), (patt_ref, 2048, 3584), (gate_ref, 3584, 4096), (ba_ref, 4096, 4224)):
            ref[...] = jnp.dot(h, w_ref[:, lo:hi], preferred_element_type=F32)
        pdn = jnp.dot(h, w_ref[:, 0:3 * D_DN], preferred_element_type=F32)
        pdn_ref[...] = pdn
        _dn_prep_tile(pdn, halo_ref, cw_ref, qkv_ref)

    row = lambda n: pl.BlockSpec((tm, n), lambda i: (i, 0))
    full = lambda a: pl.BlockSpec(a.shape, lambda i: (0,) * a.ndim)
    return pl.pallas_call(
        body, name="inproj", grid=(nsteps,),
        in_specs=[row(D_MODEL), full(w), full(cw), ANY_SPEC],
        out_specs=[row(1536), row(1536), row(512), row(1536), row(512), row(128), ANY_SPEC],
        out_shape=[jax.ShapeDtypeStruct((t, n), F32) for n in (1536, 1536, 512, 1536, 512, 128)] +
                  [jax.ShapeDtypeStruct((N_DEV,) + wout_shard.shape, wout_shard.dtype)],
        scratch_shapes=[pltpu.VMEM((8, 3 * D_DN), F32)] + GATHER_SEMS(1),
        compiler_params=_params(("arbitrary",)),
    )(h_all, w, cw, wout_shard)


CONV_ROWS = 512


def _conv_taps(u_ref, c, w_ref):
    r0 = c * CONV_ROWS
    if c == 0:
        ext = jnp.concatenate([jnp.zeros((8, 128), F32), u_ref[0:CONV_ROWS, :]], axis=0)
    else:
        ext = u_ref[r0 - 8:r0 + CONV_ROWS, :]
    taps = [ext[8:, :]] + [pltpu.roll(ext, k, 0)[8:, :] for k in (1, 2, 3)]
    y = taps[0] * w_ref[3:4, :]
    for k in (1, 2, 3):
        y = y + taps[k] * w_ref[3 - k:4 - k, :]
    return taps, y


def _dn_prep_tile(pdn, halo_ref, cw_ref, out_ref):
    rows = pdn.shape[0]
    ext = jnp.concatenate([halo_ref[...], pdn], axis=0)
    halo_ref[...] = pdn[rows - 8:, :]
    for j in range(3 * D_DN // 128):
        cols = slice(j * 128, (j + 1) * 128)
        e = ext[:, cols]
        y = e[8:, :] * cw_ref[3:4, cols]
        for k in (1, 2, 3):
            y = y + pltpu.roll(e, k, 0)[8:, :] * cw_ref[3 - k:4 - k, cols]
        a = _silu(y)
        if j < 2 * DN_HEADS:
            a = a * lax.rsqrt(jnp.sum(a * a, axis=1, keepdims=True) + EPS)
        if j < DN_HEADS:
            a = a * DK ** -0.5
        out_ref[:, cols] = a


def _chunk_common(qkv, ba, arow, dtb):
    c = CHUNK
    ri, ci = _iota((c, c), 0), _iota((c, c), 1)
    lane = _iota((c, 128), 1)
    g_all = jnp.where((lane >= DN_HEADS) & (lane < 2 * DN_HEADS), arow * _softplus(ba + dtb), 0.0)
    gc_all = _hdot((ri >= ci).astype(F32), g_all, "a")
    gc_t = gc_all.T
    beta_all = _sigmoid(ba)
    out = []
    for h in range(DN_HEADS):
        gc = _lane_col(gc_all, DN_HEADS + h)
        gcr = gc_t[DN_HEADS + h:DN_HEADS + h + 1, :]
        gl = gc[c - 1:c, :]
        out.append(dict(
            q=qkv[:, h * DK:(h + 1) * DK], k=qkv[:, D_DN + h * DK:D_DN + (h + 1) * DK],
            v=qkv[:, 2 * D_DN + h * DK:2 * D_DN + (h + 1) * DK],
            beta=_lane_col(beta_all, h), g=_lane_col(g_all, DN_HEADS + h),
            a_raw=_lane_col(ba, DN_HEADS + h), a_h=_lane_col(arow, DN_HEADS + h), dt_h=_lane_col(dtb, DN_HEADS + h),
            decay=jnp.exp(jnp.where(ri >= ci, gc - gcr, NEG)), eg=jnp.exp(gc), egl=jnp.exp(gl), etail=jnp.exp(gl - gc)))
    return out, ri, ci


SCAN_CHUNKS = 8


def _dn_scan_fwd(qkv, ba, z, arow, dtb, dnw):
    t = qkv.shape[0]
    n = t // CHUNK
    c = CHUNK
    cps = SCAN_CHUNKS
    hs = range(DN_HEADS)
    chains = [(j, h) for j in range(cps) for h in hs]

    def body(qkv_ref, ba_ref, z_ref, arow_ref, dtb_ref, dnw_ref, o_ref, y_ref, sh_ref, th_ref, s_ref):
        @pl.when(pl.program_id(0) == 0)
        def _():
            s_ref[...] = jnp.zeros_like(s_ref)

        ms = {}
        for j in range(cps):
            rows = slice(j * c, (j + 1) * c)
            mj, ri, ci = _chunk_common(qkv_ref[rows, :], ba_ref[rows, :], arow_ref[...], dtb_ref[...])
            for h in hs:
                ms[j, h] = mj[h]
        kb = {x: ms[x]["k"] * ms[x]["beta"] for x in chains}
        amat = {x: jnp.where(ri > ci, _dot_nt(kb[x], ms[x]["k"]) * ms[x]["decay"], 0.0) for x in chains}
        attn = {x: jnp.where(ri >= ci, _dot_nt(ms[x]["q"], ms[x]["k"]) * ms[x]["decay"], 0.0) for x in chains}
        tinv = _wy_inverses(amat, (ri == ci).astype(F32))
        uw = {x: _hdot(tinv[x], jnp.concatenate([ms[x]["v"] * ms[x]["beta"], kb[x] * ms[x]["eg"]], axis=1))
              for x in chains}
        u = {x: uw[x][:, :DK] for x in chains}
        w = {x: uw[x][:, DK:] for x in chains}
        q_dec = {x: ms[x]["q"] * ms[x]["eg"] for x in chains}
        k_tail = {x: ms[x]["k"] * ms[x]["etail"] for x in chains}
        s = [s_ref[h] for h in hs]
        for j in range(cps):
            rows = slice(j * c, (j + 1) * c)
            v_new = [u[j, h] - _dot(w[j, h], s[h]) for h in hs]
            o = [_dot(q_dec[j, h], s[h]) + _dot(attn[j, h], v_new[h]) for h in hs]
            for h in hs:
                sh_ref[j, h] = s[h]
                th_ref[j, h] = tinv[j, h]
            s = [s[h] * ms[j, h]["egl"] + _dot_tn(k_tail[j, h], v_new[h]) for h in hs]
            for h in hs:
                cols = slice(h * DK, (h + 1) * DK)
                o_ref[rows, cols] = o[h]
                rs = lax.rsqrt(jnp.mean(o[h] * o[h], axis=1, keepdims=True) + EPS)
                y_ref[rows, cols] = o[h] * rs * dnw_ref[...] * _silu(z_ref[rows, cols])
        for h in hs:
            s_ref[h] = s[h]

    row = lambda w_: pl.BlockSpec((cps * c, w_), lambda i: (i, 0))
    one = pl.BlockSpec((1, 128), lambda i: (0, 0))
    return pl.pallas_call(
        body, name="dn_scan_fwd", grid=(n // cps,),
        in_specs=[row(1536), row(128), row(512), one, one, one],
        out_specs=[row(512), row(512), pl.BlockSpec((cps, DN_HEADS, DK, DK), lambda i: (i, 0, 0, 0)),
                   pl.BlockSpec((cps, DN_HEADS, c, c), lambda i: (i, 0, 0, 0))],
        out_shape=[jax.ShapeDtypeStruct((t, 512), F32), jax.ShapeDtypeStruct((t, 512), F32),
                   jax.ShapeDtypeStruct((n, DN_HEADS, DK, DK), F32), jax.ShapeDtypeStruct((n, DN_HEADS, c, c), F32)],
        scratch_shapes=[pltpu.VMEM((DN_HEADS, DK, DK), F32)],
        compiler_params=_params(("arbitrary",)),
    )(qkv, ba, z, arow, dtb, dnw)


ATT_ROWS = 512


def _pair_rstd(xv, g2_ref):
    return lax.rsqrt(_hdot(xv * xv, g2_ref[...], "b") * (1.0 / HD) + EPS)


def _pair_norm(t, raw_refs, w_refs, out_refs, g2_ref):
    for c in range(t // ATT_ROWS):
        sl = slice(c * ATT_ROWS, (c + 1) * ATT_ROWS)
        for raw, w_ref, out in zip(raw_refs, w_refs, out_refs):
            xv = raw[sl, :]
            out[sl, :] = xv * _pair_rstd(xv, g2_ref) * w_ref[...]


BIAS_SPEC = pl.BlockSpec((len(PATTERNS), None, 2 * BLK, 2 * BLK), lambda i: (0, i, 0, 0))
PAIR_ROW_SPEC = pl.BlockSpec((1, 128), lambda i: (0, i))


def _stack_heads(xb, h0):
    return jnp.concatenate([jnp.where(h0, xb, 0.0), jnp.where(h0, 0.0, xb)], axis=0).astype(MXU)


def _block_rows(t, r, n):
    per_class = (t // r) // BLK
    res = n // per_class
    j = n % per_class
    start = res + BLK * r * j
    pstart = res + BLK * r * jnp.maximum(j - 1, 0)
    if r == 1:
        return pl.ds(pl.multiple_of(start, BLK), BLK), pl.ds(pl.multiple_of(pstart, BLK), BLK), j
    return pl.ds(start, BLK, stride=r), pl.ds(pstart, BLK, stride=r), j


def _att_fwd(qkv, gate, bias, wq, wk, g2):
    t = qkv.shape[0]
    rows = ATT_ROWS

    def body(bias_ref, qraw_ref, kraw_ref, v_ref, g_ref, wq_ref, wk_ref, g2_ref, o_ref, y_ref, lse_ref,
             o0_ref, o1_ref, o2_ref, l0_ref, l1_ref, l2_ref, q_ref, k_ref):
        h0 = _iota((BLK, 128), 1) < HD
        prev_cols = _iota((2 * BLK, 2 * BLK), 1) < BLK
        op_refs, lp_refs = (o0_ref, o1_ref, o2_ref), (l0_ref, l1_ref, l2_ref)
        _pair_norm(t, (qraw_ref, kraw_ref), (wq_ref, wk_ref), (q_ref, k_ref), g2_ref)

        for p, (_, r) in enumerate(PATTERNS):
            def blk(n, carry, p=p, r=r):
                cur, prev, j = _block_rows(t, r, n)
                q2 = _stack_heads(q_ref[cur, :], h0)
                k2 = jnp.concatenate([k_ref[prev, :], k_ref[cur, :]], axis=0).astype(MXU)
                v2 = jnp.concatenate([v_ref[prev, :], v_ref[cur, :]], axis=0).astype(MXU)
                s = _dot_nt(q2, k2) + bias_ref[p] + jnp.where(prev_cols & (j == 0), NEG, 0.0)
                m = jnp.max(s, axis=1, keepdims=True)
                e = jnp.exp(s - m)
                l = jnp.sum(e, axis=1, keepdims=True)
                pv = _dot(e, v2) / l
                lse = m + jnp.log(l)
                op_refs[p][cur, :] = jnp.where(h0, pv[:BLK], pv[BLK:])
                lp_refs[p][cur, :] = jnp.where(h0, lse[:BLK], lse[BLK:])
                return carry

            lax.fori_loop(0, t // BLK, blk, 0, unroll=16)

        for c in range(t // rows):
            sl = slice(c * rows, (c + 1) * rows)
            ls = [ref[sl, :] for ref in lp_refs]
            mx = jnp.maximum(jnp.maximum(ls[0], ls[1]), ls[2])
            ws = [jnp.exp(v_ - mx) for v_ in ls]
            den = ws[0] + ws[1] + ws[2]
            o = (ws[0] * o0_ref[sl, :] + ws[1] * o1_ref[sl, :] + ws[2] * o2_ref[sl, :]) / den
            o_ref[sl, :] = o
            y_ref[sl, :] = o * _silu(g_ref[sl, :])
            lse_ref[sl, :] = mx + jnp.log(den)

    col = lambda off: pl.BlockSpec((t, 128), lambda i, off=off: (0, off + i))
    return pl.pallas_call(
        body, name="att_fwd", grid=(ATT_HEADS // 2,),
        in_specs=[BIAS_SPEC, col(0), col(4), col(8), col(0), PAIR_ROW_SPEC, PAIR_ROW_SPEC,
                  pl.BlockSpec((128, 128), lambda i: (0, 0))],
        out_specs=[col(0), col(0), col(0)],
        out_shape=[jax.ShapeDtypeStruct((t, 512), F32)] * 3,
        scratch_shapes=[pltpu.VMEM((t, 128), F32)] * 8,
        compiler_params=_params(("arbitrary",)),
    )(bias, qkv, qkv, qkv, gate, wq, wk, g2)


def _outproj_loss(x, ydn, yatt, wout, target):
    t = x.shape[0]
    tm = 512

    def body(x_ref, a_ref, b_ref, w_ref, t_ref, dy_ref, mix_ref, loss_ref):
        @pl.when(pl.program_id(0) == 0)
        def _():
            loss_ref[...] = jnp.zeros_like(loss_ref)

        mixf = jnp.concatenate([a_ref[...], b_ref[...]], axis=1)
        mix_ref[...] = mixf.T.astype(MXU)
        err = x_ref[...] + jnp.dot(mixf.astype(MXU), w_ref[...], preferred_element_type=F32) - t_ref[...]
        dy_ref[...] = err * (1.0 / D_MODEL)
        loss_ref[...] += jnp.sum(err * err) * (0.5 / D_MODEL)

    row = lambda n: pl.BlockSpec((tm, n), lambda i: (i, 0))
    return pl.pallas_call(
        body, name="outproj_loss", grid=(t // tm,),
        in_specs=[row(D_MODEL), row(512), row(512), pl.BlockSpec(wout.shape, lambda i: (0, 0)), row(D_MODEL)],
        out_specs=[row(D_MODEL), pl.BlockSpec((D_MODEL, tm), lambda i: (0, i)), pl.BlockSpec((8, 128), lambda i: (0, 0))],
        out_shape=[jax.ShapeDtypeStruct((t, D_MODEL), F32), jax.ShapeDtypeStruct((D_MODEL, t), MXU),
                   jax.ShapeDtypeStruct((8, 128), F32)],
        compiler_params=_params(("arbitrary",)),
    )(x, ydn, yatt, wout, target)


def _outproj_bwd(dy, wout_t, oraw, z, dnw, oatt, gate, g, gt):
    t = dy.shape[0]
    tm = 512

    def body(dy_ref, w_ref, o_ref, z_ref, dnw_ref, oa_ref, g_ref, grp_ref, grpt_ref,
             do_ref, dz_ref, doa_ref, dg_ref, dd_ref, ddnw_ref):
        @pl.when(pl.program_id(0) == 0)
        def _():
            ddnw_ref[...] = jnp.zeros_like(ddnw_ref)

        dmix = jnp.dot(dy_ref[...].astype(MXU), w_ref[...], preferred_element_type=F32)
        dnw_v = dnw_ref[...]
        acc = jnp.zeros((1, DK), F32)
        for h in range(DN_HEADS):
            sl = slice(h * DK, (h + 1) * DK)
            o, zz, dm = o_ref[:, sl], z_ref[:, sl], dmix[:, sl]
            rs = lax.rsqrt(jnp.mean(o * o, axis=1, keepdims=True) + EPS)
            oh = o * rs
            silu_z, dsilu_z = _silu_and_grad(zz)
            dz_ref[:, sl] = dm * oh * dnw_v * dsilu_z
            d_on = dm * silu_z
            gg = d_on * dnw_v
            do_ref[:, sl] = rs * (gg - oh * jnp.mean(gg * oh, axis=1, keepdims=True))
            acc = acc + jnp.sum(d_on * oh, axis=0, keepdims=True)
        ddnw_ref[...] += jnp.broadcast_to(acc, (8, DK))
        da, gate_v, oa = dmix[:, 512:], g_ref[...], oa_ref[...]
        silu_g, dsilu_g = _silu_and_grad(gate_v)
        doa = da * silu_g
        doa_ref[...] = doa
        dg_ref[...] = da * oa * dsilu_g
        dd_ref[...] = _hdot(_hdot(doa * oa, grp_ref[...], "b"), grpt_ref[...], "b")

    row = lambda n: pl.BlockSpec((tm, n), lambda i: (i, 0))
    full = lambda a: pl.BlockSpec(a.shape, lambda i: (0,) * a.ndim)
    return pl.pallas_call(
        body, name="outproj_bwd", grid=(t // tm,),
        in_specs=[row(D_MODEL), full(wout_t), row(512), row(512), full(dnw), row(512), row(512), full(g), full(gt)],
        out_specs=[row(512)] * 5 + [pl.BlockSpec((8, DK), lambda i: (0, 0))],
        out_shape=[jax.ShapeDtypeStruct((t, 512), F32)] * 5 + [jax.ShapeDtypeStruct((8, DK), F32)],
        compiler_params=_params(("arbitrary",)),
    )(dy, wout_t, oraw, z, dnw, oatt, gate, g, gt)


def _grad_matmul(at, b, name):
    m, t = at.shape
    n = b.shape[1]
    tk = 1024
    tn = n if n <= 1536 else 512
    nk = t // tk

    def body(a_ref, b_ref, o_ref, acc_ref):
        k = pl.program_id(1)

        @pl.when(k == 0)
        def _():
            acc_ref[...] = jnp.zeros_like(acc_ref)

        acc_ref[...] += jnp.dot(a_ref[...], b_ref[...].astype(MXU), preferred_element_type=F32)

        @pl.when(k == nk - 1)
        def _():
            o_ref[...] = acc_ref[...].astype(GRAD_WIRE)

    return pl.pallas_call(
        body, name=name, grid=(n // tn, nk),
        in_specs=[pl.BlockSpec((m, tk), lambda j, k: (0, k)), pl.BlockSpec((tk, tn), lambda j, k: (k, j))],
        out_specs=pl.BlockSpec((m, tn), lambda j, k: (0, j)),
        out_shape=jax.ShapeDtypeStruct((m, n), GRAD_WIRE),
        scratch_shapes=[pltpu.VMEM((m, tn), F32)],
        compiler_params=_params(("arbitrary", "arbitrary")),
    )(at, b)


def _att_bwd(qkv, do, lse, dd, bias, bk, wq, wk, g2):
    t = qkv.shape[0]
    rows = ATT_ROWS

    def body(bias_ref, bk_ref, qraw_ref, kraw_ref, v_ref, do_ref, lse_ref, dd_ref, wq_ref, wk_ref, g2_ref,
             dq_ref, dk_ref, dv_ref, db_ref, dwq_ref, dwk_ref, ds_ref, q_ref, k_ref):
        pair = pl.program_id(0)

        @pl.when(pair == 0)
        def _():
            db_ref[...] = jnp.zeros_like(db_ref)

        _pair_norm(t, (qraw_ref, kraw_ref), (wq_ref, wk_ref), (q_ref, k_ref), g2_ref)
        ds_ref[...] = jnp.zeros_like(ds_ref)
        for c in range(t // rows):
            sl = slice(c * rows, (c + 1) * rows)
            for ref in (dq_ref, dk_ref, dv_ref):
                ref[sl, :] = jnp.zeros((rows, 128), F32)
        h0 = _iota((BLK, 128), 1) < HD
        prev_cols = _iota((2 * BLK, 2 * BLK), 1) < BLK

        def rows_of(xb):
            return jnp.concatenate([xb[:, 0:1], xb[:, HD:HD + 1]], axis=0)

        for p, (_, r) in enumerate(PATTERNS):
            def blk(n, carry, p=p, r=r):
                cur, prev, j = _block_rows(t, r, n)
                q2, do2 = _stack_heads(q_ref[cur, :], h0), _stack_heads(do_ref[cur, :], h0)
                k2 = jnp.concatenate([k_ref[prev, :], k_ref[cur, :]], axis=0).astype(MXU)
                v2 = jnp.concatenate([v_ref[prev, :], v_ref[cur, :]], axis=0).astype(MXU)
                s = _dot_nt(q2, k2) + bias_ref[p] + jnp.where(prev_cols & (j == 0), NEG, 0.0)
                prob = jnp.exp(s - rows_of(lse_ref[cur, :]))
                ds = prob * (_dot_nt(do2, v2) - rows_of(dd_ref[cur, :]))
                ds_ref[p] += ds
                dq2 = _dot(ds, k2)
                dk2 = _dot_tn(ds, q2)
                dv2 = _dot_tn(prob, do2)
                dq_ref[cur, :] += jnp.where(h0, dq2[:BLK], dq2[BLK:])
                dk_ref[prev, :] += dk2[:BLK]
                dv_ref[prev, :] += dv2[:BLK]
                dk_ref[cur, :] += dk2[BLK:]
                dv_ref[cur, :] += dv2[BLK:]
                return carry

            lax.fori_loop(0, t // BLK, blk, 0, unroll=8)

        ri, ci = _iota((8, 128), 0), _iota((8, 128), 1)
        upd = jnp.zeros((8, 128), F32)
        for p in range(len(PATTERNS)):
            bk = bk_ref[p]
            for hh in range(2):
                dsum = ds_ref[p, hh * BLK:(hh + 1) * BLK, :]
                for b in range(N_BUCKETS):
                    val = jnp.sum(jnp.where(bk == b, dsum, 0.0))
                    upd = upd + jnp.where((ri == 2 * pair + hh) & (ci == b), val, 0.0)
        db_ref[...] += upd

        for raw, d_ref, w_ref, dw_ref in ((qraw_ref, dq_ref, wq_ref, dwq_ref), (kraw_ref, dk_ref, wk_ref, dwk_ref)):
            acc = jnp.zeros((1, 128), F32)
            for c in range(t // rows):
                sl = slice(c * rows, (c + 1) * rows)
                xv, dyv = raw[sl, :], d_ref[sl, :]
                rs = _pair_rstd(xv, g2_ref)
                xh = xv * rs
                gg = dyv * w_ref[...]
                mean = _hdot(gg * xh, g2_ref[...], "b") * (1.0 / HD)
                d_ref[sl, :] = rs * (gg - xh * mean)
                acc = acc + jnp.sum(dyv * xh, axis=0, keepdims=True)
            dw_ref[...] = jnp.broadcast_to(acc, (8, 128))

    col = lambda off: pl.BlockSpec((t, 128), lambda i, off=off: (0, off + i))
    acc8 = pl.BlockSpec((8, 128), lambda i: (0, i))
    return pl.pallas_call(
        body, name="att_bwd", grid=(ATT_HEADS // 2,),
        in_specs=[BIAS_SPEC, pl.BlockSpec(bk.shape, lambda i: (0, 0, 0)),
                  col(0), col(4), col(8), col(0), col(0), col(0), PAIR_ROW_SPEC, PAIR_ROW_SPEC,
                  pl.BlockSpec((128, 128), lambda i: (0, 0))],
        out_specs=[col(0), col(0), col(0), pl.BlockSpec((8, 128), lambda i: (0, 0)), acc8, acc8],
        out_shape=[jax.ShapeDtypeStruct((t, 512), F32)] * 3 + [jax.ShapeDtypeStruct((8, 128), F32)]
                  + [jax.ShapeDtypeStruct((8, 512), F32)] * 2,
        scratch_shapes=[pltpu.VMEM((len(PATTERNS), 2 * BLK, 2 * BLK), F32)] + [pltpu.VMEM((t, 128), F32)] * 2,
        compiler_params=_params(("arbitrary",)),
    )(bias, bk, qkv, qkv, qkv, do, lse, dd, wq, wk, g2)


def _dn_scan_bwd(qkv, ba, do, sh, th, arow, dtb):
    t = qkv.shape[0]
    n = t // CHUNK
    c = CHUNK
    cps = SCAN_CHUNKS

    def body(qkv_ref, ba_ref, do_ref, sh_ref, th_ref, arow_ref, dtb_ref, dqkv_ref, dba_ref, ds_ref):
        @pl.when(pl.program_id(0) == 0)
        def _():
            ds_ref[...] = jnp.zeros_like(ds_ref)

        hs = range(DN_HEADS)
        chains = [(j, h) for j in range(cps) for h in hs]
        lane = _iota((c, 128), 1)
        row = _iota((c, 1), 0)
        ms = {}
        for j in range(cps):
            rows_j = slice(j * c, (j + 1) * c)
            mj, ri, ci = _chunk_common(qkv_ref[rows_j, :], ba_ref[rows_j, :], arow_ref[...], dtb_ref[...])
            for h in hs:
                ms[j, h] = mj[h]
        q, k, v = ({x: ms[x][nm] for x in chains} for nm in ("q", "k", "v"))
        beta, decay = ({x: ms[x][nm] for x in chains} for nm in ("beta", "decay"))
        eg, egl, etail = ({x: ms[x][nm] for x in chains} for nm in ("eg", "egl", "etail"))
        s = {x: sh_ref[x[0], x[1]] for x in chains}
        tinv = {x: th_ref[x[0], x[1]] for x in chains}
        d_o = {(j, h): do_ref[j * c:(j + 1) * c, h * DK:(h + 1) * DK] for j, h in chains}
        kb = {x: k[x] * beta[x] for x in chains}
        vb = {x: v[x] * beta[x] for x in chains}
        kbg = {x: kb[x] * eg[x] for x in chains}
        amat = {x: jnp.where(ri > ci, _dot_nt(kb[x], k[x]) * decay[x], 0.0) for x in chains}
        attn = {x: jnp.where(ri >= ci, _dot_nt(q[x], k[x]) * decay[x], 0.0) for x in chains}
        uw = {x: _hdot(tinv[x], jnp.concatenate([vb[x], kbg[x]], axis=1)) for x in chains}
        u = {x: uw[x][:, :DK] for x in chains}
        w = {x: uw[x][:, DK:] for x in chains}
        v_new = {x: u[x] - _dot(w[x], s[x]) for x in chains}
        q_dec = {x: q[x] * eg[x] for x in chains}
        k_tail = {x: k[x] * etail[x] for x in chains}
        d_attn = {x: jnp.where(ri >= ci, _dot_nt(d_o[x], v_new[x]), 0.0) for x in chains}
        d_qdec = {x: _dot_nt(d_o[x], s[x]) for x in chains}
        from_o = {x: _dot_tn(attn[x], d_o[x]) for x in chains}
        to_state = {x: _dot_tn(q_dec[x], d_o[x]) for x in chains}

        d_s, d_vnew = {}, {}
        cur = [ds_ref[h] for h in hs]
        for j in reversed(range(cps)):
            for h in hs:
                d_s[j, h] = cur[h]
                d_vnew[j, h] = from_o[j, h] + _dot(k_tail[j, h], cur[h])
            cur = [to_state[j, h] + cur[h] * egl[j, h] - _dot_tn(w[j, h], d_vnew[j, h]) for h in hs]
        for h in hs:
            ds_ref[h] = cur[h]

        d_ktail = {x: _dot_nt(v_new[x], d_s[x]) for x in chains}
        d_gl = {x: jnp.sum(s[x] * d_s[x]) * egl[x] for x in chains}
        d_w = {x: -_dot_nt(d_vnew[x], s[x]) for x in chains}
        d_both = {x: _hdot_tn(tinv[x], jnp.concatenate([d_vnew[x], d_w[x]], axis=1)) for x in chains}
        d_vb = {x: d_both[x][:, :DK] for x in chains}
        d_kbg = {x: d_both[x][:, DK:] for x in chains}
        d_a = {x: -jnp.where(ri > ci, _hdot_nt(d_both[x], uw[x]), 0.0) for x in chains}
        d_qk = {x: d_attn[x] * decay[x] for x in chains}
        d_kk = {x: d_a[x] * decay[x] for x in chains}
        d_kb = {x: _dot(d_kk[x], k[x]) + d_kbg[x] * eg[x] for x in chains}
        d_q = {x: _dot(d_qk[x], k[x]) + d_qdec[x] * eg[x] for x in chains}
        d_k = {x: _dot_tn(d_qk[x], q[x]) + _dot_tn(d_kk[x], kb[x]) + d_ktail[x] * etail[x] + d_kb[x] * beta[x]
               for x in chains}
        d_beta = {x: jnp.sum(d_kb[x] * k[x] + d_vb[x] * v[x], axis=1, keepdims=True) for x in chains}
        mm = {x: d_a[x] * amat[x] + d_attn[x] * attn[x] for x in chains}
        for j in range(cps):
            rows_j = slice(j * c, (j + 1) * c)
            rows = jnp.zeros((c, c), F32)
            for h in hs:
                rows = rows + jnp.where(ri == h, jnp.sum(mm[j, h], axis=0, keepdims=True), 0.0)
            cols_t = jnp.concatenate([rows, jnp.zeros((c, c), F32)], axis=1).T[:c, :]
            d_gc_all = jnp.zeros((c, 128), F32)
            for h in hs:
                x = (j, h)
                tail_term = jnp.sum(d_ktail[x] * k_tail[x], axis=1, keepdims=True)
                d_gc = (jnp.sum(mm[x], axis=1, keepdims=True) - _lane_col(cols_t, h)
                        + jnp.sum(d_qdec[x] * q_dec[x] + d_kbg[x] * kbg[x], axis=1, keepdims=True) - tail_term)
                d_gc = d_gc + jnp.where(row == c - 1, jnp.sum(tail_term) + d_gl[x], 0.0)
                d_gc_all = d_gc_all + jnp.where(lane == DN_HEADS + h, d_gc, 0.0)
            d_g_all = _hdot((ri <= ci).astype(F32), d_gc_all, "a")
            dba = jnp.zeros((c, 128), F32)
            for h in hs:
                x = (j, h)
                d_g = _lane_col(d_g_all, DN_HEADS + h)
                d_braw = d_beta[x] * beta[x] * (1.0 - beta[x])
                d_araw = d_g * ms[x]["a_h"] * _sigmoid(ms[x]["a_raw"] + ms[x]["dt_h"])
                dba = dba + jnp.where(lane == h, d_braw, 0.0) + jnp.where(lane == DN_HEADS + h, d_araw, 0.0) \
                    + jnp.where(lane == 2 * DN_HEADS + h, d_g * ms[x]["g"], 0.0)
                dqkv_ref[rows_j, h * DK:(h + 1) * DK] = d_q[x]
                dqkv_ref[rows_j, D_DN + h * DK:D_DN + (h + 1) * DK] = d_k[x]
                dqkv_ref[rows_j, 2 * D_DN + h * DK:2 * D_DN + (h + 1) * DK] = d_vb[x] * beta[x]
            dba_ref[rows_j, :] = dba

    nsteps = n // cps
    rev = lambda w_: pl.BlockSpec((cps * c, w_), lambda i: (nsteps - 1 - i, 0))
    one = pl.BlockSpec((1, 128), lambda i: (0, 0))
    return pl.pallas_call(
        body, name="dn_scan_bwd", grid=(nsteps,),
        in_specs=[rev(1536), rev(128), rev(512),
                  pl.BlockSpec((cps, DN_HEADS, DK, DK), lambda i: (nsteps - 1 - i, 0, 0, 0)),
                  pl.BlockSpec((cps, DN_HEADS, c, c), lambda i: (nsteps - 1 - i, 0, 0, 0)), one, one],
        out_specs=[rev(1536), rev(128)],
        out_shape=[jax.ShapeDtypeStruct((t, 1536), F32), jax.ShapeDtypeStruct((t, 128), F32)],
        scratch_shapes=[pltpu.VMEM((DN_HEADS, DK, DK), F32)],
        compiler_params=_params(("arbitrary",)),
    )(qkv, ba, do, sh, th, arow, dtb)


def _dn_prep_bwd(pdn, cw, dact):
    t = pdn.shape[0]
    nchunk = t // CONV_ROWS

    def body(u_ref, w_ref, d_ref, du_ref, dw_ref, dy_ref):
        j = pl.program_id(0)
        dy_ref[t:t + 8, :] = jnp.zeros((8, 128), F32)
        dw = [jnp.zeros((1, 128), F32) for _ in range(4)]
        for c in range(nchunk):
            sl = slice(c * CONV_ROWS, (c + 1) * CONV_ROWS)
            taps, y = _conv_taps(u_ref, c, w_ref)
            a, da_dy = _silu_and_grad(y)
            dout = d_ref[sl, :]
            rs = lax.rsqrt(jnp.sum(a * a, axis=1, keepdims=True) + EPS)
            f = jnp.where(j < 8, rs, 1.0) * jnp.where(j < 4, DK ** -0.5, 1.0)
            corr = jnp.where(j < 8, f * rs * rs * jnp.sum(dout * a, axis=1, keepdims=True), 0.0)
            dy = (f * dout - corr * a) * da_dy
            dy_ref[sl, :] = dy
            for k_ in range(4):
                dw[3 - k_] = dw[3 - k_] + jnp.sum(taps[k_] * dy, axis=0, keepdims=True)
        for i in range(4):
            dw_ref[i:i + 1, :] = dw[i]
        for c in range(nchunk):
            r0 = c * CONV_ROWS
            ext = dy_ref[r0:r0 + CONV_ROWS + 8, :]
            du = ext[:CONV_ROWS, :] * w_ref[3:4, :]
            for k_ in (1, 2, 3):
                du = du + pltpu.roll(ext, CONV_ROWS + 8 - k_, 0)[:CONV_ROWS, :] * w_ref[3 - k_:4 - k_, :]
            du_ref[r0:r0 + CONV_ROWS, :] = du

    return pl.pallas_call(
        body, name="dn_prep_bwd", grid=(12,),
        in_specs=[pl.BlockSpec((t, 128), lambda j: (0, j)), pl.BlockSpec((4, 128), lambda j: (0, j)),
                  pl.BlockSpec((t, 128), lambda j: (0, j))],
        out_specs=[pl.BlockSpec((t, 128), lambda j: (0, j)), pl.BlockSpec((4, 128), lambda j: (0, j))],
        out_shape=[jax.ShapeDtypeStruct((t, 1536), F32), jax.ShapeDtypeStruct((4, 1536), F32)],
        scratch_shapes=[pltpu.VMEM((t + 8, 128), F32)],
        compiler_params=_params(("arbitrary",)),
    )(pdn, cw, dact)


SECTIONS = (("dn", 0, 1536), ("z", 1536, 512), ("q", 2048, 512), ("k", 2560, 512), ("v", 3072, 512),
            ("gate", 3584, 512), ("ba", 4096, 128))


def _inproj_bwd(x, nw, wt, dy, dsecs, partials):
    t = x.shape[0]
    tm = 256
    npart = len(partials)
    nsteps = t // tm

    nsec = len(SECTIONS)

    def body(x_ref, nw_ref, w_ref, dy_ref, *rest):
        sec_refs, rest = rest[:nsec], rest[nsec:]
        part_refs, (gx_ref, dnw_ref, cs_ref) = rest[:npart], rest[npart:npart + 3]
        got_refs, (send, recv, loc) = rest[npart + 3:2 * npart + 3], rest[2 * npart + 3:]
        starts, waits = _chip_swap_copies(part_refs, got_refs, send, recv, loc)

        @pl.when(pl.program_id(0) == 0)
        def _():
            for start in starts:
                start()
            dnw_ref[...] = jnp.zeros_like(dnw_ref)
            cs_ref[...] = jnp.zeros_like(cs_ref)

        @pl.when(pl.program_id(0) == nsteps - 1)
        def _():
            for wait in waits:
                wait()

        dh = jnp.zeros((tm, D_MODEL), F32)
        for ref, (_, lo, width) in zip(sec_refs, SECTIONS):
            dh = dh + jnp.dot(ref[...].astype(MXU), w_ref[lo:lo + width, :], preferred_element_type=F32)
        xv = x_ref[...]
        rstd = lax.rsqrt(jnp.mean(xv * xv, axis=-1, keepdims=True) + EPS)
        xh = xv * rstd
        gg = dh * nw_ref[...]
        gx_ref[...] = rstd * (gg - xh * jnp.mean(gg * xh, axis=-1, keepdims=True)) + dy_ref[...]
        dnw_ref[...] += jnp.broadcast_to(jnp.sum(dh * xh, axis=0, keepdims=True), (8, D_MODEL))
        cs_ref[...] += jnp.broadcast_to(jnp.sum(sec_refs[nsec - 1][...], axis=0, keepdims=True), (8, 128))

    row = lambda n: pl.BlockSpec((tm, n), lambda i: (i, 0))
    full = lambda a: pl.BlockSpec(a.shape, lambda i: (0,) * a.ndim)
    res = pl.pallas_call(
        body, name="inproj_bwd", grid=(nsteps,),
        in_specs=[row(D_MODEL), full(nw), full(wt), row(D_MODEL)] + [row(width) for _, _, width in SECTIONS]
                 + [ANY_SPEC] * npart,
        out_specs=[row(D_MODEL), pl.BlockSpec((8, D_MODEL), lambda i: (0, 0)), pl.BlockSpec((8, 128), lambda i: (0, 0))]
                  + [ANY_SPEC] * npart,
        out_shape=[jax.ShapeDtypeStruct((t, D_MODEL), F32), jax.ShapeDtypeStruct((8, D_MODEL), F32),
                   jax.ShapeDtypeStruct((8, 128), F32)] + [jax.ShapeDtypeStruct(p.shape, p.dtype) for p in partials],
        scratch_shapes=[pltpu.SemaphoreType.DMA((npart, 3)), pltpu.SemaphoreType.DMA((npart, 3)),
                        pltpu.SemaphoreType.DMA((npart,))],
        compiler_params=_params(("arbitrary",)),
    )(x, nw, wt, dy, *dsecs, *partials)
    return res[0], res[1], res[2], res[3:]


def _adamw_sum(w, gs, m, v, name):
    r, c = w.shape
    nsum = gs.shape[0]
    tr = r if r <= 512 else 512
    c1 = 1.0 - ADAM_B1 ** ADAM_STEP
    c2 = 1.0 - ADAM_B2 ** ADAM_STEP

    def body(w_ref, g_ref, m_ref, v_ref, go_ref, d_ref, mo_ref, vo_ref):
        g = g_ref[0].astype(F32)
        for s in range(1, nsum):
            g = g + g_ref[s].astype(F32)
        mn = ADAM_B1 * m_ref[...] + (1.0 - ADAM_B1) * g
        vn = ADAM_B2 * v_ref[...] + (1.0 - ADAM_B2) * (g * g)
        go_ref[...] = g
        mo_ref[...] = mn
        vo_ref[...] = vn
        d_ref[...] = -ADAM_LR * ((mn / c1) / (jnp.sqrt(vn / c2) + ADAM_EPS) + ADAM_WD * w_ref[...])

    blk = pl.BlockSpec((tr, c), lambda i: (i, 0))
    return pl.pallas_call(
        body, name=name, grid=(r // tr,),
        in_specs=[blk, pl.BlockSpec((nsum, tr, c), lambda i: (0, i, 0)), blk, blk],
        out_specs=[blk] * 4, out_shape=[jax.ShapeDtypeStruct((r, c), F32)] * 4,
        compiler_params=_params(("arbitrary",)),
    )(w, gs, m, v)


def _local_step(x, target, h, ht, bias, bk, w_sect, conv_w, a_log, dt_bias, dn_norm_w, q_norm_w, k_norm_w, wout_shard):
    arow = jnp.zeros((1, 128), F32).at[0, DN_HEADS:2 * DN_HEADS].set(-jnp.exp(a_log[0]))
    dtb = jnp.zeros((1, 128), F32).at[0, DN_HEADS:2 * DN_HEADS].set(dt_bias[0])
    g_np, gt_np = _group_mats()
    g, gt = jnp.asarray(g_np), jnp.asarray(gt_np)
    g2 = jnp.asarray(np.kron(np.eye(2, dtype=np.float32), np.ones((HD, HD), np.float32)))
    wq = jnp.tile(q_norm_w, (1, ATT_HEADS)) * (HD ** -0.5)
    wk = jnp.tile(k_norm_w, (1, ATT_HEADS))

    pdn, qkv_dn, z, patt, gate, ba, wout8 = _inproj(h, w_sect, conv_w, wout_shard)
    w_out = wout8.reshape(D_MODEL, D_MODEL)
    oraw, ydn, sh, th = _dn_scan_fwd(qkv_dn, ba, z, arow, dtb, dn_norm_w)
    oatt, yatt, lse = _att_fwd(patt, gate, bias, wq, wk, g2)
    dy, mix_t, loss8 = _outproj_loss(x, ydn, yatt, w_out, target)

    do_dn, dz, do_att, dgate, dd, ddnw = _outproj_bwd(dy, w_out.T, oraw, z, dn_norm_w, oatt, gate, g, gt)
    d_wout = _grad_matmul(mix_t, dy, "dw_out")
    dq, dk, dv, drb, dwq8, dwk8 = _att_bwd(patt, do_att, lse, dd, bias, bk, wq, wk, g2)
    dqkv_dn, dba = _dn_scan_bwd(qkv_dn, ba, do_dn, sh, th, arow, dtb)
    dpdn, d_conv = _dn_prep_bwd(pdn, conv_w, dqkv_dn)
    dsecs = (dpdn, dz, dq, dk, dv, dgate, dba)
    dw_sections = [_grad_matmul(ht, d_, "dw_in_" + nm) for d_, (nm, _, _) in zip(dsecs, SECTIONS)]
    return dict(w_in_sections=dw_sections, conv_w=d_conv, w_out=d_wout, dy=dy, dsecs=dsecs,
                small_parts=(loss8, ddnw, dwq8, dwk8, drb))


def _finish_step(x, norm_w, w_sect_t, gr, partials):
    grad_x, dnw8, cs8, got = _inproj_bwd(x, norm_w, w_sect_t, gr["dy"], gr["dsecs"], partials)
    return grad_x, _pack_small_grads(dnw8, cs8, *gr["small_parts"]), got


SMALL_ROWS = 24
SMALL_AT = dict(a_log=(slice(8, 9), slice(0, 4)), dt_bias=(slice(9, 10), slice(0, 4)),
                dn_norm_w=(slice(10, 11), slice(0, 128)), q_norm_w=(slice(11, 12), slice(0, HD)),
                k_norm_w=(slice(12, 13), slice(0, HD)), rel_bias=(slice(16, 24), slice(0, N_BUCKETS)))
SMALL_NAMES = ("norm_w", "a_log", "dt_bias", "dn_norm_w", "q_norm_w", "k_norm_w", "rel_bias")


LOSS_ROW = 13


def _pack_small_grads(dnw8, cs8, loss8, ddnw8, dwq8, dwk8, drb):
    def body(dnw_ref, cs_ref, loss_ref, ddnw_ref, dwq_ref, dwk_ref, drb_ref, o_ref):
        lane = _iota((8, 128), 1)
        o_ref[...] = jnp.zeros_like(o_ref)
        o_ref[LOSS_ROW:LOSS_ROW + 1, :] = jnp.where(lane == 0, loss_ref[...], 0.0)[0:1, :]
        for k in range(D_MODEL // 128):
            o_ref[k:k + 1, :] = dnw_ref[0:1, k * 128:(k + 1) * 128]
        cs = cs_ref[...]
        o_ref[8:9, :] = jnp.where(lane < DN_HEADS, pltpu.roll(cs, 128 - 2 * DN_HEADS, 1), 0.0)[0:1, :]
        o_ref[9:10, :] = jnp.where(lane < DN_HEADS, pltpu.roll(cs, 128 - DN_HEADS, 1), 0.0)[0:1, :]
        o_ref[10:11, :] = ddnw_ref[0:1, :]
        for row, ref, scale in ((11, dwq_ref, HD ** -0.5), (12, dwk_ref, 1.0)):
            acc = ref[:, 0:128] + ref[:, 128:256] + ref[:, 256:384] + ref[:, 384:512]
            acc = (acc + pltpu.roll(acc, HD, 1)) * scale
            o_ref[row:row + 1, :] = jnp.where(lane < HD, acc, 0.0)[0:1, :]
        o_ref[16:24, :] = drb_ref[...]

    return pl.pallas_call(body, name="pack_small_grads", out_shape=jax.ShapeDtypeStruct((SMALL_ROWS, 128), F32),
                          )(dnw8, cs8, loss8, ddnw8, dwq8, dwk8, drb)


def _adam_math(w, g, m, v):
    c1 = 1.0 - ADAM_B1 ** ADAM_STEP
    c2 = 1.0 - ADAM_B2 ** ADAM_STEP
    mn = ADAM_B1 * m + (1.0 - ADAM_B1) * g
    vn = ADAM_B2 * v + (1.0 - ADAM_B2) * (g * g)
    return -ADAM_LR * ((mn / c1) / (jnp.sqrt(vn / c2) + ADAM_EPS) + ADAM_WD * w), mn, vn


def _adamw_small(gs, ws, ms, vs):
    n = len(SMALL_NAMES)

    def body(g_ref, *refs):
        w_refs, m_refs, v_refs = refs[:n], refs[n:2 * n], refs[2 * n:3 * n]
        outs, loss_ref = refs[3 * n:7 * n], refs[7 * n]
        loss = g_ref[0, LOSS_ROW:LOSS_ROW + 1, :]
        for s in range(1, gs.shape[0]):
            loss = loss + g_ref[s, LOSS_ROW:LOSS_ROW + 1, :]
        loss_ref[...] = loss

        def one(i, rows, lanes, at):
            g = g_ref[0, rows, lanes]
            for s in range(1, gs.shape[0]):
                g = g + g_ref[s, rows, lanes]
            d, mn, vn = _adam_math(w_refs[i][at], g, m_refs[i][at], v_refs[i][at])
            for kind, val in enumerate((g, d, mn, vn)):
                outs[kind * n + i][at] = val

        for k in range(D_MODEL // 128):
            one(0, slice(k, k + 1), slice(0, 128), (slice(0, 1), slice(k * 128, (k + 1) * 128)))
        for i, nm in enumerate(SMALL_NAMES[1:], start=1):
            rows, lanes = SMALL_AT[nm]
            one(i, rows, lanes, (slice(None), slice(None)))

    shapes = [jax.ShapeDtypeStruct(w.shape, F32) for w in ws]
    res = pl.pallas_call(body, name="adamw_small",
                         out_shape=shapes * 4 + [jax.ShapeDtypeStruct((1, 128), F32)])(gs, *ws, *ms, *vs)
    return [res[k * n:(k + 1) * n] for k in range(4)], res[4 * n]


def kernel(x, norm_w, w_in, conv_w, a_log, dt_bias, dn_norm_w, q_norm_w, k_norm_w, rel_bias, w_out, loss_target, m_norm_w, m_w_in, m_conv_w, m_a_log, m_dt_bias, m_dn_norm_w, m_q_norm_w, m_k_norm_w, m_rel_bias, m_w_out, v_norm_w, v_w_in, v_conv_w, v_a_log, v_dt_bias, v_dn_norm_w, v_q_norm_w, v_k_norm_w, v_rel_bias, v_w_out):
    assert w_in.shape[2] == SHARD_COLS
    bk = jnp.asarray(_bucket_tables())
    h, ht, bias, (win8, conv8) = _norm_and_gather(x[0], norm_w, [w_in[0].astype(MXU), conv_w[0]], rel_bias, bk)
    w_sect, w_sect_t = _build_w(win8)
    conv_full = conv8.transpose(1, 0, 2).reshape(4, 3 * D_DN)

    gr = _local_step(x[0], loss_target[0], h, ht, bias, bk, w_sect, conv_full, a_log, dt_bias, dn_norm_w, q_norm_w,
                     k_norm_w, w_out[0].astype(MXU))

    slabs = [_build_slabs(gr["w_in_sections"]),
             gr["w_out"].reshape(4, 2, D_MODEL // N_DEV, D_MODEL).transpose(1, 0, 2, 3),
             gr["conv_w"].reshape(4, 4, 2, 3 * D_DN // N_DEV).transpose(2, 1, 0, 3)]
    core = lax.axis_index("c").astype(jnp.int32).reshape(1)
    from_sibling = _swap_siblings(slabs)
    wires = (GRAD_WIRE, GRAD_WIRE, F32)
    partial = [_chip_sum(slabs[i], from_sibling[i], core, wires[i], "chip_sum_%d" % i) for i in range(3)]
    grad_x, small_pack, (r_win, r_wout, r_conv) = _finish_step(x[0], norm_w, w_sect_t, gr, partial)
    r_small = _share_small(small_pack)

    g_win, d_win, m_win, v_win = _adamw_sum(w_in[0], r_win, m_w_in[0], v_w_in[0], "adamw_w_in")
    g_wout, d_wout, m_wout, v_wout = _adamw_sum(w_out[0], r_wout, m_w_out[0], v_w_out[0], "adamw_w_out")
    g_conv, d_conv, m_conv, v_conv = _adamw_sum(conv_w[0], r_conv, m_conv_w[0], v_conv_w[0], "adamw_conv_w")
    small, loss_row = _adamw_small(r_small,
                                   (norm_w, a_log, dt_bias, dn_norm_w, q_norm_w, k_norm_w, rel_bias),
                                   (m_norm_w, m_a_log, m_dt_bias, m_dn_norm_w, m_q_norm_w, m_k_norm_w, m_rel_bias),
                                   (v_norm_w, v_a_log, v_dt_bias, v_dn_norm_w, v_q_norm_w, v_k_norm_w, v_rel_bias))

    loss = loss_row[0, 0]
    names = ("norm_w", "w_in", "conv_w", "a_log", "dt_bias", "dn_norm_w", "q_norm_w", "k_norm_w", "rel_bias", "w_out")
    big = dict(w_in=(g_win, d_win, m_win, v_win), conv_w=(g_conv, d_conv, m_conv, v_conv),
               w_out=(g_wout, d_wout, m_wout, v_wout))
    outs = [loss, grad_x[None]]
    for kind in range(4):
        for nm in names:
            outs.append(big[nm][kind][None] if nm in big else small[kind][SMALL_NAMES.index(nm)])
    return tuple(outs)
```

```python
import math

import numpy as np
import jax
import jax.numpy as jnp
from jax import lax
from jax.experimental import pallas as pl
from jax.experimental.pallas import tpu as pltpu

F32 = jnp.float32
MXU = jnp.bfloat16
GRAD_WIRE = jnp.bfloat16

D_MODEL = 1024
D_DN = 512
DN_HEADS = 4
DK = 128
CHUNK = 64
D_ATT = 512
ATT_HEADS = 8
HD = 64
PATTERNS = ((128, 1), (512, 4), (2048, 16))
BLK = 128
N_BUCKETS = 32
MAX_DISTANCE = 2048
EPS = 1e-6
W_COLS = 4224
N_DEV = 8
AXES = ("x", "y", "c")

ADAM_LR = 0.001
ADAM_B1 = 0.9
ADAM_B2 = 0.999
ADAM_EPS = 1e-08
ADAM_WD = 0.01
ADAM_STEP = 10

VMEM_LIMIT = 56 * 1024 * 1024
NEG = -1e30


def _dot(a, b):
    return jnp.dot(a.astype(MXU), b.astype(MXU), preferred_element_type=F32)


def _dot_nt(a, b):
    return lax.dot_general(a.astype(MXU), b.astype(MXU), (((1,), (1,)), ((), ())), preferred_element_type=F32)


def _dot_tn(a, b):
    return lax.dot_general(a.astype(MXU), b.astype(MXU), (((0,), (0,)), ((), ())), preferred_element_type=F32)


def _split(a):
    hi = a.astype(jnp.bfloat16)
    return hi, (a - hi.astype(F32)).astype(jnp.bfloat16)


def _dot_split(a, b, dims, exact):
    dg = lambda u, v: lax.dot_general(u, v, (dims, ((), ())), preferred_element_type=F32)
    if exact == "b":
        ah, al = _split(a)
        bh = b.astype(jnp.bfloat16)
        return dg(ah, bh) + dg(al, bh)
    if exact == "a":
        bh, bm = _split(b)
        bl = (b - bh.astype(F32) - bm.astype(F32)).astype(jnp.bfloat16)
        ah = a.astype(jnp.bfloat16)
        return dg(ah, bh) + (dg(ah, bm) + dg(ah, bl))
    ah, al = _split(a)
    bh, bl = _split(b)
    return dg(ah, bh) + (dg(ah, bl) + dg(al, bh))


def _wy_inverses(amat, eye):
    tinv = {x: eye - amat[x] for x in amat}
    pw = amat
    for _ in range(5):
        pw = {x: _hdot(pw[x], pw[x]) for x in amat}
        tinv = {x: tinv[x] + _hdot(tinv[x], pw[x]) for x in amat}
    return tinv


def _hdot(a, b, exact=None):
    return _dot_split(a, b, ((1,), (0,)), exact)


def _hdot_nt(a, b, exact=None):
    return _dot_split(a, b, ((1,), (1,)), exact)


def _hdot_tn(a, b, exact=None):
    return _dot_split(a, b, ((0,), (0,)), exact)


def _sigmoid(x):
    return 1.0 / (1.0 + jnp.exp(-x))


def _silu(x):
    return x * _sigmoid(x)


def _silu_and_grad(x):
    s = _sigmoid(x)
    return x * s, s * (1.0 + x * (1.0 - s))


def _softplus(x):
    return jnp.maximum(x, 0.0) + jnp.log(1.0 + jnp.exp(-jnp.abs(x)))


def _iota(shape, dim):
    return lax.broadcasted_iota(jnp.int32, shape, dim)


def _lane_col(x, k):
    return jnp.sum(jnp.where(_iota(x.shape, 1) == k, x, 0.0), axis=1, keepdims=True)


def _params(sem=None):
    return pltpu.CompilerParams(dimension_semantics=sem, vmem_limit_bytes=VMEM_LIMIT)


def _t5_bucket(dist):
    max_exact = N_BUCKETS // 2
    d = np.maximum(dist, 1).astype(np.float64)
    large = max_exact + (np.log(d / max_exact) / math.log(MAX_DISTANCE / max_exact)
                         * (N_BUCKETS - max_exact)).astype(np.int32)
    large = np.minimum(large, N_BUCKETS - 1)
    return np.where(dist < max_exact, dist, large).astype(np.int32)


def _bucket_tables():
    qi = np.arange(BLK)[:, None]
    kj = np.arange(2 * BLK)[None, :]
    step = qi - kj + BLK
    band = (step >= 0) & (step <= BLK)
    out = []
    for _, r in PATTERNS:
        b = _t5_bucket(np.clip(step, 0, None) * r)
        out.append(np.where(band, b, -1))
    return np.stack(out).astype(np.int32)


def _group_mats():
    g = np.zeros((D_ATT, 128), np.float32)
    for h in range(ATT_HEADS):
        g[h * HD:(h + 1) * HD, h] = 1.0
    return g, np.ascontiguousarray(g.T)


CHIP_FLIPS = ((1, 0), (0, 1), (1, 1))
ANY_SPEC = pl.BlockSpec(memory_space=pl.ANY)
MESH_ID = pl.DeviceIdType.MESH


def _other_chips():
    x, y = lax.axis_index("x"), lax.axis_index("y")
    return [((1 - x if fx else x), (1 - y if fy else y)) for fx, fy in CHIP_FLIPS]


def _gather_plan(ins, outs, send, recv, loc):
    n = len(ins)
    x, y, c = (lax.axis_index(a) for a in AXES)
    sib = (x, y, 1 - c)
    chips = _other_chips()
    lin = lambda px, py, pc: 4 * px + 2 * py + pc

    def copy(a, k, block, to, src=None):
        slot = outs[a].at[lin(*block)]
        return pltpu.make_async_remote_copy(src_ref=slot if src is None else src, dst_ref=slot,
                                            send_sem=send.at[a, k], recv_sem=recv.at[a, k],
                                            device_id=to, device_id_type=MESH_ID)

    mine = [pltpu.make_async_copy(ins[a], outs[a].at[lin(x, y, c)], loc.at[a]) for a in range(n)]
    firsts = []
    for a in range(n):
        firsts.append(copy(a, 0, (x, y, c), sib, src=ins[a]))
        firsts += [copy(a, 1 + j, (x, y, c), (*chip, c), src=ins[a]) for j, chip in enumerate(chips)]

    def begin():
        for cp in mine + firsts:
            cp.start()

    def finish():
        passed = []
        for j, chip in enumerate(chips):
            for a in range(n):
                copy(a, 1 + j, (*chip, c), (x, y, c)).wait_recv()
                fw = copy(a, 4 + j, (*chip, c), sib)
                fw.start()
                passed.append(fw)
        for a in range(n):
            copy(a, 0, sib, (x, y, c)).wait_recv()
            for j, chip in enumerate(chips):
                copy(a, 4 + j, (*chip, 1 - c), (x, y, c)).wait_recv()
        for cp in firsts + passed:
            cp.wait_send()
        for cp in mine:
            cp.wait()

    return begin, finish


GATHER_SEMS = lambda n: [pltpu.SemaphoreType.DMA((n, 7)), pltpu.SemaphoreType.DMA((n, 7)), pltpu.SemaphoreType.DMA((n,))]


def _swap_siblings(arrs):
    n = len(arrs)

    def body(*refs):
        ins, outs = refs[:n], refs[n:2 * n]
        send, recv = refs[2 * n:]
        x, y, c = (lax.axis_index(a) for a in AXES)
        cps = [pltpu.make_async_remote_copy(src_ref=ins[a].at[1 - c], dst_ref=outs[a], send_sem=send.at[a],
                                            recv_sem=recv.at[a], device_id=(x, y, 1 - c), device_id_type=MESH_ID)
               for a in range(n)]
        for cp in cps:
            cp.start()
        for cp in cps:
            cp.wait()

    return pl.pallas_call(
        body, name="swap_siblings", out_shape=[jax.ShapeDtypeStruct(a.shape[1:], a.dtype) for a in arrs],
        in_specs=[ANY_SPEC] * n, out_specs=[ANY_SPEC] * n,
        scratch_shapes=[pltpu.SemaphoreType.DMA((n,)), pltpu.SemaphoreType.DMA((n,))],
    )(*arrs)


def _chip_sum(mine2, theirs, core, wire, name):
    _, nchip, r, cdim = mine2.shape
    tr = r if r <= 1024 else 1024

    def body(core_ref, a_ref, b_ref, o_ref):
        del core_ref
        o_ref[...] = (a_ref[...].astype(F32) + b_ref[...].astype(F32)).astype(wire)

    grid_spec = pltpu.PrefetchScalarGridSpec(
        num_scalar_prefetch=1, grid=(nchip, r // tr),
        in_specs=[pl.BlockSpec((None, None, tr, cdim), lambda j, i, cr: (cr[0], j, i, 0)),
                  pl.BlockSpec((None, tr, cdim), lambda j, i, cr: (j, i, 0))],
        out_specs=pl.BlockSpec((None, tr, cdim), lambda j, i, cr: (j, i, 0)))
    return pl.pallas_call(
        body, name=name, grid_spec=grid_spec, out_shape=jax.ShapeDtypeStruct((nchip, r, cdim), wire),
        compiler_params=_params(("arbitrary", "arbitrary")),
    )(core, mine2, theirs)


def _chip_swap_copies(ins, outs, send, recv, loc):
    x, y, c = (lax.axis_index(a) for a in AXES)
    me = 2 * x + y
    starts, arrivals, drains = [], [], []
    for a in range(len(ins)):
        lc = pltpu.make_async_copy(ins[a].at[me], outs[a].at[me], loc.at[a])
        starts.append(lc.start)
        drains.append(lc.wait)
        for j, (px, py) in enumerate(_other_chips()):
            them = 2 * px + py
            cp = pltpu.make_async_remote_copy(src_ref=ins[a].at[them], dst_ref=outs[a].at[me], send_sem=send.at[a, j],
                                              recv_sem=recv.at[a, j], device_id=(px, py, c), device_id_type=MESH_ID)
            landing = pltpu.make_async_remote_copy(src_ref=ins[a].at[them], dst_ref=outs[a].at[them],
                                                   send_sem=send.at[a, j], recv_sem=recv.at[a, j],
                                                   device_id=(px, py, c), device_id_type=MESH_ID)
            starts.append(cp.start)
            arrivals.append(landing.wait_recv)
            drains.append(cp.wait_send)
    return starts, arrivals + drains


def _share_small(pack):
    def body(in_ref, out_ref, send, recv, loc):
        x, y, c = (lax.axis_index(a) for a in AXES)
        me = 4 * x + 2 * y + c
        lc = pltpu.make_async_copy(in_ref, out_ref.at[me], loc.at[0])
        lc.start()
        sends, arrivals = [], []
        for k in range(1, N_DEV):
            px = 1 - x if k & 4 else x
            py = 1 - y if k & 2 else y
            pc = 1 - c if k & 1 else c
            cp = pltpu.make_async_remote_copy(src_ref=in_ref, dst_ref=out_ref.at[me], send_sem=send.at[k - 1],
                                              recv_sem=recv.at[k - 1], device_id=(px, py, pc), device_id_type=MESH_ID)
            cp.start()
            sends.append(cp)
            arrivals.append(pltpu.make_async_remote_copy(src_ref=in_ref, dst_ref=out_ref.at[4 * px + 2 * py + pc],
                                                         send_sem=send.at[k - 1], recv_sem=recv.at[k - 1],
                                                         device_id=(px, py, pc), device_id_type=MESH_ID))
        for cp in arrivals:
            cp.wait_recv()
        for cp in sends:
            cp.wait_send()
        lc.wait()

    return pl.pallas_call(
        body, name="share_small", out_shape=jax.ShapeDtypeStruct((N_DEV,) + pack.shape, pack.dtype),
        in_specs=[ANY_SPEC], out_specs=ANY_SPEC,
        scratch_shapes=[pltpu.SemaphoreType.DMA((N_DEV - 1,)), pltpu.SemaphoreType.DMA((N_DEV - 1,)),
                        pltpu.SemaphoreType.DMA((1,))],
    )(pack)


W_PARTS = ((0, 0, 2048), (2048, 4096, 8), (2056, 2048, 2048))
SHARD_COLS = 513


def _pieces(lo, hi, parts):
    out = []
    for ref_start, tgt_start, width in parts:
        a, b = max(lo, ref_start), min(hi, ref_start + width)
        if a < b:
            out.append((a - lo, tgt_start + a - ref_start, b - a))
    return out


def _build_w(win8):
    tr = 512

    def body(in_ref, w_ref, wt_ref):
        w_ref[:, 4096:W_COLS] = jnp.zeros((tr, W_COLS - 4096), MXU)
        for p in range(N_DEV):
            for src, dst, width in _pieces(p * SHARD_COLS, (p + 1) * SHARD_COLS, W_PARTS):
                w_ref[:, dst:dst + width] = in_ref[p, :, src:src + width]
        for k in range(W_COLS // 128):
            wt_ref[k * 128:(k + 1) * 128, :] = w_ref[:, k * 128:(k + 1) * 128].astype(F32).T.astype(MXU)

    return pl.pallas_call(
        body, name="build_w", grid=(D_MODEL // tr,),
        in_specs=[pl.BlockSpec((N_DEV, tr, SHARD_COLS), lambda i: (0, i, 0))],
        out_specs=[pl.BlockSpec((tr, W_COLS), lambda i: (i, 0)), pl.BlockSpec((W_COLS, tr), lambda i: (0, i))],
        out_shape=[jax.ShapeDtypeStruct((D_MODEL, W_COLS), MXU), jax.ShapeDtypeStruct((W_COLS, D_MODEL), MXU)],
        compiler_params=_params(("arbitrary",)),
    )(win8)


def _build_slabs(secs):
    tr = 512
    parts = ((0, 0, 1536), (1536, 1, 512), (2048, 6, 8), (2056, 2, 512), (2568, 3, 512), (3080, 4, 512),
             (3592, 5, 512))

    def body(*refs):
        o_ref = refs[len(secs)]
        for p in range(N_DEV):
            lo, hi = p * SHARD_COLS, (p + 1) * SHARD_COLS
            for ref_start, idx, width in parts:
                a, b = max(lo, ref_start), min(hi, ref_start + width)
                if a < b:
                    o_ref[p % 2, p // 2, :, a - lo:b - lo] = refs[idx][:, a - ref_start:b - ref_start]

    return pl.pallas_call(
        body, name="build_slabs", grid=(D_MODEL // tr,),
        in_specs=[pl.BlockSpec((tr, s.shape[1]), lambda i: (i, 0)) for s in secs],
        out_specs=pl.BlockSpec((2, 4, tr, SHARD_COLS), lambda i: (0, 0, i, 0)),
        out_shape=jax.ShapeDtypeStruct((2, 4, D_MODEL, SHARD_COLS), secs[0].dtype),
        compiler_params=_params(("arbitrary",)),
    )(*secs)


def _fill_bias(rb_ref, bk_ref, bias_ref, pair):
    for p in range(len(PATTERNS)):
        bk_p = bk_ref[p]
        for hh in range(2):
            head = 2 * pair + hh
            bm = jnp.full((BLK, 2 * BLK), NEG, F32)
            for b in range(N_BUCKETS):
                bm = jnp.where(bk_p == b, rb_ref[head, b], bm)
            bias_ref[p, hh * BLK:(hh + 1) * BLK, :] = bm


def _norm_and_gather(x, nw, shards, rb, bk):
    t = x.shape[0]
    tm = 512
    nsteps = t // tm
    n = len(shards)
    npair = ATT_HEADS // 2
    assert nsteps >= npair

    def body(x_ref, nw_ref, *rest):
        ins, (rb_ref, bk_ref, h_ref, ht_ref, bias_ref) = rest[:n], rest[n:n + 5]
        outs, sems = rest[n + 5:2 * n + 5], rest[2 * n + 5:]
        begin, finish = _gather_plan(ins, outs, *sems)
        step = pl.program_id(0)

        @pl.when(step == 0)
        def _():
            begin()

        @pl.when(step == nsteps - 1)
        def _():
            finish()

        @pl.when(step < npair)
        def _():
            _fill_bias(rb_ref, bk_ref, bias_ref, step)

        xv = x_ref[...]
        rstd = lax.rsqrt(jnp.mean(xv * xv, axis=-1, keepdims=True) + EPS)
        hf = xv * rstd * nw_ref[...]
        h_ref[...] = hf.astype(MXU)
        ht_ref[...] = hf.T.astype(MXU)

    res = pl.pallas_call(
        body, name="norm_and_gather", grid=(nsteps,),
        in_specs=[pl.BlockSpec((tm, D_MODEL), lambda i: (i, 0)), pl.BlockSpec(nw.shape, lambda i: (0, 0))]
                 + [ANY_SPEC] * n
                 + [pl.BlockSpec(memory_space=pltpu.SMEM), pl.BlockSpec(bk.shape, lambda i: (0, 0, 0))],
        out_specs=[pl.BlockSpec((tm, D_MODEL), lambda i: (i, 0)), pl.BlockSpec((D_MODEL, tm), lambda i: (0, i)),
                   pl.BlockSpec((len(PATTERNS), None, 2 * BLK, 2 * BLK), lambda i: (0, jnp.minimum(i, npair - 1), 0, 0))]
                  + [ANY_SPEC] * n,
        out_shape=[jax.ShapeDtypeStruct((t, D_MODEL), MXU), jax.ShapeDtypeStruct((D_MODEL, t), MXU),
                   jax.ShapeDtypeStruct((len(PATTERNS), npair, 2 * BLK, 2 * BLK), F32)]
                  + [jax.ShapeDtypeStruct((N_DEV,) + a.shape, a.dtype) for a in shards],
        scratch_shapes=GATHER_SEMS(n),
        compiler_params=_params(("arbitrary",)),
    )(x, nw, *shards, rb, bk)
    return res[0], res[1], res[2], res[3:]


def _inproj(h_all, w, cw, wout_shard):
    t = h_all.shape[0]
    tm = 512
    nsteps = t // tm

    def body(h_ref, w_ref, cw_ref, wo_ref, pdn_ref, qkv_ref, z_ref, patt_ref, gate_ref, ba_ref,
             wo8_ref, halo_ref, send, recv, loc):
        begin, finish = _gather_plan([wo_ref], [wo8_ref], send, recv, loc)

        @pl.when(pl.program_id(0) == 0)
        def _():
            begin()
            halo_ref[...] = jnp.zeros_like(halo_ref)

        @pl.when(pl.program_id(0) == nsteps - 1)
        def _():
            finish()

        h = h_ref[...]
        for ref, lo, hi in ((z_ref, 1536, 2048), (patt_ref, 2048, 3584), (gate_ref, 3584, 4096), (ba_ref, 4096, 4224)):
            ref[...] = jnp.dot(h, w_ref[:, lo:hi], preferred_element_type=F32)
        pdn = jnp.dot(h, w_ref[:, 0:3 * D_DN], preferred_element_type=F32)
        pdn_ref[...] = pdn
        _dn_prep_tile(pdn, halo_ref, cw_ref, qkv_ref)

    row = lambda n: pl.BlockSpec((tm, n), lambda i: (i, 0))
    full = lambda a: pl.BlockSpec(a.shape, lambda i: (0,) * a.ndim)
    return pl.pallas_call(
        body, name="inproj", grid=(nsteps,),
        in_specs=[row(D_MODEL), full(w), full(cw), ANY_SPEC],
        out_specs=[row(1536), row(1536), row(512), row(1536), row(512), row(128), ANY_SPEC],
        out_shape=[jax.ShapeDtypeStruct((t, n), F32) for n in (1536, 1536, 512, 1536, 512, 128)] +
                  [jax.ShapeDtypeStruct((N_DEV,) + wout_shard.shape, wout_shard.dtype)],
        scratch_shapes=[pltpu.VMEM((8, 3 * D_DN), F32)] + GATHER_SEMS(1),
        compiler_params=_params(("arbitrary",)),
    )(h_all, w, cw, wout_shard)


CONV_ROWS = 512


def _conv_taps(u_ref, c, w_ref):
    r0 = c * CONV_ROWS
    if c == 0:
        ext = jnp.concatenate([jnp.zeros((8, 128), F32), u_ref[0:CONV_ROWS, :]], axis=0)
    else:
        ext = u_ref[r0 - 8:r0 + CONV_ROWS, :]
    taps = [ext[8:, :]] + [pltpu.roll(ext, k, 0)[8:, :] for k in (1, 2, 3)]
    y = taps[0] * w_ref[3:4, :]
    for k in (1, 2, 3):
        y = y + taps[k] * w_ref[3 - k:4 - k, :]
    return taps, y


def _dn_prep_tile(pdn, halo_ref, cw_ref, out_ref):
    rows = pdn.shape[0]
    ext = jnp.concatenate([halo_ref[...], pdn], axis=0)
    halo_ref[...] = pdn[rows - 8:, :]
    for j in range(3 * D_DN // 128):
        cols = slice(j * 128, (j + 1) * 128)
        e = ext[:, cols]
        y = e[8:, :] * cw_ref[3:4, cols]
        for k in (1, 2, 3):
            y = y + pltpu.roll(e, k, 0)[8:, :] * cw_ref[3 - k:4 - k, cols]
        a = _silu(y)
        if j < 2 * DN_HEADS:
            a = a * lax.rsqrt(jnp.sum(a * a, axis=1, keepdims=True) + EPS)
        if j < DN_HEADS:
            a = a * DK ** -0.5
        out_ref[:, cols] = a


def _chunk_common(qkv, ba, arow, dtb):
    c = CHUNK
    ri, ci = _iota((c, c), 0), _iota((c, c), 1)
    lane = _iota((c, 128), 1)
    g_all = jnp.where((lane >= DN_HEADS) & (lane < 2 * DN_HEADS), arow * _softplus(ba + dtb), 0.0)
    gc_all = _hdot((ri >= ci).astype(F32), g_all, "a")
    gc_t = gc_all.T
    beta_all = _sigmoid(ba)
    out = []
    for h in range(DN_HEADS):
        gc = _lane_col(gc_all, DN_HEADS + h)
        gcr = gc_t[DN_HEADS + h:DN_HEADS + h + 1, :]
        gl = gc[c - 1:c, :]
        out.append(dict(
            q=qkv[:, h * DK:(h + 1) * DK], k=qkv[:, D_DN + h * DK:D_DN + (h + 1) * DK],
            v=qkv[:, 2 * D_DN + h * DK:2 * D_DN + (h + 1) * DK],
            beta=_lane_col(beta_all, h), g=_lane_col(g_all, DN_HEADS + h),
            a_raw=_lane_col(ba, DN_HEADS + h), a_h=_lane_col(arow, DN_HEADS + h), dt_h=_lane_col(dtb, DN_HEADS + h),
            decay=jnp.exp(jnp.where(ri >= ci, gc - gcr, NEG)), eg=jnp.exp(gc), egl=jnp.exp(gl), etail=jnp.exp(gl - gc)))
    return out, ri, ci


SCAN_CHUNKS = 8


def _dn_scan_fwd(qkv, ba, z, arow, dtb, dnw):
    t = qkv.shape[0]
    n = t // CHUNK
    c = CHUNK
    cps = SCAN_CHUNKS
    hs = range(DN_HEADS)
    chains = [(j, h) for j in range(cps) for h in hs]

    def body(qkv_ref, ba_ref, z_ref, arow_ref, dtb_ref, dnw_ref, o_ref, y_ref, sh_ref, th_ref, s_ref):
        @pl.when(pl.program_id(0) == 0)
        def _():
            s_ref[...] = jnp.zeros_like(s_ref)

        ms = {}
        for j in range(cps):
            rows = slice(j * c, (j + 1) * c)
            mj, ri, ci = _chunk_common(qkv_ref[rows, :], ba_ref[rows, :], arow_ref[...], dtb_ref[...])
            for h in hs:
                ms[j, h] = mj[h]
        kb = {x: ms[x]["k"] * ms[x]["beta"] for x in chains}
        amat = {x: jnp.where(ri > ci, _dot_nt(kb[x], ms[x]["k"]) * ms[x]["decay"], 0.0) for x in chains}
        attn = {x: jnp.where(ri >= ci, _dot_nt(ms[x]["q"], ms[x]["k"]) * ms[x]["decay"], 0.0) for x in chains}
        tinv = _wy_inverses(amat, (ri == ci).astype(F32))
        uw = {x: _hdot(tinv[x], jnp.concatenate([ms[x]["v"] * ms[x]["beta"], kb[x] * ms[x]["eg"]], axis=1))
              for x in chains}
        u = {x: uw[x][:, :DK] for x in chains}
        w = {x: uw[x][:, DK:] for x in chains}
        q_dec = {x: ms[x]["q"] * ms[x]["eg"] for x in chains}
        k_tail = {x: ms[x]["k"] * ms[x]["etail"] for x in chains}
        s = [s_ref[h] for h in hs]
        for j in range(cps):
            rows = slice(j * c, (j + 1) * c)
            v_new = [u[j, h] - _dot(w[j, h], s[h]) for h in hs]
            o = [_dot(q_dec[j, h], s[h]) + _dot(attn[j, h], v_new[h]) for h in hs]
            for h in hs:
                sh_ref[j, h] = s[h]
                th_ref[j, h] = tinv[j, h]
            s = [s[h] * ms[j, h]["egl"] + _dot_tn(k_tail[j, h], v_new[h]) for h in hs]
            for h in hs:
                cols = slice(h * DK, (h + 1) * DK)
                o_ref[rows, cols] = o[h]
                rs = lax.rsqrt(jnp.mean(o[h] * o[h], axis=1, keepdims=True) + EPS)
                y_ref[rows, cols] = o[h] * rs * dnw_ref[...] * _silu(z_ref[rows, cols])
        for h in hs:
            s_ref[h] = s[h]

    row = lambda w_: pl.BlockSpec((cps * c, w_), lambda i: (i, 0))
    one = pl.BlockSpec((1, 128), lambda i: (0, 0))
    return pl.pallas_call(
        body, name="dn_scan_fwd", grid=(n // cps,),
        in_specs=[row(1536), row(128), row(512), one, one, one],
        out_specs=[row(512), row(512), pl.BlockSpec((cps, DN_HEADS, DK, DK), lambda i: (i, 0, 0, 0)),
                   pl.BlockSpec((cps, DN_HEADS, c, c), lambda i: (i, 0, 0, 0))],
        out_shape=[jax.ShapeDtypeStruct((t, 512), F32), jax.ShapeDtypeStruct((t, 512), F32),
                   jax.ShapeDtypeStruct((n, DN_HEADS, DK, DK), F32), jax.ShapeDtypeStruct((n, DN_HEADS, c, c), F32)],
        scratch_shapes=[pltpu.VMEM((DN_HEADS, DK, DK), F32)],
        compiler_params=_params(("arbitrary",)),
    )(qkv, ba, z, arow, dtb, dnw)


ATT_ROWS = 512


def _pair_rstd(xv, g2_ref):
    return lax.rsqrt(_hdot(xv * xv, g2_ref[...], "b") * (1.0 / HD) + EPS)


def _pair_norm(t, raw_refs, w_refs, out_refs, g2_ref):
    for c in range(t // ATT_ROWS):
        sl = slice(c * ATT_ROWS, (c + 1) * ATT_ROWS)
        for raw, w_ref, out in zip(raw_refs, w_refs, out_refs):
            xv = raw[sl, :]
            out[sl, :] = xv * _pair_rstd(xv, g2_ref) * w_ref[...]


BIAS_SPEC = pl.BlockSpec((len(PATTERNS), None, 2 * BLK, 2 * BLK), lambda i: (0, i, 0, 0))
PAIR_ROW_SPEC = pl.BlockSpec((1, 128), lambda i: (0, i))


def _stack_heads(xb, h0):
    return jnp.concatenate([jnp.where(h0, xb, 0.0), jnp.where(h0, 0.0, xb)], axis=0).astype(MXU)


def _block_rows(t, r, n):
    per_class = (t // r) // BLK
    res = n // per_class
    j = n % per_class
    start = res + BLK * r * j
    pstart = res + BLK * r * jnp.maximum(j - 1, 0)
    if r == 1:
        return pl.ds(pl.multiple_of(start, BLK), BLK), pl.ds(pl.multiple_of(pstart, BLK), BLK), j
    return pl.ds(start, BLK, stride=r), pl.ds(pstart, BLK, stride=r), j


def _att_fwd(qkv, gate, bias, wq, wk, g2):
    t = qkv.shape[0]
    rows = ATT_ROWS

    def body(bias_ref, qraw_ref, kraw_ref, v_ref, g_ref, wq_ref, wk_ref, g2_ref, o_ref, y_ref, lse_ref,
             o0_ref, o1_ref, o2_ref, l0_ref, l1_ref, l2_ref, q_ref, k_ref):
        h0 = _iota((BLK, 128), 1) < HD
        prev_cols = _iota((2 * BLK, 2 * BLK), 1) < BLK
        op_refs, lp_refs = (o0_ref, o1_ref, o2_ref), (l0_ref, l1_ref, l2_ref)
        _pair_norm(t, (qraw_ref, kraw_ref), (wq_ref, wk_ref), (q_ref, k_ref), g2_ref)

        for p, (_, r) in enumerate(PATTERNS):
            def blk(n, carry, p=p, r=r):
                cur, prev, j = _block_rows(t, r, n)
                q2 = _stack_heads(q_ref[cur, :], h0)
                k2 = jnp.concatenate([k_ref[prev, :], k_ref[cur, :]], axis=0).astype(MXU)
                v2 = jnp.concatenate([v_ref[prev, :], v_ref[cur, :]], axis=0).astype(MXU)
                s = _dot_nt(q2, k2) + bias_ref[p] + jnp.where(prev_cols & (j == 0), NEG, 0.0)
                m = jnp.max(s, axis=1, keepdims=True)
                e = jnp.exp(s - m)
                l = jnp.sum(e, axis=1, keepdims=True)
                pv = _dot(e, v2) / l
                lse = m + jnp.log(l)
                op_refs[p][cur, :] = jnp.where(h0, pv[:BLK], pv[BLK:])
                lp_refs[p][cur, :] = jnp.where(h0, lse[:BLK], lse[BLK:])
                return carry

            lax.fori_loop(0, t // BLK, blk, 0, unroll=16)

        for c in range(t // rows):
            sl = slice(c * rows, (c + 1) * rows)
            ls = [ref[sl, :] for ref in lp_refs]
            mx = jnp.maximum(jnp.maximum(ls[0], ls[1]), ls[2])
            ws = [jnp.exp(v_ - mx) for v_ in ls]
            den = ws[0] + ws[1] + ws[2]
            o = (ws[0] * o0_ref[sl, :] + ws[1] * o1_ref[sl, :] + ws[2] * o2_ref[sl, :]) / den
            o_ref[sl, :] = o
            y_ref[sl, :] = o * _silu(g_ref[sl, :])
            lse_ref[sl, :] = mx + jnp.log(den)

    col = lambda off: pl.BlockSpec((t, 128), lambda i, off=off: (0, off + i))
    return pl.pallas_call(
        body, name="att_fwd", grid=(ATT_HEADS // 2,),
        in_specs=[BIAS_SPEC, col(0), col(4), col(8), col(0), PAIR_ROW_SPEC, PAIR_ROW_SPEC,
                  pl.BlockSpec((128, 128), lambda i: (0, 0))],
        out_specs=[col(0), col(0), col(0)],
        out_shape=[jax.ShapeDtypeStruct((t, 512), F32)] * 3,
        scratch_shapes=[pltpu.VMEM((t, 128), F32)] * 8,
        compiler_params=_params(("arbitrary",)),
    )(bias, qkv, qkv, qkv, gate, wq, wk, g2)


def _outproj_loss(x, ydn, yatt, wout, target):
    t = x.shape[0]
    tm = 512

    def body(x_ref, a_ref, b_ref, w_ref, t_ref, dy_ref, mix_ref, loss_ref):
        @pl.when(pl.program_id(0) == 0)
        def _():
            loss_ref[...] = jnp.zeros_like(loss_ref)

        mixf = jnp.concatenate([a_ref[...], b_ref[...]], axis=1)
        mix_ref[...] = mixf.T.astype(MXU)
        err = x_ref[...] + jnp.dot(mixf.astype(MXU), w_ref[...], preferred_element_type=F32) - t_ref[...]
        dy_ref[...] = err * (1.0 / D_MODEL)
        loss_ref[...] += jnp.sum(err * err) * (0.5 / D_MODEL)

    row = lambda n: pl.BlockSpec((tm, n), lambda i: (i, 0))
    return pl.pallas_call(
        body, name="outproj_loss", grid=(t // tm,),
        in_specs=[row(D_MODEL), row(512), row(512), pl.BlockSpec(wout.shape, lambda i: (0, 0)), row(D_MODEL)],
        out_specs=[row(D_MODEL), pl.BlockSpec((D_MODEL, tm), lambda i: (0, i)), pl.BlockSpec((8, 128), lambda i: (0, 0))],
        out_shape=[jax.ShapeDtypeStruct((t, D_MODEL), F32), jax.ShapeDtypeStruct((D_MODEL, t), MXU),
                   jax.ShapeDtypeStruct((8, 128), F32)],
        compiler_params=_params(("arbitrary",)),
    )(x, ydn, yatt, wout, target)


def _outproj_bwd(dy, wout_t, oraw, z, dnw, oatt, gate, g, gt):
    t = dy.shape[0]
    tm = 512

    def body(dy_ref, w_ref, o_ref, z_ref, dnw_ref, oa_ref, g_ref, grp_ref, grpt_ref,
             do_ref, dz_ref, doa_ref, dg_ref, dd_ref, ddnw_ref):
        @pl.when(pl.program_id(0) == 0)
        def _():
            ddnw_ref[...] = jnp.zeros_like(ddnw_ref)

        dmix = jnp.dot(dy_ref[...].astype(MXU), w_ref[...], preferred_element_type=F32)
        dnw_v = dnw_ref[...]
        acc = jnp.zeros((1, DK), F32)
        for h in range(DN_HEADS):
            sl = slice(h * DK, (h + 1) * DK)
            o, zz, dm = o_ref[:, sl], z_ref[:, sl], dmix[:, sl]
            rs = lax.rsqrt(jnp.mean(o * o, axis=1, keepdims=True) + EPS)
            oh = o * rs
            silu_z, dsilu_z = _silu_and_grad(zz)
            dz_ref[:, sl] = dm * oh * dnw_v * dsilu_z
            d_on = dm * silu_z
            gg = d_on * dnw_v
            do_ref[:, sl] = rs * (gg - oh * jnp.mean(gg * oh, axis=1, keepdims=True))
            acc = acc + jnp.sum(d_on * oh, axis=0, keepdims=True)
        ddnw_ref[...] += jnp.broadcast_to(acc, (8, DK))
        da, gate_v, oa = dmix[:, 512:], g_ref[...], oa_ref[...]
        silu_g, dsilu_g = _silu_and_grad(gate_v)
        doa = da * silu_g
        doa_ref[...] = doa
        dg_ref[...] = da * oa * dsilu_g
        dd_ref[...] = _hdot(_hdot(doa * oa, grp_ref[...], "b"), grpt_ref[...], "b")

    row = lambda n: pl.BlockSpec((tm, n), lambda i: (i, 0))
    full = lambda a: pl.BlockSpec(a.shape, lambda i: (0,) * a.ndim)
    return pl.pallas_call(
        body, name="outproj_bwd", grid=(t // tm,),
        in_specs=[row(D_MODEL), full(wout_t), row(512), row(512), full(dnw), row(512), row(512), full(g), full(gt)],
        out_specs=[row(512)] * 5 + [pl.BlockSpec((8, DK), lambda i: (0, 0))],
        out_shape=[jax.ShapeDtypeStruct((t, 512), F32)] * 5 + [jax.ShapeDtypeStruct((8, DK), F32)],
        compiler_params=_params(("arbitrary",)),
    )(dy, wout_t, oraw, z, dnw, oatt, gate, g, gt)


def _grad_matmul(at, b, name):
    m, t = at.shape
    n = b.shape[1]
    tk = 1024
    tn = n if n <= 1536 else 512
    nk = t // tk

    def body(a_ref, b_ref, o_ref, acc_ref):
        k = pl.program_id(1)

        @pl.when(k == 0)
        def _():
            acc_ref[...] = jnp.zeros_like(acc_ref)

        acc_ref[...] += jnp.dot(a_ref[...], b_ref[...].astype(MXU), preferred_element_type=F32)

        @pl.when(k == nk - 1)
        def _():
            o_ref[...] = acc_ref[...].astype(GRAD_WIRE)

    return pl.pallas_call(
        body, name=name, grid=(n // tn, nk),
        in_specs=[pl.BlockSpec((m, tk), lambda j, k: (0, k)), pl.BlockSpec((tk, tn), lambda j, k: (k, j))],
        out_specs=pl.BlockSpec((m, tn), lambda j, k: (0, j)),
        out_shape=jax.ShapeDtypeStruct((m, n), GRAD_WIRE),
        scratch_shapes=[pltpu.VMEM((m, tn), F32)],
        compiler_params=_params(("arbitrary", "arbitrary")),
    )(at, b)


def _grad_matmul_many(at, bs, name):
    m, t = at.shape
    n = bs[0].shape[1]
    nb = len(bs)
    tk = 1024
    nk = t // tk

    def body(a_ref, *refs):
        b_refs, o_refs, acc_ref = refs[:nb], refs[nb:2 * nb], refs[2 * nb]
        s, k = pl.program_id(0), pl.program_id(1)

        @pl.when(k == 0)
        def _():
            acc_ref[...] = jnp.zeros_like(acc_ref)

        for i in range(nb):
            @pl.when(s == i)
            def _(i=i):
                acc_ref[...] += jnp.dot(a_ref[...], b_refs[i][...].astype(MXU), preferred_element_type=F32)

                @pl.when(k == nk - 1)
                def _():
                    o_refs[i][...] = acc_ref[...].astype(GRAD_WIRE)

    def b_spec(i):
        return pl.BlockSpec((tk, n), lambda s, k: (jnp.where(s == i, k, jnp.where(s < i, 0, nk - 1)), 0))

    return pl.pallas_call(
        body, name=name, grid=(nb, nk),
        in_specs=[pl.BlockSpec((m, tk), lambda s, k: (0, k))] + [b_spec(i) for i in range(nb)],
        out_specs=[pl.BlockSpec((m, n), lambda s, k: (0, 0))] * nb,
        out_shape=[jax.ShapeDtypeStruct((m, n), GRAD_WIRE)] * nb,
        scratch_shapes=[pltpu.VMEM((m, n), F32)],
        compiler_params=_params(("arbitrary", "arbitrary")),
    )(at, *bs)


def _att_bwd(qkv, do, lse, dd, bias, bk, wq, wk, g2):
    t = qkv.shape[0]
    rows = ATT_ROWS

    def body(bias_ref, bk_ref, qraw_ref, kraw_ref, v_ref, do_ref, lse_ref, dd_ref, wq_ref, wk_ref, g2_ref,
             dq_ref, dk_ref, dv_ref, db_ref, dwq_ref, dwk_ref, ds_ref, q_ref, k_ref):
        pair = pl.program_id(0)

        @pl.when(pair == 0)
        def _():
            db_ref[...] = jnp.zeros_like(db_ref)

        _pair_norm(t, (qraw_ref, kraw_ref), (wq_ref, wk_ref), (q_ref, k_ref), g2_ref)
        ds_ref[...] = jnp.zeros_like(ds_ref)
        for c in range(t // rows):
            sl = slice(c * rows, (c + 1) * rows)
            for ref in (dq_ref, dk_ref, dv_ref):
                ref[sl, :] = jnp.zeros((rows, 128), F32)
        h0 = _iota((BLK, 128), 1) < HD
        prev_cols = _iota((2 * BLK, 2 * BLK), 1) < BLK

        def rows_of(xb):
            return jnp.concatenate([xb[:, 0:1], xb[:, HD:HD + 1]], axis=0)

        for p, (_, r) in enumerate(PATTERNS):
            def blk(n, carry, p=p, r=r):
                cur, prev, j = _block_rows(t, r, n)
                q2, do2 = _stack_heads(q_ref[cur, :], h0), _stack_heads(do_ref[cur, :], h0)
                k2 = jnp.concatenate([k_ref[prev, :], k_ref[cur, :]], axis=0).astype(MXU)
                v2 = jnp.concatenate([v_ref[prev, :], v_ref[cur, :]], axis=0).astype(MXU)
                s = _dot_nt(q2, k2) + bias_ref[p] + jnp.where(prev_cols & (j == 0), NEG, 0.0)
                prob = jnp.exp(s - rows_of(lse_ref[cur, :]))
                ds = prob * (_dot_nt(do2, v2) - rows_of(dd_ref[cur, :]))
                ds_ref[p] += ds
                dq2 = _dot(ds, k2)
                dk2 = _dot_tn(ds, q2)
                dv2 = _dot_tn(prob, do2)
                dq_ref[cur, :] += jnp.where(h0, dq2[:BLK], dq2[BLK:])
                dk_ref[prev, :] += dk2[:BLK]
                dv_ref[prev, :] += dv2[:BLK]
                dk_ref[cur, :] += dk2[BLK:]
                dv_ref[cur, :] += dv2[BLK:]
                return carry

            lax.fori_loop(0, t // BLK, blk, 0, unroll=8)

        ri, ci = _iota((8, 128), 0), _iota((8, 128), 1)
        upd = jnp.zeros((8, 128), F32)
        for p in range(len(PATTERNS)):
            bk = bk_ref[p]
            for hh in range(2):
                dsum = ds_ref[p, hh * BLK:(hh + 1) * BLK, :]
                for b in range(N_BUCKETS):
                    val = jnp.sum(jnp.where(bk == b, dsum, 0.0))
                    upd = upd + jnp.where((ri == 2 * pair + hh) & (ci == b), val, 0.0)
        db_ref[...] += upd

        for raw, d_ref, w_ref, dw_ref in ((qraw_ref, dq_ref, wq_ref, dwq_ref), (kraw_ref, dk_ref, wk_ref, dwk_ref)):
            acc = jnp.zeros((1, 128), F32)
            for c in range(t // rows):
                sl = slice(c * rows, (c + 1) * rows)
                xv, dyv = raw[sl, :], d_ref[sl, :]
                rs = _pair_rstd(xv, g2_ref)
                xh = xv * rs
                gg = dyv * w_ref[...]
                mean = _hdot(gg * xh, g2_ref[...], "b") * (1.0 / HD)
                d_ref[sl, :] = rs * (gg - xh * mean)
                acc = acc + jnp.sum(dyv * xh, axis=0, keepdims=True)
            dw_ref[...] = jnp.broadcast_to(acc, (8, 128))

    col = lambda off: pl.BlockSpec((t, 128), lambda i, off=off: (0, off + i))
    acc8 = pl.BlockSpec((8, 128), lambda i: (0, i))
    return pl.pallas_call(
        body, name="att_bwd", grid=(ATT_HEADS // 2,),
        in_specs=[BIAS_SPEC, pl.BlockSpec(bk.shape, lambda i: (0, 0, 0)),
                  col(0), col(4), col(8), col(0), col(0), col(0), PAIR_ROW_SPEC, PAIR_ROW_SPEC,
                  pl.BlockSpec((128, 128), lambda i: (0, 0))],
        out_specs=[col(0), col(0), col(0), pl.BlockSpec((8, 128), lambda i: (0, 0)), acc8, acc8],
        out_shape=[jax.ShapeDtypeStruct((t, 512), F32)] * 3 + [jax.ShapeDtypeStruct((8, 128), F32)]
                  + [jax.ShapeDtypeStruct((8, 512), F32)] * 2,
        scratch_shapes=[pltpu.VMEM((len(PATTERNS), 2 * BLK, 2 * BLK), F32)] + [pltpu.VMEM((t, 128), F32)] * 2,
        compiler_params=_params(("arbitrary",)),
    )(bias, bk, qkv, qkv, qkv, do, lse, dd, wq, wk, g2)


def _dn_scan_bwd(qkv, ba, do, sh, th, arow, dtb):
    t = qkv.shape[0]
    n = t // CHUNK
    c = CHUNK
    cps = SCAN_CHUNKS

    def body(qkv_ref, ba_ref, do_ref, sh_ref, th_ref, arow_ref, dtb_ref, dqkv_ref, dba_ref, ds_ref):
        @pl.when(pl.program_id(0) == 0)
        def _():
            ds_ref[...] = jnp.zeros_like(ds_ref)

        hs = range(DN_HEADS)
        chains = [(j, h) for j in range(cps) for h in hs]
        lane = _iota((c, 128), 1)
        row = _iota((c, 1), 0)
        ms = {}
        for j in range(cps):
            rows_j = slice(j * c, (j + 1) * c)
            mj, ri, ci = _chunk_common(qkv_ref[rows_j, :], ba_ref[rows_j, :], arow_ref[...], dtb_ref[...])
            for h in hs:
                ms[j, h] = mj[h]
        q, k, v = ({x: ms[x][nm] for x in chains} for nm in ("q", "k", "v"))
        beta, decay = ({x: ms[x][nm] for x in chains} for nm in ("beta", "decay"))
        eg, egl, etail = ({x: ms[x][nm] for x in chains} for nm in ("eg", "egl", "etail"))
        s = {x: sh_ref[x[0], x[1]] for x in chains}
        tinv = {x: th_ref[x[0], x[1]] for x in chains}
        d_o = {(j, h): do_ref[j * c:(j + 1) * c, h * DK:(h + 1) * DK] for j, h in chains}
        kb = {x: k[x] * beta[x] for x in chains}
        vb = {x: v[x] * beta[x] for x in chains}
        kbg = {x: kb[x] * eg[x] for x in chains}
        amat = {x: jnp.where(ri > ci, _dot_nt(kb[x], k[x]) * decay[x], 0.0) for x in chains}
        attn = {x: jnp.where(ri >= ci, _dot_nt(q[x], k[x]) * decay[x], 0.0) for x in chains}
        uw = {x: _hdot(tinv[x], jnp.concatenate([vb[x], kbg[x]], axis=1)) for x in chains}
        u = {x: uw[x][:, :DK] for x in chains}
        w = {x: uw[x][:, DK:] for x in chains}
        v_new = {x: u[x] - _dot(w[x], s[x]) for x in chains}
        q_dec = {x: q[x] * eg[x] for x in chains}
        k_tail = {x: k[x] * etail[x] for x in chains}
        d_attn = {x: jnp.where(ri >= ci, _dot_nt(d_o[x], v_new[x]), 0.0) for x in chains}
        d_qdec = {x: _dot_nt(d_o[x], s[x]) for x in chains}
        from_o = {x: _dot_tn(attn[x], d_o[x]) for x in chains}
        to_state = {x: _dot_tn(q_dec[x], d_o[x]) for x in chains}

        d_s, d_vnew = {}, {}
        cur = [ds_ref[h] for h in hs]
        for j in reversed(range(cps)):
            for h in hs:
                d_s[j, h] = cur[h]
                d_vnew[j, h] = from_o[j, h] + _dot(k_tail[j, h], cur[h])
            cur = [to_state[j, h] + cur[h] * egl[j, h] - _dot_tn(w[j, h], d_vnew[j, h]) for h in hs]
        for h in hs:
            ds_ref[h] = cur[h]

        d_ktail = {x: _dot_nt(v_new[x], d_s[x]) for x in chains}
        d_gl = {x: jnp.sum(s[x] * d_s[x]) * egl[x] for x in chains}
        d_w = {x: -_dot_nt(d_vnew[x], s[x]) for x in chains}
        d_both = {x: _hdot_tn(tinv[x], jnp.concatenate([d_vnew[x], d_w[x]], axis=1)) for x in chains}
        d_vb = {x: d_both[x][:, :DK] for x in chains}
        d_kbg = {x: d_both[x][:, DK:] for x in chains}
        d_a = {x: -jnp.where(ri > ci, _hdot_nt(d_both[x], uw[x]), 0.0) for x in chains}
        d_qk = {x: d_attn[x] * decay[x] for x in chains}
        d_kk = {x: d_a[x] * decay[x] for x in chains}
        d_kb = {x: _dot(d_kk[x], k[x]) + d_kbg[x] * eg[x] for x in chains}
        d_q = {x: _dot(d_qk[x], k[x]) + d_qdec[x] * eg[x] for x in chains}
        d_k = {x: _dot_tn(d_qk[x], q[x]) + _dot_tn(d_kk[x], kb[x]) + d_ktail[x] * etail[x] + d_kb[x] * beta[x]
               for x in chains}
        d_beta = {x: jnp.sum(d_kb[x] * k[x] + d_vb[x] * v[x], axis=1, keepdims=True) for x in chains}
        mm = {x: d_a[x] * amat[x] + d_attn[x] * attn[x] for x in chains}
        for j in range(cps):
            rows_j = slice(j * c, (j + 1) * c)
            rows = jnp.zeros((c, c), F32)
            for h in hs:
                rows = rows + jnp.where(ri == h, jnp.sum(mm[j, h], axis=0, keepdims=True), 0.0)
            cols_t = jnp.concatenate([rows, jnp.zeros((c, c), F32)], axis=1).T[:c, :]
            d_gc_all = jnp.zeros((c, 128), F32)
            for h in hs:
                x = (j, h)
                tail_term = jnp.sum(d_ktail[x] * k_tail[x], axis=1, keepdims=True)
                d_gc = (jnp.sum(mm[x], axis=1, keepdims=True) - _lane_col(cols_t, h)
                        + jnp.sum(d_qdec[x] * q_dec[x] + d_kbg[x] * kbg[x], axis=1, keepdims=True) - tail_term)
                d_gc = d_gc + jnp.where(row == c - 1, jnp.sum(tail_term) + d_gl[x], 0.0)
                d_gc_all = d_gc_all + jnp.where(lane == DN_HEADS + h, d_gc, 0.0)
            d_g_all = _hdot((ri <= ci).astype(F32), d_gc_all, "a")
            dba = jnp.zeros((c, 128), F32)
            for h in hs:
                x = (j, h)
                d_g = _lane_col(d_g_all, DN_HEADS + h)
                d_braw = d_beta[x] * beta[x] * (1.0 - beta[x])
                d_araw = d_g * ms[x]["a_h"] * _sigmoid(ms[x]["a_raw"] + ms[x]["dt_h"])
                dba = dba + jnp.where(lane == h, d_braw, 0.0) + jnp.where(lane == DN_HEADS + h, d_araw, 0.0) \
                    + jnp.where(lane == 2 * DN_HEADS + h, d_g * ms[x]["g"], 0.0)
                dqkv_ref[rows_j, h * DK:(h + 1) * DK] = d_q[x]
                dqkv_ref[rows_j, D_DN + h * DK:D_DN + (h + 1) * DK] = d_k[x]
                dqkv_ref[rows_j, 2 * D_DN + h * DK:2 * D_DN + (h + 1) * DK] = d_vb[x] * beta[x]
            dba_ref[rows_j, :] = dba

    nsteps = n // cps
    rev = lambda w_: pl.BlockSpec((cps * c, w_), lambda i: (nsteps - 1 - i, 0))
    one = pl.BlockSpec((1, 128), lambda i: (0, 0))
    return pl.pallas_call(
        body, name="dn_scan_bwd", grid=(nsteps,),
        in_specs=[rev(1536), rev(128), rev(512),
                  pl.BlockSpec((cps, DN_HEADS, DK, DK), lambda i: (nsteps - 1 - i, 0, 0, 0)),
                  pl.BlockSpec((cps, DN_HEADS, c, c), lambda i: (nsteps - 1 - i, 0, 0, 0)), one, one],
        out_specs=[rev(1536), rev(128)],
        out_shape=[jax.ShapeDtypeStruct((t, 1536), F32), jax.ShapeDtypeStruct((t, 128), F32)],
        scratch_shapes=[pltpu.VMEM((DN_HEADS, DK, DK), F32)],
        compiler_params=_params(("arbitrary",)),
    )(qkv, ba, do, sh, th, arow, dtb)


def _dn_prep_bwd(pdn, cw, dact):
    t = pdn.shape[0]
    nchunk = t // CONV_ROWS

    def body(u_ref, w_ref, d_ref, du_ref, dw_ref, dy_ref):
        j = pl.program_id(0)
        dy_ref[t:t + 8, :] = jnp.zeros((8, 128), F32)
        dw = [jnp.zeros((1, 128), F32) for _ in range(4)]
        for c in range(nchunk):
            sl = slice(c * CONV_ROWS, (c + 1) * CONV_ROWS)
            taps, y = _conv_taps(u_ref, c, w_ref)
            a, da_dy = _silu_and_grad(y)
            dout = d_ref[sl, :]
            rs = lax.rsqrt(jnp.sum(a * a, axis=1, keepdims=True) + EPS)
            f = jnp.where(j < 8, rs, 1.0) * jnp.where(j < 4, DK ** -0.5, 1.0)
            corr = jnp.where(j < 8, f * rs * rs * jnp.sum(dout * a, axis=1, keepdims=True), 0.0)
            dy = (f * dout - corr * a) * da_dy
            dy_ref[sl, :] = dy
            for k_ in range(4):
                dw[3 - k_] = dw[3 - k_] + jnp.sum(taps[k_] * dy, axis=0, keepdims=True)
        for i in range(4):
            dw_ref[i:i + 1, :] = dw[i]
        for c in range(nchunk):
            r0 = c * CONV_ROWS
            ext = dy_ref[r0:r0 + CONV_ROWS + 8, :]
            du = ext[:CONV_ROWS, :] * w_ref[3:4, :]
            for k_ in (1, 2, 3):
                du = du + pltpu.roll(ext, CONV_ROWS + 8 - k_, 0)[:CONV_ROWS, :] * w_ref[3 - k_:4 - k_, :]
            du_ref[r0:r0 + CONV_ROWS, :] = du

    return pl.pallas_call(
        body, name="dn_prep_bwd", grid=(12,),
        in_specs=[pl.BlockSpec((t, 128), lambda j: (0, j)), pl.BlockSpec((4, 128), lambda j: (0, j)),
                  pl.BlockSpec((t, 128), lambda j: (0, j))],
        out_specs=[pl.BlockSpec((t, 128), lambda j: (0, j)), pl.BlockSpec((4, 128), lambda j: (0, j))],
        out_shape=[jax.ShapeDtypeStruct((t, 1536), F32), jax.ShapeDtypeStruct((4, 1536), F32)],
        scratch_shapes=[pltpu.VMEM((t + 8, 128), F32)],
        compiler_params=_params(("arbitrary",)),
    )(pdn, cw, dact)


SECTIONS = (("dn", 0, 1536), ("z", 1536, 512), ("q", 2048, 512), ("k", 2560, 512), ("v", 3072, 512),
            ("gate", 3584, 512), ("ba", 4096, 128))


def _inproj_bwd(x, nw, wt, dy, dsecs, partials):
    t = x.shape[0]
    tm = 256
    npart = len(partials)
    nsteps = t // tm

    nsec = len(SECTIONS)

    def body(x_ref, nw_ref, w_ref, dy_ref, *rest):
        sec_refs, rest = rest[:nsec], rest[nsec:]
        part_refs, (gx_ref, dnw_ref, cs_ref) = rest[:npart], rest[npart:npart + 3]
        got_refs, (send, recv, loc) = rest[npart + 3:2 * npart + 3], rest[2 * npart + 3:]
        starts, waits = _chip_swap_copies(part_refs, got_refs, send, recv, loc)

        @pl.when(pl.program_id(0) == 0)
        def _():
            for start in starts:
                start()
            dnw_ref[...] = jnp.zeros_like(dnw_ref)
            cs_ref[...] = jnp.zeros_like(cs_ref)

        @pl.when(pl.program_id(0) == nsteps - 1)
        def _():
            for wait in waits:
                wait()

        dh = jnp.zeros((tm, D_MODEL), F32)
        for ref, (_, lo, width) in zip(sec_refs, SECTIONS):
            dh = dh + jnp.dot(ref[...].astype(MXU), w_ref[lo:lo + width, :], preferred_element_type=F32)
        xv = x_ref[...]
        rstd = lax.rsqrt(jnp.mean(xv * xv, axis=-1, keepdims=True) + EPS)
        xh = xv * rstd
        gg = dh * nw_ref[...]
        gx_ref[...] = rstd * (gg - xh * jnp.mean(gg * xh, axis=-1, keepdims=True)) + dy_ref[...]
        dnw_ref[...] += jnp.broadcast_to(jnp.sum(dh * xh, axis=0, keepdims=True), (8, D_MODEL))
        cs_ref[...] += jnp.broadcast_to(jnp.sum(sec_refs[nsec - 1][...], axis=0, keepdims=True), (8, 128))

    row = lambda n: pl.BlockSpec((tm, n), lambda i: (i, 0))
    full = lambda a: pl.BlockSpec(a.shape, lambda i: (0,) * a.ndim)
    res = pl.pallas_call(
        body, name="inproj_bwd", grid=(nsteps,),
        in_specs=[row(D_MODEL), full(nw), full(wt), row(D_MODEL)] + [row(width) for _, _, width in SECTIONS]
                 + [ANY_SPEC] * npart,
        out_specs=[row(D_MODEL), pl.BlockSpec((8, D_MODEL), lambda i: (0, 0)), pl.BlockSpec((8, 128), lambda i: (0, 0))]
                  + [ANY_SPEC] * npart,
        out_shape=[jax.ShapeDtypeStruct((t, D_MODEL), F32), jax.ShapeDtypeStruct((8, D_MODEL), F32),
                   jax.ShapeDtypeStruct((8, 128), F32)] + [jax.ShapeDtypeStruct(p.shape, p.dtype) for p in partials],
        scratch_shapes=[pltpu.SemaphoreType.DMA((npart, 3)), pltpu.SemaphoreType.DMA((npart, 3)),
                        pltpu.SemaphoreType.DMA((npart,))],
        compiler_params=_params(("arbitrary",)),
    )(x, nw, wt, dy, *dsecs, *partials)
    return res[0], res[1], res[2], res[3:]


def _adamw_sum(w, gs, m, v, name):
    r, c = w.shape
    nsum = gs.shape[0]
    tr = r if r <= 512 else 512
    c1 = 1.0 - ADAM_B1 ** ADAM_STEP
    c2 = 1.0 - ADAM_B2 ** ADAM_STEP

    def body(w_ref, g_ref, m_ref, v_ref, go_ref, d_ref, mo_ref, vo_ref):
        g = g_ref[0].astype(F32)
        for s in range(1, nsum):
            g = g + g_ref[s].astype(F32)
        mn = ADAM_B1 * m_ref[...] + (1.0 - ADAM_B1) * g
        vn = ADAM_B2 * v_ref[...] + (1.0 - ADAM_B2) * (g * g)
        go_ref[...] = g
        mo_ref[...] = mn
        vo_ref[...] = vn
        d_ref[...] = -ADAM_LR * ((mn / c1) / (jnp.sqrt(vn / c2) + ADAM_EPS) + ADAM_WD * w_ref[...])

    blk = pl.BlockSpec((tr, c), lambda i: (i, 0))
    return pl.pallas_call(
        body, name=name, grid=(r // tr,),
        in_specs=[blk, pl.BlockSpec((nsum, tr, c), lambda i: (0, i, 0)), blk, blk],
        out_specs=[blk] * 4, out_shape=[jax.ShapeDtypeStruct((r, c), F32)] * 4,
        compiler_params=_params(("arbitrary",)),
    )(w, gs, m, v)


def _local_step(x, target, h, ht, bias, bk, w_sect, conv_w, a_log, dt_bias, dn_norm_w, q_norm_w, k_norm_w, wout_shard):
    arow = jnp.zeros((1, 128), F32).at[0, DN_HEADS:2 * DN_HEADS].set(-jnp.exp(a_log[0]))
    dtb = jnp.zeros((1, 128), F32).at[0, DN_HEADS:2 * DN_HEADS].set(dt_bias[0])
    g_np, gt_np = _group_mats()
    g, gt = jnp.asarray(g_np), jnp.asarray(gt_np)
    g2 = jnp.asarray(np.kron(np.eye(2, dtype=np.float32), np.ones((HD, HD), np.float32)))
    wq = jnp.tile(q_norm_w, (1, ATT_HEADS)) * (HD ** -0.5)
    wk = jnp.tile(k_norm_w, (1, ATT_HEADS))

    pdn, qkv_dn, z, patt, gate, ba, wout8 = _inproj(h, w_sect, conv_w, wout_shard)
    w_out = wout8.reshape(D_MODEL, D_MODEL)
    oraw, ydn, sh, th = _dn_scan_fwd(qkv_dn, ba, z, arow, dtb, dn_norm_w)
    oatt, yatt, lse = _att_fwd(patt, gate, bias, wq, wk, g2)
    dy, mix_t, loss8 = _outproj_loss(x, ydn, yatt, w_out, target)

    do_dn, dz, do_att, dgate, dd, ddnw = _outproj_bwd(dy, w_out.T, oraw, z, dn_norm_w, oatt, gate, g, gt)
    d_wout = _grad_matmul(mix_t, dy, "dw_out")
    dq, dk, dv, drb, dwq8, dwk8 = _att_bwd(patt, do_att, lse, dd, bias, bk, wq, wk, g2)
    dqkv_dn, dba = _dn_scan_bwd(qkv_dn, ba, do_dn, sh, th, arow, dtb)
    dpdn, d_conv = _dn_prep_bwd(pdn, conv_w, dqkv_dn)
    dsecs = (dpdn, dz, dq, dk, dv, dgate, dba)
    dw_mid = _grad_matmul_many(ht, dsecs[1:6], "dw_in_mid")
    dw_sections = [_grad_matmul(ht, dpdn, "dw_in_dn"), *dw_mid, _grad_matmul(ht, dba, "dw_in_ba")]
    return dict(w_in_sections=dw_sections, conv_w=d_conv, w_out=d_wout, dy=dy, dsecs=dsecs,
                small_parts=(loss8, ddnw, dwq8, dwk8, drb))


def _finish_step(x, norm_w, w_sect_t, gr, partials):
    grad_x, dnw8, cs8, got = _inproj_bwd(x, norm_w, w_sect_t, gr["dy"], gr["dsecs"], partials)
    return grad_x, _pack_small_grads(dnw8, cs8, *gr["small_parts"]), got


SMALL_ROWS = 24
SMALL_AT = dict(a_log=(slice(8, 9), slice(0, 4)), dt_bias=(slice(9, 10), slice(0, 4)),
                dn_norm_w=(slice(10, 11), slice(0, 128)), q_norm_w=(slice(11, 12), slice(0, HD)),
                k_norm_w=(slice(12, 13), slice(0, HD)), rel_bias=(slice(16, 24), slice(0, N_BUCKETS)))
SMALL_NAMES = ("norm_w", "a_log", "dt_bias", "dn_norm_w", "q_norm_w", "k_norm_w", "rel_bias")


LOSS_ROW = 13


def _pack_small_grads(dnw8, cs8, loss8, ddnw8, dwq8, dwk8, drb):
    def body(dnw_ref, cs_ref, loss_ref, ddnw_ref, dwq_ref, dwk_ref, drb_ref, o_ref):
        lane = _iota((8, 128), 1)
        o_ref[...] = jnp.zeros_like(o_ref)
        o_ref[LOSS_ROW:LOSS_ROW + 1, :] = jnp.where(lane == 0, loss_ref[...], 0.0)[0:1, :]
        for k in range(D_MODEL // 128):
            o_ref[k:k + 1, :] = dnw_ref[0:1, k * 128:(k + 1) * 128]
        cs = cs_ref[...]
        o_ref[8:9, :] = jnp.where(lane < DN_HEADS, pltpu.roll(cs, 128 - 2 * DN_HEADS, 1), 0.0)[0:1, :]
        o_ref[9:10, :] = jnp.where(lane < DN_HEADS, pltpu.roll(cs, 128 - DN_HEADS, 1), 0.0)[0:1, :]
        o_ref[10:11, :] = ddnw_ref[0:1, :]
        for row, ref, scale in ((11, dwq_ref, HD ** -0.5), (12, dwk_ref, 1.0)):
            acc = ref[:, 0:128] + ref[:, 128:256] + ref[:, 256:384] + ref[:, 384:512]
            acc = (acc + pltpu.roll(acc, HD, 1)) * scale
            o_ref[row:row + 1, :] = jnp.where(lane < HD, acc, 0.0)[0:1, :]
        o_ref[16:24, :] = drb_ref[...]

    return pl.pallas_call(body, name="pack_small_grads", out_shape=jax.ShapeDtypeStruct((SMALL_ROWS, 128), F32),
                          )(dnw8, cs8, loss8, ddnw8, dwq8, dwk8, drb)


def _adam_math(w, g, m, v):
    c1 = 1.0 - ADAM_B1 ** ADAM_STEP
    c2 = 1.0 - ADAM_B2 ** ADAM_STEP
    mn = ADAM_B1 * m + (1.0 - ADAM_B1) * g
    vn = ADAM_B2 * v + (1.0 - ADAM_B2) * (g * g)
    return -ADAM_LR * ((mn / c1) / (jnp.sqrt(vn / c2) + ADAM_EPS) + ADAM_WD * w), mn, vn


def _adamw_small(gs, ws, ms, vs):
    n = len(SMALL_NAMES)

    def body(g_ref, *refs):
        w_refs, m_refs, v_refs = refs[:n], refs[n:2 * n], refs[2 * n:3 * n]
        outs, loss_ref = refs[3 * n:7 * n], refs[7 * n]
        loss = g_ref[0, LOSS_ROW:LOSS_ROW + 1, :]
        for s in range(1, gs.shape[0]):
            loss = loss + g_ref[s, LOSS_ROW:LOSS_ROW + 1, :]
        loss_ref[...] = loss

        def one(i, rows, lanes, at):
            g = g_ref[0, rows, lanes]
            for s in range(1, gs.shape[0]):
                g = g + g_ref[s, rows, lanes]
            d, mn, vn = _adam_math(w_refs[i][at], g, m_refs[i][at], v_refs[i][at])
            for kind, val in enumerate((g, d, mn, vn)):
                outs[kind * n + i][at] = val

        for k in range(D_MODEL // 128):
            one(0, slice(k, k + 1), slice(0, 128), (slice(0, 1), slice(k * 128, (k + 1) * 128)))
        for i, nm in enumerate(SMALL_NAMES[1:], start=1):
            rows, lanes = SMALL_AT[nm]
            one(i, rows, lanes, (slice(None), slice(None)))

    shapes = [jax.ShapeDtypeStruct(w.shape, F32) for w in ws]
    res = pl.pallas_call(body, name="adamw_small",
                         out_shape=shapes * 4 + [jax.ShapeDtypeStruct((1, 128), F32)])(gs, *ws, *ms, *vs)
    return [res[k * n:(k + 1) * n] for k in range(4)], res[4 * n]


def kernel(x, norm_w, w_in, conv_w, a_log, dt_bias, dn_norm_w, q_norm_w, k_norm_w, rel_bias, w_out, loss_target, m_norm_w, m_w_in, m_conv_w, m_a_log, m_dt_bias, m_dn_norm_w, m_q_norm_w, m_k_norm_w, m_rel_bias, m_w_out, v_norm_w, v_w_in, v_conv_w, v_a_log, v_dt_bias, v_dn_norm_w, v_q_norm_w, v_k_norm_w, v_rel_bias, v_w_out):
    assert w_in.shape[2] == SHARD_COLS
    bk = jnp.asarray(_bucket_tables())
    h, ht, bias, (win8, conv8) = _norm_and_gather(x[0], norm_w, [w_in[0].astype(MXU), conv_w[0]], rel_bias, bk)
    w_sect, w_sect_t = _build_w(win8)
    conv_full = conv8.transpose(1, 0, 2).reshape(4, 3 * D_DN)

    gr = _local_step(x[0], loss_target[0], h, ht, bias, bk, w_sect, conv_full, a_log, dt_bias, dn_norm_w, q_norm_w,
                     k_norm_w, w_out[0].astype(MXU))

    slabs = [_build_slabs(gr["w_in_sections"]),
             gr["w_out"].reshape(4, 2, D_MODEL // N_DEV, D_MODEL).transpose(1, 0, 2, 3),
             gr["conv_w"].reshape(4, 4, 2, 3 * D_DN // N_DEV).transpose(2, 1, 0, 3)]
    core = lax.axis_index("c").astype(jnp.int32).reshape(1)
    from_sibling = _swap_siblings(slabs)
    wires = (GRAD_WIRE, GRAD_WIRE, F32)
    partial = [_chip_sum(slabs[i], from_sibling[i], core, wires[i], "chip_sum_%d" % i) for i in range(3)]
    grad_x, small_pack, (r_win, r_wout, r_conv) = _finish_step(x[0], norm_w, w_sect_t, gr, partial)
    r_small = _share_small(small_pack)

    g_win, d_win, m_win, v_win = _adamw_sum(w_in[0], r_win, m_w_in[0], v_w_in[0], "adamw_w_in")
    g_wout, d_wout, m_wout, v_wout = _adamw_sum(w_out[0], r_wout, m_w_out[0], v_w_out[0], "adamw_w_out")
    g_conv, d_conv, m_conv, v_conv = _adamw_sum(conv_w[0], r_conv, m_conv_w[0], v_conv_w[0], "adamw_conv_w")
    small, loss_row = _adamw_small(r_small,
                                   (norm_w, a_log, dt_bias, dn_norm_w, q_norm_w, k_norm_w, rel_bias),
                                   (m_norm_w, m_a_log, m_dt_bias, m_dn_norm_w, m_q_norm_w, m_k_norm_w, m_rel_bias),
                                   (v_norm_w, v_a_log, v_dt_bias, v_dn_norm_w, v_q_norm_w, v_k_norm_w, v_rel_bias))

    loss = loss_row[0, 0]
    names = ("norm_w", "w_in", "conv_w", "a_log", "dt_bias", "dn_norm_w", "q_norm_w", "k_norm_w", "rel_bias", "w_out")
    big = dict(w_in=(g_win, d_win, m_win, v_win), conv_w=(g_conv, d_conv, m_conv, v_conv),
               w_out=(g_wout, d_wout, m_wout, v_wout))
    outs = [loss, grad_x[None]]
    for kind in range(4):
        for nm in names:
            outs.append(big[nm][kind][None] if nm in big else small[kind][SMALL_NAMES.index(nm)])
    return tuple(outs)
```

```python
import math

import numpy as np
import jax
import jax.numpy as jnp
from jax import lax
from jax.experimental import pallas as pl
from jax.experimental.pallas import tpu as pltpu

F32 = jnp.float32
MXU = jnp.bfloat16
GRAD_WIRE = jnp.bfloat16

D_MODEL = 1024
D_DN = 512
DN_HEADS = 4
DK = 128
CHUNK = 64
D_ATT = 512
ATT_HEADS = 8
HD = 64
PATTERNS = ((128, 1), (512, 4), (2048, 16))
BLK = 128
N_BUCKETS = 32
MAX_DISTANCE = 2048
EPS = 1e-6
W_COLS = 4224
N_DEV = 8
AXES = ("x", "y", "c")

ADAM_LR = 0.001
ADAM_B1 = 0.9
ADAM_B2 = 0.999
ADAM_EPS = 1e-08
ADAM_WD = 0.01
ADAM_STEP = 10

VMEM_LIMIT = 56 * 1024 * 1024
NEG = -1e30


def _dot(a, b):
    return jnp.dot(a.astype(MXU), b.astype(MXU), preferred_element_type=F32)


def _dot_nt(a, b):
    return lax.dot_general(a.astype(MXU), b.astype(MXU), (((1,), (1,)), ((), ())), preferred_element_type=F32)


def _dot_tn(a, b):
    return lax.dot_general(a.astype(MXU), b.astype(MXU), (((0,), (0,)), ((), ())), preferred_element_type=F32)


def _split(a):
    hi = a.astype(jnp.bfloat16)
    return hi, (a - hi.astype(F32)).astype(jnp.bfloat16)


def _dot_split(a, b, dims, exact):
    dg = lambda u, v: lax.dot_general(u, v, (dims, ((), ())), preferred_element_type=F32)
    if exact == "b":
        ah, al = _split(a)
        bh = b.astype(jnp.bfloat16)
        return dg(ah, bh) + dg(al, bh)
    if exact == "a":
        bh, bm = _split(b)
        bl = (b - bh.astype(F32) - bm.astype(F32)).astype(jnp.bfloat16)
        ah = a.astype(jnp.bfloat16)
        return dg(ah, bh) + (dg(ah, bm) + dg(ah, bl))
    ah, al = _split(a)
    bh, bl = _split(b)
    return dg(ah, bh) + (dg(ah, bl) + dg(al, bh))


def _wy_inverses(amat, eye):
    tinv = {x: eye - amat[x] for x in amat}
    pw = amat
    for _ in range(5):
        pw = {x: _hdot(pw[x], pw[x]) for x in amat}
        tinv = {x: tinv[x] + _hdot(tinv[x], pw[x]) for x in amat}
    return tinv


def _hdot(a, b, exact=None):
    return _dot_split(a, b, ((1,), (0,)), exact)


def _hdot_nt(a, b, exact=None):
    return _dot_split(a, b, ((1,), (1,)), exact)


def _hdot_tn(a, b, exact=None):
    return _dot_split(a, b, ((0,), (0,)), exact)


def _sigmoid(x):
    return 1.0 / (1.0 + jnp.exp(-x))


def _silu(x):
    return x * _sigmoid(x)


def _silu_and_grad(x):
    s = _sigmoid(x)
    return x * s, s * (1.0 + x * (1.0 - s))


def _softplus(x):
    return jnp.maximum(x, 0.0) + jnp.log(1.0 + jnp.exp(-jnp.abs(x)))


def _iota(shape, dim):
    return lax.broadcasted_iota(jnp.int32, shape, dim)


def _lane_col(x, k):
    return jnp.sum(jnp.where(_iota(x.shape, 1) == k, x, 0.0), axis=1, keepdims=True)


def _params(sem=None):
    return pltpu.CompilerParams(dimension_semantics=sem, vmem_limit_bytes=VMEM_LIMIT)


def _t5_bucket(dist):
    max_exact = N_BUCKETS // 2
    d = np.maximum(dist, 1).astype(np.float64)
    large = max_exact + (np.log(d / max_exact) / math.log(MAX_DISTANCE / max_exact)
                         * (N_BUCKETS - max_exact)).astype(np.int32)
    large = np.minimum(large, N_BUCKETS - 1)
    return np.where(dist < max_exact, dist, large).astype(np.int32)


def _bucket_tables():
    qi = np.arange(BLK)[:, None]
    kj = np.arange(2 * BLK)[None, :]
    step = qi - kj + BLK
    band = (step >= 0) & (step <= BLK)
    out = []
    for _, r in PATTERNS:
        b = _t5_bucket(np.clip(step, 0, None) * r)
        out.append(np.where(band, b, -1))
    return np.stack(out).astype(np.int32)


def _group_mats():
    g = np.zeros((D_ATT, 128), np.float32)
    for h in range(ATT_HEADS):
        g[h * HD:(h + 1) * HD, h] = 1.0
    return g, np.ascontiguousarray(g.T)


CHIP_FLIPS = ((1, 0), (0, 1), (1, 1))
ANY_SPEC = pl.BlockSpec(memory_space=pl.ANY)
MESH_ID = pl.DeviceIdType.MESH


def _other_chips():
    x, y = lax.axis_index("x"), lax.axis_index("y")
    return [((1 - x if fx else x), (1 - y if fy else y)) for fx, fy in CHIP_FLIPS]


def _gather_plan(ins, outs, send, recv, loc):
    n = len(ins)
    x, y, c = (lax.axis_index(a) for a in AXES)
    sib = (x, y, 1 - c)
    chips = _other_chips()
    lin = lambda px, py, pc: 4 * px + 2 * py + pc

    def copy(a, k, block, to, src=None):
        slot = outs[a].at[lin(*block)]
        return pltpu.make_async_remote_copy(src_ref=slot if src is None else src, dst_ref=slot,
                                            send_sem=send.at[a, k], recv_sem=recv.at[a, k],
                                            device_id=to, device_id_type=MESH_ID)

    mine = [pltpu.make_async_copy(ins[a], outs[a].at[lin(x, y, c)], loc.at[a]) for a in range(n)]
    firsts = []
    for a in range(n):
        firsts.append(copy(a, 0, (x, y, c), sib, src=ins[a]))
        firsts += [copy(a, 1 + j, (x, y, c), (*chip, c), src=ins[a]) for j, chip in enumerate(chips)]

    def begin():
        for cp in mine + firsts:
            cp.start()

    def finish():
        passed = []
        for j, chip in enumerate(chips):
            for a in range(n):
                copy(a, 1 + j, (*chip, c), (x, y, c)).wait_recv()
                fw = copy(a, 4 + j, (*chip, c), sib)
                fw.start()
                passed.append(fw)
        for a in range(n):
            copy(a, 0, sib, (x, y, c)).wait_recv()
            for j, chip in enumerate(chips):
                copy(a, 4 + j, (*chip, 1 - c), (x, y, c)).wait_recv()
        for cp in firsts + passed:
            cp.wait_send()
        for cp in mine:
            cp.wait()

    return begin, finish


GATHER_SEMS = lambda n: [pltpu.SemaphoreType.DMA((n, 7)), pltpu.SemaphoreType.DMA((n, 7)), pltpu.SemaphoreType.DMA((n,))]


def _swap_siblings(arrs):
    n = len(arrs)

    def body(*refs):
        ins, outs = refs[:n], refs[n:2 * n]
        send, recv = refs[2 * n:]
        x, y, c = (lax.axis_index(a) for a in AXES)
        cps = [pltpu.make_async_remote_copy(src_ref=ins[a].at[1 - c], dst_ref=outs[a], send_sem=send.at[a],
                                            recv_sem=recv.at[a], device_id=(x, y, 1 - c), device_id_type=MESH_ID)
               for a in range(n)]
        for cp in cps:
            cp.start()
        for cp in cps:
            cp.wait()

    return pl.pallas_call(
        body, name="swap_siblings", out_shape=[jax.ShapeDtypeStruct(a.shape[1:], a.dtype) for a in arrs],
        in_specs=[ANY_SPEC] * n, out_specs=[ANY_SPEC] * n,
        scratch_shapes=[pltpu.SemaphoreType.DMA((n,)), pltpu.SemaphoreType.DMA((n,))],
    )(*arrs)


def _chip_sum(mine2, theirs, core, wire, name):
    _, nchip, r, cdim = mine2.shape
    tr = r if r <= 1024 else 1024

    def body(core_ref, a_ref, b_ref, o_ref):
        del core_ref
        o_ref[...] = (a_ref[...].astype(F32) + b_ref[...].astype(F32)).astype(wire)

    grid_spec = pltpu.PrefetchScalarGridSpec(
        num_scalar_prefetch=1, grid=(nchip, r // tr),
        in_specs=[pl.BlockSpec((None, None, tr, cdim), lambda j, i, cr: (cr[0], j, i, 0)),
                  pl.BlockSpec((None, tr, cdim), lambda j, i, cr: (j, i, 0))],
        out_specs=pl.BlockSpec((None, tr, cdim), lambda j, i, cr: (j, i, 0)))
    return pl.pallas_call(
        body, name=name, grid_spec=grid_spec, out_shape=jax.ShapeDtypeStruct((nchip, r, cdim), wire),
        compiler_params=_params(("arbitrary", "arbitrary")),
    )(core, mine2, theirs)


def _chip_swap_copies(ins, outs, send, recv, loc):
    x, y, c = (lax.axis_index(a) for a in AXES)
    me = 2 * x + y
    starts, arrivals, drains = [], [], []
    for a in range(len(ins)):
        lc = pltpu.make_async_copy(ins[a].at[me], outs[a].at[me], loc.at[a])
        starts.append(lc.start)
        drains.append(lc.wait)
        for j, (px, py) in enumerate(_other_chips()):
            them = 2 * px + py
            cp = pltpu.make_async_remote_copy(src_ref=ins[a].at[them], dst_ref=outs[a].at[me], send_sem=send.at[a, j],
                                              recv_sem=recv.at[a, j], device_id=(px, py, c), device_id_type=MESH_ID)
            landing = pltpu.make_async_remote_copy(src_ref=ins[a].at[them], dst_ref=outs[a].at[them],
                                                   send_sem=send.at[a, j], recv_sem=recv.at[a, j],
                                                   device_id=(px, py, c), device_id_type=MESH_ID)
            starts.append(cp.start)
            arrivals.append(landing.wait_recv)
            drains.append(cp.wait_send)
    return starts, arrivals + drains


def _share_small(pack):
    def body(in_ref, out_ref, send, recv, loc):
        x, y, c = (lax.axis_index(a) for a in AXES)
        me = 4 * x + 2 * y + c
        lc = pltpu.make_async_copy(in_ref, out_ref.at[me], loc.at[0])
        lc.start()
        sends, arrivals = [], []
        for k in range(1, N_DEV):
            px = 1 - x if k & 4 else x
            py = 1 - y if k & 2 else y
            pc = 1 - c if k & 1 else c
            cp = pltpu.make_async_remote_copy(src_ref=in_ref, dst_ref=out_ref.at[me], send_sem=send.at[k - 1],
                                              recv_sem=recv.at[k - 1], device_id=(px, py, pc), device_id_type=MESH_ID)
            cp.start()
            sends.append(cp)
            arrivals.append(pltpu.make_async_remote_copy(src_ref=in_ref, dst_ref=out_ref.at[4 * px + 2 * py + pc],
                                                         send_sem=send.at[k - 1], recv_sem=recv.at[k - 1],
                                                         device_id=(px, py, pc), device_id_type=MESH_ID))
        for cp in arrivals:
            cp.wait_recv()
        for cp in sends:
            cp.wait_send()
        lc.wait()

    return pl.pallas_call(
        body, name="share_small", out_shape=jax.ShapeDtypeStruct((N_DEV,) + pack.shape, pack.dtype),
        in_specs=[ANY_SPEC], out_specs=ANY_SPEC,
        scratch_shapes=[pltpu.SemaphoreType.DMA((N_DEV - 1,)), pltpu.SemaphoreType.DMA((N_DEV - 1,)),
                        pltpu.SemaphoreType.DMA((1,))],
    )(pack)


W_PARTS = ((0, 0, 2048), (2048, 4096, 8), (2056, 2048, 2048))
SHARD_COLS = 513


def _pieces(lo, hi, parts):
    out = []
    for ref_start, tgt_start, width in parts:
        a, b = max(lo, ref_start), min(hi, ref_start + width)
        if a < b:
            out.append((a - lo, tgt_start + a - ref_start, b - a))
    return out


def _build_w(win8):
    tr = 512

    def body(in_ref, w_ref, wt_ref):
        w_ref[:, 4096:W_COLS] = jnp.zeros((tr, W_COLS - 4096), MXU)
        for p in range(N_DEV):
            for src, dst, width in _pieces(p * SHARD_COLS, (p + 1) * SHARD_COLS, W_PARTS):
                w_ref[:, dst:dst + width] = in_ref[p, :, src:src + width]
        for k in range(W_COLS // 128):
            wt_ref[k * 128:(k + 1) * 128, :] = w_ref[:, k * 128:(k + 1) * 128].astype(F32).T.astype(MXU)

    return pl.pallas_call(
        body, name="build_w", grid=(D_MODEL // tr,),
        in_specs=[pl.BlockSpec((N_DEV, tr, SHARD_COLS), lambda i: (0, i, 0))],
        out_specs=[pl.BlockSpec((tr, W_COLS), lambda i: (i, 0)), pl.BlockSpec((W_COLS, tr), lambda i: (0, i))],
        out_shape=[jax.ShapeDtypeStruct((D_MODEL, W_COLS), MXU), jax.ShapeDtypeStruct((W_COLS, D_MODEL), MXU)],
        compiler_params=_params(("arbitrary",)),
    )(win8)


def _build_slabs(secs):
    tr = 512
    parts = ((0, 0, 1536), (1536, 1, 512), (2048, 6, 8), (2056, 2, 512), (2568, 3, 512), (3080, 4, 512),
             (3592, 5, 512))

    def body(*refs):
        o_ref = refs[len(secs)]
        for p in range(N_DEV):
            lo, hi = p * SHARD_COLS, (p + 1) * SHARD_COLS
            for ref_start, idx, width in parts:
                a, b = max(lo, ref_start), min(hi, ref_start + width)
                if a < b:
                    o_ref[p % 2, p // 2, :, a - lo:b - lo] = refs[idx][:, a - ref_start:b - ref_start]

    return pl.pallas_call(
        body, name="build_slabs", grid=(D_MODEL // tr,),
        in_specs=[pl.BlockSpec((tr, s.shape[1]), lambda i: (i, 0)) for s in secs],
        out_specs=pl.BlockSpec((2, 4, tr, SHARD_COLS), lambda i: (0, 0, i, 0)),
        out_shape=jax.ShapeDtypeStruct((2, 4, D_MODEL, SHARD_COLS), secs[0].dtype),
        compiler_params=_params(("arbitrary",)),
    )(*secs)


def _fill_bias(rb_ref, bk_ref, bias_ref, pair):
    for p in range(len(PATTERNS)):
        bk_p = bk_ref[p]
        for hh in range(2):
            head = 2 * pair + hh
            bm = jnp.full((BLK, 2 * BLK), NEG, F32)
            for b in range(N_BUCKETS):
                bm = jnp.where(bk_p == b, rb_ref[head, b], bm)
            bias_ref[p, hh * BLK:(hh + 1) * BLK, :] = bm


def _norm_and_gather(x, nw, shards, rb, bk):
    t = x.shape[0]
    tm = 512
    nsteps = t // tm
    n = len(shards)
    npair = ATT_HEADS // 2
    assert nsteps >= npair

    def body(x_ref, nw_ref, *rest):
        ins, (rb_ref, bk_ref, h_ref, ht_ref, bias_ref) = rest[:n], rest[n:n + 5]
        outs, sems = rest[n + 5:2 * n + 5], rest[2 * n + 5:]
        begin, finish = _gather_plan(ins, outs, *sems)
        step = pl.program_id(0)

        @pl.when(step == 0)
        def _():
            begin()

        @pl.when(step == nsteps - 1)
        def _():
            finish()

        @pl.when(step < npair)
        def _():
            _fill_bias(rb_ref, bk_ref, bias_ref, step)

        xv = x_ref[...]
        rstd = lax.rsqrt(jnp.mean(xv * xv, axis=-1, keepdims=True) + EPS)
        hf = xv * rstd * nw_ref[...]
        h_ref[...] = hf.astype(MXU)
        ht_ref[...] = hf.T.astype(MXU)

    res = pl.pallas_call(
        body, name="norm_and_gather", grid=(nsteps,),
        in_specs=[pl.BlockSpec((tm, D_MODEL), lambda i: (i, 0)), pl.BlockSpec(nw.shape, lambda i: (0, 0))]
                 + [ANY_SPEC] * n
                 + [pl.BlockSpec(memory_space=pltpu.SMEM), pl.BlockSpec(bk.shape, lambda i: (0, 0, 0))],
        out_specs=[pl.BlockSpec((tm, D_MODEL), lambda i: (i, 0)), pl.BlockSpec((D_MODEL, tm), lambda i: (0, i)),
                   pl.BlockSpec((len(PATTERNS), None, 2 * BLK, 2 * BLK), lambda i: (0, jnp.minimum(i, npair - 1), 0, 0))]
                  + [ANY_SPEC] * n,
        out_shape=[jax.ShapeDtypeStruct((t, D_MODEL), MXU), jax.ShapeDtypeStruct((D_MODEL, t), MXU),
                   jax.ShapeDtypeStruct((len(PATTERNS), npair, 2 * BLK, 2 * BLK), F32)]
                  + [jax.ShapeDtypeStruct((N_DEV,) + a.shape, a.dtype) for a in shards],
        scratch_shapes=GATHER_SEMS(n),
        compiler_params=_params(("arbitrary",)),
    )(x, nw, *shards, rb, bk)
    return res[0], res[1], res[2], res[3:]


def _inproj(h_all, w, cw, wout_shard):
    t = h_all.shape[0]
    tm = 512
    nsteps = t // tm

    def body(h_ref, w_ref, cw_ref, wo_ref, pdn_ref, qkv_ref, z_ref, patt_ref, gate_ref, ba_ref,
             wo8_ref, halo_ref, send, recv, loc):
        begin, finish = _gather_plan([wo_ref], [wo8_ref], send, recv, loc)

        @pl.when(pl.program_id(0) == 0)
        def _():
            begin()
            halo_ref[...] = jnp.zeros_like(halo_ref)

        @pl.when(pl.program_id(0) == nsteps - 1)
        def _():
            finish()

        h = h_ref[...]
        for ref, lo, hi in ((z_ref, 1536, 2048), (patt_ref, 2048, 3584), (gate_ref, 3584, 4096), (ba_ref, 4096, 4224)):
            ref[...] = jnp.dot(h, w_ref[:, lo:hi], preferred_element_type=F32)
        pdn = jnp.dot(h, w_ref[:, 0:3 * D_DN], preferred_element_type=F32)
        pdn_ref[...] = pdn
        _dn_prep_tile(pdn, halo_ref, cw_ref, qkv_ref)

    row = lambda n: pl.BlockSpec((tm, n), lambda i: (i, 0))
    full = lambda a: pl.BlockSpec(a.shape, lambda i: (0,) * a.ndim)
    return pl.pallas_call(
        body, name="inproj", grid=(nsteps,),
        in_specs=[row(D_MODEL), full(w), full(cw), ANY_SPEC],
        out_specs=[row(1536), row(1536), row(512), row(1536), row(512), row(128), ANY_SPEC],
        out_shape=[jax.ShapeDtypeStruct((t, n), F32) for n in (1536, 1536, 512, 1536, 512, 128)] +
                  [jax.ShapeDtypeStruct((N_DEV,) + wout_shard.shape, wout_shard.dtype)],
        scratch_shapes=[pltpu.VMEM((8, 3 * D_DN), F32)] + GATHER_SEMS(1),
        compiler_params=_params(("arbitrary",)),
    )(h_all, w, cw, wout_shard)


CONV_ROWS = 512


def _conv_taps(u_ref, c, w_ref):
    r0 = c * CONV_ROWS
    if c == 0:
        ext = jnp.concatenate([jnp.zeros((8, 128), F32), u_ref[0:CONV_ROWS, :]], axis=0)
    else:
        ext = u_ref[r0 - 8:r0 + CONV_ROWS, :]
    taps = [ext[8:, :]] + [pltpu.roll(ext, k, 0)[8:, :] for k in (1, 2, 3)]
    y = taps[0] * w_ref[3:4, :]
    for k in (1, 2, 3):
        y = y + taps[k] * w_ref[3 - k:4 - k, :]
    return taps, y


def _dn_prep_tile(pdn, halo_ref, cw_ref, out_ref):
    rows = pdn.shape[0]
    ext = jnp.concatenate([halo_ref[...], pdn], axis=0)
    halo_ref[...] = pdn[rows - 8:, :]
    for j in range(3 * D_DN // 128):
        cols = slice(j * 128, (j + 1) * 128)
        e = ext[:, cols]
        y = e[8:, :] * cw_ref[3:4, cols]
        for k in (1, 2, 3):
            y = y + pltpu.roll(e, k, 0)[8:, :] * cw_ref[3 - k:4 - k, cols]
        a = _silu(y)
        if j < 2 * DN_HEADS:
            a = a * lax.rsqrt(jnp.sum(a * a, axis=1, keepdims=True) + EPS)
        if j < DN_HEADS:
            a = a * DK ** -0.5
        out_ref[:, cols] = a


def _chunk_common(qkv, ba, arow, dtb):
    c = CHUNK
    ri, ci = _iota((c, c), 0), _iota((c, c), 1)
    lane = _iota((c, 128), 1)
    g_all = jnp.where((lane >= DN_HEADS) & (lane < 2 * DN_HEADS), arow * _softplus(ba + dtb), 0.0)
    gc_all = _hdot((ri >= ci).astype(F32), g_all, "a")
    gc_t = gc_all.T
    beta_all = _sigmoid(ba)
    out = []
    for h in range(DN_HEADS):
        gc = _lane_col(gc_all, DN_HEADS + h)
        gcr = gc_t[DN_HEADS + h:DN_HEADS + h + 1, :]
        gl = gc[c - 1:c, :]
        out.append(dict(
            q=qkv[:, h * DK:(h + 1) * DK], k=qkv[:, D_DN + h * DK:D_DN + (h + 1) * DK],
            v=qkv[:, 2 * D_DN + h * DK:2 * D_DN + (h + 1) * DK],
            beta=_lane_col(beta_all, h), g=_lane_col(g_all, DN_HEADS + h),
            a_raw=_lane_col(ba, DN_HEADS + h), a_h=_lane_col(arow, DN_HEADS + h), dt_h=_lane_col(dtb, DN_HEADS + h),
            decay=jnp.exp(jnp.where(ri >= ci, gc - gcr, NEG)), eg=jnp.exp(gc), egl=jnp.exp(gl), etail=jnp.exp(gl - gc)))
    return out, ri, ci


SCAN_CHUNKS = 8


def _dn_scan_fwd(qkv, ba, z, arow, dtb, dnw):
    t = qkv.shape[0]
    n = t // CHUNK
    c = CHUNK
    cps = SCAN_CHUNKS
    hs = range(DN_HEADS)
    chains = [(j, h) for j in range(cps) for h in hs]

    def body(qkv_ref, ba_ref, z_ref, arow_ref, dtb_ref, dnw_ref, o_ref, y_ref, sh_ref, th_ref, s_ref):
        @pl.when(pl.program_id(0) == 0)
        def _():
            s_ref[...] = jnp.zeros_like(s_ref)

        ms = {}
        for j in range(cps):
            rows = slice(j * c, (j + 1) * c)
            mj, ri, ci = _chunk_common(qkv_ref[rows, :], ba_ref[rows, :], arow_ref[...], dtb_ref[...])
            for h in hs:
                ms[j, h] = mj[h]
        kb = {x: ms[x]["k"] * ms[x]["beta"] for x in chains}
        amat = {x: jnp.where(ri > ci, _dot_nt(kb[x], ms[x]["k"]) * ms[x]["decay"], 0.0) for x in chains}
        attn = {x: jnp.where(ri >= ci, _dot_nt(ms[x]["q"], ms[x]["k"]) * ms[x]["decay"], 0.0) for x in chains}
        tinv = _wy_inverses(amat, (ri == ci).astype(F32))
        uw = {x: _hdot(tinv[x], jnp.concatenate([ms[x]["v"] * ms[x]["beta"], kb[x] * ms[x]["eg"]], axis=1))
              for x in chains}
        u = {x: uw[x][:, :DK] for x in chains}
        w = {x: uw[x][:, DK:] for x in chains}
        q_dec = {x: ms[x]["q"] * ms[x]["eg"] for x in chains}
        k_tail = {x: ms[x]["k"] * ms[x]["etail"] for x in chains}
        s = [s_ref[h] for h in hs]
        for j in range(cps):
            rows = slice(j * c, (j + 1) * c)
            v_new = [u[j, h] - _dot(w[j, h], s[h]) for h in hs]
            o = [_dot(q_dec[j, h], s[h]) + _dot(attn[j, h], v_new[h]) for h in hs]
            for h in hs:
                sh_ref[j, h] = s[h]
                th_ref[j, h] = tinv[j, h]
            s = [s[h] * ms[j, h]["egl"] + _dot_tn(k_tail[j, h], v_new[h]) for h in hs]
            for h in hs:
                cols = slice(h * DK, (h + 1) * DK)
                o_ref[rows, cols] = o[h]
                rs = lax.rsqrt(jnp.mean(o[h] * o[h], axis=1, keepdims=True) + EPS)
                y_ref[rows, cols] = o[h] * rs * dnw_ref[...] * _silu(z_ref[rows, cols])
        for h in hs:
            s_ref[h] = s[h]

    row = lambda w_: pl.BlockSpec((cps * c, w_), lambda i: (i, 0))
    one = pl.BlockSpec((1, 128), lambda i: (0, 0))
    return pl.pallas_call(
        body, name="dn_scan_fwd", grid=(n // cps,),
        in_specs=[row(1536), row(128), row(512), one, one, one],
        out_specs=[row(512), row(512), pl.BlockSpec((cps, DN_HEADS, DK, DK), lambda i: (i, 0, 0, 0)),
                   pl.BlockSpec((cps, DN_HEADS, c, c), lambda i: (i, 0, 0, 0))],
        out_shape=[jax.ShapeDtypeStruct((t, 512), F32), jax.ShapeDtypeStruct((t, 512), F32),
                   jax.ShapeDtypeStruct((n, DN_HEADS, DK, DK), F32), jax.ShapeDtypeStruct((n, DN_HEADS, c, c), F32)],
        scratch_shapes=[pltpu.VMEM((DN_HEADS, DK, DK), F32)],
        compiler_params=_params(("arbitrary",)),
    )(qkv, ba, z, arow, dtb, dnw)


ATT_ROWS = 512


def _pair_rstd(xv, g2_ref):
    return lax.rsqrt(_hdot(xv * xv, g2_ref[...], "b") * (1.0 / HD) + EPS)


def _pair_norm(t, raw_refs, w_refs, out_refs, g2_ref):
    for c in range(t // ATT_ROWS):
        sl = slice(c * ATT_ROWS, (c + 1) * ATT_ROWS)
        for raw, w_ref, out in zip(raw_refs, w_refs, out_refs):
            xv = raw[sl, :]
            out[sl, :] = xv * _pair_rstd(xv, g2_ref) * w_ref[...]


BIAS_SPEC = pl.BlockSpec((len(PATTERNS), None, 2 * BLK, 2 * BLK), lambda i: (0, i, 0, 0))
PAIR_ROW_SPEC = pl.BlockSpec((1, 128), lambda i: (0, i))


def _stack_heads(xb, h0):
    return jnp.concatenate([jnp.where(h0, xb, 0.0), jnp.where(h0, 0.0, xb)], axis=0).astype(MXU)


def _block_rows(t, r, n):
    per_class = (t // r) // BLK
    res = n // per_class
    j = n % per_class
    start = res + BLK * r * j
    pstart = res + BLK * r * jnp.maximum(j - 1, 0)
    if r == 1:
        return pl.ds(pl.multiple_of(start, BLK), BLK), pl.ds(pl.multiple_of(pstart, BLK), BLK), j
    return pl.ds(start, BLK, stride=r), pl.ds(pstart, BLK, stride=r), j


def _att_fwd(qkv, gate, bias, wq, wk, g2):
    t = qkv.shape[0]
    rows = ATT_ROWS

    def body(bias_ref, qraw_ref, kraw_ref, v_ref, g_ref, wq_ref, wk_ref, g2_ref, o_ref, y_ref, lse_ref,
             o0_ref, o1_ref, o2_ref, l0_ref, l1_ref, l2_ref, q_ref, k_ref):
        h0 = _iota((BLK, 128), 1) < HD
        prev_cols = _iota((2 * BLK, 2 * BLK), 1) < BLK
        op_refs, lp_refs = (o0_ref, o1_ref, o2_ref), (l0_ref, l1_ref, l2_ref)
        _pair_norm(t, (qraw_ref, kraw_ref), (wq_ref, wk_ref), (q_ref, k_ref), g2_ref)

        for p, (_, r) in enumerate(PATTERNS):
            def blk(n, carry, p=p, r=r):
                cur, prev, j = _block_rows(t, r, n)
                q2 = _stack_heads(q_ref[cur, :], h0)
                k2 = jnp.concatenate([k_ref[prev, :], k_ref[cur, :]], axis=0).astype(MXU)
                v2 = jnp.concatenate([v_ref[prev, :], v_ref[cur, :]], axis=0).astype(MXU)
                s = _dot_nt(q2, k2) + bias_ref[p] + jnp.where(prev_cols & (j == 0), NEG, 0.0)
                m = jnp.max(s, axis=1, keepdims=True)
                e = jnp.exp(s - m)
                l = jnp.sum(e, axis=1, keepdims=True)
                pv = _dot(e, v2) / l
                lse = m + jnp.log(l)
                op_refs[p][cur, :] = jnp.where(h0, pv[:BLK], pv[BLK:])
                lp_refs[p][cur, :] = jnp.where(h0, lse[:BLK], lse[BLK:])
                return carry

            lax.fori_loop(0, t // BLK, blk, 0, unroll=16)

        for c in range(t // rows):
            sl = slice(c * rows, (c + 1) * rows)
            ls = [ref[sl, :] for ref in lp_refs]
            mx = jnp.maximum(jnp.maximum(ls[0], ls[1]), ls[2])
            ws = [jnp.exp(v_ - mx) for v_ in ls]
            den = ws[0] + ws[1] + ws[2]
            o = (ws[0] * o0_ref[sl, :] + ws[1] * o1_ref[sl, :] + ws[2] * o2_ref[sl, :]) / den
            o_ref[sl, :] = o
            y_ref[sl, :] = o * _silu(g_ref[sl, :])
            lse_ref[sl, :] = mx + jnp.log(den)

    col = lambda off: pl.BlockSpec((t, 128), lambda i, off=off: (0, off + i))
    return pl.pallas_call(
        body, name="att_fwd", grid=(ATT_HEADS // 2,),
        in_specs=[BIAS_SPEC, col(0), col(4), col(8), col(0), PAIR_ROW_SPEC, PAIR_ROW_SPEC,
                  pl.BlockSpec((128, 128), lambda i: (0, 0))],
        out_specs=[col(0), col(0), col(0)],
        out_shape=[jax.ShapeDtypeStruct((t, 512), F32)] * 3,
        scratch_shapes=[pltpu.VMEM((t, 128), F32)] * 8,
        compiler_params=_params(("arbitrary",)),
    )(bias, qkv, qkv, qkv, gate, wq, wk, g2)


def _outproj_loss(x, ydn, yatt, wout, target):
    t = x.shape[0]
    tm = 512

    def body(x_ref, a_ref, b_ref, w_ref, t_ref, dy_ref, mix_ref, loss_ref):
        @pl.when(pl.program_id(0) == 0)
        def _():
            loss_ref[...] = jnp.zeros_like(loss_ref)

        mixf = jnp.concatenate([a_ref[...], b_ref[...]], axis=1)
        mix_ref[...] = mixf.T.astype(MXU)
        err = x_ref[...] + jnp.dot(mixf.astype(MXU), w_ref[...], preferred_element_type=F32) - t_ref[...]
        dy_ref[...] = err * (1.0 / D_MODEL)
        loss_ref[...] += jnp.sum(err * err) * (0.5 / D_MODEL)

    row = lambda n: pl.BlockSpec((tm, n), lambda i: (i, 0))
    return pl.pallas_call(
        body, name="outproj_loss", grid=(t // tm,),
        in_specs=[row(D_MODEL), row(512), row(512), pl.BlockSpec(wout.shape, lambda i: (0, 0)), row(D_MODEL)],
        out_specs=[row(D_MODEL), pl.BlockSpec((D_MODEL, tm), lambda i: (0, i)), pl.BlockSpec((8, 128), lambda i: (0, 0))],
        out_shape=[jax.ShapeDtypeStruct((t, D_MODEL), F32), jax.ShapeDtypeStruct((D_MODEL, t), MXU),
                   jax.ShapeDtypeStruct((8, 128), F32)],
        compiler_params=_params(("arbitrary",)),
    )(x, ydn, yatt, wout, target)


def _outproj_bwd(dy, wout_t, oraw, z, dnw, oatt, gate, g, gt):
    t = dy.shape[0]
    tm = 512

    def body(dy_ref, w_ref, o_ref, z_ref, dnw_ref, oa_ref, g_ref, grp_ref, grpt_ref,
             do_ref, dz_ref, doa_ref, dg_ref, dd_ref, ddnw_ref):
        @pl.when(pl.program_id(0) == 0)
        def _():
            ddnw_ref[...] = jnp.zeros_like(ddnw_ref)

        dmix = jnp.dot(dy_ref[...].astype(MXU), w_ref[...], preferred_element_type=F32)
        dnw_v = dnw_ref[...]
        acc = jnp.zeros((1, DK), F32)
        for h in range(DN_HEADS):
            sl = slice(h * DK, (h + 1) * DK)
            o, zz, dm = o_ref[:, sl], z_ref[:, sl], dmix[:, sl]
            rs = lax.rsqrt(jnp.mean(o * o, axis=1, keepdims=True) + EPS)
            oh = o * rs
            silu_z, dsilu_z = _silu_and_grad(zz)
            dz_ref[:, sl] = (dm * oh * dnw_v * dsilu_z).astype(MXU)
            d_on = dm * silu_z
            gg = d_on * dnw_v
            do_ref[:, sl] = rs * (gg - oh * jnp.mean(gg * oh, axis=1, keepdims=True))
            acc = acc + jnp.sum(d_on * oh, axis=0, keepdims=True)
        ddnw_ref[...] += jnp.broadcast_to(acc, (8, DK))
        da, gate_v, oa = dmix[:, 512:], g_ref[...], oa_ref[...]
        silu_g, dsilu_g = _silu_and_grad(gate_v)
        doa = da * silu_g
        doa_ref[...] = doa
        dg_ref[...] = (da * oa * dsilu_g).astype(MXU)
        dd_ref[...] = _hdot(_hdot(doa * oa, grp_ref[...], "b"), grpt_ref[...], "b")

    row = lambda n: pl.BlockSpec((tm, n), lambda i: (i, 0))
    full = lambda a: pl.BlockSpec(a.shape, lambda i: (0,) * a.ndim)
    return pl.pallas_call(
        body, name="outproj_bwd", grid=(t // tm,),
        in_specs=[row(D_MODEL), full(wout_t), row(512), row(512), full(dnw), row(512), row(512), full(g), full(gt)],
        out_specs=[row(512)] * 5 + [pl.BlockSpec((8, DK), lambda i: (0, 0))],
        out_shape=[jax.ShapeDtypeStruct((t, 512), dt_) for dt_ in (F32, MXU, F32, MXU, F32)]
                  + [jax.ShapeDtypeStruct((8, DK), F32)],
        compiler_params=_params(("arbitrary",)),
    )(dy, wout_t, oraw, z, dnw, oatt, gate, g, gt)


def _grad_matmul(at, b, name):
    m, t = at.shape
    n = b.shape[1]
    tk = 1024
    tn = n if n <= 1536 else 512
    nk = t // tk

    def body(a_ref, b_ref, o_ref, acc_ref):
        k = pl.program_id(1)

        @pl.when(k == 0)
        def _():
            acc_ref[...] = jnp.zeros_like(acc_ref)

        acc_ref[...] += jnp.dot(a_ref[...], b_ref[...].astype(MXU), preferred_element_type=F32)

        @pl.when(k == nk - 1)
        def _():
            o_ref[...] = acc_ref[...].astype(GRAD_WIRE)

    return pl.pallas_call(
        body, name=name, grid=(n // tn, nk),
        in_specs=[pl.BlockSpec((m, tk), lambda j, k: (0, k)), pl.BlockSpec((tk, tn), lambda j, k: (k, j))],
        out_specs=pl.BlockSpec((m, tn), lambda j, k: (0, j)),
        out_shape=jax.ShapeDtypeStruct((m, n), GRAD_WIRE),
        scratch_shapes=[pltpu.VMEM((m, tn), F32)],
        compiler_params=_params(("arbitrary", "arbitrary")),
    )(at, b)


def _grad_matmul_many(at, bs, name):
    m, t = at.shape
    n = bs[0].shape[1]
    nb = len(bs)
    tk = 1024
    nk = t // tk

    def body(a_ref, *refs):
        b_refs, o_refs, acc_ref = refs[:nb], refs[nb:2 * nb], refs[2 * nb]
        s, k = pl.program_id(0), pl.program_id(1)

        @pl.when(k == 0)
        def _():
            acc_ref[...] = jnp.zeros_like(acc_ref)

        for i in range(nb):
            @pl.when(s == i)
            def _(i=i):
                acc_ref[...] += jnp.dot(a_ref[...], b_refs[i][...].astype(MXU), preferred_element_type=F32)

                @pl.when(k == nk - 1)
                def _():
                    o_refs[i][...] = acc_ref[...].astype(GRAD_WIRE)

    def b_spec(i):
        return pl.BlockSpec((tk, n), lambda s, k: (jnp.where(s == i, k, jnp.where(s < i, 0, nk - 1)), 0))

    return pl.pallas_call(
        body, name=name, grid=(nb, nk),
        in_specs=[pl.BlockSpec((m, tk), lambda s, k: (0, k))] + [b_spec(i) for i in range(nb)],
        out_specs=[pl.BlockSpec((m, n), lambda s, k: (0, 0))] * nb,
        out_shape=[jax.ShapeDtypeStruct((m, n), GRAD_WIRE)] * nb,
        scratch_shapes=[pltpu.VMEM((m, n), F32)],
        compiler_params=_params(("arbitrary", "arbitrary")),
    )(at, *bs)


def _att_bwd(qkv, do, lse, dd, bias, bk, wq, wk, g2):
    t = qkv.shape[0]
    rows = ATT_ROWS

    def body(bias_ref, bk_ref, qraw_ref, kraw_ref, v_ref, do_ref, lse_ref, dd_ref, wq_ref, wk_ref, g2_ref,
             dq_ref, dk_ref, dv_ref, db_ref, dwq_ref, dwk_ref, ds_ref, q_ref, k_ref):
        pair = pl.program_id(0)

        @pl.when(pair == 0)
        def _():
            db_ref[...] = jnp.zeros_like(db_ref)

        _pair_norm(t, (qraw_ref, kraw_ref), (wq_ref, wk_ref), (q_ref, k_ref), g2_ref)
        ds_ref[...] = jnp.zeros_like(ds_ref)
        for c in range(t // rows):
            sl = slice(c * rows, (c + 1) * rows)
            for ref in (dq_ref, dk_ref, dv_ref):
                ref[sl, :] = jnp.zeros((rows, 128), F32)
        h0 = _iota((BLK, 128), 1) < HD
        prev_cols = _iota((2 * BLK, 2 * BLK), 1) < BLK

        def rows_of(xb):
            return jnp.concatenate([xb[:, 0:1], xb[:, HD:HD + 1]], axis=0)

        for p, (_, r) in enumerate(PATTERNS):
            def blk(n, carry, p=p, r=r):
                cur, prev, j = _block_rows(t, r, n)
                q2, do2 = _stack_heads(q_ref[cur, :], h0), _stack_heads(do_ref[cur, :], h0)
                k2 = jnp.concatenate([k_ref[prev, :], k_ref[cur, :]], axis=0).astype(MXU)
                v2 = jnp.concatenate([v_ref[prev, :], v_ref[cur, :]], axis=0).astype(MXU)
                s = _dot_nt(q2, k2) + bias_ref[p] + jnp.where(prev_cols & (j == 0), NEG, 0.0)
                prob = jnp.exp(s - rows_of(lse_ref[cur, :]))
                ds = prob * (_dot_nt(do2, v2) - rows_of(dd_ref[cur, :]))
                ds_ref[p] += ds
                dq2 = _dot(ds, k2)
                dk2 = _dot_tn(ds, q2)
                dv2 = _dot_tn(prob, do2)
                dq_ref[cur, :] += jnp.where(h0, dq2[:BLK], dq2[BLK:])
                dk_ref[prev, :] += dk2[:BLK]
                dv_ref[prev, :] += dv2[:BLK]
                dk_ref[cur, :] += dk2[BLK:]
                dv_ref[cur, :] += dv2[BLK:]
                return carry

            lax.fori_loop(0, t // BLK, blk, 0, unroll=8)

        ri, ci = _iota((8, 128), 0), _iota((8, 128), 1)
        upd = jnp.zeros((8, 128), F32)
        for p in range(len(PATTERNS)):
            bk = bk_ref[p]
            for hh in range(2):
                dsum = ds_ref[p, hh * BLK:(hh + 1) * BLK, :]
                for b in range(N_BUCKETS):
                    val = jnp.sum(jnp.where(bk == b, dsum, 0.0))
                    upd = upd + jnp.where((ri == 2 * pair + hh) & (ci == b), val, 0.0)
        db_ref[...] += upd

        for raw, d_ref, w_ref, dw_ref in ((qraw_ref, dq_ref, wq_ref, dwq_ref), (kraw_ref, dk_ref, wk_ref, dwk_ref)):
            acc = jnp.zeros((1, 128), F32)
            for c in range(t // rows):
                sl = slice(c * rows, (c + 1) * rows)
                xv, dyv = raw[sl, :], d_ref[sl, :]
                rs = _pair_rstd(xv, g2_ref)
                xh = xv * rs
                gg = dyv * w_ref[...]
                mean = _hdot(gg * xh, g2_ref[...], "b") * (1.0 / HD)
                d_ref[sl, :] = rs * (gg - xh * mean)
                acc = acc + jnp.sum(dyv * xh, axis=0, keepdims=True)
            dw_ref[...] = jnp.broadcast_to(acc, (8, 128))

    col = lambda off: pl.BlockSpec((t, 128), lambda i, off=off: (0, off + i))
    acc8 = pl.BlockSpec((8, 128), lambda i: (0, i))
    return pl.pallas_call(
        body, name="att_bwd", grid=(ATT_HEADS // 2,),
        in_specs=[BIAS_SPEC, pl.BlockSpec(bk.shape, lambda i: (0, 0, 0)),
                  col(0), col(4), col(8), col(0), col(0), col(0), PAIR_ROW_SPEC, PAIR_ROW_SPEC,
                  pl.BlockSpec((128, 128), lambda i: (0, 0))],
        out_specs=[col(0), col(0), col(0), pl.BlockSpec((8, 128), lambda i: (0, 0)), acc8, acc8],
        out_shape=[jax.ShapeDtypeStruct((t, 512), F32)] * 3 + [jax.ShapeDtypeStruct((8, 128), F32)]
                  + [jax.ShapeDtypeStruct((8, 512), F32)] * 2,
        scratch_shapes=[pltpu.VMEM((len(PATTERNS), 2 * BLK, 2 * BLK), F32)] + [pltpu.VMEM((t, 128), F32)] * 2,
        compiler_params=_params(("arbitrary",)),
    )(bias, bk, qkv, qkv, qkv, do, lse, dd, wq, wk, g2)


def _dn_scan_bwd(qkv, ba, do, sh, th, arow, dtb):
    t = qkv.shape[0]
    n = t // CHUNK
    c = CHUNK
    cps = SCAN_CHUNKS

    def body(qkv_ref, ba_ref, do_ref, sh_ref, th_ref, arow_ref, dtb_ref, dqkv_ref, dba_ref, ds_ref):
        @pl.when(pl.program_id(0) == 0)
        def _():
            ds_ref[...] = jnp.zeros_like(ds_ref)

        hs = range(DN_HEADS)
        chains = [(j, h) for j in range(cps) for h in hs]
        lane = _iota((c, 128), 1)
        row = _iota((c, 1), 0)
        ms = {}
        for j in range(cps):
            rows_j = slice(j * c, (j + 1) * c)
            mj, ri, ci = _chunk_common(qkv_ref[rows_j, :], ba_ref[rows_j, :], arow_ref[...], dtb_ref[...])
            for h in hs:
                ms[j, h] = mj[h]
        q, k, v = ({x: ms[x][nm] for x in chains} for nm in ("q", "k", "v"))
        beta, decay = ({x: ms[x][nm] for x in chains} for nm in ("beta", "decay"))
        eg, egl, etail = ({x: ms[x][nm] for x in chains} for nm in ("eg", "egl", "etail"))
        s = {x: sh_ref[x[0], x[1]] for x in chains}
        tinv = {x: th_ref[x[0], x[1]] for x in chains}
        d_o = {(j, h): do_ref[j * c:(j + 1) * c, h * DK:(h + 1) * DK] for j, h in chains}
        kb = {x: k[x] * beta[x] for x in chains}
        vb = {x: v[x] * beta[x] for x in chains}
        kbg = {x: kb[x] * eg[x] for x in chains}
        amat = {x: jnp.where(ri > ci, _dot_nt(kb[x], k[x]) * decay[x], 0.0) for x in chains}
        attn = {x: jnp.where(ri >= ci, _dot_nt(q[x], k[x]) * decay[x], 0.0) for x in chains}
        uw = {x: _hdot(tinv[x], jnp.concatenate([vb[x], kbg[x]], axis=1)) for x in chains}
        u = {x: uw[x][:, :DK] for x in chains}
        w = {x: uw[x][:, DK:] for x in chains}
        v_new = {x: u[x] - _dot(w[x], s[x]) for x in chains}
        q_dec = {x: q[x] * eg[x] for x in chains}
        k_tail = {x: k[x] * etail[x] for x in chains}
        d_attn = {x: jnp.where(ri >= ci, _dot_nt(d_o[x], v_new[x]), 0.0) for x in chains}
        d_qdec = {x: _dot_nt(d_o[x], s[x]) for x in chains}
        from_o = {x: _dot_tn(attn[x], d_o[x]) for x in chains}
        to_state = {x: _dot_tn(q_dec[x], d_o[x]) for x in chains}

        d_s, d_vnew = {}, {}
        cur = [ds_ref[h] for h in hs]
        for j in reversed(range(cps)):
            for h in hs:
                d_s[j, h] = cur[h]
                d_vnew[j, h] = from_o[j, h] + _dot(k_tail[j, h], cur[h])
            cur = [to_state[j, h] + cur[h] * egl[j, h] - _dot_tn(w[j, h], d_vnew[j, h]) for h in hs]
        for h in hs:
            ds_ref[h] = cur[h]

        d_ktail = {x: _dot_nt(v_new[x], d_s[x]) for x in chains}
        d_gl = {x: jnp.sum(s[x] * d_s[x]) * egl[x] for x in chains}
        d_w = {x: -_dot_nt(d_vnew[x], s[x]) for x in chains}
        d_both = {x: _hdot_tn(tinv[x], jnp.concatenate([d_vnew[x], d_w[x]], axis=1)) for x in chains}
        d_vb = {x: d_both[x][:, :DK] for x in chains}
        d_kbg = {x: d_both[x][:, DK:] for x in chains}
        d_a = {x: -jnp.where(ri > ci, _hdot_nt(d_both[x], uw[x]), 0.0) for x in chains}
        d_qk = {x: d_attn[x] * decay[x] for x in chains}
        d_kk = {x: d_a[x] * decay[x] for x in chains}
        d_kb = {x: _dot(d_kk[x], k[x]) + d_kbg[x] * eg[x] for x in chains}
        d_q = {x: _dot(d_qk[x], k[x]) + d_qdec[x] * eg[x] for x in chains}
        d_k = {x: _dot_tn(d_qk[x], q[x]) + _dot_tn(d_kk[x], kb[x]) + d_ktail[x] * etail[x] + d_kb[x] * beta[x]
               for x in chains}
        d_beta = {x: jnp.sum(d_kb[x] * k[x] + d_vb[x] * v[x], axis=1, keepdims=True) for x in chains}
        mm = {x: d_a[x] * amat[x] + d_attn[x] * attn[x] for x in chains}
        for j in range(cps):
            rows_j = slice(j * c, (j + 1) * c)
            rows = jnp.zeros((c, c), F32)
            for h in hs:
                rows = rows + jnp.where(ri == h, jnp.sum(mm[j, h], axis=0, keepdims=True), 0.0)
            cols_t = jnp.concatenate([rows, jnp.zeros((c, c), F32)], axis=1).T[:c, :]
            d_gc_all = jnp.zeros((c, 128), F32)
            for h in hs:
                x = (j, h)
                tail_term = jnp.sum(d_ktail[x] * k_tail[x], axis=1, keepdims=True)
                d_gc = (jnp.sum(mm[x], axis=1, keepdims=True) - _lane_col(cols_t, h)
                        + jnp.sum(d_qdec[x] * q_dec[x] + d_kbg[x] * kbg[x], axis=1, keepdims=True) - tail_term)
                d_gc = d_gc + jnp.where(row == c - 1, jnp.sum(tail_term) + d_gl[x], 0.0)
                d_gc_all = d_gc_all + jnp.where(lane == DN_HEADS + h, d_gc, 0.0)
            d_g_all = _hdot((ri <= ci).astype(F32), d_gc_all, "a")
            dba = jnp.zeros((c, 128), F32)
            for h in hs:
                x = (j, h)
                d_g = _lane_col(d_g_all, DN_HEADS + h)
                d_braw = d_beta[x] * beta[x] * (1.0 - beta[x])
                d_araw = d_g * ms[x]["a_h"] * _sigmoid(ms[x]["a_raw"] + ms[x]["dt_h"])
                dba = dba + jnp.where(lane == h, d_braw, 0.0) + jnp.where(lane == DN_HEADS + h, d_araw, 0.0) \
                    + jnp.where(lane == 2 * DN_HEADS + h, d_g * ms[x]["g"], 0.0)
                dqkv_ref[rows_j, h * DK:(h + 1) * DK] = d_q[x]
                dqkv_ref[rows_j, D_DN + h * DK:D_DN + (h + 1) * DK] = d_k[x]
                dqkv_ref[rows_j, 2 * D_DN + h * DK:2 * D_DN + (h + 1) * DK] = d_vb[x] * beta[x]
            dba_ref[rows_j, :] = dba

    nsteps = n // cps
    rev = lambda w_: pl.BlockSpec((cps * c, w_), lambda i: (nsteps - 1 - i, 0))
    one = pl.BlockSpec((1, 128), lambda i: (0, 0))
    return pl.pallas_call(
        body, name="dn_scan_bwd", grid=(nsteps,),
        in_specs=[rev(1536), rev(128), rev(512),
                  pl.BlockSpec((cps, DN_HEADS, DK, DK), lambda i: (nsteps - 1 - i, 0, 0, 0)),
                  pl.BlockSpec((cps, DN_HEADS, c, c), lambda i: (nsteps - 1 - i, 0, 0, 0)), one, one],
        out_specs=[rev(1536), rev(128)],
        out_shape=[jax.ShapeDtypeStruct((t, 1536), F32), jax.ShapeDtypeStruct((t, 128), F32)],
        scratch_shapes=[pltpu.VMEM((DN_HEADS, DK, DK), F32)],
        compiler_params=_params(("arbitrary",)),
    )(qkv, ba, do, sh, th, arow, dtb)


def _dn_prep_bwd(pdn, cw, dact):
    t = pdn.shape[0]
    nchunk = t // CONV_ROWS

    def body(u_ref, w_ref, d_ref, du_ref, dw_ref, dy_ref):
        j = pl.program_id(0)
        dy_ref[t:t + 8, :] = jnp.zeros((8, 128), F32)
        dw = [jnp.zeros((1, 128), F32) for _ in range(4)]
        for c in range(nchunk):
            sl = slice(c * CONV_ROWS, (c + 1) * CONV_ROWS)
            taps, y = _conv_taps(u_ref, c, w_ref)
            a, da_dy = _silu_and_grad(y)
            dout = d_ref[sl, :]
            rs = lax.rsqrt(jnp.sum(a * a, axis=1, keepdims=True) + EPS)
            f = jnp.where(j < 8, rs, 1.0) * jnp.where(j < 4, DK ** -0.5, 1.0)
            corr = jnp.where(j < 8, f * rs * rs * jnp.sum(dout * a, axis=1, keepdims=True), 0.0)
            dy = (f * dout - corr * a) * da_dy
            dy_ref[sl, :] = dy
            for k_ in range(4):
                dw[3 - k_] = dw[3 - k_] + jnp.sum(taps[k_] * dy, axis=0, keepdims=True)
        for i in range(4):
            dw_ref[i:i + 1, :] = dw[i]
        for c in range(nchunk):
            r0 = c * CONV_ROWS
            ext = dy_ref[r0:r0 + CONV_ROWS + 8, :]
            du = ext[:CONV_ROWS, :] * w_ref[3:4, :]
            for k_ in (1, 2, 3):
                du = du + pltpu.roll(ext, CONV_ROWS + 8 - k_, 0)[:CONV_ROWS, :] * w_ref[3 - k_:4 - k_, :]
            du_ref[r0:r0 + CONV_ROWS, :] = du.astype(MXU)

    return pl.pallas_call(
        body, name="dn_prep_bwd", grid=(12,),
        in_specs=[pl.BlockSpec((t, 128), lambda j: (0, j)), pl.BlockSpec((4, 128), lambda j: (0, j)),
                  pl.BlockSpec((t, 128), lambda j: (0, j))],
        out_specs=[pl.BlockSpec((t, 128), lambda j: (0, j)), pl.BlockSpec((4, 128), lambda j: (0, j))],
        out_shape=[jax.ShapeDtypeStruct((t, 1536), MXU), jax.ShapeDtypeStruct((4, 1536), F32)],
        scratch_shapes=[pltpu.VMEM((t + 8, 128), F32)],
        compiler_params=_params(("arbitrary",)),
    )(pdn, cw, dact)


SECTIONS = (("dn", 0, 1536), ("z", 1536, 512), ("q", 2048, 512), ("k", 2560, 512), ("v", 3072, 512),
            ("gate", 3584, 512), ("ba", 4096, 128))


def _inproj_bwd(x, nw, wt, dy, dsecs, partials):
    t = x.shape[0]
    tm = 256
    npart = len(partials)
    nsteps = t // tm

    nsec = len(SECTIONS)

    def body(x_ref, nw_ref, w_ref, dy_ref, *rest):
        sec_refs, rest = rest[:nsec], rest[nsec:]
        part_refs, (gx_ref, dnw_ref, cs_ref) = rest[:npart], rest[npart:npart + 3]
        got_refs, (send, recv, loc) = rest[npart + 3:2 * npart + 3], rest[2 * npart + 3:]
        starts, waits = _chip_swap_copies(part_refs, got_refs, send, recv, loc)

        @pl.when(pl.program_id(0) == 0)
        def _():
            for start in starts:
                start()
            dnw_ref[...] = jnp.zeros_like(dnw_ref)
            cs_ref[...] = jnp.zeros_like(cs_ref)

        @pl.when(pl.program_id(0) == nsteps - 1)
        def _():
            for wait in waits:
                wait()

        dh = jnp.zeros((tm, D_MODEL), F32)
        for ref, (_, lo, width) in zip(sec_refs, SECTIONS):
            dh = dh + jnp.dot(ref[...].astype(MXU), w_ref[lo:lo + width, :], preferred_element_type=F32)
        xv = x_ref[...]
        rstd = lax.rsqrt(jnp.mean(xv * xv, axis=-1, keepdims=True) + EPS)
        xh = xv * rstd
        gg = dh * nw_ref[...]
        gx_ref[...] = rstd * (gg - xh * jnp.mean(gg * xh, axis=-1, keepdims=True)) + dy_ref[...]
        dnw_ref[...] += jnp.broadcast_to(jnp.sum(dh * xh, axis=0, keepdims=True), (8, D_MODEL))
        cs_ref[...] += jnp.broadcast_to(jnp.sum(sec_refs[nsec - 1][...], axis=0, keepdims=True), (8, 128))

    row = lambda n: pl.BlockSpec((tm, n), lambda i: (i, 0))
    full = lambda a: pl.BlockSpec(a.shape, lambda i: (0,) * a.ndim)
    res = pl.pallas_call(
        body, name="inproj_bwd", grid=(nsteps,),
        in_specs=[row(D_MODEL), full(nw), full(wt), row(D_MODEL)] + [row(width) for _, _, width in SECTIONS]
                 + [ANY_SPEC] * npart,
        out_specs=[row(D_MODEL), pl.BlockSpec((8, D_MODEL), lambda i: (0, 0)), pl.BlockSpec((8, 128), lambda i: (0, 0))]
                  + [ANY_SPEC] * npart,
        out_shape=[jax.ShapeDtypeStruct((t, D_MODEL), F32), jax.ShapeDtypeStruct((8, D_MODEL), F32),
                   jax.ShapeDtypeStruct((8, 128), F32)] + [jax.ShapeDtypeStruct(p.shape, p.dtype) for p in partials],
        scratch_shapes=[pltpu.SemaphoreType.DMA((npart, 3)), pltpu.SemaphoreType.DMA((npart, 3)),
                        pltpu.SemaphoreType.DMA((npart,))],
        compiler_params=_params(("arbitrary",)),
    )(x, nw, wt, dy, *dsecs, *partials)
    return res[0], res[1], res[2], res[3:]


def _adamw_sum(w, gs, m, v, name):
    r, c = w.shape
    nsum = gs.shape[0]
    tr = r if r <= 512 else 512
    c1 = 1.0 - ADAM_B1 ** ADAM_STEP
    c2 = 1.0 - ADAM_B2 ** ADAM_STEP

    def body(w_ref, g_ref, m_ref, v_ref, go_ref, d_ref, mo_ref, vo_ref):
        g = g_ref[0].astype(F32)
        for s in range(1, nsum):
            g = g + g_ref[s].astype(F32)
        mn = ADAM_B1 * m_ref[...] + (1.0 - ADAM_B1) * g
        vn = ADAM_B2 * v_ref[...] + (1.0 - ADAM_B2) * (g * g)
        go_ref[...] = g
        mo_ref[...] = mn
        vo_ref[...] = vn
        d_ref[...] = -ADAM_LR * ((mn / c1) / (jnp.sqrt(vn / c2) + ADAM_EPS) + ADAM_WD * w_ref[...])

    blk = pl.BlockSpec((tr, c), lambda i: (i, 0))
    return pl.pallas_call(
        body, name=name, grid=(r // tr,),
        in_specs=[blk, pl.BlockSpec((nsum, tr, c), lambda i: (0, i, 0)), blk, blk],
        out_specs=[blk] * 4, out_shape=[jax.ShapeDtypeStruct((r, c), F32)] * 4,
        compiler_params=_params(("arbitrary",)),
    )(w, gs, m, v)


def _local_step(x, target, h, ht, bias, bk, w_sect, conv_w, a_log, dt_bias, dn_norm_w, q_norm_w, k_norm_w, wout_shard):
    arow = jnp.zeros((1, 128), F32).at[0, DN_HEADS:2 * DN_HEADS].set(-jnp.exp(a_log[0]))
    dtb = jnp.zeros((1, 128), F32).at[0, DN_HEADS:2 * DN_HEADS].set(dt_bias[0])
    g_np, gt_np = _group_mats()
    g, gt = jnp.asarray(g_np), jnp.asarray(gt_np)
    g2 = jnp.asarray(np.kron(np.eye(2, dtype=np.float32), np.ones((HD, HD), np.float32)))
    wq = jnp.tile(q_norm_w, (1, ATT_HEADS)) * (HD ** -0.5)
    wk = jnp.tile(k_norm_w, (1, ATT_HEADS))

    pdn, qkv_dn, z, patt, gate, ba, wout8 = _inproj(h, w_sect, conv_w, wout_shard)
    w_out = wout8.reshape(D_MODEL, D_MODEL)
    oraw, ydn, sh, th = _dn_scan_fwd(qkv_dn, ba, z, arow, dtb, dn_norm_w)
    oatt, yatt, lse = _att_fwd(patt, gate, bias, wq, wk, g2)
    dy, mix_t, loss8 = _outproj_loss(x, ydn, yatt, w_out, target)

    do_dn, dz, do_att, dgate, dd, ddnw = _outproj_bwd(dy, w_out.T, oraw, z, dn_norm_w, oatt, gate, g, gt)
    d_wout = _grad_matmul(mix_t, dy, "dw_out")
    dq, dk, dv, drb, dwq8, dwk8 = _att_bwd(patt, do_att, lse, dd, bias, bk, wq, wk, g2)
    dqkv_dn, dba = _dn_scan_bwd(qkv_dn, ba, do_dn, sh, th, arow, dtb)
    dpdn, d_conv = _dn_prep_bwd(pdn, conv_w, dqkv_dn)
    dsecs = (dpdn, dz, dq, dk, dv, dgate, dba)
    dw_mid = _grad_matmul_many(ht, dsecs[1:6], "dw_in_mid")
    dw_sections = [_grad_matmul(ht, dpdn, "dw_in_dn"), *dw_mid, _grad_matmul(ht, dba, "dw_in_ba")]
    return dict(w_in_sections=dw_sections, conv_w=d_conv, w_out=d_wout, dy=dy, dsecs=dsecs,
                small_parts=(loss8, ddnw, dwq8, dwk8, drb))


def _finish_step(x, norm_w, w_sect_t, gr, partials):
    grad_x, dnw8, cs8, got = _inproj_bwd(x, norm_w, w_sect_t, gr["dy"], gr["dsecs"], partials)
    return grad_x, _pack_small_grads(dnw8, cs8, *gr["small_parts"]), got


SMALL_ROWS = 24
SMALL_AT = dict(a_log=(slice(8, 9), slice(0, 4)), dt_bias=(slice(9, 10), slice(0, 4)),
                dn_norm_w=(slice(10, 11), slice(0, 128)), q_norm_w=(slice(11, 12), slice(0, HD)),
                k_norm_w=(slice(12, 13), slice(0, HD)), rel_bias=(slice(16, 24), slice(0, N_BUCKETS)))
SMALL_NAMES = ("norm_w", "a_log", "dt_bias", "dn_norm_w", "q_norm_w", "k_norm_w", "rel_bias")


LOSS_ROW = 13


def _pack_small_grads(dnw8, cs8, loss8, ddnw8, dwq8, dwk8, drb):
    def body(dnw_ref, cs_ref, loss_ref, ddnw_ref, dwq_ref, dwk_ref, drb_ref, o_ref):
        lane = _iota((8, 128), 1)
        o_ref[...] = jnp.zeros_like(o_ref)
        o_ref[LOSS_ROW:LOSS_ROW + 1, :] = jnp.where(lane == 0, loss_ref[...], 0.0)[0:1, :]
        for k in range(D_MODEL // 128):
            o_ref[k:k + 1, :] = dnw_ref[0:1, k * 128:(k + 1) * 128]
        cs = cs_ref[...]
        o_ref[8:9, :] = jnp.where(lane < DN_HEADS, pltpu.roll(cs, 128 - 2 * DN_HEADS, 1), 0.0)[0:1, :]
        o_ref[9:10, :] = jnp.where(lane < DN_HEADS, pltpu.roll(cs, 128 - DN_HEADS, 1), 0.0)[0:1, :]
        o_ref[10:11, :] = ddnw_ref[0:1, :]
        for row, ref, scale in ((11, dwq_ref, HD ** -0.5), (12, dwk_ref, 1.0)):
            acc = ref[:, 0:128] + ref[:, 128:256] + ref[:, 256:384] + ref[:, 384:512]
            acc = (acc + pltpu.roll(acc, HD, 1)) * scale
            o_ref[row:row + 1, :] = jnp.where(lane < HD, acc, 0.0)[0:1, :]
        o_ref[16:24, :] = drb_ref[...]

    return pl.pallas_call(body, name="pack_small_grads", out_shape=jax.ShapeDtypeStruct((SMALL_ROWS, 128), F32),
                          )(dnw8, cs8, loss8, ddnw8, dwq8, dwk8, drb)


def _adam_math(w, g, m, v):
    c1 = 1.0 - ADAM_B1 ** ADAM_STEP
    c2 = 1.0 - ADAM_B2 ** ADAM_STEP
    mn = ADAM_B1 * m + (1.0 - ADAM_B1) * g
    vn = ADAM_B2 * v + (1.0 - ADAM_B2) * (g * g)
    return -ADAM_LR * ((mn / c1) / (jnp.sqrt(vn / c2) + ADAM_EPS) + ADAM_WD * w), mn, vn


def _adamw_small(gs, ws, ms, vs):
    n = len(SMALL_NAMES)

    def body(g_ref, *refs):
        w_refs, m_refs, v_refs = refs[:n], refs[n:2 * n], refs[2 * n:3 * n]
        outs, loss_ref = refs[3 * n:7 * n], refs[7 * n]
        loss = g_ref[0, LOSS_ROW:LOSS_ROW + 1, :]
        for s in range(1, gs.shape[0]):
            loss = loss + g_ref[s, LOSS_ROW:LOSS_ROW + 1, :]
        loss_ref[...] = loss

        def one(i, rows, lanes, at):
            g = g_ref[0, rows, lanes]
            for s in range(1, gs.shape[0]):
                g = g + g_ref[s, rows, lanes]
            d, mn, vn = _adam_math(w_refs[i][at], g, m_refs[i][at], v_refs[i][at])
            for kind, val in enumerate((g, d, mn, vn)):
                outs[kind * n + i][at] = val

        for k in range(D_MODEL // 128):
            one(0, slice(k, k + 1), slice(0, 128), (slice(0, 1), slice(k * 128, (k + 1) * 128)))
        for i, nm in enumerate(SMALL_NAMES[1:], start=1):
            rows, lanes = SMALL_AT[nm]
            one(i, rows, lanes, (slice(None), slice(None)))

    shapes = [jax.ShapeDtypeStruct(w.shape, F32) for w in ws]
    res = pl.pallas_call(body, name="adamw_small",
                         out_shape=shapes * 4 + [jax.ShapeDtypeStruct((1, 128), F32)])(gs, *ws, *ms, *vs)
    return [res[k * n:(k + 1) * n] for k in range(4)], res[4 * n]


def kernel(x, norm_w, w_in, conv_w, a_log, dt_bias, dn_norm_w, q_norm_w, k_norm_w, rel_bias, w_out, loss_target, m_norm_w, m_w_in, m_conv_w, m_a_log, m_dt_bias, m_dn_norm_w, m_q_norm_w, m_k_norm_w, m_rel_bias, m_w_out, v_norm_w, v_w_in, v_conv_w, v_a_log, v_dt_bias, v_dn_norm_w, v_q_norm_w, v_k_norm_w, v_rel_bias, v_w_out):
    assert w_in.shape[2] == SHARD_COLS
    bk = jnp.asarray(_bucket_tables())
    h, ht, bias, (win8, conv8) = _norm_and_gather(x[0], norm_w, [w_in[0].astype(MXU), conv_w[0]], rel_bias, bk)
    w_sect, w_sect_t = _build_w(win8)
    conv_full = conv8.transpose(1, 0, 2).reshape(4, 3 * D_DN)

    gr = _local_step(x[0], loss_target[0], h, ht, bias, bk, w_sect, conv_full, a_log, dt_bias, dn_norm_w, q_norm_w,
                     k_norm_w, w_out[0].astype(MXU))

    slabs = [_build_slabs(gr["w_in_sections"]),
             gr["w_out"].reshape(4, 2, D_MODEL // N_DEV, D_MODEL).transpose(1, 0, 2, 3),
             gr["conv_w"].reshape(4, 4, 2, 3 * D_DN // N_DEV).transpose(2, 1, 0, 3)]
    core = lax.axis_index("c").astype(jnp.int32).reshape(1)
    from_sibling = _swap_siblings(slabs)
    wires = (GRAD_WIRE, GRAD_WIRE, F32)
    partial = [_chip_sum(slabs[i], from_sibling[i], core, wires[i], "chip_sum_%d" % i) for i in range(3)]
    grad_x, small_pack, (r_win, r_wout, r_conv) = _finish_step(x[0], norm_w, w_sect_t, gr, partial)
    r_small = _share_small(small_pack)

    g_win, d_win, m_win, v_win = _adamw_sum(w_in[0], r_win, m_w_in[0], v_w_in[0], "adamw_w_in")
    g_wout, d_wout, m_wout, v_wout = _adamw_sum(w_out[0], r_wout, m_w_out[0], v_w_out[0], "adamw_w_out")
    g_conv, d_conv, m_conv, v_conv = _adamw_sum(conv_w[0], r_conv, m_conv_w[0], v_conv_w[0], "adamw_conv_w")
    small, loss_row = _adamw_small(r_small,
                                   (norm_w, a_log, dt_bias, dn_norm_w, q_norm_w, k_norm_w, rel_bias),
                                   (m_norm_w, m_a_log, m_dt_bias, m_dn_norm_w, m_q_norm_w, m_k_norm_w, m_rel_bias),
                                   (v_norm_w, v_a_log, v_dt_bias, v_dn_norm_w, v_q_norm_w, v_k_norm_w, v_rel_bias))

    loss = loss_row[0, 0]
    names = ("norm_w", "w_in", "conv_w", "a_log", "dt_bias", "dn_norm_w", "q_norm_w", "k_norm_w", "rel_bias", "w_out")
    big = dict(w_in=(g_win, d_win, m_win, v_win), conv_w=(g_conv, d_conv, m_conv, v_conv),
               w_out=(g_wout, d_wout, m_wout, v_wout))
    outs = [loss, grad_x[None]]
    for kind in range(4):
        for nm in names:
            outs.append(big[nm][kind][None] if nm in big else small[kind][SMALL_NAMES.index(nm)])
    return tuple(outs)
```

```python
import math

import numpy as np
import jax
import jax.numpy as jnp
from jax import lax
from jax.experimental import pallas as pl
from jax.experimental.pallas import tpu as pltpu

F32 = jnp.float32
MXU = jnp.bfloat16
GRAD_WIRE = jnp.bfloat16

D_MODEL = 1024
D_DN = 512
DN_HEADS = 4
DK = 128
CHUNK = 64
D_ATT = 512
ATT_HEADS = 8
HD = 64
PATTERNS = ((128, 1), (512, 4), (2048, 16))
BLK = 128
N_BUCKETS = 32
MAX_DISTANCE = 2048
EPS = 1e-6
W_COLS = 4224
N_DEV = 8
AXES = ("x", "y", "c")

ADAM_LR = 0.001
ADAM_B1 = 0.9
ADAM_B2 = 0.999
ADAM_EPS = 1e-08
ADAM_WD = 0.01
ADAM_STEP = 10

VMEM_LIMIT = 56 * 1024 * 1024
NEG = -1e30


def _dot(a, b):
    return jnp.dot(a.astype(MXU), b.astype(MXU), preferred_element_type=F32)


def _dot_nt(a, b):
    return lax.dot_general(a.astype(MXU), b.astype(MXU), (((1,), (1,)), ((), ())), preferred_element_type=F32)


def _dot_tn(a, b):
    return lax.dot_general(a.astype(MXU), b.astype(MXU), (((0,), (0,)), ((), ())), preferred_element_type=F32)


def _split(a):
    hi = a.astype(jnp.bfloat16)
    return hi, (a - hi.astype(F32)).astype(jnp.bfloat16)


def _dot_split(a, b, dims, exact):
    dg = lambda u, v: lax.dot_general(u, v, (dims, ((), ())), preferred_element_type=F32)
    if exact == "b":
        ah, al = _split(a)
        bh = b.astype(jnp.bfloat16)
        return dg(ah, bh) + dg(al, bh)
    if exact == "a":
        bh, bm = _split(b)
        bl = (b - bh.astype(F32) - bm.astype(F32)).astype(jnp.bfloat16)
        ah = a.astype(jnp.bfloat16)
        return dg(ah, bh) + (dg(ah, bm) + dg(ah, bl))
    ah, al = _split(a)
    bh, bl = _split(b)
    return dg(ah, bh) + (dg(ah, bl) + dg(al, bh))


def _wy_inverses(amat, eye):
    tinv = {x: eye - amat[x] for x in amat}
    pw = amat
    for _ in range(5):
        pw = {x: _hdot(pw[x], pw[x]) for x in amat}
        tinv = {x: tinv[x] + _hdot(tinv[x], pw[x]) for x in amat}
    return tinv


def _hdot(a, b, exact=None):
    return _dot_split(a, b, ((1,), (0,)), exact)


def _hdot_nt(a, b, exact=None):
    return _dot_split(a, b, ((1,), (1,)), exact)


def _hdot_tn(a, b, exact=None):
    return _dot_split(a, b, ((0,), (0,)), exact)


def _sigmoid(x):
    return 1.0 / (1.0 + jnp.exp(-x))


def _silu(x):
    return x * _sigmoid(x)


def _silu_and_grad(x):
    s = _sigmoid(x)
    return x * s, s * (1.0 + x * (1.0 - s))


def _softplus(x):
    return jnp.maximum(x, 0.0) + jnp.log(1.0 + jnp.exp(-jnp.abs(x)))


def _iota(shape, dim):
    return lax.broadcasted_iota(jnp.int32, shape, dim)


def _lane_col(x, k):
    return jnp.sum(jnp.where(_iota(x.shape, 1) == k, x, 0.0), axis=1, keepdims=True)


def _params(sem=None):
    return pltpu.CompilerParams(dimension_semantics=sem, vmem_limit_bytes=VMEM_LIMIT)


def _t5_bucket(dist):
    max_exact = N_BUCKETS // 2
    d = np.maximum(dist, 1).astype(np.float64)
    large = max_exact + (np.log(d / max_exact) / math.log(MAX_DISTANCE / max_exact)
                         * (N_BUCKETS - max_exact)).astype(np.int32)
    large = np.minimum(large, N_BUCKETS - 1)
    return np.where(dist < max_exact, dist, large).astype(np.int32)


def _bucket_tables():
    qi = np.arange(BLK)[:, None]
    kj = np.arange(2 * BLK)[None, :]
    step = qi - kj + BLK
    band = (step >= 0) & (step <= BLK)
    out = []
    for _, r in PATTERNS:
        b = _t5_bucket(np.clip(step, 0, None) * r)
        out.append(np.where(band, b, -1))
    return np.stack(out).astype(np.int32)


def _group_mats():
    g = np.zeros((D_ATT, 128), np.float32)
    for h in range(ATT_HEADS):
        g[h * HD:(h + 1) * HD, h] = 1.0
    return g, np.ascontiguousarray(g.T)


CHIP_FLIPS = ((1, 0), (0, 1), (1, 1))
ANY_SPEC = pl.BlockSpec(memory_space=pl.ANY)
MESH_ID = pl.DeviceIdType.MESH


def _other_chips():
    x, y = lax.axis_index("x"), lax.axis_index("y")
    return [((1 - x if fx else x), (1 - y if fy else y)) for fx, fy in CHIP_FLIPS]


def _gather_plan(ins, outs, send, recv, loc):
    n = len(ins)
    x, y, c = (lax.axis_index(a) for a in AXES)
    sib = (x, y, 1 - c)
    chips = _other_chips()
    lin = lambda px, py, pc: 4 * px + 2 * py + pc

    def copy(a, k, block, to, src=None):
        slot = outs[a].at[lin(*block)]
        return pltpu.make_async_remote_copy(src_ref=slot if src is None else src, dst_ref=slot,
                                            send_sem=send.at[a, k], recv_sem=recv.at[a, k],
                                            device_id=to, device_id_type=MESH_ID)

    mine = [pltpu.make_async_copy(ins[a], outs[a].at[lin(x, y, c)], loc.at[a]) for a in range(n)]
    firsts = []
    for a in range(n):
        firsts.append(copy(a, 0, (x, y, c), sib, src=ins[a]))
        firsts += [copy(a, 1 + j, (x, y, c), (*chip, c), src=ins[a]) for j, chip in enumerate(chips)]

    def begin():
        for cp in mine + firsts:
            cp.start()

    def finish():
        passed = []
        for j, chip in enumerate(chips):
            for a in range(n):
                copy(a, 1 + j, (*chip, c), (x, y, c)).wait_recv()
                fw = copy(a, 4 + j, (*chip, c), sib)
                fw.start()
                passed.append(fw)
        for a in range(n):
            copy(a, 0, sib, (x, y, c)).wait_recv()
            for j, chip in enumerate(chips):
                copy(a, 4 + j, (*chip, 1 - c), (x, y, c)).wait_recv()
        for cp in firsts + passed:
            cp.wait_send()
        for cp in mine:
            cp.wait()

    return begin, finish


GATHER_SEMS = lambda n: [pltpu.SemaphoreType.DMA((n, 7)), pltpu.SemaphoreType.DMA((n, 7)), pltpu.SemaphoreType.DMA((n,))]


def _swap_siblings(arrs):
    n = len(arrs)

    def body(*refs):
        ins, outs = refs[:n], refs[n:2 * n]
        send, recv = refs[2 * n:]
        x, y, c = (lax.axis_index(a) for a in AXES)
        cps = [pltpu.make_async_remote_copy(src_ref=ins[a].at[1 - c], dst_ref=outs[a], send_sem=send.at[a],
                                            recv_sem=recv.at[a], device_id=(x, y, 1 - c), device_id_type=MESH_ID)
               for a in range(n)]
        for cp in cps:
            cp.start()
        for cp in cps:
            cp.wait()

    return pl.pallas_call(
        body, name="swap_siblings", out_shape=[jax.ShapeDtypeStruct(a.shape[1:], a.dtype) for a in arrs],
        in_specs=[ANY_SPEC] * n, out_specs=[ANY_SPEC] * n,
        scratch_shapes=[pltpu.SemaphoreType.DMA((n,)), pltpu.SemaphoreType.DMA((n,))],
    )(*arrs)


def _chip_sum(mine2, theirs, core, wire, name):
    _, nchip, r, cdim = mine2.shape
    tr = r if r <= 1024 else 1024

    def body(core_ref, a_ref, b_ref, o_ref):
        del core_ref
        o_ref[...] = (a_ref[...].astype(F32) + b_ref[...].astype(F32)).astype(wire)

    grid_spec = pltpu.PrefetchScalarGridSpec(
        num_scalar_prefetch=1, grid=(nchip, r // tr),
        in_specs=[pl.BlockSpec((None, None, tr, cdim), lambda j, i, cr: (cr[0], j, i, 0)),
                  pl.BlockSpec((None, tr, cdim), lambda j, i, cr: (j, i, 0))],
        out_specs=pl.BlockSpec((None, tr, cdim), lambda j, i, cr: (j, i, 0)))
    return pl.pallas_call(
        body, name=name, grid_spec=grid_spec, out_shape=jax.ShapeDtypeStruct((nchip, r, cdim), wire),
        compiler_params=_params(("arbitrary", "arbitrary")),
    )(core, mine2, theirs)


def _chip_swap_copies(ins, outs, send, recv, loc):
    x, y, c = (lax.axis_index(a) for a in AXES)
    me = 2 * x + y
    starts, arrivals, drains = [], [], []
    for a in range(len(ins)):
        lc = pltpu.make_async_copy(ins[a].at[me], outs[a].at[me], loc.at[a])
        starts.append(lc.start)
        drains.append(lc.wait)
        for j, (px, py) in enumerate(_other_chips()):
            them = 2 * px + py
            cp = pltpu.make_async_remote_copy(src_ref=ins[a].at[them], dst_ref=outs[a].at[me], send_sem=send.at[a, j],
                                              recv_sem=recv.at[a, j], device_id=(px, py, c), device_id_type=MESH_ID)
            landing = pltpu.make_async_remote_copy(src_ref=ins[a].at[them], dst_ref=outs[a].at[them],
                                                   send_sem=send.at[a, j], recv_sem=recv.at[a, j],
                                                   device_id=(px, py, c), device_id_type=MESH_ID)
            starts.append(cp.start)
            arrivals.append(landing.wait_recv)
            drains.append(cp.wait_send)
    return starts, arrivals + drains


def _share_small(pack):
    def body(in_ref, out_ref, send, recv, loc):
        x, y, c = (lax.axis_index(a) for a in AXES)
        me = 4 * x + 2 * y + c
        lc = pltpu.make_async_copy(in_ref, out_ref.at[me], loc.at[0])
        lc.start()
        sends, arrivals = [], []
        for k in range(1, N_DEV):
            px = 1 - x if k & 4 else x
            py = 1 - y if k & 2 else y
            pc = 1 - c if k & 1 else c
            cp = pltpu.make_async_remote_copy(src_ref=in_ref, dst_ref=out_ref.at[me], send_sem=send.at[k - 1],
                                              recv_sem=recv.at[k - 1], device_id=(px, py, pc), device_id_type=MESH_ID)
            cp.start()
            sends.append(cp)
            arrivals.append(pltpu.make_async_remote_copy(src_ref=in_ref, dst_ref=out_ref.at[4 * px + 2 * py + pc],
                                                         send_sem=send.at[k - 1], recv_sem=recv.at[k - 1],
                                                         device_id=(px, py, pc), device_id_type=MESH_ID))
        for cp in arrivals:
            cp.wait_recv()
        for cp in sends:
            cp.wait_send()
        lc.wait()

    return pl.pallas_call(
        body, name="share_small", out_shape=jax.ShapeDtypeStruct((N_DEV,) + pack.shape, pack.dtype),
        in_specs=[ANY_SPEC], out_specs=ANY_SPEC,
        scratch_shapes=[pltpu.SemaphoreType.DMA((N_DEV - 1,)), pltpu.SemaphoreType.DMA((N_DEV - 1,)),
                        pltpu.SemaphoreType.DMA((1,))],
    )(pack)


W_PARTS = ((0, 0, 2048), (2048, 4096, 8), (2056, 2048, 2048))
SHARD_COLS = 513


def _pieces(lo, hi, parts):
    out = []
    for ref_start, tgt_start, width in parts:
        a, b = max(lo, ref_start), min(hi, ref_start + width)
        if a < b:
            out.append((a - lo, tgt_start + a - ref_start, b - a))
    return out


def _build_w(win8):
    tr = 512

    def body(in_ref, w_ref, wt_ref):
        w_ref[:, 4096:W_COLS] = jnp.zeros((tr, W_COLS - 4096), MXU)
        for p in range(N_DEV):
            for src, dst, width in _pieces(p * SHARD_COLS, (p + 1) * SHARD_COLS, W_PARTS):
                w_ref[:, dst:dst + width] = in_ref[p, :, src:src + width]
        for k in range(W_COLS // 128):
            wt_ref[k * 128:(k + 1) * 128, :] = w_ref[:, k * 128:(k + 1) * 128].astype(F32).T.astype(MXU)

    return pl.pallas_call(
        body, name="build_w", grid=(D_MODEL // tr,),
        in_specs=[pl.BlockSpec((N_DEV, tr, SHARD_COLS), lambda i: (0, i, 0))],
        out_specs=[pl.BlockSpec((tr, W_COLS), lambda i: (i, 0)), pl.BlockSpec((W_COLS, tr), lambda i: (0, i))],
        out_shape=[jax.ShapeDtypeStruct((D_MODEL, W_COLS), MXU), jax.ShapeDtypeStruct((W_COLS, D_MODEL), MXU)],
        compiler_params=_params(("arbitrary",)),
    )(win8)


def _build_slabs(secs):
    tr = 512
    parts = ((0, 0, 1536), (1536, 1, 512), (2048, 6, 8), (2056, 2, 512), (2568, 3, 512), (3080, 4, 512),
             (3592, 5, 512))

    def body(*refs):
        o_ref = refs[len(secs)]
        for p in range(N_DEV):
            lo, hi = p * SHARD_COLS, (p + 1) * SHARD_COLS
            for ref_start, idx, width in parts:
                a, b = max(lo, ref_start), min(hi, ref_start + width)
                if a < b:
                    o_ref[p % 2, p // 2, :, a - lo:b - lo] = refs[idx][:, a - ref_start:b - ref_start]

    return pl.pallas_call(
        body, name="build_slabs", grid=(D_MODEL // tr,),
        in_specs=[pl.BlockSpec((tr, s.shape[1]), lambda i: (i, 0)) for s in secs],
        out_specs=pl.BlockSpec((2, 4, tr, SHARD_COLS), lambda i: (0, 0, i, 0)),
        out_shape=jax.ShapeDtypeStruct((2, 4, D_MODEL, SHARD_COLS), secs[0].dtype),
        compiler_params=_params(("arbitrary",)),
    )(*secs)


def _fill_bias(rb_ref, bk_ref, bias_ref, pair):
    for p in range(len(PATTERNS)):
        bk_p = bk_ref[p]
        for hh in range(2):
            head = 2 * pair + hh
            bm = jnp.full((BLK, 2 * BLK), NEG, F32)
            for b in range(N_BUCKETS):
                bm = jnp.where(bk_p == b, rb_ref[head, b], bm)
            bias_ref[p, hh * BLK:(hh + 1) * BLK, :] = bm


def _norm_and_gather(x, nw, shards, rb, bk):
    t = x.shape[0]
    tm = 512
    nsteps = t // tm
    n = len(shards)
    npair = ATT_HEADS // 2
    assert nsteps >= npair

    def body(x_ref, nw_ref, *rest):
        ins, (rb_ref, bk_ref, h_ref, ht_ref, bias_ref) = rest[:n], rest[n:n + 5]
        outs, sems = rest[n + 5:2 * n + 5], rest[2 * n + 5:]
        begin, finish = _gather_plan(ins, outs, *sems)
        step = pl.program_id(0)

        @pl.when(step == 0)
        def _():
            begin()

        @pl.when(step == nsteps - 1)
        def _():
            finish()

        @pl.when(step < npair)
        def _():
            _fill_bias(rb_ref, bk_ref, bias_ref, step)

        xv = x_ref[...]
        rstd = lax.rsqrt(jnp.mean(xv * xv, axis=-1, keepdims=True) + EPS)
        hf = xv * rstd * nw_ref[...]
        h_ref[...] = hf.astype(MXU)
        ht_ref[...] = hf.T.astype(MXU)

    res = pl.pallas_call(
        body, name="norm_and_gather", grid=(nsteps,),
        in_specs=[pl.BlockSpec((tm, D_MODEL), lambda i: (i, 0)), pl.BlockSpec(nw.shape, lambda i: (0, 0))]
                 + [ANY_SPEC] * n
                 + [pl.BlockSpec(memory_space=pltpu.SMEM), pl.BlockSpec(bk.shape, lambda i: (0, 0, 0))],
        out_specs=[pl.BlockSpec((tm, D_MODEL), lambda i: (i, 0)), pl.BlockSpec((D_MODEL, tm), lambda i: (0, i)),
                   pl.BlockSpec((len(PATTERNS), None, 2 * BLK, 2 * BLK), lambda i: (0, jnp.minimum(i, npair - 1), 0, 0))]
                  + [ANY_SPEC] * n,
        out_shape=[jax.ShapeDtypeStruct((t, D_MODEL), MXU), jax.ShapeDtypeStruct((D_MODEL, t), MXU),
                   jax.ShapeDtypeStruct((len(PATTERNS), npair, 2 * BLK, 2 * BLK), F32)]
                  + [jax.ShapeDtypeStruct((N_DEV,) + a.shape, a.dtype) for a in shards],
        scratch_shapes=GATHER_SEMS(n),
        compiler_params=_params(("arbitrary",)),
    )(x, nw, *shards, rb, bk)
    return res[0], res[1], res[2], res[3:]


def _inproj(h_all, w, cw, wout_shard):
    t = h_all.shape[0]
    tm = 512
    nsteps = t // tm

    def body(h_ref, w_ref, cw_ref, wo_ref, pdn_ref, qkv_ref, z_ref, patt_ref, gate_ref, ba_ref,
             wo8_ref, halo_ref, send, recv, loc):
        begin, finish = _gather_plan([wo_ref], [wo8_ref], send, recv, loc)

        @pl.when(pl.program_id(0) == 0)
        def _():
            begin()
            halo_ref[...] = jnp.zeros_like(halo_ref)

        @pl.when(pl.program_id(0) == nsteps - 1)
        def _():
            finish()

        h = h_ref[...]
        for ref, lo, hi in ((z_ref, 1536, 2048), (patt_ref, 2048, 3584), (gate_ref, 3584, 4096), (ba_ref, 4096, 4224)):
            ref[...] = jnp.dot(h, w_ref[:, lo:hi], preferred_element_type=F32)
        pdn = jnp.dot(h, w_ref[:, 0:3 * D_DN], preferred_element_type=F32)
        pdn_ref[...] = pdn
        _dn_prep_tile(pdn, halo_ref, cw_ref, qkv_ref)

    row = lambda n: pl.BlockSpec((tm, n), lambda i: (i, 0))
    full = lambda a: pl.BlockSpec(a.shape, lambda i: (0,) * a.ndim)
    return pl.pallas_call(
        body, name="inproj", grid=(nsteps,),
        in_specs=[row(D_MODEL), full(w), full(cw), ANY_SPEC],
        out_specs=[row(1536), row(1536), row(512), row(1536), row(512), row(128), ANY_SPEC],
        out_shape=[jax.ShapeDtypeStruct((t, n), F32) for n in (1536, 1536, 512, 1536, 512, 128)] +
                  [jax.ShapeDtypeStruct((N_DEV,) + wout_shard.shape, wout_shard.dtype)],
        scratch_shapes=[pltpu.VMEM((8, 3 * D_DN), F32)] + GATHER_SEMS(1),
        compiler_params=_params(("arbitrary",)),
    )(h_all, w, cw, wout_shard)


CONV_ROWS = 512


def _conv_taps(u_ref, c, w_ref):
    r0 = c * CONV_ROWS
    if c == 0:
        ext = jnp.concatenate([jnp.zeros((8, 128), F32), u_ref[0:CONV_ROWS, :]], axis=0)
    else:
        ext = u_ref[r0 - 8:r0 + CONV_ROWS, :]
    taps = [ext[8:, :]] + [pltpu.roll(ext, k, 0)[8:, :] for k in (1, 2, 3)]
    y = taps[0] * w_ref[3:4, :]
    for k in (1, 2, 3):
        y = y + taps[k] * w_ref[3 - k:4 - k, :]
    return taps, y


def _dn_prep_tile(pdn, halo_ref, cw_ref, out_ref):
    rows = pdn.shape[0]
    ext = jnp.concatenate([halo_ref[...], pdn], axis=0)
    halo_ref[...] = pdn[rows - 8:, :]
    for j in range(3 * D_DN // 128):
        cols = slice(j * 128, (j + 1) * 128)
        e = ext[:, cols]
        y = e[8:, :] * cw_ref[3:4, cols]
        for k in (1, 2, 3):
            y = y + pltpu.roll(e, k, 0)[8:, :] * cw_ref[3 - k:4 - k, cols]
        a = _silu(y)
        if j < 2 * DN_HEADS:
            a = a * lax.rsqrt(jnp.sum(a * a, axis=1, keepdims=True) + EPS)
        if j < DN_HEADS:
            a = a * DK ** -0.5
        out_ref[:, cols] = a


def _chunk_common(qkv, ba, arow, dtb):
    c = CHUNK
    ri, ci = _iota((c, c), 0), _iota((c, c), 1)
    lane = _iota((c, 128), 1)
    g_all = jnp.where((lane >= DN_HEADS) & (lane < 2 * DN_HEADS), arow * _softplus(ba + dtb), 0.0)
    gc_all = _hdot((ri >= ci).astype(F32), g_all, "a")
    gc_t = gc_all.T
    beta_all = _sigmoid(ba)
    out = []
    for h in range(DN_HEADS):
        gc = _lane_col(gc_all, DN_HEADS + h)
        gcr = gc_t[DN_HEADS + h:DN_HEADS + h + 1, :]
        gl = gc[c - 1:c, :]
        out.append(dict(
            q=qkv[:, h * DK:(h + 1) * DK], k=qkv[:, D_DN + h * DK:D_DN + (h + 1) * DK],
            v=qkv[:, 2 * D_DN + h * DK:2 * D_DN + (h + 1) * DK],
            beta=_lane_col(beta_all, h), g=_lane_col(g_all, DN_HEADS + h),
            a_raw=_lane_col(ba, DN_HEADS + h), a_h=_lane_col(arow, DN_HEADS + h), dt_h=_lane_col(dtb, DN_HEADS + h),
            decay=jnp.exp(jnp.where(ri >= ci, gc - gcr, NEG)), eg=jnp.exp(gc), egl=jnp.exp(gl), etail=jnp.exp(gl - gc)))
    return out, ri, ci


SCAN_CHUNKS = 8


def _dn_scan_fwd(qkv, ba, z, arow, dtb, dnw):
    t = qkv.shape[0]
    n = t // CHUNK
    c = CHUNK
    cps = SCAN_CHUNKS
    hs = range(DN_HEADS)
    chains = [(j, h) for j in range(cps) for h in hs]

    def body(qkv_ref, ba_ref, z_ref, arow_ref, dtb_ref, dnw_ref, o_ref, y_ref, sh_ref, th_ref, s_ref):
        @pl.when(pl.program_id(0) == 0)
        def _():
            s_ref[...] = jnp.zeros_like(s_ref)

        ms = {}
        for j in range(cps):
            rows = slice(j * c, (j + 1) * c)
            mj, ri, ci = _chunk_common(qkv_ref[rows, :], ba_ref[rows, :], arow_ref[...], dtb_ref[...])
            for h in hs:
                ms[j, h] = mj[h]
        kb = {x: ms[x]["k"] * ms[x]["beta"] for x in chains}
        amat = {x: jnp.where(ri > ci, _dot_nt(kb[x], ms[x]["k"]) * ms[x]["decay"], 0.0) for x in chains}
        attn = {x: jnp.where(ri >= ci, _dot_nt(ms[x]["q"], ms[x]["k"]) * ms[x]["decay"], 0.0) for x in chains}
        tinv = _wy_inverses(amat, (ri == ci).astype(F32))
        uw = {x: _hdot(tinv[x], jnp.concatenate([ms[x]["v"] * ms[x]["beta"], kb[x] * ms[x]["eg"]], axis=1))
              for x in chains}
        u = {x: uw[x][:, :DK] for x in chains}
        w = {x: uw[x][:, DK:] for x in chains}
        q_dec = {x: ms[x]["q"] * ms[x]["eg"] for x in chains}
        k_tail = {x: ms[x]["k"] * ms[x]["etail"] for x in chains}
        s = [s_ref[h] for h in hs]
        for j in range(cps):
            rows = slice(j * c, (j + 1) * c)
            v_new = [u[j, h] - _dot(w[j, h], s[h]) for h in hs]
            o = [_dot(q_dec[j, h], s[h]) + _dot(attn[j, h], v_new[h]) for h in hs]
            for h in hs:
                sh_ref[j, h] = s[h]
                th_ref[j, h] = tinv[j, h]
            s = [s[h] * ms[j, h]["egl"] + _dot_tn(k_tail[j, h], v_new[h]) for h in hs]
            for h in hs:
                cols = slice(h * DK, (h + 1) * DK)
                o_ref[rows, cols] = o[h]
                rs = lax.rsqrt(jnp.mean(o[h] * o[h], axis=1, keepdims=True) + EPS)
                y_ref[rows, cols] = o[h] * rs * dnw_ref[...] * _silu(z_ref[rows, cols])
        for h in hs:
            s_ref[h] = s[h]

    row = lambda w_: pl.BlockSpec((cps * c, w_), lambda i: (i, 0))
    one = pl.BlockSpec((1, 128), lambda i: (0, 0))
    return pl.pallas_call(
        body, name="dn_scan_fwd", grid=(n // cps,),
        in_specs=[row(1536), row(128), row(512), one, one, one],
        out_specs=[row(512), row(512), pl.BlockSpec((cps, DN_HEADS, DK, DK), lambda i: (i, 0, 0, 0)),
                   pl.BlockSpec((cps, DN_HEADS, c, c), lambda i: (i, 0, 0, 0))],
        out_shape=[jax.ShapeDtypeStruct((t, 512), F32), jax.ShapeDtypeStruct((t, 512), F32),
                   jax.ShapeDtypeStruct((n, DN_HEADS, DK, DK), F32), jax.ShapeDtypeStruct((n, DN_HEADS, c, c), F32)],
        scratch_shapes=[pltpu.VMEM((DN_HEADS, DK, DK), F32)],
        compiler_params=_params(("arbitrary",)),
    )(qkv, ba, z, arow, dtb, dnw)


ATT_ROWS = 512


def _pair_rstd(xv, g2_ref):
    return lax.rsqrt(_hdot(xv * xv, g2_ref[...], "b") * (1.0 / HD) + EPS)


def _pair_norm(t, raw_refs, w_refs, out_refs, g2_ref):
    for c in range(t // ATT_ROWS):
        sl = slice(c * ATT_ROWS, (c + 1) * ATT_ROWS)
        for raw, w_ref, out in zip(raw_refs, w_refs, out_refs):
            xv = raw[sl, :]
            out[sl, :] = xv * _pair_rstd(xv, g2_ref) * w_ref[...]


BIAS_SPEC = pl.BlockSpec((len(PATTERNS), None, 2 * BLK, 2 * BLK), lambda i: (0, i, 0, 0))
PAIR_ROW_SPEC = pl.BlockSpec((1, 128), lambda i: (0, i))


def _stack_heads(xb, h0):
    return jnp.concatenate([jnp.where(h0, xb, 0.0), jnp.where(h0, 0.0, xb)], axis=0).astype(MXU)


def _block_rows(t, r, n):
    per_class = (t // r) // BLK
    res = n // per_class
    j = n % per_class
    start = res + BLK * r * j
    pstart = res + BLK * r * jnp.maximum(j - 1, 0)
    if r == 1:
        return pl.ds(pl.multiple_of(start, BLK), BLK), pl.ds(pl.multiple_of(pstart, BLK), BLK), j
    return pl.ds(start, BLK, stride=r), pl.ds(pstart, BLK, stride=r), j


def _att_fwd(qkv, gate, bias, wq, wk, g2):
    t = qkv.shape[0]
    rows = ATT_ROWS

    def body(bias_ref, qraw_ref, kraw_ref, v_ref, g_ref, wq_ref, wk_ref, g2_ref, o_ref, y_ref, lse_ref,
             o0_ref, o1_ref, o2_ref, l0_ref, l1_ref, l2_ref, q_ref, k_ref):
        h0 = _iota((BLK, 128), 1) < HD
        prev_cols = _iota((2 * BLK, 2 * BLK), 1) < BLK
        op_refs, lp_refs = (o0_ref, o1_ref, o2_ref), (l0_ref, l1_ref, l2_ref)
        _pair_norm(t, (qraw_ref, kraw_ref), (wq_ref, wk_ref), (q_ref, k_ref), g2_ref)

        for p, (_, r) in enumerate(PATTERNS):
            def blk(n, carry, p=p, r=r):
                cur, prev, j = _block_rows(t, r, n)
                q2 = _stack_heads(q_ref[cur, :], h0)
                k2 = jnp.concatenate([k_ref[prev, :], k_ref[cur, :]], axis=0).astype(MXU)
                v2 = jnp.concatenate([v_ref[prev, :], v_ref[cur, :]], axis=0).astype(MXU)
                s = _dot_nt(q2, k2) + bias_ref[p] + jnp.where(prev_cols & (j == 0), NEG, 0.0)
                m = jnp.max(s, axis=1, keepdims=True)
                e = jnp.exp(s - m)
                l = jnp.sum(e, axis=1, keepdims=True)
                pv = _dot(e, v2) / l
                lse = m + jnp.log(l)
                op_refs[p][cur, :] = jnp.where(h0, pv[:BLK], pv[BLK:])
                lp_refs[p][cur, :] = jnp.where(h0, lse[:BLK], lse[BLK:])
                return carry

            lax.fori_loop(0, t // BLK, blk, 0, unroll=16)

        for c in range(t // rows):
            sl = slice(c * rows, (c + 1) * rows)
            ls = [ref[sl, :] for ref in lp_refs]
            mx = jnp.maximum(jnp.maximum(ls[0], ls[1]), ls[2])
            ws = [jnp.exp(v_ - mx) for v_ in ls]
            den = ws[0] + ws[1] + ws[2]
            o = (ws[0] * o0_ref[sl, :] + ws[1] * o1_ref[sl, :] + ws[2] * o2_ref[sl, :]) / den
            o_ref[sl, :] = o
            y_ref[sl, :] = o * _silu(g_ref[sl, :])
            lse_ref[sl, :] = mx + jnp.log(den)

    col = lambda off: pl.BlockSpec((t, 128), lambda i, off=off: (0, off + i))
    return pl.pallas_call(
        body, name="att_fwd", grid=(ATT_HEADS // 2,),
        in_specs=[BIAS_SPEC, col(0), col(4), col(8), col(0), PAIR_ROW_SPEC, PAIR_ROW_SPEC,
                  pl.BlockSpec((128, 128), lambda i: (0, 0))],
        out_specs=[col(0), col(0), col(0)],
        out_shape=[jax.ShapeDtypeStruct((t, 512), F32)] * 3,
        scratch_shapes=[pltpu.VMEM((t, 128), F32)] * 8,
        compiler_params=_params(("arbitrary",)),
    )(bias, qkv, qkv, qkv, gate, wq, wk, g2)


def _outproj_loss(x, ydn, yatt, wout, target):
    t = x.shape[0]
    tm = 512

    def body(x_ref, a_ref, b_ref, w_ref, t_ref, dy_ref, mix_ref, loss_ref):
        @pl.when(pl.program_id(0) == 0)
        def _():
            loss_ref[...] = jnp.zeros_like(loss_ref)

        mixf = jnp.concatenate([a_ref[...], b_ref[...]], axis=1)
        mix_ref[...] = mixf.T.astype(MXU)
        err = x_ref[...] + jnp.dot(mixf.astype(MXU), w_ref[...], preferred_element_type=F32) - t_ref[...]
        dy_ref[...] = err * (1.0 / D_MODEL)
        loss_ref[...] += jnp.sum(err * err) * (0.5 / D_MODEL)

    row = lambda n: pl.BlockSpec((tm, n), lambda i: (i, 0))
    return pl.pallas_call(
        body, name="outproj_loss", grid=(t // tm,),
        in_specs=[row(D_MODEL), row(512), row(512), pl.BlockSpec(wout.shape, lambda i: (0, 0)), row(D_MODEL)],
        out_specs=[row(D_MODEL), pl.BlockSpec((D_MODEL, tm), lambda i: (0, i)), pl.BlockSpec((8, 128), lambda i: (0, 0))],
        out_shape=[jax.ShapeDtypeStruct((t, D_MODEL), F32), jax.ShapeDtypeStruct((D_MODEL, t), MXU),
                   jax.ShapeDtypeStruct((8, 128), F32)],
        compiler_params=_params(("arbitrary",)),
    )(x, ydn, yatt, wout, target)


def _outproj_bwd(dy, wout_t, oraw, z, dnw, oatt, gate, g, gt):
    t = dy.shape[0]
    tm = 512

    def body(dy_ref, w_ref, o_ref, z_ref, dnw_ref, oa_ref, g_ref, grp_ref, grpt_ref,
             do_ref, dz_ref, doa_ref, dg_ref, dd_ref, ddnw_ref):
        @pl.when(pl.program_id(0) == 0)
        def _():
            ddnw_ref[...] = jnp.zeros_like(ddnw_ref)

        dmix = jnp.dot(dy_ref[...].astype(MXU), w_ref[...], preferred_element_type=F32)
        dnw_v = dnw_ref[...]
        acc = jnp.zeros((1, DK), F32)
        for h in range(DN_HEADS):
            sl = slice(h * DK, (h + 1) * DK)
            o, zz, dm = o_ref[:, sl], z_ref[:, sl], dmix[:, sl]
            rs = lax.rsqrt(jnp.mean(o * o, axis=1, keepdims=True) + EPS)
            oh = o * rs
            silu_z, dsilu_z = _silu_and_grad(zz)
            dz_ref[:, sl] = (dm * oh * dnw_v * dsilu_z).astype(MXU)
            d_on = dm * silu_z
            gg = d_on * dnw_v
            do_ref[:, sl] = rs * (gg - oh * jnp.mean(gg * oh, axis=1, keepdims=True))
            acc = acc + jnp.sum(d_on * oh, axis=0, keepdims=True)
        ddnw_ref[...] += jnp.broadcast_to(acc, (8, DK))
        da, gate_v, oa = dmix[:, 512:], g_ref[...], oa_ref[...]
        silu_g, dsilu_g = _silu_and_grad(gate_v)
        doa = da * silu_g
        doa_ref[...] = doa
        dg_ref[...] = (da * oa * dsilu_g).astype(MXU)
        dd_ref[...] = _hdot(_hdot(doa * oa, grp_ref[...], "b"), grpt_ref[...], "b")

    row = lambda n: pl.BlockSpec((tm, n), lambda i: (i, 0))
    full = lambda a: pl.BlockSpec(a.shape, lambda i: (0,) * a.ndim)
    return pl.pallas_call(
        body, name="outproj_bwd", grid=(t // tm,),
        in_specs=[row(D_MODEL), full(wout_t), row(512), row(512), full(dnw), row(512), row(512), full(g), full(gt)],
        out_specs=[row(512)] * 5 + [pl.BlockSpec((8, DK), lambda i: (0, 0))],
        out_shape=[jax.ShapeDtypeStruct((t, 512), dt_) for dt_ in (F32, MXU, F32, MXU, F32)]
                  + [jax.ShapeDtypeStruct((8, DK), F32)],
        compiler_params=_params(("arbitrary",)),
    )(dy, wout_t, oraw, z, dnw, oatt, gate, g, gt)


def _grad_matmul(at, b, name):
    m, t = at.shape
    n = b.shape[1]
    tk = 1024
    tn = n if n <= 1536 else 512
    nk = t // tk

    def body(a_ref, b_ref, o_ref, acc_ref):
        k = pl.program_id(1)

        @pl.when(k == 0)
        def _():
            acc_ref[...] = jnp.zeros_like(acc_ref)

        acc_ref[...] += jnp.dot(a_ref[...], b_ref[...].astype(MXU), preferred_element_type=F32)

        @pl.when(k == nk - 1)
        def _():
            o_ref[...] = acc_ref[...].astype(GRAD_WIRE)

    return pl.pallas_call(
        body, name=name, grid=(n // tn, nk),
        in_specs=[pl.BlockSpec((m, tk), lambda j, k: (0, k)), pl.BlockSpec((tk, tn), lambda j, k: (k, j))],
        out_specs=pl.BlockSpec((m, tn), lambda j, k: (0, j)),
        out_shape=jax.ShapeDtypeStruct((m, n), GRAD_WIRE),
        scratch_shapes=[pltpu.VMEM((m, tn), F32)],
        compiler_params=_params(("arbitrary", "arbitrary")),
    )(at, b)


def _grad_matmul_many(at, bs, name):
    m, t = at.shape
    n = bs[0].shape[1]
    nb = len(bs)
    tk = 1024
    nk = t // tk

    def body(a_ref, *refs):
        b_refs, o_refs, acc_ref = refs[:nb], refs[nb:2 * nb], refs[2 * nb]
        s, k = pl.program_id(0), pl.program_id(1)

        @pl.when(k == 0)
        def _():
            acc_ref[...] = jnp.zeros_like(acc_ref)

        for i in range(nb):
            @pl.when(s == i)
            def _(i=i):
                acc_ref[...] += jnp.dot(a_ref[...], b_refs[i][...].astype(MXU), preferred_element_type=F32)

                @pl.when(k == nk - 1)
                def _():
                    o_refs[i][...] = acc_ref[...].astype(GRAD_WIRE)

    def b_spec(i):
        return pl.BlockSpec((tk, n), lambda s, k: (jnp.where(s == i, k, jnp.where(s < i, 0, nk - 1)), 0))

    return pl.pallas_call(
        body, name=name, grid=(nb, nk),
        in_specs=[pl.BlockSpec((m, tk), lambda s, k: (0, k))] + [b_spec(i) for i in range(nb)],
        out_specs=[pl.BlockSpec((m, n), lambda s, k: (0, 0))] * nb,
        out_shape=[jax.ShapeDtypeStruct((m, n), GRAD_WIRE)] * nb,
        scratch_shapes=[pltpu.VMEM((m, n), F32)],
        compiler_params=_params(("arbitrary", "arbitrary")),
    )(at, *bs)


def _att_bwd(qkv, do, lse, dd, bias, bk, wq, wk, g2):
    t = qkv.shape[0]
    rows = ATT_ROWS

    def body(bias_ref, bk_ref, qraw_ref, kraw_ref, v_ref, do_ref, lse_ref, dd_ref, wq_ref, wk_ref, g2_ref,
             dq_out, dk_out, dv_out, db_ref, dwq_ref, dwk_ref, ds_ref, q_ref, k_ref, dq_ref, dk_ref, dv_ref):
        pair = pl.program_id(0)

        @pl.when(pair == 0)
        def _():
            db_ref[...] = jnp.zeros_like(db_ref)

        _pair_norm(t, (qraw_ref, kraw_ref), (wq_ref, wk_ref), (q_ref, k_ref), g2_ref)
        ds_ref[...] = jnp.zeros_like(ds_ref)
        for c in range(t // rows):
            sl = slice(c * rows, (c + 1) * rows)
            for ref in (dq_ref, dk_ref, dv_ref):
                ref[sl, :] = jnp.zeros((rows, 128), F32)
        h0 = _iota((BLK, 128), 1) < HD
        prev_cols = _iota((2 * BLK, 2 * BLK), 1) < BLK

        def rows_of(xb):
            return jnp.concatenate([xb[:, 0:1], xb[:, HD:HD + 1]], axis=0)

        for p, (_, r) in enumerate(PATTERNS):
            def blk(n, carry, p=p, r=r):
                cur, prev, j = _block_rows(t, r, n)
                q2, do2 = _stack_heads(q_ref[cur, :], h0), _stack_heads(do_ref[cur, :], h0)
                k2 = jnp.concatenate([k_ref[prev, :], k_ref[cur, :]], axis=0).astype(MXU)
                v2 = jnp.concatenate([v_ref[prev, :], v_ref[cur, :]], axis=0).astype(MXU)
                s = _dot_nt(q2, k2) + bias_ref[p] + jnp.where(prev_cols & (j == 0), NEG, 0.0)
                prob = jnp.exp(s - rows_of(lse_ref[cur, :]))
                ds = prob * (_dot_nt(do2, v2) - rows_of(dd_ref[cur, :]))
                ds_ref[p] += ds
                dq2 = _dot(ds, k2)
                dk2 = _dot_tn(ds, q2)
                dv2 = _dot_tn(prob, do2)
                dq_ref[cur, :] += jnp.where(h0, dq2[:BLK], dq2[BLK:])
                dk_ref[prev, :] += dk2[:BLK]
                dv_ref[prev, :] += dv2[:BLK]
                dk_ref[cur, :] += dk2[BLK:]
                dv_ref[cur, :] += dv2[BLK:]
                return carry

            lax.fori_loop(0, t // BLK, blk, 0, unroll=8)

        ri, ci = _iota((8, 128), 0), _iota((8, 128), 1)
        upd = jnp.zeros((8, 128), F32)
        for p in range(len(PATTERNS)):
            bk = bk_ref[p]
            for hh in range(2):
                dsum = ds_ref[p, hh * BLK:(hh + 1) * BLK, :]
                for b in range(N_BUCKETS):
                    val = jnp.sum(jnp.where(bk == b, dsum, 0.0))
                    upd = upd + jnp.where((ri == 2 * pair + hh) & (ci == b), val, 0.0)
        db_ref[...] += upd

        for raw, d_ref, out, w_ref, dw_ref in ((qraw_ref, dq_ref, dq_out, wq_ref, dwq_ref),
                                               (kraw_ref, dk_ref, dk_out, wk_ref, dwk_ref)):
            acc = jnp.zeros((1, 128), F32)
            for c in range(t // rows):
                sl = slice(c * rows, (c + 1) * rows)
                xv, dyv = raw[sl, :], d_ref[sl, :]
                rs = _pair_rstd(xv, g2_ref)
                xh = xv * rs
                gg = dyv * w_ref[...]
                mean = _hdot(gg * xh, g2_ref[...], "b") * (1.0 / HD)
                out[sl, :] = (rs * (gg - xh * mean)).astype(MXU)
                acc = acc + jnp.sum(dyv * xh, axis=0, keepdims=True)
            dw_ref[...] = jnp.broadcast_to(acc, (8, 128))
        for c in range(t // rows):
            sl = slice(c * rows, (c + 1) * rows)
            dv_out[sl, :] = dv_ref[sl, :].astype(MXU)

    col = lambda off: pl.BlockSpec((t, 128), lambda i, off=off: (0, off + i))
    acc8 = pl.BlockSpec((8, 128), lambda i: (0, i))
    return pl.pallas_call(
        body, name="att_bwd", grid=(ATT_HEADS // 2,),
        in_specs=[BIAS_SPEC, pl.BlockSpec(bk.shape, lambda i: (0, 0, 0)),
                  col(0), col(4), col(8), col(0), col(0), col(0), PAIR_ROW_SPEC, PAIR_ROW_SPEC,
                  pl.BlockSpec((128, 128), lambda i: (0, 0))],
        out_specs=[col(0), col(0), col(0), pl.BlockSpec((8, 128), lambda i: (0, 0)), acc8, acc8],
        out_shape=[jax.ShapeDtypeStruct((t, 512), MXU)] * 3 + [jax.ShapeDtypeStruct((8, 128), F32)]
                  + [jax.ShapeDtypeStruct((8, 512), F32)] * 2,
        scratch_shapes=[pltpu.VMEM((len(PATTERNS), 2 * BLK, 2 * BLK), F32)] + [pltpu.VMEM((t, 128), F32)] * 5,
        compiler_params=_params(("arbitrary",)),
    )(bias, bk, qkv, qkv, qkv, do, lse, dd, wq, wk, g2)


def _dn_scan_bwd(qkv, ba, do, sh, th, arow, dtb):
    t = qkv.shape[0]
    n = t // CHUNK
    c = CHUNK
    cps = SCAN_CHUNKS

    def body(qkv_ref, ba_ref, do_ref, sh_ref, th_ref, arow_ref, dtb_ref, dqkv_ref, dba_ref, ds_ref):
        @pl.when(pl.program_id(0) == 0)
        def _():
            ds_ref[...] = jnp.zeros_like(ds_ref)

        hs = range(DN_HEADS)
        chains = [(j, h) for j in range(cps) for h in hs]
        lane = _iota((c, 128), 1)
        row = _iota((c, 1), 0)
        ms = {}
        for j in range(cps):
            rows_j = slice(j * c, (j + 1) * c)
            mj, ri, ci = _chunk_common(qkv_ref[rows_j, :], ba_ref[rows_j, :], arow_ref[...], dtb_ref[...])
            for h in hs:
                ms[j, h] = mj[h]
        q, k, v = ({x: ms[x][nm] for x in chains} for nm in ("q", "k", "v"))
        beta, decay = ({x: ms[x][nm] for x in chains} for nm in ("beta", "decay"))
        eg, egl, etail = ({x: ms[x][nm] for x in chains} for nm in ("eg", "egl", "etail"))
        s = {x: sh_ref[x[0], x[1]] for x in chains}
        tinv = {x: th_ref[x[0], x[1]] for x in chains}
        d_o = {(j, h): do_ref[j * c:(j + 1) * c, h * DK:(h + 1) * DK] for j, h in chains}
        kb = {x: k[x] * beta[x] for x in chains}
        vb = {x: v[x] * beta[x] for x in chains}
        kbg = {x: kb[x] * eg[x] for x in chains}
        amat = {x: jnp.where(ri > ci, _dot_nt(kb[x], k[x]) * decay[x], 0.0) for x in chains}
        attn = {x: jnp.where(ri >= ci, _dot_nt(q[x], k[x]) * decay[x], 0.0) for x in chains}
        uw = {x: _hdot(tinv[x], jnp.concatenate([vb[x], kbg[x]], axis=1)) for x in chains}
        u = {x: uw[x][:, :DK] for x in chains}
        w = {x: uw[x][:, DK:] for x in chains}
        v_new = {x: u[x] - _dot(w[x], s[x]) for x in chains}
        q_dec = {x: q[x] * eg[x] for x in chains}
        k_tail = {x: k[x] * etail[x] for x in chains}
        d_attn = {x: jnp.where(ri >= ci, _dot_nt(d_o[x], v_new[x]), 0.0) for x in chains}
        d_qdec = {x: _dot_nt(d_o[x], s[x]) for x in chains}
        from_o = {x: _dot_tn(attn[x], d_o[x]) for x in chains}
        to_state = {x: _dot_tn(q_dec[x], d_o[x]) for x in chains}

        d_s, d_vnew = {}, {}
        cur = [ds_ref[h] for h in hs]
        for j in reversed(range(cps)):
            for h in hs:
                d_s[j, h] = cur[h]
                d_vnew[j, h] = from_o[j, h] + _dot(k_tail[j, h], cur[h])
            cur = [to_state[j, h] + cur[h] * egl[j, h] - _dot_tn(w[j, h], d_vnew[j, h]) for h in hs]
        for h in hs:
            ds_ref[h] = cur[h]

        d_ktail = {x: _dot_nt(v_new[x], d_s[x]) for x in chains}
        d_gl = {x: jnp.sum(s[x] * d_s[x]) * egl[x] for x in chains}
        d_w = {x: -_dot_nt(d_vnew[x], s[x]) for x in chains}
        d_both = {x: _hdot_tn(tinv[x], jnp.concatenate([d_vnew[x], d_w[x]], axis=1)) for x in chains}
        d_vb = {x: d_both[x][:, :DK] for x in chains}
        d_kbg = {x: d_both[x][:, DK:] for x in chains}
        d_a = {x: -jnp.where(ri > ci, _hdot_nt(d_both[x], uw[x]), 0.0) for x in chains}
        d_qk = {x: d_attn[x] * decay[x] for x in chains}
        d_kk = {x: d_a[x] * decay[x] for x in chains}
        d_kb = {x: _dot(d_kk[x], k[x]) + d_kbg[x] * eg[x] for x in chains}
        d_q = {x: _dot(d_qk[x], k[x]) + d_qdec[x] * eg[x] for x in chains}
        d_k = {x: _dot_tn(d_qk[x], q[x]) + _dot_tn(d_kk[x], kb[x]) + d_ktail[x] * etail[x] + d_kb[x] * beta[x]
               for x in chains}
        d_beta = {x: jnp.sum(d_kb[x] * k[x] + d_vb[x] * v[x], axis=1, keepdims=True) for x in chains}
        mm = {x: d_a[x] * amat[x] + d_attn[x] * attn[x] for x in chains}
        for j in range(cps):
            rows_j = slice(j * c, (j + 1) * c)
            rows = jnp.zeros((c, c), F32)
            for h in hs:
                rows = rows + jnp.where(ri == h, jnp.sum(mm[j, h], axis=0, keepdims=True), 0.0)
            cols_t = jnp.concatenate([rows, jnp.zeros((c, c), F32)], axis=1).T[:c, :]
            d_gc_all = jnp.zeros((c, 128), F32)
            for h in hs:
                x = (j, h)
                tail_term = jnp.sum(d_ktail[x] * k_tail[x], axis=1, keepdims=True)
                d_gc = (jnp.sum(mm[x], axis=1, keepdims=True) - _lane_col(cols_t, h)
                        + jnp.sum(d_qdec[x] * q_dec[x] + d_kbg[x] * kbg[x], axis=1, keepdims=True) - tail_term)
                d_gc = d_gc + jnp.where(row == c - 1, jnp.sum(tail_term) + d_gl[x], 0.0)
                d_gc_all = d_gc_all + jnp.where(lane == DN_HEADS + h, d_gc, 0.0)
            d_g_all = _hdot((ri <= ci).astype(F32), d_gc_all, "a")
            dba = jnp.zeros((c, 128), F32)
            for h in hs:
                x = (j, h)
                d_g = _lane_col(d_g_all, DN_HEADS + h)
                d_braw = d_beta[x] * beta[x] * (1.0 - beta[x])
                d_araw = d_g * ms[x]["a_h"] * _sigmoid(ms[x]["a_raw"] + ms[x]["dt_h"])
                dba = dba + jnp.where(lane == h, d_braw, 0.0) + jnp.where(lane == DN_HEADS + h, d_araw, 0.0) \
                    + jnp.where(lane == 2 * DN_HEADS + h, d_g * ms[x]["g"], 0.0)
                dqkv_ref[rows_j, h * DK:(h + 1) * DK] = d_q[x]
                dqkv_ref[rows_j, D_DN + h * DK:D_DN + (h + 1) * DK] = d_k[x]
                dqkv_ref[rows_j, 2 * D_DN + h * DK:2 * D_DN + (h + 1) * DK] = d_vb[x] * beta[x]
            dba_ref[rows_j, :] = dba

    nsteps = n // cps
    rev = lambda w_: pl.BlockSpec((cps * c, w_), lambda i: (nsteps - 1 - i, 0))
    one = pl.BlockSpec((1, 128), lambda i: (0, 0))
    return pl.pallas_call(
        body, name="dn_scan_bwd", grid=(nsteps,),
        in_specs=[rev(1536), rev(128), rev(512),
                  pl.BlockSpec((cps, DN_HEADS, DK, DK), lambda i: (nsteps - 1 - i, 0, 0, 0)),
                  pl.BlockSpec((cps, DN_HEADS, c, c), lambda i: (nsteps - 1 - i, 0, 0, 0)), one, one],
        out_specs=[rev(1536), rev(128)],
        out_shape=[jax.ShapeDtypeStruct((t, 1536), F32), jax.ShapeDtypeStruct((t, 128), F32)],
        scratch_shapes=[pltpu.VMEM((DN_HEADS, DK, DK), F32)],
        compiler_params=_params(("arbitrary",)),
    )(qkv, ba, do, sh, th, arow, dtb)


def _dn_prep_bwd(pdn, cw, dact):
    t = pdn.shape[0]
    nchunk = t // CONV_ROWS

    def body(u_ref, w_ref, d_ref, du_ref, dw_ref, dy_ref):
        j = pl.program_id(0)
        dy_ref[t:t + 8, :] = jnp.zeros((8, 128), F32)
        dw = [jnp.zeros((1, 128), F32) for _ in range(4)]
        for c in range(nchunk):
            sl = slice(c * CONV_ROWS, (c + 1) * CONV_ROWS)
            taps, y = _conv_taps(u_ref, c, w_ref)
            a, da_dy = _silu_and_grad(y)
            dout = d_ref[sl, :]
            rs = lax.rsqrt(jnp.sum(a * a, axis=1, keepdims=True) + EPS)
            f = jnp.where(j < 8, rs, 1.0) * jnp.where(j < 4, DK ** -0.5, 1.0)
            corr = jnp.where(j < 8, f * rs * rs * jnp.sum(dout * a, axis=1, keepdims=True), 0.0)
            dy = (f * dout - corr * a) * da_dy
            dy_ref[sl, :] = dy
            for k_ in range(4):
                dw[3 - k_] = dw[3 - k_] + jnp.sum(taps[k_] * dy, axis=0, keepdims=True)
        for i in range(4):
            dw_ref[i:i + 1, :] = dw[i]
        for c in range(nchunk):
            r0 = c * CONV_ROWS
            ext = dy_ref[r0:r0 + CONV_ROWS + 8, :]
            du = ext[:CONV_ROWS, :] * w_ref[3:4, :]
            for k_ in (1, 2, 3):
                du = du + pltpu.roll(ext, CONV_ROWS + 8 - k_, 0)[:CONV_ROWS, :] * w_ref[3 - k_:4 - k_, :]
            du_ref[r0:r0 + CONV_ROWS, :] = du.astype(MXU)

    return pl.pallas_call(
        body, name="dn_prep_bwd", grid=(12,),
        in_specs=[pl.BlockSpec((t, 128), lambda j: (0, j)), pl.BlockSpec((4, 128), lambda j: (0, j)),
                  pl.BlockSpec((t, 128), lambda j: (0, j))],
        out_specs=[pl.BlockSpec((t, 128), lambda j: (0, j)), pl.BlockSpec((4, 128), lambda j: (0, j))],
        out_shape=[jax.ShapeDtypeStruct((t, 1536), MXU), jax.ShapeDtypeStruct((4, 1536), F32)],
        scratch_shapes=[pltpu.VMEM((t + 8, 128), F32)],
        compiler_params=_params(("arbitrary",)),
    )(pdn, cw, dact)


SECTIONS = (("dn", 0, 1536), ("z", 1536, 512), ("q", 2048, 512), ("k", 2560, 512), ("v", 3072, 512),
            ("gate", 3584, 512), ("ba", 4096, 128))


def _inproj_bwd(x, nw, wt, dy, dsecs, partials):
    t = x.shape[0]
    tm = 256
    npart = len(partials)
    nsteps = t // tm

    nsec = len(SECTIONS)

    def body(x_ref, nw_ref, w_ref, dy_ref, *rest):
        sec_refs, rest = rest[:nsec], rest[nsec:]
        part_refs, (gx_ref, dnw_ref, cs_ref) = rest[:npart], rest[npart:npart + 3]
        got_refs, (send, recv, loc) = rest[npart + 3:2 * npart + 3], rest[2 * npart + 3:]
        starts, waits = _chip_swap_copies(part_refs, got_refs, send, recv, loc)

        @pl.when(pl.program_id(0) == 0)
        def _():
            for start in starts:
                start()
            dnw_ref[...] = jnp.zeros_like(dnw_ref)
            cs_ref[...] = jnp.zeros_like(cs_ref)

        @pl.when(pl.program_id(0) == nsteps - 1)
        def _():
            for wait in waits:
                wait()

        dh = jnp.zeros((tm, D_MODEL), F32)
        for ref, (_, lo, width) in zip(sec_refs, SECTIONS):
            dh = dh + jnp.dot(ref[...].astype(MXU), w_ref[lo:lo + width, :], preferred_element_type=F32)
        xv = x_ref[...]
        rstd = lax.rsqrt(jnp.mean(xv * xv, axis=-1, keepdims=True) + EPS)
        xh = xv * rstd
        gg = dh * nw_ref[...]
        gx_ref[...] = rstd * (gg - xh * jnp.mean(gg * xh, axis=-1, keepdims=True)) + dy_ref[...]
        dnw_ref[...] += jnp.broadcast_to(jnp.sum(dh * xh, axis=0, keepdims=True), (8, D_MODEL))
        cs_ref[...] += jnp.broadcast_to(jnp.sum(sec_refs[nsec - 1][...], axis=0, keepdims=True), (8, 128))

    row = lambda n: pl.BlockSpec((tm, n), lambda i: (i, 0))
    full = lambda a: pl.BlockSpec(a.shape, lambda i: (0,) * a.ndim)
    res = pl.pallas_call(
        body, name="inproj_bwd", grid=(nsteps,),
        in_specs=[row(D_MODEL), full(nw), full(wt), row(D_MODEL)] + [row(width) for _, _, width in SECTIONS]
                 + [ANY_SPEC] * npart,
        out_specs=[row(D_MODEL), pl.BlockSpec((8, D_MODEL), lambda i: (0, 0)), pl.BlockSpec((8, 128), lambda i: (0, 0))]
                  + [ANY_SPEC] * npart,
        out_shape=[jax.ShapeDtypeStruct((t, D_MODEL), F32), jax.ShapeDtypeStruct((8, D_MODEL), F32),
                   jax.ShapeDtypeStruct((8, 128), F32)] + [jax.ShapeDtypeStruct(p.shape, p.dtype) for p in partials],
        scratch_shapes=[pltpu.SemaphoreType.DMA((npart, 3)), pltpu.SemaphoreType.DMA((npart, 3)),
                        pltpu.SemaphoreType.DMA((npart,))],
        compiler_params=_params(("arbitrary",)),
    )(x, nw, wt, dy, *dsecs, *partials)
    return res[0], res[1], res[2], res[3:]


def _adamw_sum(w, gs, m, v, name):
    r, c = w.shape
    nsum = gs.shape[0]
    tr = r if r <= 512 else 512
    c1 = 1.0 - ADAM_B1 ** ADAM_STEP
    c2 = 1.0 - ADAM_B2 ** ADAM_STEP

    def body(w_ref, g_ref, m_ref, v_ref, go_ref, d_ref, mo_ref, vo_ref):
        g = g_ref[0].astype(F32)
        for s in range(1, nsum):
            g = g + g_ref[s].astype(F32)
        mn = ADAM_B1 * m_ref[...] + (1.0 - ADAM_B1) * g
        vn = ADAM_B2 * v_ref[...] + (1.0 - ADAM_B2) * (g * g)
        go_ref[...] = g
        mo_ref[...] = mn
        vo_ref[...] = vn
        d_ref[...] = -ADAM_LR * ((mn / c1) / (jnp.sqrt(vn / c2) + ADAM_EPS) + ADAM_WD * w_ref[...])

    blk = pl.BlockSpec((tr, c), lambda i: (i, 0))
    return pl.pallas_call(
        body, name=name, grid=(r // tr,),
        in_specs=[blk, pl.BlockSpec((nsum, tr, c), lambda i: (0, i, 0)), blk, blk],
        out_specs=[blk] * 4, out_shape=[jax.ShapeDtypeStruct((r, c), F32)] * 4,
        compiler_params=_params(("arbitrary",)),
    )(w, gs, m, v)


def _local_step(x, target, h, ht, bias, bk, w_sect, conv_w, a_log, dt_bias, dn_norm_w, q_norm_w, k_norm_w, wout_shard):
    arow = jnp.zeros((1, 128), F32).at[0, DN_HEADS:2 * DN_HEADS].set(-jnp.exp(a_log[0]))
    dtb = jnp.zeros((1, 128), F32).at[0, DN_HEADS:2 * DN_HEADS].set(dt_bias[0])
    g_np, gt_np = _group_mats()
    g, gt = jnp.asarray(g_np), jnp.asarray(gt_np)
    g2 = jnp.asarray(np.kron(np.eye(2, dtype=np.float32), np.ones((HD, HD), np.float32)))
    wq = jnp.tile(q_norm_w, (1, ATT_HEADS)) * (HD ** -0.5)
    wk = jnp.tile(k_norm_w, (1, ATT_HEADS))

    pdn, qkv_dn, z, patt, gate, ba, wout8 = _inproj(h, w_sect, conv_w, wout_shard)
    w_out = wout8.reshape(D_MODEL, D_MODEL)
    oraw, ydn, sh, th = _dn_scan_fwd(qkv_dn, ba, z, arow, dtb, dn_norm_w)
    oatt, yatt, lse = _att_fwd(patt, gate, bias, wq, wk, g2)
    dy, mix_t, loss8 = _outproj_loss(x, ydn, yatt, w_out, target)

    do_dn, dz, do_att, dgate, dd, ddnw = _outproj_bwd(dy, w_out.T, oraw, z, dn_norm_w, oatt, gate, g, gt)
    d_wout = _grad_matmul(mix_t, dy, "dw_out")
    dq, dk, dv, drb, dwq8, dwk8 = _att_bwd(patt, do_att, lse, dd, bias, bk, wq, wk, g2)
    dqkv_dn, dba = _dn_scan_bwd(qkv_dn, ba, do_dn, sh, th, arow, dtb)
    dpdn, d_conv = _dn_prep_bwd(pdn, conv_w, dqkv_dn)
    dsecs = (dpdn, dz, dq, dk, dv, dgate, dba)
    dw_mid = _grad_matmul_many(ht, dsecs[1:6], "dw_in_mid")
    dw_sections = [_grad_matmul(ht, dpdn, "dw_in_dn"), *dw_mid, _grad_matmul(ht, dba, "dw_in_ba")]
    return dict(w_in_sections=dw_sections, conv_w=d_conv, w_out=d_wout, dy=dy, dsecs=dsecs,
                small_parts=(loss8, ddnw, dwq8, dwk8, drb))


def _finish_step(x, norm_w, w_sect_t, gr, partials):
    grad_x, dnw8, cs8, got = _inproj_bwd(x, norm_w, w_sect_t, gr["dy"], gr["dsecs"], partials)
    return grad_x, _pack_small_grads(dnw8, cs8, *gr["small_parts"]), got


SMALL_ROWS = 24
SMALL_AT = dict(a_log=(slice(8, 9), slice(0, 4)), dt_bias=(slice(9, 10), slice(0, 4)),
                dn_norm_w=(slice(10, 11), slice(0, 128)), q_norm_w=(slice(11, 12), slice(0, HD)),
                k_norm_w=(slice(12, 13), slice(0, HD)), rel_bias=(slice(16, 24), slice(0, N_BUCKETS)))
SMALL_NAMES = ("norm_w", "a_log", "dt_bias", "dn_norm_w", "q_norm_w", "k_norm_w", "rel_bias")


LOSS_ROW = 13


def _pack_small_grads(dnw8, cs8, loss8, ddnw8, dwq8, dwk8, drb):
    def body(dnw_ref, cs_ref, loss_ref, ddnw_ref, dwq_ref, dwk_ref, drb_ref, o_ref):
        lane = _iota((8, 128), 1)
        o_ref[...] = jnp.zeros_like(o_ref)
        o_ref[LOSS_ROW:LOSS_ROW + 1, :] = jnp.where(lane == 0, loss_ref[...], 0.0)[0:1, :]
        for k in range(D_MODEL // 128):
            o_ref[k:k + 1, :] = dnw_ref[0:1, k * 128:(k + 1) * 128]
        cs = cs_ref[...]
        o_ref[8:9, :] = jnp.where(lane < DN_HEADS, pltpu.roll(cs, 128 - 2 * DN_HEADS, 1), 0.0)[0:1, :]
        o_ref[9:10, :] = jnp.where(lane < DN_HEADS, pltpu.roll(cs, 128 - DN_HEADS, 1), 0.0)[0:1, :]
        o_ref[10:11, :] = ddnw_ref[0:1, :]
        for row, ref, scale in ((11, dwq_ref, HD ** -0.5), (12, dwk_ref, 1.0)):
            acc = ref[:, 0:128] + ref[:, 128:256] + ref[:, 256:384] + ref[:, 384:512]
            acc = (acc + pltpu.roll(acc, HD, 1)) * scale
            o_ref[row:row + 1, :] = jnp.where(lane < HD, acc, 0.0)[0:1, :]
        o_ref[16:24, :] = drb_ref[...]

    return pl.pallas_call(body, name="pack_small_grads", out_shape=jax.ShapeDtypeStruct((SMALL_ROWS, 128), F32),
                          )(dnw8, cs8, loss8, ddnw8, dwq8, dwk8, drb)


def _adam_math(w, g, m, v):
    c1 = 1.0 - ADAM_B1 ** ADAM_STEP
    c2 = 1.0 - ADAM_B2 ** ADAM_STEP
    mn = ADAM_B1 * m + (1.0 - ADAM_B1) * g
    vn = ADAM_B2 * v + (1.0 - ADAM_B2) * (g * g)
    return -ADAM_LR * ((mn / c1) / (jnp.sqrt(vn / c2) + ADAM_EPS) + ADAM_WD * w), mn, vn


def _adamw_small(gs, ws, ms, vs):
    n = len(SMALL_NAMES)

    def body(g_ref, *refs):
        w_refs, m_refs, v_refs = refs[:n], refs[n:2 * n], refs[2 * n:3 * n]
        outs, loss_ref = refs[3 * n:7 * n], refs[7 * n]
        loss = g_ref[0, LOSS_ROW:LOSS_ROW + 1, :]
        for s in range(1, gs.shape[0]):
            loss = loss + g_ref[s, LOSS_ROW:LOSS_ROW + 1, :]
        loss_ref[...] = loss

        def one(i, rows, lanes, at):
            g = g_ref[0, rows, lanes]
            for s in range(1, gs.shape[0]):
                g = g + g_ref[s, rows, lanes]
            d, mn, vn = _adam_math(w_refs[i][at], g, m_refs[i][at], v_refs[i][at])
            for kind, val in enumerate((g, d, mn, vn)):
                outs[kind * n + i][at] = val

        for k in range(D_MODEL // 128):
            one(0, slice(k, k + 1), slice(0, 128), (slice(0, 1), slice(k * 128, (k + 1) * 128)))
        for i, nm in enumerate(SMALL_NAMES[1:], start=1):
            rows, lanes = SMALL_AT[nm]
            one(i, rows, lanes, (slice(None), slice(None)))

    shapes = [jax.ShapeDtypeStruct(w.shape, F32) for w in ws]
    res = pl.pallas_call(body, name="adamw_small",
                         out_shape=shapes * 4 + [jax.ShapeDtypeStruct((1, 128), F32)])(gs, *ws, *ms, *vs)
    return [res[k * n:(k + 1) * n] for k in range(4)], res[4 * n]


def kernel(x, norm_w, w_in, conv_w, a_log, dt_bias, dn_norm_w, q_norm_w, k_norm_w, rel_bias, w_out, loss_target, m_norm_w, m_w_in, m_conv_w, m_a_log, m_dt_bias, m_dn_norm_w, m_q_norm_w, m_k_norm_w, m_rel_bias, m_w_out, v_norm_w, v_w_in, v_conv_w, v_a_log, v_dt_bias, v_dn_norm_w, v_q_norm_w, v_k_norm_w, v_rel_bias, v_w_out):
    assert w_in.shape[2] == SHARD_COLS
    bk = jnp.asarray(_bucket_tables())
    h, ht, bias, (win8, conv8) = _norm_and_gather(x[0], norm_w, [w_in[0].astype(MXU), conv_w[0]], rel_bias, bk)
    w_sect, w_sect_t = _build_w(win8)
    conv_full = conv8.transpose(1, 0, 2).reshape(4, 3 * D_DN)

    gr = _local_step(x[0], loss_target[0], h, ht, bias, bk, w_sect, conv_full, a_log, dt_bias, dn_norm_w, q_norm_w,
                     k_norm_w, w_out[0].astype(MXU))

    slabs = [_build_slabs(gr["w_in_sections"]),
             gr["w_out"].reshape(4, 2, D_MODEL // N_DEV, D_MODEL).transpose(1, 0, 2, 3),
             gr["conv_w"].reshape(4, 4, 2, 3 * D_DN // N_DEV).transpose(2, 1, 0, 3)]
    core = lax.axis_index("c").astype(jnp.int32).reshape(1)
    from_sibling = _swap_siblings(slabs)
    wires = (GRAD_WIRE, GRAD_WIRE, F32)
    partial = [_chip_sum(slabs[i], from_sibling[i], core, wires[i], "chip_sum_%d" % i) for i in range(3)]
    grad_x, small_pack, (r_win, r_wout, r_conv) = _finish_step(x[0], norm_w, w_sect_t, gr, partial)
    r_small = _share_small(small_pack)

    g_win, d_win, m_win, v_win = _adamw_sum(w_in[0], r_win, m_w_in[0], v_w_in[0], "adamw_w_in")
    g_wout, d_wout, m_wout, v_wout = _adamw_sum(w_out[0], r_wout, m_w_out[0], v_w_out[0], "adamw_w_out")
    g_conv, d_conv, m_conv, v_conv = _adamw_sum(conv_w[0], r_conv, m_conv_w[0], v_conv_w[0], "adamw_conv_w")
    small, loss_row = _adamw_small(r_small,
                                   (norm_w, a_log, dt_bias, dn_norm_w, q_norm_w, k_norm_w, rel_bias),
                                   (m_norm_w, m_a_log, m_dt_bias, m_dn_norm_w, m_q_norm_w, m_k_norm_w, m_rel_bias),
                                   (v_norm_w, v_a_log, v_dt_bias, v_dn_norm_w, v_q_norm_w, v_k_norm_w, v_rel_bias))

    loss = loss_row[0, 0]
    names = ("norm_w", "w_in", "conv_w", "a_log", "dt_bias", "dn_norm_w", "q_norm_w", "k_norm_w", "rel_bias", "w_out")
    big = dict(w_in=(g_win, d_win, m_win, v_win), conv_w=(g_conv, d_conv, m_conv, v_conv),
               w_out=(g_wout, d_wout, m_wout, v_wout))
    outs = [loss, grad_x[None]]
    for kind in range(4):
        for nm in names:
            outs.append(big[nm][kind][None] if nm in big else small[kind][SMALL_NAMES.index(nm)])
    return tuple(outs)
```

```python
import math

import numpy as np
import jax
import jax.numpy as jnp
from jax import lax
from jax.experimental import pallas as pl
from jax.experimental.pallas import tpu as pltpu

F32 = jnp.float32
MXU = jnp.bfloat16
GRAD_WIRE = jnp.bfloat16

D_MODEL = 1024
D_DN = 512
DN_HEADS = 4
DK = 128
CHUNK = 64
D_ATT = 512
ATT_HEADS = 8
HD = 64
PATTERNS = ((128, 1), (512, 4), (2048, 16))
BLK = 128
N_BUCKETS = 32
MAX_DISTANCE = 2048
EPS = 1e-6
W_COLS = 4224
N_DEV = 8
AXES = ("x", "y", "c")

ADAM_LR = 0.001
ADAM_B1 = 0.9
ADAM_B2 = 0.999
ADAM_EPS = 1e-08
ADAM_WD = 0.01
ADAM_STEP = 10

VMEM_LIMIT = 56 * 1024 * 1024
NEG = -1e30


def _dot(a, b):
    return jnp.dot(a.astype(MXU), b.astype(MXU), preferred_element_type=F32)


def _dot_nt(a, b):
    return lax.dot_general(a.astype(MXU), b.astype(MXU), (((1,), (1,)), ((), ())), preferred_element_type=F32)


def _dot_tn(a, b):
    return lax.dot_general(a.astype(MXU), b.astype(MXU), (((0,), (0,)), ((), ())), preferred_element_type=F32)


def _split(a):
    hi = a.astype(jnp.bfloat16)
    return hi, (a - hi.astype(F32)).astype(jnp.bfloat16)


def _dot_split(a, b, dims, exact):
    dg = lambda u, v: lax.dot_general(u, v, (dims, ((), ())), preferred_element_type=F32)
    if exact == "b":
        ah, al = _split(a)
        bh = b.astype(jnp.bfloat16)
        return dg(ah, bh) + dg(al, bh)
    if exact == "a":
        bh, bm = _split(b)
        bl = (b - bh.astype(F32) - bm.astype(F32)).astype(jnp.bfloat16)
        ah = a.astype(jnp.bfloat16)
        return dg(ah, bh) + (dg(ah, bm) + dg(ah, bl))
    ah, al = _split(a)
    bh, bl = _split(b)
    return dg(ah, bh) + (dg(ah, bl) + dg(al, bh))


def _wy_inverses(amat, eye):
    tinv = {x: eye - amat[x] for x in amat}
    pw = amat
    for _ in range(5):
        pw = {x: _hdot(pw[x], pw[x]) for x in amat}
        tinv = {x: tinv[x] + _hdot(tinv[x], pw[x]) for x in amat}
    return tinv


def _hdot(a, b, exact=None):
    return _dot_split(a, b, ((1,), (0,)), exact)


def _hdot_nt(a, b, exact=None):
    return _dot_split(a, b, ((1,), (1,)), exact)


def _hdot_tn(a, b, exact=None):
    return _dot_split(a, b, ((0,), (0,)), exact)


def _sigmoid(x):
    return 1.0 / (1.0 + jnp.exp(-x))


def _silu(x):
    return x * _sigmoid(x)


def _silu_and_grad(x):
    s = _sigmoid(x)
    return x * s, s * (1.0 + x * (1.0 - s))


def _softplus(x):
    return jnp.maximum(x, 0.0) + jnp.log(1.0 + jnp.exp(-jnp.abs(x)))


def _iota(shape, dim):
    return lax.broadcasted_iota(jnp.int32, shape, dim)


def _lane_col(x, k):
    return jnp.sum(jnp.where(_iota(x.shape, 1) == k, x, 0.0), axis=1, keepdims=True)


def _params(sem=None):
    return pltpu.CompilerParams(dimension_semantics=sem, vmem_limit_bytes=VMEM_LIMIT)


def _t5_bucket(dist):
    max_exact = N_BUCKETS // 2
    d = np.maximum(dist, 1).astype(np.float64)
    large = max_exact + (np.log(d / max_exact) / math.log(MAX_DISTANCE / max_exact)
                         * (N_BUCKETS - max_exact)).astype(np.int32)
    large = np.minimum(large, N_BUCKETS - 1)
    return np.where(dist < max_exact, dist, large).astype(np.int32)


def _bucket_tables():
    qi = np.arange(BLK)[:, None]
    kj = np.arange(2 * BLK)[None, :]
    step = qi - kj + BLK
    band = (step >= 0) & (step <= BLK)
    out = []
    for _, r in PATTERNS:
        b = _t5_bucket(np.clip(step, 0, None) * r)
        out.append(np.where(band, b, -1))
    return np.stack(out).astype(np.int32)


def _group_mats():
    g = np.zeros((D_ATT, 128), np.float32)
    for h in range(ATT_HEADS):
        g[h * HD:(h + 1) * HD, h] = 1.0
    return g, np.ascontiguousarray(g.T)


CHIP_FLIPS = ((1, 0), (0, 1), (1, 1))
ANY_SPEC = pl.BlockSpec(memory_space=pl.ANY)
MESH_ID = pl.DeviceIdType.MESH


def _other_chips():
    x, y = lax.axis_index("x"), lax.axis_index("y")
    return [((1 - x if fx else x), (1 - y if fy else y)) for fx, fy in CHIP_FLIPS]


def _gather_plan(ins, outs, send, recv, loc):
    n = len(ins)
    x, y, c = (lax.axis_index(a) for a in AXES)
    sib = (x, y, 1 - c)
    chips = _other_chips()
    lin = lambda px, py, pc: 4 * px + 2 * py + pc

    def copy(a, k, block, to, src=None):
        slot = outs[a].at[lin(*block)]
        return pltpu.make_async_remote_copy(src_ref=slot if src is None else src, dst_ref=slot,
                                            send_sem=send.at[a, k], recv_sem=recv.at[a, k],
                                            device_id=to, device_id_type=MESH_ID)

    mine = [pltpu.make_async_copy(ins[a], outs[a].at[lin(x, y, c)], loc.at[a]) for a in range(n)]
    firsts = []
    for a in range(n):
        firsts.append(copy(a, 0, (x, y, c), sib, src=ins[a]))
        firsts += [copy(a, 1 + j, (x, y, c), (*chip, c), src=ins[a]) for j, chip in enumerate(chips)]

    def begin():
        for cp in mine + firsts:
            cp.start()

    def finish():
        passed = []
        for j, chip in enumerate(chips):
            for a in range(n):
                copy(a, 1 + j, (*chip, c), (x, y, c)).wait_recv()
                fw = copy(a, 4 + j, (*chip, c), sib)
                fw.start()
                passed.append(fw)
        for a in range(n):
            copy(a, 0, sib, (x, y, c)).wait_recv()
            for j, chip in enumerate(chips):
                copy(a, 4 + j, (*chip, 1 - c), (x, y, c)).wait_recv()
        for cp in firsts + passed:
            cp.wait_send()
        for cp in mine:
            cp.wait()

    return begin, finish


GATHER_SEMS = lambda n: [pltpu.SemaphoreType.DMA((n, 7)), pltpu.SemaphoreType.DMA((n, 7)), pltpu.SemaphoreType.DMA((n,))]


def _swap_siblings(arrs):
    n = len(arrs)

    def body(*refs):
        ins, outs = refs[:n], refs[n:2 * n]
        send, recv = refs[2 * n:]
        x, y, c = (lax.axis_index(a) for a in AXES)
        cps = [pltpu.make_async_remote_copy(src_ref=ins[a].at[1 - c], dst_ref=outs[a], send_sem=send.at[a],
                                            recv_sem=recv.at[a], device_id=(x, y, 1 - c), device_id_type=MESH_ID)
               for a in range(n)]
        for cp in cps:
            cp.start()
        for cp in cps:
            cp.wait()

    return pl.pallas_call(
        body, name="swap_siblings", out_shape=[jax.ShapeDtypeStruct(a.shape[1:], a.dtype) for a in arrs],
        in_specs=[ANY_SPEC] * n, out_specs=[ANY_SPEC] * n,
        scratch_shapes=[pltpu.SemaphoreType.DMA((n,)), pltpu.SemaphoreType.DMA((n,))],
    )(*arrs)


def _chip_sum(mine2, theirs, core, wire, name):
    _, nchip, r, cdim = mine2.shape
    tr = r if r <= 1024 else 1024

    def body(core_ref, a_ref, b_ref, o_ref):
        del core_ref
        o_ref[...] = (a_ref[...].astype(F32) + b_ref[...].astype(F32)).astype(wire)

    grid_spec = pltpu.PrefetchScalarGridSpec(
        num_scalar_prefetch=1, grid=(nchip, r // tr),
        in_specs=[pl.BlockSpec((None, None, tr, cdim), lambda j, i, cr: (cr[0], j, i, 0)),
                  pl.BlockSpec((None, tr, cdim), lambda j, i, cr: (j, i, 0))],
        out_specs=pl.BlockSpec((None, tr, cdim), lambda j, i, cr: (j, i, 0)))
    return pl.pallas_call(
        body, name=name, grid_spec=grid_spec, out_shape=jax.ShapeDtypeStruct((nchip, r, cdim), wire),
        compiler_params=_params(("arbitrary", "arbitrary")),
    )(core, mine2, theirs)


def _chip_swap_copies(ins, outs, send, recv, loc):
    x, y, c = (lax.axis_index(a) for a in AXES)
    me = 2 * x + y
    starts, arrivals, drains = [], [], []
    for a in range(len(ins)):
        lc = pltpu.make_async_copy(ins[a].at[me], outs[a].at[me], loc.at[a])
        starts.append(lc.start)
        drains.append(lc.wait)
        for j, (px, py) in enumerate(_other_chips()):
            them = 2 * px + py
            cp = pltpu.make_async_remote_copy(src_ref=ins[a].at[them], dst_ref=outs[a].at[me], send_sem=send.at[a, j],
                                              recv_sem=recv.at[a, j], device_id=(px, py, c), device_id_type=MESH_ID)
            landing = pltpu.make_async_remote_copy(src_ref=ins[a].at[them], dst_ref=outs[a].at[them],
                                                   send_sem=send.at[a, j], recv_sem=recv.at[a, j],
                                                   device_id=(px, py, c), device_id_type=MESH_ID)
            starts.append(cp.start)
            arrivals.append(landing.wait_recv)
            drains.append(cp.wait_send)
    return starts, arrivals + drains


def _share_small(pack):
    def body(in_ref, out_ref, send, recv, loc):
        x, y, c = (lax.axis_index(a) for a in AXES)
        me = 4 * x + 2 * y + c
        lc = pltpu.make_async_copy(in_ref, out_ref.at[me], loc.at[0])
        lc.start()
        sends, arrivals = [], []
        for k in range(1, N_DEV):
            px = 1 - x if k & 4 else x
            py = 1 - y if k & 2 else y
            pc = 1 - c if k & 1 else c
            cp = pltpu.make_async_remote_copy(src_ref=in_ref, dst_ref=out_ref.at[me], send_sem=send.at[k - 1],
                                              recv_sem=recv.at[k - 1], device_id=(px, py, pc), device_id_type=MESH_ID)
            cp.start()
            sends.append(cp)
            arrivals.append(pltpu.make_async_remote_copy(src_ref=in_ref, dst_ref=out_ref.at[4 * px + 2 * py + pc],
                                                         send_sem=send.at[k - 1], recv_sem=recv.at[k - 1],
                                                         device_id=(px, py, pc), device_id_type=MESH_ID))
        for cp in arrivals:
            cp.wait_recv()
        for cp in sends:
            cp.wait_send()
        lc.wait()

    return pl.pallas_call(
        body, name="share_small", out_shape=jax.ShapeDtypeStruct((N_DEV,) + pack.shape, pack.dtype),
        in_specs=[ANY_SPEC], out_specs=ANY_SPEC,
        scratch_shapes=[pltpu.SemaphoreType.DMA((N_DEV - 1,)), pltpu.SemaphoreType.DMA((N_DEV - 1,)),
                        pltpu.SemaphoreType.DMA((1,))],
    )(pack)


W_PARTS = ((0, 0, 2048), (2048, 4096, 8), (2056, 2048, 2048))
SHARD_COLS = 513


def _pieces(lo, hi, parts):
    out = []
    for ref_start, tgt_start, width in parts:
        a, b = max(lo, ref_start), min(hi, ref_start + width)
        if a < b:
            out.append((a - lo, tgt_start + a - ref_start, b - a))
    return out


def _build_w(win8):
    tr = 512

    def body(in_ref, w_ref, wt_ref):
        w_ref[:, 4096:W_COLS] = jnp.zeros((tr, W_COLS - 4096), MXU)
        for p in range(N_DEV):
            for src, dst, width in _pieces(p * SHARD_COLS, (p + 1) * SHARD_COLS, W_PARTS):
                w_ref[:, dst:dst + width] = in_ref[p, :, src:src + width]
        for k in range(W_COLS // 128):
            wt_ref[k * 128:(k + 1) * 128, :] = w_ref[:, k * 128:(k + 1) * 128].astype(F32).T.astype(MXU)

    return pl.pallas_call(
        body, name="build_w", grid=(D_MODEL // tr,),
        in_specs=[pl.BlockSpec((N_DEV, tr, SHARD_COLS), lambda i: (0, i, 0))],
        out_specs=[pl.BlockSpec((tr, W_COLS), lambda i: (i, 0)), pl.BlockSpec((W_COLS, tr), lambda i: (0, i))],
        out_shape=[jax.ShapeDtypeStruct((D_MODEL, W_COLS), MXU), jax.ShapeDtypeStruct((W_COLS, D_MODEL), MXU)],
        compiler_params=_params(("arbitrary",)),
    )(win8)


def _build_slabs(secs):
    tr = 512
    parts = ((0, 0, 1536), (1536, 1, 512), (2048, 6, 8), (2056, 2, 512), (2568, 3, 512), (3080, 4, 512),
             (3592, 5, 512))

    def body(*refs):
        o_ref = refs[len(secs)]
        for p in range(N_DEV):
            lo, hi = p * SHARD_COLS, (p + 1) * SHARD_COLS
            for ref_start, idx, width in parts:
                a, b = max(lo, ref_start), min(hi, ref_start + width)
                if a < b:
                    o_ref[p % 2, p // 2, :, a - lo:b - lo] = refs[idx][:, a - ref_start:b - ref_start]

    return pl.pallas_call(
        body, name="build_slabs", grid=(D_MODEL // tr,),
        in_specs=[pl.BlockSpec((tr, s.shape[1]), lambda i: (i, 0)) for s in secs],
        out_specs=pl.BlockSpec((2, 4, tr, SHARD_COLS), lambda i: (0, 0, i, 0)),
        out_shape=jax.ShapeDtypeStruct((2, 4, D_MODEL, SHARD_COLS), secs[0].dtype),
        compiler_params=_params(("arbitrary",)),
    )(*secs)


def _fill_bias(rb_ref, bk_ref, bias_ref, pair):
    for p in range(len(PATTERNS)):
        bk_p = bk_ref[p]
        for hh in range(2):
            head = 2 * pair + hh
            bm = jnp.full((BLK, 2 * BLK), NEG, F32)
            for b in range(N_BUCKETS):
                bm = jnp.where(bk_p == b, rb_ref[head, b], bm)
            bias_ref[p, hh * BLK:(hh + 1) * BLK, :] = bm


def _norm_and_gather(x, nw, shards, rb, bk):
    t = x.shape[0]
    tm = 512
    nsteps = t // tm
    n = len(shards)
    npair = ATT_HEADS // 2
    assert nsteps >= npair

    def body(x_ref, nw_ref, *rest):
        ins, (rb_ref, bk_ref, h_ref, ht_ref, bias_ref) = rest[:n], rest[n:n + 5]
        outs, sems = rest[n + 5:2 * n + 5], rest[2 * n + 5:]
        begin, finish = _gather_plan(ins, outs, *sems)
        step = pl.program_id(0)

        @pl.when(step == 0)
        def _():
            begin()

        @pl.when(step == nsteps - 1)
        def _():
            finish()

        @pl.when(step < npair)
        def _():
            _fill_bias(rb_ref, bk_ref, bias_ref, step)

        xv = x_ref[...]
        rstd = lax.rsqrt(jnp.mean(xv * xv, axis=-1, keepdims=True) + EPS)
        hf = xv * rstd * nw_ref[...]
        h_ref[...] = hf.astype(MXU)
        ht_ref[...] = hf.T.astype(MXU)

    res = pl.pallas_call(
        body, name="norm_and_gather", grid=(nsteps,),
        in_specs=[pl.BlockSpec((tm, D_MODEL), lambda i: (i, 0)), pl.BlockSpec(nw.shape, lambda i: (0, 0))]
                 + [ANY_SPEC] * n
                 + [pl.BlockSpec(memory_space=pltpu.SMEM), pl.BlockSpec(bk.shape, lambda i: (0, 0, 0))],
        out_specs=[pl.BlockSpec((tm, D_MODEL), lambda i: (i, 0)), pl.BlockSpec((D_MODEL, tm), lambda i: (0, i)),
                   pl.BlockSpec((len(PATTERNS), None, 2 * BLK, 2 * BLK), lambda i: (0, jnp.minimum(i, npair - 1), 0, 0))]
                  + [ANY_SPEC] * n,
        out_shape=[jax.ShapeDtypeStruct((t, D_MODEL), MXU), jax.ShapeDtypeStruct((D_MODEL, t), MXU),
                   jax.ShapeDtypeStruct((len(PATTERNS), npair, 2 * BLK, 2 * BLK), F32)]
                  + [jax.ShapeDtypeStruct((N_DEV,) + a.shape, a.dtype) for a in shards],
        scratch_shapes=GATHER_SEMS(n),
        compiler_params=_params(("arbitrary",)),
    )(x, nw, *shards, rb, bk)
    return res[0], res[1], res[2], res[3:]


def _inproj(h_all, w, cw, wout_shard):
    t = h_all.shape[0]
    tm = 512
    nsteps = t // tm

    def body(h_ref, w_ref, cw_ref, wo_ref, pdn_ref, qkv_ref, z_ref, patt_ref, gate_ref, ba_ref,
             wo8_ref, halo_ref, send, recv, loc):
        begin, finish = _gather_plan([wo_ref], [wo8_ref], send, recv, loc)

        @pl.when(pl.program_id(0) == 0)
        def _():
            begin()
            halo_ref[...] = jnp.zeros_like(halo_ref)

        @pl.when(pl.program_id(0) == nsteps - 1)
        def _():
            finish()

        h = h_ref[...]
        for ref, lo, hi in ((z_ref, 1536, 2048), (patt_ref, 2048, 3584), (gate_ref, 3584, 4096), (ba_ref, 4096, 4224)):
            ref[...] = jnp.dot(h, w_ref[:, lo:hi], preferred_element_type=F32)
        pdn = jnp.dot(h, w_ref[:, 0:3 * D_DN], preferred_element_type=F32)
        pdn_ref[...] = pdn
        _dn_prep_tile(pdn, halo_ref, cw_ref, qkv_ref)

    row = lambda n: pl.BlockSpec((tm, n), lambda i: (i, 0))
    full = lambda a: pl.BlockSpec(a.shape, lambda i: (0,) * a.ndim)
    return pl.pallas_call(
        body, name="inproj", grid=(nsteps,),
        in_specs=[row(D_MODEL), full(w), full(cw), ANY_SPEC],
        out_specs=[row(1536), row(1536), row(512), row(1536), row(512), row(128), ANY_SPEC],
        out_shape=[jax.ShapeDtypeStruct((t, n), F32) for n in (1536, 1536, 512, 1536, 512, 128)] +
                  [jax.ShapeDtypeStruct((N_DEV,) + wout_shard.shape, wout_shard.dtype)],
        scratch_shapes=[pltpu.VMEM((8, 3 * D_DN), F32)] + GATHER_SEMS(1),
        compiler_params=_params(("arbitrary",)),
    )(h_all, w, cw, wout_shard)


CONV_ROWS = 512


def _conv_taps(u_ref, c, w_ref):
    r0 = c * CONV_ROWS
    if c == 0:
        ext = jnp.concatenate([jnp.zeros((8, 128), F32), u_ref[0:CONV_ROWS, :]], axis=0)
    else:
        ext = u_ref[r0 - 8:r0 + CONV_ROWS, :]
    taps = [ext[8:, :]] + [pltpu.roll(ext, k, 0)[8:, :] for k in (1, 2, 3)]
    y = taps[0] * w_ref[3:4, :]
    for k in (1, 2, 3):
        y = y + taps[k] * w_ref[3 - k:4 - k, :]
    return taps, y


def _dn_prep_tile(pdn, halo_ref, cw_ref, out_ref):
    rows = pdn.shape[0]
    ext = jnp.concatenate([halo_ref[...], pdn], axis=0)
    halo_ref[...] = pdn[rows - 8:, :]
    for j in range(3 * D_DN // 128):
        cols = slice(j * 128, (j + 1) * 128)
        e = ext[:, cols]
        y = e[8:, :] * cw_ref[3:4, cols]
        for k in (1, 2, 3):
            y = y + pltpu.roll(e, k, 0)[8:, :] * cw_ref[3 - k:4 - k, cols]
        a = _silu(y)
        if j < 2 * DN_HEADS:
            a = a * lax.rsqrt(jnp.sum(a * a, axis=1, keepdims=True) + EPS)
        if j < DN_HEADS:
            a = a * DK ** -0.5
        out_ref[:, cols] = a


def _chunk_common(qkv, ba, arow, dtb):
    c = CHUNK
    ri, ci = _iota((c, c), 0), _iota((c, c), 1)
    lane = _iota((c, 128), 1)
    g_all = jnp.where((lane >= DN_HEADS) & (lane < 2 * DN_HEADS), arow * _softplus(ba + dtb), 0.0)
    gc_all = _hdot((ri >= ci).astype(F32), g_all, "a")
    gc_t = gc_all.T
    beta_all = _sigmoid(ba)
    out = []
    for h in range(DN_HEADS):
        gc = _lane_col(gc_all, DN_HEADS + h)
        gcr = gc_t[DN_HEADS + h:DN_HEADS + h + 1, :]
        gl = gc[c - 1:c, :]
        out.append(dict(
            q=qkv[:, h * DK:(h + 1) * DK], k=qkv[:, D_DN + h * DK:D_DN + (h + 1) * DK],
            v=qkv[:, 2 * D_DN + h * DK:2 * D_DN + (h + 1) * DK],
            beta=_lane_col(beta_all, h), g=_lane_col(g_all, DN_HEADS + h),
            a_raw=_lane_col(ba, DN_HEADS + h), a_h=_lane_col(arow, DN_HEADS + h), dt_h=_lane_col(dtb, DN_HEADS + h),
            decay=jnp.exp(jnp.where(ri >= ci, gc - gcr, NEG)), eg=jnp.exp(gc), egl=jnp.exp(gl), etail=jnp.exp(gl - gc)))
    return out, ri, ci


SCAN_CHUNKS = 8


def _dn_scan_fwd(qkv, ba, z, arow, dtb, dnw):
    t = qkv.shape[0]
    n = t // CHUNK
    c = CHUNK
    cps = SCAN_CHUNKS
    hs = range(DN_HEADS)
    chains = [(j, h) for j in range(cps) for h in hs]

    def body(qkv_ref, ba_ref, z_ref, arow_ref, dtb_ref, dnw_ref, o_ref, y_ref, sh_ref, th_ref, s_ref):
        @pl.when(pl.program_id(0) == 0)
        def _():
            s_ref[...] = jnp.zeros_like(s_ref)

        ms = {}
        for j in range(cps):
            rows = slice(j * c, (j + 1) * c)
            mj, ri, ci = _chunk_common(qkv_ref[rows, :], ba_ref[rows, :], arow_ref[...], dtb_ref[...])
            for h in hs:
                ms[j, h] = mj[h]
        kb = {x: ms[x]["k"] * ms[x]["beta"] for x in chains}
        amat = {x: jnp.where(ri > ci, _dot_nt(kb[x], ms[x]["k"]) * ms[x]["decay"], 0.0) for x in chains}
        attn = {x: jnp.where(ri >= ci, _dot_nt(ms[x]["q"], ms[x]["k"]) * ms[x]["decay"], 0.0) for x in chains}
        tinv = _wy_inverses(amat, (ri == ci).astype(F32))
        uw = {x: _hdot(tinv[x], jnp.concatenate([ms[x]["v"] * ms[x]["beta"], kb[x] * ms[x]["eg"]], axis=1))
              for x in chains}
        u = {x: uw[x][:, :DK] for x in chains}
        w = {x: uw[x][:, DK:] for x in chains}
        q_dec = {x: ms[x]["q"] * ms[x]["eg"] for x in chains}
        k_tail = {x: ms[x]["k"] * ms[x]["etail"] for x in chains}
        s = [s_ref[h] for h in hs]
        for j in range(cps):
            rows = slice(j * c, (j + 1) * c)
            v_new = [u[j, h] - _dot(w[j, h], s[h]) for h in hs]
            o = [_dot(q_dec[j, h], s[h]) + _dot(attn[j, h], v_new[h]) for h in hs]
            for h in hs:
                sh_ref[j, h] = s[h]
                th_ref[j, h] = tinv[j, h]
            s = [s[h] * ms[j, h]["egl"] + _dot_tn(k_tail[j, h], v_new[h]) for h in hs]
            for h in hs:
                cols = slice(h * DK, (h + 1) * DK)
                o_ref[rows, cols] = o[h]
                rs = lax.rsqrt(jnp.mean(o[h] * o[h], axis=1, keepdims=True) + EPS)
                y_ref[rows, cols] = (o[h] * rs * dnw_ref[...] * _silu(z_ref[rows, cols])).astype(MXU)
        for h in hs:
            s_ref[h] = s[h]

    row = lambda w_: pl.BlockSpec((cps * c, w_), lambda i: (i, 0))
    one = pl.BlockSpec((1, 128), lambda i: (0, 0))
    return pl.pallas_call(
        body, name="dn_scan_fwd", grid=(n // cps,),
        in_specs=[row(1536), row(128), row(512), one, one, one],
        out_specs=[row(512), row(512), pl.BlockSpec((cps, DN_HEADS, DK, DK), lambda i: (i, 0, 0, 0)),
                   pl.BlockSpec((cps, DN_HEADS, c, c), lambda i: (i, 0, 0, 0))],
        out_shape=[jax.ShapeDtypeStruct((t, 512), F32), jax.ShapeDtypeStruct((t, 512), MXU),
                   jax.ShapeDtypeStruct((n, DN_HEADS, DK, DK), F32), jax.ShapeDtypeStruct((n, DN_HEADS, c, c), F32)],
        scratch_shapes=[pltpu.VMEM((DN_HEADS, DK, DK), F32)],
        compiler_params=_params(("arbitrary",)),
    )(qkv, ba, z, arow, dtb, dnw)


ATT_ROWS = 512


def _pair_rstd(xv, g2_ref):
    return lax.rsqrt(_hdot(xv * xv, g2_ref[...], "b") * (1.0 / HD) + EPS)


def _pair_norm(t, raw_refs, w_refs, out_refs, g2_ref):
    for c in range(t // ATT_ROWS):
        sl = slice(c * ATT_ROWS, (c + 1) * ATT_ROWS)
        for raw, w_ref, out in zip(raw_refs, w_refs, out_refs):
            xv = raw[sl, :]
            out[sl, :] = xv * _pair_rstd(xv, g2_ref) * w_ref[...]


BIAS_SPEC = pl.BlockSpec((len(PATTERNS), None, 2 * BLK, 2 * BLK), lambda i: (0, i, 0, 0))
PAIR_ROW_SPEC = pl.BlockSpec((1, 128), lambda i: (0, i))


def _stack_heads(xb, h0):
    return jnp.concatenate([jnp.where(h0, xb, 0.0), jnp.where(h0, 0.0, xb)], axis=0).astype(MXU)


def _block_rows(t, r, n):
    per_class = (t // r) // BLK
    res = n // per_class
    j = n % per_class
    start = res + BLK * r * j
    pstart = res + BLK * r * jnp.maximum(j - 1, 0)
    if r == 1:
        return pl.ds(pl.multiple_of(start, BLK), BLK), pl.ds(pl.multiple_of(pstart, BLK), BLK), j
    return pl.ds(start, BLK, stride=r), pl.ds(pstart, BLK, stride=r), j


def _att_fwd(qkv, gate, bias, wq, wk, g2):
    t = qkv.shape[0]
    rows = ATT_ROWS

    def body(bias_ref, qraw_ref, kraw_ref, v_ref, g_ref, wq_ref, wk_ref, g2_ref, o_ref, y_ref, lse_ref,
             o0_ref, o1_ref, o2_ref, l0_ref, l1_ref, l2_ref, q_ref, k_ref):
        h0 = _iota((BLK, 128), 1) < HD
        prev_cols = _iota((2 * BLK, 2 * BLK), 1) < BLK
        op_refs, lp_refs = (o0_ref, o1_ref, o2_ref), (l0_ref, l1_ref, l2_ref)
        _pair_norm(t, (qraw_ref, kraw_ref), (wq_ref, wk_ref), (q_ref, k_ref), g2_ref)

        for p, (_, r) in enumerate(PATTERNS):
            def blk(n, carry, p=p, r=r):
                cur, prev, j = _block_rows(t, r, n)
                q2 = _stack_heads(q_ref[cur, :], h0)
                k2 = jnp.concatenate([k_ref[prev, :], k_ref[cur, :]], axis=0).astype(MXU)
                v2 = jnp.concatenate([v_ref[prev, :], v_ref[cur, :]], axis=0).astype(MXU)
                s = _dot_nt(q2, k2) + bias_ref[p] + jnp.where(prev_cols & (j == 0), NEG, 0.0)
                m = jnp.max(s, axis=1, keepdims=True)
                e = jnp.exp(s - m)
                l = jnp.sum(e, axis=1, keepdims=True)
                pv = _dot(e, v2) / l
                lse = m + jnp.log(l)
                op_refs[p][cur, :] = jnp.where(h0, pv[:BLK], pv[BLK:])
                lp_refs[p][cur, :] = jnp.where(h0, lse[:BLK], lse[BLK:])
                return carry

            lax.fori_loop(0, t // BLK, blk, 0, unroll=16)

        for c in range(t // rows):
            sl = slice(c * rows, (c + 1) * rows)
            ls = [ref[sl, :] for ref in lp_refs]
            mx = jnp.maximum(jnp.maximum(ls[0], ls[1]), ls[2])
            ws = [jnp.exp(v_ - mx) for v_ in ls]
            den = ws[0] + ws[1] + ws[2]
            o = (ws[0] * o0_ref[sl, :] + ws[1] * o1_ref[sl, :] + ws[2] * o2_ref[sl, :]) / den
            o_ref[sl, :] = o
            y_ref[sl, :] = (o * _silu(g_ref[sl, :])).astype(MXU)
            lse_ref[sl, :] = mx + jnp.log(den)

    col = lambda off: pl.BlockSpec((t, 128), lambda i, off=off: (0, off + i))
    return pl.pallas_call(
        body, name="att_fwd", grid=(ATT_HEADS // 2,),
        in_specs=[BIAS_SPEC, col(0), col(4), col(8), col(0), PAIR_ROW_SPEC, PAIR_ROW_SPEC,
                  pl.BlockSpec((128, 128), lambda i: (0, 0))],
        out_specs=[col(0), col(0), col(0)],
        out_shape=[jax.ShapeDtypeStruct((t, 512), dt_) for dt_ in (F32, MXU, F32)],
        scratch_shapes=[pltpu.VMEM((t, 128), F32)] * 8,
        compiler_params=_params(("arbitrary",)),
    )(bias, qkv, qkv, qkv, gate, wq, wk, g2)


def _outproj_loss(x, ydn, yatt, wout, target):
    t = x.shape[0]
    tm = 512

    def body(x_ref, a_ref, b_ref, w_ref, t_ref, dy_ref, mix_ref, loss_ref):
        @pl.when(pl.program_id(0) == 0)
        def _():
            loss_ref[...] = jnp.zeros_like(loss_ref)

        mix = jnp.concatenate([a_ref[...], b_ref[...]], axis=1)
        mix_ref[...] = mix.astype(F32).T.astype(MXU)
        err = x_ref[...] + jnp.dot(mix, w_ref[...], preferred_element_type=F32) - t_ref[...]
        dy_ref[...] = err * (1.0 / D_MODEL)
        loss_ref[...] += jnp.sum(err * err) * (0.5 / D_MODEL)

    row = lambda n: pl.BlockSpec((tm, n), lambda i: (i, 0))
    return pl.pallas_call(
        body, name="outproj_loss", grid=(t // tm,),
        in_specs=[row(D_MODEL), row(512), row(512), pl.BlockSpec(wout.shape, lambda i: (0, 0)), row(D_MODEL)],
        out_specs=[row(D_MODEL), pl.BlockSpec((D_MODEL, tm), lambda i: (0, i)), pl.BlockSpec((8, 128), lambda i: (0, 0))],
        out_shape=[jax.ShapeDtypeStruct((t, D_MODEL), F32), jax.ShapeDtypeStruct((D_MODEL, t), MXU),
                   jax.ShapeDtypeStruct((8, 128), F32)],
        compiler_params=_params(("arbitrary",)),
    )(x, ydn, yatt, wout, target)


def _outproj_bwd(dy, wout_t, oraw, z, dnw, oatt, gate, g, gt):
    t = dy.shape[0]
    tm = 512

    def body(dy_ref, w_ref, o_ref, z_ref, dnw_ref, oa_ref, g_ref, grp_ref, grpt_ref,
             do_ref, dz_ref, doa_ref, dg_ref, dd_ref, ddnw_ref):
        @pl.when(pl.program_id(0) == 0)
        def _():
            ddnw_ref[...] = jnp.zeros_like(ddnw_ref)

        dmix = jnp.dot(dy_ref[...].astype(MXU), w_ref[...], preferred_element_type=F32)
        dnw_v = dnw_ref[...]
        acc = jnp.zeros((1, DK), F32)
        for h in range(DN_HEADS):
            sl = slice(h * DK, (h + 1) * DK)
            o, zz, dm = o_ref[:, sl], z_ref[:, sl], dmix[:, sl]
            rs = lax.rsqrt(jnp.mean(o * o, axis=1, keepdims=True) + EPS)
            oh = o * rs
            silu_z, dsilu_z = _silu_and_grad(zz)
            dz_ref[:, sl] = (dm * oh * dnw_v * dsilu_z).astype(MXU)
            d_on = dm * silu_z
            gg = d_on * dnw_v
            do_ref[:, sl] = rs * (gg - oh * jnp.mean(gg * oh, axis=1, keepdims=True))
            acc = acc + jnp.sum(d_on * oh, axis=0, keepdims=True)
        ddnw_ref[...] += jnp.broadcast_to(acc, (8, DK))
        da, gate_v, oa = dmix[:, 512:], g_ref[...], oa_ref[...]
        silu_g, dsilu_g = _silu_and_grad(gate_v)
        doa = da * silu_g
        doa_ref[...] = doa
        dg_ref[...] = (da * oa * dsilu_g).astype(MXU)
        dd_ref[...] = _hdot(_hdot(doa * oa, grp_ref[...], "b"), grpt_ref[...], "b")

    row = lambda n: pl.BlockSpec((tm, n), lambda i: (i, 0))
    full = lambda a: pl.BlockSpec(a.shape, lambda i: (0,) * a.ndim)
    return pl.pallas_call(
        body, name="outproj_bwd", grid=(t // tm,),
        in_specs=[row(D_MODEL), full(wout_t), row(512), row(512), full(dnw), row(512), row(512), full(g), full(gt)],
        out_specs=[row(512)] * 5 + [pl.BlockSpec((8, DK), lambda i: (0, 0))],
        out_shape=[jax.ShapeDtypeStruct((t, 512), dt_) for dt_ in (F32, MXU, F32, MXU, F32)]
                  + [jax.ShapeDtypeStruct((8, DK), F32)],
        compiler_params=_params(("arbitrary",)),
    )(dy, wout_t, oraw, z, dnw, oatt, gate, g, gt)


def _grad_matmul(at, b, name):
    m, t = at.shape
    n = b.shape[1]
    tk = 1024
    tn = n if n <= 1536 else 512
    nk = t // tk

    def body(a_ref, b_ref, o_ref, acc_ref):
        k = pl.program_id(1)

        @pl.when(k == 0)
        def _():
            acc_ref[...] = jnp.zeros_like(acc_ref)

        acc_ref[...] += jnp.dot(a_ref[...], b_ref[...].astype(MXU), preferred_element_type=F32)

        @pl.when(k == nk - 1)
        def _():
            o_ref[...] = acc_ref[...].astype(GRAD_WIRE)

    return pl.pallas_call(
        body, name=name, grid=(n // tn, nk),
        in_specs=[pl.BlockSpec((m, tk), lambda j, k: (0, k)), pl.BlockSpec((tk, tn), lambda j, k: (k, j))],
        out_specs=pl.BlockSpec((m, tn), lambda j, k: (0, j)),
        out_shape=jax.ShapeDtypeStruct((m, n), GRAD_WIRE),
        scratch_shapes=[pltpu.VMEM((m, tn), F32)],
        compiler_params=_params(("arbitrary", "arbitrary")),
    )(at, b)


def _grad_matmul_many(at, bs, name):
    m, t = at.shape
    n = bs[0].shape[1]
    nb = len(bs)
    tk = 1024
    nk = t // tk

    def body(a_ref, *refs):
        b_refs, o_refs, acc_ref = refs[:nb], refs[nb:2 * nb], refs[2 * nb]
        s, k = pl.program_id(0), pl.program_id(1)

        @pl.when(k == 0)
        def _():
            acc_ref[...] = jnp.zeros_like(acc_ref)

        for i in range(nb):
            @pl.when(s == i)
            def _(i=i):
                acc_ref[...] += jnp.dot(a_ref[...], b_refs[i][...].astype(MXU), preferred_element_type=F32)

                @pl.when(k == nk - 1)
                def _():
                    o_refs[i][...] = acc_ref[...].astype(GRAD_WIRE)

    def b_spec(i):
        return pl.BlockSpec((tk, n), lambda s, k: (jnp.where(s == i, k, jnp.where(s < i, 0, nk - 1)), 0))

    return pl.pallas_call(
        body, name=name, grid=(nb, nk),
        in_specs=[pl.BlockSpec((m, tk), lambda s, k: (0, k))] + [b_spec(i) for i in range(nb)],
        out_specs=[pl.BlockSpec((m, n), lambda s, k: (0, 0))] * nb,
        out_shape=[jax.ShapeDtypeStruct((m, n), GRAD_WIRE)] * nb,
        scratch_shapes=[pltpu.VMEM((m, n), F32)],
        compiler_params=_params(("arbitrary", "arbitrary")),
    )(at, *bs)


def _att_bwd(qkv, do, lse, dd, bias, bk, wq, wk, g2):
    t = qkv.shape[0]
    rows = ATT_ROWS

    def body(bias_ref, bk_ref, qraw_ref, kraw_ref, v_ref, do_ref, lse_ref, dd_ref, wq_ref, wk_ref, g2_ref,
             dq_out, dk_out, dv_out, db_ref, dwq_ref, dwk_ref, ds_ref, q_ref, k_ref, dq_ref, dk_ref, dv_ref):
        pair = pl.program_id(0)

        @pl.when(pair == 0)
        def _():
            db_ref[...] = jnp.zeros_like(db_ref)

        _pair_norm(t, (qraw_ref, kraw_ref), (wq_ref, wk_ref), (q_ref, k_ref), g2_ref)
        ds_ref[...] = jnp.zeros_like(ds_ref)
        for c in range(t // rows):
            sl = slice(c * rows, (c + 1) * rows)
            for ref in (dq_ref, dk_ref, dv_ref):
                ref[sl, :] = jnp.zeros((rows, 128), F32)
        h0 = _iota((BLK, 128), 1) < HD
        prev_cols = _iota((2 * BLK, 2 * BLK), 1) < BLK

        def rows_of(xb):
            return jnp.concatenate([xb[:, 0:1], xb[:, HD:HD + 1]], axis=0)

        for p, (_, r) in enumerate(PATTERNS):
            def blk(n, carry, p=p, r=r):
                cur, prev, j = _block_rows(t, r, n)
                q2, do2 = _stack_heads(q_ref[cur, :], h0), _stack_heads(do_ref[cur, :], h0)
                k2 = jnp.concatenate([k_ref[prev, :], k_ref[cur, :]], axis=0).astype(MXU)
                v2 = jnp.concatenate([v_ref[prev, :], v_ref[cur, :]], axis=0).astype(MXU)
                s = _dot_nt(q2, k2) + bias_ref[p] + jnp.where(prev_cols & (j == 0), NEG, 0.0)
                prob = jnp.exp(s - rows_of(lse_ref[cur, :]))
                ds = prob * (_dot_nt(do2, v2) - rows_of(dd_ref[cur, :]))
                ds_ref[p] += ds
                dq2 = _dot(ds, k2)
                dk2 = _dot_tn(ds, q2)
                dv2 = _dot_tn(prob, do2)
                dq_ref[cur, :] += jnp.where(h0, dq2[:BLK], dq2[BLK:])
                dk_ref[prev, :] += dk2[:BLK]
                dv_ref[prev, :] += dv2[:BLK]
                dk_ref[cur, :] += dk2[BLK:]
                dv_ref[cur, :] += dv2[BLK:]
                return carry

            lax.fori_loop(0, t // BLK, blk, 0, unroll=8)

        ri, ci = _iota((8, 128), 0), _iota((8, 128), 1)
        upd = jnp.zeros((8, 128), F32)
        for p in range(len(PATTERNS)):
            bk = bk_ref[p]
            for hh in range(2):
                dsum = ds_ref[p, hh * BLK:(hh + 1) * BLK, :]
                for b in range(N_BUCKETS):
                    val = jnp.sum(jnp.where(bk == b, dsum, 0.0))
                    upd = upd + jnp.where((ri == 2 * pair + hh) & (ci == b), val, 0.0)
        db_ref[...] += upd

        for raw, d_ref, out, w_ref, dw_ref in ((qraw_ref, dq_ref, dq_out, wq_ref, dwq_ref),
                                               (kraw_ref, dk_ref, dk_out, wk_ref, dwk_ref)):
            acc = jnp.zeros((1, 128), F32)
            for c in range(t // rows):
                sl = slice(c * rows, (c + 1) * rows)
                xv, dyv = raw[sl, :], d_ref[sl, :]
                rs = _pair_rstd(xv, g2_ref)
                xh = xv * rs
                gg = dyv * w_ref[...]
                mean = _hdot(gg * xh, g2_ref[...], "b") * (1.0 / HD)
                out[sl, :] = (rs * (gg - xh * mean)).astype(MXU)
                acc = acc + jnp.sum(dyv * xh, axis=0, keepdims=True)
            dw_ref[...] = jnp.broadcast_to(acc, (8, 128))
        for c in range(t // rows):
            sl = slice(c * rows, (c + 1) * rows)
            dv_out[sl, :] = dv_ref[sl, :].astype(MXU)

    col = lambda off: pl.BlockSpec((t, 128), lambda i, off=off: (0, off + i))
    acc8 = pl.BlockSpec((8, 128), lambda i: (0, i))
    return pl.pallas_call(
        body, name="att_bwd", grid=(ATT_HEADS // 2,),
        in_specs=[BIAS_SPEC, pl.BlockSpec(bk.shape, lambda i: (0, 0, 0)),
                  col(0), col(4), col(8), col(0), col(0), col(0), PAIR_ROW_SPEC, PAIR_ROW_SPEC,
                  pl.BlockSpec((128, 128), lambda i: (0, 0))],
        out_specs=[col(0), col(0), col(0), pl.BlockSpec((8, 128), lambda i: (0, 0)), acc8, acc8],
        out_shape=[jax.ShapeDtypeStruct((t, 512), MXU)] * 3 + [jax.ShapeDtypeStruct((8, 128), F32)]
                  + [jax.ShapeDtypeStruct((8, 512), F32)] * 2,
        scratch_shapes=[pltpu.VMEM((len(PATTERNS), 2 * BLK, 2 * BLK), F32)] + [pltpu.VMEM((t, 128), F32)] * 5,
        compiler_params=_params(("arbitrary",)),
    )(bias, bk, qkv, qkv, qkv, do, lse, dd, wq, wk, g2)


def _dn_scan_bwd(qkv, ba, do, sh, th, arow, dtb):
    t = qkv.shape[0]
    n = t // CHUNK
    c = CHUNK
    cps = SCAN_CHUNKS

    def body(qkv_ref, ba_ref, do_ref, sh_ref, th_ref, arow_ref, dtb_ref, dqkv_ref, dba_ref, ds_ref):
        @pl.when(pl.program_id(0) == 0)
        def _():
            ds_ref[...] = jnp.zeros_like(ds_ref)

        hs = range(DN_HEADS)
        chains = [(j, h) for j in range(cps) for h in hs]
        lane = _iota((c, 128), 1)
        row = _iota((c, 1), 0)
        ms = {}
        for j in range(cps):
            rows_j = slice(j * c, (j + 1) * c)
            mj, ri, ci = _chunk_common(qkv_ref[rows_j, :], ba_ref[rows_j, :], arow_ref[...], dtb_ref[...])
            for h in hs:
                ms[j, h] = mj[h]
        q, k, v = ({x: ms[x][nm] for x in chains} for nm in ("q", "k", "v"))
        beta, decay = ({x: ms[x][nm] for x in chains} for nm in ("beta", "decay"))
        eg, egl, etail = ({x: ms[x][nm] for x in chains} for nm in ("eg", "egl", "etail"))
        s = {x: sh_ref[x[0], x[1]] for x in chains}
        tinv = {x: th_ref[x[0], x[1]] for x in chains}
        d_o = {(j, h): do_ref[j * c:(j + 1) * c, h * DK:(h + 1) * DK] for j, h in chains}
        kb = {x: k[x] * beta[x] for x in chains}
        vb = {x: v[x] * beta[x] for x in chains}
        kbg = {x: kb[x] * eg[x] for x in chains}
        amat = {x: jnp.where(ri > ci, _dot_nt(kb[x], k[x]) * decay[x], 0.0) for x in chains}
        attn = {x: jnp.where(ri >= ci, _dot_nt(q[x], k[x]) * decay[x], 0.0) for x in chains}
        uw = {x: _hdot(tinv[x], jnp.concatenate([vb[x], kbg[x]], axis=1)) for x in chains}
        u = {x: uw[x][:, :DK] for x in chains}
        w = {x: uw[x][:, DK:] for x in chains}
        v_new = {x: u[x] - _dot(w[x], s[x]) for x in chains}
        q_dec = {x: q[x] * eg[x] for x in chains}
        k_tail = {x: k[x] * etail[x] for x in chains}
        d_attn = {x: jnp.where(ri >= ci, _dot_nt(d_o[x], v_new[x]), 0.0) for x in chains}
        d_qdec = {x: _dot_nt(d_o[x], s[x]) for x in chains}
        from_o = {x: _dot_tn(attn[x], d_o[x]) for x in chains}
        to_state = {x: _dot_tn(q_dec[x], d_o[x]) for x in chains}

        d_s, d_vnew = {}, {}
        cur = [ds_ref[h] for h in hs]
        for j in reversed(range(cps)):
            for h in hs:
                d_s[j, h] = cur[h]
                d_vnew[j, h] = from_o[j, h] + _dot(k_tail[j, h], cur[h])
            cur = [to_state[j, h] + cur[h] * egl[j, h] - _dot_tn(w[j, h], d_vnew[j, h]) for h in hs]
        for h in hs:
            ds_ref[h] = cur[h]

        d_ktail = {x: _dot_nt(v_new[x], d_s[x]) for x in chains}
        d_gl = {x: jnp.sum(s[x] * d_s[x]) * egl[x] for x in chains}
        d_w = {x: -_dot_nt(d_vnew[x], s[x]) for x in chains}
        d_both = {x: _hdot_tn(tinv[x], jnp.concatenate([d_vnew[x], d_w[x]], axis=1)) for x in chains}
        d_vb = {x: d_both[x][:, :DK] for x in chains}
        d_kbg = {x: d_both[x][:, DK:] for x in chains}
        d_a = {x: -jnp.where(ri > ci, _hdot_nt(d_both[x], uw[x]), 0.0) for x in chains}
        d_qk = {x: d_attn[x] * decay[x] for x in chains}
        d_kk = {x: d_a[x] * decay[x] for x in chains}
        d_kb = {x: _dot(d_kk[x], k[x]) + d_kbg[x] * eg[x] for x in chains}
        d_q = {x: _dot(d_qk[x], k[x]) + d_qdec[x] * eg[x] for x in chains}
        d_k = {x: _dot_tn(d_qk[x], q[x]) + _dot_tn(d_kk[x], kb[x]) + d_ktail[x] * etail[x] + d_kb[x] * beta[x]
               for x in chains}
        d_beta = {x: jnp.sum(d_kb[x] * k[x] + d_vb[x] * v[x], axis=1, keepdims=True) for x in chains}
        mm = {x: d_a[x] * amat[x] + d_attn[x] * attn[x] for x in chains}
        for j in range(cps):
            rows_j = slice(j * c, (j + 1) * c)
            rows = jnp.zeros((c, c), F32)
            for h in hs:
                rows = rows + jnp.where(ri == h, jnp.sum(mm[j, h], axis=0, keepdims=True), 0.0)
            cols_t = jnp.concatenate([rows, jnp.zeros((c, c), F32)], axis=1).T[:c, :]
            d_gc_all = jnp.zeros((c, 128), F32)
            for h in hs:
                x = (j, h)
                tail_term = jnp.sum(d_ktail[x] * k_tail[x], axis=1, keepdims=True)
                d_gc = (jnp.sum(mm[x], axis=1, keepdims=True) - _lane_col(cols_t, h)
                        + jnp.sum(d_qdec[x] * q_dec[x] + d_kbg[x] * kbg[x], axis=1, keepdims=True) - tail_term)
                d_gc = d_gc + jnp.where(row == c - 1, jnp.sum(tail_term) + d_gl[x], 0.0)
                d_gc_all = d_gc_all + jnp.where(lane == DN_HEADS + h, d_gc, 0.0)
            d_g_all = _hdot((ri <= ci).astype(F32), d_gc_all, "a")
            dba = jnp.zeros((c, 128), F32)
            for h in hs:
                x = (j, h)
                d_g = _lane_col(d_g_all, DN_HEADS + h)
                d_braw = d_beta[x] * beta[x] * (1.0 - beta[x])
                d_araw = d_g * ms[x]["a_h"] * _sigmoid(ms[x]["a_raw"] + ms[x]["dt_h"])
                dba = dba + jnp.where(lane == h, d_braw, 0.0) + jnp.where(lane == DN_HEADS + h, d_araw, 0.0) \
                    + jnp.where(lane == 2 * DN_HEADS + h, d_g * ms[x]["g"], 0.0)
                dqkv_ref[rows_j, h * DK:(h + 1) * DK] = d_q[x]
                dqkv_ref[rows_j, D_DN + h * DK:D_DN + (h + 1) * DK] = d_k[x]
                dqkv_ref[rows_j, 2 * D_DN + h * DK:2 * D_DN + (h + 1) * DK] = d_vb[x] * beta[x]
            dba_ref[rows_j, :] = dba

    nsteps = n // cps
    rev = lambda w_: pl.BlockSpec((cps * c, w_), lambda i: (nsteps - 1 - i, 0))
    one = pl.BlockSpec((1, 128), lambda i: (0, 0))
    return pl.pallas_call(
        body, name="dn_scan_bwd", grid=(nsteps,),
        in_specs=[rev(1536), rev(128), rev(512),
                  pl.BlockSpec((cps, DN_HEADS, DK, DK), lambda i: (nsteps - 1 - i, 0, 0, 0)),
                  pl.BlockSpec((cps, DN_HEADS, c, c), lambda i: (nsteps - 1 - i, 0, 0, 0)), one, one],
        out_specs=[rev(1536), rev(128)],
        out_shape=[jax.ShapeDtypeStruct((t, 1536), F32), jax.ShapeDtypeStruct((t, 128), F32)],
        scratch_shapes=[pltpu.VMEM((DN_HEADS, DK, DK), F32)],
        compiler_params=_params(("arbitrary",)),
    )(qkv, ba, do, sh, th, arow, dtb)


def _dn_prep_bwd(pdn, cw, dact):
    t = pdn.shape[0]
    nchunk = t // CONV_ROWS

    def body(u_ref, w_ref, d_ref, du_ref, dw_ref, dy_ref):
        j = pl.program_id(0)
        dy_ref[t:t + 8, :] = jnp.zeros((8, 128), F32)
        dw = [jnp.zeros((1, 128), F32) for _ in range(4)]
        for c in range(nchunk):
            sl = slice(c * CONV_ROWS, (c + 1) * CONV_ROWS)
            taps, y = _conv_taps(u_ref, c, w_ref)
            a, da_dy = _silu_and_grad(y)
            dout = d_ref[sl, :]
            rs = lax.rsqrt(jnp.sum(a * a, axis=1, keepdims=True) + EPS)
            f = jnp.where(j < 8, rs, 1.0) * jnp.where(j < 4, DK ** -0.5, 1.0)
            corr = jnp.where(j < 8, f * rs * rs * jnp.sum(dout * a, axis=1, keepdims=True), 0.0)
            dy = (f * dout - corr * a) * da_dy
            dy_ref[sl, :] = dy
            for k_ in range(4):
                dw[3 - k_] = dw[3 - k_] + jnp.sum(taps[k_] * dy, axis=0, keepdims=True)
        for i in range(4):
            dw_ref[i:i + 1, :] = dw[i]
        for c in range(nchunk):
            r0 = c * CONV_ROWS
            ext = dy_ref[r0:r0 + CONV_ROWS + 8, :]
            du = ext[:CONV_ROWS, :] * w_ref[3:4, :]
            for k_ in (1, 2, 3):
                du = du + pltpu.roll(ext, CONV_ROWS + 8 - k_, 0)[:CONV_ROWS, :] * w_ref[3 - k_:4 - k_, :]
            du_ref[r0:r0 + CONV_ROWS, :] = du.astype(MXU)

    return pl.pallas_call(
        body, name="dn_prep_bwd", grid=(12,),
        in_specs=[pl.BlockSpec((t, 128), lambda j: (0, j)), pl.BlockSpec((4, 128), lambda j: (0, j)),
                  pl.BlockSpec((t, 128), lambda j: (0, j))],
        out_specs=[pl.BlockSpec((t, 128), lambda j: (0, j)), pl.BlockSpec((4, 128), lambda j: (0, j))],
        out_shape=[jax.ShapeDtypeStruct((t, 1536), MXU), jax.ShapeDtypeStruct((4, 1536), F32)],
        scratch_shapes=[pltpu.VMEM((t + 8, 128), F32)],
        compiler_params=_params(("arbitrary",)),
    )(pdn, cw, dact)


SECTIONS = (("dn", 0, 1536), ("z", 1536, 512), ("q", 2048, 512), ("k", 2560, 512), ("v", 3072, 512),
            ("gate", 3584, 512), ("ba", 4096, 128))


def _inproj_bwd(x, nw, wt, dy, dsecs, partials):
    t = x.shape[0]
    tm = 256
    npart = len(partials)
    nsteps = t // tm

    nsec = len(SECTIONS)

    def body(x_ref, nw_ref, w_ref, dy_ref, *rest):
        sec_refs, rest = rest[:nsec], rest[nsec:]
        part_refs, (gx_ref, dnw_ref, cs_ref) = rest[:npart], rest[npart:npart + 3]
        got_refs, (send, recv, loc) = rest[npart + 3:2 * npart + 3], rest[2 * npart + 3:]
        starts, waits = _chip_swap_copies(part_refs, got_refs, send, recv, loc)

        @pl.when(pl.program_id(0) == 0)
        def _():
            for start in starts:
                start()
            dnw_ref[...] = jnp.zeros_like(dnw_ref)
            cs_ref[...] = jnp.zeros_like(cs_ref)

        @pl.when(pl.program_id(0) == nsteps - 1)
        def _():
            for wait in waits:
                wait()

        dh = jnp.zeros((tm, D_MODEL), F32)
        for ref, (_, lo, width) in zip(sec_refs, SECTIONS):
            dh = dh + jnp.dot(ref[...].astype(MXU), w_ref[lo:lo + width, :], preferred_element_type=F32)
        xv = x_ref[...]
        rstd = lax.rsqrt(jnp.mean(xv * xv, axis=-1, keepdims=True) + EPS)
        xh = xv * rstd
        gg = dh * nw_ref[...]
        gx_ref[...] = rstd * (gg - xh * jnp.mean(gg * xh, axis=-1, keepdims=True)) + dy_ref[...]
        dnw_ref[...] += jnp.broadcast_to(jnp.sum(dh * xh, axis=0, keepdims=True), (8, D_MODEL))
        cs_ref[...] += jnp.broadcast_to(jnp.sum(sec_refs[nsec - 1][...], axis=0, keepdims=True), (8, 128))

    row = lambda n: pl.BlockSpec((tm, n), lambda i: (i, 0))
    full = lambda a: pl.BlockSpec(a.shape, lambda i: (0,) * a.ndim)
    res = pl.pallas_call(
        body, name="inproj_bwd", grid=(nsteps,),
        in_specs=[row(D_MODEL), full(nw), full(wt), row(D_MODEL)] + [row(width) for _, _, width in SECTIONS]
                 + [ANY_SPEC] * npart,
        out_specs=[row(D_MODEL), pl.BlockSpec((8, D_MODEL), lambda i: (0, 0)), pl.BlockSpec((8, 128), lambda i: (0, 0))]
                  + [ANY_SPEC] * npart,
        out_shape=[jax.ShapeDtypeStruct((t, D_MODEL), F32), jax.ShapeDtypeStruct((8, D_MODEL), F32),
                   jax.ShapeDtypeStruct((8, 128), F32)] + [jax.ShapeDtypeStruct(p.shape, p.dtype) for p in partials],
        scratch_shapes=[pltpu.SemaphoreType.DMA((npart, 3)), pltpu.SemaphoreType.DMA((npart, 3)),
                        pltpu.SemaphoreType.DMA((npart,))],
        compiler_params=_params(("arbitrary",)),
    )(x, nw, wt, dy, *dsecs, *partials)
    return res[0], res[1], res[2], res[3:]


def _adamw_sum(w, gs, m, v, name):
    r, c = w.shape
    nsum = gs.shape[0]
    tr = r if r <= 512 else 512
    c1 = 1.0 - ADAM_B1 ** ADAM_STEP
    c2 = 1.0 - ADAM_B2 ** ADAM_STEP

    def body(w_ref, g_ref, m_ref, v_ref, go_ref, d_ref, mo_ref, vo_ref):
        g = g_ref[0].astype(F32)
        for s in range(1, nsum):
            g = g + g_ref[s].astype(F32)
        mn = ADAM_B1 * m_ref[...] + (1.0 - ADAM_B1) * g
        vn = ADAM_B2 * v_ref[...] + (1.0 - ADAM_B2) * (g * g)
        go_ref[...] = g
        mo_ref[...] = mn
        vo_ref[...] = vn
        d_ref[...] = -ADAM_LR * ((mn / c1) / (jnp.sqrt(vn / c2) + ADAM_EPS) + ADAM_WD * w_ref[...])

    blk = pl.BlockSpec((tr, c), lambda i: (i, 0))
    return pl.pallas_call(
        body, name=name, grid=(r // tr,),
        in_specs=[blk, pl.BlockSpec((nsum, tr, c), lambda i: (0, i, 0)), blk, blk],
        out_specs=[blk] * 4, out_shape=[jax.ShapeDtypeStruct((r, c), F32)] * 4,
        compiler_params=_params(("arbitrary",)),
    )(w, gs, m, v)


def _local_step(x, target, h, ht, bias, bk, w_sect, conv_w, a_log, dt_bias, dn_norm_w, q_norm_w, k_norm_w, wout_shard):
    arow = jnp.zeros((1, 128), F32).at[0, DN_HEADS:2 * DN_HEADS].set(-jnp.exp(a_log[0]))
    dtb = jnp.zeros((1, 128), F32).at[0, DN_HEADS:2 * DN_HEADS].set(dt_bias[0])
    g_np, gt_np = _group_mats()
    g, gt = jnp.asarray(g_np), jnp.asarray(gt_np)
    g2 = jnp.asarray(np.kron(np.eye(2, dtype=np.float32), np.ones((HD, HD), np.float32)))
    wq = jnp.tile(q_norm_w, (1, ATT_HEADS)) * (HD ** -0.5)
    wk = jnp.tile(k_norm_w, (1, ATT_HEADS))

    pdn, qkv_dn, z, patt, gate, ba, wout8 = _inproj(h, w_sect, conv_w, wout_shard)
    w_out = wout8.reshape(D_MODEL, D_MODEL)
    oraw, ydn, sh, th = _dn_scan_fwd(qkv_dn, ba, z, arow, dtb, dn_norm_w)
    oatt, yatt, lse = _att_fwd(patt, gate, bias, wq, wk, g2)
    dy, mix_t, loss8 = _outproj_loss(x, ydn, yatt, w_out, target)

    do_dn, dz, do_att, dgate, dd, ddnw = _outproj_bwd(dy, w_out.T, oraw, z, dn_norm_w, oatt, gate, g, gt)
    d_wout = _grad_matmul(mix_t, dy, "dw_out")
    dq, dk, dv, drb, dwq8, dwk8 = _att_bwd(patt, do_att, lse, dd, bias, bk, wq, wk, g2)
    dqkv_dn, dba = _dn_scan_bwd(qkv_dn, ba, do_dn, sh, th, arow, dtb)
    dpdn, d_conv = _dn_prep_bwd(pdn, conv_w, dqkv_dn)
    dsecs = (dpdn, dz, dq, dk, dv, dgate, dba)
    dw_mid = _grad_matmul_many(ht, dsecs[1:6], "dw_in_mid")
    dw_sections = [_grad_matmul(ht, dpdn, "dw_in_dn"), *dw_mid, _grad_matmul(ht, dba, "dw_in_ba")]
    return dict(w_in_sections=dw_sections, conv_w=d_conv, w_out=d_wout, dy=dy, dsecs=dsecs,
                small_parts=(loss8, ddnw, dwq8, dwk8, drb))


def _finish_step(x, norm_w, w_sect_t, gr, partials):
    grad_x, dnw8, cs8, got = _inproj_bwd(x, norm_w, w_sect_t, gr["dy"], gr["dsecs"], partials)
    return grad_x, _pack_small_grads(dnw8, cs8, *gr["small_parts"]), got


SMALL_ROWS = 24
SMALL_AT = dict(a_log=(slice(8, 9), slice(0, 4)), dt_bias=(slice(9, 10), slice(0, 4)),
                dn_norm_w=(slice(10, 11), slice(0, 128)), q_norm_w=(slice(11, 12), slice(0, HD)),
                k_norm_w=(slice(12, 13), slice(0, HD)), rel_bias=(slice(16, 24), slice(0, N_BUCKETS)))
SMALL_NAMES = ("norm_w", "a_log", "dt_bias", "dn_norm_w", "q_norm_w", "k_norm_w", "rel_bias")


LOSS_ROW = 13


def _pack_small_grads(dnw8, cs8, loss8, ddnw8, dwq8, dwk8, drb):
    def body(dnw_ref, cs_ref, loss_ref, ddnw_ref, dwq_ref, dwk_ref, drb_ref, o_ref):
        lane = _iota((8, 128), 1)
        o_ref[...] = jnp.zeros_like(o_ref)
        o_ref[LOSS_ROW:LOSS_ROW + 1, :] = jnp.where(lane == 0, loss_ref[...], 0.0)[0:1, :]
        for k in range(D_MODEL // 128):
            o_ref[k:k + 1, :] = dnw_ref[0:1, k * 128:(k + 1) * 128]
        cs = cs_ref[...]
        o_ref[8:9, :] = jnp.where(lane < DN_HEADS, pltpu.roll(cs, 128 - 2 * DN_HEADS, 1), 0.0)[0:1, :]
        o_ref[9:10, :] = jnp.where(lane < DN_HEADS, pltpu.roll(cs, 128 - DN_HEADS, 1), 0.0)[0:1, :]
        o_ref[10:11, :] = ddnw_ref[0:1, :]
        for row, ref, scale in ((11, dwq_ref, HD ** -0.5), (12, dwk_ref, 1.0)):
            acc = ref[:, 0:128] + ref[:, 128:256] + ref[:, 256:384] + ref[:, 384:512]
            acc = (acc + pltpu.roll(acc, HD, 1)) * scale
            o_ref[row:row + 1, :] = jnp.where(lane < HD, acc, 0.0)[0:1, :]
        o_ref[16:24, :] = drb_ref[...]

    return pl.pallas_call(body, name="pack_small_grads", out_shape=jax.ShapeDtypeStruct((SMALL_ROWS, 128), F32),
                          )(dnw8, cs8, loss8, ddnw8, dwq8, dwk8, drb)


def _adam_math(w, g, m, v):
    c1 = 1.0 - ADAM_B1 ** ADAM_STEP
    c2 = 1.0 - ADAM_B2 ** ADAM_STEP
    mn = ADAM_B1 * m + (1.0 - ADAM_B1) * g
    vn = ADAM_B2 * v + (1.0 - ADAM_B2) * (g * g)
    return -ADAM_LR * ((mn / c1) / (jnp.sqrt(vn / c2) + ADAM_EPS) + ADAM_WD * w), mn, vn


def _adamw_small(gs, ws, ms, vs):
    n = len(SMALL_NAMES)

    def body(g_ref, *refs):
        w_refs, m_refs, v_refs = refs[:n], refs[n:2 * n], refs[2 * n:3 * n]
        outs, loss_ref = refs[3 * n:7 * n], refs[7 * n]
        loss = g_ref[0, LOSS_ROW:LOSS_ROW + 1, :]
        for s in range(1, gs.shape[0]):
            loss = loss + g_ref[s, LOSS_ROW:LOSS_ROW + 1, :]
        loss_ref[...] = loss

        def one(i, rows, lanes, at):
            g = g_ref[0, rows, lanes]
            for s in range(1, gs.shape[0]):
                g = g + g_ref[s, rows, lanes]
            d, mn, vn = _adam_math(w_refs[i][at], g, m_refs[i][at], v_refs[i][at])
            for kind, val in enumerate((g, d, mn, vn)):
                outs[kind * n + i][at] = val

        for k in range(D_MODEL // 128):
            one(0, slice(k, k + 1), slice(0, 128), (slice(0, 1), slice(k * 128, (k + 1) * 128)))
        for i, nm in enumerate(SMALL_NAMES[1:], start=1):
            rows, lanes = SMALL_AT[nm]
            one(i, rows, lanes, (slice(None), slice(None)))

    shapes = [jax.ShapeDtypeStruct(w.shape, F32) for w in ws]
    res = pl.pallas_call(body, name="adamw_small",
                         out_shape=shapes * 4 + [jax.ShapeDtypeStruct((1, 128), F32)])(gs, *ws, *ms, *vs)
    return [res[k * n:(k + 1) * n] for k in range(4)], res[4 * n]


def kernel(x, norm_w, w_in, conv_w, a_log, dt_bias, dn_norm_w, q_norm_w, k_norm_w, rel_bias, w_out, loss_target, m_norm_w, m_w_in, m_conv_w, m_a_log, m_dt_bias, m_dn_norm_w, m_q_norm_w, m_k_norm_w, m_rel_bias, m_w_out, v_norm_w, v_w_in, v_conv_w, v_a_log, v_dt_bias, v_dn_norm_w, v_q_norm_w, v_k_norm_w, v_rel_bias, v_w_out):
    assert w_in.shape[2] == SHARD_COLS
    bk = jnp.asarray(_bucket_tables())
    h, ht, bias, (win8, conv8) = _norm_and_gather(x[0], norm_w, [w_in[0].astype(MXU), conv_w[0]], rel_bias, bk)
    w_sect, w_sect_t = _build_w(win8)
    conv_full = conv8.transpose(1, 0, 2).reshape(4, 3 * D_DN)

    gr = _local_step(x[0], loss_target[0], h, ht, bias, bk, w_sect, conv_full, a_log, dt_bias, dn_norm_w, q_norm_w,
                     k_norm_w, w_out[0].astype(MXU))

    slabs = [_build_slabs(gr["w_in_sections"]),
             gr["w_out"].reshape(4, 2, D_MODEL // N_DEV, D_MODEL).transpose(1, 0, 2, 3),
             gr["conv_w"].reshape(4, 4, 2, 3 * D_DN // N_DEV).transpose(2, 1, 0, 3)]
    core = lax.axis_index("c").astype(jnp.int32).reshape(1)
    from_sibling = _swap_siblings(slabs)
    wires = (GRAD_WIRE, GRAD_WIRE, F32)
    partial = [_chip_sum(slabs[i], from_sibling[i], core, wires[i], "chip_sum_%d" % i) for i in range(3)]
    grad_x, small_pack, (r_win, r_wout, r_conv) = _finish_step(x[0], norm_w, w_sect_t, gr, partial)
    r_small = _share_small(small_pack)

    g_win, d_win, m_win, v_win = _adamw_sum(w_in[0], r_win, m_w_in[0], v_w_in[0], "adamw_w_in")
    g_wout, d_wout, m_wout, v_wout = _adamw_sum(w_out[0], r_wout, m_w_out[0], v_w_out[0], "adamw_w_out")
    g_conv, d_conv, m_conv, v_conv = _adamw_sum(conv_w[0], r_conv, m_conv_w[0], v_conv_w[0], "adamw_conv_w")
    small, loss_row = _adamw_small(r_small,
                                   (norm_w, a_log, dt_bias, dn_norm_w, q_norm_w, k_norm_w, rel_bias),
                                   (m_norm_w, m_a_log, m_dt_bias, m_dn_norm_w, m_q_norm_w, m_k_norm_w, m_rel_bias),
                                   (v_norm_w, v_a_log, v_dt_bias, v_dn_norm_w, v_q_norm_w, v_k_norm_w, v_rel_bias))

    loss = loss_row[0, 0]
    names = ("norm_w", "w_in", "conv_w", "a_log", "dt_bias", "dn_norm_w", "q_norm_w", "k_norm_w", "rel_bias", "w_out")
    big = dict(w_in=(g_win, d_win, m_win, v_win), conv_w=(g_conv, d_conv, m_conv, v_conv),
               w_out=(g_wout, d_wout, m_wout, v_wout))
    outs = [loss, grad_x[None]]
    for kind in range(4):
        for nm in names:
            outs.append(big[nm][kind][None] if nm in big else small[kind][SMALL_NAMES.index(nm)])
    return tuple(outs)
```

```python
import math

import numpy as np
import jax
import jax.numpy as jnp
from jax import lax
from jax.experimental import pallas as pl
from jax.experimental.pallas import tpu as pltpu

F32 = jnp.float32
MXU = jnp.bfloat16
GRAD_WIRE = jnp.bfloat16

D_MODEL = 1024
D_DN = 512
DN_HEADS = 4
DK = 128
CHUNK = 64
D_ATT = 512
ATT_HEADS = 8
HD = 64
PATTERNS = ((128, 1), (512, 4), (2048, 16))
BLK = 128
N_BUCKETS = 32
MAX_DISTANCE = 2048
EPS = 1e-6
W_COLS = 4224
N_DEV = 8
AXES = ("x", "y", "c")

ADAM_LR = 0.001
ADAM_B1 = 0.9
ADAM_B2 = 0.999
ADAM_EPS = 1e-08
ADAM_WD = 0.01
ADAM_STEP = 10

VMEM_LIMIT = 56 * 1024 * 1024
NEG = -1e30


def _dot(a, b):
    return jnp.dot(a.astype(MXU), b.astype(MXU), preferred_element_type=F32)


def _dot_nt(a, b):
    return lax.dot_general(a.astype(MXU), b.astype(MXU), (((1,), (1,)), ((), ())), preferred_element_type=F32)


def _dot_tn(a, b):
    return lax.dot_general(a.astype(MXU), b.astype(MXU), (((0,), (0,)), ((), ())), preferred_element_type=F32)


def _split(a):
    hi = a.astype(jnp.bfloat16)
    return hi, (a - hi.astype(F32)).astype(jnp.bfloat16)


def _dot_split(a, b, dims, exact):
    dg = lambda u, v: lax.dot_general(u, v, (dims, ((), ())), preferred_element_type=F32)
    if exact == "b":
        ah, al = _split(a)
        bh = b.astype(jnp.bfloat16)
        return dg(ah, bh) + dg(al, bh)
    if exact == "a":
        bh, bm = _split(b)
        bl = (b - bh.astype(F32) - bm.astype(F32)).astype(jnp.bfloat16)
        ah = a.astype(jnp.bfloat16)
        return dg(ah, bh) + (dg(ah, bm) + dg(ah, bl))
    ah, al = _split(a)
    bh, bl = _split(b)
    return dg(ah, bh) + (dg(ah, bl) + dg(al, bh))


def _wy_inverses(amat, eye):
    tinv = {x: eye - amat[x] for x in amat}
    pw = amat
    for _ in range(5):
        pw = {x: _hdot(pw[x], pw[x]) for x in amat}
        tinv = {x: tinv[x] + _hdot(tinv[x], pw[x]) for x in amat}
    return tinv


def _hdot(a, b, exact=None):
    return _dot_split(a, b, ((1,), (0,)), exact)


def _hdot_nt(a, b, exact=None):
    return _dot_split(a, b, ((1,), (1,)), exact)


def _hdot_tn(a, b, exact=None):
    return _dot_split(a, b, ((0,), (0,)), exact)


def _sigmoid(x):
    return 1.0 / (1.0 + jnp.exp(-x))


def _silu(x):
    return x * _sigmoid(x)


def _silu_and_grad(x):
    s = _sigmoid(x)
    return x * s, s * (1.0 + x * (1.0 - s))


def _softplus(x):
    return jnp.maximum(x, 0.0) + jnp.log(1.0 + jnp.exp(-jnp.abs(x)))


def _iota(shape, dim):
    return lax.broadcasted_iota(jnp.int32, shape, dim)


def _lane_col(x, k):
    return jnp.sum(jnp.where(_iota(x.shape, 1) == k, x, 0.0), axis=1, keepdims=True)


def _params(sem=None):
    return pltpu.CompilerParams(dimension_semantics=sem, vmem_limit_bytes=VMEM_LIMIT)


def _t5_bucket(dist):
    max_exact = N_BUCKETS // 2
    d = np.maximum(dist, 1).astype(np.float64)
    large = max_exact + (np.log(d / max_exact) / math.log(MAX_DISTANCE / max_exact)
                         * (N_BUCKETS - max_exact)).astype(np.int32)
    large = np.minimum(large, N_BUCKETS - 1)
    return np.where(dist < max_exact, dist, large).astype(np.int32)


def _bucket_tables():
    qi = np.arange(BLK)[:, None]
    kj = np.arange(2 * BLK)[None, :]
    step = qi - kj + BLK
    band = (step >= 0) & (step <= BLK)
    out = []
    for _, r in PATTERNS:
        b = _t5_bucket(np.clip(step, 0, None) * r)
        out.append(np.where(band, b, -1))
    return np.stack(out).astype(np.int32)


def _group_mats():
    g = np.zeros((D_ATT, 128), np.float32)
    for h in range(ATT_HEADS):
        g[h * HD:(h + 1) * HD, h] = 1.0
    return g, np.ascontiguousarray(g.T)


CHIP_FLIPS = ((1, 0), (0, 1), (1, 1))
ANY_SPEC = pl.BlockSpec(memory_space=pl.ANY)
MESH_ID = pl.DeviceIdType.MESH


def _other_chips():
    x, y = lax.axis_index("x"), lax.axis_index("y")
    return [((1 - x if fx else x), (1 - y if fy else y)) for fx, fy in CHIP_FLIPS]


def _gather_plan(ins, outs, send, recv, loc):
    n = len(ins)
    x, y, c = (lax.axis_index(a) for a in AXES)
    sib = (x, y, 1 - c)
    chips = _other_chips()
    lin = lambda px, py, pc: 4 * px + 2 * py + pc

    def copy(a, k, block, to, src=None):
        slot = outs[a].at[lin(*block)]
        return pltpu.make_async_remote_copy(src_ref=slot if src is None else src, dst_ref=slot,
                                            send_sem=send.at[a, k], recv_sem=recv.at[a, k],
                                            device_id=to, device_id_type=MESH_ID)

    mine = [pltpu.make_async_copy(ins[a], outs[a].at[lin(x, y, c)], loc.at[a]) for a in range(n)]
    firsts = []
    for a in range(n):
        firsts.append(copy(a, 0, (x, y, c), sib, src=ins[a]))
        firsts += [copy(a, 1 + j, (x, y, c), (*chip, c), src=ins[a]) for j, chip in enumerate(chips)]

    def begin():
        for cp in mine + firsts:
            cp.start()

    def finish():
        passed = []
        for j, chip in enumerate(chips):
            for a in range(n):
                copy(a, 1 + j, (*chip, c), (x, y, c)).wait_recv()
                fw = copy(a, 4 + j, (*chip, c), sib)
                fw.start()
                passed.append(fw)
        for a in range(n):
            copy(a, 0, sib, (x, y, c)).wait_recv()
            for j, chip in enumerate(chips):
                copy(a, 4 + j, (*chip, 1 - c), (x, y, c)).wait_recv()
        for cp in firsts + passed:
            cp.wait_send()
        for cp in mine:
            cp.wait()

    return begin, finish


GATHER_SEMS = lambda n: [pltpu.SemaphoreType.DMA((n, 7)), pltpu.SemaphoreType.DMA((n, 7)), pltpu.SemaphoreType.DMA((n,))]


def _swap_siblings(arrs):
    n = len(arrs)

    def body(*refs):
        ins, outs = refs[:n], refs[n:2 * n]
        send, recv = refs[2 * n:]
        x, y, c = (lax.axis_index(a) for a in AXES)
        cps = [pltpu.make_async_remote_copy(src_ref=ins[a].at[1 - c], dst_ref=outs[a], send_sem=send.at[a],
                                            recv_sem=recv.at[a], device_id=(x, y, 1 - c), device_id_type=MESH_ID)
               for a in range(n)]
        for cp in cps:
            cp.start()
        for cp in cps:
            cp.wait()

    return pl.pallas_call(
        body, name="swap_siblings", out_shape=[jax.ShapeDtypeStruct(a.shape[1:], a.dtype) for a in arrs],
        in_specs=[ANY_SPEC] * n, out_specs=[ANY_SPEC] * n,
        scratch_shapes=[pltpu.SemaphoreType.DMA((n,)), pltpu.SemaphoreType.DMA((n,))],
    )(*arrs)


def _chip_sum(mine2, theirs, core, wire, name):
    _, nchip, r, cdim = mine2.shape
    tr = r if r <= 1024 else 1024

    def body(core_ref, a_ref, b_ref, o_ref):
        del core_ref
        o_ref[...] = (a_ref[...].astype(F32) + b_ref[...].astype(F32)).astype(wire)

    grid_spec = pltpu.PrefetchScalarGridSpec(
        num_scalar_prefetch=1, grid=(nchip, r // tr),
        in_specs=[pl.BlockSpec((None, None, tr, cdim), lambda j, i, cr: (cr[0], j, i, 0)),
                  pl.BlockSpec((None, tr, cdim), lambda j, i, cr: (j, i, 0))],
        out_specs=pl.BlockSpec((None, tr, cdim), lambda j, i, cr: (j, i, 0)))
    return pl.pallas_call(
        body, name=name, grid_spec=grid_spec, out_shape=jax.ShapeDtypeStruct((nchip, r, cdim), wire),
        compiler_params=_params(("arbitrary", "arbitrary")),
    )(core, mine2, theirs)


def _chip_swap_copies(ins, outs, send, recv, loc):
    x, y, c = (lax.axis_index(a) for a in AXES)
    me = 2 * x + y
    starts, arrivals, drains = [], [], []
    for a in range(len(ins)):
        lc = pltpu.make_async_copy(ins[a].at[me], outs[a].at[me], loc.at[a])
        starts.append(lc.start)
        drains.append(lc.wait)
        for j, (px, py) in enumerate(_other_chips()):
            them = 2 * px + py
            cp = pltpu.make_async_remote_copy(src_ref=ins[a].at[them], dst_ref=outs[a].at[me], send_sem=send.at[a, j],
                                              recv_sem=recv.at[a, j], device_id=(px, py, c), device_id_type=MESH_ID)
            landing = pltpu.make_async_remote_copy(src_ref=ins[a].at[them], dst_ref=outs[a].at[them],
                                                   send_sem=send.at[a, j], recv_sem=recv.at[a, j],
                                                   device_id=(px, py, c), device_id_type=MESH_ID)
            starts.append(cp.start)
            arrivals.append(landing.wait_recv)
            drains.append(cp.wait_send)
    return starts, arrivals + drains


def _share_small(pack):
    def body(in_ref, out_ref, send, recv, loc):
        x, y, c = (lax.axis_index(a) for a in AXES)
        me = 4 * x + 2 * y + c
        lc = pltpu.make_async_copy(in_ref, out_ref.at[me], loc.at[0])
        lc.start()
        sends, arrivals = [], []
        for k in range(1, N_DEV):
            px = 1 - x if k & 4 else x
            py = 1 - y if k & 2 else y
            pc = 1 - c if k & 1 else c
            cp = pltpu.make_async_remote_copy(src_ref=in_ref, dst_ref=out_ref.at[me], send_sem=send.at[k - 1],
                                              recv_sem=recv.at[k - 1], device_id=(px, py, pc), device_id_type=MESH_ID)
            cp.start()
            sends.append(cp)
            arrivals.append(pltpu.make_async_remote_copy(src_ref=in_ref, dst_ref=out_ref.at[4 * px + 2 * py + pc],
                                                         send_sem=send.at[k - 1], recv_sem=recv.at[k - 1],
                                                         device_id=(px, py, pc), device_id_type=MESH_ID))
        for cp in arrivals:
            cp.wait_recv()
        for cp in sends:
            cp.wait_send()
        lc.wait()

    return pl.pallas_call(
        body, name="share_small", out_shape=jax.ShapeDtypeStruct((N_DEV,) + pack.shape, pack.dtype),
        in_specs=[ANY_SPEC], out_specs=ANY_SPEC,
        scratch_shapes=[pltpu.SemaphoreType.DMA((N_DEV - 1,)), pltpu.SemaphoreType.DMA((N_DEV - 1,)),
                        pltpu.SemaphoreType.DMA((1,))],
    )(pack)


W_PARTS = ((0, 0, 2048), (2048, 4096, 8), (2056, 2048, 2048))
SHARD_COLS = 513


def _pieces(lo, hi, parts):
    out = []
    for ref_start, tgt_start, width in parts:
        a, b = max(lo, ref_start), min(hi, ref_start + width)
        if a < b:
            out.append((a - lo, tgt_start + a - ref_start, b - a))
    return out


def _build_w(win8):
    tr = 512

    def body(in_ref, w_ref, wt_ref):
        w_ref[:, 4096:W_COLS] = jnp.zeros((tr, W_COLS - 4096), MXU)
        for p in range(N_DEV):
            for src, dst, width in _pieces(p * SHARD_COLS, (p + 1) * SHARD_COLS, W_PARTS):
                w_ref[:, dst:dst + width] = in_ref[p, :, src:src + width]
        for k in range(W_COLS // 128):
            wt_ref[k * 128:(k + 1) * 128, :] = w_ref[:, k * 128:(k + 1) * 128].astype(F32).T.astype(MXU)

    return pl.pallas_call(
        body, name="build_w", grid=(D_MODEL // tr,),
        in_specs=[pl.BlockSpec((N_DEV, tr, SHARD_COLS), lambda i: (0, i, 0))],
        out_specs=[pl.BlockSpec((tr, W_COLS), lambda i: (i, 0)), pl.BlockSpec((W_COLS, tr), lambda i: (0, i))],
        out_shape=[jax.ShapeDtypeStruct((D_MODEL, W_COLS), MXU), jax.ShapeDtypeStruct((W_COLS, D_MODEL), MXU)],
        compiler_params=_params(("arbitrary",)),
    )(win8)


def _build_slabs(secs):
    tr = 512
    parts = ((0, 0, 1536), (1536, 1, 512), (2048, 6, 8), (2056, 2, 512), (2568, 3, 512), (3080, 4, 512),
             (3592, 5, 512))

    def body(*refs):
        o_ref = refs[len(secs)]
        for p in range(N_DEV):
            lo, hi = p * SHARD_COLS, (p + 1) * SHARD_COLS
            for ref_start, idx, width in parts:
                a, b = max(lo, ref_start), min(hi, ref_start + width)
                if a < b:
                    o_ref[p % 2, p // 2, :, a - lo:b - lo] = refs[idx][:, a - ref_start:b - ref_start]

    return pl.pallas_call(
        body, name="build_slabs", grid=(D_MODEL // tr,),
        in_specs=[pl.BlockSpec((tr, s.shape[1]), lambda i: (i, 0)) for s in secs],
        out_specs=pl.BlockSpec((2, 4, tr, SHARD_COLS), lambda i: (0, 0, i, 0)),
        out_shape=jax.ShapeDtypeStruct((2, 4, D_MODEL, SHARD_COLS), secs[0].dtype),
        compiler_params=_params(("arbitrary",)),
    )(*secs)


def _fill_bias(rb_ref, bk_ref, bias_ref, pair):
    for p in range(len(PATTERNS)):
        bk_p = bk_ref[p]
        for hh in range(2):
            head = 2 * pair + hh
            bm = jnp.full((BLK, 2 * BLK), NEG, F32)
            for b in range(N_BUCKETS):
                bm = jnp.where(bk_p == b, rb_ref[head, b], bm)
            bias_ref[p, hh * BLK:(hh + 1) * BLK, :] = bm


def _norm_and_gather(x, nw, shards, rb, bk):
    t = x.shape[0]
    tm = 512
    nsteps = t // tm
    n = len(shards)
    npair = ATT_HEADS // 2
    assert nsteps >= npair

    def body(x_ref, nw_ref, *rest):
        ins, (rb_ref, bk_ref, h_ref, ht_ref, bias_ref) = rest[:n], rest[n:n + 5]
        outs, sems = rest[n + 5:2 * n + 5], rest[2 * n + 5:]
        begin, finish = _gather_plan(ins, outs, *sems)
        step = pl.program_id(0)

        @pl.when(step == 0)
        def _():
            begin()

        @pl.when(step == nsteps - 1)
        def _():
            finish()

        @pl.when(step < npair)
        def _():
            _fill_bias(rb_ref, bk_ref, bias_ref, step)

        xv = x_ref[...]
        rstd = lax.rsqrt(jnp.mean(xv * xv, axis=-1, keepdims=True) + EPS)
        hf = xv * rstd * nw_ref[...]
        h_ref[...] = hf.astype(MXU)
        ht_ref[...] = hf.T.astype(MXU)

    res = pl.pallas_call(
        body, name="norm_and_gather", grid=(nsteps,),
        in_specs=[pl.BlockSpec((tm, D_MODEL), lambda i: (i, 0)), pl.BlockSpec(nw.shape, lambda i: (0, 0))]
                 + [ANY_SPEC] * n
                 + [pl.BlockSpec(memory_space=pltpu.SMEM), pl.BlockSpec(bk.shape, lambda i: (0, 0, 0))],
        out_specs=[pl.BlockSpec((tm, D_MODEL), lambda i: (i, 0)), pl.BlockSpec((D_MODEL, tm), lambda i: (0, i)),
                   pl.BlockSpec((len(PATTERNS), None, 2 * BLK, 2 * BLK), lambda i: (0, jnp.minimum(i, npair - 1), 0, 0))]
                  + [ANY_SPEC] * n,
        out_shape=[jax.ShapeDtypeStruct((t, D_MODEL), MXU), jax.ShapeDtypeStruct((D_MODEL, t), MXU),
                   jax.ShapeDtypeStruct((len(PATTERNS), npair, 2 * BLK, 2 * BLK), F32)]
                  + [jax.ShapeDtypeStruct((N_DEV,) + a.shape, a.dtype) for a in shards],
        scratch_shapes=GATHER_SEMS(n),
        compiler_params=_params(("arbitrary",)),
    )(x, nw, *shards, rb, bk)
    return res[0], res[1], res[2], res[3:]


def _inproj(h_all, w, cw, wout_shard):
    t = h_all.shape[0]
    tm = 512
    nsteps = t // tm

    def body(h_ref, w_ref, cw_ref, wo_ref, pdn_ref, qkv_ref, z_ref, patt_ref, gate_ref, ba_ref,
             wo8_ref, halo_ref, send, recv, loc):
        begin, finish = _gather_plan([wo_ref], [wo8_ref], send, recv, loc)

        @pl.when(pl.program_id(0) == 0)
        def _():
            begin()
            halo_ref[...] = jnp.zeros_like(halo_ref)

        @pl.when(pl.program_id(0) == nsteps - 1)
        def _():
            finish()

        h = h_ref[...]
        for ref, lo, hi in ((z_ref, 1536, 2048), (patt_ref, 2048, 3584), (gate_ref, 3584, 4096), (ba_ref, 4096, 4224)):
            ref[...] = jnp.dot(h, w_ref[:, lo:hi], preferred_element_type=F32)
        pdn = jnp.dot(h, w_ref[:, 0:3 * D_DN], preferred_element_type=F32)
        pdn_ref[...] = pdn
        _dn_prep_tile(pdn, halo_ref, cw_ref, qkv_ref)

    row = lambda n: pl.BlockSpec((tm, n), lambda i: (i, 0))
    full = lambda a: pl.BlockSpec(a.shape, lambda i: (0,) * a.ndim)
    return pl.pallas_call(
        body, name="inproj", grid=(nsteps,),
        in_specs=[row(D_MODEL), full(w), full(cw), ANY_SPEC],
        out_specs=[row(1536), row(1536), row(512), row(1536), row(512), row(128), ANY_SPEC],
        out_shape=[jax.ShapeDtypeStruct((t, n), F32) for n in (1536, 1536, 512, 1536, 512, 128)] +
                  [jax.ShapeDtypeStruct((N_DEV,) + wout_shard.shape, wout_shard.dtype)],
        scratch_shapes=[pltpu.VMEM((8, 3 * D_DN), F32)] + GATHER_SEMS(1),
        compiler_params=_params(("arbitrary",)),
    )(h_all, w, cw, wout_shard)


CONV_ROWS = 512


def _conv_taps(u_ref, c, w_ref):
    r0 = c * CONV_ROWS
    if c == 0:
        ext = jnp.concatenate([jnp.zeros((8, 128), F32), u_ref[0:CONV_ROWS, :]], axis=0)
    else:
        ext = u_ref[r0 - 8:r0 + CONV_ROWS, :]
    taps = [ext[8:, :]] + [pltpu.roll(ext, k, 0)[8:, :] for k in (1, 2, 3)]
    y = taps[0] * w_ref[3:4, :]
    for k in (1, 2, 3):
        y = y + taps[k] * w_ref[3 - k:4 - k, :]
    return taps, y


def _dn_prep_tile(pdn, halo_ref, cw_ref, out_ref):
    rows = pdn.shape[0]
    ext = jnp.concatenate([halo_ref[...], pdn], axis=0)
    halo_ref[...] = pdn[rows - 8:, :]
    for j in range(3 * D_DN // 128):
        cols = slice(j * 128, (j + 1) * 128)
        e = ext[:, cols]
        y = e[8:, :] * cw_ref[3:4, cols]
        for k in (1, 2, 3):
            y = y + pltpu.roll(e, k, 0)[8:, :] * cw_ref[3 - k:4 - k, cols]
        a = _silu(y)
        if j < 2 * DN_HEADS:
            a = a * lax.rsqrt(jnp.sum(a * a, axis=1, keepdims=True) + EPS)
        if j < DN_HEADS:
            a = a * DK ** -0.5
        out_ref[:, cols] = a


def _chunk_common(qkv, ba, arow, dtb):
    c = CHUNK
    ri, ci = _iota((c, c), 0), _iota((c, c), 1)
    lane = _iota((c, 128), 1)
    g_all = jnp.where((lane >= DN_HEADS) & (lane < 2 * DN_HEADS), arow * _softplus(ba + dtb), 0.0)
    gc_all = _hdot((ri >= ci).astype(F32), g_all, "a")
    gc_t = gc_all.T
    beta_all = _sigmoid(ba)
    out = []
    for h in range(DN_HEADS):
        gc = _lane_col(gc_all, DN_HEADS + h)
        gcr = gc_t[DN_HEADS + h:DN_HEADS + h + 1, :]
        gl = gc[c - 1:c, :]
        out.append(dict(
            q=qkv[:, h * DK:(h + 1) * DK], k=qkv[:, D_DN + h * DK:D_DN + (h + 1) * DK],
            v=qkv[:, 2 * D_DN + h * DK:2 * D_DN + (h + 1) * DK],
            beta=_lane_col(beta_all, h), g=_lane_col(g_all, DN_HEADS + h),
            a_raw=_lane_col(ba, DN_HEADS + h), a_h=_lane_col(arow, DN_HEADS + h), dt_h=_lane_col(dtb, DN_HEADS + h),
            decay=jnp.exp(jnp.where(ri >= ci, gc - gcr, NEG)), eg=jnp.exp(gc), egl=jnp.exp(gl), etail=jnp.exp(gl - gc)))
    return out, ri, ci


SCAN_CHUNKS = 8


def _dn_scan_fwd(qkv, ba, z, arow, dtb, dnw):
    t = qkv.shape[0]
    n = t // CHUNK
    c = CHUNK
    cps = SCAN_CHUNKS
    hs = range(DN_HEADS)
    chains = [(j, h) for j in range(cps) for h in hs]

    def body(qkv_ref, ba_ref, z_ref, arow_ref, dtb_ref, dnw_ref, o_ref, y_ref, sh_ref, th_ref, s_ref):
        @pl.when(pl.program_id(0) == 0)
        def _():
            s_ref[...] = jnp.zeros_like(s_ref)

        ms = {}
        for j in range(cps):
            rows = slice(j * c, (j + 1) * c)
            mj, ri, ci = _chunk_common(qkv_ref[rows, :], ba_ref[rows, :], arow_ref[...], dtb_ref[...])
            for h in hs:
                ms[j, h] = mj[h]
        kb = {x: ms[x]["k"] * ms[x]["beta"] for x in chains}
        amat = {x: jnp.where(ri > ci, _dot_nt(kb[x], ms[x]["k"]) * ms[x]["decay"], 0.0) for x in chains}
        attn = {x: jnp.where(ri >= ci, _dot_nt(ms[x]["q"], ms[x]["k"]) * ms[x]["decay"], 0.0) for x in chains}
        tinv = _wy_inverses(amat, (ri == ci).astype(F32))
        uw = {x: _hdot(tinv[x], jnp.concatenate([ms[x]["v"] * ms[x]["beta"], kb[x] * ms[x]["eg"]], axis=1))
              for x in chains}
        u = {x: uw[x][:, :DK] for x in chains}
        w = {x: uw[x][:, DK:] for x in chains}
        q_dec = {x: ms[x]["q"] * ms[x]["eg"] for x in chains}
        k_tail = {x: ms[x]["k"] * ms[x]["etail"] for x in chains}
        s = [s_ref[h] for h in hs]
        for j in range(cps):
            rows = slice(j * c, (j + 1) * c)
            v_new = [u[j, h] - _dot(w[j, h], s[h]) for h in hs]
            o = [_dot(q_dec[j, h], s[h]) + _dot(attn[j, h], v_new[h]) for h in hs]
            for h in hs:
                sh_ref[j, h] = s[h]
                th_ref[j, h] = tinv[j, h]
            s = [s[h] * ms[j, h]["egl"] + _dot_tn(k_tail[j, h], v_new[h]) for h in hs]
            for h in hs:
                cols = slice(h * DK, (h + 1) * DK)
                o_ref[rows, cols] = o[h]
                rs = lax.rsqrt(jnp.mean(o[h] * o[h], axis=1, keepdims=True) + EPS)
                y_ref[rows, cols] = (o[h] * rs * dnw_ref[...] * _silu(z_ref[rows, cols])).astype(MXU)
        for h in hs:
            s_ref[h] = s[h]

    row = lambda w_: pl.BlockSpec((cps * c, w_), lambda i: (i, 0))
    one = pl.BlockSpec((1, 128), lambda i: (0, 0))
    return pl.pallas_call(
        body, name="dn_scan_fwd", grid=(n // cps,),
        in_specs=[row(1536), row(128), row(512), one, one, one],
        out_specs=[row(512), row(512), pl.BlockSpec((cps, DN_HEADS, DK, DK), lambda i: (i, 0, 0, 0)),
                   pl.BlockSpec((cps, DN_HEADS, c, c), lambda i: (i, 0, 0, 0))],
        out_shape=[jax.ShapeDtypeStruct((t, 512), F32), jax.ShapeDtypeStruct((t, 512), MXU),
                   jax.ShapeDtypeStruct((n, DN_HEADS, DK, DK), F32), jax.ShapeDtypeStruct((n, DN_HEADS, c, c), F32)],
        scratch_shapes=[pltpu.VMEM((DN_HEADS, DK, DK), F32)],
        compiler_params=_params(("arbitrary",)),
    )(qkv, ba, z, arow, dtb, dnw)


ATT_ROWS = 512


def _pair_rstd(xv, g2_ref):
    return lax.rsqrt(_hdot(xv * xv, g2_ref[...], "b") * (1.0 / HD) + EPS)


def _pair_norm(t, raw_refs, w_refs, out_refs, g2_ref):
    for c in range(t // ATT_ROWS):
        sl = slice(c * ATT_ROWS, (c + 1) * ATT_ROWS)
        for raw, w_ref, out in zip(raw_refs, w_refs, out_refs):
            xv = raw[sl, :]
            out[sl, :] = xv * _pair_rstd(xv, g2_ref) * w_ref[...]


BIAS_SPEC = pl.BlockSpec((len(PATTERNS), None, 2 * BLK, 2 * BLK), lambda i: (0, i, 0, 0))
PAIR_ROW_SPEC = pl.BlockSpec((1, 128), lambda i: (0, i))


def _stack_heads(xb, h0):
    return jnp.concatenate([jnp.where(h0, xb, 0.0), jnp.where(h0, 0.0, xb)], axis=0).astype(MXU)


def _block_rows(t, r, n):
    per_class = (t // r) // BLK
    res = n // per_class
    j = n % per_class
    start = res + BLK * r * j
    pstart = res + BLK * r * jnp.maximum(j - 1, 0)
    if r == 1:
        return pl.ds(pl.multiple_of(start, BLK), BLK), pl.ds(pl.multiple_of(pstart, BLK), BLK), j
    return pl.ds(start, BLK, stride=r), pl.ds(pstart, BLK, stride=r), j


def _att_fwd(qkv, gate, bias, wq, wk, g2):
    t = qkv.shape[0]
    rows = ATT_ROWS

    def body(bias_ref, qraw_ref, kraw_ref, v_ref, g_ref, wq_ref, wk_ref, g2_ref, o_ref, y_ref, lse_ref,
             o0_ref, o1_ref, o2_ref, l0_ref, l1_ref, l2_ref, q_ref, k_ref):
        h0 = _iota((BLK, 128), 1) < HD
        prev_cols = _iota((2 * BLK, 2 * BLK), 1) < BLK
        op_refs, lp_refs = (o0_ref, o1_ref, o2_ref), (l0_ref, l1_ref, l2_ref)
        _pair_norm(t, (qraw_ref, kraw_ref), (wq_ref, wk_ref), (q_ref, k_ref), g2_ref)

        for p, (_, r) in enumerate(PATTERNS):
            def blk(n, carry, p=p, r=r):
                cur, prev, j = _block_rows(t, r, n)
                q2 = _stack_heads(q_ref[cur, :], h0)
                k2 = jnp.concatenate([k_ref[prev, :], k_ref[cur, :]], axis=0).astype(MXU)
                v2 = jnp.concatenate([v_ref[prev, :], v_ref[cur, :]], axis=0).astype(MXU)
                s = _dot_nt(q2, k2) + bias_ref[p] + jnp.where(prev_cols & (j == 0), NEG, 0.0)
                m = jnp.max(s, axis=1, keepdims=True)
                e = jnp.exp(s - m)
                l = jnp.sum(e, axis=1, keepdims=True)
                pv = _dot(e, v2) / l
                lse = m + jnp.log(l)
                op_refs[p][cur, :] = jnp.where(h0, pv[:BLK], pv[BLK:])
                lp_refs[p][cur, :] = jnp.where(h0, lse[:BLK], lse[BLK:])
                return carry

            lax.fori_loop(0, t // BLK, blk, 0, unroll=16)

        for c in range(t // rows):
            sl = slice(c * rows, (c + 1) * rows)
            ls = [ref[sl, :] for ref in lp_refs]
            mx = jnp.maximum(jnp.maximum(ls[0], ls[1]), ls[2])
            ws = [jnp.exp(v_ - mx) for v_ in ls]
            den = ws[0] + ws[1] + ws[2]
            o = (ws[0] * o0_ref[sl, :] + ws[1] * o1_ref[sl, :] + ws[2] * o2_ref[sl, :]) / den
            o_ref[sl, :] = o
            y_ref[sl, :] = (o * _silu(g_ref[sl, :])).astype(MXU)
            lse_ref[sl, :] = mx + jnp.log(den)

    col = lambda off: pl.BlockSpec((t, 128), lambda i, off=off: (0, off + i))
    return pl.pallas_call(
        body, name="att_fwd", grid=(ATT_HEADS // 2,),
        in_specs=[BIAS_SPEC, col(0), col(4), col(8), col(0), PAIR_ROW_SPEC, PAIR_ROW_SPEC,
                  pl.BlockSpec((128, 128), lambda i: (0, 0))],
        out_specs=[col(0), col(0), col(0)],
        out_shape=[jax.ShapeDtypeStruct((t, 512), dt_) for dt_ in (F32, MXU, F32)],
        scratch_shapes=[pltpu.VMEM((t, 128), F32)] * 8,
        compiler_params=_params(("arbitrary",)),
    )(bias, qkv, qkv, qkv, gate, wq, wk, g2)


def _outproj_both(x, ydn, yatt, wout, target, wout_t, oraw, z, dnw, oatt, gate, g, gt):
    t = x.shape[0]
    tm = 512

    def body(x_ref, a_ref, b_ref, wf_ref, t_ref, w_ref, o_ref, z_ref, dnw_ref, oa_ref, g_ref, grp_ref, grpt_ref,
             dy_ref, mix_ref, loss_ref, do_ref, dz_ref, doa_ref, dg_ref, dd_ref, ddnw_ref):
        @pl.when(pl.program_id(0) == 0)
        def _():
            loss_ref[...] = jnp.zeros_like(loss_ref)
            ddnw_ref[...] = jnp.zeros_like(ddnw_ref)

        mix = jnp.concatenate([a_ref[...], b_ref[...]], axis=1)
        mix_ref[...] = mix.astype(F32).T.astype(MXU)
        err = x_ref[...] + jnp.dot(mix, wf_ref[...], preferred_element_type=F32) - t_ref[...]
        dy = err * (1.0 / D_MODEL)
        dy_ref[...] = dy
        loss_ref[...] += jnp.sum(err * err) * (0.5 / D_MODEL)

        dmix = jnp.dot(dy.astype(MXU), w_ref[...], preferred_element_type=F32)
        dnw_v = dnw_ref[...]
        acc = jnp.zeros((1, DK), F32)
        for h in range(DN_HEADS):
            sl = slice(h * DK, (h + 1) * DK)
            o, zz, dm = o_ref[:, sl], z_ref[:, sl], dmix[:, sl]
            rs = lax.rsqrt(jnp.mean(o * o, axis=1, keepdims=True) + EPS)
            oh = o * rs
            silu_z, dsilu_z = _silu_and_grad(zz)
            dz_ref[:, sl] = (dm * oh * dnw_v * dsilu_z).astype(MXU)
            d_on = dm * silu_z
            gg = d_on * dnw_v
            do_ref[:, sl] = rs * (gg - oh * jnp.mean(gg * oh, axis=1, keepdims=True))
            acc = acc + jnp.sum(d_on * oh, axis=0, keepdims=True)
        ddnw_ref[...] += jnp.broadcast_to(acc, (8, DK))
        da, gate_v, oa = dmix[:, 512:], g_ref[...], oa_ref[...]
        silu_g, dsilu_g = _silu_and_grad(gate_v)
        doa = da * silu_g
        doa_ref[...] = doa
        dg_ref[...] = (da * oa * dsilu_g).astype(MXU)
        dd_ref[...] = _hdot(_hdot(doa * oa, grp_ref[...], "b"), grpt_ref[...], "b")

    row = lambda n: pl.BlockSpec((tm, n), lambda i: (i, 0))
    full = lambda a: pl.BlockSpec(a.shape, lambda i: (0,) * a.ndim)
    return pl.pallas_call(
        body, name="outproj_both", grid=(t // tm,),
        in_specs=[row(D_MODEL), row(512), row(512), full(wout), row(D_MODEL), full(wout_t), row(512), row(512),
                  full(dnw), row(512), row(512), full(g), full(gt)],
        out_specs=[row(D_MODEL), pl.BlockSpec((D_MODEL, tm), lambda i: (0, i)), pl.BlockSpec((8, 128), lambda i: (0, 0))]
                  + [row(512)] * 5 + [pl.BlockSpec((8, DK), lambda i: (0, 0))],
        out_shape=[jax.ShapeDtypeStruct((t, D_MODEL), F32), jax.ShapeDtypeStruct((D_MODEL, t), MXU),
                   jax.ShapeDtypeStruct((8, 128), F32)]
                  + [jax.ShapeDtypeStruct((t, 512), dt_) for dt_ in (F32, MXU, F32, MXU, F32)]
                  + [jax.ShapeDtypeStruct((8, DK), F32)],
        compiler_params=_params(("arbitrary",)),
    )(x, ydn, yatt, wout, target, wout_t, oraw, z, dnw, oatt, gate, g, gt)


def _grad_matmul(at, b, name):
    m, t = at.shape
    n = b.shape[1]
    tk = 1024
    tn = n if n <= 1536 else 512
    nk = t // tk

    def body(a_ref, b_ref, o_ref, acc_ref):
        k = pl.program_id(1)

        @pl.when(k == 0)
        def _():
            acc_ref[...] = jnp.zeros_like(acc_ref)

        acc_ref[...] += jnp.dot(a_ref[...], b_ref[...].astype(MXU), preferred_element_type=F32)

        @pl.when(k == nk - 1)
        def _():
            o_ref[...] = acc_ref[...].astype(GRAD_WIRE)

    return pl.pallas_call(
        body, name=name, grid=(n // tn, nk),
        in_specs=[pl.BlockSpec((m, tk), lambda j, k: (0, k)), pl.BlockSpec((tk, tn), lambda j, k: (k, j))],
        out_specs=pl.BlockSpec((m, tn), lambda j, k: (0, j)),
        out_shape=jax.ShapeDtypeStruct((m, n), GRAD_WIRE),
        scratch_shapes=[pltpu.VMEM((m, tn), F32)],
        compiler_params=_params(("arbitrary", "arbitrary")),
    )(at, b)


def _grad_matmul_many(at, bs, name):
    m, t = at.shape
    n = bs[0].shape[1]
    nb = len(bs)
    tk = 1024
    nk = t // tk

    def body(a_ref, *refs):
        b_refs, o_refs, acc_ref = refs[:nb], refs[nb:2 * nb], refs[2 * nb]
        s, k = pl.program_id(0), pl.program_id(1)

        @pl.when(k == 0)
        def _():
            acc_ref[...] = jnp.zeros_like(acc_ref)

        for i in range(nb):
            @pl.when(s == i)
            def _(i=i):
                acc_ref[...] += jnp.dot(a_ref[...], b_refs[i][...].astype(MXU), preferred_element_type=F32)

                @pl.when(k == nk - 1)
                def _():
                    o_refs[i][...] = acc_ref[...].astype(GRAD_WIRE)

    def b_spec(i):
        return pl.BlockSpec((tk, n), lambda s, k: (jnp.where(s == i, k, jnp.where(s < i, 0, nk - 1)), 0))

    return pl.pallas_call(
        body, name=name, grid=(nb, nk),
        in_specs=[pl.BlockSpec((m, tk), lambda s, k: (0, k))] + [b_spec(i) for i in range(nb)],
        out_specs=[pl.BlockSpec((m, n), lambda s, k: (0, 0))] * nb,
        out_shape=[jax.ShapeDtypeStruct((m, n), GRAD_WIRE)] * nb,
        scratch_shapes=[pltpu.VMEM((m, n), F32)],
        compiler_params=_params(("arbitrary", "arbitrary")),
    )(at, *bs)


def _att_bwd(qkv, do, lse, dd, bias, bk, wq, wk, g2):
    t = qkv.shape[0]
    rows = ATT_ROWS

    def body(bias_ref, bk_ref, qraw_ref, kraw_ref, v_ref, do_ref, lse_ref, dd_ref, wq_ref, wk_ref, g2_ref,
             dq_out, dk_out, dv_out, db_ref, dwq_ref, dwk_ref, ds_ref, q_ref, k_ref, dq_ref, dk_ref, dv_ref):
        pair = pl.program_id(0)

        @pl.when(pair == 0)
        def _():
            db_ref[...] = jnp.zeros_like(db_ref)

        _pair_norm(t, (qraw_ref, kraw_ref), (wq_ref, wk_ref), (q_ref, k_ref), g2_ref)
        ds_ref[...] = jnp.zeros_like(ds_ref)
        for c in range(t // rows):
            sl = slice(c * rows, (c + 1) * rows)
            for ref in (dq_ref, dk_ref, dv_ref):
                ref[sl, :] = jnp.zeros((rows, 128), F32)
        h0 = _iota((BLK, 128), 1) < HD
        prev_cols = _iota((2 * BLK, 2 * BLK), 1) < BLK

        def rows_of(xb):
            return jnp.concatenate([xb[:, 0:1], xb[:, HD:HD + 1]], axis=0)

        for p, (_, r) in enumerate(PATTERNS):
            def blk(n, carry, p=p, r=r):
                cur, prev, j = _block_rows(t, r, n)
                q2, do2 = _stack_heads(q_ref[cur, :], h0), _stack_heads(do_ref[cur, :], h0)
                k2 = jnp.concatenate([k_ref[prev, :], k_ref[cur, :]], axis=0).astype(MXU)
                v2 = jnp.concatenate([v_ref[prev, :], v_ref[cur, :]], axis=0).astype(MXU)
                s = _dot_nt(q2, k2) + bias_ref[p] + jnp.where(prev_cols & (j == 0), NEG, 0.0)
                prob = jnp.exp(s - rows_of(lse_ref[cur, :]))
                ds = prob * (_dot_nt(do2, v2) - rows_of(dd_ref[cur, :]))
                ds_ref[p] += ds
                dq2 = _dot(ds, k2)
                dk2 = _dot_tn(ds, q2)
                dv2 = _dot_tn(prob, do2)
                dq_ref[cur, :] += jnp.where(h0, dq2[:BLK], dq2[BLK:])
                dk_ref[prev, :] += dk2[:BLK]
                dv_ref[prev, :] += dv2[:BLK]
                dk_ref[cur, :] += dk2[BLK:]
                dv_ref[cur, :] += dv2[BLK:]
                return carry

            lax.fori_loop(0, t // BLK, blk, 0, unroll=8)

        ri, ci = _iota((8, 128), 0), _iota((8, 128), 1)
        upd = jnp.zeros((8, 128), F32)
        for p in range(len(PATTERNS)):
            bk = bk_ref[p]
            for hh in range(2):
                dsum = ds_ref[p, hh * BLK:(hh + 1) * BLK, :]
                for b in range(N_BUCKETS):
                    val = jnp.sum(jnp.where(bk == b, dsum, 0.0))
                    upd = upd + jnp.where((ri == 2 * pair + hh) & (ci == b), val, 0.0)
        db_ref[...] += upd

        for raw, d_ref, out, w_ref, dw_ref in ((qraw_ref, dq_ref, dq_out, wq_ref, dwq_ref),
                                               (kraw_ref, dk_ref, dk_out, wk_ref, dwk_ref)):
            acc = jnp.zeros((1, 128), F32)
            for c in range(t // rows):
                sl = slice(c * rows, (c + 1) * rows)
                xv, dyv = raw[sl, :], d_ref[sl, :]
                rs = _pair_rstd(xv, g2_ref)
                xh = xv * rs
                gg = dyv * w_ref[...]
                mean = _hdot(gg * xh, g2_ref[...], "b") * (1.0 / HD)
                out[sl, :] = (rs * (gg - xh * mean)).astype(MXU)
                acc = acc + jnp.sum(dyv * xh, axis=0, keepdims=True)
            dw_ref[...] = jnp.broadcast_to(acc, (8, 128))
        for c in range(t // rows):
            sl = slice(c * rows, (c + 1) * rows)
            dv_out[sl, :] = dv_ref[sl, :].astype(MXU)

    col = lambda off: pl.BlockSpec((t, 128), lambda i, off=off: (0, off + i))
    acc8 = pl.BlockSpec((8, 128), lambda i: (0, i))
    return pl.pallas_call(
        body, name="att_bwd", grid=(ATT_HEADS // 2,),
        in_specs=[BIAS_SPEC, pl.BlockSpec(bk.shape, lambda i: (0, 0, 0)),
                  col(0), col(4), col(8), col(0), col(0), col(0), PAIR_ROW_SPEC, PAIR_ROW_SPEC,
                  pl.BlockSpec((128, 128), lambda i: (0, 0))],
        out_specs=[col(0), col(0), col(0), pl.BlockSpec((8, 128), lambda i: (0, 0)), acc8, acc8],
        out_shape=[jax.ShapeDtypeStruct((t, 512), MXU)] * 3 + [jax.ShapeDtypeStruct((8, 128), F32)]
                  + [jax.ShapeDtypeStruct((8, 512), F32)] * 2,
        scratch_shapes=[pltpu.VMEM((len(PATTERNS), 2 * BLK, 2 * BLK), F32)] + [pltpu.VMEM((t, 128), F32)] * 5,
        compiler_params=_params(("arbitrary",)),
    )(bias, bk, qkv, qkv, qkv, do, lse, dd, wq, wk, g2)


def _dn_scan_bwd(qkv, ba, do, sh, th, arow, dtb):
    t = qkv.shape[0]
    n = t // CHUNK
    c = CHUNK
    cps = SCAN_CHUNKS

    def body(qkv_ref, ba_ref, do_ref, sh_ref, th_ref, arow_ref, dtb_ref, dqkv_ref, dba_ref, ds_ref):
        @pl.when(pl.program_id(0) == 0)
        def _():
            ds_ref[...] = jnp.zeros_like(ds_ref)

        hs = range(DN_HEADS)
        chains = [(j, h) for j in range(cps) for h in hs]
        lane = _iota((c, 128), 1)
        row = _iota((c, 1), 0)
        ms = {}
        for j in range(cps):
            rows_j = slice(j * c, (j + 1) * c)
            mj, ri, ci = _chunk_common(qkv_ref[rows_j, :], ba_ref[rows_j, :], arow_ref[...], dtb_ref[...])
            for h in hs:
                ms[j, h] = mj[h]
        q, k, v = ({x: ms[x][nm] for x in chains} for nm in ("q", "k", "v"))
        beta, decay = ({x: ms[x][nm] for x in chains} for nm in ("beta", "decay"))
        eg, egl, etail = ({x: ms[x][nm] for x in chains} for nm in ("eg", "egl", "etail"))
        s = {x: sh_ref[x[0], x[1]] for x in chains}
        tinv = {x: th_ref[x[0], x[1]] for x in chains}
        d_o = {(j, h): do_ref[j * c:(j + 1) * c, h * DK:(h + 1) * DK] for j, h in chains}
        kb = {x: k[x] * beta[x] for x in chains}
        vb = {x: v[x] * beta[x] for x in chains}
        kbg = {x: kb[x] * eg[x] for x in chains}
        amat = {x: jnp.where(ri > ci, _dot_nt(kb[x], k[x]) * decay[x], 0.0) for x in chains}
        attn = {x: jnp.where(ri >= ci, _dot_nt(q[x], k[x]) * decay[x], 0.0) for x in chains}
        uw = {x: _hdot(tinv[x], jnp.concatenate([vb[x], kbg[x]], axis=1)) for x in chains}
        u = {x: uw[x][:, :DK] for x in chains}
        w = {x: uw[x][:, DK:] for x in chains}
        v_new = {x: u[x] - _dot(w[x], s[x]) for x in chains}
        q_dec = {x: q[x] * eg[x] for x in chains}
        k_tail = {x: k[x] * etail[x] for x in chains}
        d_attn = {x: jnp.where(ri >= ci, _dot_nt(d_o[x], v_new[x]), 0.0) for x in chains}
        d_qdec = {x: _dot_nt(d_o[x], s[x]) for x in chains}
        from_o = {x: _dot_tn(attn[x], d_o[x]) for x in chains}
        to_state = {x: _dot_tn(q_dec[x], d_o[x]) for x in chains}

        d_s, d_vnew = {}, {}
        cur = [ds_ref[h] for h in hs]
        for j in reversed(range(cps)):
            for h in hs:
                d_s[j, h] = cur[h]
                d_vnew[j, h] = from_o[j, h] + _dot(k_tail[j, h], cur[h])
            cur = [to_state[j, h] + cur[h] * egl[j, h] - _dot_tn(w[j, h], d_vnew[j, h]) for h in hs]
        for h in hs:
            ds_ref[h] = cur[h]

        d_ktail = {x: _dot_nt(v_new[x], d_s[x]) for x in chains}
        d_gl = {x: jnp.sum(s[x] * d_s[x]) * egl[x] for x in chains}
        d_w = {x: -_dot_nt(d_vnew[x], s[x]) for x in chains}
        d_both = {x: _hdot_tn(tinv[x], jnp.concatenate([d_vnew[x], d_w[x]], axis=1)) for x in chains}
        d_vb = {x: d_both[x][:, :DK] for x in chains}
        d_kbg = {x: d_both[x][:, DK:] for x in chains}
        d_a = {x: -jnp.where(ri > ci, _hdot_nt(d_both[x], uw[x]), 0.0) for x in chains}
        d_qk = {x: d_attn[x] * decay[x] for x in chains}
        d_kk = {x: d_a[x] * decay[x] for x in chains}
        d_kb = {x: _dot(d_kk[x], k[x]) + d_kbg[x] * eg[x] for x in chains}
        d_q = {x: _dot(d_qk[x], k[x]) + d_qdec[x] * eg[x] for x in chains}
        d_k = {x: _dot_tn(d_qk[x], q[x]) + _dot_tn(d_kk[x], kb[x]) + d_ktail[x] * etail[x] + d_kb[x] * beta[x]
               for x in chains}
        d_beta = {x: jnp.sum(d_kb[x] * k[x] + d_vb[x] * v[x], axis=1, keepdims=True) for x in chains}
        mm = {x: d_a[x] * amat[x] + d_attn[x] * attn[x] for x in chains}
        for j in range(cps):
            rows_j = slice(j * c, (j + 1) * c)
            rows = jnp.zeros((c, c), F32)
            for h in hs:
                rows = rows + jnp.where(ri == h, jnp.sum(mm[j, h], axis=0, keepdims=True), 0.0)
            cols_t = jnp.concatenate([rows, jnp.zeros((c, c), F32)], axis=1).T[:c, :]
            d_gc_all = jnp.zeros((c, 128), F32)
            for h in hs:
                x = (j, h)
                tail_term = jnp.sum(d_ktail[x] * k_tail[x], axis=1, keepdims=True)
                d_gc = (jnp.sum(mm[x], axis=1, keepdims=True) - _lane_col(cols_t, h)
                        + jnp.sum(d_qdec[x] * q_dec[x] + d_kbg[x] * kbg[x], axis=1, keepdims=True) - tail_term)
                d_gc = d_gc + jnp.where(row == c - 1, jnp.sum(tail_term) + d_gl[x], 0.0)
                d_gc_all = d_gc_all + jnp.where(lane == DN_HEADS + h, d_gc, 0.0)
            d_g_all = _hdot((ri <= ci).astype(F32), d_gc_all, "a")
            dba = jnp.zeros((c, 128), F32)
            for h in hs:
                x = (j, h)
                d_g = _lane_col(d_g_all, DN_HEADS + h)
                d_braw = d_beta[x] * beta[x] * (1.0 - beta[x])
                d_araw = d_g * ms[x]["a_h"] * _sigmoid(ms[x]["a_raw"] + ms[x]["dt_h"])
                dba = dba + jnp.where(lane == h, d_braw, 0.0) + jnp.where(lane == DN_HEADS + h, d_araw, 0.0) \
                    + jnp.where(lane == 2 * DN_HEADS + h, d_g * ms[x]["g"], 0.0)
                dqkv_ref[rows_j, h * DK:(h + 1) * DK] = d_q[x]
                dqkv_ref[rows_j, D_DN + h * DK:D_DN + (h + 1) * DK] = d_k[x]
                dqkv_ref[rows_j, 2 * D_DN + h * DK:2 * D_DN + (h + 1) * DK] = d_vb[x] * beta[x]
            dba_ref[rows_j, :] = dba

    nsteps = n // cps
    rev = lambda w_: pl.BlockSpec((cps * c, w_), lambda i: (nsteps - 1 - i, 0))
    one = pl.BlockSpec((1, 128), lambda i: (0, 0))
    return pl.pallas_call(
        body, name="dn_scan_bwd", grid=(nsteps,),
        in_specs=[rev(1536), rev(128), rev(512),
                  pl.BlockSpec((cps, DN_HEADS, DK, DK), lambda i: (nsteps - 1 - i, 0, 0, 0)),
                  pl.BlockSpec((cps, DN_HEADS, c, c), lambda i: (nsteps - 1 - i, 0, 0, 0)), one, one],
        out_specs=[rev(1536), rev(128)],
        out_shape=[jax.ShapeDtypeStruct((t, 1536), F32), jax.ShapeDtypeStruct((t, 128), F32)],
        scratch_shapes=[pltpu.VMEM((DN_HEADS, DK, DK), F32)],
        compiler_params=_params(("arbitrary",)),
    )(qkv, ba, do, sh, th, arow, dtb)


def _dn_prep_bwd(pdn, cw, dact):
    t = pdn.shape[0]
    nchunk = t // CONV_ROWS

    def body(u_ref, w_ref, d_ref, du_ref, dw_ref, dy_ref):
        j = pl.program_id(0)
        dy_ref[t:t + 8, :] = jnp.zeros((8, 128), F32)
        dw = [jnp.zeros((1, 128), F32) for _ in range(4)]
        for c in range(nchunk):
            sl = slice(c * CONV_ROWS, (c + 1) * CONV_ROWS)
            taps, y = _conv_taps(u_ref, c, w_ref)
            a, da_dy = _silu_and_grad(y)
            dout = d_ref[sl, :]
            rs = lax.rsqrt(jnp.sum(a * a, axis=1, keepdims=True) + EPS)
            f = jnp.where(j < 8, rs, 1.0) * jnp.where(j < 4, DK ** -0.5, 1.0)
            corr = jnp.where(j < 8, f * rs * rs * jnp.sum(dout * a, axis=1, keepdims=True), 0.0)
            dy = (f * dout - corr * a) * da_dy
            dy_ref[sl, :] = dy
            for k_ in range(4):
                dw[3 - k_] = dw[3 - k_] + jnp.sum(taps[k_] * dy, axis=0, keepdims=True)
        for i in range(4):
            dw_ref[i:i + 1, :] = dw[i]
        for c in range(nchunk):
            r0 = c * CONV_ROWS
            ext = dy_ref[r0:r0 + CONV_ROWS + 8, :]
            du = ext[:CONV_ROWS, :] * w_ref[3:4, :]
            for k_ in (1, 2, 3):
                du = du + pltpu.roll(ext, CONV_ROWS + 8 - k_, 0)[:CONV_ROWS, :] * w_ref[3 - k_:4 - k_, :]
            du_ref[r0:r0 + CONV_ROWS, :] = du.astype(MXU)

    return pl.pallas_call(
        body, name="dn_prep_bwd", grid=(12,),
        in_specs=[pl.BlockSpec((t, 128), lambda j: (0, j)), pl.BlockSpec((4, 128), lambda j: (0, j)),
                  pl.BlockSpec((t, 128), lambda j: (0, j))],
        out_specs=[pl.BlockSpec((t, 128), lambda j: (0, j)), pl.BlockSpec((4, 128), lambda j: (0, j))],
        out_shape=[jax.ShapeDtypeStruct((t, 1536), MXU), jax.ShapeDtypeStruct((4, 1536), F32)],
        scratch_shapes=[pltpu.VMEM((t + 8, 128), F32)],
        compiler_params=_params(("arbitrary",)),
    )(pdn, cw, dact)


SECTIONS = (("dn", 0, 1536), ("z", 1536, 512), ("q", 2048, 512), ("k", 2560, 512), ("v", 3072, 512),
            ("gate", 3584, 512), ("ba", 4096, 128))


def _inproj_bwd(x, nw, wt, dy, dsecs, partials):
    t = x.shape[0]
    tm = 256
    npart = len(partials)
    nsteps = t // tm

    nsec = len(SECTIONS)

    def body(x_ref, nw_ref, w_ref, dy_ref, *rest):
        sec_refs, rest = rest[:nsec], rest[nsec:]
        part_refs, (gx_ref, dnw_ref, cs_ref) = rest[:npart], rest[npart:npart + 3]
        got_refs, (send, recv, loc) = rest[npart + 3:2 * npart + 3], rest[2 * npart + 3:]
        starts, waits = _chip_swap_copies(part_refs, got_refs, send, recv, loc)

        @pl.when(pl.program_id(0) == 0)
        def _():
            for start in starts:
                start()
            dnw_ref[...] = jnp.zeros_like(dnw_ref)
            cs_ref[...] = jnp.zeros_like(cs_ref)

        @pl.when(pl.program_id(0) == nsteps - 1)
        def _():
            for wait in waits:
                wait()

        dh = jnp.zeros((tm, D_MODEL), F32)
        for ref, (_, lo, width) in zip(sec_refs, SECTIONS):
            dh = dh + jnp.dot(ref[...].astype(MXU), w_ref[lo:lo + width, :], preferred_element_type=F32)
        xv = x_ref[...]
        rstd = lax.rsqrt(jnp.mean(xv * xv, axis=-1, keepdims=True) + EPS)
        xh = xv * rstd
        gg = dh * nw_ref[...]
        gx_ref[...] = rstd * (gg - xh * jnp.mean(gg * xh, axis=-1, keepdims=True)) + dy_ref[...]
        dnw_ref[...] += jnp.broadcast_to(jnp.sum(dh * xh, axis=0, keepdims=True), (8, D_MODEL))
        cs_ref[...] += jnp.broadcast_to(jnp.sum(sec_refs[nsec - 1][...], axis=0, keepdims=True), (8, 128))

    row = lambda n: pl.BlockSpec((tm, n), lambda i: (i, 0))
    full = lambda a: pl.BlockSpec(a.shape, lambda i: (0,) * a.ndim)
    res = pl.pallas_call(
        body, name="inproj_bwd", grid=(nsteps,),
        in_specs=[row(D_MODEL), full(nw), full(wt), row(D_MODEL)] + [row(width) for _, _, width in SECTIONS]
                 + [ANY_SPEC] * npart,
        out_specs=[row(D_MODEL), pl.BlockSpec((8, D_MODEL), lambda i: (0, 0)), pl.BlockSpec((8, 128), lambda i: (0, 0))]
                  + [ANY_SPEC] * npart,
        out_shape=[jax.ShapeDtypeStruct((t, D_MODEL), F32), jax.ShapeDtypeStruct((8, D_MODEL), F32),
                   jax.ShapeDtypeStruct((8, 128), F32)] + [jax.ShapeDtypeStruct(p.shape, p.dtype) for p in partials],
        scratch_shapes=[pltpu.SemaphoreType.DMA((npart, 3)), pltpu.SemaphoreType.DMA((npart, 3)),
                        pltpu.SemaphoreType.DMA((npart,))],
        compiler_params=_params(("arbitrary",)),
    )(x, nw, wt, dy, *dsecs, *partials)
    return res[0], res[1], res[2], res[3:]


def _adamw_sum(w, gs, m, v, name):
    r, c = w.shape
    nsum = gs.shape[0]
    tr = r if r <= 512 else 512
    c1 = 1.0 - ADAM_B1 ** ADAM_STEP
    c2 = 1.0 - ADAM_B2 ** ADAM_STEP

    def body(w_ref, g_ref, m_ref, v_ref, go_ref, d_ref, mo_ref, vo_ref):
        g = g_ref[0].astype(F32)
        for s in range(1, nsum):
            g = g + g_ref[s].astype(F32)
        mn = ADAM_B1 * m_ref[...] + (1.0 - ADAM_B1) * g
        vn = ADAM_B2 * v_ref[...] + (1.0 - ADAM_B2) * (g * g)
        go_ref[...] = g
        mo_ref[...] = mn
        vo_ref[...] = vn
        d_ref[...] = -ADAM_LR * ((mn / c1) / (jnp.sqrt(vn / c2) + ADAM_EPS) + ADAM_WD * w_ref[...])

    blk = pl.BlockSpec((tr, c), lambda i: (i, 0))
    return pl.pallas_call(
        body, name=name, grid=(r // tr,),
        in_specs=[blk, pl.BlockSpec((nsum, tr, c), lambda i: (0, i, 0)), blk, blk],
        out_specs=[blk] * 4, out_shape=[jax.ShapeDtypeStruct((r, c), F32)] * 4,
        compiler_params=_params(("arbitrary",)),
    )(w, gs, m, v)


def _local_step(x, target, h, ht, bias, bk, w_sect, conv_w, a_log, dt_bias, dn_norm_w, q_norm_w, k_norm_w, wout_shard):
    arow = jnp.zeros((1, 128), F32).at[0, DN_HEADS:2 * DN_HEADS].set(-jnp.exp(a_log[0]))
    dtb = jnp.zeros((1, 128), F32).at[0, DN_HEADS:2 * DN_HEADS].set(dt_bias[0])
    g_np, gt_np = _group_mats()
    g, gt = jnp.asarray(g_np), jnp.asarray(gt_np)
    g2 = jnp.asarray(np.kron(np.eye(2, dtype=np.float32), np.ones((HD, HD), np.float32)))
    wq = jnp.tile(q_norm_w, (1, ATT_HEADS)) * (HD ** -0.5)
    wk = jnp.tile(k_norm_w, (1, ATT_HEADS))

    pdn, qkv_dn, z, patt, gate, ba, wout8 = _inproj(h, w_sect, conv_w, wout_shard)
    w_out = wout8.reshape(D_MODEL, D_MODEL)
    oraw, ydn, sh, th = _dn_scan_fwd(qkv_dn, ba, z, arow, dtb, dn_norm_w)
    oatt, yatt, lse = _att_fwd(patt, gate, bias, wq, wk, g2)
    dy, mix_t, loss8, do_dn, dz, do_att, dgate, dd, ddnw = _outproj_both(
        x, ydn, yatt, w_out, target, w_out.T, oraw, z, dn_norm_w, oatt, gate, g, gt)
    d_wout = _grad_matmul(mix_t, dy, "dw_out")
    dq, dk, dv, drb, dwq8, dwk8 = _att_bwd(patt, do_att, lse, dd, bias, bk, wq, wk, g2)
    dqkv_dn, dba = _dn_scan_bwd(qkv_dn, ba, do_dn, sh, th, arow, dtb)
    dpdn, d_conv = _dn_prep_bwd(pdn, conv_w, dqkv_dn)
    dsecs = (dpdn, dz, dq, dk, dv, dgate, dba)
    dw_mid = _grad_matmul_many(ht, dsecs[1:6], "dw_in_mid")
    dw_sections = [_grad_matmul(ht, dpdn, "dw_in_dn"), *dw_mid, _grad_matmul(ht, dba, "dw_in_ba")]
    return dict(w_in_sections=dw_sections, conv_w=d_conv, w_out=d_wout, dy=dy, dsecs=dsecs,
                small_parts=(loss8, ddnw, dwq8, dwk8, drb))


def _finish_step(x, norm_w, w_sect_t, gr, partials):
    grad_x, dnw8, cs8, got = _inproj_bwd(x, norm_w, w_sect_t, gr["dy"], gr["dsecs"], partials)
    return grad_x, _pack_small_grads(dnw8, cs8, *gr["small_parts"]), got


SMALL_ROWS = 24
SMALL_AT = dict(a_log=(slice(8, 9), slice(0, 4)), dt_bias=(slice(9, 10), slice(0, 4)),
                dn_norm_w=(slice(10, 11), slice(0, 128)), q_norm_w=(slice(11, 12), slice(0, HD)),
                k_norm_w=(slice(12, 13), slice(0, HD)), rel_bias=(slice(16, 24), slice(0, N_BUCKETS)))
SMALL_NAMES = ("norm_w", "a_log", "dt_bias", "dn_norm_w", "q_norm_w", "k_norm_w", "rel_bias")


LOSS_ROW = 13


def _pack_small_grads(dnw8, cs8, loss8, ddnw8, dwq8, dwk8, drb):
    def body(dnw_ref, cs_ref, loss_ref, ddnw_ref, dwq_ref, dwk_ref, drb_ref, o_ref):
        lane = _iota((8, 128), 1)
        o_ref[...] = jnp.zeros_like(o_ref)
        o_ref[LOSS_ROW:LOSS_ROW + 1, :] = jnp.where(lane == 0, loss_ref[...], 0.0)[0:1, :]
        for k in range(D_MODEL // 128):
            o_ref[k:k + 1, :] = dnw_ref[0:1, k * 128:(k + 1) * 128]
        cs = cs_ref[...]
        o_ref[8:9, :] = jnp.where(lane < DN_HEADS, pltpu.roll(cs, 128 - 2 * DN_HEADS, 1), 0.0)[0:1, :]
        o_ref[9:10, :] = jnp.where(lane < DN_HEADS, pltpu.roll(cs, 128 - DN_HEADS, 1), 0.0)[0:1, :]
        o_ref[10:11, :] = ddnw_ref[0:1, :]
        for row, ref, scale in ((11, dwq_ref, HD ** -0.5), (12, dwk_ref, 1.0)):
            acc = ref[:, 0:128] + ref[:, 128:256] + ref[:, 256:384] + ref[:, 384:512]
            acc = (acc + pltpu.roll(acc, HD, 1)) * scale
            o_ref[row:row + 1, :] = jnp.where(lane < HD, acc, 0.0)[0:1, :]
        o_ref[16:24, :] = drb_ref[...]

    return pl.pallas_call(body, name="pack_small_grads", out_shape=jax.ShapeDtypeStruct((SMALL_ROWS, 128), F32),
                          )(dnw8, cs8, loss8, ddnw8, dwq8, dwk8, drb)


def _adam_math(w, g, m, v):
    c1 = 1.0 - ADAM_B1 ** ADAM_STEP
    c2 = 1.0 - ADAM_B2 ** ADAM_STEP
    mn = ADAM_B1 * m + (1.0 - ADAM_B1) * g
    vn = ADAM_B2 * v + (1.0 - ADAM_B2) * (g * g)
    return -ADAM_LR * ((mn / c1) / (jnp.sqrt(vn / c2) + ADAM_EPS) + ADAM_WD * w), mn, vn


def _adamw_small(gs, ws, ms, vs):
    n = len(SMALL_NAMES)

    def body(g_ref, *refs):
        w_refs, m_refs, v_refs = refs[:n], refs[n:2 * n], refs[2 * n:3 * n]
        outs, loss_ref = refs[3 * n:7 * n], refs[7 * n]
        loss = g_ref[0, LOSS_ROW:LOSS_ROW + 1, :]
        for s in range(1, gs.shape[0]):
            loss = loss + g_ref[s, LOSS_ROW:LOSS_ROW + 1, :]
        loss_ref[...] = loss

        def one(i, rows, lanes, at):
            g = g_ref[0, rows, lanes]
            for s in range(1, gs.shape[0]):
                g = g + g_ref[s, rows, lanes]
            d, mn, vn = _adam_math(w_refs[i][at], g, m_refs[i][at], v_refs[i][at])
            for kind, val in enumerate((g, d, mn, vn)):
                outs[kind * n + i][at] = val

        for k in range(D_MODEL // 128):
            one(0, slice(k, k + 1), slice(0, 128), (slice(0, 1), slice(k * 128, (k + 1) * 128)))
        for i, nm in enumerate(SMALL_NAMES[1:], start=1):
            rows, lanes = SMALL_AT[nm]
            one(i, rows, lanes, (slice(None), slice(None)))

    shapes = [jax.ShapeDtypeStruct(w.shape, F32) for w in ws]
    res = pl.pallas_call(body, name="adamw_small",
                         out_shape=shapes * 4 + [jax.ShapeDtypeStruct((1, 128), F32)])(gs, *ws, *ms, *vs)
    return [res[k * n:(k + 1) * n] for k in range(4)], res[4 * n]


def kernel(x, norm_w, w_in, conv_w, a_log, dt_bias, dn_norm_w, q_norm_w, k_norm_w, rel_bias, w_out, loss_target, m_norm_w, m_w_in, m_conv_w, m_a_log, m_dt_bias, m_dn_norm_w, m_q_norm_w, m_k_norm_w, m_rel_bias, m_w_out, v_norm_w, v_w_in, v_conv_w, v_a_log, v_dt_bias, v_dn_norm_w, v_q_norm_w, v_k_norm_w, v_rel_bias, v_w_out):
    assert w_in.shape[2] == SHARD_COLS
    bk = jnp.asarray(_bucket_tables())
    h, ht, bias, (win8, conv8) = _norm_and_gather(x[0], norm_w, [w_in[0].astype(MXU), conv_w[0]], rel_bias, bk)
    w_sect, w_sect_t = _build_w(win8)
    conv_full = conv8.transpose(1, 0, 2).reshape(4, 3 * D_DN)

    gr = _local_step(x[0], loss_target[0], h, ht, bias, bk, w_sect, conv_full, a_log, dt_bias, dn_norm_w, q_norm_w,
                     k_norm_w, w_out[0].astype(MXU))

    slabs = [_build_slabs(gr["w_in_sections"]),
             gr["w_out"].reshape(4, 2, D_MODEL // N_DEV, D_MODEL).transpose(1, 0, 2, 3),
             gr["conv_w"].reshape(4, 4, 2, 3 * D_DN // N_DEV).transpose(2, 1, 0, 3)]
    core = lax.axis_index("c").astype(jnp.int32).reshape(1)
    from_sibling = _swap_siblings(slabs)
    wires = (GRAD_WIRE, GRAD_WIRE, F32)
    partial = [_chip_sum(slabs[i], from_sibling[i], core, wires[i], "chip_sum_%d" % i) for i in range(3)]
    grad_x, small_pack, (r_win, r_wout, r_conv) = _finish_step(x[0], norm_w, w_sect_t, gr, partial)
    r_small = _share_small(small_pack)

    g_win, d_win, m_win, v_win = _adamw_sum(w_in[0], r_win, m_w_in[0], v_w_in[0], "adamw_w_in")
    g_wout, d_wout, m_wout, v_wout = _adamw_sum(w_out[0], r_wout, m_w_out[0], v_w_out[0], "adamw_w_out")
    g_conv, d_conv, m_conv, v_conv = _adamw_sum(conv_w[0], r_conv, m_conv_w[0], v_conv_w[0], "adamw_conv_w")
    small, loss_row = _adamw_small(r_small,
                                   (norm_w, a_log, dt_bias, dn_norm_w, q_norm_w, k_norm_w, rel_bias),
                                   (m_norm_w, m_a_log, m_dt_bias, m_dn_norm_w, m_q_norm_w, m_k_norm_w, m_rel_bias),
                                   (v_norm_w, v_a_log, v_dt_bias, v_dn_norm_w, v_q_norm_w, v_k_norm_w, v_rel_bias))

    loss = loss_row[0, 0]
    names = ("norm_w", "w_in", "conv_w", "a_log", "dt_bias", "dn_norm_w", "q_norm_w", "k_norm_w", "rel_bias", "w_out")
    big = dict(w_in=(g_win, d_win, m_win, v_win), conv_w=(g_conv, d_conv, m_conv, v_conv),
               w_out=(g_wout, d_wout, m_wout, v_wout))
    outs = [loss, grad_x[None]]
    for kind in range(4):
        for nm in names:
            outs.append(big[nm][kind][None] if nm in big else small[kind][SMALL_NAMES.index(nm)])
    return tuple(outs)
```

```python
import math

import numpy as np
import jax
import jax.numpy as jnp
from jax import lax
from jax.experimental import pallas as pl
from jax.experimental.pallas import tpu as pltpu

F32 = jnp.float32
MXU = jnp.bfloat16
GRAD_WIRE = jnp.bfloat16

D_MODEL = 1024
D_DN = 512
DN_HEADS = 4
DK = 128
CHUNK = 64
D_ATT = 512
ATT_HEADS = 8
HD = 64
PATTERNS = ((128, 1), (512, 4), (2048, 16))
BLK = 128
N_BUCKETS = 32
MAX_DISTANCE = 2048
EPS = 1e-6
W_COLS = 4224
N_DEV = 8
AXES = ("x", "y", "c")

ADAM_LR = 0.001
ADAM_B1 = 0.9
ADAM_B2 = 0.999
ADAM_EPS = 1e-08
ADAM_WD = 0.01
ADAM_STEP = 10

VMEM_LIMIT = 56 * 1024 * 1024
NEG = -1e30


def _dot(a, b):
    return jnp.dot(a.astype(MXU), b.astype(MXU), preferred_element_type=F32)


def _dot_nt(a, b):
    return lax.dot_general(a.astype(MXU), b.astype(MXU), (((1,), (1,)), ((), ())), preferred_element_type=F32)


def _dot_tn(a, b):
    return lax.dot_general(a.astype(MXU), b.astype(MXU), (((0,), (0,)), ((), ())), preferred_element_type=F32)


def _split(a):
    hi = a.astype(jnp.bfloat16)
    return hi, (a - hi.astype(F32)).astype(jnp.bfloat16)


def _dot_split(a, b, dims, exact):
    dg = lambda u, v: lax.dot_general(u, v, (dims, ((), ())), preferred_element_type=F32)
    if exact == "b":
        ah, al = _split(a)
        bh = b.astype(jnp.bfloat16)
        return dg(ah, bh) + dg(al, bh)
    if exact == "a":
        bh, bm = _split(b)
        bl = (b - bh.astype(F32) - bm.astype(F32)).astype(jnp.bfloat16)
        ah = a.astype(jnp.bfloat16)
        return dg(ah, bh) + (dg(ah, bm) + dg(ah, bl))
    ah, al = _split(a)
    bh, bl = _split(b)
    return dg(ah, bh) + (dg(ah, bl) + dg(al, bh))


def _wy_inverses(amat, eye):
    tinv = {x: eye - amat[x] for x in amat}
    pw = amat
    for _ in range(5):
        pw = {x: _hdot(pw[x], pw[x]) for x in amat}
        tinv = {x: tinv[x] + _hdot(tinv[x], pw[x]) for x in amat}
    return tinv


def _hdot(a, b, exact=None):
    return _dot_split(a, b, ((1,), (0,)), exact)


def _hdot_nt(a, b, exact=None):
    return _dot_split(a, b, ((1,), (1,)), exact)


def _hdot_tn(a, b, exact=None):
    return _dot_split(a, b, ((0,), (0,)), exact)


def _sigmoid(x):
    return 1.0 / (1.0 + jnp.exp(-x))


def _silu(x):
    return x * _sigmoid(x)


def _silu_and_grad(x):
    s = _sigmoid(x)
    return x * s, s * (1.0 + x * (1.0 - s))


def _softplus(x):
    return jnp.maximum(x, 0.0) + jnp.log(1.0 + jnp.exp(-jnp.abs(x)))


def _iota(shape, dim):
    return lax.broadcasted_iota(jnp.int32, shape, dim)


def _lane_col(x, k):
    return jnp.sum(jnp.where(_iota(x.shape, 1) == k, x, 0.0), axis=1, keepdims=True)


def _params(sem=None):
    return pltpu.CompilerParams(dimension_semantics=sem, vmem_limit_bytes=VMEM_LIMIT)


def _t5_bucket(dist):
    max_exact = N_BUCKETS // 2
    d = np.maximum(dist, 1).astype(np.float64)
    large = max_exact + (np.log(d / max_exact) / math.log(MAX_DISTANCE / max_exact)
                         * (N_BUCKETS - max_exact)).astype(np.int32)
    large = np.minimum(large, N_BUCKETS - 1)
    return np.where(dist < max_exact, dist, large).astype(np.int32)


def _bucket_tables():
    qi = np.arange(BLK)[:, None]
    kj = np.arange(2 * BLK)[None, :]
    step = qi - kj + BLK
    band = (step >= 0) & (step <= BLK)
    out = []
    for _, r in PATTERNS:
        b = _t5_bucket(np.clip(step, 0, None) * r)
        out.append(np.where(band, b, -1))
    return np.stack(out).astype(np.int32)


def _group_mats():
    g = np.zeros((D_ATT, 128), np.float32)
    for h in range(ATT_HEADS):
        g[h * HD:(h + 1) * HD, h] = 1.0
    return g, np.ascontiguousarray(g.T)


CHIP_FLIPS = ((1, 0), (0, 1), (1, 1))
ANY_SPEC = pl.BlockSpec(memory_space=pl.ANY)
MESH_ID = pl.DeviceIdType.MESH


def _other_chips():
    x, y = lax.axis_index("x"), lax.axis_index("y")
    return [((1 - x if fx else x), (1 - y if fy else y)) for fx, fy in CHIP_FLIPS]


def _gather_plan(ins, outs, send, recv, loc):
    n = len(ins)
    x, y, c = (lax.axis_index(a) for a in AXES)
    sib = (x, y, 1 - c)
    chips = _other_chips()
    lin = lambda px, py, pc: 4 * px + 2 * py + pc

    def copy(a, k, block, to, src=None):
        slot = outs[a].at[lin(*block)]
        return pltpu.make_async_remote_copy(src_ref=slot if src is None else src, dst_ref=slot,
                                            send_sem=send.at[a, k], recv_sem=recv.at[a, k],
                                            device_id=to, device_id_type=MESH_ID)

    mine = [pltpu.make_async_copy(ins[a], outs[a].at[lin(x, y, c)], loc.at[a]) for a in range(n)]
    firsts = []
    for a in range(n):
        firsts.append(copy(a, 0, (x, y, c), sib, src=ins[a]))
        firsts += [copy(a, 1 + j, (x, y, c), (*chip, c), src=ins[a]) for j, chip in enumerate(chips)]

    def begin():
        for cp in mine + firsts:
            cp.start()

    def finish():
        passed = []
        for j, chip in enumerate(chips):
            for a in range(n):
                copy(a, 1 + j, (*chip, c), (x, y, c)).wait_recv()
                fw = copy(a, 4 + j, (*chip, c), sib)
                fw.start()
                passed.append(fw)
        for a in range(n):
            copy(a, 0, sib, (x, y, c)).wait_recv()
            for j, chip in enumerate(chips):
                copy(a, 4 + j, (*chip, 1 - c), (x, y, c)).wait_recv()
        for cp in firsts + passed:
            cp.wait_send()
        for cp in mine:
            cp.wait()

    return begin, finish


GATHER_SEMS = lambda n: [pltpu.SemaphoreType.DMA((n, 7)), pltpu.SemaphoreType.DMA((n, 7)), pltpu.SemaphoreType.DMA((n,))]


def _swap_siblings(arrs):
    n = len(arrs)

    def body(*refs):
        ins, outs = refs[:n], refs[n:2 * n]
        send, recv = refs[2 * n:]
        x, y, c = (lax.axis_index(a) for a in AXES)
        cps = [pltpu.make_async_remote_copy(src_ref=ins[a].at[1 - c], dst_ref=outs[a], send_sem=send.at[a],
                                            recv_sem=recv.at[a], device_id=(x, y, 1 - c), device_id_type=MESH_ID)
               for a in range(n)]
        for cp in cps:
            cp.start()
        for cp in cps:
            cp.wait()

    return pl.pallas_call(
        body, name="swap_siblings", out_shape=[jax.ShapeDtypeStruct(a.shape[1:], a.dtype) for a in arrs],
        in_specs=[ANY_SPEC] * n, out_specs=[ANY_SPEC] * n,
        scratch_shapes=[pltpu.SemaphoreType.DMA((n,)), pltpu.SemaphoreType.DMA((n,))],
    )(*arrs)


def _chip_sum(mine2, theirs, core, wire, name):
    _, nchip, r, cdim = mine2.shape
    tr = r if r <= 1024 else 1024

    def body(core_ref, a_ref, b_ref, o_ref):
        del core_ref
        o_ref[...] = (a_ref[...].astype(F32) + b_ref[...].astype(F32)).astype(wire)

    grid_spec = pltpu.PrefetchScalarGridSpec(
        num_scalar_prefetch=1, grid=(nchip, r // tr),
        in_specs=[pl.BlockSpec((None, None, tr, cdim), lambda j, i, cr: (cr[0], j, i, 0)),
                  pl.BlockSpec((None, tr, cdim), lambda j, i, cr: (j, i, 0))],
        out_specs=pl.BlockSpec((None, tr, cdim), lambda j, i, cr: (j, i, 0)))
    return pl.pallas_call(
        body, name=name, grid_spec=grid_spec, out_shape=jax.ShapeDtypeStruct((nchip, r, cdim), wire),
        compiler_params=_params(("arbitrary", "arbitrary")),
    )(core, mine2, theirs)


def _chip_swap_copies(ins, outs, send, recv, loc):
    x, y, c = (lax.axis_index(a) for a in AXES)
    me = 2 * x + y
    starts, arrivals, drains = [], [], []
    for a in range(len(ins)):
        lc = pltpu.make_async_copy(ins[a].at[me], outs[a].at[me], loc.at[a])
        starts.append(lc.start)
        drains.append(lc.wait)
        for j, (px, py) in enumerate(_other_chips()):
            them = 2 * px + py
            cp = pltpu.make_async_remote_copy(src_ref=ins[a].at[them], dst_ref=outs[a].at[me], send_sem=send.at[a, j],
                                              recv_sem=recv.at[a, j], device_id=(px, py, c), device_id_type=MESH_ID)
            landing = pltpu.make_async_remote_copy(src_ref=ins[a].at[them], dst_ref=outs[a].at[them],
                                                   send_sem=send.at[a, j], recv_sem=recv.at[a, j],
                                                   device_id=(px, py, c), device_id_type=MESH_ID)
            starts.append(cp.start)
            arrivals.append(landing.wait_recv)
            drains.append(cp.wait_send)
    return starts, arrivals + drains


def _share_small(pack):
    def body(in_ref, out_ref, send, recv, loc):
        x, y, c = (lax.axis_index(a) for a in AXES)
        me = 4 * x + 2 * y + c
        lc = pltpu.make_async_copy(in_ref, out_ref.at[me], loc.at[0])
        lc.start()
        sends, arrivals = [], []
        for k in range(1, N_DEV):
            px = 1 - x if k & 4 else x
            py = 1 - y if k & 2 else y
            pc = 1 - c if k & 1 else c
            cp = pltpu.make_async_remote_copy(src_ref=in_ref, dst_ref=out_ref.at[me], send_sem=send.at[k - 1],
                                              recv_sem=recv.at[k - 1], device_id=(px, py, pc), device_id_type=MESH_ID)
            cp.start()
            sends.append(cp)
            arrivals.append(pltpu.make_async_remote_copy(src_ref=in_ref, dst_ref=out_ref.at[4 * px + 2 * py + pc],
                                                         send_sem=send.at[k - 1], recv_sem=recv.at[k - 1],
                                                         device_id=(px, py, pc), device_id_type=MESH_ID))
        for cp in arrivals:
            cp.wait_recv()
        for cp in sends:
            cp.wait_send()
        lc.wait()

    return pl.pallas_call(
        body, name="share_small", out_shape=jax.ShapeDtypeStruct((N_DEV,) + pack.shape, pack.dtype),
        in_specs=[ANY_SPEC], out_specs=ANY_SPEC,
        scratch_shapes=[pltpu.SemaphoreType.DMA((N_DEV - 1,)), pltpu.SemaphoreType.DMA((N_DEV - 1,)),
                        pltpu.SemaphoreType.DMA((1,))],
    )(pack)


W_PARTS = ((0, 0, 2048), (2048, 4096, 8), (2056, 2048, 2048))
SHARD_COLS = 513


def _pieces(lo, hi, parts):
    out = []
    for ref_start, tgt_start, width in parts:
        a, b = max(lo, ref_start), min(hi, ref_start + width)
        if a < b:
            out.append((a - lo, tgt_start + a - ref_start, b - a))
    return out


def _build_w(win8):
    tr = 512

    def body(in_ref, w_ref, wt_ref):
        w_ref[:, 4096:W_COLS] = jnp.zeros((tr, W_COLS - 4096), MXU)
        for p in range(N_DEV):
            for src, dst, width in _pieces(p * SHARD_COLS, (p + 1) * SHARD_COLS, W_PARTS):
                w_ref[:, dst:dst + width] = in_ref[p, :, src:src + width]
        for k in range(W_COLS // 128):
            wt_ref[k * 128:(k + 1) * 128, :] = w_ref[:, k * 128:(k + 1) * 128].astype(F32).T.astype(MXU)

    return pl.pallas_call(
        body, name="build_w", grid=(D_MODEL // tr,),
        in_specs=[pl.BlockSpec((N_DEV, tr, SHARD_COLS), lambda i: (0, i, 0))],
        out_specs=[pl.BlockSpec((tr, W_COLS), lambda i: (i, 0)), pl.BlockSpec((W_COLS, tr), lambda i: (0, i))],
        out_shape=[jax.ShapeDtypeStruct((D_MODEL, W_COLS), MXU), jax.ShapeDtypeStruct((W_COLS, D_MODEL), MXU)],
        compiler_params=_params(("arbitrary",)),
    )(win8)


def _build_slabs(secs):
    tr = 512
    parts = ((0, 0, 1536), (1536, 1, 512), (2048, 6, 8), (2056, 2, 512), (2568, 3, 512), (3080, 4, 512),
             (3592, 5, 512))

    def body(*refs):
        o_ref = refs[len(secs)]
        for p in range(N_DEV):
            lo, hi = p * SHARD_COLS, (p + 1) * SHARD_COLS
            for ref_start, idx, width in parts:
                a, b = max(lo, ref_start), min(hi, ref_start + width)
                if a < b:
                    o_ref[p % 2, p // 2, :, a - lo:b - lo] = refs[idx][:, a - ref_start:b - ref_start]

    return pl.pallas_call(
        body, name="build_slabs", grid=(D_MODEL // tr,),
        in_specs=[pl.BlockSpec((tr, s.shape[1]), lambda i: (i, 0)) for s in secs],
        out_specs=pl.BlockSpec((2, 4, tr, SHARD_COLS), lambda i: (0, 0, i, 0)),
        out_shape=jax.ShapeDtypeStruct((2, 4, D_MODEL, SHARD_COLS), secs[0].dtype),
        compiler_params=_params(("arbitrary",)),
    )(*secs)


def _fill_bias(rb_ref, bk_ref, bias_ref, pair):
    for p in range(len(PATTERNS)):
        bk_p = bk_ref[p]
        for hh in range(2):
            head = 2 * pair + hh
            bm = jnp.full((BLK, 2 * BLK), NEG, F32)
            for b in range(N_BUCKETS):
                bm = jnp.where(bk_p == b, rb_ref[head, b], bm)
            bias_ref[p, hh * BLK:(hh + 1) * BLK, :] = bm


def _norm_and_gather(x, nw, shards, rb, bk):
    t = x.shape[0]
    tm = 512
    nsteps = t // tm
    n = len(shards)
    npair = ATT_HEADS // 2
    assert nsteps >= npair

    def body(x_ref, nw_ref, *rest):
        ins, (rb_ref, bk_ref, h_ref, ht_ref, bias_ref) = rest[:n], rest[n:n + 5]
        outs, sems = rest[n + 5:2 * n + 5], rest[2 * n + 5:]
        begin, finish = _gather_plan(ins, outs, *sems)
        step = pl.program_id(0)

        @pl.when(step == 0)
        def _():
            begin()

        @pl.when(step == nsteps - 1)
        def _():
            finish()

        @pl.when(step < npair)
        def _():
            _fill_bias(rb_ref, bk_ref, bias_ref, step)

        xv = x_ref[...]
        rstd = lax.rsqrt(jnp.mean(xv * xv, axis=-1, keepdims=True) + EPS)
        hf = xv * rstd * nw_ref[...]
        h_ref[...] = hf.astype(MXU)
        ht_ref[...] = hf.T.astype(MXU)

    res = pl.pallas_call(
        body, name="norm_and_gather", grid=(nsteps,),
        in_specs=[pl.BlockSpec((tm, D_MODEL), lambda i: (i, 0)), pl.BlockSpec(nw.shape, lambda i: (0, 0))]
                 + [ANY_SPEC] * n
                 + [pl.BlockSpec(memory_space=pltpu.SMEM), pl.BlockSpec(bk.shape, lambda i: (0, 0, 0))],
        out_specs=[pl.BlockSpec((tm, D_MODEL), lambda i: (i, 0)), pl.BlockSpec((D_MODEL, tm), lambda i: (0, i)),
                   pl.BlockSpec((len(PATTERNS), None, 2 * BLK, 2 * BLK), lambda i: (0, jnp.minimum(i, npair - 1), 0, 0))]
                  + [ANY_SPEC] * n,
        out_shape=[jax.ShapeDtypeStruct((t, D_MODEL), MXU), jax.ShapeDtypeStruct((D_MODEL, t), MXU),
                   jax.ShapeDtypeStruct((len(PATTERNS), npair, 2 * BLK, 2 * BLK), F32)]
                  + [jax.ShapeDtypeStruct((N_DEV,) + a.shape, a.dtype) for a in shards],
        scratch_shapes=GATHER_SEMS(n),
        compiler_params=_params(("arbitrary",)),
    )(x, nw, *shards, rb, bk)
    return res[0], res[1], res[2], res[3:]


def _inproj(h_all, w, cw, wout_shard):
    t = h_all.shape[0]
    tm = 512
    nsteps = t // tm

    def body(h_ref, w_ref, cw_ref, wo_ref, pdn_ref, qkv_ref, z_ref, patt_ref, gate_ref, ba_ref,
             wo8_ref, halo_ref, send, recv, loc):
        begin, finish = _gather_plan([wo_ref], [wo8_ref], send, recv, loc)

        @pl.when(pl.program_id(0) == 0)
        def _():
            begin()
            halo_ref[...] = jnp.zeros_like(halo_ref)

        @pl.when(pl.program_id(0) == nsteps - 1)
        def _():
            finish()

        h = h_ref[...]
        for ref, lo, hi in ((z_ref, 1536, 2048), (patt_ref, 2048, 3584), (gate_ref, 3584, 4096), (ba_ref, 4096, 4224)):
            ref[...] = jnp.dot(h, w_ref[:, lo:hi], preferred_element_type=F32)
        pdn = jnp.dot(h, w_ref[:, 0:3 * D_DN], preferred_element_type=F32)
        pdn_ref[...] = pdn
        _dn_prep_tile(pdn, halo_ref, cw_ref, qkv_ref)

    row = lambda n: pl.BlockSpec((tm, n), lambda i: (i, 0))
    full = lambda a: pl.BlockSpec(a.shape, lambda i: (0,) * a.ndim)
    return pl.pallas_call(
        body, name="inproj", grid=(nsteps,),
        in_specs=[row(D_MODEL), full(w), full(cw), ANY_SPEC],
        out_specs=[row(1536), row(1536), row(512), row(1536), row(512), row(128), ANY_SPEC],
        out_shape=[jax.ShapeDtypeStruct((t, n), F32) for n in (1536, 1536, 512, 1536, 512, 128)] +
                  [jax.ShapeDtypeStruct((N_DEV,) + wout_shard.shape, wout_shard.dtype)],
        scratch_shapes=[pltpu.VMEM((8, 3 * D_DN), F32)] + GATHER_SEMS(1),
        compiler_params=_params(("arbitrary",)),
    )(h_all, w, cw, wout_shard)


CONV_ROWS = 512


def _conv_taps(u_ref, c, w_ref):
    r0 = c * CONV_ROWS
    if c == 0:
        ext = jnp.concatenate([jnp.zeros((8, 128), F32), u_ref[0:CONV_ROWS, :]], axis=0)
    else:
        ext = u_ref[r0 - 8:r0 + CONV_ROWS, :]
    taps = [ext[8:, :]] + [pltpu.roll(ext, k, 0)[8:, :] for k in (1, 2, 3)]
    y = taps[0] * w_ref[3:4, :]
    for k in (1, 2, 3):
        y = y + taps[k] * w_ref[3 - k:4 - k, :]
    return taps, y


def _dn_prep_tile(pdn, halo_ref, cw_ref, out_ref):
    rows = pdn.shape[0]
    ext = jnp.concatenate([halo_ref[...], pdn], axis=0)
    halo_ref[...] = pdn[rows - 8:, :]
    for j in range(3 * D_DN // 128):
        cols = slice(j * 128, (j + 1) * 128)
        e = ext[:, cols]
        y = e[8:, :] * cw_ref[3:4, cols]
        for k in (1, 2, 3):
            y = y + pltpu.roll(e, k, 0)[8:, :] * cw_ref[3 - k:4 - k, cols]
        a = _silu(y)
        if j < 2 * DN_HEADS:
            a = a * lax.rsqrt(jnp.sum(a * a, axis=1, keepdims=True) + EPS)
        if j < DN_HEADS:
            a = a * DK ** -0.5
        out_ref[:, cols] = a


def _chunk_common(qkv, ba, arow, dtb):
    c = CHUNK
    ri, ci = _iota((c, c), 0), _iota((c, c), 1)
    lane = _iota((c, 128), 1)
    g_all = jnp.where((lane >= DN_HEADS) & (lane < 2 * DN_HEADS), arow * _softplus(ba + dtb), 0.0)
    gc_all = _hdot((ri >= ci).astype(F32), g_all, "a")
    gc_t = gc_all.T
    beta_all = _sigmoid(ba)
    out = []
    for h in range(DN_HEADS):
        gc = _lane_col(gc_all, DN_HEADS + h)
        gcr = gc_t[DN_HEADS + h:DN_HEADS + h + 1, :]
        gl = gc[c - 1:c, :]
        out.append(dict(
            q=qkv[:, h * DK:(h + 1) * DK], k=qkv[:, D_DN + h * DK:D_DN + (h + 1) * DK],
            v=qkv[:, 2 * D_DN + h * DK:2 * D_DN + (h + 1) * DK],
            beta=_lane_col(beta_all, h), g=_lane_col(g_all, DN_HEADS + h),
            a_raw=_lane_col(ba, DN_HEADS + h), a_h=_lane_col(arow, DN_HEADS + h), dt_h=_lane_col(dtb, DN_HEADS + h),
            decay=jnp.exp(jnp.where(ri >= ci, gc - gcr, NEG)), eg=jnp.exp(gc), egl=jnp.exp(gl), etail=jnp.exp(gl - gc)))
    return out, ri, ci


SCAN_CHUNKS = 8


def _dn_scan_fwd(qkv, ba, z, arow, dtb, dnw):
    t = qkv.shape[0]
    n = t // CHUNK
    c = CHUNK
    cps = SCAN_CHUNKS
    hs = range(DN_HEADS)
    chains = [(j, h) for j in range(cps) for h in hs]

    def body(qkv_ref, ba_ref, z_ref, arow_ref, dtb_ref, dnw_ref, o_ref, y_ref, sh_ref, th_ref, s_ref):
        @pl.when(pl.program_id(0) == 0)
        def _():
            s_ref[...] = jnp.zeros_like(s_ref)

        ms = {}
        for j in range(cps):
            rows = slice(j * c, (j + 1) * c)
            mj, ri, ci = _chunk_common(qkv_ref[rows, :], ba_ref[rows, :], arow_ref[...], dtb_ref[...])
            for h in hs:
                ms[j, h] = mj[h]
        kb = {x: ms[x]["k"] * ms[x]["beta"] for x in chains}
        amat = {x: jnp.where(ri > ci, _dot_nt(kb[x], ms[x]["k"]) * ms[x]["decay"], 0.0) for x in chains}
        attn = {x: jnp.where(ri >= ci, _dot_nt(ms[x]["q"], ms[x]["k"]) * ms[x]["decay"], 0.0) for x in chains}
        tinv = _wy_inverses(amat, (ri == ci).astype(F32))
        uw = {x: _hdot(tinv[x], jnp.concatenate([ms[x]["v"] * ms[x]["beta"], kb[x] * ms[x]["eg"]], axis=1))
              for x in chains}
        u = {x: uw[x][:, :DK] for x in chains}
        w = {x: uw[x][:, DK:] for x in chains}
        q_dec = {x: ms[x]["q"] * ms[x]["eg"] for x in chains}
        k_tail = {x: ms[x]["k"] * ms[x]["etail"] for x in chains}
        s = [s_ref[h] for h in hs]
        for j in range(cps):
            rows = slice(j * c, (j + 1) * c)
            v_new = [u[j, h] - _dot(w[j, h], s[h]) for h in hs]
            o = [_dot(q_dec[j, h], s[h]) + _dot(attn[j, h], v_new[h]) for h in hs]
            for h in hs:
                sh_ref[j, h] = s[h]
                th_ref[j, h] = tinv[j, h]
            s = [s[h] * ms[j, h]["egl"] + _dot_tn(k_tail[j, h], v_new[h]) for h in hs]
            for h in hs:
                cols = slice(h * DK, (h + 1) * DK)
                o_ref[rows, cols] = o[h]
                rs = lax.rsqrt(jnp.mean(o[h] * o[h], axis=1, keepdims=True) + EPS)
                y_ref[rows, cols] = (o[h] * rs * dnw_ref[...] * _silu(z_ref[rows, cols])).astype(MXU)
        for h in hs:
            s_ref[h] = s[h]

    row = lambda w_: pl.BlockSpec((cps * c, w_), lambda i: (i, 0))
    one = pl.BlockSpec((1, 128), lambda i: (0, 0))
    return pl.pallas_call(
        body, name="dn_scan_fwd", grid=(n // cps,),
        in_specs=[row(1536), row(128), row(512), one, one, one],
        out_specs=[row(512), row(512), pl.BlockSpec((cps, DN_HEADS, DK, DK), lambda i: (i, 0, 0, 0)),
                   pl.BlockSpec((cps, DN_HEADS, c, c), lambda i: (i, 0, 0, 0))],
        out_shape=[jax.ShapeDtypeStruct((t, 512), F32), jax.ShapeDtypeStruct((t, 512), MXU),
                   jax.ShapeDtypeStruct((n, DN_HEADS, DK, DK), F32), jax.ShapeDtypeStruct((n, DN_HEADS, c, c), F32)],
        scratch_shapes=[pltpu.VMEM((DN_HEADS, DK, DK), F32)],
        compiler_params=_params(("arbitrary",)),
    )(qkv, ba, z, arow, dtb, dnw)


ATT_ROWS = 512


def _pair_rstd(xv, g2_ref):
    return lax.rsqrt(_hdot(xv * xv, g2_ref[...], "b") * (1.0 / HD) + EPS)


def _pair_norm(t, raw_refs, w_refs, out_refs, g2_ref):
    for c in range(t // ATT_ROWS):
        sl = slice(c * ATT_ROWS, (c + 1) * ATT_ROWS)
        for raw, w_ref, out in zip(raw_refs, w_refs, out_refs):
            xv = raw[sl, :]
            out[sl, :] = xv * _pair_rstd(xv, g2_ref) * w_ref[...]


BIAS_SPEC = pl.BlockSpec((len(PATTERNS), None, 2 * BLK, 2 * BLK), lambda i: (0, i, 0, 0))
PAIR_ROW_SPEC = pl.BlockSpec((1, 128), lambda i: (0, i))


def _stack_heads(xb, h0):
    return jnp.concatenate([jnp.where(h0, xb, 0.0), jnp.where(h0, 0.0, xb)], axis=0).astype(MXU)


def _block_rows(t, r, n):
    per_class = (t // r) // BLK
    res = n // per_class
    j = n % per_class
    start = res + BLK * r * j
    pstart = res + BLK * r * jnp.maximum(j - 1, 0)
    if r == 1:
        return pl.ds(pl.multiple_of(start, BLK), BLK), pl.ds(pl.multiple_of(pstart, BLK), BLK), j
    return pl.ds(start, BLK, stride=r), pl.ds(pstart, BLK, stride=r), j


def _att_fwd(qkv, gate, bias, wq, wk, g2):
    t = qkv.shape[0]
    rows = ATT_ROWS

    def body(bias_ref, qraw_ref, kraw_ref, v_ref, g_ref, wq_ref, wk_ref, g2_ref, o_ref, y_ref, lse_ref,
             o0_ref, o1_ref, o2_ref, l0_ref, l1_ref, l2_ref, q_ref, k_ref):
        h0 = _iota((BLK, 128), 1) < HD
        prev_cols = _iota((2 * BLK, 2 * BLK), 1) < BLK
        op_refs, lp_refs = (o0_ref, o1_ref, o2_ref), (l0_ref, l1_ref, l2_ref)
        _pair_norm(t, (qraw_ref, kraw_ref), (wq_ref, wk_ref), (q_ref, k_ref), g2_ref)

        for p, (_, r) in enumerate(PATTERNS):
            def blk(n, carry, p=p, r=r):
                cur, prev, j = _block_rows(t, r, n)
                q2 = _stack_heads(q_ref[cur, :], h0)
                k2 = jnp.concatenate([k_ref[prev, :], k_ref[cur, :]], axis=0).astype(MXU)
                v2 = jnp.concatenate([v_ref[prev, :], v_ref[cur, :]], axis=0).astype(MXU)
                s = _dot_nt(q2, k2) + bias_ref[p] + jnp.where(prev_cols & (j == 0), NEG, 0.0)
                m = jnp.max(s, axis=1, keepdims=True)
                e = jnp.exp(s - m)
                l = jnp.sum(e, axis=1, keepdims=True)
                pv = _dot(e, v2) / l
                lse = m + jnp.log(l)
                op_refs[p][cur, :] = jnp.where(h0, pv[:BLK], pv[BLK:])
                lp_refs[p][cur, :] = jnp.where(h0, lse[:BLK], lse[BLK:])
                return carry

            lax.fori_loop(0, t // BLK, blk, 0, unroll=16)

        for c in range(t // rows):
            sl = slice(c * rows, (c + 1) * rows)
            ls = [ref[sl, :] for ref in lp_refs]
            mx = jnp.maximum(jnp.maximum(ls[0], ls[1]), ls[2])
            ws = [jnp.exp(v_ - mx) for v_ in ls]
            den = ws[0] + ws[1] + ws[2]
            o = (ws[0] * o0_ref[sl, :] + ws[1] * o1_ref[sl, :] + ws[2] * o2_ref[sl, :]) / den
            o_ref[sl, :] = o
            y_ref[sl, :] = (o * _silu(g_ref[sl, :])).astype(MXU)
            lse_ref[sl, :] = mx + jnp.log(den)

    col = lambda off: pl.BlockSpec((t, 128), lambda i, off=off: (0, off + i))
    return pl.pallas_call(
        body, name="att_fwd", grid=(ATT_HEADS // 2,),
        in_specs=[BIAS_SPEC, col(0), col(4), col(8), col(0), PAIR_ROW_SPEC, PAIR_ROW_SPEC,
                  pl.BlockSpec((128, 128), lambda i: (0, 0))],
        out_specs=[col(0), col(0), col(0)],
        out_shape=[jax.ShapeDtypeStruct((t, 512), dt_) for dt_ in (F32, MXU, F32)],
        scratch_shapes=[pltpu.VMEM((t, 128), F32)] * 8,
        compiler_params=_params(("arbitrary",)),
    )(bias, qkv, qkv, qkv, gate, wq, wk, g2)


def _outproj_both(x, ydn, yatt, wout, target, wout_t, oraw, z, dnw, oatt, gate, g, gt):
    t = x.shape[0]
    tm = 512

    def body(x_ref, a_ref, b_ref, wf_ref, t_ref, w_ref, o_ref, z_ref, dnw_ref, oa_ref, g_ref, grp_ref, grpt_ref,
             dy_ref, dw_ref, loss_ref, do_ref, dz_ref, doa_ref, dg_ref, dd_ref, ddnw_ref, acc_ref):
        @pl.when(pl.program_id(0) == 0)
        def _():
            loss_ref[...] = jnp.zeros_like(loss_ref)
            ddnw_ref[...] = jnp.zeros_like(ddnw_ref)
            acc_ref[...] = jnp.zeros_like(acc_ref)

        mix = jnp.concatenate([a_ref[...], b_ref[...]], axis=1)
        err = x_ref[...] + jnp.dot(mix, wf_ref[...], preferred_element_type=F32) - t_ref[...]
        dy = err * (1.0 / D_MODEL)
        dy_ref[...] = dy
        loss_ref[...] += jnp.sum(err * err) * (0.5 / D_MODEL)
        acc_ref[...] += jnp.dot(mix.astype(F32).T.astype(MXU), dy.astype(MXU), preferred_element_type=F32)

        @pl.when(pl.program_id(0) == t // tm - 1)
        def _():
            dw_ref[...] = acc_ref[...].astype(GRAD_WIRE)

        dmix = jnp.dot(dy.astype(MXU), w_ref[...], preferred_element_type=F32)
        dnw_v = dnw_ref[...]
        acc = jnp.zeros((1, DK), F32)
        for h in range(DN_HEADS):
            sl = slice(h * DK, (h + 1) * DK)
            o, zz, dm = o_ref[:, sl], z_ref[:, sl], dmix[:, sl]
            rs = lax.rsqrt(jnp.mean(o * o, axis=1, keepdims=True) + EPS)
            oh = o * rs
            silu_z, dsilu_z = _silu_and_grad(zz)
            dz_ref[:, sl] = (dm * oh * dnw_v * dsilu_z).astype(MXU)
            d_on = dm * silu_z
            gg = d_on * dnw_v
            do_ref[:, sl] = rs * (gg - oh * jnp.mean(gg * oh, axis=1, keepdims=True))
            acc = acc + jnp.sum(d_on * oh, axis=0, keepdims=True)
        ddnw_ref[...] += jnp.broadcast_to(acc, (8, DK))
        da, gate_v, oa = dmix[:, 512:], g_ref[...], oa_ref[...]
        silu_g, dsilu_g = _silu_and_grad(gate_v)
        doa = da * silu_g
        doa_ref[...] = doa
        dg_ref[...] = (da * oa * dsilu_g).astype(MXU)
        dd_ref[...] = _hdot(_hdot(doa * oa, grp_ref[...], "b"), grpt_ref[...], "b")

    row = lambda n: pl.BlockSpec((tm, n), lambda i: (i, 0))
    full = lambda a: pl.BlockSpec(a.shape, lambda i: (0,) * a.ndim)
    return pl.pallas_call(
        body, name="outproj_both", grid=(t // tm,),
        in_specs=[row(D_MODEL), row(512), row(512), full(wout), row(D_MODEL), full(wout_t), row(512), row(512),
                  full(dnw), row(512), row(512), full(g), full(gt)],
        out_specs=[row(D_MODEL), pl.BlockSpec((D_MODEL, D_MODEL), lambda i: (0, 0)),
                   pl.BlockSpec((8, 128), lambda i: (0, 0))]
                  + [row(512)] * 5 + [pl.BlockSpec((8, DK), lambda i: (0, 0))],
        out_shape=[jax.ShapeDtypeStruct((t, D_MODEL), F32), jax.ShapeDtypeStruct((D_MODEL, D_MODEL), GRAD_WIRE),
                   jax.ShapeDtypeStruct((8, 128), F32)]
                  + [jax.ShapeDtypeStruct((t, 512), dt_) for dt_ in (F32, MXU, F32, MXU, F32)]
                  + [jax.ShapeDtypeStruct((8, DK), F32)],
        scratch_shapes=[pltpu.VMEM((D_MODEL, D_MODEL), F32)],
        compiler_params=_params(("arbitrary",)),
    )(x, ydn, yatt, wout, target, wout_t, oraw, z, dnw, oatt, gate, g, gt)


def _grad_matmul(at, b, name):
    m, t = at.shape
    n = b.shape[1]
    tk = 1024
    tn = n if n <= 1536 else 512
    nk = t // tk

    def body(a_ref, b_ref, o_ref, acc_ref):
        k = pl.program_id(1)

        @pl.when(k == 0)
        def _():
            acc_ref[...] = jnp.zeros_like(acc_ref)

        acc_ref[...] += jnp.dot(a_ref[...], b_ref[...].astype(MXU), preferred_element_type=F32)

        @pl.when(k == nk - 1)
        def _():
            o_ref[...] = acc_ref[...].astype(GRAD_WIRE)

    return pl.pallas_call(
        body, name=name, grid=(n // tn, nk),
        in_specs=[pl.BlockSpec((m, tk), lambda j, k: (0, k)), pl.BlockSpec((tk, tn), lambda j, k: (k, j))],
        out_specs=pl.BlockSpec((m, tn), lambda j, k: (0, j)),
        out_shape=jax.ShapeDtypeStruct((m, n), GRAD_WIRE),
        scratch_shapes=[pltpu.VMEM((m, tn), F32)],
        compiler_params=_params(("arbitrary", "arbitrary")),
    )(at, b)


def _grad_matmul_many(at, bs, name):
    m, t = at.shape
    n = bs[0].shape[1]
    nb = len(bs)
    tk = 1024
    nk = t // tk

    def body(a_ref, *refs):
        b_refs, o_refs, acc_ref = refs[:nb], refs[nb:2 * nb], refs[2 * nb]
        s, k = pl.program_id(0), pl.program_id(1)

        @pl.when(k == 0)
        def _():
            acc_ref[...] = jnp.zeros_like(acc_ref)

        for i in range(nb):
            @pl.when(s == i)
            def _(i=i):
                acc_ref[...] += jnp.dot(a_ref[...], b_refs[i][...].astype(MXU), preferred_element_type=F32)

                @pl.when(k == nk - 1)
                def _():
                    o_refs[i][...] = acc_ref[...].astype(GRAD_WIRE)

    def b_spec(i):
        return pl.BlockSpec((tk, n), lambda s, k: (jnp.where(s == i, k, jnp.where(s < i, 0, nk - 1)), 0))

    return pl.pallas_call(
        body, name=name, grid=(nb, nk),
        in_specs=[pl.BlockSpec((m, tk), lambda s, k: (0, k))] + [b_spec(i) for i in range(nb)],
        out_specs=[pl.BlockSpec((m, n), lambda s, k: (0, 0))] * nb,
        out_shape=[jax.ShapeDtypeStruct((m, n), GRAD_WIRE)] * nb,
        scratch_shapes=[pltpu.VMEM((m, n), F32)],
        compiler_params=_params(("arbitrary", "arbitrary")),
    )(at, *bs)


def _att_bwd(qkv, do, lse, dd, bias, bk, wq, wk, g2):
    t = qkv.shape[0]
    rows = ATT_ROWS

    def body(bias_ref, bk_ref, qraw_ref, kraw_ref, v_ref, do_ref, lse_ref, dd_ref, wq_ref, wk_ref, g2_ref,
             dq_out, dk_out, dv_out, db_ref, dwq_ref, dwk_ref, ds_ref, q_ref, k_ref, dq_ref, dk_ref, dv_ref):
        pair = pl.program_id(0)

        @pl.when(pair == 0)
        def _():
            db_ref[...] = jnp.zeros_like(db_ref)

        _pair_norm(t, (qraw_ref, kraw_ref), (wq_ref, wk_ref), (q_ref, k_ref), g2_ref)
        ds_ref[...] = jnp.zeros_like(ds_ref)
        for c in range(t // rows):
            sl = slice(c * rows, (c + 1) * rows)
            for ref in (dq_ref, dk_ref, dv_ref):
                ref[sl, :] = jnp.zeros((rows, 128), F32)
        h0 = _iota((BLK, 128), 1) < HD
        prev_cols = _iota((2 * BLK, 2 * BLK), 1) < BLK

        def rows_of(xb):
            return jnp.concatenate([xb[:, 0:1], xb[:, HD:HD + 1]], axis=0)

        for p, (_, r) in enumerate(PATTERNS):
            def blk(n, carry, p=p, r=r):
                cur, prev, j = _block_rows(t, r, n)
                q2, do2 = _stack_heads(q_ref[cur, :], h0), _stack_heads(do_ref[cur, :], h0)
                k2 = jnp.concatenate([k_ref[prev, :], k_ref[cur, :]], axis=0).astype(MXU)
                v2 = jnp.concatenate([v_ref[prev, :], v_ref[cur, :]], axis=0).astype(MXU)
                s = _dot_nt(q2, k2) + bias_ref[p] + jnp.where(prev_cols & (j == 0), NEG, 0.0)
                prob = jnp.exp(s - rows_of(lse_ref[cur, :]))
                ds = prob * (_dot_nt(do2, v2) - rows_of(dd_ref[cur, :]))
                ds_ref[p] += ds
                dq2 = _dot(ds, k2)
                dk2 = _dot_tn(ds, q2)
                dv2 = _dot_tn(prob, do2)
                dq_ref[cur, :] += jnp.where(h0, dq2[:BLK], dq2[BLK:])
                dk_ref[prev, :] += dk2[:BLK]
                dv_ref[prev, :] += dv2[:BLK]
                dk_ref[cur, :] += dk2[BLK:]
                dv_ref[cur, :] += dv2[BLK:]
                return carry

            lax.fori_loop(0, t // BLK, blk, 0, unroll=8)

        ri, ci = _iota((8, 128), 0), _iota((8, 128), 1)
        upd = jnp.zeros((8, 128), F32)
        for p in range(len(PATTERNS)):
            bk = bk_ref[p]
            for hh in range(2):
                dsum = ds_ref[p, hh * BLK:(hh + 1) * BLK, :]
                for b in range(N_BUCKETS):
                    val = jnp.sum(jnp.where(bk == b, dsum, 0.0))
                    upd = upd + jnp.where((ri == 2 * pair + hh) & (ci == b), val, 0.0)
        db_ref[...] += upd

        for raw, d_ref, out, w_ref, dw_ref in ((qraw_ref, dq_ref, dq_out, wq_ref, dwq_ref),
                                               (kraw_ref, dk_ref, dk_out, wk_ref, dwk_ref)):
            acc = jnp.zeros((1, 128), F32)
            for c in range(t // rows):
                sl = slice(c * rows, (c + 1) * rows)
                xv, dyv = raw[sl, :], d_ref[sl, :]
                rs = _pair_rstd(xv, g2_ref)
                xh = xv * rs
                gg = dyv * w_ref[...]
                mean = _hdot(gg * xh, g2_ref[...], "b") * (1.0 / HD)
                out[sl, :] = (rs * (gg - xh * mean)).astype(MXU)
                acc = acc + jnp.sum(dyv * xh, axis=0, keepdims=True)
            dw_ref[...] = jnp.broadcast_to(acc, (8, 128))
        for c in range(t // rows):
            sl = slice(c * rows, (c + 1) * rows)
            dv_out[sl, :] = dv_ref[sl, :].astype(MXU)

    col = lambda off: pl.BlockSpec((t, 128), lambda i, off=off: (0, off + i))
    acc8 = pl.BlockSpec((8, 128), lambda i: (0, i))
    return pl.pallas_call(
        body, name="att_bwd", grid=(ATT_HEADS // 2,),
        in_specs=[BIAS_SPEC, pl.BlockSpec(bk.shape, lambda i: (0, 0, 0)),
                  col(0), col(4), col(8), col(0), col(0), col(0), PAIR_ROW_SPEC, PAIR_ROW_SPEC,
                  pl.BlockSpec((128, 128), lambda i: (0, 0))],
        out_specs=[col(0), col(0), col(0), pl.BlockSpec((8, 128), lambda i: (0, 0)), acc8, acc8],
        out_shape=[jax.ShapeDtypeStruct((t, 512), MXU)] * 3 + [jax.ShapeDtypeStruct((8, 128), F32)]
                  + [jax.ShapeDtypeStruct((8, 512), F32)] * 2,
        scratch_shapes=[pltpu.VMEM((len(PATTERNS), 2 * BLK, 2 * BLK), F32)] + [pltpu.VMEM((t, 128), F32)] * 5,
        compiler_params=_params(("arbitrary",)),
    )(bias, bk, qkv, qkv, qkv, do, lse, dd, wq, wk, g2)


def _dn_scan_bwd(qkv, ba, do, sh, th, arow, dtb):
    t = qkv.shape[0]
    n = t // CHUNK
    c = CHUNK
    cps = SCAN_CHUNKS

    def body(qkv_ref, ba_ref, do_ref, sh_ref, th_ref, arow_ref, dtb_ref, dqkv_ref, dba_ref, ds_ref):
        @pl.when(pl.program_id(0) == 0)
        def _():
            ds_ref[...] = jnp.zeros_like(ds_ref)

        hs = range(DN_HEADS)
        chains = [(j, h) for j in range(cps) for h in hs]
        lane = _iota((c, 128), 1)
        row = _iota((c, 1), 0)
        ms = {}
        for j in range(cps):
            rows_j = slice(j * c, (j + 1) * c)
            mj, ri, ci = _chunk_common(qkv_ref[rows_j, :], ba_ref[rows_j, :], arow_ref[...], dtb_ref[...])
            for h in hs:
                ms[j, h] = mj[h]
        q, k, v = ({x: ms[x][nm] for x in chains} for nm in ("q", "k", "v"))
        beta, decay = ({x: ms[x][nm] for x in chains} for nm in ("beta", "decay"))
        eg, egl, etail = ({x: ms[x][nm] for x in chains} for nm in ("eg", "egl", "etail"))
        s = {x: sh_ref[x[0], x[1]] for x in chains}
        tinv = {x: th_ref[x[0], x[1]] for x in chains}
        d_o = {(j, h): do_ref[j * c:(j + 1) * c, h * DK:(h + 1) * DK] for j, h in chains}
        kb = {x: k[x] * beta[x] for x in chains}
        vb = {x: v[x] * beta[x] for x in chains}
        kbg = {x: kb[x] * eg[x] for x in chains}
        amat = {x: jnp.where(ri > ci, _dot_nt(kb[x], k[x]) * decay[x], 0.0) for x in chains}
        attn = {x: jnp.where(ri >= ci, _dot_nt(q[x], k[x]) * decay[x], 0.0) for x in chains}
        uw = {x: _hdot(tinv[x], jnp.concatenate([vb[x], kbg[x]], axis=1)) for x in chains}
        u = {x: uw[x][:, :DK] for x in chains}
        w = {x: uw[x][:, DK:] for x in chains}
        v_new = {x: u[x] - _dot(w[x], s[x]) for x in chains}
        q_dec = {x: q[x] * eg[x] for x in chains}
        k_tail = {x: k[x] * etail[x] for x in chains}
        d_attn = {x: jnp.where(ri >= ci, _dot_nt(d_o[x], v_new[x]), 0.0) for x in chains}
        d_qdec = {x: _dot_nt(d_o[x], s[x]) for x in chains}
        from_o = {x: _dot_tn(attn[x], d_o[x]) for x in chains}
        to_state = {x: _dot_tn(q_dec[x], d_o[x]) for x in chains}

        d_s, d_vnew = {}, {}
        cur = [ds_ref[h] for h in hs]
        for j in reversed(range(cps)):
            for h in hs:
                d_s[j, h] = cur[h]
                d_vnew[j, h] = from_o[j, h] + _dot(k_tail[j, h], cur[h])
            cur = [to_state[j, h] + cur[h] * egl[j, h] - _dot_tn(w[j, h], d_vnew[j, h]) for h in hs]
        for h in hs:
            ds_ref[h] = cur[h]

        d_ktail = {x: _dot_nt(v_new[x], d_s[x]) for x in chains}
        d_gl = {x: jnp.sum(s[x] * d_s[x]) * egl[x] for x in chains}
        d_w = {x: -_dot_nt(d_vnew[x], s[x]) for x in chains}
        d_both = {x: _hdot_tn(tinv[x], jnp.concatenate([d_vnew[x], d_w[x]], axis=1)) for x in chains}
        d_vb = {x: d_both[x][:, :DK] for x in chains}
        d_kbg = {x: d_both[x][:, DK:] for x in chains}
        d_a = {x: -jnp.where(ri > ci, _hdot_nt(d_both[x], uw[x]), 0.0) for x in chains}
        d_qk = {x: d_attn[x] * decay[x] for x in chains}
        d_kk = {x: d_a[x] * decay[x] for x in chains}
        d_kb = {x: _dot(d_kk[x], k[x]) + d_kbg[x] * eg[x] for x in chains}
        d_q = {x: _dot(d_qk[x], k[x]) + d_qdec[x] * eg[x] for x in chains}
        d_k = {x: _dot_tn(d_qk[x], q[x]) + _dot_tn(d_kk[x], kb[x]) + d_ktail[x] * etail[x] + d_kb[x] * beta[x]
               for x in chains}
        d_beta = {x: jnp.sum(d_kb[x] * k[x] + d_vb[x] * v[x], axis=1, keepdims=True) for x in chains}
        mm = {x: d_a[x] * amat[x] + d_attn[x] * attn[x] for x in chains}
        for j in range(cps):
            rows_j = slice(j * c, (j + 1) * c)
            rows = jnp.zeros((c, c), F32)
            for h in hs:
                rows = rows + jnp.where(ri == h, jnp.sum(mm[j, h], axis=0, keepdims=True), 0.0)
            cols_t = jnp.concatenate([rows, jnp.zeros((c, c), F32)], axis=1).T[:c, :]
            d_gc_all = jnp.zeros((c, 128), F32)
            for h in hs:
                x = (j, h)
                tail_term = jnp.sum(d_ktail[x] * k_tail[x], axis=1, keepdims=True)
                d_gc = (jnp.sum(mm[x], axis=1, keepdims=True) - _lane_col(cols_t, h)
                        + jnp.sum(d_qdec[x] * q_dec[x] + d_kbg[x] * kbg[x], axis=1, keepdims=True) - tail_term)
                d_gc = d_gc + jnp.where(row == c - 1, jnp.sum(tail_term) + d_gl[x], 0.0)
                d_gc_all = d_gc_all + jnp.where(lane == DN_HEADS + h, d_gc, 0.0)
            d_g_all = _hdot((ri <= ci).astype(F32), d_gc_all, "a")
            dba = jnp.zeros((c, 128), F32)
            for h in hs:
                x = (j, h)
                d_g = _lane_col(d_g_all, DN_HEADS + h)
                d_braw = d_beta[x] * beta[x] * (1.0 - beta[x])
                d_araw = d_g * ms[x]["a_h"] * _sigmoid(ms[x]["a_raw"] + ms[x]["dt_h"])
                dba = dba + jnp.where(lane == h, d_braw, 0.0) + jnp.where(lane == DN_HEADS + h, d_araw, 0.0) \
                    + jnp.where(lane == 2 * DN_HEADS + h, d_g * ms[x]["g"], 0.0)
                dqkv_ref[rows_j, h * DK:(h + 1) * DK] = d_q[x]
                dqkv_ref[rows_j, D_DN + h * DK:D_DN + (h + 1) * DK] = d_k[x]
                dqkv_ref[rows_j, 2 * D_DN + h * DK:2 * D_DN + (h + 1) * DK] = d_vb[x] * beta[x]
            dba_ref[rows_j, :] = dba

    nsteps = n // cps
    rev = lambda w_: pl.BlockSpec((cps * c, w_), lambda i: (nsteps - 1 - i, 0))
    one = pl.BlockSpec((1, 128), lambda i: (0, 0))
    return pl.pallas_call(
        body, name="dn_scan_bwd", grid=(nsteps,),
        in_specs=[rev(1536), rev(128), rev(512),
                  pl.BlockSpec((cps, DN_HEADS, DK, DK), lambda i: (nsteps - 1 - i, 0, 0, 0)),
                  pl.BlockSpec((cps, DN_HEADS, c, c), lambda i: (nsteps - 1 - i, 0, 0, 0)), one, one],
        out_specs=[rev(1536), rev(128)],
        out_shape=[jax.ShapeDtypeStruct((t, 1536), F32), jax.ShapeDtypeStruct((t, 128), F32)],
        scratch_shapes=[pltpu.VMEM((DN_HEADS, DK, DK), F32)],
        compiler_params=_params(("arbitrary",)),
    )(qkv, ba, do, sh, th, arow, dtb)


def _dn_prep_bwd(pdn, cw, dact):
    t = pdn.shape[0]
    nchunk = t // CONV_ROWS

    def body(u_ref, w_ref, d_ref, du_ref, dw_ref, dy_ref):
        j = pl.program_id(0)
        dy_ref[t:t + 8, :] = jnp.zeros((8, 128), F32)
        dw = [jnp.zeros((1, 128), F32) for _ in range(4)]
        for c in range(nchunk):
            sl = slice(c * CONV_ROWS, (c + 1) * CONV_ROWS)
            taps, y = _conv_taps(u_ref, c, w_ref)
            a, da_dy = _silu_and_grad(y)
            dout = d_ref[sl, :]
            rs = lax.rsqrt(jnp.sum(a * a, axis=1, keepdims=True) + EPS)
            f = jnp.where(j < 8, rs, 1.0) * jnp.where(j < 4, DK ** -0.5, 1.0)
            corr = jnp.where(j < 8, f * rs * rs * jnp.sum(dout * a, axis=1, keepdims=True), 0.0)
            dy = (f * dout - corr * a) * da_dy
            dy_ref[sl, :] = dy
            for k_ in range(4):
                dw[3 - k_] = dw[3 - k_] + jnp.sum(taps[k_] * dy, axis=0, keepdims=True)
        for i in range(4):
            dw_ref[i:i + 1, :] = dw[i]
        for c in range(nchunk):
            r0 = c * CONV_ROWS
            ext = dy_ref[r0:r0 + CONV_ROWS + 8, :]
            du = ext[:CONV_ROWS, :] * w_ref[3:4, :]
            for k_ in (1, 2, 3):
                du = du + pltpu.roll(ext, CONV_ROWS + 8 - k_, 0)[:CONV_ROWS, :] * w_ref[3 - k_:4 - k_, :]
            du_ref[r0:r0 + CONV_ROWS, :] = du.astype(MXU)

    return pl.pallas_call(
        body, name="dn_prep_bwd", grid=(12,),
        in_specs=[pl.BlockSpec((t, 128), lambda j: (0, j)), pl.BlockSpec((4, 128), lambda j: (0, j)),
                  pl.BlockSpec((t, 128), lambda j: (0, j))],
        out_specs=[pl.BlockSpec((t, 128), lambda j: (0, j)), pl.BlockSpec((4, 128), lambda j: (0, j))],
        out_shape=[jax.ShapeDtypeStruct((t, 1536), MXU), jax.ShapeDtypeStruct((4, 1536), F32)],
        scratch_shapes=[pltpu.VMEM((t + 8, 128), F32)],
        compiler_params=_params(("arbitrary",)),
    )(pdn, cw, dact)


SECTIONS = (("dn", 0, 1536), ("z", 1536, 512), ("q", 2048, 512), ("k", 2560, 512), ("v", 3072, 512),
            ("gate", 3584, 512), ("ba", 4096, 128))


def _inproj_bwd(x, nw, wt, dy, dsecs, partials):
    t = x.shape[0]
    tm = 256
    npart = len(partials)
    nsteps = t // tm

    nsec = len(SECTIONS)

    def body(x_ref, nw_ref, w_ref, dy_ref, *rest):
        sec_refs, rest = rest[:nsec], rest[nsec:]
        part_refs, (gx_ref, dnw_ref, cs_ref) = rest[:npart], rest[npart:npart + 3]
        got_refs, (send, recv, loc) = rest[npart + 3:2 * npart + 3], rest[2 * npart + 3:]
        starts, waits = _chip_swap_copies(part_refs, got_refs, send, recv, loc)

        @pl.when(pl.program_id(0) == 0)
        def _():
            for start in starts:
                start()
            dnw_ref[...] = jnp.zeros_like(dnw_ref)
            cs_ref[...] = jnp.zeros_like(cs_ref)

        @pl.when(pl.program_id(0) == nsteps - 1)
        def _():
            for wait in waits:
                wait()

        dh = jnp.zeros((tm, D_MODEL), F32)
        for ref, (_, lo, width) in zip(sec_refs, SECTIONS):
            dh = dh + jnp.dot(ref[...].astype(MXU), w_ref[lo:lo + width, :], preferred_element_type=F32)
        xv = x_ref[...]
        rstd = lax.rsqrt(jnp.mean(xv * xv, axis=-1, keepdims=True) + EPS)
        xh = xv * rstd
        gg = dh * nw_ref[...]
        gx_ref[...] = rstd * (gg - xh * jnp.mean(gg * xh, axis=-1, keepdims=True)) + dy_ref[...]
        dnw_ref[...] += jnp.broadcast_to(jnp.sum(dh * xh, axis=0, keepdims=True), (8, D_MODEL))
        cs_ref[...] += jnp.broadcast_to(jnp.sum(sec_refs[nsec - 1][...], axis=0, keepdims=True), (8, 128))

    row = lambda n: pl.BlockSpec((tm, n), lambda i: (i, 0))
    full = lambda a: pl.BlockSpec(a.shape, lambda i: (0,) * a.ndim)
    res = pl.pallas_call(
        body, name="inproj_bwd", grid=(nsteps,),
        in_specs=[row(D_MODEL), full(nw), full(wt), row(D_MODEL)] + [row(width) for _, _, width in SECTIONS]
                 + [ANY_SPEC] * npart,
        out_specs=[row(D_MODEL), pl.BlockSpec((8, D_MODEL), lambda i: (0, 0)), pl.BlockSpec((8, 128), lambda i: (0, 0))]
                  + [ANY_SPEC] * npart,
        out_shape=[jax.ShapeDtypeStruct((t, D_MODEL), F32), jax.ShapeDtypeStruct((8, D_MODEL), F32),
                   jax.ShapeDtypeStruct((8, 128), F32)] + [jax.ShapeDtypeStruct(p.shape, p.dtype) for p in partials],
        scratch_shapes=[pltpu.SemaphoreType.DMA((npart, 3)), pltpu.SemaphoreType.DMA((npart, 3)),
                        pltpu.SemaphoreType.DMA((npart,))],
        compiler_params=_params(("arbitrary",)),
    )(x, nw, wt, dy, *dsecs, *partials)
    return res[0], res[1], res[2], res[3:]


def _adamw_sum(w, gs, m, v, name):
    r, c = w.shape
    nsum = gs.shape[0]
    tr = r if r <= 512 else 512
    c1 = 1.0 - ADAM_B1 ** ADAM_STEP
    c2 = 1.0 - ADAM_B2 ** ADAM_STEP

    def body(w_ref, g_ref, m_ref, v_ref, go_ref, d_ref, mo_ref, vo_ref):
        g = g_ref[0].astype(F32)
        for s in range(1, nsum):
            g = g + g_ref[s].astype(F32)
        mn = ADAM_B1 * m_ref[...] + (1.0 - ADAM_B1) * g
        vn = ADAM_B2 * v_ref[...] + (1.0 - ADAM_B2) * (g * g)
        go_ref[...] = g
        mo_ref[...] = mn
        vo_ref[...] = vn
        d_ref[...] = -ADAM_LR * ((mn / c1) / (jnp.sqrt(vn / c2) + ADAM_EPS) + ADAM_WD * w_ref[...])

    blk = pl.BlockSpec((tr, c), lambda i: (i, 0))
    return pl.pallas_call(
        body, name=name, grid=(r // tr,),
        in_specs=[blk, pl.BlockSpec((nsum, tr, c), lambda i: (0, i, 0)), blk, blk],
        out_specs=[blk] * 4, out_shape=[jax.ShapeDtypeStruct((r, c), F32)] * 4,
        compiler_params=_params(("arbitrary",)),
    )(w, gs, m, v)


def _local_step(x, target, h, ht, bias, bk, w_sect, conv_w, a_log, dt_bias, dn_norm_w, q_norm_w, k_norm_w, wout_shard):
    arow = jnp.zeros((1, 128), F32).at[0, DN_HEADS:2 * DN_HEADS].set(-jnp.exp(a_log[0]))
    dtb = jnp.zeros((1, 128), F32).at[0, DN_HEADS:2 * DN_HEADS].set(dt_bias[0])
    g_np, gt_np = _group_mats()
    g, gt = jnp.asarray(g_np), jnp.asarray(gt_np)
    g2 = jnp.asarray(np.kron(np.eye(2, dtype=np.float32), np.ones((HD, HD), np.float32)))
    wq = jnp.tile(q_norm_w, (1, ATT_HEADS)) * (HD ** -0.5)
    wk = jnp.tile(k_norm_w, (1, ATT_HEADS))

    pdn, qkv_dn, z, patt, gate, ba, wout8 = _inproj(h, w_sect, conv_w, wout_shard)
    w_out = wout8.reshape(D_MODEL, D_MODEL)
    oraw, ydn, sh, th = _dn_scan_fwd(qkv_dn, ba, z, arow, dtb, dn_norm_w)
    oatt, yatt, lse = _att_fwd(patt, gate, bias, wq, wk, g2)
    dy, d_wout, loss8, do_dn, dz, do_att, dgate, dd, ddnw = _outproj_both(
        x, ydn, yatt, w_out, target, w_out.T, oraw, z, dn_norm_w, oatt, gate, g, gt)
    dq, dk, dv, drb, dwq8, dwk8 = _att_bwd(patt, do_att, lse, dd, bias, bk, wq, wk, g2)
    dqkv_dn, dba = _dn_scan_bwd(qkv_dn, ba, do_dn, sh, th, arow, dtb)
    dpdn, d_conv = _dn_prep_bwd(pdn, conv_w, dqkv_dn)
    dsecs = (dpdn, dz, dq, dk, dv, dgate, dba)
    dw_mid = _grad_matmul_many(ht, dsecs[1:6], "dw_in_mid")
    dw_sections = [_grad_matmul(ht, dpdn, "dw_in_dn"), *dw_mid, _grad_matmul(ht, dba, "dw_in_ba")]
    return dict(w_in_sections=dw_sections, conv_w=d_conv, w_out=d_wout, dy=dy, dsecs=dsecs,
                small_parts=(loss8, ddnw, dwq8, dwk8, drb))


def _finish_step(x, norm_w, w_sect_t, gr, partials):
    grad_x, dnw8, cs8, got = _inproj_bwd(x, norm_w, w_sect_t, gr["dy"], gr["dsecs"], partials)
    return grad_x, _pack_small_grads(dnw8, cs8, *gr["small_parts"]), got


SMALL_ROWS = 24
SMALL_AT = dict(a_log=(slice(8, 9), slice(0, 4)), dt_bias=(slice(9, 10), slice(0, 4)),
                dn_norm_w=(slice(10, 11), slice(0, 128)), q_norm_w=(slice(11, 12), slice(0, HD)),
                k_norm_w=(slice(12, 13), slice(0, HD)), rel_bias=(slice(16, 24), slice(0, N_BUCKETS)))
SMALL_NAMES = ("norm_w", "a_log", "dt_bias", "dn_norm_w", "q_norm_w", "k_norm_w", "rel_bias")


LOSS_ROW = 13


def _pack_small_grads(dnw8, cs8, loss8, ddnw8, dwq8, dwk8, drb):
    def body(dnw_ref, cs_ref, loss_ref, ddnw_ref, dwq_ref, dwk_ref, drb_ref, o_ref):
        lane = _iota((8, 128), 1)
        o_ref[...] = jnp.zeros_like(o_ref)
        o_ref[LOSS_ROW:LOSS_ROW + 1, :] = jnp.where(lane == 0, loss_ref[...], 0.0)[0:1, :]
        for k in range(D_MODEL // 128):
            o_ref[k:k + 1, :] = dnw_ref[0:1, k * 128:(k + 1) * 128]
        cs = cs_ref[...]
        o_ref[8:9, :] = jnp.where(lane < DN_HEADS, pltpu.roll(cs, 128 - 2 * DN_HEADS, 1), 0.0)[0:1, :]
        o_ref[9:10, :] = jnp.where(lane < DN_HEADS, pltpu.roll(cs, 128 - DN_HEADS, 1), 0.0)[0:1, :]
        o_ref[10:11, :] = ddnw_ref[0:1, :]
        for row, ref, scale in ((11, dwq_ref, HD ** -0.5), (12, dwk_ref, 1.0)):
            acc = ref[:, 0:128] + ref[:, 128:256] + ref[:, 256:384] + ref[:, 384:512]
            acc = (acc + pltpu.roll(acc, HD, 1)) * scale
            o_ref[row:row + 1, :] = jnp.where(lane < HD, acc, 0.0)[0:1, :]
        o_ref[16:24, :] = drb_ref[...]

    return pl.pallas_call(body, name="pack_small_grads", out_shape=jax.ShapeDtypeStruct((SMALL_ROWS, 128), F32),
                          )(dnw8, cs8, loss8, ddnw8, dwq8, dwk8, drb)


def _adam_math(w, g, m, v):
    c1 = 1.0 - ADAM_B1 ** ADAM_STEP
    c2 = 1.0 - ADAM_B2 ** ADAM_STEP
    mn = ADAM_B1 * m + (1.0 - ADAM_B1) * g
    vn = ADAM_B2 * v + (1.0 - ADAM_B2) * (g * g)
    return -ADAM_LR * ((mn / c1) / (jnp.sqrt(vn / c2) + ADAM_EPS) + ADAM_WD * w), mn, vn


def _adamw_small(gs, ws, ms, vs):
    n = len(SMALL_NAMES)

    def body(g_ref, *refs):
        w_refs, m_refs, v_refs = refs[:n], refs[n:2 * n], refs[2 * n:3 * n]
        outs, loss_ref = refs[3 * n:7 * n], refs[7 * n]
        loss = g_ref[0, LOSS_ROW:LOSS_ROW + 1, :]
        for s in range(1, gs.shape[0]):
            loss = loss + g_ref[s, LOSS_ROW:LOSS_ROW + 1, :]
        loss_ref[...] = loss

        def one(i, rows, lanes, at):
            g = g_ref[0, rows, lanes]
            for s in range(1, gs.shape[0]):
                g = g + g_ref[s, rows, lanes]
            d, mn, vn = _adam_math(w_refs[i][at], g, m_refs[i][at], v_refs[i][at])
            for kind, val in enumerate((g, d, mn, vn)):
                outs[kind * n + i][at] = val

        for k in range(D_MODEL // 128):
            one(0, slice(k, k + 1), slice(0, 128), (slice(0, 1), slice(k * 128, (k + 1) * 128)))
        for i, nm in enumerate(SMALL_NAMES[1:], start=1):
            rows, lanes = SMALL_AT[nm]
            one(i, rows, lanes, (slice(None), slice(None)))

    shapes = [jax.ShapeDtypeStruct(w.shape, F32) for w in ws]
    res = pl.pallas_call(body, name="adamw_small",
                         out_shape=shapes * 4 + [jax.ShapeDtypeStruct((1, 128), F32)])(gs, *ws, *ms, *vs)
    return [res[k * n:(k + 1) * n] for k in range(4)], res[4 * n]


def kernel(x, norm_w, w_in, conv_w, a_log, dt_bias, dn_norm_w, q_norm_w, k_norm_w, rel_bias, w_out, loss_target, m_norm_w, m_w_in, m_conv_w, m_a_log, m_dt_bias, m_dn_norm_w, m_q_norm_w, m_k_norm_w, m_rel_bias, m_w_out, v_norm_w, v_w_in, v_conv_w, v_a_log, v_dt_bias, v_dn_norm_w, v_q_norm_w, v_k_norm_w, v_rel_bias, v_w_out):
    assert w_in.shape[2] == SHARD_COLS
    bk = jnp.asarray(_bucket_tables())
    h, ht, bias, (win8, conv8) = _norm_and_gather(x[0], norm_w, [w_in[0].astype(MXU), conv_w[0]], rel_bias, bk)
    w_sect, w_sect_t = _build_w(win8)
    conv_full = conv8.transpose(1, 0, 2).reshape(4, 3 * D_DN)

    gr = _local_step(x[0], loss_target[0], h, ht, bias, bk, w_sect, conv_full, a_log, dt_bias, dn_norm_w, q_norm_w,
                     k_norm_w, w_out[0].astype(MXU))

    slabs = [_build_slabs(gr["w_in_sections"]),
             gr["w_out"].reshape(4, 2, D_MODEL // N_DEV, D_MODEL).transpose(1, 0, 2, 3),
             gr["conv_w"].reshape(4, 4, 2, 3 * D_DN // N_DEV).transpose(2, 1, 0, 3)]
    core = lax.axis_index("c").astype(jnp.int32).reshape(1)
    from_sibling = _swap_siblings(slabs)
    wires = (GRAD_WIRE, GRAD_WIRE, F32)
    partial = [_chip_sum(slabs[i], from_sibling[i], core, wires[i], "chip_sum_%d" % i) for i in range(3)]
    grad_x, small_pack, (r_win, r_wout, r_conv) = _finish_step(x[0], norm_w, w_sect_t, gr, partial)
    r_small = _share_small(small_pack)

    g_win, d_win, m_win, v_win = _adamw_sum(w_in[0], r_win, m_w_in[0], v_w_in[0], "adamw_w_in")
    g_wout, d_wout, m_wout, v_wout = _adamw_sum(w_out[0], r_wout, m_w_out[0], v_w_out[0], "adamw_w_out")
    g_conv, d_conv, m_conv, v_conv = _adamw_sum(conv_w[0], r_conv, m_conv_w[0], v_conv_w[0], "adamw_conv_w")
    small, loss_row = _adamw_small(r_small,
                                   (norm_w, a_log, dt_bias, dn_norm_w, q_norm_w, k_norm_w, rel_bias),
                                   (m_norm_w, m_a_log, m_dt_bias, m_dn_norm_w, m_q_norm_w, m_k_norm_w, m_rel_bias),
                                   (v_norm_w, v_a_log, v_dt_bias, v_dn_norm_w, v_q_norm_w, v_k_norm_w, v_rel_bias))

    loss = loss_row[0, 0]
    names = ("norm_w", "w_in", "conv_w", "a_log", "dt_bias", "dn_norm_w", "q_norm_w", "k_norm_w", "rel_bias", "w_out")
    big = dict(w_in=(g_win, d_win, m_win, v_win), conv_w=(g_conv, d_conv, m_conv, v_conv),
               w_out=(g_wout, d_wout, m_wout, v_wout))
    outs = [loss, grad_x[None]]
    for kind in range(4):
        for nm in names:
            outs.append(big[nm][kind][None] if nm in big else small[kind][SMALL_NAMES.index(nm)])
    return tuple(outs)
```
